```python
import math
import jax, jax.numpy as jnp
from jax import lax
import numpy as np

D_MODEL = 1024
BATCH = 8
SEQ = 4096
DEPTH = 2

GLA_HEADS = 4
GLA_DK = D_MODEL // 2
GLA_DV = D_MODEL
GLA_DKH = GLA_DK // GLA_HEADS
GLA_DVH = GLA_DV // GLA_HEADS
GATE_RANK = 16
GATE_TAU = 16.0
CHUNK = 64
CONV_DIM = D_MODEL
CONV_WIDTH = 3
IN_SIZES = (GLA_DK, GLA_DK, GLA_DV, GATE_RANK, GLA_DV, CONV_DIM, CONV_DIM, CONV_DIM, D_MODEL, D_MODEL)
IN_COLS = sum(IN_SIZES)
IN_SPLITS = tuple(int(s) for s in np.cumsum(IN_SIZES)[:-1])
N_GROUPS = 8
EXPERTS_PER_GROUP = 8
N_EXPERTS = N_GROUPS * EXPERTS_PER_GROUP
TOP_K = 2
D_EXPERT = D_MODEL // 2
MOE_BLOCK = 256
EPS = 1e-6

kernel_name = "hybrid_gla_shortconv_hiermoe_adaln"


def rmsnorm(x, g):
    xf = x.astype(jnp.float32)
    y = xf * lax.rsqrt(jnp.mean(xf * xf, axis=-1, keepdims=True) + EPS)
    return (y * g.astype(jnp.float32)).astype(x.dtype)


def gla_chunked(q, k, v, log_a):
    Bsz, S, H, dk = q.shape
    dv = v.shape[-1]
    n = S // CHUNK

    def to_chunks(t):
        return t.reshape(Bsz, n, CHUNK, H, t.shape[-1]).transpose(1, 0, 3, 2, 4)

    qc, kc, vc, gc = to_chunks(q), to_chunks(k), to_chunks(v), to_chunks(log_a)
    b = jnp.cumsum(gc, axis=3)
    b_last = b[:, :, :, -1:, :]
    q_t = qc * jnp.exp(b)
    k_t = kc * jnp.exp(-b)
    k_s = kc * jnp.exp(b_last - b)
    causal = jnp.tril(jnp.ones((CHUNK, CHUNK), dtype=bool))
    attn = jnp.einsum('nbhid,nbhjd->nbhij', q_t, k_t)
    attn = jnp.where(causal, attn, 0.0)
    o_intra = jnp.einsum('nbhij,nbhjv->nbhiv', attn, vc)
    decay = jnp.exp(b_last)[:, :, :, 0, :]

    def step(state, inp):
        q_i, k_i, v_i, d_i = inp
        o_i = jnp.einsum('bhid,bhdv->bhiv', q_i, state)
        state = d_i[..., None] * state + jnp.einsum('bhid,bhiv->bhdv', k_i, v_i)
        return state, o_i

    s0 = jnp.zeros((Bsz, H, dk, dv), jnp.float32)
    _, o_inter = lax.scan(step, s0, (q_t, k_s, vc, decay))
    o = o_intra + o_inter
    return o.transpose(1, 0, 3, 2, 4).reshape(Bsz, S, H, dv)


def causal_short_conv(u, w):
    S = u.shape[1]
    up = jnp.pad(u, ((0, 0), (CONV_WIDTH - 1, 0), (0, 0)))
    return sum(w[i] * up[:, i:i + S] for i in range(CONV_WIDTH))


def token_mixer(h, w_in, gate_w2, gate_b, gla_norm_g, conv_w, w_gla_out, w_conv_out, w_out):
    Bsz, S, _ = h.shape
    z = h @ w_in
    q, k, v, g_dn, r, conv_b, conv_c, conv_h, gate_a, gate_c = jnp.split(z, IN_SPLITS, axis=-1)
    f32 = jnp.float32
    q = q.astype(f32).reshape(Bsz, S, GLA_HEADS, GLA_DKH) * (GLA_DKH ** -0.5)
    k = k.astype(f32).reshape(Bsz, S, GLA_HEADS, GLA_DKH)
    v = v.astype(f32).reshape(Bsz, S, GLA_HEADS, GLA_DVH)
    log_a = jax.nn.log_sigmoid((g_dn @ gate_w2 + gate_b).astype(f32)) / GATE_TAU
    log_a = log_a.reshape(Bsz, S, GLA_HEADS, GLA_DKH)
    o = gla_chunked(q, k, v, log_a)
    o = rmsnorm(o, gla_norm_g.reshape(GLA_HEADS, GLA_DVH)).reshape(Bsz, S, GLA_DV)
    o = o.astype(h.dtype) * jax.nn.silu(r)
    y_gla = o @ w_gla_out
    conv_out = causal_short_conv(conv_c * conv_h, conv_w)
    y_conv = (conv_b * conv_out) @ w_conv_out
    y = jax.nn.sigmoid(gate_a) * y_gla + jax.nn.sigmoid(gate_c) * y_conv
    return y @ w_out


def hier_moe(h, rg_w, rg_b, re_w, re_b, w1, w3, w2):
    Bsz, S, D = h.shape
    N = Bsz * S
    NK = N * TOP_K
    t = h.reshape(N, D)
    p_group = jax.nn.softmax((t @ rg_w + rg_b).astype(jnp.float32), axis=-1)
    g_w, g_idx = lax.top_k(p_group, 1)
    e_logits = (t @ re_w + re_b).astype(jnp.float32).reshape(N, N_GROUPS, EXPERTS_PER_GROUP)
    e_logits = jnp.take_along_axis(e_logits, g_idx[:, :, None], axis=1)[:, 0]
    p_exp = jax.nn.softmax(e_logits, axis=-1)
    e_w, e_loc = lax.top_k(p_exp, TOP_K)
    e_w = e_w / jnp.sum(e_w, axis=-1, keepdims=True) * g_w
    e_id = g_idx * EXPERTS_PER_GROUP + e_loc
    flat_e = e_id.reshape(NK)
    flat_w = e_w.reshape(NK)
    flat_tok = jnp.repeat(jnp.arange(N, dtype=jnp.int32), TOP_K)
    order = jnp.argsort(flat_e)
    se = flat_e[order]
    counts = jnp.bincount(flat_e, length=N_EXPERTS).astype(jnp.int32)
    padded = ((counts + MOE_BLOCK - 1) // MOE_BLOCK) * MOE_BLOCK
    pad_end = jnp.cumsum(padded)
    pad_start = pad_end - padded
    start = jnp.cumsum(counts) - counts
    dest = pad_start[se] + (jnp.arange(NK, dtype=jnp.int32) - start[se])
    n_blocks = (NK + MOE_BLOCK - 1) // MOE_BLOCK + N_EXPERTS
    P = n_blocks * MOE_BLOCK
    buf_tok = jnp.full((P,), N, jnp.int32).at[dest].set(flat_tok[order])
    buf_w = jnp.zeros((P,), h.dtype).at[dest].set(flat_w[order].astype(h.dtype))
    blk_e = jnp.searchsorted(pad_end, jnp.arange(n_blocks, dtype=jnp.int32) * MOE_BLOCK, side='right')
    blk_e = jnp.minimum(blk_e, N_EXPERTS - 1)
    t_pad = jnp.concatenate([t, jnp.zeros((1, D), t.dtype)], axis=0)
    xb = t_pad[buf_tok].reshape(n_blocks, MOE_BLOCK, D)

    def expert_block(args):
        xblk, e = args
        return (jax.nn.silu(xblk @ w1[e]) * (xblk @ w3[e])) @ w2[e]

    yb = lax.map(expert_block, (xb, blk_e)).reshape(P, D)
    out = jnp.zeros((N + 1, D), h.dtype).at[buf_tok].add(yb * buf_w[:, None])[:N]
    return out.reshape(Bsz, S, D)


def setup_inputs(seed: int = 0) -> dict:
    key = jax.random.key(seed)
    ks = jax.random.split(key, 24)
    D, L = D_MODEL, DEPTH

    def nrm(k, shape, scale):
        return jax.random.normal(k, shape, jnp.float32) * scale

    return {
        "x": nrm(ks[0], (BATCH, SEQ, D), 1.0),
        "c": nrm(ks[1], (BATCH, D), 1.0),
        "mod_w": nrm(ks[2], (L, D, 6 * D), 0.5 * D ** -0.5),
        "mod_b": nrm(ks[3], (L, 6 * D), 0.02),
        "norm1_g": 1.0 + nrm(ks[4], (L, D), 0.02),
        "w_in": nrm(ks[5], (L, D, IN_COLS), D ** -0.5),
        "gate_w2": nrm(ks[6], (L, GATE_RANK, GLA_DK), GATE_RANK ** -0.5),
        "gate_b": nrm(ks[7], (L, GLA_DK), 0.1),
        "gla_norm_g": 1.0 + nrm(ks[8], (L, GLA_DV), 0.02),
        "conv_w": nrm(ks[9], (L, CONV_WIDTH, CONV_DIM), CONV_WIDTH ** -0.5),
        "w_gla_out": nrm(ks[10], (L, GLA_DV, D), GLA_DV ** -0.5),
        "w_conv_out": nrm(ks[11], (L, CONV_DIM, D), CONV_DIM ** -0.5),
        "w_out": nrm(ks[12], (L, D, D), D ** -0.5),
        "norm2_g": 1.0 + nrm(ks[13], (L, D), 0.02),
        "router_group_w": nrm(ks[14], (L, D, N_GROUPS), D ** -0.5),
        "router_group_b": nrm(ks[15], (L, N_GROUPS), 0.01),
        "router_expert_w": nrm(ks[16], (L, D, N_EXPERTS), D ** -0.5),
        "router_expert_b": nrm(ks[17], (L, N_EXPERTS), 0.01),
        "expert_w1": nrm(ks[18], (L, N_EXPERTS, D, D_EXPERT), D ** -0.5),
        "expert_w3": nrm(ks[19], (L, N_EXPERTS, D, D_EXPERT), D ** -0.5),
        "expert_w2": nrm(ks[20], (L, N_EXPERTS, D_EXPERT, D), D_EXPERT ** -0.5),
        "final_norm_g": 1.0 + nrm(ks[21], (D,), 0.02),
    }


def reference(x, c, mod_w, mod_b, norm1_g, w_in, gate_w2, gate_b, gla_norm_g, conv_w,
              w_gla_out, w_conv_out, w_out, norm2_g, router_group_w, router_group_b,
              router_expert_w, router_expert_b, expert_w1, expert_w3, expert_w2, final_norm_g):
    for l in range(DEPTH):
        mod = jax.nn.silu(c) @ mod_w[l] + mod_b[l]
        sh1, sc1, g1, sh2, sc2, g2 = jnp.split(mod, 6, axis=-1)
        h = rmsnorm(x, norm1_g[l]) * (1.0 + sc1[:, None]) + sh1[:, None]
        y = token_mixer(h, w_in[l], gate_w2[l], gate_b[l], gla_norm_g[l], conv_w[l],
                        w_gla_out[l], w_conv_out[l], w_out[l])
        x = x + g1[:, None] * y
        h = rmsnorm(x, norm2_g[l]) * (1.0 + sc2[:, None]) + sh2[:, None]
        y = hier_moe(h, router_group_w[l], router_group_b[l], router_expert_w[l],
                     router_expert_b[l], expert_w1[l], expert_w3[l], expert_w2[l])
        x = x + g2[:, None] * y
    return rmsnorm(x, final_norm_g)
```

```python
import functools

import jax
import jax.numpy as jnp
from jax import lax
from jax.experimental import pallas as pl
from jax.experimental.pallas import tpu as pltpu

F32 = jnp.float32
BF16 = jnp.bfloat16
HIGHEST = lax.Precision.HIGHEST

GLA_HEADS = 4
GATE_TAU = 16.0
GLA_CHUNK = 64
N_GROUPS = 8
EXPERTS_PER_GROUP = 8
N_EXPERTS = N_GROUPS * EXPERTS_PER_GROUP
NORM_EPS = 1e-6

LANES = 128
VMEM_LIMIT_BYTES = 56 * 1024 * 1024

IN_TILE = 512
GLA_TILE = 512
MIX_TILE = 256
MOE_BLOCK = 256
ROW_TILE = 256

R_ID1, R_ID2, R_W1, R_W2, R_RANK1, R_RANK2 = 0, 1, 2, 3, 4, 5


def _dot(a, b):
    return jnp.dot(a, b, preferred_element_type=F32)


def _const_spec(shape):
    nd = len(shape)
    return pl.BlockSpec(shape, lambda *_: (0,) * nd, pipeline_mode=pl.Buffered(1))


def _params(sem):
    return pltpu.CompilerParams(dimension_semantics=sem, vmem_limit_bytes=VMEM_LIMIT_BYTES)


def _mod_kernel(c_ref, w_ref, b_ref, o_ref):
    c = c_ref[...]
    sc = c * jax.nn.sigmoid(c)
    o_ref[0] = jnp.dot(sc, w_ref[0], precision=HIGHEST, preferred_element_type=F32) + b_ref[0]


def _modulation(c, mod_w, mod_b):
    L, D, D6 = mod_w.shape
    B = c.shape[0]
    nj = D6 // D
    return pl.pallas_call(
        _mod_kernel,
        grid=(L, nj),
        in_specs=[
            pl.BlockSpec((B, D), lambda l, j: (0, 0)),
            pl.BlockSpec((1, D, D), lambda l, j: (l, 0, j)),
            pl.BlockSpec((1, 1, D), lambda l, j: (l, 0, j)),
        ],
        out_specs=pl.BlockSpec((1, B, D), lambda l, j: (l, 0, j)),
        out_shape=jax.ShapeDtypeStruct((L, B, D6), F32),
        compiler_params=_params(("arbitrary", "arbitrary")),
        name="adaln_mod",
    )(c, mod_w, mod_b.reshape(L, 1, D6))


def _inproj_kernel(x_ref, mod_ref, g_ref, wa_ref, wb_ref, wg_ref, za_ref, zb_ref, zg_ref):
    x = x_ref[...]
    ms = jnp.mean(x * x, axis=-1, keepdims=True)
    y = x * lax.rsqrt(ms + NORM_EPS) * g_ref[...]
    m = mod_ref[0]
    h = (y * (1.0 + m[1:2]) + m[0:1]).astype(BF16)
    d = x.shape[1]
    for j in range(za_ref.shape[1] // d):
        za_ref[:, j * d:(j + 1) * d] = _dot(h, wa_ref[:, j * d:(j + 1) * d]).astype(BF16)
    for j in range(zb_ref.shape[1] // d):
        zb_ref[:, j * d:(j + 1) * d] = _dot(h, wb_ref[:, j * d:(j + 1) * d]).astype(BF16)
    zg_ref[...] = _dot(h, wg_ref[...])


def _in_projection(x, modl, norm_g, wa, wb, wg, seq):
    N, D = x.shape
    tm = IN_TILE
    ca, cb = wa.shape[1], wb.shape[1]
    return pl.pallas_call(
        _inproj_kernel,
        grid=(N // tm,),
        in_specs=[
            pl.BlockSpec((tm, D), lambda i: (i, 0)),
            pl.BlockSpec((1, 6, D), lambda i: ((i * tm) // seq, 0, 0)),
            _const_spec((1, D)),
            _const_spec((D, ca)),
            _const_spec((D, cb)),
            _const_spec((D, LANES)),
        ],
        out_specs=[
            pl.BlockSpec((tm, ca), lambda i: (i, 0)),
            pl.BlockSpec((tm, cb), lambda i: (i, 0)),
            pl.BlockSpec((tm, LANES), lambda i: (i, 0)),
        ],
        out_shape=[
            jax.ShapeDtypeStruct((N, ca), BF16),
            jax.ShapeDtypeStruct((N, cb), BF16),
            jax.ShapeDtypeStruct((N, LANES), F32),
        ],
        compiler_params=_params(("arbitrary",)),
        name="in_projection",
    )(x, modl, norm_g, wa, wb, wg)


def _log_sigmoid(x):
    return jnp.minimum(x, 0.0) - jnp.log1p(jnp.exp(-jnp.abs(x)))


def _gla_kernel(za_ref, zg_ref, gw2_ref, gb_ref, ng_ref, o_ref, st_ref, lg_ref, *, dk, dv):
    heads = GLA_HEADS
    dkh, dvh = dk // heads, dv // heads
    c = GLA_CHUNK
    ts = za_ref.shape[0]

    @pl.when(pl.program_id(1) == 0)
    def _():
        st_ref[...] = jnp.zeros_like(st_ref)

    pre = jnp.dot(zg_ref[...], gw2_ref[...], precision=HIGHEST, preferred_element_type=F32) + gb_ref[...]
    lg_ref[...] = _log_sigmoid(pre) * (1.0 / GATE_TAU)

    row = lax.broadcasted_iota(jnp.int32, (c, c), 0)
    col = lax.broadcasted_iota(jnp.int32, (c, c), 1)
    causal = row >= col
    tril = causal.astype(F32)
    qscale = dkh ** -0.5

    def chunk(ci, carry):
        r0 = pl.multiple_of(ci * c, c)
        rows = pl.ds(r0, c)
        b_all = jnp.dot(tril, lg_ref[rows, :], precision=HIGHEST, preferred_element_type=F32)
        for hd in range(heads):
            ks = slice(hd * dkh, (hd + 1) * dkh)
            q = za_ref[rows, hd * dkh:(hd + 1) * dkh].astype(F32) * qscale
            k = za_ref[rows, dk + hd * dkh:dk + (hd + 1) * dkh].astype(F32)
            v = za_ref[rows, 2 * dk + hd * dvh:2 * dk + (hd + 1) * dvh]
            r = za_ref[rows, 2 * dk + dv + hd * dvh:2 * dk + dv + (hd + 1) * dvh].astype(F32)
            b = b_all[:, ks]
            b_last = b[c - 1:c, :]
            q_t = (q * jnp.exp(b)).astype(BF16)
            k_t = (k * jnp.exp(-b)).astype(BF16)
            k_s = (k * jnp.exp(b_last - b)).astype(BF16)
            decay = jnp.exp(b_last)
            attn = lax.dot_general(q_t, k_t, (((1,), (1,)), ((), ())), preferred_element_type=F32)
            attn = jnp.where(causal, attn, 0.0).astype(BF16)
            st = st_ref[hd]
            o = _dot(attn, v) + lax.dot_general(
                q_t, st.astype(BF16), (((1,), (1,)), ((), ())), preferred_element_type=F32)
            upd = lax.dot_general(v, k_s, (((0,), (0,)), ((), ())), preferred_element_type=F32)
            st_ref[hd] = st * decay + upd
            ms = jnp.mean(o * o, axis=-1, keepdims=True)
            on = o * lax.rsqrt(ms + NORM_EPS) * ng_ref[:, hd * dvh:(hd + 1) * dvh]
            o_ref[rows, hd * dvh:(hd + 1) * dvh] = (on * (r * jax.nn.sigmoid(r))).astype(BF16)
        return carry

    lax.fori_loop(0, ts // c, chunk, 0)


def _gla(za, zg, gw2, gb, ng, batch, seq, dk, dv):
    N = za.shape[0]
    ts = GLA_TILE
    ns = seq // ts
    heads = GLA_HEADS
    kern = functools.partial(_gla_kernel, dk=dk, dv=dv)
    return pl.pallas_call(
        kern,
        grid=(batch, ns),
        in_specs=[
            pl.BlockSpec((ts, za.shape[1]), lambda b, s: (b * ns + s, 0)),
            pl.BlockSpec((ts, LANES), lambda b, s: (b * ns + s, 0)),
            _const_spec((LANES, dk)),
            _const_spec((1, dk)),
            _const_spec((1, dv)),
        ],
        out_specs=pl.BlockSpec((ts, dv), lambda b, s: (b * ns + s, 0)),
        out_shape=jax.ShapeDtypeStruct((N, dv), BF16),
        scratch_shapes=[
            pltpu.VMEM((heads, dv // heads, dk // heads), F32),
            pltpu.VMEM((ts, dk), F32),
        ],
        compiler_params=_params(("arbitrary", "arbitrary")),
        name="gla",
    )(za, zg, gw2, gb, ng)


def _mixout_kernel(og_ref, zb_ref, x_ref, mod_ref, cw_ref, wga_ref, wco_ref, wo_ref, n2_ref,
                   wr_ref, br_ref, x1_ref, h2_ref, route_ref, cnt_ref, carry_ref, run_ref):
    tm, d = x_ref.shape
    first_in_seq = pl.program_id(1) == 0

    @pl.when(jnp.logical_and(pl.program_id(0) == 0, first_in_seq))
    def _():
        run_ref[...] = jnp.zeros_like(run_ref)

    @pl.when(first_in_seq)
    def _():
        carry_ref[...] = jnp.zeros_like(carry_ref)

    m = mod_ref[0]
    cb = zb_ref[:, 0:d].astype(F32)
    cc = zb_ref[:, d:2 * d].astype(F32)
    ch = zb_ref[:, 2 * d:3 * d].astype(F32)
    ga = zb_ref[:, 3 * d:4 * d].astype(F32)
    gc = zb_ref[:, 4 * d:5 * d].astype(F32)

    u = cc * ch
    prev = carry_ref[...]
    rowi = lax.broadcasted_iota(jnp.int32, (tm, d), 0)
    u1 = jnp.where(rowi == 0, prev[7:8], pltpu.roll(u, 1, 0))
    u2 = jnp.where(rowi == 0, prev[6:7], jnp.where(rowi == 1, prev[7:8], pltpu.roll(u, 2, 0)))
    carry_ref[...] = u[tm - 8:tm]
    conv = cw_ref[0:1] * u2 + cw_ref[1:2] * u1 + cw_ref[2:3] * u

    y_conv = _dot((cb * conv).astype(BF16), wco_ref[...])
    y_gla = _dot(og_ref[...], wga_ref[...])
    y = jax.nn.sigmoid(ga) * y_gla + jax.nn.sigmoid(gc) * y_conv
    y = _dot(y.astype(BF16), wo_ref[...])
    x1 = x_ref[...] + m[2:3] * y
    x1_ref[...] = x1

    ms = jnp.mean(x1 * x1, axis=-1, keepdims=True)
    h2 = x1 * lax.rsqrt(ms + NORM_EPS) * n2_ref[...]
    h2 = h2 * (1.0 + m[4:5]) + m[3:4]
    h2_ref[...] = h2

    logits = jnp.dot(h2, wr_ref[...], precision=HIGHEST, preferred_element_type=F32) + br_ref[...]
    lane = lax.broadcasted_iota(jnp.int32, (tm, LANES), 1)
    neg = -jnp.inf
    gl = jnp.where(lane < N_GROUPS, logits, neg)
    gmax = jnp.max(gl, axis=-1, keepdims=True)
    gsum = jnp.sum(jnp.exp(gl - gmax), axis=-1, keepdims=True)
    g_w = 1.0 / gsum
    g_idx = jnp.min(jnp.where(gl == gmax, lane, LANES), axis=-1, keepdims=True)
    lo = N_GROUPS + g_idx * EXPERTS_PER_GROUP
    in_group = jnp.logical_and(lane >= lo, lane < lo + EXPERTS_PER_GROUP)
    el = jnp.where(in_group, logits, neg)
    e1 = jnp.max(el, axis=-1, keepdims=True)
    i1 = jnp.min(jnp.where(el == e1, lane, LANES), axis=-1, keepdims=True)
    el2 = jnp.where(lane == i1, neg, el)
    e2 = jnp.max(el2, axis=-1, keepdims=True)
    i2 = jnp.min(jnp.where(el2 == e2, lane, LANES), axis=-1, keepdims=True)
    ratio = jnp.exp(e2 - e1)
    w1 = g_w / (1.0 + ratio)
    w2 = g_w * ratio / (1.0 + ratio)

    oh1 = lane == i1
    oh2 = lane == i2
    oh1f = oh1.astype(F32)
    oh2f = oh2.astype(F32)
    tr = lax.broadcasted_iota(jnp.int32, (tm, tm), 0)
    tc = lax.broadcasted_iota(jnp.int32, (tm, tm), 1)
    before = (tr > tc).astype(BF16)
    run = run_ref[0:1]
    tot1 = jnp.sum(oh1f, axis=0, keepdims=True)
    tot2 = jnp.sum(oh2f, axis=0, keepdims=True)
    c1 = _dot(before, oh1f.astype(BF16)) + run
    c2 = _dot(before, oh2f.astype(BF16)) + (run + tot1)
    rank1 = jnp.sum(jnp.where(oh1, c1, 0.0), axis=-1, keepdims=True)
    rank2 = jnp.sum(jnp.where(oh2, c2, 0.0), axis=-1, keepdims=True)
    new_run = run + tot1 + tot2
    run_ref[...] = jnp.broadcast_to(new_run, run_ref.shape)
    cnt_ref[...] = jnp.broadcast_to(new_run, cnt_ref.shape)

    rec = jnp.where(lane == R_ID1, (i1 - N_GROUPS).astype(F32), 0.0)
    rec = jnp.where(lane == R_ID2, (i2 - N_GROUPS).astype(F32), rec)
    rec = jnp.where(lane == R_W1, w1, rec)
    rec = jnp.where(lane == R_W2, w2, rec)
    rec = jnp.where(lane == R_RANK1, rank1, rec)
    rec = jnp.where(lane == R_RANK2, rank2, rec)
    route_ref[...] = rec


def _mixer_out(og, zb, x, modl, conv_w, wga, wco, wo, n2g, wr, br, batch, seq):
    N, D = x.shape
    tm = MIX_TILE
    ns = seq // tm
    tok = lambda b, s: (b * ns + s, 0)
    return pl.pallas_call(
        _mixout_kernel,
        grid=(batch, ns),
        in_specs=[
            pl.BlockSpec((tm, D), tok),
            pl.BlockSpec((tm, zb.shape[1]), tok),
            pl.BlockSpec((tm, D), tok),
            pl.BlockSpec((1, 6, D), lambda b, s: (b, 0, 0)),
            _const_spec(conv_w.shape),
            _const_spec((D, D)),
            _const_spec((D, D)),
            _const_spec((D, D)),
            _const_spec((1, D)),
            _const_spec((D, LANES)),
            _const_spec((1, LANES)),
        ],
        out_specs=[
            pl.BlockSpec((tm, D), tok),
            pl.BlockSpec((tm, D), tok),
            pl.BlockSpec((tm, LANES), tok),
            pl.BlockSpec((8, LANES), lambda b, s: (0, 0)),
        ],
        out_shape=[
            jax.ShapeDtypeStruct((N, D), F32),
            jax.ShapeDtypeStruct((N, D), F32),
            jax.ShapeDtypeStruct((N, LANES), F32),
            jax.ShapeDtypeStruct((8, LANES), F32),
        ],
        scratch_shapes=[pltpu.VMEM((8, D), F32), pltpu.VMEM((8, LANES), F32)],
        compiler_params=_params(("arbitrary", "arbitrary")),
        name="mixer_out",
    )(og, zb, x, modl, conv_w, wga, wco, wo, n2g, wr, br)


def _dispatch_kernel(d1_ref, d2_ref, h_ref, xs_ref, sem):
    tm = h_ref.shape[0]
    base = pl.program_id(0) * tm

    def issue(t, carry):
        src = h_ref.at[pl.ds(t, 1)]
        pltpu.make_async_copy(src, xs_ref.at[pl.ds(d1_ref[base + t], 1)], sem).start()
        pltpu.make_async_copy(src, xs_ref.at[pl.ds(d2_ref[base + t], 1)], sem).start()
        return carry

    lax.fori_loop(0, tm, issue, 0, unroll=8)
    for _ in range(2):
        pltpu.make_async_copy(h_ref, xs_ref.at[pl.ds(0, tm)], sem).wait()


def _dispatch(h2, dest1, dest2, n_rows):
    N, D = h2.shape
    tm = ROW_TILE
    return pl.pallas_call(
        _dispatch_kernel,
        grid_spec=pltpu.PrefetchScalarGridSpec(
            num_scalar_prefetch=2,
            grid=(N // tm,),
            in_specs=[pl.BlockSpec((tm, D), lambda i, d1, d2: (i, 0))],
            out_specs=pl.BlockSpec(memory_space=pl.ANY),
            scratch_shapes=[pltpu.SemaphoreType.DMA(())],
        ),
        out_shape=jax.ShapeDtypeStruct((n_rows, D), F32),
        compiler_params=_params(("arbitrary",)),
        name="moe_dispatch",
    )(dest1, dest2, h2)


def _expert_kernel(ib_ref, ie_ref, lo_ref, hi_ref, ni_ref, xs_ref, w1_ref, w3_ref, w2_ref, y_ref,
                   w1b, w3b, w2b):
    j = pl.program_id(0)
    jp = jnp.maximum(j - 1, 0)
    live = j < ni_ref[0]
    new_expert = jnp.logical_or(j == 0, ie_ref[j] != ie_ref[jp])
    new_block = jnp.logical_or(j == 0, ib_ref[j] != ib_ref[jp])

    @pl.when(jnp.logical_and(new_expert, live))
    def _():
        w1b[...] = w1_ref[0].astype(BF16)
        w3b[...] = w3_ref[0].astype(BF16)
        w2b[...] = w2_ref[0].astype(BF16)

    @pl.when(live)
    def _():
        xb = xs_ref[...].astype(BF16)
        h1 = _dot(xb, w1b[...])
        h3 = _dot(xb, w3b[...])
        a = (h1 * jax.nn.sigmoid(h1) * h3).astype(BF16)
        y = _dot(a, w2b[...])
        row = lax.broadcasted_iota(jnp.int32, y.shape, 0)
        mine = jnp.logical_and(row >= lo_ref[j], row < hi_ref[j])

        @pl.when(new_block)
        def _():
            y_ref[...] = jnp.where(mine, y, 0.0)

        @pl.when(jnp.logical_not(new_block))
        def _():
            y_ref[...] = jnp.where(mine, y, y_ref[...])


def _experts(xs, items, w1, w3, w2):
    P, D = xs.shape
    E, _, DE = w1.shape
    bm = MOE_BLOCK
    n_items_max = P // bm + E
    rows = lambda j, ib, ie, lo, hi, ni: (ib[j], 0)
    wsel = lambda j, ib, ie, lo, hi, ni: (ie[j], 0, 0)
    return pl.pallas_call(
        _expert_kernel,
        grid_spec=pltpu.PrefetchScalarGridSpec(
            num_scalar_prefetch=5,
            grid=(n_items_max,),
            in_specs=[
                pl.BlockSpec((bm, D), rows),
                pl.BlockSpec((1, D, DE), wsel),
                pl.BlockSpec((1, D, DE), wsel),
                pl.BlockSpec((1, DE, D), wsel),
            ],
            out_specs=pl.BlockSpec((bm, D), rows),
            scratch_shapes=[
                pltpu.VMEM((D, DE), BF16),
                pltpu.VMEM((D, DE), BF16),
                pltpu.VMEM((DE, D), BF16),
            ],
        ),
        out_shape=jax.ShapeDtypeStruct((P, D), F32),
        compiler_params=_params(("arbitrary",)),
        name="moe_experts",
    )(*items, xs, w1, w3, w2)


def _combine_kernel(d1_ref, d2_ref, x1_ref, route_ref, mod_ref, fg_ref, yb_ref, o_ref, buf, sem,
                    *, final_norm):
    tm = x1_ref.shape[0]
    base = pl.program_id(0) * tm

    def issue(t, carry):
        pltpu.make_async_copy(yb_ref.at[pl.ds(d1_ref[base + t], 1)], buf.at[0, pl.ds(t, 1)], sem).start()
        pltpu.make_async_copy(yb_ref.at[pl.ds(d2_ref[base + t], 1)], buf.at[1, pl.ds(t, 1)], sem).start()
        return carry

    lax.fori_loop(0, tm, issue, 0, unroll=8)
    for k in range(2):
        pltpu.make_async_copy(yb_ref.at[pl.ds(0, tm)], buf.at[k], sem).wait()

    rec = route_ref[...]
    w1 = rec[:, R_W1:R_W1 + 1]
    w2 = rec[:, R_W2:R_W2 + 1]
    y = w1 * buf[0] + w2 * buf[1]
    x2 = x1_ref[...] + mod_ref[0][5:6] * y
    if final_norm:
        ms = jnp.mean(x2 * x2, axis=-1, keepdims=True)
        x2 = x2 * lax.rsqrt(ms + NORM_EPS) * fg_ref[...]
    o_ref[...] = x2


def _combine(x1, route, modl, final_g, yb, dest1, dest2, seq, final_norm):
    N, D = x1.shape
    tm = ROW_TILE
    kern = functools.partial(_combine_kernel, final_norm=final_norm)
    tok = lambda i, d1, d2: (i, 0)
    return pl.pallas_call(
        kern,
        grid_spec=pltpu.PrefetchScalarGridSpec(
            num_scalar_prefetch=2,
            grid=(N // tm,),
            in_specs=[
                pl.BlockSpec((tm, D), tok),
                pl.BlockSpec((tm, LANES), tok),
                pl.BlockSpec((1, 6, D), lambda i, d1, d2: ((i * tm) // seq, 0, 0)),
                pl.BlockSpec((1, D), lambda i, d1, d2: (0, 0)),
                pl.BlockSpec(memory_space=pl.ANY),
            ],
            out_specs=pl.BlockSpec((tm, D), tok),
            scratch_shapes=[pltpu.VMEM((2, tm, D), F32), pltpu.SemaphoreType.DMA(())],
        ),
        out_shape=jax.ShapeDtypeStruct((N, D), F32),
        compiler_params=_params(("arbitrary",)),
        name="moe_combine",
    )(dest1, dest2, x1, route, modl, final_g, yb)


def _routing_tables(route, cnt, n_items_max):
    bm = MOE_BLOCK
    i32 = jnp.int32
    id1 = route[:, R_ID1].astype(i32)
    id2 = route[:, R_ID2].astype(i32)
    rank1 = route[:, R_RANK1].astype(i32)
    rank2 = route[:, R_RANK2].astype(i32)
    counts = cnt[0, N_GROUPS:N_GROUPS + N_EXPERTS].astype(i32)
    end = jnp.cumsum(counts)
    start = end - counts
    dest1 = start[id1] + rank1
    dest2 = start[id2] + rank2
    first = start // bm
    last = jnp.maximum(end - 1, 0) // bm
    per_e = jnp.where(counts > 0, last - first + 1, 0)
    item_end = jnp.cumsum(per_e)
    item_start = item_end - per_e
    n_items = item_end[-1]
    j = jnp.minimum(jnp.arange(n_items_max, dtype=i32), n_items - 1)
    ie = jnp.minimum(jnp.searchsorted(item_end, j, side="right"), N_EXPERTS - 1).astype(i32)
    ib = first[ie] + (j - item_start[ie])
    lo = jnp.maximum(start[ie], ib * bm) - ib * bm
    hi = jnp.minimum(end[ie], (ib + 1) * bm) - ib * bm
    items = (ib.astype(i32), ie, lo.astype(i32), hi.astype(i32), n_items.reshape(1).astype(i32))
    return dest1, dest2, items


def kernel(x, c, mod_w, mod_b, norm1_g, w_in, gate_w2, gate_b, gla_norm_g, conv_w, w_gla_out,
           w_conv_out, w_out, norm2_g, router_group_w, router_group_b, router_expert_w,
           router_expert_b, expert_w1, expert_w3, expert_w2, final_norm_g):
    B, S, D = x.shape
    L = mod_w.shape[0]
    N = B * S
    dk = gate_w2.shape[2]
    rank = gate_w2.shape[1]
    dv = gla_norm_g.shape[1]
    n_slots = N * 2
    n_items_max = n_slots // MOE_BLOCK + N_EXPERTS
    assert S % GLA_TILE == 0 and S % MIX_TILE == 0 and S % IN_TILE == 0 and N % ROW_TILE == 0
    assert n_slots % MOE_BLOCK == 0
    assert N_GROUPS + N_EXPERTS <= LANES and rank <= LANES

    mod = _modulation(c, mod_w, mod_b)
    xf = x.reshape(N, D)
    o_gd = 2 * dk + dv
    o_r = o_gd + rank
    o_b = o_r + dv
    for l in range(L):
        modl = mod[l].reshape(B, 6, D)
        wl = w_in[l]
        wa = jnp.concatenate([wl[:, :o_gd], wl[:, o_r:o_b]], axis=1).astype(BF16)
        wb = wl[:, o_b:].astype(BF16)
        wg = jnp.pad(wl[:, o_gd:o_r], ((0, 0), (0, LANES - rank))).astype(BF16)
        gw2 = jnp.pad(gate_w2[l], ((0, LANES - rank), (0, 0)))
        za, zb, zg = _in_projection(xf, modl, norm1_g[l].reshape(1, D), wa, wb, wg, S)
        og = _gla(za, zg, gw2, gate_b[l].reshape(1, dk), gla_norm_g[l].reshape(1, dv), B, S, dk, dv)
        wr = jnp.pad(jnp.concatenate([router_group_w[l], router_expert_w[l]], axis=1),
                     ((0, 0), (0, LANES - N_GROUPS - N_EXPERTS)))
        br = jnp.pad(jnp.concatenate([router_group_b[l], router_expert_b[l]]),
                     (0, LANES - N_GROUPS - N_EXPERTS)).reshape(1, LANES)
        x1, h2, route, cnt = _mixer_out(
            og, zb, xf, modl, conv_w[l], w_gla_out[l].astype(BF16), w_conv_out[l].astype(BF16),
            w_out[l].astype(BF16), norm2_g[l].reshape(1, D), wr, br, B, S)
        dest1, dest2, items = _routing_tables(route, cnt, n_items_max)
        xs = _dispatch(h2, dest1, dest2, n_slots)
        yb = _experts(xs, items, expert_w1[l], expert_w3[l], expert_w2[l])
        xf = _combine(x1, route, modl, final_norm_g.reshape(1, D), yb, dest1, dest2, S,
                      final_norm=(l == L - 1))
    return xf.reshape(B, S, D)
```

```python
import functools

import jax
import jax.numpy as jnp
from jax import lax
from jax.experimental import pallas as pl
from jax.experimental.pallas import tpu as pltpu

F32 = jnp.float32
BF16 = jnp.bfloat16
HIGHEST = lax.Precision.HIGHEST

GLA_HEADS = 4
GATE_TAU = 16.0
GLA_CHUNK = 64
N_GROUPS = 8
EXPERTS_PER_GROUP = 8
N_EXPERTS = N_GROUPS * EXPERTS_PER_GROUP
NORM_EPS = 1e-6

LANES = 128
VMEM_LIMIT_BYTES = 56 * 1024 * 1024

IN_TILE = 512
GLA_TILE = 512
MIX_TILE = 256
MOE_BLOCK = 256
ROW_TILE = 256

R_ID1, R_ID2, R_W1, R_W2, R_RANK1, R_RANK2 = 0, 1, 2, 3, 4, 5


def _dot(a, b):
    return jnp.dot(a, b, preferred_element_type=F32)


def _const_spec(shape):
    nd = len(shape)
    return pl.BlockSpec(shape, lambda *_: (0,) * nd, pipeline_mode=pl.Buffered(1))


def _params(sem):
    return pltpu.CompilerParams(dimension_semantics=sem, vmem_limit_bytes=VMEM_LIMIT_BYTES)


def _mod_kernel(c_ref, w_ref, b_ref, o_ref):
    c = c_ref[...]
    sc = c * jax.nn.sigmoid(c)
    o_ref[0] = jnp.dot(sc, w_ref[0], precision=HIGHEST, preferred_element_type=F32) + b_ref[0]


def _modulation(c, mod_w, mod_b):
    L, D, D6 = mod_w.shape
    B = c.shape[0]
    nj = D6 // D
    return pl.pallas_call(
        _mod_kernel,
        grid=(L, nj),
        in_specs=[
            pl.BlockSpec((B, D), lambda l, j: (0, 0)),
            pl.BlockSpec((1, D, D), lambda l, j: (l, 0, j)),
            pl.BlockSpec((1, 1, D), lambda l, j: (l, 0, j)),
        ],
        out_specs=pl.BlockSpec((1, B, D), lambda l, j: (l, 0, j)),
        out_shape=jax.ShapeDtypeStruct((L, B, D6), F32),
        compiler_params=_params(("arbitrary", "arbitrary")),
        name="adaln_mod",
    )(c, mod_w, mod_b.reshape(L, 1, D6))


def _inproj_kernel(x_ref, mod_ref, g_ref, wa_ref, wb_ref, wg_ref, za_ref, zb_ref, zg_ref):
    x = x_ref[...]
    ms = jnp.mean(x * x, axis=-1, keepdims=True)
    y = x * lax.rsqrt(ms + NORM_EPS) * g_ref[...]
    m = mod_ref[0]
    h = (y * (1.0 + m[1:2]) + m[0:1]).astype(BF16)
    d = x.shape[1]
    for j in range(za_ref.shape[1] // d):
        za_ref[:, j * d:(j + 1) * d] = _dot(h, wa_ref[:, j * d:(j + 1) * d]).astype(BF16)
    for j in range(zb_ref.shape[1] // d):
        zb_ref[:, j * d:(j + 1) * d] = _dot(h, wb_ref[:, j * d:(j + 1) * d]).astype(BF16)
    zg_ref[...] = _dot(h, wg_ref[...])


def _in_projection(x, modl, norm_g, wa, wb, wg, seq):
    N, D = x.shape
    tm = IN_TILE
    ca, cb = wa.shape[1], wb.shape[1]
    return pl.pallas_call(
        _inproj_kernel,
        grid=(N // tm,),
        in_specs=[
            pl.BlockSpec((tm, D), lambda i: (i, 0)),
            pl.BlockSpec((1, 6, D), lambda i: ((i * tm) // seq, 0, 0)),
            _const_spec((1, D)),
            _const_spec((D, ca)),
            _const_spec((D, cb)),
            _const_spec((D, LANES)),
        ],
        out_specs=[
            pl.BlockSpec((tm, ca), lambda i: (i, 0)),
            pl.BlockSpec((tm, cb), lambda i: (i, 0)),
            pl.BlockSpec((tm, LANES), lambda i: (i, 0)),
        ],
        out_shape=[
            jax.ShapeDtypeStruct((N, ca), BF16),
            jax.ShapeDtypeStruct((N, cb), BF16),
            jax.ShapeDtypeStruct((N, LANES), F32),
        ],
        compiler_params=_params(("arbitrary",)),
        name="in_projection",
    )(x, modl, norm_g, wa, wb, wg)


def _log_sigmoid(x):
    return jnp.minimum(x, 0.0) - jnp.log1p(jnp.exp(-jnp.abs(x)))


def _gla_kernel(za_ref, zg_ref, gw2_ref, gb_ref, ng_ref, o_ref, st_ref, lg_ref, *, dk, dv):
    heads = GLA_HEADS
    dkh, dvh = dk // heads, dv // heads
    c = GLA_CHUNK
    ts = za_ref.shape[0]

    @pl.when(pl.program_id(1) == 0)
    def _():
        st_ref[...] = jnp.zeros_like(st_ref)

    pre = jnp.dot(zg_ref[...], gw2_ref[...], precision=HIGHEST, preferred_element_type=F32) + gb_ref[...]
    lg_ref[...] = _log_sigmoid(pre) * (1.0 / GATE_TAU)

    row = lax.broadcasted_iota(jnp.int32, (c, c), 0)
    col = lax.broadcasted_iota(jnp.int32, (c, c), 1)
    causal = row >= col
    tril = causal.astype(F32)
    qscale = dkh ** -0.5

    def chunk(ci, carry):
        r0 = pl.multiple_of(ci * c, c)
        rows = pl.ds(r0, c)
        b_all = jnp.dot(tril, lg_ref[rows, :], precision=HIGHEST, preferred_element_type=F32)
        for hd in range(heads):
            ks = slice(hd * dkh, (hd + 1) * dkh)
            q = za_ref[rows, hd * dkh:(hd + 1) * dkh].astype(F32) * qscale
            k = za_ref[rows, dk + hd * dkh:dk + (hd + 1) * dkh].astype(F32)
            v = za_ref[rows, 2 * dk + hd * dvh:2 * dk + (hd + 1) * dvh]
            r = za_ref[rows, 2 * dk + dv + hd * dvh:2 * dk + dv + (hd + 1) * dvh].astype(F32)
            b = b_all[:, ks]
            b_last = b[c - 1:c, :]
            q_t = (q * jnp.exp(b)).astype(BF16)
            k_t = (k * jnp.exp(-b)).astype(BF16)
            k_s = (k * jnp.exp(b_last - b)).astype(BF16)
            decay = jnp.exp(b_last)
            attn = lax.dot_general(q_t, k_t, (((1,), (1,)), ((), ())), preferred_element_type=F32)
            attn = jnp.where(causal, attn, 0.0).astype(BF16)
            st = st_ref[hd]
            o = _dot(attn, v) + lax.dot_general(
                q_t, st.astype(BF16), (((1,), (1,)), ((), ())), preferred_element_type=F32)
            upd = lax.dot_general(v, k_s, (((0,), (0,)), ((), ())), preferred_element_type=F32)
            st_ref[hd] = st * decay + upd
            ms = jnp.mean(o * o, axis=-1, keepdims=True)
            on = o * lax.rsqrt(ms + NORM_EPS) * ng_ref[:, hd * dvh:(hd + 1) * dvh]
            o_ref[rows, hd * dvh:(hd + 1) * dvh] = (on * (r * jax.nn.sigmoid(r))).astype(BF16)
        return carry

    lax.fori_loop(0, ts // c, chunk, 0)


def _gla(za, zg, gw2, gb, ng, batch, seq, dk, dv):
    N = za.shape[0]
    ts = GLA_TILE
    ns = seq // ts
    heads = GLA_HEADS
    kern = functools.partial(_gla_kernel, dk=dk, dv=dv)
    return pl.pallas_call(
        kern,
        grid=(batch, ns),
        in_specs=[
            pl.BlockSpec((ts, za.shape[1]), lambda b, s: (b * ns + s, 0)),
            pl.BlockSpec((ts, LANES), lambda b, s: (b * ns + s, 0)),
            _const_spec((LANES, dk)),
            _const_spec((1, dk)),
            _const_spec((1, dv)),
        ],
        out_specs=pl.BlockSpec((ts, dv), lambda b, s: (b * ns + s, 0)),
        out_shape=jax.ShapeDtypeStruct((N, dv), BF16),
        scratch_shapes=[
            pltpu.VMEM((heads, dv // heads, dk // heads), F32),
            pltpu.VMEM((ts, dk), F32),
        ],
        compiler_params=_params(("arbitrary", "arbitrary")),
        name="gla",
    )(za, zg, gw2, gb, ng)


def _mixout_kernel(og_ref, zb_ref, x_ref, mod_ref, cw_ref, wga_ref, wco_ref, wo_ref, n2_ref,
                   wr_ref, br_ref, x1_ref, h2_ref, route_ref, cnt_ref, carry_ref, run_ref):
    tm, d = x_ref.shape
    first_in_seq = pl.program_id(1) == 0

    @pl.when(jnp.logical_and(pl.program_id(0) == 0, first_in_seq))
    def _():
        run_ref[...] = jnp.zeros_like(run_ref)

    @pl.when(first_in_seq)
    def _():
        carry_ref[...] = jnp.zeros_like(carry_ref)

    m = mod_ref[0]
    cb = zb_ref[:, 0:d].astype(F32)
    cc = zb_ref[:, d:2 * d].astype(F32)
    ch = zb_ref[:, 2 * d:3 * d].astype(F32)
    ga = zb_ref[:, 3 * d:4 * d].astype(F32)
    gc = zb_ref[:, 4 * d:5 * d].astype(F32)

    u = cc * ch
    prev = carry_ref[...]
    rowi = lax.broadcasted_iota(jnp.int32, (tm, d), 0)
    u1 = jnp.where(rowi == 0, prev[7:8], pltpu.roll(u, 1, 0))
    u2 = jnp.where(rowi == 0, prev[6:7], jnp.where(rowi == 1, prev[7:8], pltpu.roll(u, 2, 0)))
    carry_ref[...] = u[tm - 8:tm]
    conv = cw_ref[0:1] * u2 + cw_ref[1:2] * u1 + cw_ref[2:3] * u

    y_conv = _dot((cb * conv).astype(BF16), wco_ref[...])
    y_gla = _dot(og_ref[...], wga_ref[...])
    y = jax.nn.sigmoid(ga) * y_gla + jax.nn.sigmoid(gc) * y_conv
    y = _dot(y.astype(BF16), wo_ref[...])
    x1 = x_ref[...] + m[2:3] * y
    x1_ref[...] = x1

    ms = jnp.mean(x1 * x1, axis=-1, keepdims=True)
    h2 = x1 * lax.rsqrt(ms + NORM_EPS) * n2_ref[...]
    h2 = h2 * (1.0 + m[4:5]) + m[3:4]
    h2_ref[...] = h2

    logits = jnp.dot(h2, wr_ref[...], precision=HIGHEST, preferred_element_type=F32) + br_ref[...]
    lane = lax.broadcasted_iota(jnp.int32, (tm, LANES), 1)
    neg = -jnp.inf
    gl = jnp.where(lane < N_GROUPS, logits, neg)
    gmax = jnp.max(gl, axis=-1, keepdims=True)
    gsum = jnp.sum(jnp.exp(gl - gmax), axis=-1, keepdims=True)
    g_w = 1.0 / gsum
    g_idx = jnp.min(jnp.where(gl == gmax, lane, LANES), axis=-1, keepdims=True)
    lo = N_GROUPS + g_idx * EXPERTS_PER_GROUP
    in_group = jnp.logical_and(lane >= lo, lane < lo + EXPERTS_PER_GROUP)
    el = jnp.where(in_group, logits, neg)
    e1 = jnp.max(el, axis=-1, keepdims=True)
    i1 = jnp.min(jnp.where(el == e1, lane, LANES), axis=-1, keepdims=True)
    el2 = jnp.where(lane == i1, neg, el)
    e2 = jnp.max(el2, axis=-1, keepdims=True)
    i2 = jnp.min(jnp.where(el2 == e2, lane, LANES), axis=-1, keepdims=True)
    ratio = jnp.exp(e2 - e1)
    w1 = g_w / (1.0 + ratio)
    w2 = g_w * ratio / (1.0 + ratio)

    oh1 = lane == i1
    oh2 = lane == i2
    oh1f = oh1.astype(F32)
    oh2f = oh2.astype(F32)
    tr = lax.broadcasted_iota(jnp.int32, (tm, tm), 0)
    tc = lax.broadcasted_iota(jnp.int32, (tm, tm), 1)
    before = (tr > tc).astype(BF16)
    run = run_ref[0:1]
    tot1 = jnp.sum(oh1f, axis=0, keepdims=True)
    tot2 = jnp.sum(oh2f, axis=0, keepdims=True)
    c1 = _dot(before, oh1f.astype(BF16)) + run
    c2 = _dot(before, oh2f.astype(BF16)) + (run + tot1)
    rank1 = jnp.sum(jnp.where(oh1, c1, 0.0), axis=-1, keepdims=True)
    rank2 = jnp.sum(jnp.where(oh2, c2, 0.0), axis=-1, keepdims=True)
    new_run = run + tot1 + tot2
    run_ref[...] = jnp.broadcast_to(new_run, run_ref.shape)
    cnt_ref[...] = jnp.broadcast_to(new_run, cnt_ref.shape)

    rec = jnp.where(lane == R_ID1, (i1 - N_GROUPS).astype(F32), 0.0)
    rec = jnp.where(lane == R_ID2, (i2 - N_GROUPS).astype(F32), rec)
    rec = jnp.where(lane == R_W1, w1, rec)
    rec = jnp.where(lane == R_W2, w2, rec)
    rec = jnp.where(lane == R_RANK1, rank1, rec)
    rec = jnp.where(lane == R_RANK2, rank2, rec)
    route_ref[...] = rec


def _mixer_out(og, zb, x, modl, conv_w, wga, wco, wo, n2g, wr, br, batch, seq):
    N, D = x.shape
    tm = MIX_TILE
    ns = seq // tm
    tok = lambda b, s: (b * ns + s, 0)
    return pl.pallas_call(
        _mixout_kernel,
        grid=(batch, ns),
        in_specs=[
            pl.BlockSpec((tm, D), tok),
            pl.BlockSpec((tm, zb.shape[1]), tok),
            pl.BlockSpec((tm, D), tok),
            pl.BlockSpec((1, 6, D), lambda b, s: (b, 0, 0)),
            _const_spec(conv_w.shape),
            _const_spec((D, D)),
            _const_spec((D, D)),
            _const_spec((D, D)),
            _const_spec((1, D)),
            _const_spec((D, LANES)),
            _const_spec((1, LANES)),
        ],
        out_specs=[
            pl.BlockSpec((tm, D), tok),
            pl.BlockSpec((tm, D), tok),
            pl.BlockSpec((tm, LANES), tok),
            pl.BlockSpec((8, LANES), lambda b, s: (0, 0)),
        ],
        out_shape=[
            jax.ShapeDtypeStruct((N, D), F32),
            jax.ShapeDtypeStruct((N, D), F32),
            jax.ShapeDtypeStruct((N, LANES), F32),
            jax.ShapeDtypeStruct((8, LANES), F32),
        ],
        scratch_shapes=[pltpu.VMEM((8, D), F32), pltpu.VMEM((8, LANES), F32)],
        compiler_params=_params(("arbitrary", "arbitrary")),
        name="mixer_out",
    )(og, zb, x, modl, conv_w, wga, wco, wo, n2g, wr, br)


def _dispatch_kernel(d1_ref, d2_ref, h_ref, xs_ref, sem):
    tm = h_ref.shape[0]
    base = pl.program_id(0) * tm

    def issue(t, carry):
        src = h_ref.at[pl.ds(t, 1)]
        pltpu.make_async_copy(src, xs_ref.at[pl.ds(d1_ref[base + t], 1)], sem).start()
        pltpu.make_async_copy(src, xs_ref.at[pl.ds(d2_ref[base + t], 1)], sem).start()
        return carry

    lax.fori_loop(0, tm, issue, 0, unroll=8)
    for _ in range(2):
        pltpu.make_async_copy(h_ref, xs_ref.at[pl.ds(0, tm)], sem).wait()


def _dispatch(h2, dest1, dest2, n_rows):
    N, D = h2.shape
    tm = ROW_TILE
    return pl.pallas_call(
        _dispatch_kernel,
        grid_spec=pltpu.PrefetchScalarGridSpec(
            num_scalar_prefetch=2,
            grid=(N // tm,),
            in_specs=[pl.BlockSpec((tm, D), lambda i, d1, d2: (i, 0))],
            out_specs=pl.BlockSpec(memory_space=pl.ANY),
            scratch_shapes=[pltpu.SemaphoreType.DMA(())],
        ),
        out_shape=jax.ShapeDtypeStruct((n_rows, D), F32),
        compiler_params=_params(("arbitrary",)),
        name="moe_dispatch",
    )(dest1, dest2, h2)


def _expert_kernel(ib_ref, ie_ref, lo_ref, hi_ref, ni_ref, xs_ref, w1_ref, w3_ref, w2_ref, y_ref,
                   w1b, w3b, w2b):
    j = pl.program_id(0)
    jp = jnp.maximum(j - 1, 0)
    live = j < ni_ref[0]
    new_expert = jnp.logical_or(j == 0, ie_ref[j] != ie_ref[jp])
    new_block = jnp.logical_or(j == 0, ib_ref[j] != ib_ref[jp])

    @pl.when(jnp.logical_and(new_expert, live))
    def _():
        w1b[...] = w1_ref[0, 0].astype(BF16)
        w3b[...] = w3_ref[0, 0].astype(BF16)
        w2b[...] = w2_ref[0, 0].astype(BF16)

    @pl.when(live)
    def _():
        xb = xs_ref[...].astype(BF16)
        h1 = _dot(xb, w1b[...])
        h3 = _dot(xb, w3b[...])
        a = (h1 * jax.nn.sigmoid(h1) * h3).astype(BF16)
        y = _dot(a, w2b[...])
        row = lax.broadcasted_iota(jnp.int32, y.shape, 0)
        mine = jnp.logical_and(row >= lo_ref[j], row < hi_ref[j])

        @pl.when(new_block)
        def _():
            y_ref[...] = jnp.where(mine, y, 0.0)

        @pl.when(jnp.logical_not(new_block))
        def _():
            y_ref[...] = jnp.where(mine, y, y_ref[...])


def _experts(xs, items, w1, w3, w2, layer):
    P, D = xs.shape
    _, E, _, DE = w1.shape
    bm = MOE_BLOCK
    n_items_max = P // bm + E
    rows = lambda j, ib, ie, lo, hi, ni: (ib[j], 0)
    wsel = lambda j, ib, ie, lo, hi, ni: (layer, ie[j], 0, 0)
    return pl.pallas_call(
        _expert_kernel,
        grid_spec=pltpu.PrefetchScalarGridSpec(
            num_scalar_prefetch=5,
            grid=(n_items_max,),
            in_specs=[
                pl.BlockSpec((bm, D), rows),
                pl.BlockSpec((1, 1, D, DE), wsel),
                pl.BlockSpec((1, 1, D, DE), wsel),
                pl.BlockSpec((1, 1, DE, D), wsel),
            ],
            out_specs=pl.BlockSpec((bm, D), rows),
            scratch_shapes=[
                pltpu.VMEM((D, DE), BF16),
                pltpu.VMEM((D, DE), BF16),
                pltpu.VMEM((DE, D), BF16),
            ],
        ),
        out_shape=jax.ShapeDtypeStruct((P, D), F32),
        compiler_params=_params(("arbitrary",)),
        name="moe_experts",
    )(*items, xs, w1, w3, w2)


def _combine_kernel(d1_ref, d2_ref, x1_ref, route_ref, mod_ref, fg_ref, yb_ref, o_ref, buf, sem,
                    *, final_norm):
    tm = x1_ref.shape[0]
    base = pl.program_id(0) * tm

    def issue(t, carry):
        pltpu.make_async_copy(yb_ref.at[pl.ds(d1_ref[base + t], 1)], buf.at[0, pl.ds(t, 1)], sem).start()
        pltpu.make_async_copy(yb_ref.at[pl.ds(d2_ref[base + t], 1)], buf.at[1, pl.ds(t, 1)], sem).start()
        return carry

    lax.fori_loop(0, tm, issue, 0, unroll=8)
    for k in range(2):
        pltpu.make_async_copy(yb_ref.at[pl.ds(0, tm)], buf.at[k], sem).wait()

    rec = route_ref[...]
    w1 = rec[:, R_W1:R_W1 + 1]
    w2 = rec[:, R_W2:R_W2 + 1]
    y = w1 * buf[0] + w2 * buf[1]
    x2 = x1_ref[...] + mod_ref[0][5:6] * y
    if final_norm:
        ms = jnp.mean(x2 * x2, axis=-1, keepdims=True)
        x2 = x2 * lax.rsqrt(ms + NORM_EPS) * fg_ref[...]
    o_ref[...] = x2


def _combine(x1, route, modl, final_g, yb, dest1, dest2, seq, final_norm):
    N, D = x1.shape
    tm = ROW_TILE
    kern = functools.partial(_combine_kernel, final_norm=final_norm)
    tok = lambda i, d1, d2: (i, 0)
    return pl.pallas_call(
        kern,
        grid_spec=pltpu.PrefetchScalarGridSpec(
            num_scalar_prefetch=2,
            grid=(N // tm,),
            in_specs=[
                pl.BlockSpec((tm, D), tok),
                pl.BlockSpec((tm, LANES), tok),
                pl.BlockSpec((1, 6, D), lambda i, d1, d2: ((i * tm) // seq, 0, 0)),
                pl.BlockSpec((1, D), lambda i, d1, d2: (0, 0)),
                pl.BlockSpec(memory_space=pl.ANY),
            ],
            out_specs=pl.BlockSpec((tm, D), tok),
            scratch_shapes=[pltpu.VMEM((2, tm, D), F32), pltpu.SemaphoreType.DMA(())],
        ),
        out_shape=jax.ShapeDtypeStruct((N, D), F32),
        compiler_params=_params(("arbitrary",)),
        name="moe_combine",
    )(dest1, dest2, x1, route, modl, final_g, yb)


def _dest_kernel(route_ref, cnt_ref, d1_ref, d2_ref):
    tm = route_ref.shape[0]
    rec = route_ref[...]
    cnt = cnt_ref[...]
    r = lax.broadcasted_iota(jnp.int32, (LANES, LANES), 0)
    c = lax.broadcasted_iota(jnp.int32, (LANES, LANES), 1)
    upper = (r <= c).astype(F32)
    end = jnp.dot(cnt, upper, precision=HIGHEST, preferred_element_type=F32)
    start = (end - cnt)[0:1]
    lane = lax.broadcasted_iota(jnp.int32, (tm, LANES), 1).astype(F32)
    ones = jnp.ones((8, LANES), F32)
    for k, (col_id, col_rank, out) in enumerate(((R_ID1, R_RANK1, d1_ref), (R_ID2, R_RANK2, d2_ref))):
        sel = lane == rec[:, col_id:col_id + 1] + float(N_GROUPS)
        a = jnp.where(sel, start + rec[:, col_rank:col_rank + 1], 0.0)
        row = lax.dot_general(ones, a, (((1,), (1,)), ((), ())), precision=HIGHEST,
                              preferred_element_type=F32)
        out[0] = row[0:1].astype(jnp.int32)


def _destinations(route, cnt):
    N = route.shape[0]
    tm = 1024 if N % 1024 == 0 else ROW_TILE
    out = jax.ShapeDtypeStruct((N // tm, 1, tm), jnp.int32)
    d1, d2 = pl.pallas_call(
        _dest_kernel,
        grid=(N // tm,),
        in_specs=[pl.BlockSpec((tm, LANES), lambda i: (i, 0)), pl.BlockSpec((8, LANES), lambda i: (0, 0))],
        out_specs=[pl.BlockSpec((1, 1, tm), lambda i: (i, 0, 0))] * 2,
        out_shape=[out, out],
        compiler_params=_params(("arbitrary",)),
        name="moe_dest",
    )(route, cnt)
    return d1.reshape(N), d2.reshape(N)


def _work_items(cnt, n_items_max):
    bm = MOE_BLOCK
    i32 = jnp.int32
    counts = cnt[0, N_GROUPS:N_GROUPS + N_EXPERTS].astype(i32)
    end = jnp.cumsum(counts)
    start = end - counts
    first = start // bm
    last = jnp.maximum(end - 1, 0) // bm
    per_e = jnp.where(counts > 0, last - first + 1, 0)
    item_end = jnp.cumsum(per_e)
    item_start = item_end - per_e
    n_items = item_end[-1]
    j = jnp.minimum(jnp.arange(n_items_max, dtype=i32), n_items - 1)
    ie = jnp.minimum(jnp.sum((item_end[None, :] <= j[:, None]).astype(i32), axis=1), N_EXPERTS - 1)
    onehot = ie[:, None] == jnp.arange(N_EXPERTS, dtype=i32)[None, :]
    pick = lambda tbl: jnp.sum(jnp.where(onehot, tbl[None, :], 0), axis=1)
    ib = pick(first) + (j - pick(item_start))
    lo = jnp.maximum(pick(start), ib * bm) - ib * bm
    hi = jnp.minimum(pick(end), (ib + 1) * bm) - ib * bm
    return (ib.astype(i32), ie.astype(i32), lo.astype(i32), hi.astype(i32),
            n_items.reshape(1).astype(i32))


def kernel(x, c, mod_w, mod_b, norm1_g, w_in, gate_w2, gate_b, gla_norm_g, conv_w, w_gla_out,
           w_conv_out, w_out, norm2_g, router_group_w, router_group_b, router_expert_w,
           router_expert_b, expert_w1, expert_w3, expert_w2, final_norm_g):
    B, S, D = x.shape
    L = mod_w.shape[0]
    N = B * S
    dk = gate_w2.shape[2]
    rank = gate_w2.shape[1]
    dv = gla_norm_g.shape[1]
    n_slots = N * 2
    n_items_max = n_slots // MOE_BLOCK + N_EXPERTS
    assert S % GLA_TILE == 0 and S % MIX_TILE == 0 and S % IN_TILE == 0 and N % ROW_TILE == 0
    assert n_slots % MOE_BLOCK == 0
    assert N_GROUPS + N_EXPERTS <= LANES and rank <= LANES

    mod = _modulation(c, mod_w, mod_b)
    xf = x.reshape(N, D)
    o_gd = 2 * dk + dv
    o_r = o_gd + rank
    o_b = o_r + dv
    for l in range(L):
        modl = mod[l].reshape(B, 6, D)
        wl = w_in[l]
        wa = jnp.concatenate([wl[:, :o_gd], wl[:, o_r:o_b]], axis=1).astype(BF16)
        wb = wl[:, o_b:].astype(BF16)
        wg = jnp.pad(wl[:, o_gd:o_r], ((0, 0), (0, LANES - rank))).astype(BF16)
        gw2 = jnp.pad(gate_w2[l], ((0, LANES - rank), (0, 0)))
        za, zb, zg = _in_projection(xf, modl, norm1_g[l].reshape(1, D), wa, wb, wg, S)
        og = _gla(za, zg, gw2, gate_b[l].reshape(1, dk), gla_norm_g[l].reshape(1, dv), B, S, dk, dv)
        wr = jnp.pad(jnp.concatenate([router_group_w[l], router_expert_w[l]], axis=1),
                     ((0, 0), (0, LANES - N_GROUPS - N_EXPERTS)))
        br = jnp.pad(jnp.concatenate([router_group_b[l], router_expert_b[l]]),
                     (0, LANES - N_GROUPS - N_EXPERTS)).reshape(1, LANES)
        x1, h2, route, cnt = _mixer_out(
            og, zb, xf, modl, conv_w[l], w_gla_out[l].astype(BF16), w_conv_out[l].astype(BF16),
            w_out[l].astype(BF16), norm2_g[l].reshape(1, D), wr, br, B, S)
        dest1, dest2 = _destinations(route, cnt)
        items = _work_items(cnt, n_items_max)
        xs = _dispatch(h2, dest1, dest2, n_slots)
        yb = _experts(xs, items, expert_w1, expert_w3, expert_w2, l)
        xf = _combine(x1, route, modl, final_norm_g.reshape(1, D), yb, dest1, dest2, S,
                      final_norm=(l == L - 1))
    return xf.reshape(B, S, D)
```

```python
import functools

import jax
import jax.numpy as jnp
from jax import lax
from jax.experimental import pallas as pl
from jax.experimental.pallas import tpu as pltpu

F32 = jnp.float32
BF16 = jnp.bfloat16
HIGHEST = lax.Precision.HIGHEST

GLA_HEADS = 4
GATE_TAU = 16.0
GLA_CHUNK = 64
N_GROUPS = 8
EXPERTS_PER_GROUP = 8
N_EXPERTS = N_GROUPS * EXPERTS_PER_GROUP
NORM_EPS = 1e-6

LANES = 128
VMEM_LIMIT_BYTES = 56 * 1024 * 1024

IN_TILE = 512
GLA_TILE = 512
MIX_TILE = 256
MOE_BLOCK = 256
ROW_TILE = 256

R_ID1, R_ID2, R_W1, R_W2, R_RANK1, R_RANK2 = 0, 1, 2, 3, 4, 5


def _dot(a, b):
    return jnp.dot(a, b, preferred_element_type=F32)


def _const_spec(shape):
    nd = len(shape)
    return pl.BlockSpec(shape, lambda *_: (0,) * nd, pipeline_mode=pl.Buffered(1))


def _params(sem):
    return pltpu.CompilerParams(dimension_semantics=sem, vmem_limit_bytes=VMEM_LIMIT_BYTES)


def _mod_kernel(c_ref, w_ref, b_ref, o_ref):
    c = c_ref[...]
    sc = c * jax.nn.sigmoid(c)
    o_ref[0] = jnp.dot(sc, w_ref[0], precision=HIGHEST, preferred_element_type=F32) + b_ref[0]


def _modulation(c, mod_w, mod_b):
    L, D, D6 = mod_w.shape
    B = c.shape[0]
    nj = D6 // D
    return pl.pallas_call(
        _mod_kernel,
        grid=(L, nj),
        in_specs=[
            pl.BlockSpec((B, D), lambda l, j: (0, 0)),
            pl.BlockSpec((1, D, D), lambda l, j: (l, 0, j)),
            pl.BlockSpec((1, 1, D), lambda l, j: (l, 0, j)),
        ],
        out_specs=pl.BlockSpec((1, B, D), lambda l, j: (l, 0, j)),
        out_shape=jax.ShapeDtypeStruct((L, B, D6), F32),
        compiler_params=_params(("arbitrary", "arbitrary")),
        name="adaln_mod",
    )(c, mod_w, mod_b.reshape(L, 1, D6))


def _inproj_kernel(x_ref, mod_ref, g_ref, wa_ref, wb_ref, wg_ref, za_ref, zb_ref, zg_ref):
    x = x_ref[...]
    ms = jnp.mean(x * x, axis=-1, keepdims=True)
    y = x * lax.rsqrt(ms + NORM_EPS) * g_ref[...]
    m = mod_ref[0]
    h = (y * (1.0 + m[1:2]) + m[0:1]).astype(BF16)
    d = x.shape[1]
    for j in range(za_ref.shape[1] // d):
        za_ref[:, j * d:(j + 1) * d] = _dot(h, wa_ref[:, j * d:(j + 1) * d]).astype(BF16)
    for j in range(zb_ref.shape[1] // d):
        zb_ref[:, j * d:(j + 1) * d] = _dot(h, wb_ref[:, j * d:(j + 1) * d]).astype(BF16)
    zg_ref[...] = _dot(h, wg_ref[...])


def _in_projection(x, modl, norm_g, wa, wb, wg, seq):
    N, D = x.shape
    tm = IN_TILE
    ca, cb = wa.shape[1], wb.shape[1]
    return pl.pallas_call(
        _inproj_kernel,
        grid=(N // tm,),
        in_specs=[
            pl.BlockSpec((tm, D), lambda i: (i, 0)),
            pl.BlockSpec((1, 6, D), lambda i: ((i * tm) // seq, 0, 0)),
            _const_spec((1, D)),
            _const_spec((D, ca)),
            _const_spec((D, cb)),
            _const_spec((D, LANES)),
        ],
        out_specs=[
            pl.BlockSpec((tm, ca), lambda i: (i, 0)),
            pl.BlockSpec((tm, cb), lambda i: (i, 0)),
            pl.BlockSpec((tm, LANES), lambda i: (i, 0)),
        ],
        out_shape=[
            jax.ShapeDtypeStruct((N, ca), BF16),
            jax.ShapeDtypeStruct((N, cb), BF16),
            jax.ShapeDtypeStruct((N, LANES), F32),
        ],
        compiler_params=_params(("arbitrary",)),
        name="in_projection",
    )(x, modl, norm_g, wa, wb, wg)


def _log_sigmoid(x):
    return jnp.minimum(x, 0.0) - jnp.log1p(jnp.exp(-jnp.abs(x)))


def _gla_kernel(za_ref, zg_ref, gw2_ref, gb_ref, ng_ref, o_ref, st_ref, lg_ref, *, dk, dv):
    heads = GLA_HEADS
    dkh, dvh = dk // heads, dv // heads
    c = GLA_CHUNK
    ts = za_ref.shape[0]

    @pl.when(pl.program_id(1) == 0)
    def _():
        st_ref[...] = jnp.zeros_like(st_ref)

    pre = jnp.dot(zg_ref[...], gw2_ref[...], precision=HIGHEST, preferred_element_type=F32) + gb_ref[...]
    lg_ref[...] = _log_sigmoid(pre) * (1.0 / GATE_TAU)

    row = lax.broadcasted_iota(jnp.int32, (c, c), 0)
    col = lax.broadcasted_iota(jnp.int32, (c, c), 1)
    causal = row >= col
    tril = causal.astype(F32)
    qscale = dkh ** -0.5

    def chunk(ci, carry):
        r0 = pl.multiple_of(ci * c, c)
        rows = pl.ds(r0, c)
        b_all = jnp.dot(tril, lg_ref[rows, :], precision=HIGHEST, preferred_element_type=F32)
        for hd in range(heads):
            ks = slice(hd * dkh, (hd + 1) * dkh)
            q = za_ref[rows, hd * dkh:(hd + 1) * dkh].astype(F32) * qscale
            k = za_ref[rows, dk + hd * dkh:dk + (hd + 1) * dkh].astype(F32)
            v = za_ref[rows, 2 * dk + hd * dvh:2 * dk + (hd + 1) * dvh]
            r = za_ref[rows, 2 * dk + dv + hd * dvh:2 * dk + dv + (hd + 1) * dvh].astype(F32)
            b = b_all[:, ks]
            b_last = b[c - 1:c, :]
            q_t = (q * jnp.exp(b)).astype(BF16)
            k_t = (k * jnp.exp(-b)).astype(BF16)
            k_s = (k * jnp.exp(b_last - b)).astype(BF16)
            decay = jnp.exp(b_last)
            attn = lax.dot_general(q_t, k_t, (((1,), (1,)), ((), ())), preferred_element_type=F32)
            attn = jnp.where(causal, attn, 0.0).astype(BF16)
            st = st_ref[hd]
            o = _dot(attn, v) + lax.dot_general(
                q_t, st.astype(BF16), (((1,), (1,)), ((), ())), preferred_element_type=F32)
            upd = lax.dot_general(v, k_s, (((0,), (0,)), ((), ())), preferred_element_type=F32)
            st_ref[hd] = st * decay + upd
            ms = jnp.mean(o * o, axis=-1, keepdims=True)
            on = o * lax.rsqrt(ms + NORM_EPS) * ng_ref[:, hd * dvh:(hd + 1) * dvh]
            o_ref[rows, hd * dvh:(hd + 1) * dvh] = (on * (r * jax.nn.sigmoid(r))).astype(BF16)
        return carry

    lax.fori_loop(0, ts // c, chunk, 0)


def _gla(za, zg, gw2, gb, ng, batch, seq, dk, dv):
    N = za.shape[0]
    ts = GLA_TILE
    ns = seq // ts
    heads = GLA_HEADS
    kern = functools.partial(_gla_kernel, dk=dk, dv=dv)
    return pl.pallas_call(
        kern,
        grid=(batch, ns),
        in_specs=[
            pl.BlockSpec((ts, za.shape[1]), lambda b, s: (b * ns + s, 0)),
            pl.BlockSpec((ts, LANES), lambda b, s: (b * ns + s, 0)),
            _const_spec((LANES, dk)),
            _const_spec((1, dk)),
            _const_spec((1, dv)),
        ],
        out_specs=pl.BlockSpec((ts, dv), lambda b, s: (b * ns + s, 0)),
        out_shape=jax.ShapeDtypeStruct((N, dv), BF16),
        scratch_shapes=[
            pltpu.VMEM((heads, dv // heads, dk // heads), F32),
            pltpu.VMEM((ts, dk), F32),
        ],
        compiler_params=_params(("arbitrary", "arbitrary")),
        name="gla",
    )(za, zg, gw2, gb, ng)


def _mixout_kernel(og_ref, zb_ref, x_ref, mod_ref, cw_ref, wga_ref, wco_ref, wo_ref, n2_ref,
                   wr_ref, br_ref, x1_ref, h2_ref, route_ref, cnt_ref, carry_ref, run_ref):
    tm, d = x_ref.shape
    first_in_seq = pl.program_id(1) == 0

    @pl.when(jnp.logical_and(pl.program_id(0) == 0, first_in_seq))
    def _():
        run_ref[...] = jnp.zeros_like(run_ref)

    @pl.when(first_in_seq)
    def _():
        carry_ref[...] = jnp.zeros_like(carry_ref)

    m = mod_ref[0]
    cb = zb_ref[:, 0:d].astype(F32)
    cc = zb_ref[:, d:2 * d].astype(F32)
    ch = zb_ref[:, 2 * d:3 * d].astype(F32)
    ga = zb_ref[:, 3 * d:4 * d].astype(F32)
    gc = zb_ref[:, 4 * d:5 * d].astype(F32)

    u = cc * ch
    prev = carry_ref[...]
    rowi = lax.broadcasted_iota(jnp.int32, (tm, d), 0)
    u1 = jnp.where(rowi == 0, prev[7:8], pltpu.roll(u, 1, 0))
    u2 = jnp.where(rowi == 0, prev[6:7], jnp.where(rowi == 1, prev[7:8], pltpu.roll(u, 2, 0)))
    carry_ref[...] = u[tm - 8:tm]
    conv = cw_ref[0:1] * u2 + cw_ref[1:2] * u1 + cw_ref[2:3] * u

    y_conv = _dot((cb * conv).astype(BF16), wco_ref[...])
    y_gla = _dot(og_ref[...], wga_ref[...])
    y = jax.nn.sigmoid(ga) * y_gla + jax.nn.sigmoid(gc) * y_conv
    y = _dot(y.astype(BF16), wo_ref[...])
    x1 = x_ref[...] + m[2:3] * y
    x1_ref[...] = x1

    ms = jnp.mean(x1 * x1, axis=-1, keepdims=True)
    h2 = x1 * lax.rsqrt(ms + NORM_EPS) * n2_ref[...]
    h2 = h2 * (1.0 + m[4:5]) + m[3:4]
    h2_ref[...] = h2

    logits = jnp.dot(h2, wr_ref[...], precision=HIGHEST, preferred_element_type=F32) + br_ref[...]
    lane = lax.broadcasted_iota(jnp.int32, (tm, LANES), 1)
    neg = -jnp.inf
    gl = jnp.where(lane < N_GROUPS, logits, neg)
    gmax = jnp.max(gl, axis=-1, keepdims=True)
    gsum = jnp.sum(jnp.exp(gl - gmax), axis=-1, keepdims=True)
    g_w = 1.0 / gsum
    g_idx = jnp.min(jnp.where(gl == gmax, lane, LANES), axis=-1, keepdims=True)
    lo = N_GROUPS + g_idx * EXPERTS_PER_GROUP
    in_group = jnp.logical_and(lane >= lo, lane < lo + EXPERTS_PER_GROUP)
    el = jnp.where(in_group, logits, neg)
    e1 = jnp.max(el, axis=-1, keepdims=True)
    i1 = jnp.min(jnp.where(el == e1, lane, LANES), axis=-1, keepdims=True)
    el2 = jnp.where(lane == i1, neg, el)
    e2 = jnp.max(el2, axis=-1, keepdims=True)
    i2 = jnp.min(jnp.where(el2 == e2, lane, LANES), axis=-1, keepdims=True)
    ratio = jnp.exp(e2 - e1)
    w1 = g_w / (1.0 + ratio)
    w2 = g_w * ratio / (1.0 + ratio)

    oh1 = lane == i1
    oh2 = lane == i2
    oh1f = oh1.astype(F32)
    oh2f = oh2.astype(F32)
    tr = lax.broadcasted_iota(jnp.int32, (tm, tm), 0)
    tc = lax.broadcasted_iota(jnp.int32, (tm, tm), 1)
    before = (tr > tc).astype(BF16)
    run = run_ref[0:1]
    tot1 = jnp.sum(oh1f, axis=0, keepdims=True)
    tot2 = jnp.sum(oh2f, axis=0, keepdims=True)
    c1 = _dot(before, oh1f.astype(BF16)) + run
    c2 = _dot(before, oh2f.astype(BF16)) + (run + tot1)
    rank1 = jnp.sum(jnp.where(oh1, c1, 0.0), axis=-1, keepdims=True)
    rank2 = jnp.sum(jnp.where(oh2, c2, 0.0), axis=-1, keepdims=True)
    new_run = run + tot1 + tot2
    run_ref[...] = jnp.broadcast_to(new_run, run_ref.shape)
    cnt_ref[...] = jnp.broadcast_to(new_run, cnt_ref.shape)

    rec = jnp.where(lane == R_ID1, (i1 - N_GROUPS).astype(F32), 0.0)
    rec = jnp.where(lane == R_ID2, (i2 - N_GROUPS).astype(F32), rec)
    rec = jnp.where(lane == R_W1, w1, rec)
    rec = jnp.where(lane == R_W2, w2, rec)
    rec = jnp.where(lane == R_RANK1, rank1, rec)
    rec = jnp.where(lane == R_RANK2, rank2, rec)
    route_ref[...] = rec


def _mixer_out(og, zb, x, modl, conv_w, wga, wco, wo, n2g, wr, br, batch, seq):
    N, D = x.shape
    tm = MIX_TILE
    ns = seq // tm
    tok = lambda b, s: (b * ns + s, 0)
    return pl.pallas_call(
        _mixout_kernel,
        grid=(batch, ns),
        in_specs=[
            pl.BlockSpec((tm, D), tok),
            pl.BlockSpec((tm, zb.shape[1]), tok),
            pl.BlockSpec((tm, D), tok),
            pl.BlockSpec((1, 6, D), lambda b, s: (b, 0, 0)),
            _const_spec(conv_w.shape),
            _const_spec((D, D)),
            _const_spec((D, D)),
            _const_spec((D, D)),
            _const_spec((1, D)),
            _const_spec((D, LANES)),
            _const_spec((1, LANES)),
        ],
        out_specs=[
            pl.BlockSpec((tm, D), tok),
            pl.BlockSpec((tm, D), tok),
            pl.BlockSpec((tm, LANES), tok),
            pl.BlockSpec((8, LANES), lambda b, s: (0, 0)),
        ],
        out_shape=[
            jax.ShapeDtypeStruct((N, D), F32),
            jax.ShapeDtypeStruct((N, D), F32),
            jax.ShapeDtypeStruct((N, LANES), F32),
            jax.ShapeDtypeStruct((8, LANES), F32),
        ],
        scratch_shapes=[pltpu.VMEM((8, D), F32), pltpu.VMEM((8, LANES), F32)],
        compiler_params=_params(("arbitrary", "arbitrary")),
        name="mixer_out",
    )(og, zb, x, modl, conv_w, wga, wco, wo, n2g, wr, br)


def _moe_kernel(ib_ref, ie_ref, lo_ref, hi_ref, ni_ref, src_ref, dst_ref, h_ref, w1_ref, w3_ref,
                w2_ref, out_ref, xbuf, ybuf, w1b, w3b, w2b, gsem, ssem):
    j = pl.program_id(0)
    nj = pl.num_programs(0)
    n_items = ni_ref[0]
    bm = xbuf.shape[1]
    live = j < n_items
    jp = jnp.maximum(j - 1, 0)

    def gather(item, s):
        base = ib_ref[item] * bm
        for r in range(bm):
            pltpu.make_async_copy(h_ref.at[pl.ds(src_ref[base + r], 1)], xbuf.at[s, pl.ds(r, 1)],
                                  gsem.at[s]).start()

    def wait_gather(s):
        pltpu.make_async_copy(h_ref.at[pl.ds(0, bm)], xbuf.at[s], gsem.at[s]).wait()

    def wait_scatter(s, n):
        for bit in range(bm.bit_length()):
            size = 1 << bit

            @pl.when((n >> bit) & 1 == 1)
            def _():
                pltpu.make_async_copy(ybuf.at[s, pl.ds(0, size)], out_ref.at[pl.ds(0, size)],
                                      ssem.at[s]).wait()

    new_expert = jnp.logical_or(j == 0, ie_ref[j] != ie_ref[jp])

    @pl.when(jnp.logical_and(new_expert, live))
    def _():
        w1b[...] = w1_ref[0, 0].astype(BF16)
        w3b[...] = w3_ref[0, 0].astype(BF16)
        w2b[...] = w2_ref[0, 0].astype(BF16)

    def step(s):
        o = 1 - s
        if s == 0:
            @pl.when(j == 0)
            def _():
                gather(0, 0)

        @pl.when(jnp.logical_and(j >= 2, j - 2 < n_items))
        def _():
            jj = jnp.maximum(j - 2, 0)
            wait_scatter(s, hi_ref[jj] - lo_ref[jj])

        @pl.when(live)
        def _():
            lo, hi = lo_ref[j], hi_ref[j]
            wait_gather(s)
            gather(jnp.minimum(j + 1, nj - 1), o)
            xb = xbuf[s].astype(BF16)
            h1 = _dot(xb, w1b[...])
            h3 = _dot(xb, w3b[...])
            a = (h1 * jax.nn.sigmoid(h1) * h3).astype(BF16)
            ybuf[s] = _dot(a, w2b[...])
            base = ib_ref[j] * bm
            for r in range(bm):
                cp = pltpu.make_async_copy(ybuf.at[s, pl.ds(r, 1)],
                                           out_ref.at[pl.ds(dst_ref[base + r], 1)], ssem.at[s])

                @pl.when(jnp.logical_and(r >= lo, r < hi))
                def _():
                    cp.start()

        @pl.when(j == n_items)
        def _():
            wait_gather(s)

        @pl.when(j == nj - 1)
        def _():
            @pl.when(live)
            def _():
                wait_gather(o)
                wait_scatter(s, hi_ref[j] - lo_ref[j])

            @pl.when(jnp.logical_and(j >= 1, j - 1 < n_items))
            def _():
                wait_scatter(o, hi_ref[jp] - lo_ref[jp])

    @pl.when(j % 2 == 0)
    def _():
        step(0)

    @pl.when(j % 2 == 1)
    def _():
        step(1)


def _moe(h2, items, src, dst, w1, w3, w2, layer):
    N, D = h2.shape
    _, E, _, DE = w1.shape
    bm = MOE_BLOCK
    n_items_max = src.shape[0] // bm + E
    wsel = lambda j, ib, ie, lo, hi, ni, src, dst: (layer, ie[j], 0, 0)
    return pl.pallas_call(
        _moe_kernel,
        grid_spec=pltpu.PrefetchScalarGridSpec(
            num_scalar_prefetch=7,
            grid=(n_items_max,),
            in_specs=[
                pl.BlockSpec(memory_space=pl.ANY),
                pl.BlockSpec((1, 1, D, DE), wsel),
                pl.BlockSpec((1, 1, D, DE), wsel),
                pl.BlockSpec((1, 1, DE, D), wsel),
            ],
            out_specs=pl.BlockSpec(memory_space=pl.ANY),
            scratch_shapes=[
                pltpu.VMEM((2, bm, D), F32),
                pltpu.VMEM((2, bm, D), F32),
                pltpu.VMEM((D, DE), BF16),
                pltpu.VMEM((D, DE), BF16),
                pltpu.VMEM((DE, D), BF16),
                pltpu.SemaphoreType.DMA((2,)),
                pltpu.SemaphoreType.DMA((2,)),
            ],
        ),
        out_shape=jax.ShapeDtypeStruct((src.shape[0], D), F32),
        compiler_params=_params(("arbitrary",)),
        name="moe_experts",
    )(*items, src, dst, h2, w1, w3, w2)


def _combine_kernel(x1_ref, route_ref, mod_ref, fg_ref, y0_ref, y1_ref, o_ref, *, final_norm):
    rec = route_ref[...]
    w1 = rec[:, R_W1:R_W1 + 1]
    w2 = rec[:, R_W2:R_W2 + 1]
    y = w1 * y0_ref[...] + w2 * y1_ref[...]
    x2 = x1_ref[...] + mod_ref[0][5:6] * y
    if final_norm:
        ms = jnp.mean(x2 * x2, axis=-1, keepdims=True)
        x2 = x2 * lax.rsqrt(ms + NORM_EPS) * fg_ref[...]
    o_ref[...] = x2


def _combine(x1, route, modl, final_g, yk, seq, final_norm):
    N, D = x1.shape
    tm = ROW_TILE
    nt = N // tm
    kern = functools.partial(_combine_kernel, final_norm=final_norm)
    tok = lambda i: (i, 0)
    return pl.pallas_call(
        kern,
        grid=(nt,),
        in_specs=[
            pl.BlockSpec((tm, D), tok),
            pl.BlockSpec((tm, LANES), tok),
            pl.BlockSpec((1, 6, D), lambda i: ((i * tm) // seq, 0, 0)),
            pl.BlockSpec((1, D), lambda i: (0, 0)),
            pl.BlockSpec((tm, D), tok),
            pl.BlockSpec((tm, D), lambda i: (nt + i, 0)),
        ],
        out_specs=pl.BlockSpec((tm, D), tok),
        out_shape=jax.ShapeDtypeStruct((N, D), F32),
        compiler_params=_params(("arbitrary",)),
        name="moe_combine",
    )(x1, route, modl, final_g, yk, yk)


def _dest_kernel(route_ref, cnt_ref, d1_ref, d2_ref):
    tm = route_ref.shape[0]
    rec = route_ref[...]
    cnt = cnt_ref[...]
    r = lax.broadcasted_iota(jnp.int32, (LANES, LANES), 0)
    c = lax.broadcasted_iota(jnp.int32, (LANES, LANES), 1)
    upper = (r <= c).astype(F32)
    end = jnp.dot(cnt, upper, precision=HIGHEST, preferred_element_type=F32)
    start = (end - cnt)[0:1]
    lane = lax.broadcasted_iota(jnp.int32, (tm, LANES), 1).astype(F32)
    ones = jnp.ones((8, LANES), F32)
    for k, (col_id, col_rank, out) in enumerate(((R_ID1, R_RANK1, d1_ref), (R_ID2, R_RANK2, d2_ref))):
        sel = lane == rec[:, col_id:col_id + 1] + float(N_GROUPS)
        a = jnp.where(sel, start + rec[:, col_rank:col_rank + 1], 0.0)
        row = lax.dot_general(ones, a, (((1,), (1,)), ((), ())), precision=HIGHEST,
                              preferred_element_type=F32)
        out[0] = row[0:1].astype(jnp.int32)


def _destinations(route, cnt):
    N = route.shape[0]
    tm = 1024 if N % 1024 == 0 else ROW_TILE
    out = jax.ShapeDtypeStruct((N // tm, 1, tm), jnp.int32)
    d1, d2 = pl.pallas_call(
        _dest_kernel,
        grid=(N // tm,),
        in_specs=[pl.BlockSpec((tm, LANES), lambda i: (i, 0)), pl.BlockSpec((8, LANES), lambda i: (0, 0))],
        out_specs=[pl.BlockSpec((1, 1, tm), lambda i: (i, 0, 0))] * 2,
        out_shape=[out, out],
        compiler_params=_params(("arbitrary",)),
        name="moe_dest",
    )(route, cnt)
    return d1.reshape(N), d2.reshape(N)


def _work_items(cnt, n_items_max):
    bm = MOE_BLOCK
    i32 = jnp.int32
    counts = cnt[0, N_GROUPS:N_GROUPS + N_EXPERTS].astype(i32)
    end = jnp.cumsum(counts)
    start = end - counts
    first = start // bm
    last = jnp.maximum(end - 1, 0) // bm
    per_e = jnp.where(counts > 0, last - first + 1, 0)
    item_end = jnp.cumsum(per_e)
    item_start = item_end - per_e
    n_items = item_end[-1]
    j = jnp.minimum(jnp.arange(n_items_max, dtype=i32), n_items - 1)
    ie = jnp.minimum(jnp.sum((item_end[None, :] <= j[:, None]).astype(i32), axis=1), N_EXPERTS - 1)
    onehot = ie[:, None] == jnp.arange(N_EXPERTS, dtype=i32)[None, :]
    pick = lambda tbl: jnp.sum(jnp.where(onehot, tbl[None, :], 0), axis=1)
    ib = pick(first) + (j - pick(item_start))
    lo = jnp.maximum(pick(start), ib * bm) - ib * bm
    hi = jnp.minimum(pick(end), (ib + 1) * bm) - ib * bm
    return (ib.astype(i32), ie.astype(i32), lo.astype(i32), hi.astype(i32),
            n_items.reshape(1).astype(i32))


def kernel(x, c, mod_w, mod_b, norm1_g, w_in, gate_w2, gate_b, gla_norm_g, conv_w, w_gla_out,
           w_conv_out, w_out, norm2_g, router_group_w, router_group_b, router_expert_w,
           router_expert_b, expert_w1, expert_w3, expert_w2, final_norm_g):
    B, S, D = x.shape
    L = mod_w.shape[0]
    N = B * S
    dk = gate_w2.shape[2]
    rank = gate_w2.shape[1]
    dv = gla_norm_g.shape[1]
    n_slots = N * 2
    n_items_max = n_slots // MOE_BLOCK + N_EXPERTS
    assert S % GLA_TILE == 0 and S % MIX_TILE == 0 and S % IN_TILE == 0 and N % ROW_TILE == 0
    assert n_slots % MOE_BLOCK == 0
    assert N_GROUPS + N_EXPERTS <= LANES and rank <= LANES

    mod = _modulation(c, mod_w, mod_b)
    xf = x.reshape(N, D)
    o_gd = 2 * dk + dv
    o_r = o_gd + rank
    o_b = o_r + dv
    for l in range(L):
        modl = mod[l].reshape(B, 6, D)
        wl = w_in[l]
        wa = jnp.concatenate([wl[:, :o_gd], wl[:, o_r:o_b]], axis=1).astype(BF16)
        wb = wl[:, o_b:].astype(BF16)
        wg = jnp.pad(wl[:, o_gd:o_r], ((0, 0), (0, LANES - rank))).astype(BF16)
        gw2 = jnp.pad(gate_w2[l], ((0, LANES - rank), (0, 0)))
        za, zb, zg = _in_projection(xf, modl, norm1_g[l].reshape(1, D), wa, wb, wg, S)
        og = _gla(za, zg, gw2, gate_b[l].reshape(1, dk), gla_norm_g[l].reshape(1, dv), B, S, dk, dv)
        wr = jnp.pad(jnp.concatenate([router_group_w[l], router_expert_w[l]], axis=1),
                     ((0, 0), (0, LANES - N_GROUPS - N_EXPERTS)))
        br = jnp.pad(jnp.concatenate([router_group_b[l], router_expert_b[l]]),
                     (0, LANES - N_GROUPS - N_EXPERTS)).reshape(1, LANES)
        x1, h2, route, cnt = _mixer_out(
            og, zb, xf, modl, conv_w[l], w_gla_out[l].astype(BF16), w_conv_out[l].astype(BF16),
            w_out[l].astype(BF16), norm2_g[l].reshape(1, D), wr, br, B, S)
        dest1, dest2 = _destinations(route, cnt)
        items = _work_items(cnt, n_items_max)
        tok2 = 2 * jnp.arange(N, dtype=jnp.int32)
        inv = jnp.zeros((n_slots,), jnp.int32).at[jnp.concatenate([dest1, dest2])].set(
            jnp.concatenate([tok2, tok2 + 1]), unique_indices=True)
        yk = _moe(h2, items, inv >> 1, (inv & 1) * N + (inv >> 1), expert_w1, expert_w3, expert_w2, l)
        xf = _combine(x1, route, modl, final_norm_g.reshape(1, D), yk, S, final_norm=(l == L - 1))
    return xf.reshape(B, S, D)
```

```python
import functools

import jax
import jax.numpy as jnp
from jax import lax
from jax.experimental import pallas as pl
from jax.experimental.pallas import tpu as pltpu
from jax.experimental.pallas import tpu_sc as plsc

F32 = jnp.float32
BF16 = jnp.bfloat16
HIGHEST = lax.Precision.HIGHEST

GLA_HEADS = 4
GATE_TAU = 16.0
GLA_CHUNK = 64
N_GROUPS = 8
EXPERTS_PER_GROUP = 8
N_EXPERTS = N_GROUPS * EXPERTS_PER_GROUP
NORM_EPS = 1e-6

LANES = 128
VMEM_LIMIT_BYTES = 56 * 1024 * 1024

IN_TILE = 512
GLA_TILE = 512
MIX_TILE = 256
MOE_BLOCK = 256
ROW_TILE = 256

R_ID1, R_ID2, R_W1, R_W2, R_RANK1, R_RANK2 = 0, 1, 2, 3, 4, 5


def _dot(a, b):
    return jnp.dot(a, b, preferred_element_type=F32)


def _const_spec(shape):
    nd = len(shape)
    return pl.BlockSpec(shape, lambda *_: (0,) * nd, pipeline_mode=pl.Buffered(1))


def _params(sem):
    return pltpu.CompilerParams(dimension_semantics=sem, vmem_limit_bytes=VMEM_LIMIT_BYTES)


def _mod_kernel(c_ref, w_ref, b_ref, o_ref):
    c = c_ref[...]
    sc = c * jax.nn.sigmoid(c)
    o_ref[0] = jnp.dot(sc, w_ref[0], precision=HIGHEST, preferred_element_type=F32) + b_ref[0]


def _modulation(c, mod_w, mod_b):
    L, D, D6 = mod_w.shape
    B = c.shape[0]
    nj = D6 // D
    return pl.pallas_call(
        _mod_kernel,
        grid=(L, nj),
        in_specs=[
            pl.BlockSpec((B, D), lambda l, j: (0, 0)),
            pl.BlockSpec((1, D, D), lambda l, j: (l, 0, j)),
            pl.BlockSpec((1, 1, D), lambda l, j: (l, 0, j)),
        ],
        out_specs=pl.BlockSpec((1, B, D), lambda l, j: (l, 0, j)),
        out_shape=jax.ShapeDtypeStruct((L, B, D6), F32),
        compiler_params=_params(("arbitrary", "arbitrary")),
        name="adaln_mod",
    )(c, mod_w, mod_b.reshape(L, 1, D6))


def _inproj_kernel(x_ref, mod_ref, g_ref, wa_ref, wb_ref, wg_ref, za_ref, zb_ref, zg_ref):
    x = x_ref[...]
    ms = jnp.mean(x * x, axis=-1, keepdims=True)
    y = x * lax.rsqrt(ms + NORM_EPS) * g_ref[...]
    m = mod_ref[0]
    h = (y * (1.0 + m[1:2]) + m[0:1]).astype(BF16)
    d = x.shape[1]
    for j in range(za_ref.shape[1] // d):
        za_ref[:, j * d:(j + 1) * d] = _dot(h, wa_ref[:, j * d:(j + 1) * d]).astype(BF16)
    for j in range(zb_ref.shape[1] // d):
        zb_ref[:, j * d:(j + 1) * d] = _dot(h, wb_ref[:, j * d:(j + 1) * d]).astype(BF16)
    zg_ref[...] = _dot(h, wg_ref[...])


def _in_projection(x, modl, norm_g, wa, wb, wg, seq):
    N, D = x.shape
    tm = IN_TILE
    ca, cb = wa.shape[1], wb.shape[1]
    return pl.pallas_call(
        _inproj_kernel,
        grid=(N // tm,),
        in_specs=[
            pl.BlockSpec((tm, D), lambda i: (i, 0)),
            pl.BlockSpec((1, 6, D), lambda i: ((i * tm) // seq, 0, 0)),
            _const_spec((1, D)),
            _const_spec((D, ca)),
            _const_spec((D, cb)),
            _const_spec((D, LANES)),
        ],
        out_specs=[
            pl.BlockSpec((tm, ca), lambda i: (i, 0)),
            pl.BlockSpec((tm, cb), lambda i: (i, 0)),
            pl.BlockSpec((tm, LANES), lambda i: (i, 0)),
        ],
        out_shape=[
            jax.ShapeDtypeStruct((N, ca), BF16),
            jax.ShapeDtypeStruct((N, cb), BF16),
            jax.ShapeDtypeStruct((N, LANES), F32),
        ],
        compiler_params=_params(("arbitrary",)),
        name="in_projection",
    )(x, modl, norm_g, wa, wb, wg)


def _log_sigmoid(x):
    return jnp.minimum(x, 0.0) - jnp.log1p(jnp.exp(-jnp.abs(x)))


def _gla_kernel(za_ref, zg_ref, gw2_ref, gb_ref, ng_ref, o_ref, st_ref, lg_ref, *, dk, dv):
    heads = GLA_HEADS
    dkh, dvh = dk // heads, dv // heads
    c = GLA_CHUNK
    ts = za_ref.shape[0]

    @pl.when(pl.program_id(1) == 0)
    def _():
        st_ref[...] = jnp.zeros_like(st_ref)

    pre = jnp.dot(zg_ref[...], gw2_ref[...], precision=HIGHEST, preferred_element_type=F32) + gb_ref[...]
    lg_ref[...] = _log_sigmoid(pre) * (1.0 / GATE_TAU)

    row = lax.broadcasted_iota(jnp.int32, (c, c), 0)
    col = lax.broadcasted_iota(jnp.int32, (c, c), 1)
    causal = row >= col
    tril = causal.astype(F32)
    qscale = dkh ** -0.5

    def chunk(ci, carry):
        r0 = pl.multiple_of(ci * c, c)
        rows = pl.ds(r0, c)
        b_all = jnp.dot(tril, lg_ref[rows, :], precision=HIGHEST, preferred_element_type=F32)
        for hd in range(heads):
            ks = slice(hd * dkh, (hd + 1) * dkh)
            q = za_ref[rows, hd * dkh:(hd + 1) * dkh].astype(F32) * qscale
            k = za_ref[rows, dk + hd * dkh:dk + (hd + 1) * dkh].astype(F32)
            v = za_ref[rows, 2 * dk + hd * dvh:2 * dk + (hd + 1) * dvh]
            r = za_ref[rows, 2 * dk + dv + hd * dvh:2 * dk + dv + (hd + 1) * dvh].astype(F32)
            b = b_all[:, ks]
            b_last = b[c - 1:c, :]
            q_t = (q * jnp.exp(b)).astype(BF16)
            k_t = (k * jnp.exp(-b)).astype(BF16)
            k_s = (k * jnp.exp(b_last - b)).astype(BF16)
            decay = jnp.exp(b_last)
            attn = lax.dot_general(q_t, k_t, (((1,), (1,)), ((), ())), preferred_element_type=F32)
            attn = jnp.where(causal, attn, 0.0).astype(BF16)
            st = st_ref[hd]
            o = _dot(attn, v) + lax.dot_general(
                q_t, st.astype(BF16), (((1,), (1,)), ((), ())), preferred_element_type=F32)
            upd = lax.dot_general(v, k_s, (((0,), (0,)), ((), ())), preferred_element_type=F32)
            st_ref[hd] = st * decay + upd
            ms = jnp.mean(o * o, axis=-1, keepdims=True)
            on = o * lax.rsqrt(ms + NORM_EPS) * ng_ref[:, hd * dvh:(hd + 1) * dvh]
            o_ref[rows, hd * dvh:(hd + 1) * dvh] = (on * (r * jax.nn.sigmoid(r))).astype(BF16)
        return carry

    lax.fori_loop(0, ts // c, chunk, 0)


def _gla(za, zg, gw2, gb, ng, batch, seq, dk, dv):
    N = za.shape[0]
    ts = GLA_TILE
    ns = seq // ts
    heads = GLA_HEADS
    kern = functools.partial(_gla_kernel, dk=dk, dv=dv)
    return pl.pallas_call(
        kern,
        grid=(batch, ns),
        in_specs=[
            pl.BlockSpec((ts, za.shape[1]), lambda b, s: (b * ns + s, 0)),
            pl.BlockSpec((ts, LANES), lambda b, s: (b * ns + s, 0)),
            _const_spec((LANES, dk)),
            _const_spec((1, dk)),
            _const_spec((1, dv)),
        ],
        out_specs=pl.BlockSpec((ts, dv), lambda b, s: (b * ns + s, 0)),
        out_shape=jax.ShapeDtypeStruct((N, dv), BF16),
        scratch_shapes=[
            pltpu.VMEM((heads, dv // heads, dk // heads), F32),
            pltpu.VMEM((ts, dk), F32),
        ],
        compiler_params=_params(("arbitrary", "arbitrary")),
        name="gla",
    )(za, zg, gw2, gb, ng)


def _mixout_kernel(og_ref, zb_ref, x_ref, mod_ref, cw_ref, wga_ref, wco_ref, wo_ref, n2_ref,
                   wr_ref, br_ref, x1_ref, h2_ref, route_ref, cnt_ref, carry_ref, run_ref):
    tm, d = x_ref.shape
    first_in_seq = pl.program_id(1) == 0

    @pl.when(jnp.logical_and(pl.program_id(0) == 0, first_in_seq))
    def _():
        run_ref[...] = jnp.zeros_like(run_ref)

    @pl.when(first_in_seq)
    def _():
        carry_ref[...] = jnp.zeros_like(carry_ref)

    m = mod_ref[0]
    cb = zb_ref[:, 0:d].astype(F32)
    cc = zb_ref[:, d:2 * d].astype(F32)
    ch = zb_ref[:, 2 * d:3 * d].astype(F32)
    ga = zb_ref[:, 3 * d:4 * d].astype(F32)
    gc = zb_ref[:, 4 * d:5 * d].astype(F32)

    u = cc * ch
    prev = carry_ref[...]
    rowi = lax.broadcasted_iota(jnp.int32, (tm, d), 0)
    u1 = jnp.where(rowi == 0, prev[7:8], pltpu.roll(u, 1, 0))
    u2 = jnp.where(rowi == 0, prev[6:7], jnp.where(rowi == 1, prev[7:8], pltpu.roll(u, 2, 0)))
    carry_ref[...] = u[tm - 8:tm]
    conv = cw_ref[0:1] * u2 + cw_ref[1:2] * u1 + cw_ref[2:3] * u

    y_conv = _dot((cb * conv).astype(BF16), wco_ref[...])
    y_gla = _dot(og_ref[...], wga_ref[...])
    y = jax.nn.sigmoid(ga) * y_gla + jax.nn.sigmoid(gc) * y_conv
    y = _dot(y.astype(BF16), wo_ref[...])
    x1 = x_ref[...] + m[2:3] * y
    x1_ref[...] = x1

    ms = jnp.mean(x1 * x1, axis=-1, keepdims=True)
    h2 = x1 * lax.rsqrt(ms + NORM_EPS) * n2_ref[...]
    h2 = h2 * (1.0 + m[4:5]) + m[3:4]
    h2_ref[...] = h2

    logits = jnp.dot(h2, wr_ref[...], precision=HIGHEST, preferred_element_type=F32) + br_ref[...]
    lane = lax.broadcasted_iota(jnp.int32, (tm, LANES), 1)
    neg = -jnp.inf
    gl = jnp.where(lane < N_GROUPS, logits, neg)
    gmax = jnp.max(gl, axis=-1, keepdims=True)
    gsum = jnp.sum(jnp.exp(gl - gmax), axis=-1, keepdims=True)
    g_w = 1.0 / gsum
    g_idx = jnp.min(jnp.where(gl == gmax, lane, LANES), axis=-1, keepdims=True)
    lo = N_GROUPS + g_idx * EXPERTS_PER_GROUP
    in_group = jnp.logical_and(lane >= lo, lane < lo + EXPERTS_PER_GROUP)
    el = jnp.where(in_group, logits, neg)
    e1 = jnp.max(el, axis=-1, keepdims=True)
    i1 = jnp.min(jnp.where(el == e1, lane, LANES), axis=-1, keepdims=True)
    el2 = jnp.where(lane == i1, neg, el)
    e2 = jnp.max(el2, axis=-1, keepdims=True)
    i2 = jnp.min(jnp.where(el2 == e2, lane, LANES), axis=-1, keepdims=True)
    ratio = jnp.exp(e2 - e1)
    w1 = g_w / (1.0 + ratio)
    w2 = g_w * ratio / (1.0 + ratio)

    oh1 = lane == i1
    oh2 = lane == i2
    oh1f = oh1.astype(F32)
    oh2f = oh2.astype(F32)
    tr = lax.broadcasted_iota(jnp.int32, (tm, tm), 0)
    tc = lax.broadcasted_iota(jnp.int32, (tm, tm), 1)
    before = (tr > tc).astype(BF16)
    run = run_ref[0:1]
    tot1 = jnp.sum(oh1f, axis=0, keepdims=True)
    tot2 = jnp.sum(oh2f, axis=0, keepdims=True)
    c1 = _dot(before, oh1f.astype(BF16)) + run
    c2 = _dot(before, oh2f.astype(BF16)) + (run + tot1)
    rank1 = jnp.sum(jnp.where(oh1, c1, 0.0), axis=-1, keepdims=True)
    rank2 = jnp.sum(jnp.where(oh2, c2, 0.0), axis=-1, keepdims=True)
    new_run = run + tot1 + tot2
    run_ref[...] = jnp.broadcast_to(new_run, run_ref.shape)
    cnt_ref[...] = jnp.broadcast_to(new_run, cnt_ref.shape)

    rec = jnp.where(lane == R_ID1, (i1 - N_GROUPS).astype(F32), 0.0)
    rec = jnp.where(lane == R_ID2, (i2 - N_GROUPS).astype(F32), rec)
    rec = jnp.where(lane == R_W1, w1, rec)
    rec = jnp.where(lane == R_W2, w2, rec)
    rec = jnp.where(lane == R_RANK1, rank1, rec)
    rec = jnp.where(lane == R_RANK2, rank2, rec)
    route_ref[...] = rec


def _mixer_out(og, zb, x, modl, conv_w, wga, wco, wo, n2g, wr, br, batch, seq):
    N, D = x.shape
    tm = MIX_TILE
    ns = seq // tm
    tok = lambda b, s: (b * ns + s, 0)
    return pl.pallas_call(
        _mixout_kernel,
        grid=(batch, ns),
        in_specs=[
            pl.BlockSpec((tm, D), tok),
            pl.BlockSpec((tm, zb.shape[1]), tok),
            pl.BlockSpec((tm, D), tok),
            pl.BlockSpec((1, 6, D), lambda b, s: (b, 0, 0)),
            _const_spec(conv_w.shape),
            _const_spec((D, D)),
            _const_spec((D, D)),
            _const_spec((D, D)),
            _const_spec((1, D)),
            _const_spec((D, LANES)),
            _const_spec((1, LANES)),
        ],
        out_specs=[
            pl.BlockSpec((tm, D), tok),
            pl.BlockSpec((tm, D), tok),
            pl.BlockSpec((tm, LANES), tok),
            pl.BlockSpec((8, LANES), lambda b, s: (0, 0)),
        ],
        out_shape=[
            jax.ShapeDtypeStruct((N, D), F32),
            jax.ShapeDtypeStruct((N, D), F32),
            jax.ShapeDtypeStruct((N, LANES), F32),
            jax.ShapeDtypeStruct((8, LANES), F32),
        ],
        scratch_shapes=[pltpu.VMEM((8, D), F32), pltpu.VMEM((8, LANES), F32)],
        compiler_params=_params(("arbitrary", "arbitrary")),
        name="mixer_out",
    )(og, zb, x, modl, conv_w, wga, wco, wo, n2g, wr, br)


def _moe_kernel(ib_ref, ie_ref, lo_ref, hi_ref, ni_ref, src_ref, dst_ref, h_ref, w1_ref, w3_ref,
                w2_ref, out_ref, xbuf, ybuf, w1b, w3b, w2b, gsem, ssem):
    j = pl.program_id(0)
    nj = pl.num_programs(0)
    n_items = ni_ref[0]
    bm = xbuf.shape[1]
    live = j < n_items
    jp = jnp.maximum(j - 1, 0)

    def gather(item, s):
        base = ib_ref[item] * bm
        for r in range(bm):
            pltpu.make_async_copy(h_ref.at[pl.ds(src_ref[base + r], 1)], xbuf.at[s, pl.ds(r, 1)],
                                  gsem.at[s]).start()

    def wait_gather(s):
        pltpu.make_async_copy(h_ref.at[pl.ds(0, bm)], xbuf.at[s], gsem.at[s]).wait()

    def wait_scatter(s, n):
        for bit in range(bm.bit_length()):
            size = 1 << bit

            @pl.when((n >> bit) & 1 == 1)
            def _():
                pltpu.make_async_copy(ybuf.at[s, pl.ds(0, size)], out_ref.at[pl.ds(0, size)],
                                      ssem.at[s]).wait()

    new_expert = jnp.logical_or(j == 0, ie_ref[j] != ie_ref[jp])

    @pl.when(jnp.logical_and(new_expert, live))
    def _():
        w1b[...] = w1_ref[0, 0].astype(BF16)
        w3b[...] = w3_ref[0, 0].astype(BF16)
        w2b[...] = w2_ref[0, 0].astype(BF16)

    def step(s):
        o = 1 - s
        if s == 0:
            @pl.when(j == 0)
            def _():
                gather(0, 0)

        @pl.when(jnp.logical_and(j >= 2, j - 2 < n_items))
        def _():
            jj = jnp.maximum(j - 2, 0)
            wait_scatter(s, hi_ref[jj] - lo_ref[jj])

        @pl.when(live)
        def _():
            lo, hi = lo_ref[j], hi_ref[j]
            wait_gather(s)
            gather(jnp.minimum(j + 1, nj - 1), o)
            xb = xbuf[s].astype(BF16)
            h1 = _dot(xb, w1b[...])
            h3 = _dot(xb, w3b[...])
            a = (h1 * jax.nn.sigmoid(h1) * h3).astype(BF16)
            ybuf[s] = _dot(a, w2b[...])
            base = ib_ref[j] * bm
            for r in range(bm):
                cp = pltpu.make_async_copy(ybuf.at[s, pl.ds(r, 1)],
                                           out_ref.at[pl.ds(dst_ref[base + r], 1)], ssem.at[s])

                @pl.when(jnp.logical_and(r >= lo, r < hi))
                def _():
                    cp.start()

        @pl.when(j == n_items)
        def _():
            wait_gather(s)

        @pl.when(j == nj - 1)
        def _():
            @pl.when(live)
            def _():
                wait_gather(o)
                wait_scatter(s, hi_ref[j] - lo_ref[j])

            @pl.when(jnp.logical_and(j >= 1, j - 1 < n_items))
            def _():
                wait_scatter(o, hi_ref[jp] - lo_ref[jp])

    @pl.when(j % 2 == 0)
    def _():
        step(0)

    @pl.when(j % 2 == 1)
    def _():
        step(1)


def _moe(h2, items, src, dst, w1, w3, w2, layer):
    N, D = h2.shape
    _, E, _, DE = w1.shape
    bm = MOE_BLOCK
    n_items_max = src.shape[0] // bm + E
    wsel = lambda j, ib, ie, lo, hi, ni, src, dst: (layer, ie[j], 0, 0)
    return pl.pallas_call(
        _moe_kernel,
        grid_spec=pltpu.PrefetchScalarGridSpec(
            num_scalar_prefetch=7,
            grid=(n_items_max,),
            in_specs=[
                pl.BlockSpec(memory_space=pl.ANY),
                pl.BlockSpec((1, 1, D, DE), wsel),
                pl.BlockSpec((1, 1, D, DE), wsel),
                pl.BlockSpec((1, 1, DE, D), wsel),
            ],
            out_specs=pl.BlockSpec(memory_space=pl.ANY),
            scratch_shapes=[
                pltpu.VMEM((2, bm, D), F32),
                pltpu.VMEM((2, bm, D), F32),
                pltpu.VMEM((D, DE), BF16),
                pltpu.VMEM((D, DE), BF16),
                pltpu.VMEM((DE, D), BF16),
                pltpu.SemaphoreType.DMA((2,)),
                pltpu.SemaphoreType.DMA((2,)),
            ],
        ),
        out_shape=jax.ShapeDtypeStruct((src.shape[0], D), F32),
        compiler_params=_params(("arbitrary",)),
        name="moe_experts",
    )(*items, src, dst, h2, w1, w3, w2)


SC_WINDOW = 128
SC_PIECE = 256


def _sc_mesh():
    return plsc.VectorSubcoreMesh(core_axis_name="core", subcore_axis_name="subcore")


def _piece_index(rows, split):
    return (rows[:, None] * split + jnp.arange(split, dtype=jnp.int32)[None, :]).reshape(1, -1)


def _sc_dispatch(h2, dest1, dest2, n_rows):
    N, D = h2.shape
    w, pc = SC_WINDOW, SC_PIECE
    split = D // pc

    @functools.partial(pl.kernel, out_type=jax.ShapeDtypeStruct((n_rows * split, pc), h2.dtype),
                       mesh=_sc_mesh(), scratch_types=[], name="moe_dispatch_sc")
    def run(x_hbm, i1_hbm, i2_hbm, o_hbm):
        def body(x_vmem, i1_vmem, i2_vmem):
            pltpu.sync_copy(x_vmem, o_hbm.at[i1_vmem.at[0]])
            pltpu.sync_copy(x_vmem, o_hbm.at[i2_vmem.at[0]])

        pltpu.emit_pipeline(
            body,
            grid=(N * split // w,),
            in_specs=[
                pl.BlockSpec((w, pc), lambda i: (i, 0)),
                pl.BlockSpec((1, w), lambda i: (0, i)),
                pl.BlockSpec((1, w), lambda i: (0, i)),
            ],
            out_specs=[],
            core_axis_name=("core", "subcore"),
            dimension_semantics=(pltpu.PARALLEL,),
        )(x_hbm, i1_hbm, i2_hbm)

    xs = run(h2.reshape(N * split, pc), _piece_index(dest1, split), _piece_index(dest2, split))
    return xs.reshape(n_rows, D)


def _sc_return(yb, dest):
    M = dest.shape[0]
    P, D = yb.shape
    w, pc = SC_WINDOW, SC_PIECE
    split = D // pc

    @functools.partial(pl.kernel, out_type=jax.ShapeDtypeStruct((M * split, pc), yb.dtype),
                       mesh=_sc_mesh(), scratch_types=[], name="moe_return_sc")
    def run(y_hbm, i_hbm, o_hbm):
        def body(i_vmem, o_vmem):
            pltpu.sync_copy(y_hbm.at[i_vmem.at[0]], o_vmem)

        pltpu.emit_pipeline(
            body,
            grid=(M * split // w,),
            in_specs=[pl.BlockSpec((1, w), lambda i: (0, i))],
            out_specs=[pl.BlockSpec((w, pc), lambda i: (i, 0))],
            core_axis_name=("core", "subcore"),
            dimension_semantics=(pltpu.PARALLEL,),
        )(i_hbm, o_hbm)

    return run(yb.reshape(P * split, pc), _piece_index(dest, split)).reshape(M, D)


def _expert_kernel(ib_ref, ie_ref, lo_ref, hi_ref, ni_ref, xs_ref, w1_ref, w3_ref, w2_ref, y_ref,
                   w1b, w3b, w2b):
    j = pl.program_id(0)
    jp = jnp.maximum(j - 1, 0)
    live = j < ni_ref[0]
    new_expert = jnp.logical_or(j == 0, ie_ref[j] != ie_ref[jp])
    new_block = jnp.logical_or(j == 0, ib_ref[j] != ib_ref[jp])

    @pl.when(jnp.logical_and(new_expert, live))
    def _():
        w1b[...] = w1_ref[0, 0].astype(BF16)
        w3b[...] = w3_ref[0, 0].astype(BF16)
        w2b[...] = w2_ref[0, 0].astype(BF16)

    @pl.when(live)
    def _():
        xb = xs_ref[...].astype(BF16)
        h1 = _dot(xb, w1b[...])
        h3 = _dot(xb, w3b[...])
        a = (h1 * jax.nn.sigmoid(h1) * h3).astype(BF16)
        y = _dot(a, w2b[...])
        row = lax.broadcasted_iota(jnp.int32, y.shape, 0)
        mine = jnp.logical_and(row >= lo_ref[j], row < hi_ref[j])

        @pl.when(new_block)
        def _():
            y_ref[...] = jnp.where(mine, y, 0.0)

        @pl.when(jnp.logical_not(new_block))
        def _():
            y_ref[...] = jnp.where(mine, y, y_ref[...])


def _experts(xs, items, w1, w3, w2, layer):
    P, D = xs.shape
    _, E, _, DE = w1.shape
    bm = MOE_BLOCK
    n_items_max = P // bm + E
    rows = lambda j, ib, ie, lo, hi, ni: (ib[j], 0)
    wsel = lambda j, ib, ie, lo, hi, ni: (layer, ie[j], 0, 0)
    return pl.pallas_call(
        _expert_kernel,
        grid_spec=pltpu.PrefetchScalarGridSpec(
            num_scalar_prefetch=5,
            grid=(n_items_max,),
            in_specs=[
                pl.BlockSpec((bm, D), rows),
                pl.BlockSpec((1, 1, D, DE), wsel),
                pl.BlockSpec((1, 1, D, DE), wsel),
                pl.BlockSpec((1, 1, DE, D), wsel),
            ],
            out_specs=pl.BlockSpec((bm, D), rows),
            scratch_shapes=[
                pltpu.VMEM((D, DE), BF16),
                pltpu.VMEM((D, DE), BF16),
                pltpu.VMEM((DE, D), BF16),
            ],
        ),
        out_shape=jax.ShapeDtypeStruct((P, D), F32),
        compiler_params=_params(("arbitrary",)),
        name="moe_experts",
    )(*items, xs, w1, w3, w2)


def _combine_kernel(x1_ref, route_ref, mod_ref, fg_ref, y0_ref, y1_ref, o_ref, *, final_norm):
    rec = route_ref[...]
    w1 = rec[:, R_W1:R_W1 + 1]
    w2 = rec[:, R_W2:R_W2 + 1]
    y = w1 * y0_ref[...] + w2 * y1_ref[...]
    x2 = x1_ref[...] + mod_ref[0][5:6] * y
    if final_norm:
        ms = jnp.mean(x2 * x2, axis=-1, keepdims=True)
        x2 = x2 * lax.rsqrt(ms + NORM_EPS) * fg_ref[...]
    o_ref[...] = x2


def _combine(x1, route, modl, final_g, yk, seq, final_norm):
    N, D = x1.shape
    tm = ROW_TILE
    nt = N // tm
    kern = functools.partial(_combine_kernel, final_norm=final_norm)
    tok = lambda i: (i, 0)
    return pl.pallas_call(
        kern,
        grid=(nt,),
        in_specs=[
            pl.BlockSpec((tm, D), tok),
            pl.BlockSpec((tm, LANES), tok),
            pl.BlockSpec((1, 6, D), lambda i: ((i * tm) // seq, 0, 0)),
            pl.BlockSpec((1, D), lambda i: (0, 0)),
            pl.BlockSpec((tm, D), tok),
            pl.BlockSpec((tm, D), lambda i: (nt + i, 0)),
        ],
        out_specs=pl.BlockSpec((tm, D), tok),
        out_shape=jax.ShapeDtypeStruct((N, D), F32),
        compiler_params=_params(("arbitrary",)),
        name="moe_combine",
    )(x1, route, modl, final_g, yk, yk)


def _dest_kernel(route_ref, cnt_ref, d1_ref, d2_ref):
    tm = route_ref.shape[0]
    rec = route_ref[...]
    cnt = cnt_ref[...]
    r = lax.broadcasted_iota(jnp.int32, (LANES, LANES), 0)
    c = lax.broadcasted_iota(jnp.int32, (LANES, LANES), 1)
    upper = (r <= c).astype(F32)
    end = jnp.dot(cnt, upper, precision=HIGHEST, preferred_element_type=F32)
    start = (end - cnt)[0:1]
    lane = lax.broadcasted_iota(jnp.int32, (tm, LANES), 1).astype(F32)
    ones = jnp.ones((8, LANES), F32)
    for k, (col_id, col_rank, out) in enumerate(((R_ID1, R_RANK1, d1_ref), (R_ID2, R_RANK2, d2_ref))):
        sel = lane == rec[:, col_id:col_id + 1] + float(N_GROUPS)
        a = jnp.where(sel, start + rec[:, col_rank:col_rank + 1], 0.0)
        row = lax.dot_general(ones, a, (((1,), (1,)), ((), ())), precision=HIGHEST,
                              preferred_element_type=F32)
        out[0] = row[0:1].astype(jnp.int32)


def _destinations(route, cnt):
    N = route.shape[0]
    tm = 1024 if N % 1024 == 0 else ROW_TILE
    out = jax.ShapeDtypeStruct((N // tm, 1, tm), jnp.int32)
    d1, d2 = pl.pallas_call(
        _dest_kernel,
        grid=(N // tm,),
        in_specs=[pl.BlockSpec((tm, LANES), lambda i: (i, 0)), pl.BlockSpec((8, LANES), lambda i: (0, 0))],
        out_specs=[pl.BlockSpec((1, 1, tm), lambda i: (i, 0, 0))] * 2,
        out_shape=[out, out],
        compiler_params=_params(("arbitrary",)),
        name="moe_dest",
    )(route, cnt)
    return d1.reshape(N), d2.reshape(N)


def _work_items(cnt, n_items_max):
    bm = MOE_BLOCK
    i32 = jnp.int32
    counts = cnt[0, N_GROUPS:N_GROUPS + N_EXPERTS].astype(i32)
    end = jnp.cumsum(counts)
    start = end - counts
    first = start // bm
    last = jnp.maximum(end - 1, 0) // bm
    per_e = jnp.where(counts > 0, last - first + 1, 0)
    item_end = jnp.cumsum(per_e)
    item_start = item_end - per_e
    n_items = item_end[-1]
    j = jnp.minimum(jnp.arange(n_items_max, dtype=i32), n_items - 1)
    ie = jnp.minimum(jnp.sum((item_end[None, :] <= j[:, None]).astype(i32), axis=1), N_EXPERTS - 1)
    onehot = ie[:, None] == jnp.arange(N_EXPERTS, dtype=i32)[None, :]
    pick = lambda tbl: jnp.sum(jnp.where(onehot, tbl[None, :], 0), axis=1)
    ib = pick(first) + (j - pick(item_start))
    lo = jnp.maximum(pick(start), ib * bm) - ib * bm
    hi = jnp.minimum(pick(end), (ib + 1) * bm) - ib * bm
    return (ib.astype(i32), ie.astype(i32), lo.astype(i32), hi.astype(i32),
            n_items.reshape(1).astype(i32))


def kernel(x, c, mod_w, mod_b, norm1_g, w_in, gate_w2, gate_b, gla_norm_g, conv_w, w_gla_out,
           w_conv_out, w_out, norm2_g, router_group_w, router_group_b, router_expert_w,
           router_expert_b, expert_w1, expert_w3, expert_w2, final_norm_g):
    B, S, D = x.shape
    L = mod_w.shape[0]
    N = B * S
    dk = gate_w2.shape[2]
    rank = gate_w2.shape[1]
    dv = gla_norm_g.shape[1]
    n_slots = N * 2
    n_items_max = n_slots // MOE_BLOCK + N_EXPERTS
    assert S % GLA_TILE == 0 and S % MIX_TILE == 0 and S % IN_TILE == 0 and N % ROW_TILE == 0
    assert n_slots % MOE_BLOCK == 0
    assert N_GROUPS + N_EXPERTS <= LANES and rank <= LANES

    mod = _modulation(c, mod_w, mod_b)
    xf = x.reshape(N, D)
    o_gd = 2 * dk + dv
    o_r = o_gd + rank
    o_b = o_r + dv
    for l in range(L):
        modl = mod[l].reshape(B, 6, D)
        wl = w_in[l]
        wa = jnp.concatenate([wl[:, :o_gd], wl[:, o_r:o_b]], axis=1).astype(BF16)
        wb = wl[:, o_b:].astype(BF16)
        wg = jnp.pad(wl[:, o_gd:o_r], ((0, 0), (0, LANES - rank))).astype(BF16)
        gw2 = jnp.pad(gate_w2[l], ((0, LANES - rank), (0, 0)))
        za, zb, zg = _in_projection(xf, modl, norm1_g[l].reshape(1, D), wa, wb, wg, S)
        og = _gla(za, zg, gw2, gate_b[l].reshape(1, dk), gla_norm_g[l].reshape(1, dv), B, S, dk, dv)
        wr = jnp.pad(jnp.concatenate([router_group_w[l], router_expert_w[l]], axis=1),
                     ((0, 0), (0, LANES - N_GROUPS - N_EXPERTS)))
        br = jnp.pad(jnp.concatenate([router_group_b[l], router_expert_b[l]]),
                     (0, LANES - N_GROUPS - N_EXPERTS)).reshape(1, LANES)
        x1, h2, route, cnt = _mixer_out(
            og, zb, xf, modl, conv_w[l], w_gla_out[l].astype(BF16), w_conv_out[l].astype(BF16),
            w_out[l].astype(BF16), norm2_g[l].reshape(1, D), wr, br, B, S)
        dest1, dest2 = _destinations(route, cnt)
        items = _work_items(cnt, n_items_max)
        xs = _sc_dispatch(h2, dest1, dest2, n_slots)
        yb = _experts(xs, items, expert_w1, expert_w3, expert_w2, l)
        yk = _sc_return(yb, jnp.concatenate([dest1, dest2]))
        xf = _combine(x1, route, modl, final_norm_g.reshape(1, D), yk, S, final_norm=(l == L - 1))
    return xf.reshape(B, S, D)
```

```python
import functools

import jax
import jax.numpy as jnp
from jax import lax
from jax.experimental import pallas as pl
from jax.experimental.pallas import tpu as pltpu
from jax.experimental.pallas import tpu_sc as plsc

F32 = jnp.float32
BF16 = jnp.bfloat16
HIGHEST = lax.Precision.HIGHEST

GLA_HEADS = 4
GATE_TAU = 16.0
GLA_CHUNK = 64
N_GROUPS = 8
EXPERTS_PER_GROUP = 8
N_EXPERTS = N_GROUPS * EXPERTS_PER_GROUP
NORM_EPS = 1e-6

LANES = 128
VMEM_LIMIT_BYTES = 56 * 1024 * 1024

IN_TILE = 512
GLA_TILE = 512
MIX_TILE = 256
MOE_BLOCK = 256
ROW_TILE = 256
ROW_PIECE = 256

R_ID1, R_ID2, R_W1, R_W2, R_RANK1, R_RANK2 = 0, 1, 2, 3, 4, 5


def _dot(a, b):
    return jnp.dot(a, b, preferred_element_type=F32)


def _const_spec(shape):
    nd = len(shape)
    return pl.BlockSpec(shape, lambda *_: (0,) * nd, pipeline_mode=pl.Buffered(1))


def _params(sem):
    return pltpu.CompilerParams(dimension_semantics=sem, vmem_limit_bytes=VMEM_LIMIT_BYTES)


def _mod_kernel(c_ref, w_ref, b_ref, o_ref):
    c = c_ref[...]
    sc = c * jax.nn.sigmoid(c)
    o_ref[0] = jnp.dot(sc, w_ref[0], precision=HIGHEST, preferred_element_type=F32) + b_ref[0]


def _modulation(c, mod_w, mod_b):
    L, D, D6 = mod_w.shape
    B = c.shape[0]
    nj = D6 // D
    return pl.pallas_call(
        _mod_kernel,
        grid=(L, nj),
        in_specs=[
            pl.BlockSpec((B, D), lambda l, j: (0, 0)),
            pl.BlockSpec((1, D, D), lambda l, j: (l, 0, j)),
            pl.BlockSpec((1, 1, D), lambda l, j: (l, 0, j)),
        ],
        out_specs=pl.BlockSpec((1, B, D), lambda l, j: (l, 0, j)),
        out_shape=jax.ShapeDtypeStruct((L, B, D6), F32),
        compiler_params=_params(("arbitrary", "arbitrary")),
        name="adaln_mod",
    )(c, mod_w, mod_b.reshape(L, 1, D6))


def _inproj_kernel(x_ref, mod_ref, g_ref, wa_ref, wb_ref, wg_ref, za_ref, zb_ref, zg_ref):
    x = x_ref[...]
    ms = jnp.mean(x * x, axis=-1, keepdims=True)
    y = x * lax.rsqrt(ms + NORM_EPS) * g_ref[...]
    m = mod_ref[0]
    h = (y * (1.0 + m[1:2]) + m[0:1]).astype(BF16)
    d = x.shape[1]
    for j in range(za_ref.shape[1] // d):
        za_ref[:, j * d:(j + 1) * d] = _dot(h, wa_ref[:, j * d:(j + 1) * d]).astype(BF16)
    for j in range(zb_ref.shape[1] // d):
        zb_ref[:, j * d:(j + 1) * d] = _dot(h, wb_ref[:, j * d:(j + 1) * d]).astype(BF16)
    zg_ref[...] = _dot(h, wg_ref[...])


def _in_projection(x, modl, norm_g, wa, wb, wg, seq):
    N, D = x.shape
    tm = IN_TILE
    ca, cb = wa.shape[1], wb.shape[1]
    return pl.pallas_call(
        _inproj_kernel,
        grid=(N // tm,),
        in_specs=[
            pl.BlockSpec((tm, D), lambda i: (i, 0)),
            pl.BlockSpec((1, 6, D), lambda i: ((i * tm) // seq, 0, 0)),
            _const_spec((1, D)),
            _const_spec((D, ca)),
            _const_spec((D, cb)),
            _const_spec((D, LANES)),
        ],
        out_specs=[
            pl.BlockSpec((tm, ca), lambda i: (i, 0)),
            pl.BlockSpec((tm, cb), lambda i: (i, 0)),
            pl.BlockSpec((tm, LANES), lambda i: (i, 0)),
        ],
        out_shape=[
            jax.ShapeDtypeStruct((N, ca), BF16),
            jax.ShapeDtypeStruct((N, cb), BF16),
            jax.ShapeDtypeStruct((N, LANES), F32),
        ],
        compiler_params=_params(("arbitrary",)),
        name="in_projection",
    )(x, modl, norm_g, wa, wb, wg)


def _log_sigmoid(x):
    return jnp.minimum(x, 0.0) - jnp.log1p(jnp.exp(-jnp.abs(x)))


def _gla_kernel(za_ref, zg_ref, gw2_ref, gb_ref, ng_ref, o_ref, st_ref, lg_ref, *, dk, dv, rank):
    heads = GLA_HEADS
    dkh, dvh = dk // heads, dv // heads
    c = GLA_CHUNK
    ts = za_ref.shape[0]

    @pl.when(pl.program_id(1) == 0)
    def _():
        st_ref[...] = jnp.zeros_like(st_ref)

    zg = zg_ref[...]
    zg_hi = zg.astype(BF16)
    zg_lo = (zg - zg_hi.astype(F32)).astype(BF16)
    lane = lax.broadcasted_iota(jnp.int32, zg.shape, 1)
    lhs = jnp.where(jnp.logical_and(lane >= rank, lane < 2 * rank), zg_lo, zg_hi)
    pre = _dot(lhs, gw2_ref[...]) + gb_ref[...]
    lg = _log_sigmoid(pre) * (1.0 / GATE_TAU)
    lg_hi = lg.astype(BF16)
    lg_ref[:, 0:dk] = lg_hi
    lg_ref[:, dk:2 * dk] = (lg - lg_hi.astype(F32)).astype(BF16)

    row = lax.broadcasted_iota(jnp.int32, (c, c), 0)
    col = lax.broadcasted_iota(jnp.int32, (c, c), 1)
    causal = row >= col
    tril = causal.astype(BF16)
    qscale = dkh ** -0.5

    for ci in range(ts // c):
        rows = slice(ci * c, (ci + 1) * c)
        b_two = _dot(tril, lg_ref[rows, :])
        b_all = b_two[:, 0:dk] + b_two[:, dk:2 * dk]
        for hd in range(heads):
            ks = slice(hd * dkh, (hd + 1) * dkh)
            q = za_ref[rows, hd * dkh:(hd + 1) * dkh].astype(F32) * qscale
            k = za_ref[rows, dk + hd * dkh:dk + (hd + 1) * dkh].astype(F32)
            v = za_ref[rows, 2 * dk + hd * dvh:2 * dk + (hd + 1) * dvh]
            r = za_ref[rows, 2 * dk + dv + hd * dvh:2 * dk + dv + (hd + 1) * dvh].astype(F32)
            b = b_all[:, ks]
            b_last = b[c - 1:c, :]
            q_t = (q * jnp.exp(b)).astype(BF16)
            k_t = (k * jnp.exp(-b)).astype(BF16)
            k_s = (k * jnp.exp(b_last - b)).astype(BF16)
            decay = jnp.exp(b_last)
            attn = lax.dot_general(q_t, k_t, (((1,), (1,)), ((), ())), preferred_element_type=F32)
            attn = jnp.where(causal, attn, 0.0).astype(BF16)
            st = st_ref[hd]
            o = _dot(attn, v) + lax.dot_general(
                q_t, st.astype(BF16), (((1,), (1,)), ((), ())), preferred_element_type=F32)
            upd = lax.dot_general(v, k_s, (((0,), (0,)), ((), ())), preferred_element_type=F32)
            st_ref[hd] = st * decay + upd
            ms = jnp.mean(o * o, axis=-1, keepdims=True)
            on = o * lax.rsqrt(ms + NORM_EPS) * ng_ref[:, hd * dvh:(hd + 1) * dvh]
            o_ref[rows, hd * dvh:(hd + 1) * dvh] = (on * (r * jax.nn.sigmoid(r))).astype(BF16)


def _gla(za, zg, gw2, gb, ng, batch, seq, dk, dv, rank):
    N = za.shape[0]
    ts = GLA_TILE
    ns = seq // ts
    heads = GLA_HEADS
    kern = functools.partial(_gla_kernel, dk=dk, dv=dv, rank=rank)
    return pl.pallas_call(
        kern,
        grid=(batch, ns),
        in_specs=[
            pl.BlockSpec((ts, za.shape[1]), lambda b, s: (b * ns + s, 0)),
            pl.BlockSpec((ts, LANES), lambda b, s: (b * ns + s, 0)),
            _const_spec((LANES, dk)),
            _const_spec((1, dk)),
            _const_spec((1, dv)),
        ],
        out_specs=pl.BlockSpec((ts, dv), lambda b, s: (b * ns + s, 0)),
        out_shape=jax.ShapeDtypeStruct((N, dv), BF16),
        scratch_shapes=[
            pltpu.VMEM((heads, dv // heads, dk // heads), F32),
            pltpu.VMEM((ts, 2 * dk), BF16),
        ],
        compiler_params=_params(("arbitrary", "arbitrary")),
        name="gla",
    )(za, zg, gw2, gb, ng)


def _mixout_kernel(og_ref, zb_ref, x_ref, mod_ref, cw_ref, wga_ref, wco_ref, wo_ref, n2_ref,
                   wr_ref, br_ref, x1_ref, h2_ref, route_ref, cnt_ref, carry_ref, run_ref):
    tm, d = x_ref.shape
    first_in_seq = pl.program_id(1) == 0

    @pl.when(jnp.logical_and(pl.program_id(0) == 0, first_in_seq))
    def _():
        run_ref[...] = jnp.zeros_like(run_ref)

    @pl.when(first_in_seq)
    def _():
        carry_ref[...] = jnp.zeros_like(carry_ref)

    m = mod_ref[0]
    cb = zb_ref[:, 0:d].astype(F32)
    cc = zb_ref[:, d:2 * d].astype(F32)
    ch = zb_ref[:, 2 * d:3 * d].astype(F32)
    ga = zb_ref[:, 3 * d:4 * d].astype(F32)
    gc = zb_ref[:, 4 * d:5 * d].astype(F32)

    u = cc * ch
    prev = carry_ref[...]
    rowi = lax.broadcasted_iota(jnp.int32, (tm, d), 0)
    u1 = jnp.where(rowi == 0, prev[7:8], pltpu.roll(u, 1, 0))
    u2 = jnp.where(rowi == 0, prev[6:7], jnp.where(rowi == 1, prev[7:8], pltpu.roll(u, 2, 0)))
    carry_ref[...] = u[tm - 8:tm]
    conv = cw_ref[0:1] * u2 + cw_ref[1:2] * u1 + cw_ref[2:3] * u

    y_conv = _dot((cb * conv).astype(BF16), wco_ref[...])
    y_gla = _dot(og_ref[...], wga_ref[...])
    y = jax.nn.sigmoid(ga) * y_gla + jax.nn.sigmoid(gc) * y_conv
    y = _dot(y.astype(BF16), wo_ref[...])
    x1 = x_ref[...] + m[2:3] * y
    x1_ref[...] = x1

    ms = jnp.mean(x1 * x1, axis=-1, keepdims=True)
    h2 = x1 * lax.rsqrt(ms + NORM_EPS) * n2_ref[...]
    h2 = h2 * (1.0 + m[4:5]) + m[3:4]
    pc = h2_ref.shape[2]
    for p in range(h2_ref.shape[0]):
        h2_ref[p] = h2[:, p * pc:(p + 1) * pc]

    h_hi = h2.astype(BF16)
    h_lo = (h2 - h_hi.astype(F32)).astype(BF16)
    two = _dot(h_hi, wr_ref[...])
    logits = two[:, 0:LANES] + two[:, LANES:2 * LANES] + _dot(h_lo, wr_ref[:, 0:LANES]) + br_ref[...]
    lane = lax.broadcasted_iota(jnp.int32, (tm, LANES), 1)
    neg = -jnp.inf
    gl = jnp.where(lane < N_GROUPS, logits, neg)
    gmax = jnp.max(gl, axis=-1, keepdims=True)
    gsum = jnp.sum(jnp.exp(gl - gmax), axis=-1, keepdims=True)
    g_w = 1.0 / gsum
    g_idx = jnp.min(jnp.where(gl == gmax, lane, LANES), axis=-1, keepdims=True)
    lo = N_GROUPS + g_idx * EXPERTS_PER_GROUP
    in_group = jnp.logical_and(lane >= lo, lane < lo + EXPERTS_PER_GROUP)
    el = jnp.where(in_group, logits, neg)
    e1 = jnp.max(el, axis=-1, keepdims=True)
    i1 = jnp.min(jnp.where(el == e1, lane, LANES), axis=-1, keepdims=True)
    el2 = jnp.where(lane == i1, neg, el)
    e2 = jnp.max(el2, axis=-1, keepdims=True)
    i2 = jnp.min(jnp.where(el2 == e2, lane, LANES), axis=-1, keepdims=True)
    ratio = jnp.exp(e2 - e1)
    w1 = g_w / (1.0 + ratio)
    w2 = g_w * ratio / (1.0 + ratio)

    oh1 = lane == i1
    oh2 = lane == i2
    oh1f = oh1.astype(F32)
    oh2f = oh2.astype(F32)
    tr = lax.broadcasted_iota(jnp.int32, (tm, tm), 0)
    tc = lax.broadcasted_iota(jnp.int32, (tm, tm), 1)
    before = (tr > tc).astype(BF16)
    run = run_ref[0:1]
    tot1 = jnp.sum(oh1f, axis=0, keepdims=True)
    tot2 = jnp.sum(oh2f, axis=0, keepdims=True)
    c1 = _dot(before, oh1f.astype(BF16)) + run
    c2 = _dot(before, oh2f.astype(BF16)) + (run + tot1)
    rank1 = jnp.sum(jnp.where(oh1, c1, 0.0), axis=-1, keepdims=True)
    rank2 = jnp.sum(jnp.where(oh2, c2, 0.0), axis=-1, keepdims=True)
    new_run = run + tot1 + tot2
    run_ref[...] = jnp.broadcast_to(new_run, run_ref.shape)
    cnt_ref[...] = jnp.broadcast_to(new_run, cnt_ref.shape)

    rec = jnp.where(lane == R_ID1, (i1 - N_GROUPS).astype(F32), 0.0)
    rec = jnp.where(lane == R_ID2, (i2 - N_GROUPS).astype(F32), rec)
    rec = jnp.where(lane == R_W1, w1, rec)
    rec = jnp.where(lane == R_W2, w2, rec)
    rec = jnp.where(lane == R_RANK1, rank1, rec)
    rec = jnp.where(lane == R_RANK2, rank2, rec)
    route_ref[...] = rec


def _mixer_out(og, zb, x, modl, conv_w, wga, wco, wo, n2g, wr, br, batch, seq):
    N, D = x.shape
    tm = MIX_TILE
    ns = seq // tm
    tok = lambda b, s: (b * ns + s, 0)
    return pl.pallas_call(
        _mixout_kernel,
        grid=(batch, ns),
        in_specs=[
            pl.BlockSpec((tm, D), tok),
            pl.BlockSpec((tm, zb.shape[1]), tok),
            pl.BlockSpec((tm, D), tok),
            pl.BlockSpec((1, 6, D), lambda b, s: (b, 0, 0)),
            _const_spec(conv_w.shape),
            _const_spec((D, D)),
            _const_spec((D, D)),
            _const_spec((D, D)),
            _const_spec((1, D)),
            _const_spec((D, 2 * LANES)),
            _const_spec((1, LANES)),
        ],
        out_specs=[
            pl.BlockSpec((tm, D), tok),
            pl.BlockSpec((D // ROW_PIECE, tm, ROW_PIECE), lambda b, s: (0, b * ns + s, 0)),
            pl.BlockSpec((tm, LANES), tok),
            pl.BlockSpec((8, LANES), lambda b, s: (0, 0)),
        ],
        out_shape=[
            jax.ShapeDtypeStruct((N, D), F32),
            jax.ShapeDtypeStruct((D // ROW_PIECE, N, ROW_PIECE), F32),
            jax.ShapeDtypeStruct((N, LANES), F32),
            jax.ShapeDtypeStruct((8, LANES), F32),
        ],
        scratch_shapes=[pltpu.VMEM((8, D), F32), pltpu.VMEM((8, LANES), F32)],
        compiler_params=_params(("arbitrary", "arbitrary")),
        name="mixer_out",
    )(og, zb, x, modl, conv_w, wga, wco, wo, n2g, wr, br)


SC_WINDOW = 128


def _sc_mesh():
    return plsc.VectorSubcoreMesh(core_axis_name="core", subcore_axis_name="subcore")


def _piece_index(rows, pieces, n_rows):
    return (jnp.arange(pieces, dtype=jnp.int32)[:, None] * n_rows + rows[None, :]).reshape(1, -1)


def _sc_dispatch(h2, dest1, dest2, n_rows):
    pieces, N, pc = h2.shape
    w = SC_WINDOW

    @functools.partial(pl.kernel, out_type=jax.ShapeDtypeStruct((pieces * n_rows, pc), h2.dtype),
                       mesh=_sc_mesh(), scratch_types=[], name="moe_dispatch_sc")
    def run(x_hbm, i1_hbm, i2_hbm, o_hbm):
        def body(x_vmem, i1_vmem, i2_vmem):
            pltpu.sync_copy(x_vmem, o_hbm.at[i1_vmem.at[0]])
            pltpu.sync_copy(x_vmem, o_hbm.at[i2_vmem.at[0]])

        pltpu.emit_pipeline(
            body,
            grid=(pieces * N // w,),
            in_specs=[
                pl.BlockSpec((w, pc), lambda i: (i, 0)),
                pl.BlockSpec((1, w), lambda i: (0, i)),
                pl.BlockSpec((1, w), lambda i: (0, i)),
            ],
            out_specs=[],
            core_axis_name=("core", "subcore"),
            dimension_semantics=(pltpu.PARALLEL,),
        )(x_hbm, i1_hbm, i2_hbm)

    xs = run(h2.reshape(pieces * N, pc), _piece_index(dest1, pieces, n_rows),
             _piece_index(dest2, pieces, n_rows))
    return xs.reshape(pieces, n_rows, pc)


def _sc_return(yb, dest):
    M = dest.shape[0]
    pieces, P, pc = yb.shape
    w = SC_WINDOW

    @functools.partial(pl.kernel, out_type=jax.ShapeDtypeStruct((pieces * M, pc), yb.dtype),
                       mesh=_sc_mesh(), scratch_types=[], name="moe_return_sc")
    def run(y_hbm, i_hbm, o_hbm):
        def body(i_vmem, o_vmem):
            pltpu.sync_copy(y_hbm.at[i_vmem.at[0]], o_vmem)

        pltpu.emit_pipeline(
            body,
            grid=(pieces * M // w,),
            in_specs=[pl.BlockSpec((1, w), lambda i: (0, i))],
            out_specs=[pl.BlockSpec((w, pc), lambda i: (i, 0))],
            core_axis_name=("core", "subcore"),
            dimension_semantics=(pltpu.PARALLEL,),
        )(i_hbm, o_hbm)

    return run(yb.reshape(pieces * P, pc), _piece_index(dest, pieces, P)).reshape(pieces, M, pc)


def _expert_kernel(ib_ref, ie_ref, lo_ref, hi_ref, ni_ref, xs_ref, w1_ref, w3_ref, w2_ref, y_ref,
                   w1b, w3b, w2b):
    j = pl.program_id(0)
    jp = jnp.maximum(j - 1, 0)
    live = j < ni_ref[0]
    new_expert = jnp.logical_or(j == 0, ie_ref[j] != ie_ref[jp])
    new_block = jnp.logical_or(j == 0, ib_ref[j] != ib_ref[jp])

    @pl.when(jnp.logical_and(new_expert, live))
    def _():
        w1b[...] = w1_ref[0, 0].astype(BF16)
        w3b[...] = w3_ref[0, 0].astype(BF16)
        w2b[...] = w2_ref[0, 0].astype(BF16)

    @pl.when(live)
    def _():
        pieces, _, pc = xs_ref.shape
        xb = jnp.concatenate([xs_ref[p] for p in range(pieces)], axis=-1).astype(BF16)
        h1 = _dot(xb, w1b[...])
        h3 = _dot(xb, w3b[...])
        a = (h1 * jax.nn.sigmoid(h1) * h3).astype(BF16)
        y = _dot(a, w2b[...])
        row = lax.broadcasted_iota(jnp.int32, (y.shape[0], pc), 0)
        mine = jnp.logical_and(row >= lo_ref[j], row < hi_ref[j])

        @pl.when(new_block)
        def _():
            for p in range(pieces):
                y_ref[p] = jnp.where(mine, y[:, p * pc:(p + 1) * pc], 0.0)

        @pl.when(jnp.logical_not(new_block))
        def _():
            for p in range(pieces):
                y_ref[p] = jnp.where(mine, y[:, p * pc:(p + 1) * pc], y_ref[p])


def _experts(xs, items, w1, w3, w2, layer):
    pieces, P, pc = xs.shape
    _, E, D, DE = w1.shape
    bm = MOE_BLOCK
    n_items_max = P // bm + E
    rows = lambda j, ib, ie, lo, hi, ni: (0, ib[j], 0)
    wsel = lambda j, ib, ie, lo, hi, ni: (layer, ie[j], 0, 0)
    return pl.pallas_call(
        _expert_kernel,
        grid_spec=pltpu.PrefetchScalarGridSpec(
            num_scalar_prefetch=5,
            grid=(n_items_max,),
            in_specs=[
                pl.BlockSpec((pieces, bm, pc), rows),
                pl.BlockSpec((1, 1, D, DE), wsel),
                pl.BlockSpec((1, 1, D, DE), wsel),
                pl.BlockSpec((1, 1, DE, D), wsel),
            ],
            out_specs=pl.BlockSpec((pieces, bm, pc), rows),
            scratch_shapes=[
                pltpu.VMEM((D, DE), BF16),
                pltpu.VMEM((D, DE), BF16),
                pltpu.VMEM((DE, D), BF16),
            ],
        ),
        out_shape=jax.ShapeDtypeStruct((pieces, P, pc), F32),
        compiler_params=_params(("arbitrary",)),
        name="moe_experts",
    )(*items, xs, w1, w3, w2)


def _combine_kernel(x1_ref, route_ref, mod_ref, fg_ref, y0_ref, y1_ref, o_ref, *, final_norm):
    rec = route_ref[...]
    w1 = rec[:, R_W1:R_W1 + 1]
    w2 = rec[:, R_W2:R_W2 + 1]
    pieces = y0_ref.shape[0]
    y0 = jnp.concatenate([y0_ref[p] for p in range(pieces)], axis=-1)
    y1 = jnp.concatenate([y1_ref[p] for p in range(pieces)], axis=-1)
    y = w1 * y0 + w2 * y1
    x2 = x1_ref[...] + mod_ref[0][5:6] * y
    if final_norm:
        ms = jnp.mean(x2 * x2, axis=-1, keepdims=True)
        x2 = x2 * lax.rsqrt(ms + NORM_EPS) * fg_ref[...]
    o_ref[...] = x2


def _combine(x1, route, modl, final_g, yk, seq, final_norm):
    N, D = x1.shape
    tm = ROW_TILE
    nt = N // tm
    kern = functools.partial(_combine_kernel, final_norm=final_norm)
    tok = lambda i: (i, 0)
    return pl.pallas_call(
        kern,
        grid=(nt,),
        in_specs=[
            pl.BlockSpec((tm, D), tok),
            pl.BlockSpec((tm, LANES), tok),
            pl.BlockSpec((1, 6, D), lambda i: ((i * tm) // seq, 0, 0)),
            pl.BlockSpec((1, D), lambda i: (0, 0)),
            pl.BlockSpec((D // ROW_PIECE, tm, ROW_PIECE), lambda i: (0, i, 0)),
            pl.BlockSpec((D // ROW_PIECE, tm, ROW_PIECE), lambda i: (0, nt + i, 0)),
        ],
        out_specs=pl.BlockSpec((tm, D), tok),
        out_shape=jax.ShapeDtypeStruct((N, D), F32),
        compiler_params=_params(("arbitrary",)),
        name="moe_combine",
    )(x1, route, modl, final_g, yk, yk)


def _dest_kernel(route_ref, cnt_ref, d1_ref, d2_ref):
    tm = route_ref.shape[0]
    rec = route_ref[...]
    cnt = cnt_ref[...]
    r = lax.broadcasted_iota(jnp.int32, (LANES, LANES), 0)
    c = lax.broadcasted_iota(jnp.int32, (LANES, LANES), 1)
    upper = (r <= c).astype(F32)
    end = jnp.dot(cnt, upper, precision=HIGHEST, preferred_element_type=F32)
    start = (end - cnt)[0:1]
    lane = lax.broadcasted_iota(jnp.int32, (tm, LANES), 1).astype(F32)
    ones = jnp.ones((8, LANES), F32)
    for k, (col_id, col_rank, out) in enumerate(((R_ID1, R_RANK1, d1_ref), (R_ID2, R_RANK2, d2_ref))):
        sel = lane == rec[:, col_id:col_id + 1] + float(N_GROUPS)
        a = jnp.where(sel, start + rec[:, col_rank:col_rank + 1], 0.0)
        row = lax.dot_general(ones, a, (((1,), (1,)), ((), ())), precision=HIGHEST,
                              preferred_element_type=F32)
        out[0] = row[0:1].astype(jnp.int32)


def _destinations(route, cnt):
    N = route.shape[0]
    tm = 1024 if N % 1024 == 0 else ROW_TILE
    out = jax.ShapeDtypeStruct((N // tm, 1, tm), jnp.int32)
    d1, d2 = pl.pallas_call(
        _dest_kernel,
        grid=(N // tm,),
        in_specs=[pl.BlockSpec((tm, LANES), lambda i: (i, 0)), pl.BlockSpec((8, LANES), lambda i: (0, 0))],
        out_specs=[pl.BlockSpec((1, 1, tm), lambda i: (i, 0, 0))] * 2,
        out_shape=[out, out],
        compiler_params=_params(("arbitrary",)),
        name="moe_dest",
    )(route, cnt)
    return d1.reshape(N), d2.reshape(N)


def _work_items(cnt, n_items_max):
    bm = MOE_BLOCK
    i32 = jnp.int32
    counts = cnt[0, N_GROUPS:N_GROUPS + N_EXPERTS].astype(i32)
    end = jnp.cumsum(counts)
    start = end - counts
    first = start // bm
    last = jnp.maximum(end - 1, 0) // bm
    per_e = jnp.where(counts > 0, last - first + 1, 0)
    item_end = jnp.cumsum(per_e)
    item_start = item_end - per_e
    n_items = item_end[-1]
    j = jnp.minimum(jnp.arange(n_items_max, dtype=i32), n_items - 1)
    ie = jnp.minimum(jnp.sum((item_end[None, :] <= j[:, None]).astype(i32), axis=1), N_EXPERTS - 1)
    onehot = ie[:, None] == jnp.arange(N_EXPERTS, dtype=i32)[None, :]
    pick = lambda tbl: jnp.sum(jnp.where(onehot, tbl[None, :], 0), axis=1)
    ib = pick(first) + (j - pick(item_start))
    lo = jnp.maximum(pick(start), ib * bm) - ib * bm
    hi = jnp.minimum(pick(end), (ib + 1) * bm) - ib * bm
    return (ib.astype(i32), ie.astype(i32), lo.astype(i32), hi.astype(i32),
            n_items.reshape(1).astype(i32))


def kernel(x, c, mod_w, mod_b, norm1_g, w_in, gate_w2, gate_b, gla_norm_g, conv_w, w_gla_out,
           w_conv_out, w_out, norm2_g, router_group_w, router_group_b, router_expert_w,
           router_expert_b, expert_w1, expert_w3, expert_w2, final_norm_g):
    B, S, D = x.shape
    L = mod_w.shape[0]
    N = B * S
    dk = gate_w2.shape[2]
    rank = gate_w2.shape[1]
    dv = gla_norm_g.shape[1]
    n_slots = N * 2
    n_items_max = n_slots // MOE_BLOCK + N_EXPERTS
    assert S % GLA_TILE == 0 and S % MIX_TILE == 0 and S % IN_TILE == 0 and N % ROW_TILE == 0
    assert n_slots % MOE_BLOCK == 0
    assert N_GROUPS + N_EXPERTS <= LANES and 3 * rank <= LANES and D % ROW_PIECE == 0

    mod = _modulation(c, mod_w, mod_b)
    xf = x.reshape(N, D)
    o_gd = 2 * dk + dv
    o_r = o_gd + rank
    o_b = o_r + dv
    for l in range(L):
        modl = mod[l].reshape(B, 6, D)
        wl = w_in[l]
        wa = jnp.concatenate([wl[:, :o_gd], wl[:, o_r:o_b]], axis=1).astype(BF16)
        wb = wl[:, o_b:].astype(BF16)
        wgd = wl[:, o_gd:o_r]
        wg = jnp.pad(jnp.concatenate([wgd, wgd, wgd], axis=1), ((0, 0), (0, LANES - 3 * rank))).astype(BF16)
        g_hi = gate_w2[l].astype(BF16)
        g_lo = (gate_w2[l] - g_hi.astype(F32)).astype(BF16)
        gw2 = jnp.pad(jnp.concatenate([g_hi, g_hi, g_lo], axis=0), ((0, LANES - 3 * rank), (0, 0)))
        za, zb, zg = _in_projection(xf, modl, norm1_g[l].reshape(1, D), wa, wb, wg, S)
        og = _gla(za, zg, gw2, gate_b[l].reshape(1, dk), gla_norm_g[l].reshape(1, dv), B, S, dk, dv, rank)
        wr = jnp.pad(jnp.concatenate([router_group_w[l], router_expert_w[l]], axis=1),
                     ((0, 0), (0, LANES - N_GROUPS - N_EXPERTS)))
        wr_hi = wr.astype(BF16)
        wr = jnp.concatenate([wr_hi, (wr - wr_hi.astype(F32)).astype(BF16)], axis=1)
        br = jnp.pad(jnp.concatenate([router_group_b[l], router_expert_b[l]]),
                     (0, LANES - N_GROUPS - N_EXPERTS)).reshape(1, LANES)
        x1, h2, route, cnt = _mixer_out(
            og, zb, xf, modl, conv_w[l], w_gla_out[l].astype(BF16), w_conv_out[l].astype(BF16),
            w_out[l].astype(BF16), norm2_g[l].reshape(1, D), wr, br, B, S)
        dest1, dest2 = _destinations(route, cnt)
        items = _work_items(cnt, n_items_max)
        xs = _sc_dispatch(h2, dest1, dest2, n_slots)
        yb = _experts(xs, items, expert_w1, expert_w3, expert_w2, l)
        yk = _sc_return(yb, jnp.concatenate([dest1, dest2]))
        xf = _combine(x1, route, modl, final_norm_g.reshape(1, D), yk, S, final_norm=(l == L - 1))
    return xf.reshape(B, S, D)
```

```python
import functools

import jax
import jax.numpy as jnp
from jax import lax
from jax.experimental import pallas as pl
from jax.experimental.pallas import tpu as pltpu
from jax.experimental.pallas import tpu_sc as plsc

F32 = jnp.float32
BF16 = jnp.bfloat16
HIGHEST = lax.Precision.HIGHEST

GLA_HEADS = 4
GATE_TAU = 16.0
GLA_CHUNK = 64
N_GROUPS = 8
EXPERTS_PER_GROUP = 8
N_EXPERTS = N_GROUPS * EXPERTS_PER_GROUP
NORM_EPS = 1e-6

LANES = 128
VMEM_LIMIT_BYTES = 56 * 1024 * 1024

IN_TILE = 512
GLA_TILE = 512
MIX_TILE = 512
MIX_SUB = 256
MOE_BLOCK = 512
ROW_TILE = 256
ROW_PIECE = 256

R_ID1, R_ID2, R_W1, R_W2, R_RANK1, R_RANK2 = 0, 1, 2, 3, 4, 5


def _dot(a, b):
    return jnp.dot(a, b, preferred_element_type=F32)


def _const_spec(shape):
    nd = len(shape)
    return pl.BlockSpec(shape, lambda *_: (0,) * nd, pipeline_mode=pl.Buffered(1))


def _params(sem):
    return pltpu.CompilerParams(dimension_semantics=sem, vmem_limit_bytes=VMEM_LIMIT_BYTES)


def _mod_kernel(c_ref, w_ref, b_ref, o_ref):
    c = c_ref[...]
    sc = c * jax.nn.sigmoid(c)
    o_ref[0] = jnp.dot(sc, w_ref[0], precision=HIGHEST, preferred_element_type=F32) + b_ref[0]


def _modulation(c, mod_w, mod_b):
    L, D, D6 = mod_w.shape
    B = c.shape[0]
    nj = D6 // D
    return pl.pallas_call(
        _mod_kernel,
        grid=(L, nj),
        in_specs=[
            pl.BlockSpec((B, D), lambda l, j: (0, 0)),
            pl.BlockSpec((1, D, D), lambda l, j: (l, 0, j)),
            pl.BlockSpec((1, 1, D), lambda l, j: (l, 0, j)),
        ],
        out_specs=pl.BlockSpec((1, B, D), lambda l, j: (l, 0, j)),
        out_shape=jax.ShapeDtypeStruct((L, B, D6), F32),
        compiler_params=_params(("arbitrary", "arbitrary")),
        name="adaln_mod",
    )(c, mod_w, mod_b.reshape(L, 1, D6))


CONV_COLS = 256


def _moe_mix(route_ref, y0_ref, y1_ref):
    rec = route_ref[...]
    w1 = rec[:, R_W1:R_W1 + 1]
    w2 = rec[:, R_W2:R_W2 + 1]
    pieces = y0_ref.shape[0]
    y0 = jnp.concatenate([y0_ref[p] for p in range(pieces)], axis=-1)
    y1 = jnp.concatenate([y1_ref[p] for p in range(pieces)], axis=-1)
    return w1 * y0 + w2 * y1


def _inproj_kernel(*refs, seq, fuse_combine):
    if fuse_combine:
        (x_ref, route_ref, y0_ref, y1_ref, modp_ref, mod_ref, g_ref, cw_ref, wa_ref, wb_ref, wg_ref,
         xo_ref, za_ref, zb_ref, zg_ref, carry_ref) = refs
        x = x_ref[...] + modp_ref[0][5:6] * _moe_mix(route_ref, y0_ref, y1_ref)
        xo_ref[...] = x
    else:
        (x_ref, mod_ref, g_ref, cw_ref, wa_ref, wb_ref, wg_ref, za_ref, zb_ref, zg_ref, carry_ref) = refs
        x = x_ref[...]
    tm, d = x.shape

    @pl.when((pl.program_id(0) * tm) % seq == 0)
    def _():
        carry_ref[...] = jnp.zeros_like(carry_ref)

    ms = jnp.mean(x * x, axis=-1, keepdims=True)
    y = x * lax.rsqrt(ms + NORM_EPS) * g_ref[...]
    m = mod_ref[0]
    h = (y * (1.0 + m[1:2]) + m[0:1]).astype(BF16)
    for j in range(za_ref.shape[1] // d):
        za_ref[:, j * d:(j + 1) * d] = _dot(h, wa_ref[:, j * d:(j + 1) * d]).astype(BF16)
    zg_ref[...] = _dot(h, wg_ref[...])
    for j in range(1, zb_ref.shape[1] // d):
        zb_ref[:, j * d:(j + 1) * d] = _dot(h, wb_ref[:, (j + 2) * d:(j + 3) * d]).astype(BF16)

    w = CONV_COLS
    rowi = lax.broadcasted_iota(jnp.int32, (tm, w), 0)
    for j in range(d // w):
        cols = slice(j * w, (j + 1) * w)
        cb = _dot(h, wb_ref[:, j * w:(j + 1) * w])
        cc = _dot(h, wb_ref[:, d + j * w:d + (j + 1) * w])
        ch = _dot(h, wb_ref[:, 2 * d + j * w:2 * d + (j + 1) * w])
        u = cc * ch
        prev = carry_ref[:, cols]
        u1 = jnp.where(rowi == 0, prev[7:8], pltpu.roll(u, 1, 0))
        u2 = jnp.where(rowi == 0, prev[6:7], jnp.where(rowi == 1, prev[7:8], pltpu.roll(u, 2, 0)))
        carry_ref[:, cols] = u[tm - 8:tm]
        conv = cw_ref[0:1, cols] * u2 + cw_ref[1:2, cols] * u1 + cw_ref[2:3, cols] * u
        zb_ref[:, cols] = (cb * conv).astype(BF16)


def _in_projection(x, modl, norm_g, conv_w, wa, wb, wg, seq, combine=None):
    N, D = x.shape
    tm = IN_TILE
    nt = N // tm
    ca, cb = wa.shape[1], wb.shape[1] - 2 * D
    tok = lambda i: (i, 0)
    per_batch = lambda i: ((i * tm) // seq, 0, 0)
    in_specs = [pl.BlockSpec((tm, D), tok)]
    args = [x]
    out_specs, out_shape = [], []
    if combine is not None:
        route, yk, mod_prev = combine
        piece_blk = (D // ROW_PIECE, tm, ROW_PIECE)
        in_specs += [
            pl.BlockSpec((tm, LANES), tok),
            pl.BlockSpec(piece_blk, lambda i: (0, i, 0)),
            pl.BlockSpec(piece_blk, lambda i: (0, nt + i, 0)),
            pl.BlockSpec((1, 6, D), per_batch),
        ]
        args += [route, yk, yk, mod_prev]
        out_specs.append(pl.BlockSpec((tm, D), tok))
        out_shape.append(jax.ShapeDtypeStruct((N, D), F32))
    in_specs += [
        pl.BlockSpec((1, 6, D), per_batch),
        _const_spec((1, D)),
        _const_spec(conv_w.shape),
        _const_spec((D, ca)),
        _const_spec(wb.shape),
        _const_spec((D, LANES)),
    ]
    args += [modl, norm_g, conv_w, wa, wb, wg]
    out_specs += [
        pl.BlockSpec((tm, ca), tok),
        pl.BlockSpec((tm, cb), tok),
        pl.BlockSpec((tm, LANES), tok),
    ]
    out_shape += [
        jax.ShapeDtypeStruct((N, ca), BF16),
        jax.ShapeDtypeStruct((N, cb), BF16),
        jax.ShapeDtypeStruct((N, LANES), F32),
    ]
    kern = functools.partial(_inproj_kernel, seq=seq, fuse_combine=combine is not None)
    return pl.pallas_call(
        kern,
        grid=(nt,),
        in_specs=in_specs,
        out_specs=out_specs,
        out_shape=out_shape,
        scratch_shapes=[pltpu.VMEM((8, D), F32)],
        compiler_params=_params(("arbitrary",)),
        name="in_projection",
    )(*args)


def _log_sigmoid(x):
    return jnp.minimum(x, 0.0) - jnp.log1p(jnp.exp(-jnp.abs(x)))


def _gla_kernel(za_ref, zg_ref, gw2_ref, gb_ref, ng_ref, o_ref, st_ref, lg_ref, *, dk, dv, rank):
    heads = GLA_HEADS
    dkh, dvh = dk // heads, dv // heads
    c = GLA_CHUNK
    ts = za_ref.shape[0]

    @pl.when(pl.program_id(1) == 0)
    def _():
        st_ref[...] = jnp.zeros_like(st_ref)

    zg = zg_ref[...]
    zg_hi = zg.astype(BF16)
    zg_lo = (zg - zg_hi.astype(F32)).astype(BF16)
    lane = lax.broadcasted_iota(jnp.int32, zg.shape, 1)
    lhs = jnp.where(jnp.logical_and(lane >= rank, lane < 2 * rank), zg_lo, zg_hi)
    pre = _dot(lhs, gw2_ref[...]) + gb_ref[...]
    lg = _log_sigmoid(pre) * (1.0 / GATE_TAU)
    lg_hi = lg.astype(BF16)
    lg_ref[:, 0:dk] = lg_hi
    lg_ref[:, dk:2 * dk] = (lg - lg_hi.astype(F32)).astype(BF16)

    row = lax.broadcasted_iota(jnp.int32, (c, c), 0)
    col = lax.broadcasted_iota(jnp.int32, (c, c), 1)
    causal = row >= col
    tril = causal.astype(BF16)
    qscale = dkh ** -0.5

    for ci in range(ts // c):
        rows = slice(ci * c, (ci + 1) * c)
        b_two = _dot(tril, lg_ref[rows, :])
        b_all = b_two[:, 0:dk] + b_two[:, dk:2 * dk]
        for hd in range(heads):
            ks = slice(hd * dkh, (hd + 1) * dkh)
            q = za_ref[rows, hd * dkh:(hd + 1) * dkh].astype(F32) * qscale
            k = za_ref[rows, dk + hd * dkh:dk + (hd + 1) * dkh].astype(F32)
            v = za_ref[rows, 2 * dk + hd * dvh:2 * dk + (hd + 1) * dvh]
            r = za_ref[rows, 2 * dk + dv + hd * dvh:2 * dk + dv + (hd + 1) * dvh].astype(F32)
            b = b_all[:, ks]
            b_last = b[c - 1:c, :]
            q_t = (q * jnp.exp(b)).astype(BF16)
            k_t = (k * jnp.exp(-b)).astype(BF16)
            k_s = (k * jnp.exp(b_last - b)).astype(BF16)
            decay = jnp.exp(b_last)
            attn = lax.dot_general(q_t, k_t, (((1,), (1,)), ((), ())), preferred_element_type=F32)
            attn = jnp.where(causal, attn, 0.0).astype(BF16)
            st = st_ref[hd]
            o = _dot(attn, v) + lax.dot_general(
                q_t, st.astype(BF16), (((1,), (1,)), ((), ())), preferred_element_type=F32)
            upd = lax.dot_general(v, k_s, (((0,), (0,)), ((), ())), preferred_element_type=F32)
            st_ref[hd] = st * decay + upd
            ms = jnp.mean(o * o, axis=-1, keepdims=True)
            on = o * lax.rsqrt(ms + NORM_EPS) * ng_ref[:, hd * dvh:(hd + 1) * dvh]
            o_ref[rows, hd * dvh:(hd + 1) * dvh] = (on * (r * jax.nn.sigmoid(r))).astype(BF16)


def _gla(za, zg, gw2, gb, ng, batch, seq, dk, dv, rank):
    N = za.shape[0]
    ts = GLA_TILE
    ns = seq // ts
    heads = GLA_HEADS
    kern = functools.partial(_gla_kernel, dk=dk, dv=dv, rank=rank)
    return pl.pallas_call(
        kern,
        grid=(batch, ns),
        in_specs=[
            pl.BlockSpec((ts, za.shape[1]), lambda b, s: (b * ns + s, 0)),
            pl.BlockSpec((ts, LANES), lambda b, s: (b * ns + s, 0)),
            _const_spec((LANES, dk)),
            _const_spec((1, dk)),
            _const_spec((1, dv)),
        ],
        out_specs=pl.BlockSpec((ts, dv), lambda b, s: (b * ns + s, 0)),
        out_shape=jax.ShapeDtypeStruct((N, dv), BF16),
        scratch_shapes=[
            pltpu.VMEM((heads, dv // heads, dk // heads), F32),
            pltpu.VMEM((ts, 2 * dk), BF16),
        ],
        compiler_params=_params(("arbitrary", "arbitrary")),
        name="gla",
    )(za, zg, gw2, gb, ng)


def _mixout_kernel(og_ref, zb_ref, x_ref, mod_ref, wga_ref, wco_ref, wo_ref, n2_ref,
                   wr_ref, br_ref, x1_ref, h2_ref, route_ref, cnt_ref, run_ref):
    tm, d = x_ref.shape

    @pl.when(jnp.logical_and(pl.program_id(0) == 0, pl.program_id(1) == 0))
    def _():
        run_ref[...] = jnp.zeros_like(run_ref)

    m = mod_ref[0]
    sub = MIX_SUB
    lane = lax.broadcasted_iota(jnp.int32, (sub, LANES), 1)
    tr = lax.broadcasted_iota(jnp.int32, (sub, sub), 0)
    tc = lax.broadcasted_iota(jnp.int32, (sub, sub), 1)
    before = (tr > tc).astype(BF16)
    neg = -jnp.inf
    run = run_ref[0:1]
    for r0 in range(0, tm, sub):
        rs = slice(r0, r0 + sub)
        ga = zb_ref[rs, d:2 * d].astype(F32)
        gc = zb_ref[rs, 2 * d:3 * d].astype(F32)
        y_conv = _dot(zb_ref[rs, 0:d], wco_ref[...])
        y_gla = _dot(og_ref[rs, :], wga_ref[...])
        y = jax.nn.sigmoid(ga) * y_gla + jax.nn.sigmoid(gc) * y_conv
        y = _dot(y.astype(BF16), wo_ref[...])
        x1 = x_ref[rs, :] + m[2:3] * y
        x1_ref[rs, :] = x1

        ms = jnp.mean(x1 * x1, axis=-1, keepdims=True)
        h2 = x1 * lax.rsqrt(ms + NORM_EPS) * n2_ref[...]
        h2 = h2 * (1.0 + m[4:5]) + m[3:4]
        pc = h2_ref.shape[2]
        for p in range(h2_ref.shape[0]):
            h2_ref[p, rs, :] = h2[:, p * pc:(p + 1) * pc]

        h_hi = h2.astype(BF16)
        h_lo = (h2 - h_hi.astype(F32)).astype(BF16)
        two = _dot(h_hi, wr_ref[...])
        logits = two[:, 0:LANES] + two[:, LANES:2 * LANES] + _dot(h_lo, wr_ref[:, 0:LANES]) + br_ref[...]
        gl = jnp.where(lane < N_GROUPS, logits, neg)
        gmax = jnp.max(gl, axis=-1, keepdims=True)
        gsum = jnp.sum(jnp.exp(gl - gmax), axis=-1, keepdims=True)
        g_w = 1.0 / gsum
        g_idx = jnp.min(jnp.where(gl == gmax, lane, LANES), axis=-1, keepdims=True)
        lo = N_GROUPS + g_idx * EXPERTS_PER_GROUP
        in_group = jnp.logical_and(lane >= lo, lane < lo + EXPERTS_PER_GROUP)
        el = jnp.where(in_group, logits, neg)
        e1 = jnp.max(el, axis=-1, keepdims=True)
        i1 = jnp.min(jnp.where(el == e1, lane, LANES), axis=-1, keepdims=True)
        el2 = jnp.where(lane == i1, neg, el)
        e2 = jnp.max(el2, axis=-1, keepdims=True)
        i2 = jnp.min(jnp.where(el2 == e2, lane, LANES), axis=-1, keepdims=True)
        ratio = jnp.exp(e2 - e1)
        w1 = g_w / (1.0 + ratio)
        w2 = g_w * ratio / (1.0 + ratio)

        oh1 = lane == i1
        oh2 = lane == i2
        oh1f = oh1.astype(F32)
        oh2f = oh2.astype(F32)
        tot1 = jnp.sum(oh1f, axis=0, keepdims=True)
        tot2 = jnp.sum(oh2f, axis=0, keepdims=True)
        c1 = _dot(before, oh1f.astype(BF16)) + run
        c2 = _dot(before, oh2f.astype(BF16)) + (run + tot1)
        rank1 = jnp.sum(jnp.where(oh1, c1, 0.0), axis=-1, keepdims=True)
        rank2 = jnp.sum(jnp.where(oh2, c2, 0.0), axis=-1, keepdims=True)
        run = run + tot1 + tot2

        rec = jnp.where(lane == R_ID1, (i1 - N_GROUPS).astype(F32), 0.0)
        rec = jnp.where(lane == R_ID2, (i2 - N_GROUPS).astype(F32), rec)
        rec = jnp.where(lane == R_W1, w1, rec)
        rec = jnp.where(lane == R_W2, w2, rec)
        rec = jnp.where(lane == R_RANK1, rank1, rec)
        rec = jnp.where(lane == R_RANK2, rank2, rec)
        route_ref[rs, :] = rec
    run_ref[...] = jnp.broadcast_to(run, run_ref.shape)
    cnt_ref[...] = jnp.broadcast_to(run, cnt_ref.shape)


def _mixer_out(og, zb, x, modl, wga, wco, wo, n2g, wr, br, batch, seq):
    N, D = x.shape
    tm = MIX_TILE
    ns = seq // tm
    tok = lambda b, s: (b * ns + s, 0)
    return pl.pallas_call(
        _mixout_kernel,
        grid=(batch, ns),
        in_specs=[
            pl.BlockSpec((tm, D), tok),
            pl.BlockSpec((tm, zb.shape[1]), tok),
            pl.BlockSpec((tm, D), tok),
            pl.BlockSpec((1, 6, D), lambda b, s: (b, 0, 0)),
            _const_spec((D, D)),
            _const_spec((D, D)),
            _const_spec((D, D)),
            _const_spec((1, D)),
            _const_spec((D, 2 * LANES)),
            _const_spec((1, LANES)),
        ],
        out_specs=[
            pl.BlockSpec((tm, D), tok),
            pl.BlockSpec((D // ROW_PIECE, tm, ROW_PIECE), lambda b, s: (0, b * ns + s, 0)),
            pl.BlockSpec((tm, LANES), tok),
            pl.BlockSpec((8, LANES), lambda b, s: (0, 0)),
        ],
        out_shape=[
            jax.ShapeDtypeStruct((N, D), F32),
            jax.ShapeDtypeStruct((D // ROW_PIECE, N, ROW_PIECE), F32),
            jax.ShapeDtypeStruct((N, LANES), F32),
            jax.ShapeDtypeStruct((8, LANES), F32),
        ],
        scratch_shapes=[pltpu.VMEM((8, LANES), F32)],
        compiler_params=_params(("arbitrary", "arbitrary")),
        name="mixer_out",
    )(og, zb, x, modl, wga, wco, wo, n2g, wr, br)


SC_WINDOW = 128


def _sc_mesh():
    return plsc.VectorSubcoreMesh(core_axis_name="core", subcore_axis_name="subcore")


def _piece_index(rows, pieces, n_rows):
    return (jnp.arange(pieces, dtype=jnp.int32)[:, None] * n_rows + rows[None, :]).reshape(1, -1)


def _sc_dispatch(h2, dest1, dest2, n_rows):
    pieces, N, pc = h2.shape
    w = SC_WINDOW

    @functools.partial(pl.kernel, out_type=jax.ShapeDtypeStruct((pieces * n_rows, pc), h2.dtype),
                       mesh=_sc_mesh(), scratch_types=[], name="moe_dispatch_sc")
    def run(x_hbm, i1_hbm, i2_hbm, o_hbm):
        def body(x_vmem, i1_vmem, i2_vmem):
            pltpu.sync_copy(x_vmem, o_hbm.at[i1_vmem.at[0]])
            pltpu.sync_copy(x_vmem, o_hbm.at[i2_vmem.at[0]])

        pltpu.emit_pipeline(
            body,
            grid=(pieces * N // w,),
            in_specs=[
                pl.BlockSpec((w, pc), lambda i: (i, 0)),
                pl.BlockSpec((1, w), lambda i: (0, i)),
                pl.BlockSpec((1, w), lambda i: (0, i)),
            ],
            out_specs=[],
            core_axis_name=("core", "subcore"),
            dimension_semantics=(pltpu.PARALLEL,),
        )(x_hbm, i1_hbm, i2_hbm)

    xs = run(h2.reshape(pieces * N, pc), _piece_index(dest1, pieces, n_rows),
             _piece_index(dest2, pieces, n_rows))
    return xs.reshape(pieces, n_rows, pc)


def _sc_return(yb, dest):
    M = dest.shape[0]
    pieces, P, pc = yb.shape
    w = SC_WINDOW

    @functools.partial(pl.kernel, out_type=jax.ShapeDtypeStruct((pieces * M, pc), yb.dtype),
                       mesh=_sc_mesh(), scratch_types=[], name="moe_return_sc")
    def run(y_hbm, i_hbm, o_hbm):
        def body(i_vmem, o_vmem):
            pltpu.sync_copy(y_hbm.at[i_vmem.at[0]], o_vmem)

        pltpu.emit_pipeline(
            body,
            grid=(pieces * M // w,),
            in_specs=[pl.BlockSpec((1, w), lambda i: (0, i))],
            out_specs=[pl.BlockSpec((w, pc), lambda i: (i, 0))],
            core_axis_name=("core", "subcore"),
            dimension_semantics=(pltpu.PARALLEL,),
        )(i_hbm, o_hbm)

    return run(yb.reshape(pieces * P, pc), _piece_index(dest, pieces, P)).reshape(pieces, M, pc)


def _expert_kernel(ib_ref, ie_ref, lo_ref, hi_ref, ni_ref, xs_ref, w1_ref, w3_ref, w2_ref, y_ref,
                   w1b, w3b, w2b):
    j = pl.program_id(0)
    jp = jnp.maximum(j - 1, 0)
    live = j < ni_ref[0]
    new_expert = jnp.logical_or(j == 0, ie_ref[j] != ie_ref[jp])
    new_block = jnp.logical_or(j == 0, ib_ref[j] != ib_ref[jp])

    @pl.when(jnp.logical_and(new_expert, live))
    def _():
        w1b[...] = w1_ref[0, 0].astype(BF16)
        w3b[...] = w3_ref[0, 0].astype(BF16)
        w2b[...] = w2_ref[0, 0].astype(BF16)

    @pl.when(live)
    def _():
        pieces, _, pc = xs_ref.shape
        xb = jnp.concatenate([xs_ref[p] for p in range(pieces)], axis=-1).astype(BF16)
        h1 = _dot(xb, w1b[...])
        h3 = _dot(xb, w3b[...])
        a = (h1 * jax.nn.sigmoid(h1) * h3).astype(BF16)
        y = _dot(a, w2b[...])
        row = lax.broadcasted_iota(jnp.int32, (y.shape[0], pc), 0)
        mine = jnp.logical_and(row >= lo_ref[j], row < hi_ref[j])

        @pl.when(new_block)
        def _():
            for p in range(pieces):
                y_ref[p] = jnp.where(mine, y[:, p * pc:(p + 1) * pc], 0.0)

        @pl.when(jnp.logical_not(new_block))
        def _():
            for p in range(pieces):
                y_ref[p] = jnp.where(mine, y[:, p * pc:(p + 1) * pc], y_ref[p])


def _experts(xs, items, w1, w3, w2, layer):
    pieces, P, pc = xs.shape
    _, E, D, DE = w1.shape
    bm = MOE_BLOCK
    n_items_max = P // bm + E
    rows = lambda j, ib, ie, lo, hi, ni: (0, ib[j], 0)
    wsel = lambda j, ib, ie, lo, hi, ni: (layer, ie[j], 0, 0)
    return pl.pallas_call(
        _expert_kernel,
        grid_spec=pltpu.PrefetchScalarGridSpec(
            num_scalar_prefetch=5,
            grid=(n_items_max,),
            in_specs=[
                pl.BlockSpec((pieces, bm, pc), rows),
                pl.BlockSpec((1, 1, D, DE), wsel),
                pl.BlockSpec((1, 1, D, DE), wsel),
                pl.BlockSpec((1, 1, DE, D), wsel),
            ],
            out_specs=pl.BlockSpec((pieces, bm, pc), rows),
            scratch_shapes=[
                pltpu.VMEM((D, DE), BF16),
                pltpu.VMEM((D, DE), BF16),
                pltpu.VMEM((DE, D), BF16),
            ],
        ),
        out_shape=jax.ShapeDtypeStruct((pieces, P, pc), F32),
        compiler_params=_params(("arbitrary",)),
        name="moe_experts",
    )(*items, xs, w1, w3, w2)


def _combine_kernel(x1_ref, route_ref, mod_ref, fg_ref, y0_ref, y1_ref, o_ref):
    x2 = x1_ref[...] + mod_ref[0][5:6] * _moe_mix(route_ref, y0_ref, y1_ref)
    ms = jnp.mean(x2 * x2, axis=-1, keepdims=True)
    o_ref[...] = x2 * lax.rsqrt(ms + NORM_EPS) * fg_ref[...]


def _final_combine(x1, route, modl, final_g, yk, seq):
    N, D = x1.shape
    tm = ROW_TILE
    nt = N // tm
    tok = lambda i: (i, 0)
    return pl.pallas_call(
        _combine_kernel,
        grid=(nt,),
        in_specs=[
            pl.BlockSpec((tm, D), tok),
            pl.BlockSpec((tm, LANES), tok),
            pl.BlockSpec((1, 6, D), lambda i: ((i * tm) // seq, 0, 0)),
            pl.BlockSpec((1, D), lambda i: (0, 0)),
            pl.BlockSpec((D // ROW_PIECE, tm, ROW_PIECE), lambda i: (0, i, 0)),
            pl.BlockSpec((D // ROW_PIECE, tm, ROW_PIECE), lambda i: (0, nt + i, 0)),
        ],
        out_specs=pl.BlockSpec((tm, D), tok),
        out_shape=jax.ShapeDtypeStruct((N, D), F32),
        compiler_params=_params(("arbitrary",)),
        name="moe_combine",
    )(x1, route, modl, final_g, yk, yk)


def _dest_kernel(route_ref, cnt_ref, d1_ref, d2_ref):
    tm = route_ref.shape[0]
    rec = route_ref[...]
    cnt = cnt_ref[...]
    r = lax.broadcasted_iota(jnp.int32, (LANES, LANES), 0)
    c = lax.broadcasted_iota(jnp.int32, (LANES, LANES), 1)
    upper = (r <= c).astype(F32)
    end = jnp.dot(cnt, upper, precision=HIGHEST, preferred_element_type=F32)
    start = (end - cnt)[0:1]
    lane = lax.broadcasted_iota(jnp.int32, (tm, LANES), 1).astype(F32)
    ones = jnp.ones((8, LANES), F32)
    for k, (col_id, col_rank, out) in enumerate(((R_ID1, R_RANK1, d1_ref), (R_ID2, R_RANK2, d2_ref))):
        sel = lane == rec[:, col_id:col_id + 1] + float(N_GROUPS)
        a = jnp.where(sel, start + rec[:, col_rank:col_rank + 1], 0.0)
        row = lax.dot_general(ones, a, (((1,), (1,)), ((), ())), precision=HIGHEST,
                              preferred_element_type=F32)
        out[0] = row[0:1].astype(jnp.int32)


def _destinations(route, cnt):
    N = route.shape[0]
    tm = 1024 if N % 1024 == 0 else ROW_TILE
    out = jax.ShapeDtypeStruct((N // tm, 1, tm), jnp.int32)
    d1, d2 = pl.pallas_call(
        _dest_kernel,
        grid=(N // tm,),
        in_specs=[pl.BlockSpec((tm, LANES), lambda i: (i, 0)), pl.BlockSpec((8, LANES), lambda i: (0, 0))],
        out_specs=[pl.BlockSpec((1, 1, tm), lambda i: (i, 0, 0))] * 2,
        out_shape=[out, out],
        compiler_params=_params(("arbitrary",)),
        name="moe_dest",
    )(route, cnt)
    return d1.reshape(N), d2.reshape(N)


def _work_items(cnt, n_items_max):
    bm = MOE_BLOCK
    i32 = jnp.int32
    counts = cnt[0, N_GROUPS:N_GROUPS + N_EXPERTS].astype(i32)
    end = jnp.cumsum(counts)
    start = end - counts
    first = start // bm
    last = jnp.maximum(end - 1, 0) // bm
    per_e = jnp.where(counts > 0, last - first + 1, 0)
    item_end = jnp.cumsum(per_e)
    item_start = item_end - per_e
    n_items = item_end[-1]
    j = jnp.minimum(jnp.arange(n_items_max, dtype=i32), n_items - 1)
    ie = jnp.minimum(jnp.sum((item_end[None, :] <= j[:, None]).astype(i32), axis=1), N_EXPERTS - 1)
    onehot = ie[:, None] == jnp.arange(N_EXPERTS, dtype=i32)[None, :]
    pick = lambda tbl: jnp.sum(jnp.where(onehot, tbl[None, :], 0), axis=1)
    ib = pick(first) + (j - pick(item_start))
    lo = jnp.maximum(pick(start), ib * bm) - ib * bm
    hi = jnp.minimum(pick(end), (ib + 1) * bm) - ib * bm
    return (ib.astype(i32), ie.astype(i32), lo.astype(i32), hi.astype(i32),
            n_items.reshape(1).astype(i32))


def kernel(x, c, mod_w, mod_b, norm1_g, w_in, gate_w2, gate_b, gla_norm_g, conv_w, w_gla_out,
           w_conv_out, w_out, norm2_g, router_group_w, router_group_b, router_expert_w,
           router_expert_b, expert_w1, expert_w3, expert_w2, final_norm_g):
    B, S, D = x.shape
    L = mod_w.shape[0]
    N = B * S
    dk = gate_w2.shape[2]
    rank = gate_w2.shape[1]
    dv = gla_norm_g.shape[1]
    n_slots = N * 2
    n_items_max = n_slots // MOE_BLOCK + N_EXPERTS
    assert S % GLA_TILE == 0 and S % MIX_TILE == 0 and S % IN_TILE == 0 and N % ROW_TILE == 0
    assert n_slots % MOE_BLOCK == 0
    assert N_GROUPS + N_EXPERTS <= LANES and 3 * rank <= LANES and D % ROW_PIECE == 0

    mod = _modulation(c, mod_w, mod_b)
    xf = x.reshape(N, D)
    o_gd = 2 * dk + dv
    o_r = o_gd + rank
    o_b = o_r + dv
    for l in range(L):
        modl = mod[l].reshape(B, 6, D)
        wl = w_in[l]
        wa = jnp.concatenate([wl[:, :o_gd], wl[:, o_r:o_b]], axis=1).astype(BF16)
        wb = wl[:, o_b:].astype(BF16)
        wgd = wl[:, o_gd:o_r]
        wg = jnp.pad(jnp.concatenate([wgd, wgd, wgd], axis=1), ((0, 0), (0, LANES - 3 * rank))).astype(BF16)
        g_hi = gate_w2[l].astype(BF16)
        g_lo = (gate_w2[l] - g_hi.astype(F32)).astype(BF16)
        gw2 = jnp.pad(jnp.concatenate([g_hi, g_hi, g_lo], axis=0), ((0, LANES - 3 * rank), (0, 0)))
        if l == 0:
            za, zb, zg = _in_projection(xf, modl, norm1_g[l].reshape(1, D), conv_w[l], wa, wb, wg, S)
        else:
            xf, za, zb, zg = _in_projection(x1, modl, norm1_g[l].reshape(1, D), conv_w[l], wa, wb, wg, S,
                                            combine=(route, yk, mod[l - 1].reshape(B, 6, D)))
        og = _gla(za, zg, gw2, gate_b[l].reshape(1, dk), gla_norm_g[l].reshape(1, dv), B, S, dk, dv, rank)
        wr = jnp.pad(jnp.concatenate([router_group_w[l], router_expert_w[l]], axis=1),
                     ((0, 0), (0, LANES - N_GROUPS - N_EXPERTS)))
        wr_hi = wr.astype(BF16)
        wr = jnp.concatenate([wr_hi, (wr - wr_hi.astype(F32)).astype(BF16)], axis=1)
        br = jnp.pad(jnp.concatenate([router_group_b[l], router_expert_b[l]]),
                     (0, LANES - N_GROUPS - N_EXPERTS)).reshape(1, LANES)
        x1, h2, route, cnt = _mixer_out(
            og, zb, xf, modl, w_gla_out[l].astype(BF16), w_conv_out[l].astype(BF16),
            w_out[l].astype(BF16), norm2_g[l].reshape(1, D), wr, br, B, S)
        dest1, dest2 = _destinations(route, cnt)
        items = _work_items(cnt, n_items_max)
        xs = _sc_dispatch(h2, dest1, dest2, n_slots)
        yb = _experts(xs, items, expert_w1, expert_w3, expert_w2, l)
        yk = _sc_return(yb, jnp.concatenate([dest1, dest2]))
    out = _final_combine(x1, route, modl, final_norm_g.reshape(1, D), yk, S)
    return out.reshape(B, S, D)
```

```python
import functools

import jax
import jax.numpy as jnp
from jax import lax
from jax.experimental import pallas as pl
from jax.experimental.pallas import tpu as pltpu
from jax.experimental.pallas import tpu_sc as plsc

F32 = jnp.float32
BF16 = jnp.bfloat16
HIGHEST = lax.Precision.HIGHEST

GLA_HEADS = 4
GATE_TAU = 16.0
GLA_CHUNK = 64
N_GROUPS = 8
EXPERTS_PER_GROUP = 8
N_EXPERTS = N_GROUPS * EXPERTS_PER_GROUP
NORM_EPS = 1e-6

LANES = 128
VMEM_LIMIT_BYTES = 56 * 1024 * 1024

IN_TILE = 512
GLA_TILE = 512
MIX_TILE = 512
MIX_SUB = 256
MOE_BLOCK = 512
ROW_TILE = 256
ROW_PIECE = 256
U32 = jnp.uint32

R_ID1, R_ID2, R_W1, R_W2, R_RANK1, R_RANK2 = 0, 1, 2, 3, 4, 5


def _dot(a, b):
    return jnp.dot(a, b, preferred_element_type=F32)


def _pack_rows(x):
    half = x.shape[1] // 2
    out = []
    for p in range(half // ROW_PIECE):
        lo = x[:, p * ROW_PIECE:(p + 1) * ROW_PIECE].astype(BF16).astype(F32)
        hi = x[:, half + p * ROW_PIECE:half + (p + 1) * ROW_PIECE].astype(BF16).astype(F32)
        out.append((pltpu.bitcast(lo, U32) >> 16) | pltpu.bitcast(hi, U32))
    return out


def _unpack_rows(ref):
    words = [ref[p] for p in range(ref.shape[0])]
    lo = [pltpu.bitcast(w << 16, F32) for w in words]
    hi = [pltpu.bitcast(w & jnp.uint32(0xFFFF0000), F32) for w in words]
    return jnp.concatenate(lo + hi, axis=-1)


def _const_spec(shape):
    nd = len(shape)
    return pl.BlockSpec(shape, lambda *_: (0,) * nd, pipeline_mode=pl.Buffered(1))


def _params(sem):
    return pltpu.CompilerParams(dimension_semantics=sem, vmem_limit_bytes=VMEM_LIMIT_BYTES)


def _mod_kernel(c_ref, w_ref, b_ref, o_ref):
    c = c_ref[...]
    sc = c * jax.nn.sigmoid(c)
    o_ref[0] = jnp.dot(sc, w_ref[0], precision=HIGHEST, preferred_element_type=F32) + b_ref[0]


def _modulation(c, mod_w, mod_b):
    L, D, D6 = mod_w.shape
    B = c.shape[0]
    nj = D6 // D
    return pl.pallas_call(
        _mod_kernel,
        grid=(L, nj),
        in_specs=[
            pl.BlockSpec((B, D), lambda l, j: (0, 0)),
            pl.BlockSpec((1, D, D), lambda l, j: (l, 0, j)),
            pl.BlockSpec((1, 1, D), lambda l, j: (l, 0, j)),
        ],
        out_specs=pl.BlockSpec((1, B, D), lambda l, j: (l, 0, j)),
        out_shape=jax.ShapeDtypeStruct((L, B, D6), F32),
        compiler_params=_params(("arbitrary", "arbitrary")),
        name="adaln_mod",
    )(c, mod_w, mod_b.reshape(L, 1, D6))


CONV_COLS = 256


def _moe_mix(route_ref, y0_ref, y1_ref):
    rec = route_ref[...]
    w1 = rec[:, R_W1:R_W1 + 1]
    w2 = rec[:, R_W2:R_W2 + 1]
    return w1 * _unpack_rows(y0_ref) + w2 * _unpack_rows(y1_ref)


def _inproj_kernel(*refs, seq, fuse_combine):
    if fuse_combine:
        (x_ref, route_ref, y0_ref, y1_ref, modp_ref, mod_ref, g_ref, cw_ref, wa_ref, wb_ref, wg_ref,
         xo_ref, za_ref, zb_ref, zg_ref, carry_ref) = refs
        x = x_ref[...] + modp_ref[0][5:6] * _moe_mix(route_ref, y0_ref, y1_ref)
        xo_ref[...] = x
    else:
        (x_ref, mod_ref, g_ref, cw_ref, wa_ref, wb_ref, wg_ref, za_ref, zb_ref, zg_ref, carry_ref) = refs
        x = x_ref[...]
    tm, d = x.shape

    @pl.when((pl.program_id(0) * tm) % seq == 0)
    def _():
        carry_ref[...] = jnp.zeros_like(carry_ref)

    ms = jnp.mean(x * x, axis=-1, keepdims=True)
    y = x * lax.rsqrt(ms + NORM_EPS) * g_ref[...]
    m = mod_ref[0]
    h = (y * (1.0 + m[1:2]) + m[0:1]).astype(BF16)
    for j in range(za_ref.shape[1] // d):
        za_ref[:, j * d:(j + 1) * d] = _dot(h, wa_ref[:, j * d:(j + 1) * d]).astype(BF16)
    zg_ref[...] = _dot(h, wg_ref[...])
    for j in range(1, zb_ref.shape[1] // d):
        zb_ref[:, j * d:(j + 1) * d] = _dot(h, wb_ref[:, (j + 2) * d:(j + 3) * d]).astype(BF16)

    w = CONV_COLS
    rowi = lax.broadcasted_iota(jnp.int32, (tm, w), 0)
    for j in range(d // w):
        cols = slice(j * w, (j + 1) * w)
        cb = _dot(h, wb_ref[:, j * w:(j + 1) * w])
        cc = _dot(h, wb_ref[:, d + j * w:d + (j + 1) * w])
        ch = _dot(h, wb_ref[:, 2 * d + j * w:2 * d + (j + 1) * w])
        u = cc * ch
        prev = carry_ref[:, cols]
        u1 = jnp.where(rowi == 0, prev[7:8], pltpu.roll(u, 1, 0))
        u2 = jnp.where(rowi == 0, prev[6:7], jnp.where(rowi == 1, prev[7:8], pltpu.roll(u, 2, 0)))
        carry_ref[:, cols] = u[tm - 8:tm]
        conv = cw_ref[0:1, cols] * u2 + cw_ref[1:2, cols] * u1 + cw_ref[2:3, cols] * u
        zb_ref[:, cols] = (cb * conv).astype(BF16)


def _in_projection(x, modl, norm_g, conv_w, wa, wb, wg, seq, combine=None):
    N, D = x.shape
    tm = IN_TILE
    nt = N // tm
    ca, cb = wa.shape[1], wb.shape[1] - 2 * D
    tok = lambda i: (i, 0)
    per_batch = lambda i: ((i * tm) // seq, 0, 0)
    in_specs = [pl.BlockSpec((tm, D), tok)]
    args = [x]
    out_specs, out_shape = [], []
    if combine is not None:
        route, yk, mod_prev = combine
        piece_blk = (yk.shape[0], tm, ROW_PIECE)
        in_specs += [
            pl.BlockSpec((tm, LANES), tok),
            pl.BlockSpec(piece_blk, lambda i: (0, i, 0)),
            pl.BlockSpec(piece_blk, lambda i: (0, nt + i, 0)),
            pl.BlockSpec((1, 6, D), per_batch),
        ]
        args += [route, yk, yk, mod_prev]
        out_specs.append(pl.BlockSpec((tm, D), tok))
        out_shape.append(jax.ShapeDtypeStruct((N, D), F32))
    in_specs += [
        pl.BlockSpec((1, 6, D), per_batch),
        _const_spec((1, D)),
        _const_spec(conv_w.shape),
        _const_spec((D, ca)),
        _const_spec(wb.shape),
        _const_spec((D, LANES)),
    ]
    args += [modl, norm_g, conv_w, wa, wb, wg]
    out_specs += [
        pl.BlockSpec((tm, ca), tok),
        pl.BlockSpec((tm, cb), tok),
        pl.BlockSpec((tm, LANES), tok),
    ]
    out_shape += [
        jax.ShapeDtypeStruct((N, ca), BF16),
        jax.ShapeDtypeStruct((N, cb), BF16),
        jax.ShapeDtypeStruct((N, LANES), F32),
    ]
    kern = functools.partial(_inproj_kernel, seq=seq, fuse_combine=combine is not None)
    return pl.pallas_call(
        kern,
        grid=(nt,),
        in_specs=in_specs,
        out_specs=out_specs,
        out_shape=out_shape,
        scratch_shapes=[pltpu.VMEM((8, D), F32)],
        compiler_params=_params(("arbitrary",)),
        name="in_projection",
    )(*args)


def _log_sigmoid(x):
    return jnp.minimum(x, 0.0) - jnp.log1p(jnp.exp(-jnp.abs(x)))


def _gla_kernel(za_ref, zg_ref, gw2_ref, gb_ref, ng_ref, o_ref, st_ref, lg_ref, *, dk, dv, rank):
    heads = GLA_HEADS
    dkh, dvh = dk // heads, dv // heads
    c = GLA_CHUNK
    ts = za_ref.shape[0]

    @pl.when(pl.program_id(1) == 0)
    def _():
        st_ref[...] = jnp.zeros_like(st_ref)

    zg = zg_ref[...]
    zg_hi = zg.astype(BF16)
    zg_lo = (zg - zg_hi.astype(F32)).astype(BF16)
    lane = lax.broadcasted_iota(jnp.int32, zg.shape, 1)
    lhs = jnp.where(jnp.logical_and(lane >= rank, lane < 2 * rank), zg_lo, zg_hi)
    pre = _dot(lhs, gw2_ref[...]) + gb_ref[...]
    lg = _log_sigmoid(pre) * (1.0 / GATE_TAU)
    lg_hi = lg.astype(BF16)
    lg_ref[:, 0:dk] = lg_hi
    lg_ref[:, dk:2 * dk] = (lg - lg_hi.astype(F32)).astype(BF16)

    row = lax.broadcasted_iota(jnp.int32, (c, c), 0)
    col = lax.broadcasted_iota(jnp.int32, (c, c), 1)
    causal = row >= col
    tril = causal.astype(BF16)
    qscale = dkh ** -0.5

    for ci in range(ts // c):
        rows = slice(ci * c, (ci + 1) * c)
        b_two = _dot(tril, lg_ref[rows, :])
        b_all = b_two[:, 0:dk] + b_two[:, dk:2 * dk]
        for hd in range(heads):
            ks = slice(hd * dkh, (hd + 1) * dkh)
            q = za_ref[rows, hd * dkh:(hd + 1) * dkh].astype(F32) * qscale
            k = za_ref[rows, dk + hd * dkh:dk + (hd + 1) * dkh].astype(F32)
            v = za_ref[rows, 2 * dk + hd * dvh:2 * dk + (hd + 1) * dvh]
            r = za_ref[rows, 2 * dk + dv + hd * dvh:2 * dk + dv + (hd + 1) * dvh].astype(F32)
            b = b_all[:, ks]
            b_last = b[c - 1:c, :]
            q_t = (q * jnp.exp(b)).astype(BF16)
            k_t = (k * jnp.exp(-b)).astype(BF16)
            k_s = (k * jnp.exp(b_last - b)).astype(BF16)
            decay = jnp.exp(b_last)
            attn = lax.dot_general(q_t, k_t, (((1,), (1,)), ((), ())), preferred_element_type=F32)
            attn = jnp.where(causal, attn, 0.0).astype(BF16)
            st = st_ref[hd]
            o = _dot(attn, v) + lax.dot_general(
                q_t, st.astype(BF16), (((1,), (1,)), ((), ())), preferred_element_type=F32)
            upd = lax.dot_general(v, k_s, (((0,), (0,)), ((), ())), preferred_element_type=F32)
            st_ref[hd] = st * decay + upd
            ms = jnp.mean(o * o, axis=-1, keepdims=True)
            on = o * lax.rsqrt(ms + NORM_EPS) * ng_ref[:, hd * dvh:(hd + 1) * dvh]
            o_ref[rows, hd * dvh:(hd + 1) * dvh] = (on * (r * jax.nn.sigmoid(r))).astype(BF16)


def _gla(za, zg, gw2, gb, ng, batch, seq, dk, dv, rank):
    N = za.shape[0]
    ts = GLA_TILE
    ns = seq // ts
    heads = GLA_HEADS
    kern = functools.partial(_gla_kernel, dk=dk, dv=dv, rank=rank)
    return pl.pallas_call(
        kern,
        grid=(batch, ns),
        in_specs=[
            pl.BlockSpec((ts, za.shape[1]), lambda b, s: (b * ns + s, 0)),
            pl.BlockSpec((ts, LANES), lambda b, s: (b * ns + s, 0)),
            _const_spec((LANES, dk)),
            _const_spec((1, dk)),
            _const_spec((1, dv)),
        ],
        out_specs=pl.BlockSpec((ts, dv), lambda b, s: (b * ns + s, 0)),
        out_shape=jax.ShapeDtypeStruct((N, dv), BF16),
        scratch_shapes=[
            pltpu.VMEM((heads, dv // heads, dk // heads), F32),
            pltpu.VMEM((ts, 2 * dk), BF16),
        ],
        compiler_params=_params(("arbitrary", "arbitrary")),
        name="gla",
    )(za, zg, gw2, gb, ng)


def _mixout_kernel(og_ref, zb_ref, x_ref, mod_ref, wga_ref, wco_ref, wo_ref, n2_ref,
                   wr_ref, br_ref, x1_ref, h2_ref, route_ref, cnt_ref, run_ref):
    tm, d = x_ref.shape

    @pl.when(jnp.logical_and(pl.program_id(0) == 0, pl.program_id(1) == 0))
    def _():
        run_ref[...] = jnp.zeros_like(run_ref)

    m = mod_ref[0]
    sub = MIX_SUB
    lane = lax.broadcasted_iota(jnp.int32, (sub, LANES), 1)
    tr = lax.broadcasted_iota(jnp.int32, (sub, sub), 0)
    tc = lax.broadcasted_iota(jnp.int32, (sub, sub), 1)
    before = (tr > tc).astype(BF16)
    neg = -jnp.inf
    run = run_ref[0:1]
    for r0 in range(0, tm, sub):
        rs = slice(r0, r0 + sub)
        ga = zb_ref[rs, d:2 * d].astype(F32)
        gc = zb_ref[rs, 2 * d:3 * d].astype(F32)
        y_conv = _dot(zb_ref[rs, 0:d], wco_ref[...])
        y_gla = _dot(og_ref[rs, :], wga_ref[...])
        y = jax.nn.sigmoid(ga) * y_gla + jax.nn.sigmoid(gc) * y_conv
        y = _dot(y.astype(BF16), wo_ref[...])
        x1 = x_ref[rs, :] + m[2:3] * y
        x1_ref[rs, :] = x1

        ms = jnp.mean(x1 * x1, axis=-1, keepdims=True)
        h2 = x1 * lax.rsqrt(ms + NORM_EPS) * n2_ref[...]
        h2 = h2 * (1.0 + m[4:5]) + m[3:4]
        for p, words in enumerate(_pack_rows(h2)):
            h2_ref[p, rs, :] = words

        h_hi = h2.astype(BF16)
        h_lo = (h2 - h_hi.astype(F32)).astype(BF16)
        two = _dot(h_hi, wr_ref[...])
        logits = two[:, 0:LANES] + two[:, LANES:2 * LANES] + _dot(h_lo, wr_ref[:, 0:LANES]) + br_ref[...]
        gl = jnp.where(lane < N_GROUPS, logits, neg)
        gmax = jnp.max(gl, axis=-1, keepdims=True)
        gsum = jnp.sum(jnp.exp(gl - gmax), axis=-1, keepdims=True)
        g_w = 1.0 / gsum
        g_idx = jnp.min(jnp.where(gl == gmax, lane, LANES), axis=-1, keepdims=True)
        lo = N_GROUPS + g_idx * EXPERTS_PER_GROUP
        in_group = jnp.logical_and(lane >= lo, lane < lo + EXPERTS_PER_GROUP)
        el = jnp.where(in_group, logits, neg)
        e1 = jnp.max(el, axis=-1, keepdims=True)
        i1 = jnp.min(jnp.where(el == e1, lane, LANES), axis=-1, keepdims=True)
        el2 = jnp.where(lane == i1, neg, el)
        e2 = jnp.max(el2, axis=-1, keepdims=True)
        i2 = jnp.min(jnp.where(el2 == e2, lane, LANES), axis=-1, keepdims=True)
        ratio = jnp.exp(e2 - e1)
        w1 = g_w / (1.0 + ratio)
        w2 = g_w * ratio / (1.0 + ratio)

        oh1 = lane == i1
        oh2 = lane == i2
        oh1f = oh1.astype(F32)
        oh2f = oh2.astype(F32)
        tot1 = jnp.sum(oh1f, axis=0, keepdims=True)
        tot2 = jnp.sum(oh2f, axis=0, keepdims=True)
        c1 = _dot(before, oh1f.astype(BF16)) + run
        c2 = _dot(before, oh2f.astype(BF16)) + (run + tot1)
        rank1 = jnp.sum(jnp.where(oh1, c1, 0.0), axis=-1, keepdims=True)
        rank2 = jnp.sum(jnp.where(oh2, c2, 0.0), axis=-1, keepdims=True)
        run = run + tot1 + tot2

        rec = jnp.where(lane == R_ID1, (i1 - N_GROUPS).astype(F32), 0.0)
        rec = jnp.where(lane == R_ID2, (i2 - N_GROUPS).astype(F32), rec)
        rec = jnp.where(lane == R_W1, w1, rec)
        rec = jnp.where(lane == R_W2, w2, rec)
        rec = jnp.where(lane == R_RANK1, rank1, rec)
        rec = jnp.where(lane == R_RANK2, rank2, rec)
        route_ref[rs, :] = rec
    run_ref[...] = jnp.broadcast_to(run, run_ref.shape)
    cnt_ref[...] = jnp.broadcast_to(run, cnt_ref.shape)


def _mixer_out(og, zb, x, modl, wga, wco, wo, n2g, wr, br, batch, seq):
    N, D = x.shape
    tm = MIX_TILE
    ns = seq // tm
    tok = lambda b, s: (b * ns + s, 0)
    return pl.pallas_call(
        _mixout_kernel,
        grid=(batch, ns),
        in_specs=[
            pl.BlockSpec((tm, D), tok),
            pl.BlockSpec((tm, zb.shape[1]), tok),
            pl.BlockSpec((tm, D), tok),
            pl.BlockSpec((1, 6, D), lambda b, s: (b, 0, 0)),
            _const_spec((D, D)),
            _const_spec((D, D)),
            _const_spec((D, D)),
            _const_spec((1, D)),
            _const_spec((D, 2 * LANES)),
            _const_spec((1, LANES)),
        ],
        out_specs=[
            pl.BlockSpec((tm, D), tok),
            pl.BlockSpec((D // 2 // ROW_PIECE, tm, ROW_PIECE), lambda b, s: (0, b * ns + s, 0)),
            pl.BlockSpec((tm, LANES), tok),
            pl.BlockSpec((8, LANES), lambda b, s: (0, 0)),
        ],
        out_shape=[
            jax.ShapeDtypeStruct((N, D), F32),
            jax.ShapeDtypeStruct((D // 2 // ROW_PIECE, N, ROW_PIECE), U32),
            jax.ShapeDtypeStruct((N, LANES), F32),
            jax.ShapeDtypeStruct((8, LANES), F32),
        ],
        scratch_shapes=[pltpu.VMEM((8, LANES), F32)],
        compiler_params=_params(("arbitrary", "arbitrary")),
        name="mixer_out",
    )(og, zb, x, modl, wga, wco, wo, n2g, wr, br)


SC_WINDOW = 128


def _sc_mesh():
    return plsc.VectorSubcoreMesh(core_axis_name="core", subcore_axis_name="subcore")


def _piece_index(rows, pieces, n_rows):
    return (jnp.arange(pieces, dtype=jnp.int32)[:, None] * n_rows + rows[None, :]).reshape(1, -1)


def _sc_dispatch(h2, dest1, dest2, n_rows):
    pieces, N, pc = h2.shape
    w = SC_WINDOW

    @functools.partial(pl.kernel, out_type=jax.ShapeDtypeStruct((pieces * n_rows, pc), h2.dtype),
                       mesh=_sc_mesh(), scratch_types=[], name="moe_dispatch_sc")
    def run(x_hbm, i1_hbm, i2_hbm, o_hbm):
        def body(x_vmem, i1_vmem, i2_vmem):
            pltpu.sync_copy(x_vmem, o_hbm.at[i1_vmem.at[0]])
            pltpu.sync_copy(x_vmem, o_hbm.at[i2_vmem.at[0]])

        pltpu.emit_pipeline(
            body,
            grid=(pieces * N // w,),
            in_specs=[
                pl.BlockSpec((w, pc), lambda i: (i, 0)),
                pl.BlockSpec((1, w), lambda i: (0, i)),
                pl.BlockSpec((1, w), lambda i: (0, i)),
            ],
            out_specs=[],
            core_axis_name=("core", "subcore"),
            dimension_semantics=(pltpu.PARALLEL,),
        )(x_hbm, i1_hbm, i2_hbm)

    xs = run(h2.reshape(pieces * N, pc), _piece_index(dest1, pieces, n_rows),
             _piece_index(dest2, pieces, n_rows))
    return xs.reshape(pieces, n_rows, pc)


def _sc_return(yb, dest):
    M = dest.shape[0]
    pieces, P, pc = yb.shape
    w = SC_WINDOW

    @functools.partial(pl.kernel, out_type=jax.ShapeDtypeStruct((pieces * M, pc), yb.dtype),
                       mesh=_sc_mesh(), scratch_types=[], name="moe_return_sc")
    def run(y_hbm, i_hbm, o_hbm):
        def body(i_vmem, o_vmem):
            pltpu.sync_copy(y_hbm.at[i_vmem.at[0]], o_vmem)

        pltpu.emit_pipeline(
            body,
            grid=(pieces * M // w,),
            in_specs=[pl.BlockSpec((1, w), lambda i: (0, i))],
            out_specs=[pl.BlockSpec((w, pc), lambda i: (i, 0))],
            core_axis_name=("core", "subcore"),
            dimension_semantics=(pltpu.PARALLEL,),
        )(i_hbm, o_hbm)

    return run(yb.reshape(pieces * P, pc), _piece_index(dest, pieces, P)).reshape(pieces, M, pc)


def _expert_kernel(ib_ref, ie_ref, lo_ref, hi_ref, ni_ref, xs_ref, w1_ref, w3_ref, w2_ref, y_ref,
                   w1b, w3b, w2b):
    j = pl.program_id(0)
    jp = jnp.maximum(j - 1, 0)
    live = j < ni_ref[0]
    new_expert = jnp.logical_or(j == 0, ie_ref[j] != ie_ref[jp])
    new_block = jnp.logical_or(j == 0, ib_ref[j] != ib_ref[jp])

    @pl.when(jnp.logical_and(new_expert, live))
    def _():
        w1b[...] = w1_ref[0, 0].astype(BF16)
        w3b[...] = w3_ref[0, 0].astype(BF16)
        w2b[...] = w2_ref[0, 0].astype(BF16)

    @pl.when(live)
    def _():
        xb = _unpack_rows(xs_ref).astype(BF16)
        h1 = _dot(xb, w1b[...])
        h3 = _dot(xb, w3b[...])
        a = (h1 * jax.nn.sigmoid(h1) * h3).astype(BF16)
        y = _pack_rows(_dot(a, w2b[...]))
        row = lax.broadcasted_iota(jnp.int32, y[0].shape, 0)
        mine = jnp.logical_and(row >= lo_ref[j], row < hi_ref[j])

        @pl.when(new_block)
        def _():
            for p, words in enumerate(y):
                y_ref[p] = jnp.where(mine, words, jnp.uint32(0))

        @pl.when(jnp.logical_not(new_block))
        def _():
            for p, words in enumerate(y):
                y_ref[p] = jnp.where(mine, words, y_ref[p])


def _experts(xs, items, w1, w3, w2, layer):
    pieces, P, pc = xs.shape
    _, E, D, DE = w1.shape
    bm = MOE_BLOCK
    n_items_max = P // bm + E
    rows = lambda j, ib, ie, lo, hi, ni: (0, ib[j], 0)
    wsel = lambda j, ib, ie, lo, hi, ni: (layer, ie[j], 0, 0)
    return pl.pallas_call(
        _expert_kernel,
        grid_spec=pltpu.PrefetchScalarGridSpec(
            num_scalar_prefetch=5,
            grid=(n_items_max,),
            in_specs=[
                pl.BlockSpec((pieces, bm, pc), rows),
                pl.BlockSpec((1, 1, D, DE), wsel),
                pl.BlockSpec((1, 1, D, DE), wsel),
                pl.BlockSpec((1, 1, DE, D), wsel),
            ],
            out_specs=pl.BlockSpec((pieces, bm, pc), rows),
            scratch_shapes=[
                pltpu.VMEM((D, DE), BF16),
                pltpu.VMEM((D, DE), BF16),
                pltpu.VMEM((DE, D), BF16),
            ],
        ),
        out_shape=jax.ShapeDtypeStruct((pieces, P, pc), U32),
        compiler_params=_params(("arbitrary",)),
        name="moe_experts",
    )(*items, xs, w1, w3, w2)


def _combine_kernel(x1_ref, route_ref, mod_ref, fg_ref, y0_ref, y1_ref, o_ref):
    x2 = x1_ref[...] + mod_ref[0][5:6] * _moe_mix(route_ref, y0_ref, y1_ref)
    ms = jnp.mean(x2 * x2, axis=-1, keepdims=True)
    o_ref[...] = x2 * lax.rsqrt(ms + NORM_EPS) * fg_ref[...]


def _final_combine(x1, route, modl, final_g, yk, seq):
    N, D = x1.shape
    tm = ROW_TILE
    nt = N // tm
    tok = lambda i: (i, 0)
    return pl.pallas_call(
        _combine_kernel,
        grid=(nt,),
        in_specs=[
            pl.BlockSpec((tm, D), tok),
            pl.BlockSpec((tm, LANES), tok),
            pl.BlockSpec((1, 6, D), lambda i: ((i * tm) // seq, 0, 0)),
            pl.BlockSpec((1, D), lambda i: (0, 0)),
            pl.BlockSpec((yk.shape[0], tm, ROW_PIECE), lambda i: (0, i, 0)),
            pl.BlockSpec((yk.shape[0], tm, ROW_PIECE), lambda i: (0, nt + i, 0)),
        ],
        out_specs=pl.BlockSpec((tm, D), tok),
        out_shape=jax.ShapeDtypeStruct((N, D), F32),
        compiler_params=_params(("arbitrary",)),
        name="moe_combine",
    )(x1, route, modl, final_g, yk, yk)


def _dest_kernel(route_ref, cnt_ref, d1_ref, d2_ref):
    tm = route_ref.shape[0]
    rec = route_ref[...]
    cnt = cnt_ref[...]
    r = lax.broadcasted_iota(jnp.int32, (LANES, LANES), 0)
    c = lax.broadcasted_iota(jnp.int32, (LANES, LANES), 1)
    upper = (r <= c).astype(F32)
    end = jnp.dot(cnt, upper, precision=HIGHEST, preferred_element_type=F32)
    start = (end - cnt)[0:1]
    lane = lax.broadcasted_iota(jnp.int32, (tm, LANES), 1).astype(F32)
    ones = jnp.ones((8, LANES), F32)
    for k, (col_id, col_rank, out) in enumerate(((R_ID1, R_RANK1, d1_ref), (R_ID2, R_RANK2, d2_ref))):
        sel = lane == rec[:, col_id:col_id + 1] + float(N_GROUPS)
        a = jnp.where(sel, start + rec[:, col_rank:col_rank + 1], 0.0)
        row = lax.dot_general(ones, a, (((1,), (1,)), ((), ())), precision=HIGHEST,
                              preferred_element_type=F32)
        out[0] = row[0:1].astype(jnp.int32)


def _destinations(route, cnt):
    N = route.shape[0]
    tm = 1024 if N % 1024 == 0 else ROW_TILE
    out = jax.ShapeDtypeStruct((N // tm, 1, tm), jnp.int32)
    d1, d2 = pl.pallas_call(
        _dest_kernel,
        grid=(N // tm,),
        in_specs=[pl.BlockSpec((tm, LANES), lambda i: (i, 0)), pl.BlockSpec((8, LANES), lambda i: (0, 0))],
        out_specs=[pl.BlockSpec((1, 1, tm), lambda i: (i, 0, 0))] * 2,
        out_shape=[out, out],
        compiler_params=_params(("arbitrary",)),
        name="moe_dest",
    )(route, cnt)
    return d1.reshape(N), d2.reshape(N)


def _work_items(cnt, n_items_max):
    bm = MOE_BLOCK
    i32 = jnp.int32
    counts = cnt[0, N_GROUPS:N_GROUPS + N_EXPERTS].astype(i32)
    end = jnp.cumsum(counts)
    start = end - counts
    first = start // bm
    last = jnp.maximum(end - 1, 0) // bm
    per_e = jnp.where(counts > 0, last - first + 1, 0)
    item_end = jnp.cumsum(per_e)
    item_start = item_end - per_e
    n_items = item_end[-1]
    j = jnp.minimum(jnp.arange(n_items_max, dtype=i32), n_items - 1)
    ie = jnp.minimum(jnp.sum((item_end[None, :] <= j[:, None]).astype(i32), axis=1), N_EXPERTS - 1)
    onehot = ie[:, None] == jnp.arange(N_EXPERTS, dtype=i32)[None, :]
    pick = lambda tbl: jnp.sum(jnp.where(onehot, tbl[None, :], 0), axis=1)
    ib = pick(first) + (j - pick(item_start))
    lo = jnp.maximum(pick(start), ib * bm) - ib * bm
    hi = jnp.minimum(pick(end), (ib + 1) * bm) - ib * bm
    return (ib.astype(i32), ie.astype(i32), lo.astype(i32), hi.astype(i32),
            n_items.reshape(1).astype(i32))


def kernel(x, c, mod_w, mod_b, norm1_g, w_in, gate_w2, gate_b, gla_norm_g, conv_w, w_gla_out,
           w_conv_out, w_out, norm2_g, router_group_w, router_group_b, router_expert_w,
           router_expert_b, expert_w1, expert_w3, expert_w2, final_norm_g):
    B, S, D = x.shape
    L = mod_w.shape[0]
    N = B * S
    dk = gate_w2.shape[2]
    rank = gate_w2.shape[1]
    dv = gla_norm_g.shape[1]
    n_slots = N * 2
    n_items_max = n_slots // MOE_BLOCK + N_EXPERTS
    assert S % GLA_TILE == 0 and S % MIX_TILE == 0 and S % IN_TILE == 0 and N % ROW_TILE == 0
    assert n_slots % MOE_BLOCK == 0
    assert N_GROUPS + N_EXPERTS <= LANES and 3 * rank <= LANES and D % (2 * ROW_PIECE) == 0

    mod = _modulation(c, mod_w, mod_b)
    xf = x.reshape(N, D)
    o_gd = 2 * dk + dv
    o_r = o_gd + rank
    o_b = o_r + dv
    for l in range(L):
        modl = mod[l].reshape(B, 6, D)
        wl = w_in[l]
        wa = jnp.concatenate([wl[:, :o_gd], wl[:, o_r:o_b]], axis=1).astype(BF16)
        wb = wl[:, o_b:].astype(BF16)
        wgd = wl[:, o_gd:o_r]
        wg = jnp.pad(jnp.concatenate([wgd, wgd, wgd], axis=1), ((0, 0), (0, LANES - 3 * rank))).astype(BF16)
        g_hi = gate_w2[l].astype(BF16)
        g_lo = (gate_w2[l] - g_hi.astype(F32)).astype(BF16)
        gw2 = jnp.pad(jnp.concatenate([g_hi, g_hi, g_lo], axis=0), ((0, LANES - 3 * rank), (0, 0)))
        if l == 0:
            za, zb, zg = _in_projection(xf, modl, norm1_g[l].reshape(1, D), conv_w[l], wa, wb, wg, S)
        else:
            xf, za, zb, zg = _in_projection(x1, modl, norm1_g[l].reshape(1, D), conv_w[l], wa, wb, wg, S,
                                            combine=(route, yk, mod[l - 1].reshape(B, 6, D)))
        og = _gla(za, zg, gw2, gate_b[l].reshape(1, dk), gla_norm_g[l].reshape(1, dv), B, S, dk, dv, rank)
        wr = jnp.pad(jnp.concatenate([router_group_w[l], router_expert_w[l]], axis=1),
                     ((0, 0), (0, LANES - N_GROUPS - N_EXPERTS)))
        wr_hi = wr.astype(BF16)
        wr = jnp.concatenate([wr_hi, (wr - wr_hi.astype(F32)).astype(BF16)], axis=1)
        br = jnp.pad(jnp.concatenate([router_group_b[l], router_expert_b[l]]),
                     (0, LANES - N_GROUPS - N_EXPERTS)).reshape(1, LANES)
        x1, h2, route, cnt = _mixer_out(
            og, zb, xf, modl, w_gla_out[l].astype(BF16), w_conv_out[l].astype(BF16),
            w_out[l].astype(BF16), norm2_g[l].reshape(1, D), wr, br, B, S)
        dest1, dest2 = _destinations(route, cnt)
        items = _work_items(cnt, n_items_max)
        xs = _sc_dispatch(h2, dest1, dest2, n_slots)
        yb = _experts(xs, items, expert_w1, expert_w3, expert_w2, l)
        yk = _sc_return(yb, jnp.concatenate([dest1, dest2]))
    out = _final_combine(x1, route, modl, final_norm_g.reshape(1, D), yk, S)
    return out.reshape(B, S, D)
```

```python
import functools

import jax
import jax.numpy as jnp
from jax import lax
from jax.experimental import pallas as pl
from jax.experimental.pallas import tpu as pltpu
from jax.experimental.pallas import tpu_sc as plsc

F32 = jnp.float32
BF16 = jnp.bfloat16
HIGHEST = lax.Precision.HIGHEST

GLA_HEADS = 4
GATE_TAU = 16.0
GLA_CHUNK = 64
N_GROUPS = 8
EXPERTS_PER_GROUP = 8
N_EXPERTS = N_GROUPS * EXPERTS_PER_GROUP
NORM_EPS = 1e-6

LANES = 128
VMEM_LIMIT_BYTES = 56 * 1024 * 1024

IN_TILE = 512
GLA_TILE = 512
MIX_TILE = 512
MIX_SUB = 256
MOE_BLOCK = 512
ROW_TILE = 1024
ROW_PIECE = 256
U32 = jnp.uint32

R_ID1, R_ID2, R_W1, R_W2, R_RANK1, R_RANK2 = 0, 1, 2, 3, 4, 5


def _dot(a, b):
    return jnp.dot(a, b, preferred_element_type=F32)


def _round_bf16(x):
    xb = x.astype(BF16)
    return xb, xb.astype(F32)


def _pack_rows(xr):
    half = xr.shape[1] // 2
    out = []
    for p in range(half // ROW_PIECE):
        lo = xr[:, p * ROW_PIECE:(p + 1) * ROW_PIECE]
        hi = xr[:, half + p * ROW_PIECE:half + (p + 1) * ROW_PIECE]
        out.append((pltpu.bitcast(lo, U32) >> 16) | pltpu.bitcast(hi, U32))
    return out


def _unpack_rows(ref):
    words = [ref[p] for p in range(ref.shape[0])]
    lo = [pltpu.bitcast(w << 16, F32) for w in words]
    hi = [pltpu.bitcast(w & jnp.uint32(0xFFFF0000), F32) for w in words]
    return jnp.concatenate(lo + hi, axis=-1)


def _const_spec(shape):
    nd = len(shape)
    return pl.BlockSpec(shape, lambda *_: (0,) * nd, pipeline_mode=pl.Buffered(1))


def _params(sem):
    return pltpu.CompilerParams(dimension_semantics=sem, vmem_limit_bytes=VMEM_LIMIT_BYTES)


def _mod_kernel(c_ref, w_ref, b_ref, o_ref):
    c = c_ref[...]
    sc = c * jax.nn.sigmoid(c)
    o_ref[0] = jnp.dot(sc, w_ref[0], precision=HIGHEST, preferred_element_type=F32) + b_ref[0]


def _modulation(c, mod_w, mod_b):
    L, D, D6 = mod_w.shape
    B = c.shape[0]
    nj = D6 // D
    return pl.pallas_call(
        _mod_kernel,
        grid=(L, nj),
        in_specs=[
            pl.BlockSpec((B, D), lambda l, j: (0, 0)),
            pl.BlockSpec((1, D, D), lambda l, j: (l, 0, j)),
            pl.BlockSpec((1, 1, D), lambda l, j: (l, 0, j)),
        ],
        out_specs=pl.BlockSpec((1, B, D), lambda l, j: (l, 0, j)),
        out_shape=jax.ShapeDtypeStruct((L, B, D6), F32),
        compiler_params=_params(("arbitrary", "arbitrary")),
        name="adaln_mod",
    )(c, mod_w, mod_b.reshape(L, 1, D6))


CONV_COLS = 256


def _moe_mix(route_ref, y0_ref, y1_ref):
    rec = route_ref[...]
    w1 = rec[:, R_W1:R_W1 + 1]
    w2 = rec[:, R_W2:R_W2 + 1]
    return w1 * _unpack_rows(y0_ref) + w2 * _unpack_rows(y1_ref)


def _inproj_kernel(*refs, seq, fuse_combine):
    if fuse_combine:
        (x_ref, route_ref, y0_ref, y1_ref, modp_ref, mod_ref, g_ref, cw_ref, wa_ref, wb_ref, wg_ref,
         xo_ref, za_ref, zb_ref, zg_ref, carry_ref) = refs
        x = x_ref[...] + modp_ref[0][5:6] * _moe_mix(route_ref, y0_ref, y1_ref)
        xo_ref[...] = x
    else:
        (x_ref, mod_ref, g_ref, cw_ref, wa_ref, wb_ref, wg_ref, za_ref, zb_ref, zg_ref, carry_ref) = refs
        x = x_ref[...]
    tm, d = x.shape

    @pl.when((pl.program_id(0) * tm) % seq == 0)
    def _():
        carry_ref[...] = jnp.zeros_like(carry_ref)

    ms = jnp.mean(x * x, axis=-1, keepdims=True)
    m = mod_ref[0]
    h = (x * lax.rsqrt(ms + NORM_EPS) * (g_ref[...] * (1.0 + m[1:2])) + m[0:1]).astype(BF16)
    for j in range(za_ref.shape[1] // d):
        za_ref[:, j * d:(j + 1) * d] = _dot(h, wa_ref[:, j * d:(j + 1) * d]).astype(BF16)
    zg_ref[...] = _dot(h, wg_ref[...])
    for j in range(1, zb_ref.shape[1] // d):
        zb_ref[:, j * d:(j + 1) * d] = _dot(h, wb_ref[:, (j + 2) * d:(j + 3) * d]).astype(BF16)

    w = CONV_COLS
    rowi = lax.broadcasted_iota(jnp.int32, (tm, w), 0)
    for j in range(d // w):
        cols = slice(j * w, (j + 1) * w)
        cb = _dot(h, wb_ref[:, j * w:(j + 1) * w])
        cc = _dot(h, wb_ref[:, d + j * w:d + (j + 1) * w])
        ch = _dot(h, wb_ref[:, 2 * d + j * w:2 * d + (j + 1) * w])
        u = cc * ch
        prev = carry_ref[:, cols]
        u1 = jnp.where(rowi == 0, prev[7:8], pltpu.roll(u, 1, 0))
        u2 = jnp.where(rowi == 0, prev[6:7], jnp.where(rowi == 1, prev[7:8], pltpu.roll(u, 2, 0)))
        carry_ref[:, cols] = u[tm - 8:tm]
        conv = cw_ref[0:1, cols] * u2 + cw_ref[1:2, cols] * u1 + cw_ref[2:3, cols] * u
        zb_ref[:, cols] = (cb * conv).astype(BF16)


def _in_projection(x, modl, norm_g, conv_w, wa, wb, wg, seq, combine=None):
    N, D = x.shape
    tm = IN_TILE
    nt = N // tm
    ca, cb = wa.shape[1], wb.shape[1] - 2 * D
    tok = lambda i: (i, 0)
    per_batch = lambda i: ((i * tm) // seq, 0, 0)
    in_specs = [pl.BlockSpec((tm, D), tok)]
    args = [x]
    out_specs, out_shape = [], []
    if combine is not None:
        route, yk, mod_prev = combine
        piece_blk = (yk.shape[0], tm, ROW_PIECE)
        in_specs += [
            pl.BlockSpec((tm, LANES), tok),
            pl.BlockSpec(piece_blk, lambda i: (0, i, 0)),
            pl.BlockSpec(piece_blk, lambda i: (0, nt + i, 0)),
            pl.BlockSpec((1, 6, D), per_batch),
        ]
        args += [route, yk, yk, mod_prev]
        out_specs.append(pl.BlockSpec((tm, D), tok))
        out_shape.append(jax.ShapeDtypeStruct((N, D), F32))
    in_specs += [
        pl.BlockSpec((1, 6, D), per_batch),
        _const_spec((1, D)),
        _const_spec(conv_w.shape),
        _const_spec((D, ca)),
        _const_spec(wb.shape),
        _const_spec((D, LANES)),
    ]
    args += [modl, norm_g, conv_w, wa, wb, wg]
    out_specs += [
        pl.BlockSpec((tm, ca), tok),
        pl.BlockSpec((tm, cb), tok),
        pl.BlockSpec((tm, LANES), tok),
    ]
    out_shape += [
        jax.ShapeDtypeStruct((N, ca), BF16),
        jax.ShapeDtypeStruct((N, cb), BF16),
        jax.ShapeDtypeStruct((N, LANES), F32),
    ]
    kern = functools.partial(_inproj_kernel, seq=seq, fuse_combine=combine is not None)
    return pl.pallas_call(
        kern,
        grid=(nt,),
        in_specs=in_specs,
        out_specs=out_specs,
        out_shape=out_shape,
        scratch_shapes=[pltpu.VMEM((8, D), F32)],
        compiler_params=_params(("arbitrary",)),
        name="in_projection",
    )(*args)


def _log_sigmoid(x):
    return jnp.minimum(x, 0.0) - jnp.log(1.0 + jnp.exp(-jnp.abs(x)))


def _gla_kernel(za_ref, zg_ref, gw2_ref, gb_ref, ng_ref, o_ref, st_ref, lg_ref, *, dk, dv, rank):
    heads = GLA_HEADS
    dkh, dvh = dk // heads, dv // heads
    c = GLA_CHUNK
    ts = za_ref.shape[0]

    @pl.when(pl.program_id(1) == 0)
    def _():
        st_ref[...] = jnp.zeros_like(st_ref)

    zg = zg_ref[...]
    zg_hi = zg.astype(BF16)
    zg_lo = (zg - zg_hi.astype(F32)).astype(BF16)
    lane = lax.broadcasted_iota(jnp.int32, zg.shape, 1)
    lhs = jnp.where(jnp.logical_and(lane >= rank, lane < 2 * rank), zg_lo, zg_hi)
    pre = _dot(lhs, gw2_ref[...]) + gb_ref[...]
    lg = _log_sigmoid(pre) * (1.0 / GATE_TAU)
    lg_hi = lg.astype(BF16)
    lg_ref[:, 0:dk] = lg_hi
    lg_ref[:, dk:2 * dk] = (lg - lg_hi.astype(F32)).astype(BF16)

    row = lax.broadcasted_iota(jnp.int32, (c, c), 0)
    col = lax.broadcasted_iota(jnp.int32, (c, c), 1)
    causal = row >= col
    tril = causal.astype(BF16)
    qscale = dkh ** -0.5

    for ci in range(ts // c):
        rows = slice(ci * c, (ci + 1) * c)
        b_two = _dot(tril, lg_ref[rows, :])
        b_all = b_two[:, 0:dk] + b_two[:, dk:2 * dk]
        for hd in range(heads):
            ks = slice(hd * dkh, (hd + 1) * dkh)
            q = za_ref[rows, hd * dkh:(hd + 1) * dkh].astype(F32) * qscale
            k = za_ref[rows, dk + hd * dkh:dk + (hd + 1) * dkh].astype(F32)
            v = za_ref[rows, 2 * dk + hd * dvh:2 * dk + (hd + 1) * dvh]
            r = za_ref[rows, 2 * dk + dv + hd * dvh:2 * dk + dv + (hd + 1) * dvh].astype(F32)
            b = b_all[:, ks]
            b_last = b[c - 1:c, :]
            q_t = (q * jnp.exp(b)).astype(BF16)
            k_t = (k * jnp.exp(-b)).astype(BF16)
            k_s = (k * jnp.exp(b_last - b)).astype(BF16)
            decay = jnp.exp(b_last)
            attn = lax.dot_general(q_t, k_t, (((1,), (1,)), ((), ())), preferred_element_type=F32)
            attn = jnp.where(causal, attn, 0.0).astype(BF16)
            st = st_ref[hd]
            o = _dot(attn, v) + lax.dot_general(
                q_t, st.astype(BF16), (((1,), (1,)), ((), ())), preferred_element_type=F32)
            upd = lax.dot_general(v, k_s, (((0,), (0,)), ((), ())), preferred_element_type=F32)
            st_ref[hd] = st * decay + upd
            ms = jnp.mean(o * o, axis=-1, keepdims=True)
            on = o * lax.rsqrt(ms + NORM_EPS) * ng_ref[:, hd * dvh:(hd + 1) * dvh]
            o_ref[rows, hd * dvh:(hd + 1) * dvh] = (on * (r * jax.nn.sigmoid(r))).astype(BF16)


def _gla(za, zg, gw2, gb, ng, batch, seq, dk, dv, rank):
    N = za.shape[0]
    ts = GLA_TILE
    ns = seq // ts
    heads = GLA_HEADS
    kern = functools.partial(_gla_kernel, dk=dk, dv=dv, rank=rank)
    return pl.pallas_call(
        kern,
        grid=(batch, ns),
        in_specs=[
            pl.BlockSpec((ts, za.shape[1]), lambda b, s: (b * ns + s, 0)),
            pl.BlockSpec((ts, LANES), lambda b, s: (b * ns + s, 0)),
            _const_spec((LANES, dk)),
            _const_spec((1, dk)),
            _const_spec((1, dv)),
        ],
        out_specs=pl.BlockSpec((ts, dv), lambda b, s: (b * ns + s, 0)),
        out_shape=jax.ShapeDtypeStruct((N, dv), BF16),
        scratch_shapes=[
            pltpu.VMEM((heads, dv // heads, dk // heads), F32),
            pltpu.VMEM((ts, 2 * dk), BF16),
        ],
        compiler_params=_params(("arbitrary", "arbitrary")),
        name="gla",
    )(za, zg, gw2, gb, ng)


def _mixout_kernel(og_ref, zb_ref, x_ref, mod_ref, wga_ref, wco_ref, wo_ref, n2_ref,
                   wr_ref, br_ref, x1_ref, h2_ref, logit_ref):
    tm, d = x_ref.shape
    m = mod_ref[0]
    sub = MIX_SUB
    gain2 = n2_ref[...] * (1.0 + m[4:5])
    for r0 in range(0, tm, sub):
        rs = slice(r0, r0 + sub)
        ga = zb_ref[rs, d:2 * d].astype(F32)
        gc = zb_ref[rs, 2 * d:3 * d].astype(F32)
        y_conv = _dot(zb_ref[rs, 0:d], wco_ref[...])
        y_gla = _dot(og_ref[rs, :], wga_ref[...])
        y = jax.nn.sigmoid(ga) * y_gla + jax.nn.sigmoid(gc) * y_conv
        y = _dot(y.astype(BF16), wo_ref[...])
        x1 = x_ref[rs, :] + m[2:3] * y
        x1_ref[rs, :] = x1

        ms = jnp.mean(x1 * x1, axis=-1, keepdims=True)
        h2 = x1 * lax.rsqrt(ms + NORM_EPS) * gain2 + m[3:4]
        h_hi, h_r = _round_bf16(h2)
        for p, words in enumerate(_pack_rows(h_r)):
            h2_ref[p, rs, :] = words

        h_lo = (h2 - h_r).astype(BF16)
        two = _dot(h_hi, wr_ref[...])
        logits = two[:, 0:LANES] + two[:, LANES:2 * LANES] + _dot(h_lo, wr_ref[:, 0:LANES]) + br_ref[...]
        logit_ref[rs, :] = logits


def _mixer_out(og, zb, x, modl, wga, wco, wo, n2g, wr, br, batch, seq):
    N, D = x.shape
    tm = MIX_TILE
    ns = seq // tm
    tok = lambda b, s: (b * ns + s, 0)
    return pl.pallas_call(
        _mixout_kernel,
        grid=(batch, ns),
        in_specs=[
            pl.BlockSpec((tm, D), tok),
            pl.BlockSpec((tm, zb.shape[1]), tok),
            pl.BlockSpec((tm, D), tok),
            pl.BlockSpec((1, 6, D), lambda b, s: (b, 0, 0)),
            _const_spec((D, D)),
            _const_spec((D, D)),
            _const_spec((D, D)),
            _const_spec((1, D)),
            _const_spec((D, 2 * LANES)),
            _const_spec((1, LANES)),
        ],
        out_specs=[
            pl.BlockSpec((tm, D), tok),
            pl.BlockSpec((D // 2 // ROW_PIECE, tm, ROW_PIECE), lambda b, s: (0, b * ns + s, 0)),
            pl.BlockSpec((tm, LANES), tok),
        ],
        out_shape=[
            jax.ShapeDtypeStruct((N, D), F32),
            jax.ShapeDtypeStruct((D // 2 // ROW_PIECE, N, ROW_PIECE), U32),
            jax.ShapeDtypeStruct((N, LANES), F32),
        ],
        compiler_params=_params(("arbitrary", "arbitrary")),
        name="mixer_out",
    )(og, zb, x, modl, wga, wco, wo, n2g, wr, br)


def _route_kernel(logit_ref, routet_ref, route_ref, cnt_ref, run_ref):
    tm = logit_ref.shape[0]
    sub = MIX_SUB
    rows8 = EXPERTS_PER_GROUP

    @pl.when(pl.program_id(0) == 0)
    def _():
        run_ref[...] = jnp.zeros_like(run_ref)

    sub8 = lax.broadcasted_iota(jnp.int32, (rows8, sub), 0)
    erow = lax.broadcasted_iota(jnp.int32, (N_EXPERTS, sub), 0)
    tr = lax.broadcasted_iota(jnp.int32, (sub, sub), 0)
    tc = lax.broadcasted_iota(jnp.int32, (sub, sub), 1)
    earlier = (tr < tc).astype(BF16)
    ones = jnp.ones((sub, LANES), BF16)
    neg = -jnp.inf
    run = run_ref[...]
    for r0 in range(0, tm, sub):
        lt = logit_ref[r0:r0 + sub, :].T
        gl = lt[0:N_GROUPS, :]
        gmax = jnp.max(gl, axis=0, keepdims=True)
        gsum = jnp.sum(jnp.exp(gl - gmax), axis=0, keepdims=True)
        g_w = 1.0 / gsum
        g_idx = jnp.min(jnp.where(gl == gmax, sub8, N_GROUPS), axis=0, keepdims=True)
        el = lt[N_GROUPS:N_GROUPS + rows8, :]
        for g in range(1, N_GROUPS):
            el = jnp.where(g_idx == g, lt[N_GROUPS + g * rows8:N_GROUPS + (g + 1) * rows8, :], el)
        e1 = jnp.max(el, axis=0, keepdims=True)
        i1 = jnp.min(jnp.where(el == e1, sub8, rows8), axis=0, keepdims=True)
        el2 = jnp.where(sub8 == i1, neg, el)
        e2 = jnp.max(el2, axis=0, keepdims=True)
        i2 = jnp.min(jnp.where(el2 == e2, sub8, rows8), axis=0, keepdims=True)
        ratio = jnp.exp(e2 - e1)
        w1 = g_w / (1.0 + ratio)
        w2 = g_w * ratio / (1.0 + ratio)
        id1 = g_idx * rows8 + i1
        id2 = g_idx * rows8 + i2

        oh1 = erow == id1
        oh2 = erow == id2
        oh1b = jnp.where(oh1, 1.0, 0.0).astype(BF16)
        oh2b = jnp.where(oh2, 1.0, 0.0).astype(BF16)
        tot1 = _dot(oh1b, ones)
        tot2 = _dot(oh2b, ones)
        base1 = jnp.concatenate([run] * (sub // LANES), axis=1)
        base2 = jnp.concatenate([run + tot1] * (sub // LANES), axis=1)
        c1 = _dot(oh1b, earlier) + base1
        c2 = _dot(oh2b, earlier) + base2
        rank1 = jnp.sum(jnp.where(oh1, c1, 0.0), axis=0, keepdims=True)
        rank2 = jnp.sum(jnp.where(oh2, c2, 0.0), axis=0, keepdims=True)
        run = run + tot1 + tot2

        rec = jnp.where(sub8 == R_ID1, id1.astype(F32), 0.0)
        rec = jnp.where(sub8 == R_ID2, id2.astype(F32), rec)
        rec = jnp.where(sub8 == R_W1, w1, rec)
        rec = jnp.where(sub8 == R_W2, w2, rec)
        rec = jnp.where(sub8 == R_RANK1, rank1, rec)
        rec = jnp.where(sub8 == R_RANK2, rank2, rec)
        routet_ref[:, r0:r0 + sub] = rec
        rec_full = jnp.concatenate([rec, jnp.zeros((LANES - rows8, sub), F32)], axis=0)
        route_ref[r0:r0 + sub, :] = rec_full.T
    run_ref[...] = run
    cnt_ref[...] = run


def _routing(logits):
    N = logits.shape[0]
    tm = 1024 if N % 1024 == 0 else MIX_SUB
    return pl.pallas_call(
        _route_kernel,
        grid=(N // tm,),
        in_specs=[pl.BlockSpec((tm, LANES), lambda i: (i, 0))],
        out_specs=[
            pl.BlockSpec((8, tm), lambda i: (0, i)),
            pl.BlockSpec((tm, LANES), lambda i: (i, 0)),
            pl.BlockSpec((N_EXPERTS, LANES), lambda i: (0, 0)),
        ],
        out_shape=[
            jax.ShapeDtypeStruct((8, N), F32),
            jax.ShapeDtypeStruct((N, LANES), F32),
            jax.ShapeDtypeStruct((N_EXPERTS, LANES), F32),
        ],
        scratch_shapes=[pltpu.VMEM((N_EXPERTS, LANES), F32)],
        compiler_params=_params(("arbitrary",)),
        name="moe_route",
    )(logits)


SC_WINDOW = 128


def _sc_mesh():
    return plsc.VectorSubcoreMesh(core_axis_name="core", subcore_axis_name="subcore")


def _piece_index(rows, pieces, n_rows):
    return (jnp.arange(pieces, dtype=jnp.int32)[:, None] * n_rows + rows[None, :]).reshape(1, -1)


def _sc_dispatch(h2, dest1, dest2, n_rows):
    pieces, N, pc = h2.shape
    w = SC_WINDOW

    @functools.partial(pl.kernel, out_type=jax.ShapeDtypeStruct((pieces * n_rows, pc), h2.dtype),
                       mesh=_sc_mesh(), scratch_types=[], name="moe_dispatch_sc")
    def run(x_hbm, i1_hbm, i2_hbm, o_hbm):
        def body(x_vmem, i1_vmem, i2_vmem):
            pltpu.sync_copy(x_vmem, o_hbm.at[i1_vmem.at[0]])
            pltpu.sync_copy(x_vmem, o_hbm.at[i2_vmem.at[0]])

        pltpu.emit_pipeline(
            body,
            grid=(pieces * N // w,),
            in_specs=[
                pl.BlockSpec((w, pc), lambda i: (i, 0)),
                pl.BlockSpec((1, w), lambda i: (0, i)),
                pl.BlockSpec((1, w), lambda i: (0, i)),
            ],
            out_specs=[],
            core_axis_name=("core", "subcore"),
            dimension_semantics=(pltpu.PARALLEL,),
        )(x_hbm, i1_hbm, i2_hbm)

    xs = run(h2.reshape(pieces * N, pc), _piece_index(dest1, pieces, n_rows),
             _piece_index(dest2, pieces, n_rows))
    return xs.reshape(pieces, n_rows, pc)


def _sc_return(yb, dest):
    M = dest.shape[0]
    pieces, P, pc = yb.shape
    w = SC_WINDOW

    @functools.partial(pl.kernel, out_type=jax.ShapeDtypeStruct((pieces * M, pc), yb.dtype),
                       mesh=_sc_mesh(), scratch_types=[], name="moe_return_sc")
    def run(y_hbm, i_hbm, o_hbm):
        def body(i_vmem, o_vmem):
            pltpu.sync_copy(y_hbm.at[i_vmem.at[0]], o_vmem)

        pltpu.emit_pipeline(
            body,
            grid=(pieces * M // w,),
            in_specs=[pl.BlockSpec((1, w), lambda i: (0, i))],
            out_specs=[pl.BlockSpec((w, pc), lambda i: (i, 0))],
            core_axis_name=("core", "subcore"),
            dimension_semantics=(pltpu.PARALLEL,),
        )(i_hbm, o_hbm)

    return run(yb.reshape(pieces * P, pc), _piece_index(dest, pieces, P)).reshape(pieces, M, pc)


def _expert_kernel(ib_ref, ie_ref, lo_ref, hi_ref, ni_ref, xs_ref, w1_ref, w3_ref, w2_ref, y_ref,
                   w1b, w3b, w2b):
    j = pl.program_id(0)
    jp = jnp.maximum(j - 1, 0)
    live = j < ni_ref[0]
    new_expert = jnp.logical_or(j == 0, ie_ref[j] != ie_ref[jp])
    new_block = jnp.logical_or(j == 0, ib_ref[j] != ib_ref[jp])

    @pl.when(jnp.logical_and(new_expert, live))
    def _():
        w1b[...] = w1_ref[0, 0].astype(BF16)
        w3b[...] = w3_ref[0, 0].astype(BF16)
        w2b[...] = w2_ref[0, 0].astype(BF16)

    @pl.when(live)
    def _():
        xb = _unpack_rows(xs_ref).astype(BF16)
        h1 = _dot(xb, w1b[...])
        h3 = _dot(xb, w3b[...])
        a = (h1 * jax.nn.sigmoid(h1) * h3).astype(BF16)
        y = _pack_rows(_round_bf16(_dot(a, w2b[...]))[1])
        row = lax.broadcasted_iota(jnp.int32, y[0].shape, 0)
        mine = jnp.logical_and(row >= lo_ref[j], row < hi_ref[j])

        @pl.when(new_block)
        def _():
            for p, words in enumerate(y):
                y_ref[p] = jnp.where(mine, words, jnp.uint32(0))

        @pl.when(jnp.logical_not(new_block))
        def _():
            for p, words in enumerate(y):
                y_ref[p] = jnp.where(mine, words, y_ref[p])


def _experts(xs, items, w1, w3, w2, layer):
    pieces, P, pc = xs.shape
    _, E, D, DE = w1.shape
    bm = MOE_BLOCK
    n_items_max = P // bm + E
    rows = lambda j, ib, ie, lo, hi, ni: (0, ib[j], 0)
    wsel = lambda j, ib, ie, lo, hi, ni: (layer, ie[j], 0, 0)
    return pl.pallas_call(
        _expert_kernel,
        grid_spec=pltpu.PrefetchScalarGridSpec(
            num_scalar_prefetch=5,
            grid=(n_items_max,),
            in_specs=[
                pl.BlockSpec((pieces, bm, pc), rows),
                pl.BlockSpec((1, 1, D, DE), wsel),
                pl.BlockSpec((1, 1, D, DE), wsel),
                pl.BlockSpec((1, 1, DE, D), wsel),
            ],
            out_specs=pl.BlockSpec((pieces, bm, pc), rows),
            scratch_shapes=[
                pltpu.VMEM((D, DE), BF16),
                pltpu.VMEM((D, DE), BF16),
                pltpu.VMEM((DE, D), BF16),
            ],
        ),
        out_shape=jax.ShapeDtypeStruct((pieces, P, pc), U32),
        compiler_params=_params(("arbitrary",)),
        name="moe_experts",
    )(*items, xs, w1, w3, w2)


def _combine_kernel(x1_ref, route_ref, mod_ref, fg_ref, y0_ref, y1_ref, o_ref):
    x2 = x1_ref[...] + mod_ref[0][5:6] * _moe_mix(route_ref, y0_ref, y1_ref)
    ms = jnp.mean(x2 * x2, axis=-1, keepdims=True)
    o_ref[...] = x2 * lax.rsqrt(ms + NORM_EPS) * fg_ref[...]


def _final_combine(x1, route, modl, final_g, yk, seq):
    N, D = x1.shape
    tm = ROW_TILE
    nt = N // tm
    tok = lambda i: (i, 0)
    return pl.pallas_call(
        _combine_kernel,
        grid=(nt,),
        in_specs=[
            pl.BlockSpec((tm, D), tok),
            pl.BlockSpec((tm, LANES), tok),
            pl.BlockSpec((1, 6, D), lambda i: ((i * tm) // seq, 0, 0)),
            pl.BlockSpec((1, D), lambda i: (0, 0)),
            pl.BlockSpec((yk.shape[0], tm, ROW_PIECE), lambda i: (0, i, 0)),
            pl.BlockSpec((yk.shape[0], tm, ROW_PIECE), lambda i: (0, nt + i, 0)),
        ],
        out_specs=pl.BlockSpec((tm, D), tok),
        out_shape=jax.ShapeDtypeStruct((N, D), F32),
        compiler_params=_params(("arbitrary",)),
        name="moe_combine",
    )(x1, route, modl, final_g, yk, yk)


def _dest_kernel(routet_ref, cnt_ref, d1_ref, d2_ref):
    tm = routet_ref.shape[1]
    cnt = cnt_ref[...]
    r = lax.broadcasted_iota(jnp.int32, (N_EXPERTS, N_EXPERTS), 0)
    c = lax.broadcasted_iota(jnp.int32, (N_EXPERTS, N_EXPERTS), 1)
    lower = (r >= c).astype(F32)
    end = jnp.dot(lower, cnt, precision=HIGHEST, preferred_element_type=F32)
    start = jnp.concatenate([end - cnt] * (tm // LANES), axis=1)
    expert = lax.broadcasted_iota(jnp.int32, (N_EXPERTS, tm), 0).astype(F32)
    rec = routet_ref[...]
    for row_id, row_rank, out in ((R_ID1, R_RANK1, d1_ref), (R_ID2, R_RANK2, d2_ref)):
        sel = expert == rec[row_id:row_id + 1, :]
        first = jnp.sum(jnp.where(sel, start, 0.0), axis=0, keepdims=True)
        out[0] = (first + rec[row_rank:row_rank + 1, :]).astype(jnp.int32)


def _destinations(route_t, cnt):
    N = route_t.shape[1]
    tm = 2048 if N % 2048 == 0 else MIX_SUB
    out = jax.ShapeDtypeStruct((N // tm, 1, tm), jnp.int32)
    d1, d2 = pl.pallas_call(
        _dest_kernel,
        grid=(N // tm,),
        in_specs=[pl.BlockSpec((8, tm), lambda i: (0, i)),
                  pl.BlockSpec((N_EXPERTS, LANES), lambda i: (0, 0))],
        out_specs=[pl.BlockSpec((1, 1, tm), lambda i: (i, 0, 0))] * 2,
        out_shape=[out, out],
        compiler_params=_params(("arbitrary",)),
        name="moe_dest",
    )(route_t, cnt)
    return d1.reshape(N), d2.reshape(N)


def _work_items(cnt, n_items_max):
    bm = MOE_BLOCK
    i32 = jnp.int32
    counts = cnt[:, 0].astype(i32)
    end = jnp.cumsum(counts)
    start = end - counts
    first = start // bm
    last = jnp.maximum(end - 1, 0) // bm
    per_e = jnp.where(counts > 0, last - first + 1, 0)
    item_end = jnp.cumsum(per_e)
    item_start = item_end - per_e
    n_items = item_end[-1]
    j = jnp.minimum(jnp.arange(n_items_max, dtype=i32), n_items - 1)
    ie = jnp.minimum(jnp.sum((item_end[None, :] <= j[:, None]).astype(i32), axis=1), N_EXPERTS - 1)
    onehot = ie[:, None] == jnp.arange(N_EXPERTS, dtype=i32)[None, :]
    pick = lambda tbl: jnp.sum(jnp.where(onehot, tbl[None, :], 0), axis=1)
    ib = pick(first) + (j - pick(item_start))
    lo = jnp.maximum(pick(start), ib * bm) - ib * bm
    hi = jnp.minimum(pick(end), (ib + 1) * bm) - ib * bm
    return (ib.astype(i32), ie.astype(i32), lo.astype(i32), hi.astype(i32),
            n_items.reshape(1).astype(i32))


def kernel(x, c, mod_w, mod_b, norm1_g, w_in, gate_w2, gate_b, gla_norm_g, conv_w, w_gla_out,
           w_conv_out, w_out, norm2_g, router_group_w, router_group_b, router_expert_w,
           router_expert_b, expert_w1, expert_w3, expert_w2, final_norm_g):
    B, S, D = x.shape
    L = mod_w.shape[0]
    N = B * S
    dk = gate_w2.shape[2]
    rank = gate_w2.shape[1]
    dv = gla_norm_g.shape[1]
    n_slots = N * 2
    n_items_max = n_slots // MOE_BLOCK + N_EXPERTS
    assert S % GLA_TILE == 0 and S % MIX_TILE == 0 and S % IN_TILE == 0 and N % ROW_TILE == 0
    assert n_slots % MOE_BLOCK == 0
    assert N_GROUPS + N_EXPERTS <= LANES and EXPERTS_PER_GROUP == 8 and 3 * rank <= LANES and D % (2 * ROW_PIECE) == 0

    mod = _modulation(c, mod_w, mod_b)
    xf = x.reshape(N, D)
    o_gd = 2 * dk + dv
    o_r = o_gd + rank
    o_b = o_r + dv
    for l in range(L):
        modl = mod[l].reshape(B, 6, D)
        wl = w_in[l]
        wa = jnp.concatenate([wl[:, :o_gd], wl[:, o_r:o_b]], axis=1).astype(BF16)
        wb = wl[:, o_b:].astype(BF16)
        wgd = wl[:, o_gd:o_r]
        wg = jnp.pad(jnp.concatenate([wgd, wgd, wgd], axis=1), ((0, 0), (0, LANES - 3 * rank))).astype(BF16)
        g_hi = gate_w2[l].astype(BF16)
        g_lo = (gate_w2[l] - g_hi.astype(F32)).astype(BF16)
        gw2 = jnp.pad(jnp.concatenate([g_hi, g_hi, g_lo], axis=0), ((0, LANES - 3 * rank), (0, 0)))
        if l == 0:
            za, zb, zg = _in_projection(xf, modl, norm1_g[l].reshape(1, D), conv_w[l], wa, wb, wg, S)
        else:
            xf, za, zb, zg = _in_projection(x1, modl, norm1_g[l].reshape(1, D), conv_w[l], wa, wb, wg, S,
                                            combine=(route, yk, mod[l - 1].reshape(B, 6, D)))
        og = _gla(za, zg, gw2, gate_b[l].reshape(1, dk), gla_norm_g[l].reshape(1, dv), B, S, dk, dv, rank)
        wr = jnp.pad(jnp.concatenate([router_group_w[l], router_expert_w[l]], axis=1),
                     ((0, 0), (0, LANES - N_GROUPS - N_EXPERTS)))
        wr_hi = wr.astype(BF16)
        wr = jnp.concatenate([wr_hi, (wr - wr_hi.astype(F32)).astype(BF16)], axis=1)
        br = jnp.pad(jnp.concatenate([router_group_b[l], router_expert_b[l]]),
                     (0, LANES - N_GROUPS - N_EXPERTS)).reshape(1, LANES)
        x1, h2, logits = _mixer_out(
            og, zb, xf, modl, w_gla_out[l].astype(BF16), w_conv_out[l].astype(BF16),
            w_out[l].astype(BF16), norm2_g[l].reshape(1, D), wr, br, B, S)
        route_t, route, cnt = _routing(logits)
        dest1, dest2 = _destinations(route_t, cnt)
        items = _work_items(cnt, n_items_max)
        xs = _sc_dispatch(h2, dest1, dest2, n_slots)
        yb = _experts(xs, items, expert_w1, expert_w3, expert_w2, l)
        yk = _sc_return(yb, jnp.concatenate([dest1, dest2]))
    out = _final_combine(x1, route, modl, final_norm_g.reshape(1, D), yk, S)
    return out.reshape(B, S, D)
```

```python
import functools

import jax
import jax.numpy as jnp
from jax import lax
from jax.experimental import pallas as pl
from jax.experimental.pallas import tpu as pltpu
from jax.experimental.pallas import tpu_sc as plsc

F32 = jnp.float32
BF16 = jnp.bfloat16
HIGHEST = lax.Precision.HIGHEST

GLA_HEADS = 4
GATE_TAU = 16.0
GLA_CHUNK = 64
N_GROUPS = 8
EXPERTS_PER_GROUP = 8
N_EXPERTS = N_GROUPS * EXPERTS_PER_GROUP
NORM_EPS = 1e-6

LANES = 128
VMEM_LIMIT_BYTES = 56 * 1024 * 1024

IN_TILE = 512
GLA_TILE = 512
MIX_TILE = 512
MIX_SUB = 256
MOE_BLOCK = 512
ROW_TILE = 1024
ROW_PIECE = 256
U32 = jnp.uint32

R_ID1, R_ID2, R_W1, R_W2, R_RANK1, R_RANK2 = 0, 1, 2, 3, 4, 5


def _dot(a, b):
    return jnp.dot(a, b, preferred_element_type=F32)


def _round_bf16(x):
    xb = x.astype(BF16)
    return xb, xb.astype(F32)


def _pack_rows(xr):
    half = xr.shape[1] // 2
    out = []
    for p in range(half // ROW_PIECE):
        lo = xr[:, p * ROW_PIECE:(p + 1) * ROW_PIECE]
        hi = xr[:, half + p * ROW_PIECE:half + (p + 1) * ROW_PIECE]
        out.append((pltpu.bitcast(lo, U32) >> 16) | pltpu.bitcast(hi, U32))
    return out


def _unpack_rows(ref):
    words = [ref[p] for p in range(ref.shape[0])]
    lo = [pltpu.bitcast(w << 16, F32) for w in words]
    hi = [pltpu.bitcast(w & jnp.uint32(0xFFFF0000), F32) for w in words]
    return jnp.concatenate(lo + hi, axis=-1)


def _const_spec(shape):
    nd = len(shape)
    return pl.BlockSpec(shape, lambda *_: (0,) * nd, pipeline_mode=pl.Buffered(1))


def _params(sem):
    return pltpu.CompilerParams(dimension_semantics=sem, vmem_limit_bytes=VMEM_LIMIT_BYTES)


def _mod_kernel(c_ref, w_ref, b_ref, o_ref):
    c = c_ref[...]
    sc = c * jax.nn.sigmoid(c)
    o_ref[0] = jnp.dot(sc, w_ref[0], precision=HIGHEST, preferred_element_type=F32) + b_ref[0]


def _modulation(c, mod_w, mod_b):
    L, D, D6 = mod_w.shape
    B = c.shape[0]
    nj = D6 // D
    return pl.pallas_call(
        _mod_kernel,
        grid=(L, nj),
        in_specs=[
            pl.BlockSpec((B, D), lambda l, j: (0, 0)),
            pl.BlockSpec((1, D, D), lambda l, j: (l, 0, j)),
            pl.BlockSpec((1, 1, D), lambda l, j: (l, 0, j)),
        ],
        out_specs=pl.BlockSpec((1, B, D), lambda l, j: (l, 0, j)),
        out_shape=jax.ShapeDtypeStruct((L, B, D6), F32),
        compiler_params=_params(("arbitrary", "arbitrary")),
        name="adaln_mod",
    )(c, mod_w, mod_b.reshape(L, 1, D6))


CONV_COLS = 256


def _moe_mix(route_ref, y0_ref, y1_ref):
    rec = route_ref[...]
    w1 = rec[:, R_W1:R_W1 + 1]
    w2 = rec[:, R_W2:R_W2 + 1]
    return w1 * _unpack_rows(y0_ref) + w2 * _unpack_rows(y1_ref)


def _inproj_kernel(*refs, seq, fuse_combine):
    if fuse_combine:
        (x_ref, route_ref, y0_ref, y1_ref, modp_ref, mod_ref, g_ref, cw_ref, wa_ref, wb_ref, wg_ref,
         xo_ref, za_ref, zb_ref, zg_ref, carry_ref) = refs
        x = x_ref[...] + modp_ref[0][5:6] * _moe_mix(route_ref, y0_ref, y1_ref)
        xo_ref[...] = x
    else:
        (x_ref, mod_ref, g_ref, cw_ref, wa_ref, wb_ref, wg_ref, za_ref, zb_ref, zg_ref, carry_ref) = refs
        x = x_ref[...]
    tm, d = x.shape

    @pl.when((pl.program_id(0) * tm) % seq == 0)
    def _():
        carry_ref[...] = jnp.zeros_like(carry_ref)

    ms = jnp.mean(x * x, axis=-1, keepdims=True)
    m = mod_ref[0]
    h = (x * lax.rsqrt(ms + NORM_EPS) * (g_ref[...] * (1.0 + m[1:2])) + m[0:1]).astype(BF16)
    for j in range(za_ref.shape[1] // d):
        za_ref[:, j * d:(j + 1) * d] = _dot(h, wa_ref[:, j * d:(j + 1) * d]).astype(BF16)
    zg_ref[...] = _dot(h, wg_ref[...])
    for j in range(1, zb_ref.shape[1] // d):
        zb_ref[:, j * d:(j + 1) * d] = _dot(h, wb_ref[:, (j + 2) * d:(j + 3) * d]).astype(BF16)

    w = CONV_COLS
    rowi = lax.broadcasted_iota(jnp.int32, (tm, w), 0)
    for j in range(d // w):
        cols = slice(j * w, (j + 1) * w)
        cb = _dot(h, wb_ref[:, j * w:(j + 1) * w])
        cc = _dot(h, wb_ref[:, d + j * w:d + (j + 1) * w])
        ch = _dot(h, wb_ref[:, 2 * d + j * w:2 * d + (j + 1) * w])
        u = cc * ch
        prev = carry_ref[:, cols]
        u1 = jnp.where(rowi == 0, prev[7:8], pltpu.roll(u, 1, 0))
        u2 = jnp.where(rowi == 0, prev[6:7], jnp.where(rowi == 1, prev[7:8], pltpu.roll(u, 2, 0)))
        carry_ref[:, cols] = u[tm - 8:tm]
        conv = cw_ref[0:1, cols] * u2 + cw_ref[1:2, cols] * u1 + cw_ref[2:3, cols] * u
        zb_ref[:, cols] = (cb * conv).astype(BF16)


def _in_projection(x, modl, norm_g, conv_w, wa, wb, wg, seq, combine=None):
    N, D = x.shape
    tm = IN_TILE
    nt = N // tm
    ca, cb = wa.shape[1], wb.shape[1] - 2 * D
    tok = lambda i: (i, 0)
    per_batch = lambda i: ((i * tm) // seq, 0, 0)
    in_specs = [pl.BlockSpec((tm, D), tok)]
    args = [x]
    out_specs, out_shape = [], []
    if combine is not None:
        route, yk, mod_prev = combine
        piece_blk = (yk.shape[0], tm, ROW_PIECE)
        in_specs += [
            pl.BlockSpec((tm, LANES), tok),
            pl.BlockSpec(piece_blk, lambda i: (0, i, 0)),
            pl.BlockSpec(piece_blk, lambda i: (0, nt + i, 0)),
            pl.BlockSpec((1, 6, D), per_batch),
        ]
        args += [route, yk, yk, mod_prev]
        out_specs.append(pl.BlockSpec((tm, D), tok))
        out_shape.append(jax.ShapeDtypeStruct((N, D), F32))
    in_specs += [
        pl.BlockSpec((1, 6, D), per_batch),
        _const_spec((1, D)),
        _const_spec(conv_w.shape),
        _const_spec((D, ca)),
        _const_spec(wb.shape),
        _const_spec((D, LANES)),
    ]
    args += [modl, norm_g, conv_w, wa, wb, wg]
    out_specs += [
        pl.BlockSpec((tm, ca), tok),
        pl.BlockSpec((tm, cb), tok),
        pl.BlockSpec((tm, LANES), tok),
    ]
    out_shape += [
        jax.ShapeDtypeStruct((N, ca), BF16),
        jax.ShapeDtypeStruct((N, cb), BF16),
        jax.ShapeDtypeStruct((N, LANES), F32),
    ]
    kern = functools.partial(_inproj_kernel, seq=seq, fuse_combine=combine is not None)
    return pl.pallas_call(
        kern,
        grid=(nt,),
        in_specs=in_specs,
        out_specs=out_specs,
        out_shape=out_shape,
        scratch_shapes=[pltpu.VMEM((8, D), F32)],
        compiler_params=_params(("arbitrary",)),
        name="in_projection",
    )(*args)


def _log_sigmoid(x):
    return jnp.minimum(x, 0.0) - jnp.log(1.0 + jnp.exp(-jnp.abs(x)))


def _gla_kernel(za_ref, zg_ref, gw2_ref, gb_ref, ng_ref, o_ref, st_ref, lg_ref, *, dk, dv, rank):
    heads = GLA_HEADS
    dkh, dvh = dk // heads, dv // heads
    c = GLA_CHUNK
    ts = za_ref.shape[0]

    @pl.when(pl.program_id(1) == 0)
    def _():
        st_ref[...] = jnp.zeros_like(st_ref)

    zg = zg_ref[...]
    zg_hi = zg.astype(BF16)
    zg_lo = (zg - zg_hi.astype(F32)).astype(BF16)
    lane = lax.broadcasted_iota(jnp.int32, zg.shape, 1)
    lhs = jnp.where(jnp.logical_and(lane >= rank, lane < 2 * rank), zg_lo, zg_hi)
    pre = _dot(lhs, gw2_ref[...]) + gb_ref[...]
    lg = _log_sigmoid(pre) * (1.0 / GATE_TAU)
    lg_hi = lg.astype(BF16)
    lg_ref[:, 0:dk] = lg_hi
    lg_ref[:, dk:2 * dk] = (lg - lg_hi.astype(F32)).astype(BF16)

    row = lax.broadcasted_iota(jnp.int32, (c, c), 0)
    col = lax.broadcasted_iota(jnp.int32, (c, c), 1)
    causal = row >= col
    tril = causal.astype(BF16)
    qscale = dkh ** -0.5

    for ci in range(ts // c):
        rows = slice(ci * c, (ci + 1) * c)
        b_two = _dot(tril, lg_ref[rows, :])
        b_all = b_two[:, 0:dk] + b_two[:, dk:2 * dk]
        for hd in range(heads):
            ks = slice(hd * dkh, (hd + 1) * dkh)
            q = za_ref[rows, hd * dkh:(hd + 1) * dkh].astype(F32) * qscale
            k = za_ref[rows, dk + hd * dkh:dk + (hd + 1) * dkh].astype(F32)
            v = za_ref[rows, 2 * dk + hd * dvh:2 * dk + (hd + 1) * dvh]
            r = za_ref[rows, 2 * dk + dv + hd * dvh:2 * dk + dv + (hd + 1) * dvh].astype(F32)
            b = b_all[:, ks]
            b_last = b[c - 1:c, :]
            q_t = (q * jnp.exp(b)).astype(BF16)
            k_t = (k * jnp.exp(-b)).astype(BF16)
            k_s = (k * jnp.exp(b_last - b)).astype(BF16)
            decay = jnp.exp(b_last)
            attn = lax.dot_general(q_t, k_t, (((1,), (1,)), ((), ())), preferred_element_type=F32)
            attn = jnp.where(causal, attn, 0.0).astype(BF16)
            st = st_ref[hd]
            o = _dot(attn, v) + lax.dot_general(
                q_t, st.astype(BF16), (((1,), (1,)), ((), ())), preferred_element_type=F32)
            upd = lax.dot_general(v, k_s, (((0,), (0,)), ((), ())), preferred_element_type=F32)
            st_ref[hd] = st * decay + upd
            ms = jnp.mean(o * o, axis=-1, keepdims=True)
            on = o * lax.rsqrt(ms + NORM_EPS) * ng_ref[:, hd * dvh:(hd + 1) * dvh]
            o_ref[rows, hd * dvh:(hd + 1) * dvh] = (on * (r * jax.nn.sigmoid(r))).astype(BF16)


def _gla(za, zg, gw2, gb, ng, batch, seq, dk, dv, rank):
    N = za.shape[0]
    ts = GLA_TILE
    ns = seq // ts
    heads = GLA_HEADS
    kern = functools.partial(_gla_kernel, dk=dk, dv=dv, rank=rank)
    return pl.pallas_call(
        kern,
        grid=(batch, ns),
        in_specs=[
            pl.BlockSpec((ts, za.shape[1]), lambda b, s: (b * ns + s, 0)),
            pl.BlockSpec((ts, LANES), lambda b, s: (b * ns + s, 0)),
            _const_spec((LANES, dk)),
            _const_spec((1, dk)),
            _const_spec((1, dv)),
        ],
        out_specs=pl.BlockSpec((ts, dv), lambda b, s: (b * ns + s, 0)),
        out_shape=jax.ShapeDtypeStruct((N, dv), BF16),
        scratch_shapes=[
            pltpu.VMEM((heads, dv // heads, dk // heads), F32),
            pltpu.VMEM((ts, 2 * dk), BF16),
        ],
        compiler_params=_params(("arbitrary", "arbitrary")),
        name="gla",
    )(za, zg, gw2, gb, ng)


def _mixout_kernel(og_ref, zb_ref, x_ref, mod_ref, wga_ref, wco_ref, wo_ref, n2_ref,
                   wr_ref, br_ref, x1_ref, h2_ref, logit_ref):
    tm, d = x_ref.shape
    m = mod_ref[0]
    sub = MIX_SUB
    gain2 = n2_ref[...] * (1.0 + m[4:5])
    for r0 in range(0, tm, sub):
        rs = slice(r0, r0 + sub)
        ga = zb_ref[rs, d:2 * d].astype(F32)
        gc = zb_ref[rs, 2 * d:3 * d].astype(F32)
        y_conv = _dot(zb_ref[rs, 0:d], wco_ref[...])
        y_gla = _dot(og_ref[rs, :], wga_ref[...])
        y = jax.nn.sigmoid(ga) * y_gla + jax.nn.sigmoid(gc) * y_conv
        y = _dot(y.astype(BF16), wo_ref[...])
        x1 = x_ref[rs, :] + m[2:3] * y
        x1_ref[rs, :] = x1

        ms = jnp.mean(x1 * x1, axis=-1, keepdims=True)
        h2 = x1 * lax.rsqrt(ms + NORM_EPS) * gain2 + m[3:4]
        h_hi, h_r = _round_bf16(h2)
        for p, words in enumerate(_pack_rows(h_r)):
            h2_ref[p, rs, :] = words

        h_lo = (h2 - h_r).astype(BF16)
        two = _dot(h_hi, wr_ref[...])
        logits = two[:, 0:LANES] + two[:, LANES:2 * LANES] + _dot(h_lo, wr_ref[:, 0:LANES]) + br_ref[...]
        logit_ref[rs, :] = logits


def _mixer_out(og, zb, x, modl, wga, wco, wo, n2g, wr, br, batch, seq):
    N, D = x.shape
    tm = MIX_TILE
    ns = seq // tm
    tok = lambda b, s: (b * ns + s, 0)
    return pl.pallas_call(
        _mixout_kernel,
        grid=(batch, ns),
        in_specs=[
            pl.BlockSpec((tm, D), tok),
            pl.BlockSpec((tm, zb.shape[1]), tok),
            pl.BlockSpec((tm, D), tok),
            pl.BlockSpec((1, 6, D), lambda b, s: (b, 0, 0)),
            _const_spec((D, D)),
            _const_spec((D, D)),
            _const_spec((D, D)),
            _const_spec((1, D)),
            _const_spec((D, 2 * LANES)),
            _const_spec((1, LANES)),
        ],
        out_specs=[
            pl.BlockSpec((tm, D), tok),
            pl.BlockSpec((D // 2 // ROW_PIECE, tm, ROW_PIECE), lambda b, s: (0, b * ns + s, 0)),
            pl.BlockSpec((tm, LANES), tok),
        ],
        out_shape=[
            jax.ShapeDtypeStruct((N, D), F32),
            jax.ShapeDtypeStruct((D // 2 // ROW_PIECE, N, ROW_PIECE), U32),
            jax.ShapeDtypeStruct((N, LANES), F32),
        ],
        compiler_params=_params(("arbitrary", "arbitrary")),
        name="mixer_out",
    )(og, zb, x, modl, wga, wco, wo, n2g, wr, br)


def _route_kernel(logit_ref, routet_ref, route_ref, cnt_ref, run_ref):
    tm = logit_ref.shape[0]
    sub = MIX_SUB
    rows8 = EXPERTS_PER_GROUP

    @pl.when(pl.program_id(0) == 0)
    def _():
        run_ref[...] = jnp.zeros_like(run_ref)

    sub8 = lax.broadcasted_iota(jnp.int32, (rows8, sub), 0)
    erow = lax.broadcasted_iota(jnp.int32, (N_EXPERTS, sub), 0)
    tr = lax.broadcasted_iota(jnp.int32, (sub, sub), 0)
    tc = lax.broadcasted_iota(jnp.int32, (sub, sub), 1)
    earlier = (tr < tc).astype(BF16)
    ones = jnp.ones((sub, LANES), BF16)
    neg = -jnp.inf
    run = run_ref[...]
    for r0 in range(0, tm, sub):
        lt = logit_ref[r0:r0 + sub, :].T
        gl = lt[0:N_GROUPS, :]
        gmax = jnp.max(gl, axis=0, keepdims=True)
        gsum = jnp.sum(jnp.exp(gl - gmax), axis=0, keepdims=True)
        g_w = 1.0 / gsum
        g_idx = jnp.min(jnp.where(gl == gmax, sub8, N_GROUPS), axis=0, keepdims=True)
        el = lt[N_GROUPS:N_GROUPS + rows8, :]
        for g in range(1, N_GROUPS):
            el = jnp.where(g_idx == g, lt[N_GROUPS + g * rows8:N_GROUPS + (g + 1) * rows8, :], el)
        e1 = jnp.max(el, axis=0, keepdims=True)
        i1 = jnp.min(jnp.where(el == e1, sub8, rows8), axis=0, keepdims=True)
        el2 = jnp.where(sub8 == i1, neg, el)
        e2 = jnp.max(el2, axis=0, keepdims=True)
        i2 = jnp.min(jnp.where(el2 == e2, sub8, rows8), axis=0, keepdims=True)
        ratio = jnp.exp(e2 - e1)
        w1 = g_w / (1.0 + ratio)
        w2 = g_w * ratio / (1.0 + ratio)
        id1 = g_idx * rows8 + i1
        id2 = g_idx * rows8 + i2

        oh1 = erow == id1
        oh2 = erow == id2
        oh1b = jnp.where(oh1, 1.0, 0.0).astype(BF16)
        oh2b = jnp.where(oh2, 1.0, 0.0).astype(BF16)
        tot1 = _dot(oh1b, ones)
        tot2 = _dot(oh2b, ones)
        base1 = jnp.concatenate([run] * (sub // LANES), axis=1)
        base2 = jnp.concatenate([run + tot1] * (sub // LANES), axis=1)
        c1 = _dot(oh1b, earlier) + base1
        c2 = _dot(oh2b, earlier) + base2
        rank1 = jnp.sum(jnp.where(oh1, c1, 0.0), axis=0, keepdims=True)
        rank2 = jnp.sum(jnp.where(oh2, c2, 0.0), axis=0, keepdims=True)
        run = run + tot1 + tot2

        rec = jnp.where(sub8 == R_ID1, id1.astype(F32), 0.0)
        rec = jnp.where(sub8 == R_ID2, id2.astype(F32), rec)
        rec = jnp.where(sub8 == R_W1, w1, rec)
        rec = jnp.where(sub8 == R_W2, w2, rec)
        rec = jnp.where(sub8 == R_RANK1, rank1, rec)
        rec = jnp.where(sub8 == R_RANK2, rank2, rec)
        routet_ref[:, r0:r0 + sub] = rec
        rec_full = jnp.concatenate([rec, jnp.zeros((LANES - rows8, sub), F32)], axis=0)
        route_ref[r0:r0 + sub, :] = rec_full.T
    run_ref[...] = run
    cnt_ref[...] = run


def _routing(logits):
    N = logits.shape[0]
    tm = 1024 if N % 1024 == 0 else MIX_SUB
    return pl.pallas_call(
        _route_kernel,
        grid=(N // tm,),
        in_specs=[pl.BlockSpec((tm, LANES), lambda i: (i, 0))],
        out_specs=[
            pl.BlockSpec((8, tm), lambda i: (0, i)),
            pl.BlockSpec((tm, LANES), lambda i: (i, 0)),
            pl.BlockSpec((N_EXPERTS, LANES), lambda i: (0, 0)),
        ],
        out_shape=[
            jax.ShapeDtypeStruct((8, N), F32),
            jax.ShapeDtypeStruct((N, LANES), F32),
            jax.ShapeDtypeStruct((N_EXPERTS, LANES), F32),
        ],
        scratch_shapes=[pltpu.VMEM((N_EXPERTS, LANES), F32)],
        compiler_params=_params(("arbitrary",)),
        name="moe_route",
    )(logits)


SC_WINDOW = 128


def _sc_mesh():
    return plsc.VectorSubcoreMesh(core_axis_name="core", subcore_axis_name="subcore")


def _piece_index(rows, pieces, n_rows):
    return (jnp.arange(pieces, dtype=jnp.int32)[:, None] * n_rows + rows[None, :]).reshape(1, -1)


def _sc_dispatch(h2, dest1, dest2, n_rows):
    pieces, N, pc = h2.shape
    w = SC_WINDOW

    @functools.partial(pl.kernel, out_type=jax.ShapeDtypeStruct((pieces * n_rows, pc), h2.dtype),
                       mesh=_sc_mesh(), scratch_types=[], name="moe_dispatch_sc")
    def run(x_hbm, i1_hbm, i2_hbm, o_hbm):
        def body(x_vmem, i1_vmem, i2_vmem):
            pltpu.sync_copy(x_vmem, o_hbm.at[i1_vmem.at[0]])
            pltpu.sync_copy(x_vmem, o_hbm.at[i2_vmem.at[0]])

        pltpu.emit_pipeline(
            body,
            grid=(pieces * N // w,),
            in_specs=[
                pl.BlockSpec((w, pc), lambda i: (i, 0)),
                pl.BlockSpec((1, w), lambda i: (0, i)),
                pl.BlockSpec((1, w), lambda i: (0, i)),
            ],
            out_specs=[],
            core_axis_name=("core", "subcore"),
            dimension_semantics=(pltpu.PARALLEL,),
        )(x_hbm, i1_hbm, i2_hbm)

    xs = run(h2.reshape(pieces * N, pc), _piece_index(dest1, pieces, n_rows),
             _piece_index(dest2, pieces, n_rows))
    return xs.reshape(pieces, n_rows, pc)


def _sc_return(yb, dest):
    M = dest.shape[0]
    pieces, P, pc = yb.shape
    w = SC_WINDOW

    @functools.partial(pl.kernel, out_type=jax.ShapeDtypeStruct((pieces * M, pc), yb.dtype),
                       mesh=_sc_mesh(), scratch_types=[], name="moe_return_sc")
    def run(y_hbm, i_hbm, o_hbm):
        def body(i_vmem, o_vmem):
            pltpu.sync_copy(y_hbm.at[i_vmem.at[0]], o_vmem)

        pltpu.emit_pipeline(
            body,
            grid=(pieces * M // w,),
            in_specs=[pl.BlockSpec((1, w), lambda i: (0, i))],
            out_specs=[pl.BlockSpec((w, pc), lambda i: (i, 0))],
            core_axis_name=("core", "subcore"),
            dimension_semantics=(pltpu.PARALLEL,),
        )(i_hbm, o_hbm)

    return run(yb.reshape(pieces * P, pc), _piece_index(dest, pieces, P)).reshape(pieces, M, pc)


def _expert_kernel(ib_ref, ie_ref, lo_ref, hi_ref, ni_ref, nxt_ref, slot_ref, xs_ref, w1_ref, w3_ref,
                   w2_ref, y_ref, wf1, wf3, wf2, w1b, w3b, w2b, sem, *, layer):
    j = pl.program_id(0)
    jp = jnp.maximum(j - 1, 0)
    live = j < ni_ref[0]
    new_expert = jnp.logical_or(j == 0, ie_ref[j] != ie_ref[jp])
    new_block = jnp.logical_or(j == 0, ib_ref[j] != ib_ref[jp])
    streams = ((w1_ref, wf1), (w3_ref, wf3), (w2_ref, wf2))

    def weight_copies(expert, s):
        return [pltpu.make_async_copy(w.at[layer, expert], buf.at[s], sem.at[s, k])
                for k, (w, buf) in enumerate(streams)]

    @pl.when(j == 0)
    def _():
        for cp in weight_copies(ie_ref[0], 0):
            cp.start()

    @pl.when(jnp.logical_and(new_expert, live))
    def _():
        s = slot_ref[j]
        for cp in weight_copies(ie_ref[j], s):
            cp.wait()
        w1b[...] = wf1[s].astype(BF16)
        w3b[...] = wf3[s].astype(BF16)
        w2b[...] = wf2[s].astype(BF16)

        @pl.when(nxt_ref[j] >= 0)
        def _():
            for cp in weight_copies(nxt_ref[j], 1 - s):
                cp.start()

    @pl.when(live)
    def _():
        xb = _unpack_rows(xs_ref).astype(BF16)
        h1 = _dot(xb, w1b[...])
        h3 = _dot(xb, w3b[...])
        a = (h1 * jax.nn.sigmoid(h1) * h3).astype(BF16)
        y = _pack_rows(_round_bf16(_dot(a, w2b[...]))[1])
        row = lax.broadcasted_iota(jnp.int32, y[0].shape, 0)
        mine = jnp.logical_and(row >= lo_ref[j], row < hi_ref[j])

        @pl.when(new_block)
        def _():
            for p, words in enumerate(y):
                y_ref[p] = jnp.where(mine, words, jnp.uint32(0))

        @pl.when(jnp.logical_not(new_block))
        def _():
            for p, words in enumerate(y):
                y_ref[p] = jnp.where(mine, words, y_ref[p])


def _experts(xs, items, w1, w3, w2, layer):
    pieces, P, pc = xs.shape
    _, E, D, DE = w1.shape
    bm = MOE_BLOCK
    n_items_max = P // bm + E
    rows = lambda j, *prefetch: (0, prefetch[0][j], 0)
    return pl.pallas_call(
        functools.partial(_expert_kernel, layer=layer),
        grid_spec=pltpu.PrefetchScalarGridSpec(
            num_scalar_prefetch=7,
            grid=(n_items_max,),
            in_specs=[
                pl.BlockSpec((pieces, bm, pc), rows),
                pl.BlockSpec(memory_space=pl.ANY),
                pl.BlockSpec(memory_space=pl.ANY),
                pl.BlockSpec(memory_space=pl.ANY),
            ],
            out_specs=pl.BlockSpec((pieces, bm, pc), rows),
            scratch_shapes=[
                pltpu.VMEM((2, D, DE), F32),
                pltpu.VMEM((2, D, DE), F32),
                pltpu.VMEM((2, DE, D), F32),
                pltpu.VMEM((D, DE), BF16),
                pltpu.VMEM((D, DE), BF16),
                pltpu.VMEM((DE, D), BF16),
                pltpu.SemaphoreType.DMA((2, 3)),
            ],
        ),
        out_shape=jax.ShapeDtypeStruct((pieces, P, pc), U32),
        compiler_params=_params(("arbitrary",)),
        name="moe_experts",
    )(*items, xs, w1, w3, w2)


def _combine_kernel(x1_ref, route_ref, mod_ref, fg_ref, y0_ref, y1_ref, o_ref):
    x2 = x1_ref[...] + mod_ref[0][5:6] * _moe_mix(route_ref, y0_ref, y1_ref)
    ms = jnp.mean(x2 * x2, axis=-1, keepdims=True)
    o_ref[...] = x2 * lax.rsqrt(ms + NORM_EPS) * fg_ref[...]


def _final_combine(x1, route, modl, final_g, yk, seq):
    N, D = x1.shape
    tm = ROW_TILE
    nt = N // tm
    tok = lambda i: (i, 0)
    return pl.pallas_call(
        _combine_kernel,
        grid=(nt,),
        in_specs=[
            pl.BlockSpec((tm, D), tok),
            pl.BlockSpec((tm, LANES), tok),
            pl.BlockSpec((1, 6, D), lambda i: ((i * tm) // seq, 0, 0)),
            pl.BlockSpec((1, D), lambda i: (0, 0)),
            pl.BlockSpec((yk.shape[0], tm, ROW_PIECE), lambda i: (0, i, 0)),
            pl.BlockSpec((yk.shape[0], tm, ROW_PIECE), lambda i: (0, nt + i, 0)),
        ],
        out_specs=pl.BlockSpec((tm, D), tok),
        out_shape=jax.ShapeDtypeStruct((N, D), F32),
        compiler_params=_params(("arbitrary",)),
        name="moe_combine",
    )(x1, route, modl, final_g, yk, yk)


def _dest_kernel(routet_ref, cnt_ref, d1_ref, d2_ref):
    tm = routet_ref.shape[1]
    cnt = cnt_ref[...]
    r = lax.broadcasted_iota(jnp.int32, (N_EXPERTS, N_EXPERTS), 0)
    c = lax.broadcasted_iota(jnp.int32, (N_EXPERTS, N_EXPERTS), 1)
    lower = (r >= c).astype(F32)
    end = jnp.dot(lower, cnt, precision=HIGHEST, preferred_element_type=F32)
    start = jnp.concatenate([end - cnt] * (tm // LANES), axis=1)
    expert = lax.broadcasted_iota(jnp.int32, (N_EXPERTS, tm), 0).astype(F32)
    rec = routet_ref[...]
    for row_id, row_rank, out in ((R_ID1, R_RANK1, d1_ref), (R_ID2, R_RANK2, d2_ref)):
        sel = expert == rec[row_id:row_id + 1, :]
        first = jnp.sum(jnp.where(sel, start, 0.0), axis=0, keepdims=True)
        out[0] = (first + rec[row_rank:row_rank + 1, :]).astype(jnp.int32)


def _destinations(route_t, cnt):
    N = route_t.shape[1]
    tm = 2048 if N % 2048 == 0 else MIX_SUB
    out = jax.ShapeDtypeStruct((N // tm, 1, tm), jnp.int32)
    d1, d2 = pl.pallas_call(
        _dest_kernel,
        grid=(N // tm,),
        in_specs=[pl.BlockSpec((8, tm), lambda i: (0, i)),
                  pl.BlockSpec((N_EXPERTS, LANES), lambda i: (0, 0))],
        out_specs=[pl.BlockSpec((1, 1, tm), lambda i: (i, 0, 0))] * 2,
        out_shape=[out, out],
        compiler_params=_params(("arbitrary",)),
        name="moe_dest",
    )(route_t, cnt)
    return d1.reshape(N), d2.reshape(N)


def _work_items(cnt, n_items_max):
    bm = MOE_BLOCK
    i32 = jnp.int32
    counts = cnt[:, 0].astype(i32)
    end = jnp.cumsum(counts)
    start = end - counts
    first = start // bm
    last = jnp.maximum(end - 1, 0) // bm
    per_e = jnp.where(counts > 0, last - first + 1, 0)
    item_end = jnp.cumsum(per_e)
    item_start = item_end - per_e
    n_items = item_end[-1]
    j = jnp.minimum(jnp.arange(n_items_max, dtype=i32), n_items - 1)
    ie = jnp.minimum(jnp.sum((item_end[None, :] <= j[:, None]).astype(i32), axis=1), N_EXPERTS - 1)
    onehot = ie[:, None] == jnp.arange(N_EXPERTS, dtype=i32)[None, :]
    pick = lambda tbl: jnp.sum(jnp.where(onehot, tbl[None, :], 0), axis=1)
    ib = pick(first) + (j - pick(item_start))
    lo = jnp.maximum(pick(start), ib * bm) - ib * bm
    hi = jnp.minimum(pick(end), (ib + 1) * bm) - ib * bm
    change = jnp.concatenate([jnp.ones((1,), i32), (ie[1:] != ie[:-1]).astype(i32)])
    order = jnp.cumsum(change) - 1
    follows = jnp.logical_and(order[None, :] == order[:, None] + 1, change[None, :] == 1)
    nxt = jnp.where(jnp.any(follows, axis=1), jnp.sum(jnp.where(follows, ie[None, :], 0), axis=1), -1)
    return (ib.astype(i32), ie.astype(i32), lo.astype(i32), hi.astype(i32),
            n_items.reshape(1).astype(i32), nxt.astype(i32), (order % 2).astype(i32))


def kernel(x, c, mod_w, mod_b, norm1_g, w_in, gate_w2, gate_b, gla_norm_g, conv_w, w_gla_out,
           w_conv_out, w_out, norm2_g, router_group_w, router_group_b, router_expert_w,
           router_expert_b, expert_w1, expert_w3, expert_w2, final_norm_g):
    B, S, D = x.shape
    L = mod_w.shape[0]
    N = B * S
    dk = gate_w2.shape[2]
    rank = gate_w2.shape[1]
    dv = gla_norm_g.shape[1]
    n_slots = N * 2
    n_items_max = n_slots // MOE_BLOCK + N_EXPERTS
    assert S % GLA_TILE == 0 and S % MIX_TILE == 0 and S % IN_TILE == 0 and N % ROW_TILE == 0
    assert n_slots % MOE_BLOCK == 0
    assert N_GROUPS + N_EXPERTS <= LANES and EXPERTS_PER_GROUP == 8 and 3 * rank <= LANES and D % (2 * ROW_PIECE) == 0

    mod = _modulation(c, mod_w, mod_b)
    xf = x.reshape(N, D)
    o_gd = 2 * dk + dv
    o_r = o_gd + rank
    o_b = o_r + dv
    for l in range(L):
        modl = mod[l].reshape(B, 6, D)
        wl = w_in[l]
        wa = jnp.concatenate([wl[:, :o_gd], wl[:, o_r:o_b]], axis=1).astype(BF16)
        wb = wl[:, o_b:].astype(BF16)
        wgd = wl[:, o_gd:o_r]
        wg = jnp.pad(jnp.concatenate([wgd, wgd, wgd], axis=1), ((0, 0), (0, LANES - 3 * rank))).astype(BF16)
        g_hi = gate_w2[l].astype(BF16)
        g_lo = (gate_w2[l] - g_hi.astype(F32)).astype(BF16)
        gw2 = jnp.pad(jnp.concatenate([g_hi, g_hi, g_lo], axis=0), ((0, LANES - 3 * rank), (0, 0)))
        if l == 0:
            za, zb, zg = _in_projection(xf, modl, norm1_g[l].reshape(1, D), conv_w[l], wa, wb, wg, S)
        else:
            xf, za, zb, zg = _in_projection(x1, modl, norm1_g[l].reshape(1, D), conv_w[l], wa, wb, wg, S,
                                            combine=(route, yk, mod[l - 1].reshape(B, 6, D)))
        og = _gla(za, zg, gw2, gate_b[l].reshape(1, dk), gla_norm_g[l].reshape(1, dv), B, S, dk, dv, rank)
        wr = jnp.pad(jnp.concatenate([router_group_w[l], router_expert_w[l]], axis=1),
                     ((0, 0), (0, LANES - N_GROUPS - N_EXPERTS)))
        wr_hi = wr.astype(BF16)
        wr = jnp.concatenate([wr_hi, (wr - wr_hi.astype(F32)).astype(BF16)], axis=1)
        br = jnp.pad(jnp.concatenate([router_group_b[l], router_expert_b[l]]),
                     (0, LANES - N_GROUPS - N_EXPERTS)).reshape(1, LANES)
        x1, h2, logits = _mixer_out(
            og, zb, xf, modl, w_gla_out[l].astype(BF16), w_conv_out[l].astype(BF16),
            w_out[l].astype(BF16), norm2_g[l].reshape(1, D), wr, br, B, S)
        route_t, route, cnt = _routing(logits)
        dest1, dest2 = _destinations(route_t, cnt)
        items = _work_items(cnt, n_items_max)
        xs = _sc_dispatch(h2, dest1, dest2, n_slots)
        yb = _experts(xs, items, expert_w1, expert_w3, expert_w2, l)
        yk = _sc_return(yb, jnp.concatenate([dest1, dest2]))
    out = _final_combine(x1, route, modl, final_norm_g.reshape(1, D), yk, S)
    return out.reshape(B, S, D)
```

```python
import functools

import jax
import jax.numpy as jnp
from jax import lax
from jax.experimental import pallas as pl
from jax.experimental.pallas import tpu as pltpu
from jax.experimental.pallas import tpu_sc as plsc

F32 = jnp.float32
BF16 = jnp.bfloat16
HIGHEST = lax.Precision.HIGHEST

GLA_HEADS = 4
GATE_TAU = 16.0
GLA_CHUNK = 64
N_GROUPS = 8
EXPERTS_PER_GROUP = 8
N_EXPERTS = N_GROUPS * EXPERTS_PER_GROUP
NORM_EPS = 1e-6

LANES = 128
VMEM_LIMIT_BYTES = 56 * 1024 * 1024

IN_TILE = 512
GLA_TILE = 512
MIX_TILE = 512
MIX_SUB = 256
MOE_BLOCK = 512
ROW_TILE = 1024
ROW_PIECE = 256
U32 = jnp.uint32

R_ID1, R_ID2, R_W1, R_W2, R_RANK1, R_RANK2 = 0, 1, 2, 3, 4, 5


def _dot(a, b):
    return jnp.dot(a, b, preferred_element_type=F32)


def _round_bf16(x):
    xb = x.astype(BF16)
    return xb, xb.astype(F32)


def _pack_rows(xr):
    half = xr.shape[1] // 2
    out = []
    for p in range(half // ROW_PIECE):
        lo = xr[:, p * ROW_PIECE:(p + 1) * ROW_PIECE]
        hi = xr[:, half + p * ROW_PIECE:half + (p + 1) * ROW_PIECE]
        out.append((pltpu.bitcast(lo, U32) >> 16) | pltpu.bitcast(hi, U32))
    return out


def _unpack_rows(ref):
    words = [ref[p] for p in range(ref.shape[0])]
    lo = [pltpu.bitcast(w << 16, F32) for w in words]
    hi = [pltpu.bitcast(w & jnp.uint32(0xFFFF0000), F32) for w in words]
    return jnp.concatenate(lo + hi, axis=-1)


def _const_spec(shape):
    nd = len(shape)
    return pl.BlockSpec(shape, lambda *_: (0,) * nd, pipeline_mode=pl.Buffered(1))


def _params(sem):
    return pltpu.CompilerParams(dimension_semantics=sem, vmem_limit_bytes=VMEM_LIMIT_BYTES)


def _mod_kernel(c_ref, w_ref, b_ref, o_ref):
    c = c_ref[...]
    sc = c * jax.nn.sigmoid(c)
    o_ref[0] = jnp.dot(sc, w_ref[0], precision=HIGHEST, preferred_element_type=F32) + b_ref[0]


def _modulation(c, mod_w, mod_b):
    L, D, D6 = mod_w.shape
    B = c.shape[0]
    nj = D6 // D
    return pl.pallas_call(
        _mod_kernel,
        grid=(L, nj),
        in_specs=[
            pl.BlockSpec((B, D), lambda l, j: (0, 0)),
            pl.BlockSpec((1, D, D), lambda l, j: (l, 0, j)),
            pl.BlockSpec((1, 1, D), lambda l, j: (l, 0, j)),
        ],
        out_specs=pl.BlockSpec((1, B, D), lambda l, j: (l, 0, j)),
        out_shape=jax.ShapeDtypeStruct((L, B, D6), F32),
        compiler_params=_params(("arbitrary", "arbitrary")),
        name="adaln_mod",
    )(c, mod_w, mod_b.reshape(L, 1, D6))


CONV_COLS = 256


def _moe_mix(route_ref, y0_ref, y1_ref):
    rec = route_ref[...]
    w1 = rec[:, R_W1:R_W1 + 1]
    w2 = rec[:, R_W2:R_W2 + 1]
    return w1 * _unpack_rows(y0_ref) + w2 * _unpack_rows(y1_ref)


def _inproj_kernel(*refs, seq, fuse_combine):
    if fuse_combine:
        (x_ref, route_ref, y0_ref, y1_ref, modp_ref, mod_ref, g_ref, cw_ref, wa_ref, wb_ref, wg_ref,
         xo_ref, za_ref, zb_ref, zg_ref, carry_ref) = refs
        x = x_ref[...] + modp_ref[0][5:6] * _moe_mix(route_ref, y0_ref, y1_ref)
        xo_ref[...] = x
    else:
        (x_ref, mod_ref, g_ref, cw_ref, wa_ref, wb_ref, wg_ref, za_ref, zb_ref, zg_ref, carry_ref) = refs
        x = x_ref[...]
    tm, d = x.shape

    @pl.when((pl.program_id(0) * tm) % seq == 0)
    def _():
        carry_ref[...] = jnp.zeros_like(carry_ref)

    ms = jnp.mean(x * x, axis=-1, keepdims=True)
    m = mod_ref[0]
    h = (x * lax.rsqrt(ms + NORM_EPS) * (g_ref[...] * (1.0 + m[1:2])) + m[0:1]).astype(BF16)
    for j in range(za_ref.shape[1] // d):
        za_ref[:, j * d:(j + 1) * d] = _dot(h, wa_ref[:, j * d:(j + 1) * d]).astype(BF16)
    zg_ref[...] = _dot(h, wg_ref[...])
    for j in range(1, zb_ref.shape[1] // d):
        zb_ref[:, j * d:(j + 1) * d] = _dot(h, wb_ref[:, (j + 2) * d:(j + 3) * d]).astype(BF16)

    w = CONV_COLS
    rowi = lax.broadcasted_iota(jnp.int32, (tm, w), 0)
    for j in range(d // w):
        cols = slice(j * w, (j + 1) * w)
        cb = _dot(h, wb_ref[:, j * w:(j + 1) * w])
        cc = _dot(h, wb_ref[:, d + j * w:d + (j + 1) * w])
        ch = _dot(h, wb_ref[:, 2 * d + j * w:2 * d + (j + 1) * w])
        u = cc * ch
        prev = carry_ref[:, cols]
        u1 = jnp.where(rowi == 0, prev[7:8], pltpu.roll(u, 1, 0))
        u2 = jnp.where(rowi == 0, prev[6:7], jnp.where(rowi == 1, prev[7:8], pltpu.roll(u, 2, 0)))
        carry_ref[:, cols] = u[tm - 8:tm]
        conv = cw_ref[0:1, cols] * u2 + cw_ref[1:2, cols] * u1 + cw_ref[2:3, cols] * u
        zb_ref[:, cols] = (cb * conv).astype(BF16)


def _in_projection(x, modl, norm_g, conv_w, wa, wb, wg, seq, combine=None):
    N, D = x.shape
    tm = IN_TILE
    nt = N // tm
    ca, cb = wa.shape[1], wb.shape[1] - 2 * D
    tok = lambda i: (i, 0)
    per_batch = lambda i: ((i * tm) // seq, 0, 0)
    in_specs = [pl.BlockSpec((tm, D), tok)]
    args = [x]
    out_specs, out_shape = [], []
    if combine is not None:
        route, yk, mod_prev = combine
        piece_blk = (yk.shape[0], tm, ROW_PIECE)
        in_specs += [
            pl.BlockSpec((tm, LANES), tok),
            pl.BlockSpec(piece_blk, lambda i: (0, i, 0)),
            pl.BlockSpec(piece_blk, lambda i: (0, nt + i, 0)),
            pl.BlockSpec((1, 6, D), per_batch),
        ]
        args += [route, yk, yk, mod_prev]
        out_specs.append(pl.BlockSpec((tm, D), tok))
        out_shape.append(jax.ShapeDtypeStruct((N, D), F32))
    in_specs += [
        pl.BlockSpec((1, 6, D), per_batch),
        _const_spec((1, D)),
        _const_spec(conv_w.shape),
        _const_spec((D, ca)),
        _const_spec(wb.shape),
        _const_spec((D, LANES)),
    ]
    args += [modl, norm_g, conv_w, wa, wb, wg]
    out_specs += [
        pl.BlockSpec((tm, ca), tok),
        pl.BlockSpec((tm, cb), tok),
        pl.BlockSpec((tm, LANES), tok),
    ]
    out_shape += [
        jax.ShapeDtypeStruct((N, ca), BF16),
        jax.ShapeDtypeStruct((N, cb), BF16),
        jax.ShapeDtypeStruct((N, LANES), F32),
    ]
    kern = functools.partial(_inproj_kernel, seq=seq, fuse_combine=combine is not None)
    return pl.pallas_call(
        kern,
        grid=(nt,),
        in_specs=in_specs,
        out_specs=out_specs,
        out_shape=out_shape,
        scratch_shapes=[pltpu.VMEM((8, D), F32)],
        compiler_params=_params(("arbitrary",)),
        name="in_projection",
    )(*args)


def _log_sigmoid(x):
    return jnp.minimum(x, 0.0) - jnp.log(1.0 + jnp.exp(-jnp.abs(x)))


def _gla_kernel(za_ref, zg_ref, gw2_ref, gb_ref, ng_ref, o_ref, st_ref, lg_ref, *, dk, dv, rank):
    heads = GLA_HEADS
    dkh, dvh = dk // heads, dv // heads
    c = GLA_CHUNK
    ts = za_ref.shape[0]

    @pl.when(pl.program_id(1) == 0)
    def _():
        st_ref[...] = jnp.zeros_like(st_ref)

    zg = zg_ref[...]
    zg_hi = zg.astype(BF16)
    zg_lo = (zg - zg_hi.astype(F32)).astype(BF16)
    lane = lax.broadcasted_iota(jnp.int32, zg.shape, 1)
    lhs = jnp.where(jnp.logical_and(lane >= rank, lane < 2 * rank), zg_lo, zg_hi)
    pre = _dot(lhs, gw2_ref[...]) + gb_ref[...]
    lg = _log_sigmoid(pre) * (1.0 / GATE_TAU)
    lg_hi = lg.astype(BF16)
    lg_ref[:, 0:dk] = lg_hi
    lg_ref[:, dk:2 * dk] = (lg - lg_hi.astype(F32)).astype(BF16)

    row = lax.broadcasted_iota(jnp.int32, (c, c), 0)
    col = lax.broadcasted_iota(jnp.int32, (c, c), 1)
    causal = row >= col
    tril = causal.astype(BF16)
    qscale = dkh ** -0.5

    for ci in range(ts // c):
        rows = slice(ci * c, (ci + 1) * c)
        b_two = _dot(tril, lg_ref[rows, :])
        b_all = b_two[:, 0:dk] + b_two[:, dk:2 * dk]
        for hd in range(heads):
            ks = slice(hd * dkh, (hd + 1) * dkh)
            q = za_ref[rows, hd * dkh:(hd + 1) * dkh].astype(F32) * qscale
            k = za_ref[rows, dk + hd * dkh:dk + (hd + 1) * dkh].astype(F32)
            v = za_ref[rows, 2 * dk + hd * dvh:2 * dk + (hd + 1) * dvh]
            r = za_ref[rows, 2 * dk + dv + hd * dvh:2 * dk + dv + (hd + 1) * dvh].astype(F32)
            b = b_all[:, ks]
            b_last = b[c - 1:c, :]
            q_t = (q * jnp.exp(b)).astype(BF16)
            k_t = (k * jnp.exp(-b)).astype(BF16)
            k_s = (k * jnp.exp(b_last - b)).astype(BF16)
            decay = jnp.exp(b_last)
            attn = lax.dot_general(q_t, k_t, (((1,), (1,)), ((), ())), preferred_element_type=F32)
            attn = jnp.where(causal, attn, 0.0).astype(BF16)
            st = st_ref[hd]
            o = _dot(attn, v) + lax.dot_general(
                q_t, st.astype(BF16), (((1,), (1,)), ((), ())), preferred_element_type=F32)
            upd = lax.dot_general(v, k_s, (((0,), (0,)), ((), ())), preferred_element_type=F32)
            st_ref[hd] = st * decay + upd
            ms = jnp.mean(o * o, axis=-1, keepdims=True)
            on = o * lax.rsqrt(ms + NORM_EPS) * ng_ref[:, hd * dvh:(hd + 1) * dvh]
            o_ref[rows, hd * dvh:(hd + 1) * dvh] = (on * (r * jax.nn.sigmoid(r))).astype(BF16)


def _gla(za, zg, gw2, gb, ng, batch, seq, dk, dv, rank):
    N = za.shape[0]
    ts = GLA_TILE
    ns = seq // ts
    heads = GLA_HEADS
    kern = functools.partial(_gla_kernel, dk=dk, dv=dv, rank=rank)
    return pl.pallas_call(
        kern,
        grid=(batch, ns),
        in_specs=[
            pl.BlockSpec((ts, za.shape[1]), lambda b, s: (b * ns + s, 0)),
            pl.BlockSpec((ts, LANES), lambda b, s: (b * ns + s, 0)),
            _const_spec((LANES, dk)),
            _const_spec((1, dk)),
            _const_spec((1, dv)),
        ],
        out_specs=pl.BlockSpec((ts, dv), lambda b, s: (b * ns + s, 0)),
        out_shape=jax.ShapeDtypeStruct((N, dv), BF16),
        scratch_shapes=[
            pltpu.VMEM((heads, dv // heads, dk // heads), F32),
            pltpu.VMEM((ts, 2 * dk), BF16),
        ],
        compiler_params=_params(("arbitrary", "arbitrary")),
        name="gla",
    )(za, zg, gw2, gb, ng)


def _mixout_kernel(og_ref, zb_ref, x_ref, mod_ref, wga_ref, wco_ref, wo_ref, n2_ref,
                   wr_ref, br_ref, x1_ref, h2_ref, logit_ref):
    tm, d = x_ref.shape
    m = mod_ref[0]
    sub = MIX_SUB
    gain2 = n2_ref[...] * (1.0 + m[4:5])
    for r0 in range(0, tm, sub):
        rs = slice(r0, r0 + sub)
        ga = zb_ref[rs, d:2 * d].astype(F32)
        gc = zb_ref[rs, 2 * d:3 * d].astype(F32)
        y_conv = _dot(zb_ref[rs, 0:d], wco_ref[...])
        y_gla = _dot(og_ref[rs, :], wga_ref[...])
        y = jax.nn.sigmoid(ga) * y_gla + jax.nn.sigmoid(gc) * y_conv
        y = _dot(y.astype(BF16), wo_ref[...])
        x1 = x_ref[rs, :] + m[2:3] * y
        x1_ref[rs, :] = x1

        ms = jnp.mean(x1 * x1, axis=-1, keepdims=True)
        h2 = x1 * lax.rsqrt(ms + NORM_EPS) * gain2 + m[3:4]
        h_hi, h_r = _round_bf16(h2)
        for p, words in enumerate(_pack_rows(h_r)):
            h2_ref[p, rs, :] = words

        h_lo = (h2 - h_r).astype(BF16)
        two = _dot(h_hi, wr_ref[...])
        logits = two[:, 0:LANES] + two[:, LANES:2 * LANES] + _dot(h_lo, wr_ref[:, 0:LANES]) + br_ref[...]
        logit_ref[rs, :] = logits


def _mixer_out(og, zb, x, modl, wga, wco, wo, n2g, wr, br, batch, seq):
    N, D = x.shape
    tm = MIX_TILE
    ns = seq // tm
    tok = lambda b, s: (b * ns + s, 0)
    return pl.pallas_call(
        _mixout_kernel,
        grid=(batch, ns),
        in_specs=[
            pl.BlockSpec((tm, D), tok),
            pl.BlockSpec((tm, zb.shape[1]), tok),
            pl.BlockSpec((tm, D), tok),
            pl.BlockSpec((1, 6, D), lambda b, s: (b, 0, 0)),
            _const_spec((D, D)),
            _const_spec((D, D)),
            _const_spec((D, D)),
            _const_spec((1, D)),
            _const_spec((D, 2 * LANES)),
            _const_spec((1, LANES)),
        ],
        out_specs=[
            pl.BlockSpec((tm, D), tok),
            pl.BlockSpec((D // 2 // ROW_PIECE, tm, ROW_PIECE), lambda b, s: (0, b * ns + s, 0)),
            pl.BlockSpec((tm, LANES), tok),
        ],
        out_shape=[
            jax.ShapeDtypeStruct((N, D), F32),
            jax.ShapeDtypeStruct((D // 2 // ROW_PIECE, N, ROW_PIECE), U32),
            jax.ShapeDtypeStruct((N, LANES), F32),
        ],
        compiler_params=_params(("arbitrary", "arbitrary")),
        name="mixer_out",
    )(og, zb, x, modl, wga, wco, wo, n2g, wr, br)


def _route_kernel(logit_ref, routet_ref, route_ref, cnt_ref, run_ref):
    tm = logit_ref.shape[0]
    sub = MIX_SUB
    rows8 = EXPERTS_PER_GROUP

    @pl.when(pl.program_id(0) == 0)
    def _():
        run_ref[...] = jnp.zeros_like(run_ref)

    sub8 = lax.broadcasted_iota(jnp.int32, (rows8, sub), 0)
    erow = lax.broadcasted_iota(jnp.int32, (N_EXPERTS, sub), 0)
    tr = lax.broadcasted_iota(jnp.int32, (sub, sub), 0)
    tc = lax.broadcasted_iota(jnp.int32, (sub, sub), 1)
    earlier = (tr < tc).astype(BF16)
    ones = jnp.ones((sub, LANES), BF16)
    neg = -jnp.inf
    run = run_ref[...]
    for r0 in range(0, tm, sub):
        lt = logit_ref[r0:r0 + sub, :].T
        gl = lt[0:N_GROUPS, :]
        gmax = jnp.max(gl, axis=0, keepdims=True)
        gsum = jnp.sum(jnp.exp(gl - gmax), axis=0, keepdims=True)
        g_w = 1.0 / gsum
        g_idx = jnp.min(jnp.where(gl == gmax, sub8, N_GROUPS), axis=0, keepdims=True)
        el = lt[N_GROUPS:N_GROUPS + rows8, :]
        for g in range(1, N_GROUPS):
            el = jnp.where(g_idx == g, lt[N_GROUPS + g * rows8:N_GROUPS + (g + 1) * rows8, :], el)
        e1 = jnp.max(el, axis=0, keepdims=True)
        i1 = jnp.min(jnp.where(el == e1, sub8, rows8), axis=0, keepdims=True)
        el2 = jnp.where(sub8 == i1, neg, el)
        e2 = jnp.max(el2, axis=0, keepdims=True)
        i2 = jnp.min(jnp.where(el2 == e2, sub8, rows8), axis=0, keepdims=True)
        ratio = jnp.exp(e2 - e1)
        w1 = g_w / (1.0 + ratio)
        w2 = g_w * ratio / (1.0 + ratio)
        id1 = g_idx * rows8 + i1
        id2 = g_idx * rows8 + i2

        oh1 = erow == id1
        oh2 = erow == id2
        oh1b = jnp.where(oh1, 1.0, 0.0).astype(BF16)
        oh2b = jnp.where(oh2, 1.0, 0.0).astype(BF16)
        tot1 = _dot(oh1b, ones)
        tot2 = _dot(oh2b, ones)
        base1 = jnp.concatenate([run] * (sub // LANES), axis=1)
        base2 = jnp.concatenate([run + tot1] * (sub // LANES), axis=1)
        c1 = _dot(oh1b, earlier) + base1
        c2 = _dot(oh2b, earlier) + base2
        rank1 = jnp.sum(jnp.where(oh1, c1, 0.0), axis=0, keepdims=True)
        rank2 = jnp.sum(jnp.where(oh2, c2, 0.0), axis=0, keepdims=True)
        run = run + tot1 + tot2

        rec = jnp.where(sub8 == R_ID1, id1.astype(F32), 0.0)
        rec = jnp.where(sub8 == R_ID2, id2.astype(F32), rec)
        rec = jnp.where(sub8 == R_W1, w1, rec)
        rec = jnp.where(sub8 == R_W2, w2, rec)
        rec = jnp.where(sub8 == R_RANK1, rank1, rec)
        rec = jnp.where(sub8 == R_RANK2, rank2, rec)
        routet_ref[:, r0:r0 + sub] = rec
        rec_full = jnp.concatenate([rec, jnp.zeros((LANES - rows8, sub), F32)], axis=0)
        route_ref[r0:r0 + sub, :] = rec_full.T
    run_ref[...] = run
    cnt_ref[...] = run


def _routing(logits):
    N = logits.shape[0]
    tm = 1024 if N % 1024 == 0 else MIX_SUB
    return pl.pallas_call(
        _route_kernel,
        grid=(N // tm,),
        in_specs=[pl.BlockSpec((tm, LANES), lambda i: (i, 0))],
        out_specs=[
            pl.BlockSpec((8, tm), lambda i: (0, i)),
            pl.BlockSpec((tm, LANES), lambda i: (i, 0)),
            pl.BlockSpec((N_EXPERTS, LANES), lambda i: (0, 0)),
        ],
        out_shape=[
            jax.ShapeDtypeStruct((8, N), F32),
            jax.ShapeDtypeStruct((N, LANES), F32),
            jax.ShapeDtypeStruct((N_EXPERTS, LANES), F32),
        ],
        scratch_shapes=[pltpu.VMEM((N_EXPERTS, LANES), F32)],
        compiler_params=_params(("arbitrary",)),
        name="moe_route",
    )(logits)


SC_WINDOW = 128


def _sc_mesh():
    return plsc.VectorSubcoreMesh(core_axis_name="core", subcore_axis_name="subcore")


def _piece_index(rows, pieces, n_rows):
    return (jnp.arange(pieces, dtype=jnp.int32)[:, None] * n_rows + rows[None, :]).reshape(1, -1)


def _sc_dispatch(h2, dests, n_rows):
    pieces, N, pc = h2.shape
    w = SC_WINDOW

    @functools.partial(pl.kernel, out_type=jax.ShapeDtypeStruct((pieces * n_rows, pc), h2.dtype),
                       mesh=_sc_mesh(), scratch_types=[], name="moe_dispatch_sc")
    def run(x_hbm, *refs):
        o_hbm = refs[-1]

        def body(x_vmem, *idx_vmem):
            for i_vmem in idx_vmem:
                pltpu.sync_copy(x_vmem, o_hbm.at[i_vmem.at[0]])

        pltpu.emit_pipeline(
            body,
            grid=(pieces * N // w,),
            in_specs=[pl.BlockSpec((w, pc), lambda i: (i, 0))]
            + [pl.BlockSpec((1, w), lambda i: (0, i))] * len(dests),
            out_specs=[],
            core_axis_name=("core", "subcore"),
            dimension_semantics=(pltpu.PARALLEL,),
        )(x_hbm, *refs[:-1])

    xs = run(h2.reshape(pieces * N, pc), *[_piece_index(d, pieces, n_rows) for d in dests])
    return xs.reshape(pieces, n_rows, pc)


def _sc_return(yb, dest):
    M = dest.shape[0]
    pieces, P, pc = yb.shape
    w = SC_WINDOW

    @functools.partial(pl.kernel, out_type=jax.ShapeDtypeStruct((pieces * M, pc), yb.dtype),
                       mesh=_sc_mesh(), scratch_types=[], name="moe_return_sc")
    def run(y_hbm, i_hbm, o_hbm):
        def body(i_vmem, o_vmem):
            pltpu.sync_copy(y_hbm.at[i_vmem.at[0]], o_vmem)

        pltpu.emit_pipeline(
            body,
            grid=(pieces * M // w,),
            in_specs=[pl.BlockSpec((1, w), lambda i: (0, i))],
            out_specs=[pl.BlockSpec((w, pc), lambda i: (i, 0))],
            core_axis_name=("core", "subcore"),
            dimension_semantics=(pltpu.PARALLEL,),
        )(i_hbm, o_hbm)

    return run(yb.reshape(pieces * P, pc), _piece_index(dest, pieces, P)).reshape(pieces, M, pc)


def _expert_kernel(ib_ref, ie_ref, nxt_ref, slot_ref, nb_ref, xs_ref, w1_ref, w3_ref, w2_ref, y_ref,
                   wf1, wf3, wf2, w1b, w3b, w2b, sem, *, layer):
    j = pl.program_id(0)
    jp = jnp.maximum(j - 1, 0)
    live = j < nb_ref[0]
    new_expert = jnp.logical_or(j == 0, ie_ref[j] != ie_ref[jp])
    streams = ((w1_ref, wf1), (w3_ref, wf3), (w2_ref, wf2))

    def weight_copies(expert, s):
        return [pltpu.make_async_copy(w.at[layer, expert], buf.at[s], sem.at[s, k])
                for k, (w, buf) in enumerate(streams)]

    @pl.when(j == 0)
    def _():
        for cp in weight_copies(ie_ref[0], 0):
            cp.start()

    @pl.when(jnp.logical_and(new_expert, live))
    def _():
        s = slot_ref[j]
        for cp in weight_copies(ie_ref[j], s):
            cp.wait()
        w1b[...] = wf1[s].astype(BF16)
        w3b[...] = wf3[s].astype(BF16)
        w2b[...] = wf2[s].astype(BF16)

        @pl.when(nxt_ref[j] >= 0)
        def _():
            for cp in weight_copies(nxt_ref[j], 1 - s):
                cp.start()

    @pl.when(live)
    def _():
        xb = _unpack_rows(xs_ref).astype(BF16)
        h1 = _dot(xb, w1b[...])
        h3 = _dot(xb, w3b[...])
        a = (h1 * jax.nn.sigmoid(h1) * h3).astype(BF16)
        for p, words in enumerate(_pack_rows(_round_bf16(_dot(a, w2b[...]))[1])):
            y_ref[p] = words


def _experts(xs, items, w1, w3, w2, layer):
    pieces, P, pc = xs.shape
    _, E, D, DE = w1.shape
    bm = MOE_BLOCK
    n_items_max = P // bm
    rows = lambda j, *prefetch: (0, prefetch[0][j], 0)
    return pl.pallas_call(
        functools.partial(_expert_kernel, layer=layer),
        grid_spec=pltpu.PrefetchScalarGridSpec(
            num_scalar_prefetch=5,
            grid=(n_items_max,),
            in_specs=[
                pl.BlockSpec((pieces, bm, pc), rows),
                pl.BlockSpec(memory_space=pl.ANY),
                pl.BlockSpec(memory_space=pl.ANY),
                pl.BlockSpec(memory_space=pl.ANY),
            ],
            out_specs=pl.BlockSpec((pieces, bm, pc), rows),
            scratch_shapes=[
                pltpu.VMEM((2, D, DE), F32),
                pltpu.VMEM((2, D, DE), F32),
                pltpu.VMEM((2, DE, D), F32),
                pltpu.VMEM((D, DE), BF16),
                pltpu.VMEM((D, DE), BF16),
                pltpu.VMEM((DE, D), BF16),
                pltpu.SemaphoreType.DMA((2, 3)),
            ],
        ),
        out_shape=jax.ShapeDtypeStruct((pieces, P, pc), U32),
        compiler_params=_params(("arbitrary",)),
        name="moe_experts",
    )(*items, xs, w1, w3, w2)


def _combine_kernel(x1_ref, route_ref, mod_ref, fg_ref, y0_ref, y1_ref, o_ref):
    x2 = x1_ref[...] + mod_ref[0][5:6] * _moe_mix(route_ref, y0_ref, y1_ref)
    ms = jnp.mean(x2 * x2, axis=-1, keepdims=True)
    o_ref[...] = x2 * lax.rsqrt(ms + NORM_EPS) * fg_ref[...]


def _final_combine(x1, route, modl, final_g, yk, seq):
    N, D = x1.shape
    tm = ROW_TILE
    nt = N // tm
    tok = lambda i: (i, 0)
    return pl.pallas_call(
        _combine_kernel,
        grid=(nt,),
        in_specs=[
            pl.BlockSpec((tm, D), tok),
            pl.BlockSpec((tm, LANES), tok),
            pl.BlockSpec((1, 6, D), lambda i: ((i * tm) // seq, 0, 0)),
            pl.BlockSpec((1, D), lambda i: (0, 0)),
            pl.BlockSpec((yk.shape[0], tm, ROW_PIECE), lambda i: (0, i, 0)),
            pl.BlockSpec((yk.shape[0], tm, ROW_PIECE), lambda i: (0, nt + i, 0)),
        ],
        out_specs=pl.BlockSpec((tm, D), tok),
        out_shape=jax.ShapeDtypeStruct((N, D), F32),
        compiler_params=_params(("arbitrary",)),
        name="moe_combine",
    )(x1, route, modl, final_g, yk, yk)


ITEM_LANES = 256
I_BLOCK, I_EXPERT, I_NEXT, I_SLOT, I_COUNT = 0, 1, 2, 3, 4


def _dest_kernel(routet_ref, cnt_ref, d1_ref, d2_ref, d3_ref, items_ref):
    tm = routet_ref.shape[1]
    ne = N_EXPERTS
    bm = float(MOE_BLOCK)
    cnt = cnt_ref[...]
    r = lax.broadcasted_iota(jnp.int32, (ne, ne), 0)
    c = lax.broadcasted_iota(jnp.int32, (ne, ne), 1)
    lower = (r >= c).astype(F32)
    cumsum = lambda a: jnp.dot(lower, a, precision=HIGHEST, preferred_element_type=F32)
    padded = jnp.floor((cnt + (bm - 1.0)) * (1.0 / bm)) * bm
    pend = cumsum(padded)
    pstart = pend - padded
    npad = padded - cnt
    pad_end = cumsum(npad)
    pad_start = pad_end - npad

    tile = lambda a, n: jnp.concatenate([a] * (n // LANES), axis=1)
    expert = lax.broadcasted_iota(jnp.int32, (ne, tm), 0).astype(F32)
    rec = routet_ref[...]
    pstart_t = tile(pstart, tm)

    def sorted_row(row_id, row_rank):
        sel = expert == rec[row_id:row_id + 1, :]
        return jnp.sum(jnp.where(sel, pstart_t, 0.0), axis=0, keepdims=True) + rec[row_rank:row_rank + 1, :]

    dest1 = sorted_row(R_ID1, R_RANK1)
    dest2 = sorted_row(R_ID2, R_RANK2)
    t = (lax.broadcasted_iota(jnp.int32, (1, tm), 1) + pl.program_id(0) * tm).astype(F32)
    ps_t, pe_t = tile(pad_start, tm), tile(pad_end, tm)
    in_e = jnp.logical_and(ps_t <= t, t < pe_t)
    pad_row = jnp.sum(jnp.where(in_e, tile(pstart + cnt, tm) + (t - ps_t), 0.0), axis=0, keepdims=True)
    dest3 = jnp.where(t < pe_t[ne - 1:ne, :], pad_row, dest1)
    d1_ref[0] = dest1.astype(jnp.int32)
    d2_ref[0] = dest2.astype(jnp.int32)
    d3_ref[0] = dest3.astype(jnp.int32)

    @pl.when(pl.program_id(0) == 0)
    def _():
        nl = ITEM_LANES
        pend_i = tile(pend, nl)
        n_items = pend_i[ne - 1:ne, :] * (1.0 / bm)
        blk = jnp.minimum(lax.broadcasted_iota(jnp.int32, (1, nl), 1).astype(F32), n_items - 1.0)
        erow = lax.broadcasted_iota(jnp.int32, (ne, nl), 0).astype(F32)
        ie = jnp.sum(jnp.where(pend_i <= blk * bm, 1.0, 0.0), axis=0, keepdims=True)
        nonempty = tile(cnt, nl) > 0.0
        order = jnp.sum(jnp.where(jnp.logical_and(nonempty, erow < ie), 1.0, 0.0), axis=0, keepdims=True)
        nxt = jnp.min(jnp.where(jnp.logical_and(nonempty, erow > ie), erow, float(ne)), axis=0, keepdims=True)
        nxt = jnp.where(nxt == float(ne), -1.0, nxt)
        slot = order - 2.0 * jnp.floor(order * 0.5)
        row = lax.broadcasted_iota(jnp.int32, (8, nl), 0)
        tab = jnp.where(row == I_BLOCK, blk, 0.0)
        tab = jnp.where(row == I_EXPERT, ie, tab)
        tab = jnp.where(row == I_NEXT, nxt, tab)
        tab = jnp.where(row == I_SLOT, slot, tab)
        tab = jnp.where(row == I_COUNT, n_items, tab)
        items_ref[...] = tab.astype(jnp.int32)


def _destinations(route_t, cnt, n_blocks):
    N = route_t.shape[1]
    tm = 2048 if N % 2048 == 0 else MIX_SUB
    out = jax.ShapeDtypeStruct((N // tm, 1, tm), jnp.int32)
    d1, d2, d3, items = pl.pallas_call(
        _dest_kernel,
        grid=(N // tm,),
        in_specs=[pl.BlockSpec((8, tm), lambda i: (0, i)),
                  pl.BlockSpec((N_EXPERTS, LANES), lambda i: (0, 0))],
        out_specs=[pl.BlockSpec((1, 1, tm), lambda i: (i, 0, 0))] * 3
        + [pl.BlockSpec((8, ITEM_LANES), lambda i: (0, 0))],
        out_shape=[out, out, out, jax.ShapeDtypeStruct((8, ITEM_LANES), jnp.int32)],
        compiler_params=_params(("arbitrary",)),
        name="moe_dest",
    )(route_t, cnt)
    table = tuple(items[k, :n_blocks] for k in (I_BLOCK, I_EXPERT, I_NEXT, I_SLOT)) + (items[I_COUNT, :1],)
    return d1.reshape(N), d2.reshape(N), d3.reshape(N), table


def kernel(x, c, mod_w, mod_b, norm1_g, w_in, gate_w2, gate_b, gla_norm_g, conv_w, w_gla_out,
           w_conv_out, w_out, norm2_g, router_group_w, router_group_b, router_expert_w,
           router_expert_b, expert_w1, expert_w3, expert_w2, final_norm_g):
    B, S, D = x.shape
    L = mod_w.shape[0]
    N = B * S
    dk = gate_w2.shape[2]
    rank = gate_w2.shape[1]
    dv = gla_norm_g.shape[1]
    n_slots = N * 2
    n_blocks = n_slots // MOE_BLOCK + N_EXPERTS
    assert S % GLA_TILE == 0 and S % MIX_TILE == 0 and S % IN_TILE == 0 and N % ROW_TILE == 0
    assert n_slots % MOE_BLOCK == 0 and MOE_BLOCK & (MOE_BLOCK - 1) == 0 and n_blocks <= ITEM_LANES
    assert N_EXPERTS * (MOE_BLOCK - 1) <= N
    assert N_GROUPS + N_EXPERTS <= LANES and EXPERTS_PER_GROUP == 8 and 3 * rank <= LANES and D % (2 * ROW_PIECE) == 0

    mod = _modulation(c, mod_w, mod_b)
    xf = x.reshape(N, D)
    o_gd = 2 * dk + dv
    o_r = o_gd + rank
    o_b = o_r + dv
    for l in range(L):
        modl = mod[l].reshape(B, 6, D)
        wl = w_in[l]
        wa = jnp.concatenate([wl[:, :o_gd], wl[:, o_r:o_b]], axis=1).astype(BF16)
        wb = wl[:, o_b:].astype(BF16)
        wgd = wl[:, o_gd:o_r]
        wg = jnp.pad(jnp.concatenate([wgd, wgd, wgd], axis=1), ((0, 0), (0, LANES - 3 * rank))).astype(BF16)
        g_hi = gate_w2[l].astype(BF16)
        g_lo = (gate_w2[l] - g_hi.astype(F32)).astype(BF16)
        gw2 = jnp.pad(jnp.concatenate([g_hi, g_hi, g_lo], axis=0), ((0, LANES - 3 * rank), (0, 0)))
        if l == 0:
            za, zb, zg = _in_projection(xf, modl, norm1_g[l].reshape(1, D), conv_w[l], wa, wb, wg, S)
        else:
            xf, za, zb, zg = _in_projection(x1, modl, norm1_g[l].reshape(1, D), conv_w[l], wa, wb, wg, S,
                                            combine=(route, yk, mod[l - 1].reshape(B, 6, D)))
        og = _gla(za, zg, gw2, gate_b[l].reshape(1, dk), gla_norm_g[l].reshape(1, dv), B, S, dk, dv, rank)
        wr = jnp.pad(jnp.concatenate([router_group_w[l], router_expert_w[l]], axis=1),
                     ((0, 0), (0, LANES - N_GROUPS - N_EXPERTS)))
        wr_hi = wr.astype(BF16)
        wr = jnp.concatenate([wr_hi, (wr - wr_hi.astype(F32)).astype(BF16)], axis=1)
        br = jnp.pad(jnp.concatenate([router_group_b[l], router_expert_b[l]]),
                     (0, LANES - N_GROUPS - N_EXPERTS)).reshape(1, LANES)
        x1, h2, logits = _mixer_out(
            og, zb, xf, modl, w_gla_out[l].astype(BF16), w_conv_out[l].astype(BF16),
            w_out[l].astype(BF16), norm2_g[l].reshape(1, D), wr, br, B, S)
        route_t, route, cnt = _routing(logits)
        dest1, dest2, dest3, items = _destinations(route_t, cnt, n_blocks)
        xs = _sc_dispatch(h2, (dest1, dest2, dest3), n_blocks * MOE_BLOCK)
        yb = _experts(xs, items, expert_w1, expert_w3, expert_w2, l)
        yk = _sc_return(yb, jnp.concatenate([dest1, dest2]))
    out = _final_combine(x1, route, modl, final_norm_g.reshape(1, D), yk, S)
    return out.reshape(B, S, D)
```

```python
import functools

import jax
import jax.numpy as jnp
from jax import lax
from jax.experimental import pallas as pl
from jax.experimental.pallas import tpu as pltpu
from jax.experimental.pallas import tpu_sc as plsc

F32 = jnp.float32
BF16 = jnp.bfloat16
HIGHEST = lax.Precision.HIGHEST

GLA_HEADS = 4
GATE_TAU = 16.0
GLA_CHUNK = 64
N_GROUPS = 8
EXPERTS_PER_GROUP = 8
N_EXPERTS = N_GROUPS * EXPERTS_PER_GROUP
NORM_EPS = 1e-6

LANES = 128
VMEM_LIMIT_BYTES = 56 * 1024 * 1024

IN_TILE = 512
GLA_TILE = 512
MIX_TILE = 512
MIX_SUB = 256
MOE_BLOCK = 512
ROW_TILE = 1024
FINAL_PARTS = 2
ROW_PIECE = 256
U32 = jnp.uint32

R_ID1, R_ID2, R_W1, R_W2, R_RANK1, R_RANK2 = 0, 1, 2, 3, 4, 5


def _dot(a, b):
    return jnp.dot(a, b, preferred_element_type=F32)


def _round_bf16(x):
    xb = x.astype(BF16)
    return xb, xb.astype(F32)


def _pack_rows(xr):
    half = xr.shape[1] // 2
    out = []
    for p in range(half // ROW_PIECE):
        lo = xr[:, p * ROW_PIECE:(p + 1) * ROW_PIECE]
        hi = xr[:, half + p * ROW_PIECE:half + (p + 1) * ROW_PIECE]
        out.append((pltpu.bitcast(lo, U32) >> 16) | pltpu.bitcast(hi, U32))
    return out


def _unpack_rows(ref):
    words = [ref[p] for p in range(ref.shape[0])]
    lo = [pltpu.bitcast(w << 16, F32) for w in words]
    hi = [pltpu.bitcast(w & jnp.uint32(0xFFFF0000), F32) for w in words]
    return jnp.concatenate(lo + hi, axis=-1)


def _const_spec(shape):
    nd = len(shape)
    return pl.BlockSpec(shape, lambda *_: (0,) * nd, pipeline_mode=pl.Buffered(1))


def _params(sem):
    return pltpu.CompilerParams(dimension_semantics=sem, vmem_limit_bytes=VMEM_LIMIT_BYTES)


MOD_ROWS = 256


def _mod_kernel(c_ref, w_ref, b_ref, o_ref):
    c = c_ref[...]
    sc = c * jax.nn.sigmoid(c)
    part = jnp.dot(sc, w_ref[0], precision=HIGHEST, preferred_element_type=F32)

    @pl.when(pl.program_id(1) == 0)
    def _():
        o_ref[0] = part + b_ref[0]

    @pl.when(pl.program_id(1) > 0)
    def _():
        o_ref[0] += part


def _modulation(c, mod_w, mod_b):
    L, D, D6 = mod_w.shape
    B = c.shape[0]
    kb = MOD_ROWS
    return pl.pallas_call(
        _mod_kernel,
        grid=(L, D // kb),
        in_specs=[
            pl.BlockSpec((B, kb), lambda l, k: (0, k)),
            pl.BlockSpec((1, kb, D6), lambda l, k: (l, k, 0)),
            pl.BlockSpec((1, 1, D6), lambda l, k: (l, 0, 0)),
        ],
        out_specs=pl.BlockSpec((1, B, D6), lambda l, k: (l, 0, 0)),
        out_shape=jax.ShapeDtypeStruct((L, B, D6), F32),
        compiler_params=_params(("arbitrary", "arbitrary")),
        name="adaln_mod",
    )(c, mod_w, mod_b.reshape(L, 1, D6))


CONV_COLS = 256


def _moe_mix(route_ref, y0_ref, y1_ref):
    rec = route_ref[...]
    w1 = rec[:, R_W1:R_W1 + 1]
    w2 = rec[:, R_W2:R_W2 + 1]
    return w1 * _unpack_rows(y0_ref) + w2 * _unpack_rows(y1_ref)


def _inproj_kernel(*refs, seq, fuse_combine):
    if fuse_combine:
        (x_ref, route_ref, y0_ref, y1_ref, modp_ref, mod_ref, g_ref, cw_ref, wa_ref, wb_ref, wg_ref,
         xo_ref, za_ref, zb_ref, zg_ref, carry_ref) = refs
        x = x_ref[...] + modp_ref[0][5:6] * _moe_mix(route_ref, y0_ref, y1_ref)
        xo_ref[...] = x
    else:
        (x_ref, mod_ref, g_ref, cw_ref, wa_ref, wb_ref, wg_ref, za_ref, zb_ref, zg_ref, carry_ref) = refs
        x = x_ref[...]
    tm, d = x.shape

    @pl.when((pl.program_id(0) * tm) % seq == 0)
    def _():
        carry_ref[...] = jnp.zeros_like(carry_ref)

    ms = jnp.mean(x * x, axis=-1, keepdims=True)
    m = mod_ref[0]
    h = (x * lax.rsqrt(ms + NORM_EPS) * (g_ref[...] * (1.0 + m[1:2])) + m[0:1]).astype(BF16)
    for j in range(za_ref.shape[1] // d):
        za_ref[:, j * d:(j + 1) * d] = _dot(h, wa_ref[:, j * d:(j + 1) * d]).astype(BF16)
    zg_ref[...] = _dot(h, wg_ref[...])
    for j in range(1, zb_ref.shape[1] // d):
        zb_ref[:, j * d:(j + 1) * d] = _dot(h, wb_ref[:, (j + 2) * d:(j + 3) * d]).astype(BF16)

    w = CONV_COLS
    rowi = lax.broadcasted_iota(jnp.int32, (tm, w), 0)
    for j in range(d // w):
        cols = slice(j * w, (j + 1) * w)
        cb = _dot(h, wb_ref[:, j * w:(j + 1) * w])
        cc = _dot(h, wb_ref[:, d + j * w:d + (j + 1) * w])
        ch = _dot(h, wb_ref[:, 2 * d + j * w:2 * d + (j + 1) * w])
        u = cc * ch
        prev = carry_ref[:, cols]
        u1 = jnp.where(rowi == 0, prev[7:8], pltpu.roll(u, 1, 0))
        u2 = jnp.where(rowi == 0, prev[6:7], jnp.where(rowi == 1, prev[7:8], pltpu.roll(u, 2, 0)))
        carry_ref[:, cols] = u[tm - 8:tm]
        conv = cw_ref[0:1, cols] * u2 + cw_ref[1:2, cols] * u1 + cw_ref[2:3, cols] * u
        zb_ref[:, cols] = (cb * conv).astype(BF16)


def _in_projection(x, modl, norm_g, conv_w, wa, wb, wg, seq, combine=None):
    N, D = x.shape
    tm = IN_TILE
    nt = N // tm
    ca, cb = wa.shape[1], wb.shape[1] - 2 * D
    tok = lambda i: (i, 0)
    per_batch = lambda i: ((i * tm) // seq, 0, 0)
    in_specs = [pl.BlockSpec((tm, D), tok)]
    args = [x]
    out_specs, out_shape = [], []
    if combine is not None:
        route, yk, mod_prev = combine
        piece_blk = (yk.shape[0], tm, ROW_PIECE)
        in_specs += [
            pl.BlockSpec((tm, LANES), tok),
            pl.BlockSpec(piece_blk, lambda i: (0, i, 0)),
            pl.BlockSpec(piece_blk, lambda i: (0, nt + i, 0)),
            pl.BlockSpec((1, 6, D), per_batch),
        ]
        args += [route, yk, yk, mod_prev]
        out_specs.append(pl.BlockSpec((tm, D), tok))
        out_shape.append(jax.ShapeDtypeStruct((N, D), F32))
    in_specs += [
        pl.BlockSpec((1, 6, D), per_batch),
        _const_spec((1, D)),
        _const_spec(conv_w.shape),
        _const_spec((D, ca)),
        _const_spec(wb.shape),
        _const_spec((D, LANES)),
    ]
    args += [modl, norm_g, conv_w, wa, wb, wg]
    out_specs += [
        pl.BlockSpec((tm, ca), tok),
        pl.BlockSpec((tm, cb), tok),
        pl.BlockSpec((tm, LANES), tok),
    ]
    out_shape += [
        jax.ShapeDtypeStruct((N, ca), BF16),
        jax.ShapeDtypeStruct((N, cb), BF16),
        jax.ShapeDtypeStruct((N, LANES), F32),
    ]
    kern = functools.partial(_inproj_kernel, seq=seq, fuse_combine=combine is not None)
    return pl.pallas_call(
        kern,
        grid=(nt,),
        in_specs=in_specs,
        out_specs=out_specs,
        out_shape=out_shape,
        scratch_shapes=[pltpu.VMEM((8, D), F32)],
        compiler_params=_params(("arbitrary",)),
        name="in_projection",
    )(*args)


def _log_sigmoid(x):
    return jnp.minimum(x, 0.0) - jnp.log(1.0 + jnp.exp(-jnp.abs(x)))


def _gla_kernel(za_ref, zg_ref, gw2_ref, gb_ref, ng_ref, o_ref, st_ref, lg_ref, *, dk, dv, rank):
    heads = GLA_HEADS
    dkh, dvh = dk // heads, dv // heads
    c = GLA_CHUNK
    ts = za_ref.shape[0]

    @pl.when(pl.program_id(1) == 0)
    def _():
        st_ref[...] = jnp.zeros_like(st_ref)

    zg = zg_ref[...]
    zg_hi = zg.astype(BF16)
    zg_lo = (zg - zg_hi.astype(F32)).astype(BF16)
    lane = lax.broadcasted_iota(jnp.int32, zg.shape, 1)
    lhs = jnp.where(jnp.logical_and(lane >= rank, lane < 2 * rank), zg_lo, zg_hi)
    pre = _dot(lhs, gw2_ref[...]) + gb_ref[...]
    lg = _log_sigmoid(pre) * (1.0 / GATE_TAU)
    lg_hi = lg.astype(BF16)
    lg_ref[:, 0:dk] = lg_hi
    lg_ref[:, dk:2 * dk] = (lg - lg_hi.astype(F32)).astype(BF16)

    row = lax.broadcasted_iota(jnp.int32, (c, c), 0)
    col = lax.broadcasted_iota(jnp.int32, (c, c), 1)
    causal = row >= col
    tril = causal.astype(BF16)
    qscale = dkh ** -0.5

    for ci in range(ts // c):
        rows = slice(ci * c, (ci + 1) * c)
        b_two = _dot(tril, lg_ref[rows, :])
        b_all = b_two[:, 0:dk] + b_two[:, dk:2 * dk]
        for hd in range(heads):
            ks = slice(hd * dkh, (hd + 1) * dkh)
            q = za_ref[rows, hd * dkh:(hd + 1) * dkh].astype(F32) * qscale
            k = za_ref[rows, dk + hd * dkh:dk + (hd + 1) * dkh].astype(F32)
            v = za_ref[rows, 2 * dk + hd * dvh:2 * dk + (hd + 1) * dvh]
            r = za_ref[rows, 2 * dk + dv + hd * dvh:2 * dk + dv + (hd + 1) * dvh].astype(F32)
            b = b_all[:, ks]
            b_last = b[c - 1:c, :]
            q_t = (q * jnp.exp(b)).astype(BF16)
            k_t = (k * jnp.exp(-b)).astype(BF16)
            k_s = (k * jnp.exp(b_last - b)).astype(BF16)
            decay = jnp.exp(b_last)
            attn = lax.dot_general(q_t, k_t, (((1,), (1,)), ((), ())), preferred_element_type=F32)
            attn = jnp.where(causal, attn, 0.0).astype(BF16)
            st = st_ref[hd]
            o = _dot(attn, v) + lax.dot_general(
                q_t, st.astype(BF16), (((1,), (1,)), ((), ())), preferred_element_type=F32)
            upd = lax.dot_general(v, k_s, (((0,), (0,)), ((), ())), preferred_element_type=F32)
            st_ref[hd] = st * decay + upd
            ms = jnp.mean(o * o, axis=-1, keepdims=True)
            on = o * lax.rsqrt(ms + NORM_EPS) * ng_ref[:, hd * dvh:(hd + 1) * dvh]
            o_ref[rows, hd * dvh:(hd + 1) * dvh] = (on * (r * jax.nn.sigmoid(r))).astype(BF16)


def _gla(za, zg, gw2, gb, ng, batch, seq, dk, dv, rank):
    N = za.shape[0]
    ts = GLA_TILE
    ns = seq // ts
    heads = GLA_HEADS
    kern = functools.partial(_gla_kernel, dk=dk, dv=dv, rank=rank)
    return pl.pallas_call(
        kern,
        grid=(batch, ns),
        in_specs=[
            pl.BlockSpec((ts, za.shape[1]), lambda b, s: (b * ns + s, 0)),
            pl.BlockSpec((ts, LANES), lambda b, s: (b * ns + s, 0)),
            _const_spec((LANES, dk)),
            _const_spec((1, dk)),
            _const_spec((1, dv)),
        ],
        out_specs=pl.BlockSpec((ts, dv), lambda b, s: (b * ns + s, 0)),
        out_shape=jax.ShapeDtypeStruct((N, dv), BF16),
        scratch_shapes=[
            pltpu.VMEM((heads, dv // heads, dk // heads), F32),
            pltpu.VMEM((ts, 2 * dk), BF16),
        ],
        compiler_params=_params(("arbitrary", "arbitrary")),
        name="gla",
    )(za, zg, gw2, gb, ng)


def _mixout_kernel(og_ref, zb_ref, x_ref, mod_ref, wga_ref, wco_ref, wo_ref, n2_ref,
                   wr_ref, br_ref, x1_ref, h2_ref, logit_ref):
    tm, d = x_ref.shape
    m = mod_ref[0]
    sub = MIX_SUB
    gain2 = n2_ref[...] * (1.0 + m[4:5])
    for r0 in range(0, tm, sub):
        rs = slice(r0, r0 + sub)
        ga = zb_ref[rs, d:2 * d].astype(F32)
        gc = zb_ref[rs, 2 * d:3 * d].astype(F32)
        y_conv = _dot(zb_ref[rs, 0:d], wco_ref[...])
        y_gla = _dot(og_ref[rs, :], wga_ref[...])
        y = jax.nn.sigmoid(ga) * y_gla + jax.nn.sigmoid(gc) * y_conv
        y = _dot(y.astype(BF16), wo_ref[...])
        x1 = x_ref[rs, :] + m[2:3] * y
        x1_ref[rs, :] = x1

        ms = jnp.mean(x1 * x1, axis=-1, keepdims=True)
        h2 = x1 * lax.rsqrt(ms + NORM_EPS) * gain2 + m[3:4]
        h_hi, h_r = _round_bf16(h2)
        for p, words in enumerate(_pack_rows(h_r)):
            h2_ref[p, rs, :] = words

        h_lo = (h2 - h_r).astype(BF16)
        two = _dot(h_hi, wr_ref[...])
        logits = two[:, 0:LANES] + two[:, LANES:2 * LANES] + _dot(h_lo, wr_ref[:, 0:LANES]) + br_ref[...]
        logit_ref[rs, :] = logits


def _mixer_out(og, zb, x, modl, wga, wco, wo, n2g, wr, br, batch, seq):
    N, D = x.shape
    tm = MIX_TILE
    ns = seq // tm
    tok = lambda b, s: (b * ns + s, 0)
    return pl.pallas_call(
        _mixout_kernel,
        grid=(batch, ns),
        in_specs=[
            pl.BlockSpec((tm, D), tok),
            pl.BlockSpec((tm, zb.shape[1]), tok),
            pl.BlockSpec((tm, D), tok),
            pl.BlockSpec((1, 6, D), lambda b, s: (b, 0, 0)),
            _const_spec((D, D)),
            _const_spec((D, D)),
            _const_spec((D, D)),
            _const_spec((1, D)),
            _const_spec((D, 2 * LANES)),
            _const_spec((1, LANES)),
        ],
        out_specs=[
            pl.BlockSpec((tm, D), tok),
            pl.BlockSpec((D // 2 // ROW_PIECE, tm, ROW_PIECE), lambda b, s: (0, b * ns + s, 0)),
            pl.BlockSpec((tm, LANES), tok),
        ],
        out_shape=[
            jax.ShapeDtypeStruct((N, D), F32),
            jax.ShapeDtypeStruct((D // 2 // ROW_PIECE, N, ROW_PIECE), U32),
            jax.ShapeDtypeStruct((N, LANES), F32),
        ],
        compiler_params=_params(("arbitrary", "arbitrary")),
        name="mixer_out",
    )(og, zb, x, modl, wga, wco, wo, n2g, wr, br)


def _route_kernel(logit_ref, routet_ref, route_ref, cnt_ref, run_ref):
    tm = logit_ref.shape[0]
    sub = MIX_SUB
    rows8 = EXPERTS_PER_GROUP

    @pl.when(pl.program_id(0) == 0)
    def _():
        run_ref[...] = jnp.zeros_like(run_ref)

    sub8 = lax.broadcasted_iota(jnp.int32, (rows8, sub), 0)
    erow = lax.broadcasted_iota(jnp.int32, (N_EXPERTS, sub), 0)
    tr = lax.broadcasted_iota(jnp.int32, (sub, sub), 0)
    tc = lax.broadcasted_iota(jnp.int32, (sub, sub), 1)
    earlier = (tr < tc).astype(BF16)
    ones = jnp.ones((sub, LANES), BF16)
    neg = -jnp.inf
    run = run_ref[...]
    for r0 in range(0, tm, sub):
        lt = logit_ref[r0:r0 + sub, :].T
        gl = lt[0:N_GROUPS, :]
        gmax = jnp.max(gl, axis=0, keepdims=True)
        gsum = jnp.sum(jnp.exp(gl - gmax), axis=0, keepdims=True)
        g_w = 1.0 / gsum
        g_idx = jnp.min(jnp.where(gl == gmax, sub8, N_GROUPS), axis=0, keepdims=True)
        el = lt[N_GROUPS:N_GROUPS + rows8, :]
        for g in range(1, N_GROUPS):
            el = jnp.where(g_idx == g, lt[N_GROUPS + g * rows8:N_GROUPS + (g + 1) * rows8, :], el)
        e1 = jnp.max(el, axis=0, keepdims=True)
        i1 = jnp.min(jnp.where(el == e1, sub8, rows8), axis=0, keepdims=True)
        el2 = jnp.where(sub8 == i1, neg, el)
        e2 = jnp.max(el2, axis=0, keepdims=True)
        i2 = jnp.min(jnp.where(el2 == e2, sub8, rows8), axis=0, keepdims=True)
        ratio = jnp.exp(e2 - e1)
        w1 = g_w / (1.0 + ratio)
        w2 = g_w * ratio / (1.0 + ratio)
        id1 = g_idx * rows8 + i1
        id2 = g_idx * rows8 + i2

        oh1 = erow == id1
        oh2 = erow == id2
        oh1b = jnp.where(oh1, 1.0, 0.0).astype(BF16)
        oh2b = jnp.where(oh2, 1.0, 0.0).astype(BF16)
        tot1 = _dot(oh1b, ones)
        tot2 = _dot(oh2b, ones)
        base1 = jnp.concatenate([run] * (sub // LANES), axis=1)
        base2 = jnp.concatenate([run + tot1] * (sub // LANES), axis=1)
        c1 = _dot(oh1b, earlier) + base1
        c2 = _dot(oh2b, earlier) + base2
        rank1 = jnp.sum(jnp.where(oh1, c1, 0.0), axis=0, keepdims=True)
        rank2 = jnp.sum(jnp.where(oh2, c2, 0.0), axis=0, keepdims=True)
        run = run + tot1 + tot2

        rec = jnp.where(sub8 == R_ID1, id1.astype(F32), 0.0)
        rec = jnp.where(sub8 == R_ID2, id2.astype(F32), rec)
        rec = jnp.where(sub8 == R_W1, w1, rec)
        rec = jnp.where(sub8 == R_W2, w2, rec)
        rec = jnp.where(sub8 == R_RANK1, rank1, rec)
        rec = jnp.where(sub8 == R_RANK2, rank2, rec)
        routet_ref[:, r0:r0 + sub] = rec
        rec_full = jnp.concatenate([rec, jnp.zeros((LANES - rows8, sub), F32)], axis=0)
        route_ref[r0:r0 + sub, :] = rec_full.T
    run_ref[...] = run
    cnt_ref[...] = run


def _routing(logits):
    N = logits.shape[0]
    tm = 2048 if N % 2048 == 0 else MIX_SUB
    return pl.pallas_call(
        _route_kernel,
        grid=(N // tm,),
        in_specs=[pl.BlockSpec((tm, LANES), lambda i: (i, 0))],
        out_specs=[
            pl.BlockSpec((8, tm), lambda i: (0, i)),
            pl.BlockSpec((tm, LANES), lambda i: (i, 0)),
            pl.BlockSpec((N_EXPERTS, LANES), lambda i: (0, 0)),
        ],
        out_shape=[
            jax.ShapeDtypeStruct((8, N), F32),
            jax.ShapeDtypeStruct((N, LANES), F32),
            jax.ShapeDtypeStruct((N_EXPERTS, LANES), F32),
        ],
        scratch_shapes=[pltpu.VMEM((N_EXPERTS, LANES), F32)],
        compiler_params=_params(("arbitrary",)),
        name="moe_route",
    )(logits)


SC_WINDOW = 128


def _sc_mesh():
    return plsc.VectorSubcoreMesh(core_axis_name="core", subcore_axis_name="subcore")


def _piece_index(rows, pieces, n_rows):
    return (jnp.arange(pieces, dtype=jnp.int32)[:, None] * n_rows + rows[None, :]).reshape(1, -1)


def _sc_dispatch(h2, dests, n_rows):
    pieces, N, pc = h2.shape
    w = SC_WINDOW

    @functools.partial(pl.kernel, out_type=jax.ShapeDtypeStruct((pieces * n_rows, pc), h2.dtype),
                       mesh=_sc_mesh(), scratch_types=[], name="moe_dispatch_sc")
    def run(x_hbm, *refs):
        o_hbm = refs[-1]

        def body(x_vmem, *idx_vmem):
            for i_vmem in idx_vmem:
                pltpu.sync_copy(x_vmem, o_hbm.at[i_vmem.at[0]])

        pltpu.emit_pipeline(
            body,
            grid=(pieces * N // w,),
            in_specs=[pl.BlockSpec((w, pc), lambda i: (i, 0))]
            + [pl.BlockSpec((1, w), lambda i: (0, i))] * len(dests),
            out_specs=[],
            core_axis_name=("core", "subcore"),
            dimension_semantics=(pltpu.PARALLEL,),
        )(x_hbm, *refs[:-1])

    xs = run(h2.reshape(pieces * N, pc), *[_piece_index(d, pieces, n_rows) for d in dests])
    return xs.reshape(pieces, n_rows, pc)


def _sc_return(yb, dest):
    M = dest.shape[0]
    pieces, P, pc = yb.shape
    w = SC_WINDOW

    @functools.partial(pl.kernel, out_type=jax.ShapeDtypeStruct((pieces * M, pc), yb.dtype),
                       mesh=_sc_mesh(), scratch_types=[], name="moe_return_sc")
    def run(y_hbm, i_hbm, o_hbm):
        def body(i_vmem, o_vmem):
            pltpu.sync_copy(y_hbm.at[i_vmem.at[0]], o_vmem)

        pltpu.emit_pipeline(
            body,
            grid=(pieces * M // w,),
            in_specs=[pl.BlockSpec((1, w), lambda i: (0, i))],
            out_specs=[pl.BlockSpec((w, pc), lambda i: (i, 0))],
            core_axis_name=("core", "subcore"),
            dimension_semantics=(pltpu.PARALLEL,),
        )(i_hbm, o_hbm)

    return run(yb.reshape(pieces * P, pc), _piece_index(dest, pieces, P)).reshape(pieces, M, pc)


def _expert_kernel(ib_ref, ie_ref, nxt_ref, slot_ref, nb_ref, xs_ref, w1_ref, w3_ref, w2_ref, y_ref,
                   wf1, wf3, wf2, w1b, w3b, w2b, sem, *, layer):
    j = pl.program_id(0)
    jp = jnp.maximum(j - 1, 0)
    live = j < nb_ref[0]
    new_expert = jnp.logical_or(j == 0, ie_ref[j] != ie_ref[jp])
    streams = ((w1_ref, wf1), (w3_ref, wf3), (w2_ref, wf2))

    def weight_copies(expert, s):
        return [pltpu.make_async_copy(w.at[layer, expert], buf.at[s], sem.at[s, k])
                for k, (w, buf) in enumerate(streams)]

    @pl.when(j == 0)
    def _():
        for cp in weight_copies(ie_ref[0], 0):
            cp.start()

    @pl.when(jnp.logical_and(new_expert, live))
    def _():
        s = slot_ref[j]
        for cp in weight_copies(ie_ref[j], s):
            cp.wait()
        w1b[...] = wf1[s].astype(BF16)
        w3b[...] = wf3[s].astype(BF16)
        w2b[...] = wf2[s].astype(BF16)

        @pl.when(nxt_ref[j] >= 0)
        def _():
            for cp in weight_copies(nxt_ref[j], 1 - s):
                cp.start()

    @pl.when(live)
    def _():
        xb = _unpack_rows(xs_ref).astype(BF16)
        h1 = _dot(xb, w1b[...])
        h3 = _dot(xb, w3b[...])
        a = (h1 * jax.nn.sigmoid(h1) * h3).astype(BF16)
        for p, words in enumerate(_pack_rows(_round_bf16(_dot(a, w2b[...]))[1])):
            y_ref[p] = words


def _experts(xs, items, w1, w3, w2, layer):
    pieces, P, pc = xs.shape
    _, E, D, DE = w1.shape
    bm = MOE_BLOCK
    n_items_max = P // bm
    rows = lambda j, *prefetch: (0, prefetch[0][j], 0)
    return pl.pallas_call(
        functools.partial(_expert_kernel, layer=layer),
        grid_spec=pltpu.PrefetchScalarGridSpec(
            num_scalar_prefetch=5,
            grid=(n_items_max,),
            in_specs=[
                pl.BlockSpec((pieces, bm, pc), rows),
                pl.BlockSpec(memory_space=pl.ANY),
                pl.BlockSpec(memory_space=pl.ANY),
                pl.BlockSpec(memory_space=pl.ANY),
            ],
            out_specs=pl.BlockSpec((pieces, bm, pc), rows),
            scratch_shapes=[
                pltpu.VMEM((2, D, DE), F32),
                pltpu.VMEM((2, D, DE), F32),
                pltpu.VMEM((2, DE, D), F32),
                pltpu.VMEM((D, DE), BF16),
                pltpu.VMEM((D, DE), BF16),
                pltpu.VMEM((DE, D), BF16),
                pltpu.SemaphoreType.DMA((2, 3)),
            ],
        ),
        out_shape=jax.ShapeDtypeStruct((pieces, P, pc), U32),
        compiler_params=_params(("arbitrary",)),
        name="moe_experts",
    )(*items, xs, w1, w3, w2)


def _combine_kernel(x1_ref, route_ref, mod_ref, fg_ref, y0_ref, y1_ref, *rest):
    o_ref = rest[-1]
    x2 = x1_ref[...] + mod_ref[0][5:6] * _moe_mix(route_ref, y0_ref, y1_ref)
    ms = jnp.mean(x2 * x2, axis=-1, keepdims=True)
    o_ref[...] = x2 * lax.rsqrt(ms + NORM_EPS) * fg_ref[...]


def _final_combine(x1, route, modl, final_g, yk, seq, first_tile, prev_out):
    N, D = x1.shape
    tm = ROW_TILE
    nt = yk.shape[1] // 2 // tm
    tok = lambda i: (first_tile + i, 0)
    in_specs = [
        pl.BlockSpec((tm, D), tok),
        pl.BlockSpec((tm, LANES), tok),
        pl.BlockSpec((1, 6, D), lambda i: (((first_tile + i) * tm) // seq, 0, 0)),
        pl.BlockSpec((1, D), lambda i: (0, 0)),
        pl.BlockSpec((yk.shape[0], tm, ROW_PIECE), lambda i: (0, i, 0)),
        pl.BlockSpec((yk.shape[0], tm, ROW_PIECE), lambda i: (0, nt + i, 0)),
    ]
    args = [x1, route, modl, final_g, yk, yk]
    aliases = {}
    if prev_out is not None:
        in_specs.append(pl.BlockSpec(memory_space=pl.ANY))
        args.append(prev_out)
        aliases = {len(args) - 1: 0}
    return pl.pallas_call(
        _combine_kernel,
        grid=(nt,),
        in_specs=in_specs,
        out_specs=pl.BlockSpec((tm, D), tok),
        out_shape=jax.ShapeDtypeStruct((N, D), F32),
        input_output_aliases=aliases,
        compiler_params=_params(("arbitrary",)),
        name="moe_combine",
    )(*args)


ITEM_LANES = 256
I_BLOCK, I_EXPERT, I_NEXT, I_SLOT, I_COUNT = 0, 1, 2, 3, 4


def _dest_kernel(routet_ref, cnt_ref, d1_ref, d2_ref, d3_ref, items_ref):
    tm = routet_ref.shape[1]
    ne = N_EXPERTS
    bm = float(MOE_BLOCK)
    cnt = cnt_ref[...]
    r = lax.broadcasted_iota(jnp.int32, (ne, ne), 0)
    c = lax.broadcasted_iota(jnp.int32, (ne, ne), 1)
    lower = (r >= c).astype(F32)
    cumsum = lambda a: jnp.dot(lower, a, precision=HIGHEST, preferred_element_type=F32)
    padded = jnp.floor((cnt + (bm - 1.0)) * (1.0 / bm)) * bm
    pend = cumsum(padded)
    pstart = pend - padded
    npad = padded - cnt
    pad_end = cumsum(npad)
    pad_start = pad_end - npad

    tile = lambda a, n: jnp.concatenate([a] * (n // LANES), axis=1)
    expert = lax.broadcasted_iota(jnp.int32, (ne, tm), 0).astype(F32)
    rec = routet_ref[...]
    pstart_t = tile(pstart, tm)

    def sorted_row(row_id, row_rank):
        sel = expert == rec[row_id:row_id + 1, :]
        return jnp.sum(jnp.where(sel, pstart_t, 0.0), axis=0, keepdims=True) + rec[row_rank:row_rank + 1, :]

    dest1 = sorted_row(R_ID1, R_RANK1)
    dest2 = sorted_row(R_ID2, R_RANK2)
    t = (lax.broadcasted_iota(jnp.int32, (1, tm), 1) + pl.program_id(0) * tm).astype(F32)
    ps_t, pe_t = tile(pad_start, tm), tile(pad_end, tm)
    in_e = jnp.logical_and(ps_t <= t, t < pe_t)
    pad_row = jnp.sum(jnp.where(in_e, tile(pstart + cnt, tm) + (t - ps_t), 0.0), axis=0, keepdims=True)
    dest3 = jnp.where(t < pe_t[ne - 1:ne, :], pad_row, dest1)
    d1_ref[0] = dest1.astype(jnp.int32)
    d2_ref[0] = dest2.astype(jnp.int32)
    d3_ref[0] = dest3.astype(jnp.int32)

    @pl.when(pl.program_id(0) == 0)
    def _():
        nl = ITEM_LANES
        pend_i = tile(pend, nl)
        n_items = pend_i[ne - 1:ne, :] * (1.0 / bm)
        blk = jnp.minimum(lax.broadcasted_iota(jnp.int32, (1, nl), 1).astype(F32), n_items - 1.0)
        erow = lax.broadcasted_iota(jnp.int32, (ne, nl), 0).astype(F32)
        ie = jnp.sum(jnp.where(pend_i <= blk * bm, 1.0, 0.0), axis=0, keepdims=True)
        nonempty = tile(cnt, nl) > 0.0
        order = jnp.sum(jnp.where(jnp.logical_and(nonempty, erow < ie), 1.0, 0.0), axis=0, keepdims=True)
        nxt = jnp.min(jnp.where(jnp.logical_and(nonempty, erow > ie), erow, float(ne)), axis=0, keepdims=True)
        nxt = jnp.where(nxt == float(ne), -1.0, nxt)
        slot = order - 2.0 * jnp.floor(order * 0.5)
        row = lax.broadcasted_iota(jnp.int32, (8, nl), 0)
        tab = jnp.where(row == I_BLOCK, blk, 0.0)
        tab = jnp.where(row == I_EXPERT, ie, tab)
        tab = jnp.where(row == I_NEXT, nxt, tab)
        tab = jnp.where(row == I_SLOT, slot, tab)
        tab = jnp.where(row == I_COUNT, n_items, tab)
        items_ref[...] = tab.astype(jnp.int32)


def _destinations(route_t, cnt, n_blocks):
    N = route_t.shape[1]
    tm = 2048 if N % 2048 == 0 else MIX_SUB
    out = jax.ShapeDtypeStruct((N // tm, 1, tm), jnp.int32)
    d1, d2, d3, items = pl.pallas_call(
        _dest_kernel,
        grid=(N // tm,),
        in_specs=[pl.BlockSpec((8, tm), lambda i: (0, i)),
                  pl.BlockSpec((N_EXPERTS, LANES), lambda i: (0, 0))],
        out_specs=[pl.BlockSpec((1, 1, tm), lambda i: (i, 0, 0))] * 3
        + [pl.BlockSpec((8, ITEM_LANES), lambda i: (0, 0))],
        out_shape=[out, out, out, jax.ShapeDtypeStruct((8, ITEM_LANES), jnp.int32)],
        compiler_params=_params(("arbitrary",)),
        name="moe_dest",
    )(route_t, cnt)
    table = tuple(items[k, :n_blocks] for k in (I_BLOCK, I_EXPERT, I_NEXT, I_SLOT)) + (items[I_COUNT, :1],)
    return d1.reshape(N), d2.reshape(N), d3.reshape(N), table


def kernel(x, c, mod_w, mod_b, norm1_g, w_in, gate_w2, gate_b, gla_norm_g, conv_w, w_gla_out,
           w_conv_out, w_out, norm2_g, router_group_w, router_group_b, router_expert_w,
           router_expert_b, expert_w1, expert_w3, expert_w2, final_norm_g):
    B, S, D = x.shape
    L = mod_w.shape[0]
    N = B * S
    dk = gate_w2.shape[2]
    rank = gate_w2.shape[1]
    dv = gla_norm_g.shape[1]
    n_slots = N * 2
    n_blocks = n_slots // MOE_BLOCK + N_EXPERTS
    assert S % GLA_TILE == 0 and S % MIX_TILE == 0 and S % IN_TILE == 0
    assert N % (FINAL_PARTS * ROW_TILE) == 0
    assert n_slots % MOE_BLOCK == 0 and MOE_BLOCK & (MOE_BLOCK - 1) == 0 and n_blocks <= ITEM_LANES
    assert N_EXPERTS * (MOE_BLOCK - 1) <= N
    assert N_GROUPS + N_EXPERTS <= LANES and EXPERTS_PER_GROUP == 8 and 3 * rank <= LANES and D % (2 * ROW_PIECE) == 0

    mod = _modulation(c, mod_w, mod_b)
    xf = x.reshape(N, D)
    o_gd = 2 * dk + dv
    o_r = o_gd + rank
    o_b = o_r + dv
    for l in range(L):
        modl = mod[l].reshape(B, 6, D)
        wl = w_in[l]
        wa = jnp.concatenate([wl[:, :o_gd], wl[:, o_r:o_b]], axis=1).astype(BF16)
        wb = wl[:, o_b:].astype(BF16)
        wgd = wl[:, o_gd:o_r]
        wg = jnp.pad(jnp.concatenate([wgd, wgd, wgd], axis=1), ((0, 0), (0, LANES - 3 * rank))).astype(BF16)
        g_hi = gate_w2[l].astype(BF16)
        g_lo = (gate_w2[l] - g_hi.astype(F32)).astype(BF16)
        gw2 = jnp.pad(jnp.concatenate([g_hi, g_hi, g_lo], axis=0), ((0, LANES - 3 * rank), (0, 0)))
        if l == 0:
            za, zb, zg = _in_projection(xf, modl, norm1_g[l].reshape(1, D), conv_w[l], wa, wb, wg, S)
        else:
            xf, za, zb, zg = _in_projection(x1, modl, norm1_g[l].reshape(1, D), conv_w[l], wa, wb, wg, S,
                                            combine=(route, yk, mod[l - 1].reshape(B, 6, D)))
        og = _gla(za, zg, gw2, gate_b[l].reshape(1, dk), gla_norm_g[l].reshape(1, dv), B, S, dk, dv, rank)
        wr = jnp.pad(jnp.concatenate([router_group_w[l], router_expert_w[l]], axis=1),
                     ((0, 0), (0, LANES - N_GROUPS - N_EXPERTS)))
        wr_hi = wr.astype(BF16)
        wr = jnp.concatenate([wr_hi, (wr - wr_hi.astype(F32)).astype(BF16)], axis=1)
        br = jnp.pad(jnp.concatenate([router_group_b[l], router_expert_b[l]]),
                     (0, LANES - N_GROUPS - N_EXPERTS)).reshape(1, LANES)
        x1, h2, logits = _mixer_out(
            og, zb, xf, modl, w_gla_out[l].astype(BF16), w_conv_out[l].astype(BF16),
            w_out[l].astype(BF16), norm2_g[l].reshape(1, D), wr, br, B, S)
        route_t, route, cnt = _routing(logits)
        dest1, dest2, dest3, items = _destinations(route_t, cnt, n_blocks)
        xs = _sc_dispatch(h2, (dest1, dest2, dest3), n_blocks * MOE_BLOCK)
        yb = _experts(xs, items, expert_w1, expert_w3, expert_w2, l)
        if l < L - 1:
            yk = _sc_return(yb, jnp.concatenate([dest1, dest2]))
    out = None
    part = N // FINAL_PARTS
    for p in range(FINAL_PARTS):
        tok = slice(p * part, (p + 1) * part)
        yk = _sc_return(yb, jnp.concatenate([dest1[tok], dest2[tok]]))
        out = _final_combine(x1, route, modl, final_norm_g.reshape(1, D), yk, S,
                             p * part // ROW_TILE, out)
    return out.reshape(B, S, D)
```

```python
import functools

import jax
import jax.numpy as jnp
from jax import lax
from jax.experimental import pallas as pl
from jax.experimental.pallas import tpu as pltpu
from jax.experimental.pallas import tpu_sc as plsc

F32 = jnp.float32
BF16 = jnp.bfloat16
HIGHEST = lax.Precision.HIGHEST

GLA_HEADS = 4
GATE_TAU = 16.0
GLA_CHUNK = 64
N_GROUPS = 8
EXPERTS_PER_GROUP = 8
N_EXPERTS = N_GROUPS * EXPERTS_PER_GROUP
NORM_EPS = 1e-6

LANES = 128
VMEM_LIMIT_BYTES = 56 * 1024 * 1024

IN_TILE = 512
GLA_TILE = 512
MIX_TILE = 512
MIX_SUB = 256
MOE_BLOCK = 512
ROW_TILE = 1024
FINAL_PARTS = 2
ROW_PIECE = 256
U32 = jnp.uint32

R_ID1, R_ID2, R_W1, R_W2, R_RANK1, R_RANK2 = 0, 1, 2, 3, 4, 5


def _dot(a, b):
    return jnp.dot(a, b, preferred_element_type=F32)


def _round_bf16(x):
    xb = x.astype(BF16)
    return xb, xb.astype(F32)


def _pack_rows(xr):
    half = xr.shape[1] // 2
    out = []
    for p in range(half // ROW_PIECE):
        lo = xr[:, p * ROW_PIECE:(p + 1) * ROW_PIECE]
        hi = xr[:, half + p * ROW_PIECE:half + (p + 1) * ROW_PIECE]
        out.append((pltpu.bitcast(lo, U32) >> 16) | pltpu.bitcast(hi, U32))
    return out


def _unpack_rows(ref):
    words = [ref[p] for p in range(ref.shape[0])]
    lo = [pltpu.bitcast(w << 16, F32) for w in words]
    hi = [pltpu.bitcast(w & jnp.uint32(0xFFFF0000), F32) for w in words]
    return jnp.concatenate(lo + hi, axis=-1)


def _const_spec(shape):
    nd = len(shape)
    return pl.BlockSpec(shape, lambda *_: (0,) * nd, pipeline_mode=pl.Buffered(1))


def _params(sem):
    return pltpu.CompilerParams(dimension_semantics=sem, vmem_limit_bytes=VMEM_LIMIT_BYTES)


MOD_ROWS = 256


def _mod_kernel(c_ref, w_ref, b_ref, o_ref):
    c = c_ref[...]
    sc = c * jax.nn.sigmoid(c)
    part = jnp.dot(sc, w_ref[0], precision=HIGHEST, preferred_element_type=F32)

    @pl.when(pl.program_id(1) == 0)
    def _():
        o_ref[0] = part + b_ref[0]

    @pl.when(pl.program_id(1) > 0)
    def _():
        o_ref[0] += part


def _modulation(c, mod_w, mod_b):
    L, D, D6 = mod_w.shape
    B = c.shape[0]
    kb = MOD_ROWS
    return pl.pallas_call(
        _mod_kernel,
        grid=(L, D // kb),
        in_specs=[
            pl.BlockSpec((B, kb), lambda l, k: (0, k)),
            pl.BlockSpec((1, kb, D6), lambda l, k: (l, k, 0)),
            pl.BlockSpec((1, 1, D6), lambda l, k: (l, 0, 0)),
        ],
        out_specs=pl.BlockSpec((1, B, D6), lambda l, k: (l, 0, 0)),
        out_shape=jax.ShapeDtypeStruct((L, B, D6), F32),
        compiler_params=_params(("arbitrary", "arbitrary")),
        name="adaln_mod",
    )(c, mod_w, mod_b.reshape(L, 1, D6))


PREP_ROWS = 256


def _prep_kernel(w_ref, wa_ref, wb_ref, wg_ref, *, o_gd, rank, dv):
    w = w_ref[0]
    o_r = o_gd + rank
    o_b = o_r + dv
    wa_ref[0, :, 0:o_gd] = w[:, 0:o_gd].astype(BF16)
    wa_ref[0, :, o_gd:o_gd + dv] = w[:, o_r:o_b].astype(BF16)
    wb_ref[0] = w[:, o_b:].astype(BF16)
    g = w[:, o_gd:o_r]
    pad = jnp.zeros((w.shape[0], LANES - 3 * rank), F32)
    wg_ref[0] = jnp.concatenate([g, g, g, pad], axis=1).astype(BF16)


def _prep_in_weights(w_in, o_gd, rank, dv):
    L, D, cols = w_in.shape
    ca = o_gd + dv
    cb = cols - o_gd - rank - dv
    kb = PREP_ROWS
    blk = lambda n: pl.BlockSpec((1, kb, n), lambda l, k: (l, k, 0))
    return pl.pallas_call(
        functools.partial(_prep_kernel, o_gd=o_gd, rank=rank, dv=dv),
        grid=(L, D // kb),
        in_specs=[blk(cols)],
        out_specs=[blk(ca), blk(cb), blk(LANES)],
        out_shape=[
            jax.ShapeDtypeStruct((L, D, ca), BF16),
            jax.ShapeDtypeStruct((L, D, cb), BF16),
            jax.ShapeDtypeStruct((L, D, LANES), BF16),
        ],
        compiler_params=_params(("arbitrary", "arbitrary")),
        name="in_weights",
    )(w_in)


CONV_COLS = 256


def _moe_mix(route_ref, y0_ref, y1_ref):
    rec = route_ref[...]
    w1 = rec[:, R_W1:R_W1 + 1]
    w2 = rec[:, R_W2:R_W2 + 1]
    return w1 * _unpack_rows(y0_ref) + w2 * _unpack_rows(y1_ref)


def _inproj_kernel(*refs, seq, fuse_combine):
    if fuse_combine:
        (x_ref, route_ref, y0_ref, y1_ref, modp_ref, mod_ref, g_ref, cw_ref, wa_ref, wb_ref, wg_ref,
         xo_ref, za_ref, zb_ref, zg_ref, carry_ref) = refs
        x = x_ref[...] + modp_ref[0][5:6] * _moe_mix(route_ref, y0_ref, y1_ref)
        xo_ref[...] = x
    else:
        (x_ref, mod_ref, g_ref, cw_ref, wa_ref, wb_ref, wg_ref, za_ref, zb_ref, zg_ref, carry_ref) = refs
        x = x_ref[...]
    tm, d = x.shape

    @pl.when((pl.program_id(0) * tm) % seq == 0)
    def _():
        carry_ref[...] = jnp.zeros_like(carry_ref)

    ms = jnp.mean(x * x, axis=-1, keepdims=True)
    m = mod_ref[0]
    h = (x * lax.rsqrt(ms + NORM_EPS) * (g_ref[...] * (1.0 + m[1:2])) + m[0:1]).astype(BF16)
    for j in range(za_ref.shape[1] // d):
        za_ref[:, j * d:(j + 1) * d] = _dot(h, wa_ref[:, j * d:(j + 1) * d]).astype(BF16)
    zg_ref[...] = _dot(h, wg_ref[...])
    for j in range(1, zb_ref.shape[1] // d):
        zb_ref[:, j * d:(j + 1) * d] = _dot(h, wb_ref[:, (j + 2) * d:(j + 3) * d]).astype(BF16)

    w = CONV_COLS
    rowi = lax.broadcasted_iota(jnp.int32, (tm, w), 0)
    for j in range(d // w):
        cols = slice(j * w, (j + 1) * w)
        cb = _dot(h, wb_ref[:, j * w:(j + 1) * w])
        cc = _dot(h, wb_ref[:, d + j * w:d + (j + 1) * w])
        ch = _dot(h, wb_ref[:, 2 * d + j * w:2 * d + (j + 1) * w])
        u = cc * ch
        prev = carry_ref[:, cols]
        u1 = jnp.where(rowi == 0, prev[7:8], pltpu.roll(u, 1, 0))
        u2 = jnp.where(rowi == 0, prev[6:7], jnp.where(rowi == 1, prev[7:8], pltpu.roll(u, 2, 0)))
        carry_ref[:, cols] = u[tm - 8:tm]
        conv = cw_ref[0:1, cols] * u2 + cw_ref[1:2, cols] * u1 + cw_ref[2:3, cols] * u
        zb_ref[:, cols] = (cb * conv).astype(BF16)


def _in_projection(x, modl, norm_g, conv_w, wa, wb, wg, layer, seq, combine=None):
    N, D = x.shape
    tm = IN_TILE
    nt = N // tm
    ca, cb = wa.shape[2], wb.shape[2] - 2 * D
    layer_w = lambda w: pl.BlockSpec((None,) + w.shape[1:], lambda i: (layer, 0, 0),
                                     pipeline_mode=pl.Buffered(1))
    tok = lambda i: (i, 0)
    per_batch = lambda i: ((i * tm) // seq, 0, 0)
    in_specs = [pl.BlockSpec((tm, D), tok)]
    args = [x]
    out_specs, out_shape = [], []
    if combine is not None:
        route, yk, mod_prev = combine
        piece_blk = (yk.shape[0], tm, ROW_PIECE)
        in_specs += [
            pl.BlockSpec((tm, LANES), tok),
            pl.BlockSpec(piece_blk, lambda i: (0, i, 0)),
            pl.BlockSpec(piece_blk, lambda i: (0, nt + i, 0)),
            pl.BlockSpec((1, 6, D), per_batch),
        ]
        args += [route, yk, yk, mod_prev]
        out_specs.append(pl.BlockSpec((tm, D), tok))
        out_shape.append(jax.ShapeDtypeStruct((N, D), F32))
    in_specs += [
        pl.BlockSpec((1, 6, D), per_batch),
        _const_spec((1, D)),
        _const_spec(conv_w.shape),
        layer_w(wa),
        layer_w(wb),
        layer_w(wg),
    ]
    args += [modl, norm_g, conv_w, wa, wb, wg]
    out_specs += [
        pl.BlockSpec((tm, ca), tok),
        pl.BlockSpec((tm, cb), tok),
        pl.BlockSpec((tm, LANES), tok),
    ]
    out_shape += [
        jax.ShapeDtypeStruct((N, ca), BF16),
        jax.ShapeDtypeStruct((N, cb), BF16),
        jax.ShapeDtypeStruct((N, LANES), F32),
    ]
    kern = functools.partial(_inproj_kernel, seq=seq, fuse_combine=combine is not None)
    return pl.pallas_call(
        kern,
        grid=(nt,),
        in_specs=in_specs,
        out_specs=out_specs,
        out_shape=out_shape,
        scratch_shapes=[pltpu.VMEM((8, D), F32)],
        compiler_params=_params(("arbitrary",)),
        name="in_projection",
    )(*args)


def _log_sigmoid(x):
    return jnp.minimum(x, 0.0) - jnp.log(1.0 + jnp.exp(-jnp.abs(x)))


def _gla_kernel(za_ref, zg_ref, gw2_ref, gb_ref, ng_ref, o_ref, st_ref, lg_ref, *, dk, dv, rank):
    heads = GLA_HEADS
    dkh, dvh = dk // heads, dv // heads
    c = GLA_CHUNK
    ts = za_ref.shape[0]

    @pl.when(pl.program_id(1) == 0)
    def _():
        st_ref[...] = jnp.zeros_like(st_ref)

    zg = zg_ref[...]
    zg_hi = zg.astype(BF16)
    zg_lo = (zg - zg_hi.astype(F32)).astype(BF16)
    lane = lax.broadcasted_iota(jnp.int32, zg.shape, 1)
    lhs = jnp.where(jnp.logical_and(lane >= rank, lane < 2 * rank), zg_lo, zg_hi)
    pre = _dot(lhs, gw2_ref[...]) + gb_ref[...]
    lg = _log_sigmoid(pre) * (1.0 / GATE_TAU)
    lg_hi = lg.astype(BF16)
    lg_ref[:, 0:dk] = lg_hi
    lg_ref[:, dk:2 * dk] = (lg - lg_hi.astype(F32)).astype(BF16)

    row = lax.broadcasted_iota(jnp.int32, (c, c), 0)
    col = lax.broadcasted_iota(jnp.int32, (c, c), 1)
    causal = row >= col
    tril = causal.astype(BF16)
    qscale = dkh ** -0.5

    for ci in range(ts // c):
        rows = slice(ci * c, (ci + 1) * c)
        b_two = _dot(tril, lg_ref[rows, :])
        b_all = b_two[:, 0:dk] + b_two[:, dk:2 * dk]
        for hd in range(heads):
            ks = slice(hd * dkh, (hd + 1) * dkh)
            q = za_ref[rows, hd * dkh:(hd + 1) * dkh].astype(F32) * qscale
            k = za_ref[rows, dk + hd * dkh:dk + (hd + 1) * dkh].astype(F32)
            v = za_ref[rows, 2 * dk + hd * dvh:2 * dk + (hd + 1) * dvh]
            r = za_ref[rows, 2 * dk + dv + hd * dvh:2 * dk + dv + (hd + 1) * dvh].astype(F32)
            b = b_all[:, ks]
            b_last = b[c - 1:c, :]
            q_t = (q * jnp.exp(b)).astype(BF16)
            k_t = (k * jnp.exp(-b)).astype(BF16)
            k_s = (k * jnp.exp(b_last - b)).astype(BF16)
            decay = jnp.exp(b_last)
            attn = lax.dot_general(q_t, k_t, (((1,), (1,)), ((), ())), preferred_element_type=F32)
            attn = jnp.where(causal, attn, 0.0).astype(BF16)
            st = st_ref[hd]
            o = _dot(attn, v) + lax.dot_general(
                q_t, st.astype(BF16), (((1,), (1,)), ((), ())), preferred_element_type=F32)
            upd = lax.dot_general(v, k_s, (((0,), (0,)), ((), ())), preferred_element_type=F32)
            st_ref[hd] = st * decay + upd
            ms = jnp.mean(o * o, axis=-1, keepdims=True)
            on = o * lax.rsqrt(ms + NORM_EPS) * ng_ref[:, hd * dvh:(hd + 1) * dvh]
            o_ref[rows, hd * dvh:(hd + 1) * dvh] = (on * (r * jax.nn.sigmoid(r))).astype(BF16)


def _gla(za, zg, gw2, gb, ng, batch, seq, dk, dv, rank):
    N = za.shape[0]
    ts = GLA_TILE
    ns = seq // ts
    heads = GLA_HEADS
    kern = functools.partial(_gla_kernel, dk=dk, dv=dv, rank=rank)
    return pl.pallas_call(
        kern,
        grid=(batch, ns),
        in_specs=[
            pl.BlockSpec((ts, za.shape[1]), lambda b, s: (b * ns + s, 0)),
            pl.BlockSpec((ts, LANES), lambda b, s: (b * ns + s, 0)),
            _const_spec((LANES, dk)),
            _const_spec((1, dk)),
            _const_spec((1, dv)),
        ],
        out_specs=pl.BlockSpec((ts, dv), lambda b, s: (b * ns + s, 0)),
        out_shape=jax.ShapeDtypeStruct((N, dv), BF16),
        scratch_shapes=[
            pltpu.VMEM((heads, dv // heads, dk // heads), F32),
            pltpu.VMEM((ts, 2 * dk), BF16),
        ],
        compiler_params=_params(("arbitrary", "arbitrary")),
        name="gla",
    )(za, zg, gw2, gb, ng)


def _mixout_kernel(og_ref, zb_ref, x_ref, mod_ref, wga_ref, wco_ref, wo_ref, n2_ref,
                   wr_ref, br_ref, x1_ref, h2_ref, logit_ref):
    tm, d = x_ref.shape
    m = mod_ref[0]
    sub = MIX_SUB
    gain2 = n2_ref[...] * (1.0 + m[4:5])
    for r0 in range(0, tm, sub):
        rs = slice(r0, r0 + sub)
        ga = zb_ref[rs, d:2 * d].astype(F32)
        gc = zb_ref[rs, 2 * d:3 * d].astype(F32)
        y_conv = _dot(zb_ref[rs, 0:d], wco_ref[...])
        y_gla = _dot(og_ref[rs, :], wga_ref[...])
        y = jax.nn.sigmoid(ga) * y_gla + jax.nn.sigmoid(gc) * y_conv
        y = _dot(y.astype(BF16), wo_ref[...])
        x1 = x_ref[rs, :] + m[2:3] * y
        x1_ref[rs, :] = x1

        ms = jnp.mean(x1 * x1, axis=-1, keepdims=True)
        h2 = x1 * lax.rsqrt(ms + NORM_EPS) * gain2 + m[3:4]
        h_hi, h_r = _round_bf16(h2)
        for p, words in enumerate(_pack_rows(h_r)):
            h2_ref[p, rs, :] = words

        h_lo = (h2 - h_r).astype(BF16)
        two = _dot(h_hi, wr_ref[...])
        logits = two[:, 0:LANES] + two[:, LANES:2 * LANES] + _dot(h_lo, wr_ref[:, 0:LANES]) + br_ref[...]
        logit_ref[rs, :] = logits


def _mixer_out(og, zb, x, modl, wga, wco, wo, n2g, wr, br, batch, seq):
    N, D = x.shape
    tm = MIX_TILE
    ns = seq // tm
    tok = lambda b, s: (b * ns + s, 0)
    return pl.pallas_call(
        _mixout_kernel,
        grid=(batch, ns),
        in_specs=[
            pl.BlockSpec((tm, D), tok),
            pl.BlockSpec((tm, zb.shape[1]), tok),
            pl.BlockSpec((tm, D), tok),
            pl.BlockSpec((1, 6, D), lambda b, s: (b, 0, 0)),
            _const_spec((D, D)),
            _const_spec((D, D)),
            _const_spec((D, D)),
            _const_spec((1, D)),
            _const_spec((D, 2 * LANES)),
            _const_spec((1, LANES)),
        ],
        out_specs=[
            pl.BlockSpec((tm, D), tok),
            pl.BlockSpec((D // 2 // ROW_PIECE, tm, ROW_PIECE), lambda b, s: (0, b * ns + s, 0)),
            pl.BlockSpec((tm, LANES), tok),
        ],
        out_shape=[
            jax.ShapeDtypeStruct((N, D), F32),
            jax.ShapeDtypeStruct((D // 2 // ROW_PIECE, N, ROW_PIECE), U32),
            jax.ShapeDtypeStruct((N, LANES), F32),
        ],
        compiler_params=_params(("arbitrary", "arbitrary")),
        name="mixer_out",
    )(og, zb, x, modl, wga, wco, wo, n2g, wr, br)


def _route_kernel(logit_ref, routet_ref, route_ref, cnt_ref, run_ref):
    tm = logit_ref.shape[0]
    sub = MIX_SUB
    rows8 = EXPERTS_PER_GROUP

    @pl.when(pl.program_id(0) == 0)
    def _():
        run_ref[...] = jnp.zeros_like(run_ref)

    sub8 = lax.broadcasted_iota(jnp.int32, (rows8, sub), 0)
    erow = lax.broadcasted_iota(jnp.int32, (N_EXPERTS, sub), 0)
    tr = lax.broadcasted_iota(jnp.int32, (sub, sub), 0)
    tc = lax.broadcasted_iota(jnp.int32, (sub, sub), 1)
    earlier = (tr < tc).astype(BF16)
    ones = jnp.ones((sub, LANES), BF16)
    neg = -jnp.inf
    run = run_ref[...]
    for r0 in range(0, tm, sub):
        lt = logit_ref[r0:r0 + sub, :].T
        gl = lt[0:N_GROUPS, :]
        gmax = jnp.max(gl, axis=0, keepdims=True)
        gsum = jnp.sum(jnp.exp(gl - gmax), axis=0, keepdims=True)
        g_w = 1.0 / gsum
        g_idx = jnp.min(jnp.where(gl == gmax, sub8, N_GROUPS), axis=0, keepdims=True)
        el = lt[N_GROUPS:N_GROUPS + rows8, :]
        for g in range(1, N_GROUPS):
            el = jnp.where(g_idx == g, lt[N_GROUPS + g * rows8:N_GROUPS + (g + 1) * rows8, :], el)
        e1 = jnp.max(el, axis=0, keepdims=True)
        i1 = jnp.min(jnp.where(el == e1, sub8, rows8), axis=0, keepdims=True)
        el2 = jnp.where(sub8 == i1, neg, el)
        e2 = jnp.max(el2, axis=0, keepdims=True)
        i2 = jnp.min(jnp.where(el2 == e2, sub8, rows8), axis=0, keepdims=True)
        ratio = jnp.exp(e2 - e1)
        w1 = g_w / (1.0 + ratio)
        w2 = g_w * ratio / (1.0 + ratio)
        id1 = g_idx * rows8 + i1
        id2 = g_idx * rows8 + i2

        oh1 = erow == id1
        oh2 = erow == id2
        oh1b = jnp.where(oh1, 1.0, 0.0).astype(BF16)
        oh2b = jnp.where(oh2, 1.0, 0.0).astype(BF16)
        tot1 = _dot(oh1b, ones)
        tot2 = _dot(oh2b, ones)
        base1 = jnp.concatenate([run] * (sub // LANES), axis=1)
        base2 = jnp.concatenate([run + tot1] * (sub // LANES), axis=1)
        c1 = _dot(oh1b, earlier) + base1
        c2 = _dot(oh2b, earlier) + base2
        rank1 = jnp.sum(jnp.where(oh1, c1, 0.0), axis=0, keepdims=True)
        rank2 = jnp.sum(jnp.where(oh2, c2, 0.0), axis=0, keepdims=True)
        run = run + tot1 + tot2

        rec = jnp.where(sub8 == R_ID1, id1.astype(F32), 0.0)
        rec = jnp.where(sub8 == R_ID2, id2.astype(F32), rec)
        rec = jnp.where(sub8 == R_W1, w1, rec)
        rec = jnp.where(sub8 == R_W2, w2, rec)
        rec = jnp.where(sub8 == R_RANK1, rank1, rec)
        rec = jnp.where(sub8 == R_RANK2, rank2, rec)
        routet_ref[:, r0:r0 + sub] = rec
        rec_full = jnp.concatenate([rec, jnp.zeros((LANES - rows8, sub), F32)], axis=0)
        route_ref[r0:r0 + sub, :] = rec_full.T
    run_ref[...] = run
    cnt_ref[...] = run


def _routing(logits):
    N = logits.shape[0]
    tm = 2048 if N % 2048 == 0 else MIX_SUB
    return pl.pallas_call(
        _route_kernel,
        grid=(N // tm,),
        in_specs=[pl.BlockSpec((tm, LANES), lambda i: (i, 0))],
        out_specs=[
            pl.BlockSpec((8, tm), lambda i: (0, i)),
            pl.BlockSpec((tm, LANES), lambda i: (i, 0)),
            pl.BlockSpec((N_EXPERTS, LANES), lambda i: (0, 0)),
        ],
        out_shape=[
            jax.ShapeDtypeStruct((8, N), F32),
            jax.ShapeDtypeStruct((N, LANES), F32),
            jax.ShapeDtypeStruct((N_EXPERTS, LANES), F32),
        ],
        scratch_shapes=[pltpu.VMEM((N_EXPERTS, LANES), F32)],
        compiler_params=_params(("arbitrary",)),
        name="moe_route",
    )(logits)


SC_WINDOW = 128


def _sc_mesh():
    return plsc.VectorSubcoreMesh(core_axis_name="core", subcore_axis_name="subcore")


def _piece_index(rows, pieces, n_rows):
    return (jnp.arange(pieces, dtype=jnp.int32)[:, None] * n_rows + rows[None, :]).reshape(1, -1)


def _sc_dispatch(h2, dests, n_rows):
    pieces, N, pc = h2.shape
    w = SC_WINDOW

    @functools.partial(pl.kernel, out_type=jax.ShapeDtypeStruct((pieces * n_rows, pc), h2.dtype),
                       mesh=_sc_mesh(), scratch_types=[], name="moe_dispatch_sc")
    def run(x_hbm, *refs):
        o_hbm = refs[-1]

        def body(x_vmem, *idx_vmem):
            for i_vmem in idx_vmem:
                pltpu.sync_copy(x_vmem, o_hbm.at[i_vmem.at[0]])

        pltpu.emit_pipeline(
            body,
            grid=(pieces * N // w,),
            in_specs=[pl.BlockSpec((w, pc), lambda i: (i, 0))]
            + [pl.BlockSpec((1, w), lambda i: (0, i))] * len(dests),
            out_specs=[],
            core_axis_name=("core", "subcore"),
            dimension_semantics=(pltpu.PARALLEL,),
        )(x_hbm, *refs[:-1])

    xs = run(h2.reshape(pieces * N, pc), *[_piece_index(d, pieces, n_rows) for d in dests])
    return xs.reshape(pieces, n_rows, pc)


def _sc_return(yb, dest):
    M = dest.shape[0]
    pieces, P, pc = yb.shape
    w = SC_WINDOW

    @functools.partial(pl.kernel, out_type=jax.ShapeDtypeStruct((pieces * M, pc), yb.dtype),
                       mesh=_sc_mesh(), scratch_types=[], name="moe_return_sc")
    def run(y_hbm, i_hbm, o_hbm):
        def body(i_vmem, o_vmem):
            pltpu.sync_copy(y_hbm.at[i_vmem.at[0]], o_vmem)

        pltpu.emit_pipeline(
            body,
            grid=(pieces * M // w,),
            in_specs=[pl.BlockSpec((1, w), lambda i: (0, i))],
            out_specs=[pl.BlockSpec((w, pc), lambda i: (i, 0))],
            core_axis_name=("core", "subcore"),
            dimension_semantics=(pltpu.PARALLEL,),
        )(i_hbm, o_hbm)

    return run(yb.reshape(pieces * P, pc), _piece_index(dest, pieces, P)).reshape(pieces, M, pc)


def _expert_kernel(ib_ref, ie_ref, nxt_ref, slot_ref, nb_ref, xs_ref, w1_ref, w3_ref, w2_ref, y_ref,
                   wf1, wf3, wf2, w1b, w3b, w2b, sem, *, layer):
    j = pl.program_id(0)
    jp = jnp.maximum(j - 1, 0)
    live = j < nb_ref[0]
    new_expert = jnp.logical_or(j == 0, ie_ref[j] != ie_ref[jp])
    streams = ((w1_ref, wf1), (w3_ref, wf3), (w2_ref, wf2))

    def weight_copies(expert, s):
        return [pltpu.make_async_copy(w.at[layer, expert], buf.at[s], sem.at[s, k])
                for k, (w, buf) in enumerate(streams)]

    @pl.when(j == 0)
    def _():
        for cp in weight_copies(ie_ref[0], 0):
            cp.start()

    @pl.when(jnp.logical_and(new_expert, live))
    def _():
        s = slot_ref[j]
        for cp in weight_copies(ie_ref[j], s):
            cp.wait()
        w1b[...] = wf1[s].astype(BF16)
        w3b[...] = wf3[s].astype(BF16)
        w2b[...] = wf2[s].astype(BF16)

        @pl.when(nxt_ref[j] >= 0)
        def _():
            for cp in weight_copies(nxt_ref[j], 1 - s):
                cp.start()

    @pl.when(live)
    def _():
        xb = _unpack_rows(xs_ref).astype(BF16)
        h1 = _dot(xb, w1b[...])
        h3 = _dot(xb, w3b[...])
        a = (h1 * jax.nn.sigmoid(h1) * h3).astype(BF16)
        for p, words in enumerate(_pack_rows(_round_bf16(_dot(a, w2b[...]))[1])):
            y_ref[p] = words


def _experts(xs, items, w1, w3, w2, layer):
    pieces, P, pc = xs.shape
    _, E, D, DE = w1.shape
    bm = MOE_BLOCK
    n_items_max = P // bm
    rows = lambda j, *prefetch: (0, prefetch[0][j], 0)
    return pl.pallas_call(
        functools.partial(_expert_kernel, layer=layer),
        grid_spec=pltpu.PrefetchScalarGridSpec(
            num_scalar_prefetch=5,
            grid=(n_items_max,),
            in_specs=[
                pl.BlockSpec((pieces, bm, pc), rows),
                pl.BlockSpec(memory_space=pl.ANY),
                pl.BlockSpec(memory_space=pl.ANY),
                pl.BlockSpec(memory_space=pl.ANY),
            ],
            out_specs=pl.BlockSpec((pieces, bm, pc), rows),
            scratch_shapes=[
                pltpu.VMEM((2, D, DE), F32),
                pltpu.VMEM((2, D, DE), F32),
                pltpu.VMEM((2, DE, D), F32),
                pltpu.VMEM((D, DE), BF16),
                pltpu.VMEM((D, DE), BF16),
                pltpu.VMEM((DE, D), BF16),
                pltpu.SemaphoreType.DMA((2, 3)),
            ],
        ),
        out_shape=jax.ShapeDtypeStruct((pieces, P, pc), U32),
        compiler_params=_params(("arbitrary",)),
        name="moe_experts",
    )(*items, xs, w1, w3, w2)


def _combine_kernel(x1_ref, route_ref, mod_ref, fg_ref, y0_ref, y1_ref, *rest):
    o_ref = rest[-1]
    x2 = x1_ref[...] + mod_ref[0][5:6] * _moe_mix(route_ref, y0_ref, y1_ref)
    ms = jnp.mean(x2 * x2, axis=-1, keepdims=True)
    o_ref[...] = x2 * lax.rsqrt(ms + NORM_EPS) * fg_ref[...]


def _final_combine(x1, route, modl, final_g, yk, seq, first_tile, prev_out):
    N, D = x1.shape
    tm = ROW_TILE
    nt = yk.shape[1] // 2 // tm
    tok = lambda i: (first_tile + i, 0)
    in_specs = [
        pl.BlockSpec((tm, D), tok),
        pl.BlockSpec((tm, LANES), tok),
        pl.BlockSpec((1, 6, D), lambda i: (((first_tile + i) * tm) // seq, 0, 0)),
        pl.BlockSpec((1, D), lambda i: (0, 0)),
        pl.BlockSpec((yk.shape[0], tm, ROW_PIECE), lambda i: (0, i, 0)),
        pl.BlockSpec((yk.shape[0], tm, ROW_PIECE), lambda i: (0, nt + i, 0)),
    ]
    args = [x1, route, modl, final_g, yk, yk]
    aliases = {}
    if prev_out is not None:
        in_specs.append(pl.BlockSpec(memory_space=pl.ANY))
        args.append(prev_out)
        aliases = {len(args) - 1: 0}
    return pl.pallas_call(
        _combine_kernel,
        grid=(nt,),
        in_specs=in_specs,
        out_specs=pl.BlockSpec((tm, D), tok),
        out_shape=jax.ShapeDtypeStruct((N, D), F32),
        input_output_aliases=aliases,
        compiler_params=_params(("arbitrary",)),
        name="moe_combine",
    )(*args)


ITEM_LANES = 256
I_BLOCK, I_EXPERT, I_NEXT, I_SLOT, I_COUNT = 0, 1, 2, 3, 4


def _dest_kernel(routet_ref, cnt_ref, d1_ref, d2_ref, d3_ref, items_ref):
    tm = routet_ref.shape[1]
    ne = N_EXPERTS
    bm = float(MOE_BLOCK)
    cnt = cnt_ref[...]
    r = lax.broadcasted_iota(jnp.int32, (ne, ne), 0)
    c = lax.broadcasted_iota(jnp.int32, (ne, ne), 1)
    lower = (r >= c).astype(F32)
    cumsum = lambda a: jnp.dot(lower, a, precision=HIGHEST, preferred_element_type=F32)
    padded = jnp.floor((cnt + (bm - 1.0)) * (1.0 / bm)) * bm
    pend = cumsum(padded)
    pstart = pend - padded
    npad = padded - cnt
    pad_end = cumsum(npad)
    pad_start = pad_end - npad

    tile = lambda a, n: jnp.concatenate([a] * (n // LANES), axis=1)
    expert = lax.broadcasted_iota(jnp.int32, (ne, tm), 0).astype(F32)
    rec = routet_ref[...]
    pstart_t = tile(pstart, tm)

    def sorted_row(row_id, row_rank):
        sel = expert == rec[row_id:row_id + 1, :]
        return jnp.sum(jnp.where(sel, pstart_t, 0.0), axis=0, keepdims=True) + rec[row_rank:row_rank + 1, :]

    dest1 = sorted_row(R_ID1, R_RANK1)
    dest2 = sorted_row(R_ID2, R_RANK2)
    t = (lax.broadcasted_iota(jnp.int32, (1, tm), 1) + pl.program_id(0) * tm).astype(F32)
    ps_t, pe_t = tile(pad_start, tm), tile(pad_end, tm)
    in_e = jnp.logical_and(ps_t <= t, t < pe_t)
    pad_row = jnp.sum(jnp.where(in_e, tile(pstart + cnt, tm) + (t - ps_t), 0.0), axis=0, keepdims=True)
    dest3 = jnp.where(t < pe_t[ne - 1:ne, :], pad_row, dest1)
    d1_ref[0] = dest1.astype(jnp.int32)
    d2_ref[0] = dest2.astype(jnp.int32)
    d3_ref[0] = dest3.astype(jnp.int32)

    @pl.when(pl.program_id(0) == 0)
    def _():
        nl = ITEM_LANES
        pend_i = tile(pend, nl)
        n_items = pend_i[ne - 1:ne, :] * (1.0 / bm)
        blk = jnp.minimum(lax.broadcasted_iota(jnp.int32, (1, nl), 1).astype(F32), n_items - 1.0)
        erow = lax.broadcasted_iota(jnp.int32, (ne, nl), 0).astype(F32)
        ie = jnp.sum(jnp.where(pend_i <= blk * bm, 1.0, 0.0), axis=0, keepdims=True)
        nonempty = tile(cnt, nl) > 0.0
        order = jnp.sum(jnp.where(jnp.logical_and(nonempty, erow < ie), 1.0, 0.0), axis=0, keepdims=True)
        nxt = jnp.min(jnp.where(jnp.logical_and(nonempty, erow > ie), erow, float(ne)), axis=0, keepdims=True)
        nxt = jnp.where(nxt == float(ne), -1.0, nxt)
        slot = order - 2.0 * jnp.floor(order * 0.5)
        row = lax.broadcasted_iota(jnp.int32, (8, nl), 0)
        tab = jnp.where(row == I_BLOCK, blk, 0.0)
        tab = jnp.where(row == I_EXPERT, ie, tab)
        tab = jnp.where(row == I_NEXT, nxt, tab)
        tab = jnp.where(row == I_SLOT, slot, tab)
        tab = jnp.where(row == I_COUNT, n_items, tab)
        items_ref[...] = tab.astype(jnp.int32)


def _destinations(route_t, cnt, n_blocks):
    N = route_t.shape[1]
    tm = 2048 if N % 2048 == 0 else MIX_SUB
    out = jax.ShapeDtypeStruct((N // tm, 1, tm), jnp.int32)
    d1, d2, d3, items = pl.pallas_call(
        _dest_kernel,
        grid=(N // tm,),
        in_specs=[pl.BlockSpec((8, tm), lambda i: (0, i)),
                  pl.BlockSpec((N_EXPERTS, LANES), lambda i: (0, 0))],
        out_specs=[pl.BlockSpec((1, 1, tm), lambda i: (i, 0, 0))] * 3
        + [pl.BlockSpec((8, ITEM_LANES), lambda i: (0, 0))],
        out_shape=[out, out, out, jax.ShapeDtypeStruct((8, ITEM_LANES), jnp.int32)],
        compiler_params=_params(("arbitrary",)),
        name="moe_dest",
    )(route_t, cnt)
    table = tuple(items[k, :n_blocks] for k in (I_BLOCK, I_EXPERT, I_NEXT, I_SLOT)) + (items[I_COUNT, :1],)
    return d1.reshape(N), d2.reshape(N), d3.reshape(N), table


def kernel(x, c, mod_w, mod_b, norm1_g, w_in, gate_w2, gate_b, gla_norm_g, conv_w, w_gla_out,
           w_conv_out, w_out, norm2_g, router_group_w, router_group_b, router_expert_w,
           router_expert_b, expert_w1, expert_w3, expert_w2, final_norm_g):
    B, S, D = x.shape
    L = mod_w.shape[0]
    N = B * S
    dk = gate_w2.shape[2]
    rank = gate_w2.shape[1]
    dv = gla_norm_g.shape[1]
    n_slots = N * 2
    n_blocks = n_slots // MOE_BLOCK + N_EXPERTS
    assert S % GLA_TILE == 0 and S % MIX_TILE == 0 and S % IN_TILE == 0
    assert N % (FINAL_PARTS * ROW_TILE) == 0
    assert n_slots % MOE_BLOCK == 0 and MOE_BLOCK & (MOE_BLOCK - 1) == 0 and n_blocks <= ITEM_LANES
    assert N_EXPERTS * (MOE_BLOCK - 1) <= N
    assert N_GROUPS + N_EXPERTS <= LANES and EXPERTS_PER_GROUP == 8 and 3 * rank <= LANES and D % (2 * ROW_PIECE) == 0

    mod = _modulation(c, mod_w, mod_b)
    xf = x.reshape(N, D)
    wa, wb, wg = _prep_in_weights(w_in, 2 * dk + dv, rank, dv)
    for l in range(L):
        modl = mod[l].reshape(B, 6, D)
        g_hi = gate_w2[l].astype(BF16)
        g_lo = (gate_w2[l] - g_hi.astype(F32)).astype(BF16)
        gw2 = jnp.pad(jnp.concatenate([g_hi, g_hi, g_lo], axis=0), ((0, LANES - 3 * rank), (0, 0)))
        if l == 0:
            za, zb, zg = _in_projection(xf, modl, norm1_g[l].reshape(1, D), conv_w[l], wa, wb, wg, l, S)
        else:
            xf, za, zb, zg = _in_projection(x1, modl, norm1_g[l].reshape(1, D), conv_w[l], wa, wb, wg, l, S,
                                            combine=(route, yk, mod[l - 1].reshape(B, 6, D)))
        og = _gla(za, zg, gw2, gate_b[l].reshape(1, dk), gla_norm_g[l].reshape(1, dv), B, S, dk, dv, rank)
        wr = jnp.pad(jnp.concatenate([router_group_w[l], router_expert_w[l]], axis=1),
                     ((0, 0), (0, LANES - N_GROUPS - N_EXPERTS)))
        wr_hi = wr.astype(BF16)
        wr = jnp.concatenate([wr_hi, (wr - wr_hi.astype(F32)).astype(BF16)], axis=1)
        br = jnp.pad(jnp.concatenate([router_group_b[l], router_expert_b[l]]),
                     (0, LANES - N_GROUPS - N_EXPERTS)).reshape(1, LANES)
        x1, h2, logits = _mixer_out(
            og, zb, xf, modl, w_gla_out[l].astype(BF16), w_conv_out[l].astype(BF16),
            w_out[l].astype(BF16), norm2_g[l].reshape(1, D), wr, br, B, S)
        route_t, route, cnt = _routing(logits)
        dest1, dest2, dest3, items = _destinations(route_t, cnt, n_blocks)
        xs = _sc_dispatch(h2, (dest1, dest2, dest3), n_blocks * MOE_BLOCK)
        yb = _experts(xs, items, expert_w1, expert_w3, expert_w2, l)
        if l < L - 1:
            yk = _sc_return(yb, jnp.concatenate([dest1, dest2]))
    out = None
    part = N // FINAL_PARTS
    for p in range(FINAL_PARTS):
        tok = slice(p * part, (p + 1) * part)
        yk = _sc_return(yb, jnp.concatenate([dest1[tok], dest2[tok]]))
        out = _final_combine(x1, route, modl, final_norm_g.reshape(1, D), yk, S,
                             p * part // ROW_TILE, out)
    return out.reshape(B, S, D)
```

```python
import functools

import jax
import jax.numpy as jnp
from jax import lax
from jax.experimental import pallas as pl
from jax.experimental.pallas import tpu as pltpu
from jax.experimental.pallas import tpu_sc as plsc

F32 = jnp.float32
BF16 = jnp.bfloat16
HIGHEST = lax.Precision.HIGHEST

GLA_HEADS = 4
GATE_TAU = 16.0
GLA_CHUNK = 64
N_GROUPS = 8
EXPERTS_PER_GROUP = 8
N_EXPERTS = N_GROUPS * EXPERTS_PER_GROUP
NORM_EPS = 1e-6

LANES = 128
VMEM_LIMIT_BYTES = 56 * 1024 * 1024

IN_TILE = 512
GLA_TILE = 512
MIX_TILE = 512
MIX_SUB = 256
MOE_BLOCK = 512
ROW_TILE = 1024
FINAL_PARTS = 2
ROW_PIECE = 256
U32 = jnp.uint32

R_ID1, R_ID2, R_W1, R_W2, R_RANK1, R_RANK2 = 0, 1, 2, 3, 4, 5


def _dot(a, b):
    return jnp.dot(a, b, preferred_element_type=F32)


def _round_bf16(x):
    xb = x.astype(BF16)
    return xb, xb.astype(F32)


def _pack_rows(xr):
    half = xr.shape[1] // 2
    out = []
    for p in range(half // ROW_PIECE):
        lo = xr[:, p * ROW_PIECE:(p + 1) * ROW_PIECE]
        hi = xr[:, half + p * ROW_PIECE:half + (p + 1) * ROW_PIECE]
        out.append((pltpu.bitcast(lo, U32) >> 16) | pltpu.bitcast(hi, U32))
    return out


def _unpack_rows(ref):
    words = [ref[p] for p in range(ref.shape[0])]
    lo = [pltpu.bitcast(w << 16, F32) for w in words]
    hi = [pltpu.bitcast(w & jnp.uint32(0xFFFF0000), F32) for w in words]
    return jnp.concatenate(lo + hi, axis=-1)


def _const_spec(shape):
    nd = len(shape)
    return pl.BlockSpec(shape, lambda *_: (0,) * nd, pipeline_mode=pl.Buffered(1))


def _params(sem):
    return pltpu.CompilerParams(dimension_semantics=sem, vmem_limit_bytes=VMEM_LIMIT_BYTES)


MOD_ROWS = 256


def _mod_kernel(c_ref, w_ref, b_ref, o_ref):
    c = c_ref[...]
    sc = c * jax.nn.sigmoid(c)
    part = jnp.dot(sc, w_ref[0], precision=HIGHEST, preferred_element_type=F32)

    @pl.when(pl.program_id(1) == 0)
    def _():
        o_ref[0] = part + b_ref[0]

    @pl.when(pl.program_id(1) > 0)
    def _():
        o_ref[0] += part


def _modulation(c, mod_w, mod_b):
    L, D, D6 = mod_w.shape
    B = c.shape[0]
    kb = MOD_ROWS
    return pl.pallas_call(
        _mod_kernel,
        grid=(L, D // kb),
        in_specs=[
            pl.BlockSpec((B, kb), lambda l, k: (0, k)),
            pl.BlockSpec((1, kb, D6), lambda l, k: (l, k, 0)),
            pl.BlockSpec((1, 1, D6), lambda l, k: (l, 0, 0)),
        ],
        out_specs=pl.BlockSpec((1, B, D6), lambda l, k: (l, 0, 0)),
        out_shape=jax.ShapeDtypeStruct((L, B, D6), F32),
        compiler_params=_params(("arbitrary", "arbitrary")),
        name="adaln_mod",
    )(c, mod_w, mod_b.reshape(L, 1, D6))


PREP_ROWS = 256


def _prep_kernel(w_ref, wa_ref, wb_ref, wg_ref, *, o_gd, rank, dv):
    w = w_ref[0]
    o_r = o_gd + rank
    o_b = o_r + dv
    wa_ref[0, :, 0:o_gd] = w[:, 0:o_gd].astype(BF16)
    wa_ref[0, :, o_gd:o_gd + dv] = w[:, o_r:o_b].astype(BF16)
    wb_ref[0] = w[:, o_b:].astype(BF16)
    g = w[:, o_gd:o_r]
    pad = jnp.zeros((w.shape[0], LANES - 3 * rank), F32)
    wg_ref[0] = jnp.concatenate([g, g, g, pad], axis=1).astype(BF16)


def _prep_in_weights(w_in, layer, o_gd, rank, dv):
    _, D, cols = w_in.shape
    ca = o_gd + dv
    cb = cols - o_gd - rank - dv
    kb = PREP_ROWS
    blk = lambda n: pl.BlockSpec((1, kb, n), lambda k: (0, k, 0))
    return pl.pallas_call(
        functools.partial(_prep_kernel, o_gd=o_gd, rank=rank, dv=dv),
        grid=(D // kb,),
        in_specs=[pl.BlockSpec((1, kb, cols), lambda k: (layer, k, 0))],
        out_specs=[blk(ca), blk(cb), blk(LANES)],
        out_shape=[
            jax.ShapeDtypeStruct((1, D, ca), BF16),
            jax.ShapeDtypeStruct((1, D, cb), BF16),
            jax.ShapeDtypeStruct((1, D, LANES), BF16),
        ],
        compiler_params=_params(("arbitrary",)),
        name="in_weights",
    )(w_in)


CONV_COLS = 256


def _moe_mix(route_ref, y0_ref, y1_ref):
    rec = route_ref[...]
    w1 = rec[:, R_W1:R_W1 + 1]
    w2 = rec[:, R_W2:R_W2 + 1]
    return w1 * _unpack_rows(y0_ref) + w2 * _unpack_rows(y1_ref)


def _inproj_kernel(*refs, seq, fuse_combine):
    if fuse_combine:
        (x_ref, route_ref, y0_ref, y1_ref, modp_ref, mod_ref, g_ref, cw_ref, wa_ref, wb_ref, wg_ref,
         xo_ref, za_ref, zb_ref, zg_ref, carry_ref) = refs
        x = x_ref[...] + modp_ref[0][5:6] * _moe_mix(route_ref, y0_ref, y1_ref)
        xo_ref[...] = x
    else:
        (x_ref, mod_ref, g_ref, cw_ref, wa_ref, wb_ref, wg_ref, za_ref, zb_ref, zg_ref, carry_ref) = refs
        x = x_ref[...]
    tm, d = x.shape

    @pl.when((pl.program_id(0) * tm) % seq == 0)
    def _():
        carry_ref[...] = jnp.zeros_like(carry_ref)

    ms = jnp.mean(x * x, axis=-1, keepdims=True)
    m = mod_ref[0]
    h = (x * lax.rsqrt(ms + NORM_EPS) * (g_ref[...] * (1.0 + m[1:2])) + m[0:1]).astype(BF16)
    for j in range(za_ref.shape[1] // d):
        za_ref[:, j * d:(j + 1) * d] = _dot(h, wa_ref[:, j * d:(j + 1) * d]).astype(BF16)
    zg_ref[...] = _dot(h, wg_ref[...])
    for j in range(1, zb_ref.shape[1] // d):
        zb_ref[:, j * d:(j + 1) * d] = _dot(h, wb_ref[:, (j + 2) * d:(j + 3) * d]).astype(BF16)

    w = CONV_COLS
    rowi = lax.broadcasted_iota(jnp.int32, (tm, w), 0)
    for j in range(d // w):
        cols = slice(j * w, (j + 1) * w)
        cb = _dot(h, wb_ref[:, j * w:(j + 1) * w])
        cc = _dot(h, wb_ref[:, d + j * w:d + (j + 1) * w])
        ch = _dot(h, wb_ref[:, 2 * d + j * w:2 * d + (j + 1) * w])
        u = cc * ch
        prev = carry_ref[:, cols]
        u1 = jnp.where(rowi == 0, prev[7:8], pltpu.roll(u, 1, 0))
        u2 = jnp.where(rowi == 0, prev[6:7], jnp.where(rowi == 1, prev[7:8], pltpu.roll(u, 2, 0)))
        carry_ref[:, cols] = u[tm - 8:tm]
        conv = cw_ref[0:1, cols] * u2 + cw_ref[1:2, cols] * u1 + cw_ref[2:3, cols] * u
        zb_ref[:, cols] = (cb * conv).astype(BF16)


def _in_projection(x, modl, norm_g, conv_w, wa, wb, wg, seq, combine=None):
    N, D = x.shape
    tm = IN_TILE
    nt = N // tm
    ca, cb = wa.shape[2], wb.shape[2] - 2 * D
    layer_w = lambda w: pl.BlockSpec((None,) + w.shape[1:], lambda i: (0, 0, 0),
                                     pipeline_mode=pl.Buffered(1))
    tok = lambda i: (i, 0)
    per_batch = lambda i: ((i * tm) // seq, 0, 0)
    in_specs = [pl.BlockSpec((tm, D), tok)]
    args = [x]
    out_specs, out_shape = [], []
    if combine is not None:
        route, yk, mod_prev = combine
        piece_blk = (yk.shape[0], tm, ROW_PIECE)
        in_specs += [
            pl.BlockSpec((tm, LANES), tok),
            pl.BlockSpec(piece_blk, lambda i: (0, i, 0)),
            pl.BlockSpec(piece_blk, lambda i: (0, nt + i, 0)),
            pl.BlockSpec((1, 6, D), per_batch),
        ]
        args += [route, yk, yk, mod_prev]
        out_specs.append(pl.BlockSpec((tm, D), tok))
        out_shape.append(jax.ShapeDtypeStruct((N, D), F32))
    in_specs += [
        pl.BlockSpec((1, 6, D), per_batch),
        _const_spec((1, D)),
        _const_spec(conv_w.shape),
        layer_w(wa),
        layer_w(wb),
        layer_w(wg),
    ]
    args += [modl, norm_g, conv_w, wa, wb, wg]
    out_specs += [
        pl.BlockSpec((tm, ca), tok),
        pl.BlockSpec((tm, cb), tok),
        pl.BlockSpec((tm, LANES), tok),
    ]
    out_shape += [
        jax.ShapeDtypeStruct((N, ca), BF16),
        jax.ShapeDtypeStruct((N, cb), BF16),
        jax.ShapeDtypeStruct((N, LANES), F32),
    ]
    kern = functools.partial(_inproj_kernel, seq=seq, fuse_combine=combine is not None)
    return pl.pallas_call(
        kern,
        grid=(nt,),
        in_specs=in_specs,
        out_specs=out_specs,
        out_shape=out_shape,
        scratch_shapes=[pltpu.VMEM((8, D), F32)],
        compiler_params=_params(("arbitrary",)),
        name="in_projection",
    )(*args)


def _log_sigmoid(x):
    return jnp.minimum(x, 0.0) - jnp.log(1.0 + jnp.exp(-jnp.abs(x)))


def _gla_kernel(za_ref, zg_ref, gw2_ref, gb_ref, ng_ref, o_ref, st_ref, lg_ref, *, dk, dv, rank):
    heads = GLA_HEADS
    dkh, dvh = dk // heads, dv // heads
    c = GLA_CHUNK
    ts = za_ref.shape[0]

    @pl.when(pl.program_id(1) == 0)
    def _():
        st_ref[...] = jnp.zeros_like(st_ref)

    zg = zg_ref[...]
    zg_hi = zg.astype(BF16)
    zg_lo = (zg - zg_hi.astype(F32)).astype(BF16)
    lane = lax.broadcasted_iota(jnp.int32, zg.shape, 1)
    lhs = jnp.where(jnp.logical_and(lane >= rank, lane < 2 * rank), zg_lo, zg_hi)
    pre = _dot(lhs, gw2_ref[...]) + gb_ref[...]
    lg = _log_sigmoid(pre) * (1.0 / GATE_TAU)
    lg_hi = lg.astype(BF16)
    lg_ref[:, 0:dk] = lg_hi
    lg_ref[:, dk:2 * dk] = (lg - lg_hi.astype(F32)).astype(BF16)

    row = lax.broadcasted_iota(jnp.int32, (c, c), 0)
    col = lax.broadcasted_iota(jnp.int32, (c, c), 1)
    causal = row >= col
    tril = causal.astype(BF16)
    qscale = dkh ** -0.5

    for ci in range(ts // c):
        rows = slice(ci * c, (ci + 1) * c)
        b_two = _dot(tril, lg_ref[rows, :])
        b_all = b_two[:, 0:dk] + b_two[:, dk:2 * dk]
        for hd in range(heads):
            ks = slice(hd * dkh, (hd + 1) * dkh)
            q = za_ref[rows, hd * dkh:(hd + 1) * dkh].astype(F32) * qscale
            k = za_ref[rows, dk + hd * dkh:dk + (hd + 1) * dkh].astype(F32)
            v = za_ref[rows, 2 * dk + hd * dvh:2 * dk + (hd + 1) * dvh]
            r = za_ref[rows, 2 * dk + dv + hd * dvh:2 * dk + dv + (hd + 1) * dvh].astype(F32)
            b = b_all[:, ks]
            b_last = b[c - 1:c, :]
            q_t = (q * jnp.exp(b)).astype(BF16)
            k_t = (k * jnp.exp(-b)).astype(BF16)
            k_s = (k * jnp.exp(b_last - b)).astype(BF16)
            decay = jnp.exp(b_last)
            attn = lax.dot_general(q_t, k_t, (((1,), (1,)), ((), ())), preferred_element_type=F32)
            attn = jnp.where(causal, attn, 0.0).astype(BF16)
            st = st_ref[hd]
            o = _dot(attn, v) + lax.dot_general(
                q_t, st.astype(BF16), (((1,), (1,)), ((), ())), preferred_element_type=F32)
            upd = lax.dot_general(v, k_s, (((0,), (0,)), ((), ())), preferred_element_type=F32)
            st_ref[hd] = st * decay + upd
            ms = jnp.mean(o * o, axis=-1, keepdims=True)
            on = o * lax.rsqrt(ms + NORM_EPS) * ng_ref[:, hd * dvh:(hd + 1) * dvh]
            o_ref[rows, hd * dvh:(hd + 1) * dvh] = (on * (r * jax.nn.sigmoid(r))).astype(BF16)


def _gla(za, zg, gw2, gb, ng, batch, seq, dk, dv, rank):
    N = za.shape[0]
    ts = GLA_TILE
    ns = seq // ts
    heads = GLA_HEADS
    kern = functools.partial(_gla_kernel, dk=dk, dv=dv, rank=rank)
    return pl.pallas_call(
        kern,
        grid=(batch, ns),
        in_specs=[
            pl.BlockSpec((ts, za.shape[1]), lambda b, s: (b * ns + s, 0)),
            pl.BlockSpec((ts, LANES), lambda b, s: (b * ns + s, 0)),
            _const_spec((LANES, dk)),
            _const_spec((1, dk)),
            _const_spec((1, dv)),
        ],
        out_specs=pl.BlockSpec((ts, dv), lambda b, s: (b * ns + s, 0)),
        out_shape=jax.ShapeDtypeStruct((N, dv), BF16),
        scratch_shapes=[
            pltpu.VMEM((heads, dv // heads, dk // heads), F32),
            pltpu.VMEM((ts, 2 * dk), BF16),
        ],
        compiler_params=_params(("arbitrary", "arbitrary")),
        name="gla",
    )(za, zg, gw2, gb, ng)


def _mixout_kernel(og_ref, zb_ref, x_ref, mod_ref, wga_ref, wco_ref, wo_ref, n2_ref,
                   wr_ref, br_ref, x1_ref, h2_ref, logit_ref):
    tm, d = x_ref.shape
    m = mod_ref[0]
    sub = MIX_SUB
    gain2 = n2_ref[...] * (1.0 + m[4:5])
    for r0 in range(0, tm, sub):
        rs = slice(r0, r0 + sub)
        ga = zb_ref[rs, d:2 * d].astype(F32)
        gc = zb_ref[rs, 2 * d:3 * d].astype(F32)
        y_conv = _dot(zb_ref[rs, 0:d], wco_ref[...])
        y_gla = _dot(og_ref[rs, :], wga_ref[...])
        y = jax.nn.sigmoid(ga) * y_gla + jax.nn.sigmoid(gc) * y_conv
        y = _dot(y.astype(BF16), wo_ref[...])
        x1 = x_ref[rs, :] + m[2:3] * y
        x1_ref[rs, :] = x1

        ms = jnp.mean(x1 * x1, axis=-1, keepdims=True)
        h2 = x1 * lax.rsqrt(ms + NORM_EPS) * gain2 + m[3:4]
        h_hi, h_r = _round_bf16(h2)
        for p, words in enumerate(_pack_rows(h_r)):
            h2_ref[p, rs, :] = words

        h_lo = (h2 - h_r).astype(BF16)
        two = _dot(h_hi, wr_ref[...])
        logits = two[:, 0:LANES] + two[:, LANES:2 * LANES] + _dot(h_lo, wr_ref[:, 0:LANES]) + br_ref[...]
        logit_ref[rs, :] = logits


def _mixer_out(og, zb, x, modl, wga, wco, wo, n2g, wr, br, batch, seq):
    N, D = x.shape
    tm = MIX_TILE
    ns = seq // tm
    tok = lambda b, s: (b * ns + s, 0)
    return pl.pallas_call(
        _mixout_kernel,
        grid=(batch, ns),
        in_specs=[
            pl.BlockSpec((tm, D), tok),
            pl.BlockSpec((tm, zb.shape[1]), tok),
            pl.BlockSpec((tm, D), tok),
            pl.BlockSpec((1, 6, D), lambda b, s: (b, 0, 0)),
            _const_spec((D, D)),
            _const_spec((D, D)),
            _const_spec((D, D)),
            _const_spec((1, D)),
            _const_spec((D, 2 * LANES)),
            _const_spec((1, LANES)),
        ],
        out_specs=[
            pl.BlockSpec((tm, D), tok),
            pl.BlockSpec((D // 2 // ROW_PIECE, tm, ROW_PIECE), lambda b, s: (0, b * ns + s, 0)),
            pl.BlockSpec((tm, LANES), tok),
        ],
        out_shape=[
            jax.ShapeDtypeStruct((N, D), F32),
            jax.ShapeDtypeStruct((D // 2 // ROW_PIECE, N, ROW_PIECE), U32),
            jax.ShapeDtypeStruct((N, LANES), F32),
        ],
        compiler_params=_params(("arbitrary", "arbitrary")),
        name="mixer_out",
    )(og, zb, x, modl, wga, wco, wo, n2g, wr, br)


def _route_kernel(logit_ref, routet_ref, route_ref, cnt_ref, run_ref):
    tm = logit_ref.shape[0]
    sub = MIX_SUB
    rows8 = EXPERTS_PER_GROUP

    @pl.when(pl.program_id(0) == 0)
    def _():
        run_ref[...] = jnp.zeros_like(run_ref)

    sub8 = lax.broadcasted_iota(jnp.int32, (rows8, sub), 0)
    erow = lax.broadcasted_iota(jnp.int32, (N_EXPERTS, sub), 0)
    tr = lax.broadcasted_iota(jnp.int32, (sub, sub), 0)
    tc = lax.broadcasted_iota(jnp.int32, (sub, sub), 1)
    earlier = (tr < tc).astype(BF16)
    ones = jnp.ones((sub, LANES), BF16)
    neg = -jnp.inf
    run = run_ref[...]
    for r0 in range(0, tm, sub):
        lt = logit_ref[r0:r0 + sub, :].T
        gl = lt[0:N_GROUPS, :]
        gmax = jnp.max(gl, axis=0, keepdims=True)
        gsum = jnp.sum(jnp.exp(gl - gmax), axis=0, keepdims=True)
        g_w = 1.0 / gsum
        g_idx = jnp.min(jnp.where(gl == gmax, sub8, N_GROUPS), axis=0, keepdims=True)
        el = lt[N_GROUPS:N_GROUPS + rows8, :]
        for g in range(1, N_GROUPS):
            el = jnp.where(g_idx == g, lt[N_GROUPS + g * rows8:N_GROUPS + (g + 1) * rows8, :], el)
        e1 = jnp.max(el, axis=0, keepdims=True)
        i1 = jnp.min(jnp.where(el == e1, sub8, rows8), axis=0, keepdims=True)
        el2 = jnp.where(sub8 == i1, neg, el)
        e2 = jnp.max(el2, axis=0, keepdims=True)
        i2 = jnp.min(jnp.where(el2 == e2, sub8, rows8), axis=0, keepdims=True)
        ratio = jnp.exp(e2 - e1)
        w1 = g_w / (1.0 + ratio)
        w2 = g_w * ratio / (1.0 + ratio)
        id1 = g_idx * rows8 + i1
        id2 = g_idx * rows8 + i2

        oh1 = erow == id1
        oh2 = erow == id2
        oh1b = jnp.where(oh1, 1.0, 0.0).astype(BF16)
        oh2b = jnp.where(oh2, 1.0, 0.0).astype(BF16)
        tot1 = _dot(oh1b, ones)
        tot2 = _dot(oh2b, ones)
        base1 = jnp.concatenate([run] * (sub // LANES), axis=1)
        base2 = jnp.concatenate([run + tot1] * (sub // LANES), axis=1)
        c1 = _dot(oh1b, earlier) + base1
        c2 = _dot(oh2b, earlier) + base2
        rank1 = jnp.sum(jnp.where(oh1, c1, 0.0), axis=0, keepdims=True)
        rank2 = jnp.sum(jnp.where(oh2, c2, 0.0), axis=0, keepdims=True)
        run = run + tot1 + tot2

        rec = jnp.where(sub8 == R_ID1, id1.astype(F32), 0.0)
        rec = jnp.where(sub8 == R_ID2, id2.astype(F32), rec)
        rec = jnp.where(sub8 == R_W1, w1, rec)
        rec = jnp.where(sub8 == R_W2, w2, rec)
        rec = jnp.where(sub8 == R_RANK1, rank1, rec)
        rec = jnp.where(sub8 == R_RANK2, rank2, rec)
        routet_ref[:, r0:r0 + sub] = rec
        rec_full = jnp.concatenate([rec, jnp.zeros((LANES - rows8, sub), F32)], axis=0)
        route_ref[r0:r0 + sub, :] = rec_full.T
    run_ref[...] = run
    cnt_ref[...] = run


def _routing(logits):
    N = logits.shape[0]
    tm = 2048 if N % 2048 == 0 else MIX_SUB
    return pl.pallas_call(
        _route_kernel,
        grid=(N // tm,),
        in_specs=[pl.BlockSpec((tm, LANES), lambda i: (i, 0))],
        out_specs=[
            pl.BlockSpec((8, tm), lambda i: (0, i)),
            pl.BlockSpec((tm, LANES), lambda i: (i, 0)),
            pl.BlockSpec((N_EXPERTS, LANES), lambda i: (0, 0)),
        ],
        out_shape=[
            jax.ShapeDtypeStruct((8, N), F32),
            jax.ShapeDtypeStruct((N, LANES), F32),
            jax.ShapeDtypeStruct((N_EXPERTS, LANES), F32),
        ],
        scratch_shapes=[pltpu.VMEM((N_EXPERTS, LANES), F32)],
        compiler_params=_params(("arbitrary",)),
        name="moe_route",
    )(logits)


SC_WINDOW = 128


def _sc_mesh():
    return plsc.VectorSubcoreMesh(core_axis_name="core", subcore_axis_name="subcore")


def _piece_index(rows, pieces, n_rows):
    return (jnp.arange(pieces, dtype=jnp.int32)[:, None] * n_rows + rows[None, :]).reshape(1, -1)


def _sc_dispatch(h2, dests, n_rows):
    pieces, N, pc = h2.shape
    w = SC_WINDOW

    @functools.partial(pl.kernel, out_type=jax.ShapeDtypeStruct((pieces * n_rows, pc), h2.dtype),
                       mesh=_sc_mesh(), scratch_types=[], name="moe_dispatch_sc")
    def run(x_hbm, *refs):
        o_hbm = refs[-1]

        def body(x_vmem, *idx_vmem):
            for i_vmem in idx_vmem:
                pltpu.sync_copy(x_vmem, o_hbm.at[i_vmem.at[0]])

        pltpu.emit_pipeline(
            body,
            grid=(pieces * N // w,),
            in_specs=[pl.BlockSpec((w, pc), lambda i: (i, 0))]
            + [pl.BlockSpec((1, w), lambda i: (0, i))] * len(dests),
            out_specs=[],
            core_axis_name=("core", "subcore"),
            dimension_semantics=(pltpu.PARALLEL,),
        )(x_hbm, *refs[:-1])

    xs = run(h2.reshape(pieces * N, pc), *[_piece_index(d, pieces, n_rows) for d in dests])
    return xs.reshape(pieces, n_rows, pc)


def _sc_return(yb, dest):
    M = dest.shape[0]
    pieces, P, pc = yb.shape
    w = SC_WINDOW

    @functools.partial(pl.kernel, out_type=jax.ShapeDtypeStruct((pieces * M, pc), yb.dtype),
                       mesh=_sc_mesh(), scratch_types=[], name="moe_return_sc")
    def run(y_hbm, i_hbm, o_hbm):
        def body(i_vmem, o_vmem):
            pltpu.sync_copy(y_hbm.at[i_vmem.at[0]], o_vmem)

        pltpu.emit_pipeline(
            body,
            grid=(pieces * M // w,),
            in_specs=[pl.BlockSpec((1, w), lambda i: (0, i))],
            out_specs=[pl.BlockSpec((w, pc), lambda i: (i, 0))],
            core_axis_name=("core", "subcore"),
            dimension_semantics=(pltpu.PARALLEL,),
        )(i_hbm, o_hbm)

    return run(yb.reshape(pieces * P, pc), _piece_index(dest, pieces, P)).reshape(pieces, M, pc)


def _expert_kernel(ib_ref, ie_ref, nxt_ref, slot_ref, nb_ref, xs_ref, w1_ref, w3_ref, w2_ref, y_ref,
                   wf1, wf3, wf2, w1b, w3b, w2b, sem, *, layer):
    j = pl.program_id(0)
    jp = jnp.maximum(j - 1, 0)
    live = j < nb_ref[0]
    new_expert = jnp.logical_or(j == 0, ie_ref[j] != ie_ref[jp])
    streams = ((w1_ref, wf1), (w3_ref, wf3), (w2_ref, wf2))

    def weight_copies(expert, s):
        return [pltpu.make_async_copy(w.at[layer, expert], buf.at[s], sem.at[s, k])
                for k, (w, buf) in enumerate(streams)]

    @pl.when(j == 0)
    def _():
        for cp in weight_copies(ie_ref[0], 0):
            cp.start()

    @pl.when(jnp.logical_and(new_expert, live))
    def _():
        s = slot_ref[j]
        for cp in weight_copies(ie_ref[j], s):
            cp.wait()
        w1b[...] = wf1[s].astype(BF16)
        w3b[...] = wf3[s].astype(BF16)
        w2b[...] = wf2[s].astype(BF16)

        @pl.when(nxt_ref[j] >= 0)
        def _():
            for cp in weight_copies(nxt_ref[j], 1 - s):
                cp.start()

    @pl.when(live)
    def _():
        xb = _unpack_rows(xs_ref).astype(BF16)
        h1 = _dot(xb, w1b[...])
        h3 = _dot(xb, w3b[...])
        a = (h1 * jax.nn.sigmoid(h1) * h3).astype(BF16)
        for p, words in enumerate(_pack_rows(_round_bf16(_dot(a, w2b[...]))[1])):
            y_ref[p] = words


def _experts(xs, items, w1, w3, w2, layer):
    pieces, P, pc = xs.shape
    _, E, D, DE = w1.shape
    bm = MOE_BLOCK
    n_items_max = P // bm
    rows = lambda j, *prefetch: (0, prefetch[0][j], 0)
    return pl.pallas_call(
        functools.partial(_expert_kernel, layer=layer),
        grid_spec=pltpu.PrefetchScalarGridSpec(
            num_scalar_prefetch=5,
            grid=(n_items_max,),
            in_specs=[
                pl.BlockSpec((pieces, bm, pc), rows),
                pl.BlockSpec(memory_space=pl.ANY),
                pl.BlockSpec(memory_space=pl.ANY),
                pl.BlockSpec(memory_space=pl.ANY),
            ],
            out_specs=pl.BlockSpec((pieces, bm, pc), rows),
            scratch_shapes=[
                pltpu.VMEM((2, D, DE), F32),
                pltpu.VMEM((2, D, DE), F32),
                pltpu.VMEM((2, DE, D), F32),
                pltpu.VMEM((D, DE), BF16),
                pltpu.VMEM((D, DE), BF16),
                pltpu.VMEM((DE, D), BF16),
                pltpu.SemaphoreType.DMA((2, 3)),
            ],
        ),
        out_shape=jax.ShapeDtypeStruct((pieces, P, pc), U32),
        compiler_params=_params(("arbitrary",)),
        name="moe_experts",
    )(*items, xs, w1, w3, w2)


def _combine_kernel(x1_ref, route_ref, mod_ref, fg_ref, y0_ref, y1_ref, *rest):
    o_ref = rest[-1]
    x2 = x1_ref[...] + mod_ref[0][5:6] * _moe_mix(route_ref, y0_ref, y1_ref)
    ms = jnp.mean(x2 * x2, axis=-1, keepdims=True)
    o_ref[...] = x2 * lax.rsqrt(ms + NORM_EPS) * fg_ref[...]


def _final_combine(x1, route, modl, final_g, yk, seq, first_tile, prev_out):
    N, D = x1.shape
    tm = ROW_TILE
    nt = yk.shape[1] // 2 // tm
    tok = lambda i: (first_tile + i, 0)
    in_specs = [
        pl.BlockSpec((tm, D), tok),
        pl.BlockSpec((tm, LANES), tok),
        pl.BlockSpec((1, 6, D), lambda i: (((first_tile + i) * tm) // seq, 0, 0)),
        pl.BlockSpec((1, D), lambda i: (0, 0)),
        pl.BlockSpec((yk.shape[0], tm, ROW_PIECE), lambda i: (0, i, 0)),
        pl.BlockSpec((yk.shape[0], tm, ROW_PIECE), lambda i: (0, nt + i, 0)),
    ]
    args = [x1, route, modl, final_g, yk, yk]
    aliases = {}
    if prev_out is not None:
        in_specs.append(pl.BlockSpec(memory_space=pl.ANY))
        args.append(prev_out)
        aliases = {len(args) - 1: 0}
    return pl.pallas_call(
        _combine_kernel,
        grid=(nt,),
        in_specs=in_specs,
        out_specs=pl.BlockSpec((tm, D), tok),
        out_shape=jax.ShapeDtypeStruct((N, D), F32),
        input_output_aliases=aliases,
        compiler_params=_params(("arbitrary",)),
        name="moe_combine",
    )(*args)


ITEM_LANES = 256
I_BLOCK, I_EXPERT, I_NEXT, I_SLOT, I_COUNT = 0, 1, 2, 3, 4


def _dest_kernel(routet_ref, cnt_ref, d1_ref, d2_ref, d3_ref, items_ref):
    tm = routet_ref.shape[1]
    ne = N_EXPERTS
    bm = float(MOE_BLOCK)
    cnt = cnt_ref[...]
    r = lax.broadcasted_iota(jnp.int32, (ne, ne), 0)
    c = lax.broadcasted_iota(jnp.int32, (ne, ne), 1)
    lower = (r >= c).astype(F32)
    cumsum = lambda a: jnp.dot(lower, a, precision=HIGHEST, preferred_element_type=F32)
    padded = jnp.floor((cnt + (bm - 1.0)) * (1.0 / bm)) * bm
    pend = cumsum(padded)
    pstart = pend - padded
    npad = padded - cnt
    pad_end = cumsum(npad)
    pad_start = pad_end - npad

    tile = lambda a, n: jnp.concatenate([a] * (n // LANES), axis=1)
    expert = lax.broadcasted_iota(jnp.int32, (ne, tm), 0).astype(F32)
    rec = routet_ref[...]
    pstart_t = tile(pstart, tm)

    def sorted_row(row_id, row_rank):
        sel = expert == rec[row_id:row_id + 1, :]
        return jnp.sum(jnp.where(sel, pstart_t, 0.0), axis=0, keepdims=True) + rec[row_rank:row_rank + 1, :]

    dest1 = sorted_row(R_ID1, R_RANK1)
    dest2 = sorted_row(R_ID2, R_RANK2)
    t = (lax.broadcasted_iota(jnp.int32, (1, tm), 1) + pl.program_id(0) * tm).astype(F32)
    ps_t, pe_t = tile(pad_start, tm), tile(pad_end, tm)
    in_e = jnp.logical_and(ps_t <= t, t < pe_t)
    pad_row = jnp.sum(jnp.where(in_e, tile(pstart + cnt, tm) + (t - ps_t), 0.0), axis=0, keepdims=True)
    dest3 = jnp.where(t < pe_t[ne - 1:ne, :], pad_row, dest1)
    d1_ref[0] = dest1.astype(jnp.int32)
    d2_ref[0] = dest2.astype(jnp.int32)
    d3_ref[0] = dest3.astype(jnp.int32)

    @pl.when(pl.program_id(0) == 0)
    def _():
        nl = ITEM_LANES
        pend_i = tile(pend, nl)
        n_items = pend_i[ne - 1:ne, :] * (1.0 / bm)
        blk = jnp.minimum(lax.broadcasted_iota(jnp.int32, (1, nl), 1).astype(F32), n_items - 1.0)
        erow = lax.broadcasted_iota(jnp.int32, (ne, nl), 0).astype(F32)
        ie = jnp.sum(jnp.where(pend_i <= blk * bm, 1.0, 0.0), axis=0, keepdims=True)
        nonempty = tile(cnt, nl) > 0.0
        order = jnp.sum(jnp.where(jnp.logical_and(nonempty, erow < ie), 1.0, 0.0), axis=0, keepdims=True)
        nxt = jnp.min(jnp.where(jnp.logical_and(nonempty, erow > ie), erow, float(ne)), axis=0, keepdims=True)
        nxt = jnp.where(nxt == float(ne), -1.0, nxt)
        slot = order - 2.0 * jnp.floor(order * 0.5)
        row = lax.broadcasted_iota(jnp.int32, (8, nl), 0)
        tab = jnp.where(row == I_BLOCK, blk, 0.0)
        tab = jnp.where(row == I_EXPERT, ie, tab)
        tab = jnp.where(row == I_NEXT, nxt, tab)
        tab = jnp.where(row == I_SLOT, slot, tab)
        tab = jnp.where(row == I_COUNT, n_items, tab)
        items_ref[...] = tab.astype(jnp.int32)


def _destinations(route_t, cnt, n_blocks):
    N = route_t.shape[1]
    tm = 2048 if N % 2048 == 0 else MIX_SUB
    out = jax.ShapeDtypeStruct((N // tm, 1, tm), jnp.int32)
    d1, d2, d3, items = pl.pallas_call(
        _dest_kernel,
        grid=(N // tm,),
        in_specs=[pl.BlockSpec((8, tm), lambda i: (0, i)),
                  pl.BlockSpec((N_EXPERTS, LANES), lambda i: (0, 0))],
        out_specs=[pl.BlockSpec((1, 1, tm), lambda i: (i, 0, 0))] * 3
        + [pl.BlockSpec((8, ITEM_LANES), lambda i: (0, 0))],
        out_shape=[out, out, out, jax.ShapeDtypeStruct((8, ITEM_LANES), jnp.int32)],
        compiler_params=_params(("arbitrary",)),
        name="moe_dest",
    )(route_t, cnt)
    table = tuple(items[k, :n_blocks] for k in (I_BLOCK, I_EXPERT, I_NEXT, I_SLOT)) + (items[I_COUNT, :1],)
    return d1.reshape(N), d2.reshape(N), d3.reshape(N), table


def kernel(x, c, mod_w, mod_b, norm1_g, w_in, gate_w2, gate_b, gla_norm_g, conv_w, w_gla_out,
           w_conv_out, w_out, norm2_g, router_group_w, router_group_b, router_expert_w,
           router_expert_b, expert_w1, expert_w3, expert_w2, final_norm_g):
    B, S, D = x.shape
    L = mod_w.shape[0]
    N = B * S
    dk = gate_w2.shape[2]
    rank = gate_w2.shape[1]
    dv = gla_norm_g.shape[1]
    n_slots = N * 2
    n_blocks = n_slots // MOE_BLOCK + N_EXPERTS
    assert S % GLA_TILE == 0 and S % MIX_TILE == 0 and S % IN_TILE == 0
    assert N % (FINAL_PARTS * ROW_TILE) == 0
    assert n_slots % MOE_BLOCK == 0 and MOE_BLOCK & (MOE_BLOCK - 1) == 0 and n_blocks <= ITEM_LANES
    assert N_EXPERTS * (MOE_BLOCK - 1) <= N
    assert N_GROUPS + N_EXPERTS <= LANES and EXPERTS_PER_GROUP == 8 and 3 * rank <= LANES and D % (2 * ROW_PIECE) == 0

    mod = _modulation(c, mod_w, mod_b)
    xf = x.reshape(N, D)
    for l in range(L):
        modl = mod[l].reshape(B, 6, D)
        wa, wb, wg = _prep_in_weights(w_in, l, 2 * dk + dv, rank, dv)
        g_hi = gate_w2[l].astype(BF16)
        g_lo = (gate_w2[l] - g_hi.astype(F32)).astype(BF16)
        gw2 = jnp.pad(jnp.concatenate([g_hi, g_hi, g_lo], axis=0), ((0, LANES - 3 * rank), (0, 0)))
        if l == 0:
            za, zb, zg = _in_projection(xf, modl, norm1_g[l].reshape(1, D), conv_w[l], wa, wb, wg, S)
        else:
            xf, za, zb, zg = _in_projection(x1, modl, norm1_g[l].reshape(1, D), conv_w[l], wa, wb, wg, S,
                                            combine=(route, yk, mod[l - 1].reshape(B, 6, D)))
        og = _gla(za, zg, gw2, gate_b[l].reshape(1, dk), gla_norm_g[l].reshape(1, dv), B, S, dk, dv, rank)
        wr = jnp.pad(jnp.concatenate([router_group_w[l], router_expert_w[l]], axis=1),
                     ((0, 0), (0, LANES - N_GROUPS - N_EXPERTS)))
        wr_hi = wr.astype(BF16)
        wr = jnp.concatenate([wr_hi, (wr - wr_hi.astype(F32)).astype(BF16)], axis=1)
        br = jnp.pad(jnp.concatenate([router_group_b[l], router_expert_b[l]]),
                     (0, LANES - N_GROUPS - N_EXPERTS)).reshape(1, LANES)
        x1, h2, logits = _mixer_out(
            og, zb, xf, modl, w_gla_out[l].astype(BF16), w_conv_out[l].astype(BF16),
            w_out[l].astype(BF16), norm2_g[l].reshape(1, D), wr, br, B, S)
        route_t, route, cnt = _routing(logits)
        dest1, dest2, dest3, items = _destinations(route_t, cnt, n_blocks)
        xs = _sc_dispatch(h2, (dest1, dest2, dest3), n_blocks * MOE_BLOCK)
        yb = _experts(xs, items, expert_w1, expert_w3, expert_w2, l)
        if l < L - 1:
            yk = _sc_return(yb, jnp.concatenate([dest1, dest2]))
    out = None
    part = N // FINAL_PARTS
    for p in range(FINAL_PARTS):
        tok = slice(p * part, (p + 1) * part)
        yk = _sc_return(yb, jnp.concatenate([dest1[tok], dest2[tok]]))
        out = _final_combine(x1, route, modl, final_norm_g.reshape(1, D), yk, S,
                             p * part // ROW_TILE, out)
    return out.reshape(B, S, D)
```

```python
import functools

import jax
import jax.numpy as jnp
from jax import lax
from jax.experimental import pallas as pl
from jax.experimental.pallas import tpu as pltpu
from jax.experimental.pallas import tpu_sc as plsc

F32 = jnp.float32
BF16 = jnp.bfloat16
HIGHEST = lax.Precision.HIGHEST

GLA_HEADS = 4
GATE_TAU = 16.0
GLA_CHUNK = 64
N_GROUPS = 8
EXPERTS_PER_GROUP = 8
N_EXPERTS = N_GROUPS * EXPERTS_PER_GROUP
NORM_EPS = 1e-6

LANES = 128
VMEM_LIMIT_BYTES = 56 * 1024 * 1024

IN_TILE = 512
GLA_TILE = 512
MIX_TILE = 512
MIX_SUB = 256
MOE_BLOCK = 512
ROW_TILE = 1024
FINAL_PARTS = 4
ROW_PIECE = 256
U32 = jnp.uint32

R_ID1, R_ID2, R_W1, R_W2, R_RANK1, R_RANK2 = 0, 1, 2, 3, 4, 5


def _dot(a, b):
    return jnp.dot(a, b, preferred_element_type=F32)


def _round_bf16(x):
    xb = x.astype(BF16)
    return xb, xb.astype(F32)


def _pack_rows(xr):
    half = xr.shape[1] // 2
    out = []
    for p in range(half // ROW_PIECE):
        lo = xr[:, p * ROW_PIECE:(p + 1) * ROW_PIECE]
        hi = xr[:, half + p * ROW_PIECE:half + (p + 1) * ROW_PIECE]
        out.append((pltpu.bitcast(lo, U32) >> 16) | pltpu.bitcast(hi, U32))
    return out


def _unpack_rows(ref):
    words = [ref[p] for p in range(ref.shape[0])]
    lo = [pltpu.bitcast(w << 16, F32) for w in words]
    hi = [pltpu.bitcast(w & jnp.uint32(0xFFFF0000), F32) for w in words]
    return jnp.concatenate(lo + hi, axis=-1)


def _const_spec(shape):
    nd = len(shape)
    return pl.BlockSpec(shape, lambda *_: (0,) * nd, pipeline_mode=pl.Buffered(1))


def _params(sem):
    return pltpu.CompilerParams(dimension_semantics=sem, vmem_limit_bytes=VMEM_LIMIT_BYTES)


MOD_ROWS = 256


def _mod_kernel(c_ref, w_ref, b_ref, o_ref):
    c = c_ref[...]
    sc = c * jax.nn.sigmoid(c)
    part = jnp.dot(sc, w_ref[0], precision=HIGHEST, preferred_element_type=F32)

    @pl.when(pl.program_id(1) == 0)
    def _():
        o_ref[0] = part + b_ref[0]

    @pl.when(pl.program_id(1) > 0)
    def _():
        o_ref[0] += part


def _modulation(c, mod_w, mod_b, layer):
    L, D, D6 = mod_w.shape
    B = c.shape[0]
    kb = MOD_ROWS
    return pl.pallas_call(
        _mod_kernel,
        grid=(1, D // kb),
        in_specs=[
            pl.BlockSpec((B, kb), lambda l, k: (0, k)),
            pl.BlockSpec((1, kb, D6), lambda l, k: (layer, k, 0)),
            pl.BlockSpec((1, 1, D6), lambda l, k: (layer, 0, 0)),
        ],
        out_specs=pl.BlockSpec((1, B, D6), lambda l, k: (0, 0, 0)),
        out_shape=jax.ShapeDtypeStruct((1, B, D6), F32),
        compiler_params=_params(("arbitrary", "arbitrary")),
        name="adaln_mod",
    )(c, mod_w, mod_b.reshape(L, 1, D6))


CONV_COLS = 256


def _moe_mix(route_ref, y0_ref, y1_ref):
    rec = route_ref[...]
    w1 = rec[:, R_W1:R_W1 + 1]
    w2 = rec[:, R_W2:R_W2 + 1]
    return w1 * _unpack_rows(y0_ref) + w2 * _unpack_rows(y1_ref)


def _inproj_kernel(*refs, seq, fuse_combine):
    if fuse_combine:
        (x_ref, route_ref, y0_ref, y1_ref, modp_ref, mod_ref, g_ref, cw_ref, wa_ref, wb_ref, wg_ref,
         xo_ref, za_ref, zb_ref, zg_ref, carry_ref) = refs
        x = x_ref[...] + modp_ref[0][5:6] * _moe_mix(route_ref, y0_ref, y1_ref)
        xo_ref[...] = x
    else:
        (x_ref, mod_ref, g_ref, cw_ref, wa_ref, wb_ref, wg_ref, za_ref, zb_ref, zg_ref, carry_ref) = refs
        x = x_ref[...]
    tm, d = x.shape

    @pl.when((pl.program_id(0) * tm) % seq == 0)
    def _():
        carry_ref[...] = jnp.zeros_like(carry_ref)

    ms = jnp.mean(x * x, axis=-1, keepdims=True)
    m = mod_ref[0]
    h = (x * lax.rsqrt(ms + NORM_EPS) * (g_ref[...] * (1.0 + m[1:2])) + m[0:1]).astype(BF16)
    for j in range(za_ref.shape[1] // d):
        za_ref[:, j * d:(j + 1) * d] = _dot(h, wa_ref[:, j * d:(j + 1) * d]).astype(BF16)
    zg_ref[...] = _dot(h, wg_ref[...])
    for j in range(1, zb_ref.shape[1] // d):
        zb_ref[:, j * d:(j + 1) * d] = _dot(h, wb_ref[:, (j + 2) * d:(j + 3) * d]).astype(BF16)

    w = CONV_COLS
    rowi = lax.broadcasted_iota(jnp.int32, (tm, w), 0)
    for j in range(d // w):
        cols = slice(j * w, (j + 1) * w)
        cb = _dot(h, wb_ref[:, j * w:(j + 1) * w])
        cc = _dot(h, wb_ref[:, d + j * w:d + (j + 1) * w])
        ch = _dot(h, wb_ref[:, 2 * d + j * w:2 * d + (j + 1) * w])
        u = cc * ch
        prev = carry_ref[:, cols]
        u1 = jnp.where(rowi == 0, prev[7:8], pltpu.roll(u, 1, 0))
        u2 = jnp.where(rowi == 0, prev[6:7], jnp.where(rowi == 1, prev[7:8], pltpu.roll(u, 2, 0)))
        carry_ref[:, cols] = u[tm - 8:tm]
        conv = cw_ref[0:1, cols] * u2 + cw_ref[1:2, cols] * u1 + cw_ref[2:3, cols] * u
        zb_ref[:, cols] = (cb * conv).astype(BF16)


def _in_projection(x, modl, norm_g, conv_w, wa, wb, wg, seq, combine=None):
    N, D = x.shape
    tm = IN_TILE
    nt = N // tm
    ca, cb = wa.shape[1], wb.shape[1] - 2 * D
    tok = lambda i: (i, 0)
    per_batch = lambda i: ((i * tm) // seq, 0, 0)
    in_specs = [pl.BlockSpec((tm, D), tok)]
    args = [x]
    out_specs, out_shape = [], []
    if combine is not None:
        route, yk, mod_prev = combine
        piece_blk = (yk.shape[0], tm, ROW_PIECE)
        in_specs += [
            pl.BlockSpec((tm, LANES), tok),
            pl.BlockSpec(piece_blk, lambda i: (0, i, 0)),
            pl.BlockSpec(piece_blk, lambda i: (0, nt + i, 0)),
            pl.BlockSpec((1, 6, D), per_batch),
        ]
        args += [route, yk, yk, mod_prev]
        out_specs.append(pl.BlockSpec((tm, D), tok))
        out_shape.append(jax.ShapeDtypeStruct((N, D), F32))
    in_specs += [
        pl.BlockSpec((1, 6, D), per_batch),
        _const_spec((1, D)),
        _const_spec(conv_w.shape),
        _const_spec((D, ca)),
        _const_spec(wb.shape),
        _const_spec((D, LANES)),
    ]
    args += [modl, norm_g, conv_w, wa, wb, wg]
    out_specs += [
        pl.BlockSpec((tm, ca), tok),
        pl.BlockSpec((tm, cb), tok),
        pl.BlockSpec((tm, LANES), tok),
    ]
    out_shape += [
        jax.ShapeDtypeStruct((N, ca), BF16),
        jax.ShapeDtypeStruct((N, cb), BF16),
        jax.ShapeDtypeStruct((N, LANES), F32),
    ]
    kern = functools.partial(_inproj_kernel, seq=seq, fuse_combine=combine is not None)
    return pl.pallas_call(
        kern,
        grid=(nt,),
        in_specs=in_specs,
        out_specs=out_specs,
        out_shape=out_shape,
        scratch_shapes=[pltpu.VMEM((8, D), F32)],
        compiler_params=_params(("arbitrary",)),
        name="in_projection",
    )(*args)


def _log_sigmoid(x):
    return jnp.minimum(x, 0.0) - jnp.log(1.0 + jnp.exp(-jnp.abs(x)))


def _gla_kernel(za_ref, zg_ref, gw2_ref, gb_ref, ng_ref, o_ref, st_ref, lg_ref, *, dk, dv, rank):
    heads = GLA_HEADS
    dkh, dvh = dk // heads, dv // heads
    c = GLA_CHUNK
    ts = za_ref.shape[0]

    @pl.when(pl.program_id(1) == 0)
    def _():
        st_ref[...] = jnp.zeros_like(st_ref)

    zg = zg_ref[...]
    zg_hi = zg.astype(BF16)
    zg_lo = (zg - zg_hi.astype(F32)).astype(BF16)
    lane = lax.broadcasted_iota(jnp.int32, zg.shape, 1)
    lhs = jnp.where(jnp.logical_and(lane >= rank, lane < 2 * rank), zg_lo, zg_hi)
    pre = _dot(lhs, gw2_ref[...]) + gb_ref[...]
    lg = _log_sigmoid(pre) * (1.0 / GATE_TAU)
    lg_hi = lg.astype(BF16)
    lg_ref[:, 0:dk] = lg_hi
    lg_ref[:, dk:2 * dk] = (lg - lg_hi.astype(F32)).astype(BF16)

    row = lax.broadcasted_iota(jnp.int32, (c, c), 0)
    col = lax.broadcasted_iota(jnp.int32, (c, c), 1)
    causal = row >= col
    tril = causal.astype(BF16)
    qscale = dkh ** -0.5

    for ci in range(ts // c):
        rows = slice(ci * c, (ci + 1) * c)
        b_two = _dot(tril, lg_ref[rows, :])
        b_all = b_two[:, 0:dk] + b_two[:, dk:2 * dk]
        for hd in range(heads):
            ks = slice(hd * dkh, (hd + 1) * dkh)
            q = za_ref[rows, hd * dkh:(hd + 1) * dkh].astype(F32) * qscale
            k = za_ref[rows, dk + hd * dkh:dk + (hd + 1) * dkh].astype(F32)
            v = za_ref[rows, 2 * dk + hd * dvh:2 * dk + (hd + 1) * dvh]
            r = za_ref[rows, 2 * dk + dv + hd * dvh:2 * dk + dv + (hd + 1) * dvh].astype(F32)
            b = b_all[:, ks]
            b_last = b[c - 1:c, :]
            q_t = (q * jnp.exp(b)).astype(BF16)
            k_t = (k * jnp.exp(-b)).astype(BF16)
            k_s = (k * jnp.exp(b_last - b)).astype(BF16)
            decay = jnp.exp(b_last)
            attn = lax.dot_general(q_t, k_t, (((1,), (1,)), ((), ())), preferred_element_type=F32)
            attn = jnp.where(causal, attn, 0.0).astype(BF16)
            st = st_ref[hd]
            o = _dot(attn, v) + lax.dot_general(
                q_t, st.astype(BF16), (((1,), (1,)), ((), ())), preferred_element_type=F32)
            upd = lax.dot_general(v, k_s, (((0,), (0,)), ((), ())), preferred_element_type=F32)
            st_ref[hd] = st * decay + upd
            ms = jnp.mean(o * o, axis=-1, keepdims=True)
            on = o * lax.rsqrt(ms + NORM_EPS) * ng_ref[:, hd * dvh:(hd + 1) * dvh]
            o_ref[rows, hd * dvh:(hd + 1) * dvh] = (on * (r * jax.nn.sigmoid(r))).astype(BF16)


def _gla(za, zg, gw2, gb, ng, batch, seq, dk, dv, rank):
    N = za.shape[0]
    ts = GLA_TILE
    ns = seq // ts
    heads = GLA_HEADS
    kern = functools.partial(_gla_kernel, dk=dk, dv=dv, rank=rank)
    return pl.pallas_call(
        kern,
        grid=(batch, ns),
        in_specs=[
            pl.BlockSpec((ts, za.shape[1]), lambda b, s: (b * ns + s, 0)),
            pl.BlockSpec((ts, LANES), lambda b, s: (b * ns + s, 0)),
            _const_spec((LANES, dk)),
            _const_spec((1, dk)),
            _const_spec((1, dv)),
        ],
        out_specs=pl.BlockSpec((ts, dv), lambda b, s: (b * ns + s, 0)),
        out_shape=jax.ShapeDtypeStruct((N, dv), BF16),
        scratch_shapes=[
            pltpu.VMEM((heads, dv // heads, dk // heads), F32),
            pltpu.VMEM((ts, 2 * dk), BF16),
        ],
        compiler_params=_params(("arbitrary", "arbitrary")),
        name="gla",
    )(za, zg, gw2, gb, ng)


def _mixout_kernel(og_ref, zb_ref, x_ref, mod_ref, wga_ref, wco_ref, wo_ref, n2_ref,
                   wr_ref, br_ref, x1_ref, h2_ref, logit_ref):
    tm, d = x_ref.shape
    m = mod_ref[0]
    sub = MIX_SUB
    gain2 = n2_ref[...] * (1.0 + m[4:5])
    for r0 in range(0, tm, sub):
        rs = slice(r0, r0 + sub)
        ga = zb_ref[rs, d:2 * d].astype(F32)
        gc = zb_ref[rs, 2 * d:3 * d].astype(F32)
        y_conv = _dot(zb_ref[rs, 0:d], wco_ref[...])
        y_gla = _dot(og_ref[rs, :], wga_ref[...])
        y = jax.nn.sigmoid(ga) * y_gla + jax.nn.sigmoid(gc) * y_conv
        y = _dot(y.astype(BF16), wo_ref[...])
        x1 = x_ref[rs, :] + m[2:3] * y
        x1_ref[rs, :] = x1

        ms = jnp.mean(x1 * x1, axis=-1, keepdims=True)
        h2 = x1 * lax.rsqrt(ms + NORM_EPS) * gain2 + m[3:4]
        h_hi, h_r = _round_bf16(h2)
        for p, words in enumerate(_pack_rows(h_r)):
            h2_ref[p, rs, :] = words

        h_lo = (h2 - h_r).astype(BF16)
        two = _dot(h_hi, wr_ref[...])
        logits = two[:, 0:LANES] + two[:, LANES:2 * LANES] + _dot(h_lo, wr_ref[:, 0:LANES]) + br_ref[...]
        logit_ref[rs, :] = logits


def _mixer_out(og, zb, x, modl, wga, wco, wo, n2g, wr, br, batch, seq):
    N, D = x.shape
    tm = MIX_TILE
    ns = seq // tm
    tok = lambda b, s: (b * ns + s, 0)
    return pl.pallas_call(
        _mixout_kernel,
        grid=(batch, ns),
        in_specs=[
            pl.BlockSpec((tm, D), tok),
            pl.BlockSpec((tm, zb.shape[1]), tok),
            pl.BlockSpec((tm, D), tok),
            pl.BlockSpec((1, 6, D), lambda b, s: (b, 0, 0)),
            _const_spec((D, D)),
            _const_spec((D, D)),
            _const_spec((D, D)),
            _const_spec((1, D)),
            _const_spec((D, 2 * LANES)),
            _const_spec((1, LANES)),
        ],
        out_specs=[
            pl.BlockSpec((tm, D), tok),
            pl.BlockSpec((D // 2 // ROW_PIECE, tm, ROW_PIECE), lambda b, s: (0, b * ns + s, 0)),
            pl.BlockSpec((tm, LANES), tok),
        ],
        out_shape=[
            jax.ShapeDtypeStruct((N, D), F32),
            jax.ShapeDtypeStruct((D // 2 // ROW_PIECE, N, ROW_PIECE), U32),
            jax.ShapeDtypeStruct((N, LANES), F32),
        ],
        compiler_params=_params(("arbitrary", "arbitrary")),
        name="mixer_out",
    )(og, zb, x, modl, wga, wco, wo, n2g, wr, br)


def _route_kernel(logit_ref, routet_ref, route_ref, cnt_ref, run_ref):
    tm = logit_ref.shape[0]
    sub = MIX_SUB
    rows8 = EXPERTS_PER_GROUP

    @pl.when(pl.program_id(0) == 0)
    def _():
        run_ref[...] = jnp.zeros_like(run_ref)

    sub8 = lax.broadcasted_iota(jnp.int32, (rows8, sub), 0)
    erow = lax.broadcasted_iota(jnp.int32, (N_EXPERTS, sub), 0)
    tr = lax.broadcasted_iota(jnp.int32, (sub, sub), 0)
    tc = lax.broadcasted_iota(jnp.int32, (sub, sub), 1)
    earlier = (tr < tc).astype(BF16)
    ones = jnp.ones((sub, LANES), BF16)
    neg = -jnp.inf
    run = run_ref[...]
    for r0 in range(0, tm, sub):
        lt = logit_ref[r0:r0 + sub, :].T
        gl = lt[0:N_GROUPS, :]
        gmax = jnp.max(gl, axis=0, keepdims=True)
        gsum = jnp.sum(jnp.exp(gl - gmax), axis=0, keepdims=True)
        g_w = 1.0 / gsum
        g_idx = jnp.min(jnp.where(gl == gmax, sub8, N_GROUPS), axis=0, keepdims=True)
        el = lt[N_GROUPS:N_GROUPS + rows8, :]
        for g in range(1, N_GROUPS):
            el = jnp.where(g_idx == g, lt[N_GROUPS + g * rows8:N_GROUPS + (g + 1) * rows8, :], el)
        e1 = jnp.max(el, axis=0, keepdims=True)
        i1 = jnp.min(jnp.where(el == e1, sub8, rows8), axis=0, keepdims=True)
        el2 = jnp.where(sub8 == i1, neg, el)
        e2 = jnp.max(el2, axis=0, keepdims=True)
        i2 = jnp.min(jnp.where(el2 == e2, sub8, rows8), axis=0, keepdims=True)
        ratio = jnp.exp(e2 - e1)
        w1 = g_w / (1.0 + ratio)
        w2 = g_w * ratio / (1.0 + ratio)
        id1 = g_idx * rows8 + i1
        id2 = g_idx * rows8 + i2

        oh1 = erow == id1
        oh2 = erow == id2
        oh1b = jnp.where(oh1, 1.0, 0.0).astype(BF16)
        oh2b = jnp.where(oh2, 1.0, 0.0).astype(BF16)
        tot1 = _dot(oh1b, ones)
        tot2 = _dot(oh2b, ones)
        base1 = jnp.concatenate([run] * (sub // LANES), axis=1)
        base2 = jnp.concatenate([run + tot1] * (sub // LANES), axis=1)
        c1 = _dot(oh1b, earlier) + base1
        c2 = _dot(oh2b, earlier) + base2
        rank1 = jnp.sum(jnp.where(oh1, c1, 0.0), axis=0, keepdims=True)
        rank2 = jnp.sum(jnp.where(oh2, c2, 0.0), axis=0, keepdims=True)
        run = run + tot1 + tot2

        rec = jnp.where(sub8 == R_ID1, id1.astype(F32), 0.0)
        rec = jnp.where(sub8 == R_ID2, id2.astype(F32), rec)
        rec = jnp.where(sub8 == R_W1, w1, rec)
        rec = jnp.where(sub8 == R_W2, w2, rec)
        rec = jnp.where(sub8 == R_RANK1, rank1, rec)
        rec = jnp.where(sub8 == R_RANK2, rank2, rec)
        routet_ref[:, r0:r0 + sub] = rec
        rec_full = jnp.concatenate([rec, jnp.zeros((LANES - rows8, sub), F32)], axis=0)
        route_ref[r0:r0 + sub, :] = rec_full.T
    run_ref[...] = run
    cnt_ref[...] = run


def _routing(logits):
    N = logits.shape[0]
    tm = 2048 if N % 2048 == 0 else MIX_SUB
    return pl.pallas_call(
        _route_kernel,
        grid=(N // tm,),
        in_specs=[pl.BlockSpec((tm, LANES), lambda i: (i, 0))],
        out_specs=[
            pl.BlockSpec((8, tm), lambda i: (0, i)),
            pl.BlockSpec((tm, LANES), lambda i: (i, 0)),
            pl.BlockSpec((N_EXPERTS, LANES), lambda i: (0, 0)),
        ],
        out_shape=[
            jax.ShapeDtypeStruct((8, N), F32),
            jax.ShapeDtypeStruct((N, LANES), F32),
            jax.ShapeDtypeStruct((N_EXPERTS, LANES), F32),
        ],
        scratch_shapes=[pltpu.VMEM((N_EXPERTS, LANES), F32)],
        compiler_params=_params(("arbitrary",)),
        name="moe_route",
    )(logits)


SC_WINDOW = 128


def _sc_mesh():
    return plsc.VectorSubcoreMesh(core_axis_name="core", subcore_axis_name="subcore")


def _piece_index(rows, pieces, n_rows):
    return (jnp.arange(pieces, dtype=jnp.int32)[:, None] * n_rows + rows[None, :]).reshape(1, -1)


def _sc_dispatch(h2, dests, n_rows):
    pieces, N, pc = h2.shape
    w = SC_WINDOW

    @functools.partial(pl.kernel, out_type=jax.ShapeDtypeStruct((pieces * n_rows, pc), h2.dtype),
                       mesh=_sc_mesh(), scratch_types=[], name="moe_dispatch_sc")
    def run(x_hbm, *refs):
        o_hbm = refs[-1]

        def body(x_vmem, *idx_vmem):
            for i_vmem in idx_vmem:
                pltpu.sync_copy(x_vmem, o_hbm.at[i_vmem.at[0]])

        pltpu.emit_pipeline(
            body,
            grid=(pieces * N // w,),
            in_specs=[pl.BlockSpec((w, pc), lambda i: (i, 0))]
            + [pl.BlockSpec((1, w), lambda i: (0, i))] * len(dests),
            out_specs=[],
            core_axis_name=("core", "subcore"),
            dimension_semantics=(pltpu.PARALLEL,),
        )(x_hbm, *refs[:-1])

    xs = run(h2.reshape(pieces * N, pc), *[_piece_index(d, pieces, n_rows) for d in dests])
    return xs.reshape(pieces, n_rows, pc)


def _sc_return(yb, dest):
    M = dest.shape[0]
    pieces, P, pc = yb.shape
    w = SC_WINDOW

    @functools.partial(pl.kernel, out_type=jax.ShapeDtypeStruct((pieces * M, pc), yb.dtype),
                       mesh=_sc_mesh(), scratch_types=[], name="moe_return_sc")
    def run(y_hbm, i_hbm, o_hbm):
        def body(i_vmem, o_vmem):
            pltpu.sync_copy(y_hbm.at[i_vmem.at[0]], o_vmem)

        pltpu.emit_pipeline(
            body,
            grid=(pieces * M // w,),
            in_specs=[pl.BlockSpec((1, w), lambda i: (0, i))],
            out_specs=[pl.BlockSpec((w, pc), lambda i: (i, 0))],
            core_axis_name=("core", "subcore"),
            dimension_semantics=(pltpu.PARALLEL,),
        )(i_hbm, o_hbm)

    return run(yb.reshape(pieces * P, pc), _piece_index(dest, pieces, P)).reshape(pieces, M, pc)


def _expert_kernel(ib_ref, ie_ref, nxt_ref, slot_ref, nb_ref, xs_ref, w1_ref, w3_ref, w2_ref, y_ref,
                   wf1, wf3, wf2, w1b, w3b, w2b, sem, *, layer):
    j = pl.program_id(0)
    jp = jnp.maximum(j - 1, 0)
    live = j < nb_ref[0]
    new_expert = jnp.logical_or(j == 0, ie_ref[j] != ie_ref[jp])
    streams = ((w1_ref, wf1), (w3_ref, wf3), (w2_ref, wf2))

    def weight_copies(expert, s):
        return [pltpu.make_async_copy(w.at[layer, expert], buf.at[s], sem.at[s, k])
                for k, (w, buf) in enumerate(streams)]

    @pl.when(j == 0)
    def _():
        for cp in weight_copies(ie_ref[0], 0):
            cp.start()

    @pl.when(jnp.logical_and(new_expert, live))
    def _():
        s = slot_ref[j]
        for cp in weight_copies(ie_ref[j], s):
            cp.wait()
        w1b[...] = wf1[s].astype(BF16)
        w3b[...] = wf3[s].astype(BF16)
        w2b[...] = wf2[s].astype(BF16)

        @pl.when(nxt_ref[j] >= 0)
        def _():
            for cp in weight_copies(nxt_ref[j], 1 - s):
                cp.start()

    @pl.when(live)
    def _():
        xb = _unpack_rows(xs_ref).astype(BF16)
        h1 = _dot(xb, w1b[...])
        h3 = _dot(xb, w3b[...])
        a = (h1 * jax.nn.sigmoid(h1) * h3).astype(BF16)
        for p, words in enumerate(_pack_rows(_round_bf16(_dot(a, w2b[...]))[1])):
            y_ref[p] = words


def _experts(xs, items, w1, w3, w2, layer):
    pieces, P, pc = xs.shape
    _, E, D, DE = w1.shape
    bm = MOE_BLOCK
    n_items_max = P // bm
    rows = lambda j, *prefetch: (0, prefetch[0][j], 0)
    return pl.pallas_call(
        functools.partial(_expert_kernel, layer=layer),
        grid_spec=pltpu.PrefetchScalarGridSpec(
            num_scalar_prefetch=5,
            grid=(n_items_max,),
            in_specs=[
                pl.BlockSpec((pieces, bm, pc), rows),
                pl.BlockSpec(memory_space=pl.ANY),
                pl.BlockSpec(memory_space=pl.ANY),
                pl.BlockSpec(memory_space=pl.ANY),
            ],
            out_specs=pl.BlockSpec((pieces, bm, pc), rows),
            scratch_shapes=[
                pltpu.VMEM((2, D, DE), F32),
                pltpu.VMEM((2, D, DE), F32),
                pltpu.VMEM((2, DE, D), F32),
                pltpu.VMEM((D, DE), BF16),
                pltpu.VMEM((D, DE), BF16),
                pltpu.VMEM((DE, D), BF16),
                pltpu.SemaphoreType.DMA((2, 3)),
            ],
        ),
        out_shape=jax.ShapeDtypeStruct((pieces, P, pc), U32),
        compiler_params=_params(("arbitrary",)),
        name="moe_experts",
    )(*items, xs, w1, w3, w2)


def _combine_kernel(x1_ref, route_ref, mod_ref, fg_ref, y0_ref, y1_ref, *rest):
    o_ref = rest[-1]
    x2 = x1_ref[...] + mod_ref[0][5:6] * _moe_mix(route_ref, y0_ref, y1_ref)
    ms = jnp.mean(x2 * x2, axis=-1, keepdims=True)
    o_ref[...] = x2 * lax.rsqrt(ms + NORM_EPS) * fg_ref[...]


def _final_combine(x1, route, modl, final_g, yk, seq, first_tile, prev_out):
    N, D = x1.shape
    tm = ROW_TILE
    nt = yk.shape[1] // 2 // tm
    tok = lambda i: (first_tile + i, 0)
    in_specs = [
        pl.BlockSpec((tm, D), tok),
        pl.BlockSpec((tm, LANES), tok),
        pl.BlockSpec((1, 6, D), lambda i: (((first_tile + i) * tm) // seq, 0, 0)),
        pl.BlockSpec((1, D), lambda i: (0, 0)),
        pl.BlockSpec((yk.shape[0], tm, ROW_PIECE), lambda i: (0, i, 0)),
        pl.BlockSpec((yk.shape[0], tm, ROW_PIECE), lambda i: (0, nt + i, 0)),
    ]
    args = [x1, route, modl, final_g, yk, yk]
    aliases = {}
    if prev_out is not None:
        in_specs.append(pl.BlockSpec(memory_space=pl.ANY))
        args.append(prev_out)
        aliases = {len(args) - 1: 0}
    return pl.pallas_call(
        _combine_kernel,
        grid=(nt,),
        in_specs=in_specs,
        out_specs=pl.BlockSpec((tm, D), tok),
        out_shape=jax.ShapeDtypeStruct((N, D), F32),
        input_output_aliases=aliases,
        compiler_params=_params(("arbitrary",)),
        name="moe_combine",
    )(*args)


ITEM_LANES = 256
I_BLOCK, I_EXPERT, I_NEXT, I_SLOT, I_COUNT = 0, 1, 2, 3, 4


def _dest_kernel(routet_ref, cnt_ref, d1_ref, d2_ref, d3_ref, items_ref):
    tm = routet_ref.shape[1]
    ne = N_EXPERTS
    bm = float(MOE_BLOCK)
    cnt = cnt_ref[...]
    r = lax.broadcasted_iota(jnp.int32, (ne, ne), 0)
    c = lax.broadcasted_iota(jnp.int32, (ne, ne), 1)
    lower = (r >= c).astype(F32)
    cumsum = lambda a: jnp.dot(lower, a, precision=HIGHEST, preferred_element_type=F32)
    padded = jnp.floor((cnt + (bm - 1.0)) * (1.0 / bm)) * bm
    pend = cumsum(padded)
    pstart = pend - padded
    npad = padded - cnt
    pad_end = cumsum(npad)
    pad_start = pad_end - npad

    tile = lambda a, n: jnp.concatenate([a] * (n // LANES), axis=1)
    expert = lax.broadcasted_iota(jnp.int32, (ne, tm), 0).astype(F32)
    rec = routet_ref[...]
    pstart_t = tile(pstart, tm)

    def sorted_row(row_id, row_rank):
        sel = expert == rec[row_id:row_id + 1, :]
        return jnp.sum(jnp.where(sel, pstart_t, 0.0), axis=0, keepdims=True) + rec[row_rank:row_rank + 1, :]

    dest1 = sorted_row(R_ID1, R_RANK1)
    dest2 = sorted_row(R_ID2, R_RANK2)
    t = (lax.broadcasted_iota(jnp.int32, (1, tm), 1) + pl.program_id(0) * tm).astype(F32)
    ps_t, pe_t = tile(pad_start, tm), tile(pad_end, tm)
    in_e = jnp.logical_and(ps_t <= t, t < pe_t)
    pad_row = jnp.sum(jnp.where(in_e, tile(pstart + cnt, tm) + (t - ps_t), 0.0), axis=0, keepdims=True)
    dest3 = jnp.where(t < pe_t[ne - 1:ne, :], pad_row, dest1)
    d1_ref[0] = dest1.astype(jnp.int32)
    d2_ref[0] = dest2.astype(jnp.int32)
    d3_ref[0] = dest3.astype(jnp.int32)

    @pl.when(pl.program_id(0) == 0)
    def _():
        nl = ITEM_LANES
        pend_i = tile(pend, nl)
        n_items = pend_i[ne - 1:ne, :] * (1.0 / bm)
        blk = jnp.minimum(lax.broadcasted_iota(jnp.int32, (1, nl), 1).astype(F32), n_items - 1.0)
        erow = lax.broadcasted_iota(jnp.int32, (ne, nl), 0).astype(F32)
        ie = jnp.sum(jnp.where(pend_i <= blk * bm, 1.0, 0.0), axis=0, keepdims=True)
        nonempty = tile(cnt, nl) > 0.0
        order = jnp.sum(jnp.where(jnp.logical_and(nonempty, erow < ie), 1.0, 0.0), axis=0, keepdims=True)
        nxt = jnp.min(jnp.where(jnp.logical_and(nonempty, erow > ie), erow, float(ne)), axis=0, keepdims=True)
        nxt = jnp.where(nxt == float(ne), -1.0, nxt)
        slot = order - 2.0 * jnp.floor(order * 0.5)
        row = lax.broadcasted_iota(jnp.int32, (8, nl), 0)
        tab = jnp.where(row == I_BLOCK, blk, 0.0)
        tab = jnp.where(row == I_EXPERT, ie, tab)
        tab = jnp.where(row == I_NEXT, nxt, tab)
        tab = jnp.where(row == I_SLOT, slot, tab)
        tab = jnp.where(row == I_COUNT, n_items, tab)
        items_ref[...] = tab.astype(jnp.int32)


def _destinations(route_t, cnt, n_blocks):
    N = route_t.shape[1]
    tm = 2048 if N % 2048 == 0 else MIX_SUB
    out = jax.ShapeDtypeStruct((N // tm, 1, tm), jnp.int32)
    d1, d2, d3, items = pl.pallas_call(
        _dest_kernel,
        grid=(N // tm,),
        in_specs=[pl.BlockSpec((8, tm), lambda i: (0, i)),
                  pl.BlockSpec((N_EXPERTS, LANES), lambda i: (0, 0))],
        out_specs=[pl.BlockSpec((1, 1, tm), lambda i: (i, 0, 0))] * 3
        + [pl.BlockSpec((8, ITEM_LANES), lambda i: (0, 0))],
        out_shape=[out, out, out, jax.ShapeDtypeStruct((8, ITEM_LANES), jnp.int32)],
        compiler_params=_params(("arbitrary",)),
        name="moe_dest",
    )(route_t, cnt)
    table = tuple(items[k, :n_blocks] for k in (I_BLOCK, I_EXPERT, I_NEXT, I_SLOT)) + (items[I_COUNT, :1],)
    return d1.reshape(N), d2.reshape(N), d3.reshape(N), table


def kernel(x, c, mod_w, mod_b, norm1_g, w_in, gate_w2, gate_b, gla_norm_g, conv_w, w_gla_out,
           w_conv_out, w_out, norm2_g, router_group_w, router_group_b, router_expert_w,
           router_expert_b, expert_w1, expert_w3, expert_w2, final_norm_g):
    B, S, D = x.shape
    L = mod_w.shape[0]
    N = B * S
    dk = gate_w2.shape[2]
    rank = gate_w2.shape[1]
    dv = gla_norm_g.shape[1]
    n_slots = N * 2
    n_blocks = n_slots // MOE_BLOCK + N_EXPERTS
    assert S % GLA_TILE == 0 and S % MIX_TILE == 0 and S % IN_TILE == 0
    assert N % (FINAL_PARTS * ROW_TILE) == 0
    assert n_slots % MOE_BLOCK == 0 and MOE_BLOCK & (MOE_BLOCK - 1) == 0 and n_blocks <= ITEM_LANES
    assert N_EXPERTS * (MOE_BLOCK - 1) <= N
    assert N_GROUPS + N_EXPERTS <= LANES and EXPERTS_PER_GROUP == 8 and 3 * rank <= LANES and D % (2 * ROW_PIECE) == 0

    mod = [_modulation(c, mod_w, mod_b, l)[0] for l in range(L)]
    xf = x.reshape(N, D)
    o_gd = 2 * dk + dv
    o_r = o_gd + rank
    o_b = o_r + dv
    for l in range(L):
        modl = mod[l].reshape(B, 6, D)
        wl = w_in[l]
        wa = jnp.concatenate([wl[:, :o_gd], wl[:, o_r:o_b]], axis=1).astype(BF16)
        wb = wl[:, o_b:].astype(BF16)
        wgd = wl[:, o_gd:o_r]
        wg = jnp.pad(jnp.concatenate([wgd, wgd, wgd], axis=1), ((0, 0), (0, LANES - 3 * rank))).astype(BF16)
        g_hi = gate_w2[l].astype(BF16)
        g_lo = (gate_w2[l] - g_hi.astype(F32)).astype(BF16)
        gw2 = jnp.pad(jnp.concatenate([g_hi, g_hi, g_lo], axis=0), ((0, LANES - 3 * rank), (0, 0)))
        if l == 0:
            za, zb, zg = _in_projection(xf, modl, norm1_g[l].reshape(1, D), conv_w[l], wa, wb, wg, S)
        else:
            xf, za, zb, zg = _in_projection(x1, modl, norm1_g[l].reshape(1, D), conv_w[l], wa, wb, wg, S,
                                            combine=(route, yk, mod[l - 1].reshape(B, 6, D)))
        og = _gla(za, zg, gw2, gate_b[l].reshape(1, dk), gla_norm_g[l].reshape(1, dv), B, S, dk, dv, rank)
        wr = jnp.pad(jnp.concatenate([router_group_w[l], router_expert_w[l]], axis=1),
                     ((0, 0), (0, LANES - N_GROUPS - N_EXPERTS)))
        wr_hi = wr.astype(BF16)
        wr = jnp.concatenate([wr_hi, (wr - wr_hi.astype(F32)).astype(BF16)], axis=1)
        br = jnp.pad(jnp.concatenate([router_group_b[l], router_expert_b[l]]),
                     (0, LANES - N_GROUPS - N_EXPERTS)).reshape(1, LANES)
        x1, h2, logits = _mixer_out(
            og, zb, xf, modl, w_gla_out[l].astype(BF16), w_conv_out[l].astype(BF16),
            w_out[l].astype(BF16), norm2_g[l].reshape(1, D), wr, br, B, S)
        route_t, route, cnt = _routing(logits)
        dest1, dest2, dest3, items = _destinations(route_t, cnt, n_blocks)
        xs = _sc_dispatch(h2, (dest1, dest2, dest3), n_blocks * MOE_BLOCK)
        yb = _experts(xs, items, expert_w1, expert_w3, expert_w2, l)
        if l < L - 1:
            yk = _sc_return(yb, jnp.concatenate([dest1, dest2]))
    out = None
    part = N // FINAL_PARTS
    for p in range(FINAL_PARTS):
        tok = slice(p * part, (p + 1) * part)
        yk = _sc_return(yb, jnp.concatenate([dest1[tok], dest2[tok]]))
        out = _final_combine(x1, route, modl, final_norm_g.reshape(1, D), yk, S,
                             p * part // ROW_TILE, out)
    return out.reshape(B, S, D)
```

```python
import functools

import jax
import jax.numpy as jnp
from jax import lax
from jax.experimental import pallas as pl
from jax.experimental.pallas import tpu as pltpu
from jax.experimental.pallas import tpu_sc as plsc

F32 = jnp.float32
BF16 = jnp.bfloat16
HIGHEST = lax.Precision.HIGHEST

GLA_HEADS = 4
GATE_TAU = 16.0
GLA_CHUNK = 64
N_GROUPS = 8
EXPERTS_PER_GROUP = 8
N_EXPERTS = N_GROUPS * EXPERTS_PER_GROUP
NORM_EPS = 1e-6

LANES = 128
VMEM_LIMIT_BYTES = 56 * 1024 * 1024

IN_TILE = 512
GLA_TILE = 512
MIX_TILE = 1024
MIX_SUB = 256
MOE_BLOCK = 512
ROW_TILE = 1024
FINAL_PARTS = 4
ROW_PIECE = 256
U32 = jnp.uint32

R_ID1, R_ID2, R_W1, R_W2, R_RANK1, R_RANK2 = 0, 1, 2, 3, 4, 5


def _dot(a, b):
    return jnp.dot(a, b, preferred_element_type=F32)


def _round_bf16(x):
    xb = x.astype(BF16)
    return xb, xb.astype(F32)


def _pack_rows(xr):
    half = xr.shape[1] // 2
    out = []
    for p in range(half // ROW_PIECE):
        lo = xr[:, p * ROW_PIECE:(p + 1) * ROW_PIECE]
        hi = xr[:, half + p * ROW_PIECE:half + (p + 1) * ROW_PIECE]
        out.append((pltpu.bitcast(lo, U32) >> 16) | pltpu.bitcast(hi, U32))
    return out


def _unpack_rows(ref):
    words = [ref[p] for p in range(ref.shape[0])]
    lo = [pltpu.bitcast(w << 16, F32) for w in words]
    hi = [pltpu.bitcast(w & jnp.uint32(0xFFFF0000), F32) for w in words]
    return jnp.concatenate(lo + hi, axis=-1)


def _const_spec(shape):
    nd = len(shape)
    return pl.BlockSpec(shape, lambda *_: (0,) * nd, pipeline_mode=pl.Buffered(1))


def _params(sem):
    return pltpu.CompilerParams(dimension_semantics=sem, vmem_limit_bytes=VMEM_LIMIT_BYTES)


MOD_ROWS = 256


def _mod_kernel(c_ref, w_ref, b_ref, o_ref):
    c = c_ref[...]
    sc = c * jax.nn.sigmoid(c)
    part = jnp.dot(sc, w_ref[0], precision=HIGHEST, preferred_element_type=F32)

    @pl.when(pl.program_id(1) == 0)
    def _():
        o_ref[0] = part + b_ref[0]

    @pl.when(pl.program_id(1) > 0)
    def _():
        o_ref[0] += part


def _modulation(c, mod_w, mod_b, layer):
    L, D, D6 = mod_w.shape
    B = c.shape[0]
    kb = MOD_ROWS
    return pl.pallas_call(
        _mod_kernel,
        grid=(1, D // kb),
        in_specs=[
            pl.BlockSpec((B, kb), lambda l, k: (0, k)),
            pl.BlockSpec((1, kb, D6), lambda l, k: (layer, k, 0)),
            pl.BlockSpec((1, 1, D6), lambda l, k: (layer, 0, 0)),
        ],
        out_specs=pl.BlockSpec((1, B, D6), lambda l, k: (0, 0, 0)),
        out_shape=jax.ShapeDtypeStruct((1, B, D6), F32),
        compiler_params=_params(("arbitrary", "arbitrary")),
        name="adaln_mod",
    )(c, mod_w, mod_b.reshape(L, 1, D6))


CONV_COLS = 256


def _moe_mix(route_ref, y0_ref, y1_ref):
    rec = route_ref[...]
    w1 = rec[:, R_W1:R_W1 + 1]
    w2 = rec[:, R_W2:R_W2 + 1]
    return w1 * _unpack_rows(y0_ref) + w2 * _unpack_rows(y1_ref)


def _inproj_kernel(*refs, seq, fuse_combine):
    if fuse_combine:
        (x_ref, route_ref, y0_ref, y1_ref, modp_ref, mod_ref, g_ref, cw_ref, wa_ref, wb_ref, wg_ref,
         xo_ref, za_ref, zb_ref, zg_ref, carry_ref) = refs
        x = x_ref[...] + modp_ref[0][5:6] * _moe_mix(route_ref, y0_ref, y1_ref)
        xo_ref[...] = x
    else:
        (x_ref, mod_ref, g_ref, cw_ref, wa_ref, wb_ref, wg_ref, za_ref, zb_ref, zg_ref, carry_ref) = refs
        x = x_ref[...]
    tm, d = x.shape

    @pl.when((pl.program_id(0) * tm) % seq == 0)
    def _():
        carry_ref[...] = jnp.zeros_like(carry_ref)

    ms = jnp.mean(x * x, axis=-1, keepdims=True)
    m = mod_ref[0]
    h = (x * lax.rsqrt(ms + NORM_EPS) * (g_ref[...] * (1.0 + m[1:2])) + m[0:1]).astype(BF16)
    for j in range(za_ref.shape[1] // d):
        za_ref[:, j * d:(j + 1) * d] = _dot(h, wa_ref[:, j * d:(j + 1) * d]).astype(BF16)
    zg_ref[...] = _dot(h, wg_ref[...])
    for j in range(1, zb_ref.shape[1] // d):
        zb_ref[:, j * d:(j + 1) * d] = _dot(h, wb_ref[:, (j + 2) * d:(j + 3) * d]).astype(BF16)

    w = CONV_COLS
    rowi = lax.broadcasted_iota(jnp.int32, (tm, w), 0)
    for j in range(d // w):
        cols = slice(j * w, (j + 1) * w)
        cb = _dot(h, wb_ref[:, j * w:(j + 1) * w])
        cc = _dot(h, wb_ref[:, d + j * w:d + (j + 1) * w])
        ch = _dot(h, wb_ref[:, 2 * d + j * w:2 * d + (j + 1) * w])
        u = cc * ch
        prev = carry_ref[:, cols]
        u1 = jnp.where(rowi == 0, prev[7:8], pltpu.roll(u, 1, 0))
        u2 = jnp.where(rowi == 0, prev[6:7], jnp.where(rowi == 1, prev[7:8], pltpu.roll(u, 2, 0)))
        carry_ref[:, cols] = u[tm - 8:tm]
        conv = cw_ref[0:1, cols] * u2 + cw_ref[1:2, cols] * u1 + cw_ref[2:3, cols] * u
        zb_ref[:, cols] = (cb * conv).astype(BF16)


def _in_projection(x, modl, norm_g, conv_w, wa, wb, wg, seq, combine=None):
    N, D = x.shape
    tm = IN_TILE
    nt = N // tm
    ca, cb = wa.shape[1], wb.shape[1] - 2 * D
    tok = lambda i: (i, 0)
    per_batch = lambda i: ((i * tm) // seq, 0, 0)
    in_specs = [pl.BlockSpec((tm, D), tok)]
    args = [x]
    out_specs, out_shape = [], []
    if combine is not None:
        route, yk, mod_prev = combine
        piece_blk = (yk.shape[0], tm, ROW_PIECE)
        in_specs += [
            pl.BlockSpec((tm, LANES), tok),
            pl.BlockSpec(piece_blk, lambda i: (0, i, 0)),
            pl.BlockSpec(piece_blk, lambda i: (0, nt + i, 0)),
            pl.BlockSpec((1, 6, D), per_batch),
        ]
        args += [route, yk, yk, mod_prev]
        out_specs.append(pl.BlockSpec((tm, D), tok))
        out_shape.append(jax.ShapeDtypeStruct((N, D), F32))
    in_specs += [
        pl.BlockSpec((1, 6, D), per_batch),
        _const_spec((1, D)),
        _const_spec(conv_w.shape),
        _const_spec((D, ca)),
        _const_spec(wb.shape),
        _const_spec((D, LANES)),
    ]
    args += [modl, norm_g, conv_w, wa, wb, wg]
    out_specs += [
        pl.BlockSpec((tm, ca), tok),
        pl.BlockSpec((tm, cb), tok),
        pl.BlockSpec((tm, LANES), tok),
    ]
    out_shape += [
        jax.ShapeDtypeStruct((N, ca), BF16),
        jax.ShapeDtypeStruct((N, cb), BF16),
        jax.ShapeDtypeStruct((N, LANES), F32),
    ]
    kern = functools.partial(_inproj_kernel, seq=seq, fuse_combine=combine is not None)
    return pl.pallas_call(
        kern,
        grid=(nt,),
        in_specs=in_specs,
        out_specs=out_specs,
        out_shape=out_shape,
        scratch_shapes=[pltpu.VMEM((8, D), F32)],
        compiler_params=_params(("arbitrary",)),
        name="in_projection",
    )(*args)


def _log_sigmoid(x):
    return jnp.minimum(x, 0.0) - jnp.log(1.0 + jnp.exp(-jnp.abs(x)))


def _gla_kernel(za_ref, zg_ref, gw2_ref, gb_ref, ng_ref, o_ref, st_ref, lg_ref, *, dk, dv, rank):
    heads = GLA_HEADS
    dkh, dvh = dk // heads, dv // heads
    c = GLA_CHUNK
    ts = za_ref.shape[0]

    @pl.when(pl.program_id(1) == 0)
    def _():
        st_ref[...] = jnp.zeros_like(st_ref)

    zg = zg_ref[...]
    zg_hi = zg.astype(BF16)
    zg_lo = (zg - zg_hi.astype(F32)).astype(BF16)
    lane = lax.broadcasted_iota(jnp.int32, zg.shape, 1)
    lhs = jnp.where(jnp.logical_and(lane >= rank, lane < 2 * rank), zg_lo, zg_hi)
    pre = _dot(lhs, gw2_ref[...]) + gb_ref[...]
    lg = _log_sigmoid(pre) * (1.0 / GATE_TAU)
    lg_hi = lg.astype(BF16)
    lg_ref[:, 0:dk] = lg_hi
    lg_ref[:, dk:2 * dk] = (lg - lg_hi.astype(F32)).astype(BF16)

    row = lax.broadcasted_iota(jnp.int32, (c, c), 0)
    col = lax.broadcasted_iota(jnp.int32, (c, c), 1)
    causal = row >= col
    tril = causal.astype(BF16)
    qscale = dkh ** -0.5

    for ci in range(ts // c):
        rows = slice(ci * c, (ci + 1) * c)
        b_two = _dot(tril, lg_ref[rows, :])
        b_all = b_two[:, 0:dk] + b_two[:, dk:2 * dk]
        for hd in range(heads):
            ks = slice(hd * dkh, (hd + 1) * dkh)
            q = za_ref[rows, hd * dkh:(hd + 1) * dkh].astype(F32) * qscale
            k = za_ref[rows, dk + hd * dkh:dk + (hd + 1) * dkh].astype(F32)
            v = za_ref[rows, 2 * dk + hd * dvh:2 * dk + (hd + 1) * dvh]
            r = za_ref[rows, 2 * dk + dv + hd * dvh:2 * dk + dv + (hd + 1) * dvh].astype(F32)
            b = b_all[:, ks]
            b_last = b[c - 1:c, :]
            q_t = (q * jnp.exp(b)).astype(BF16)
            k_t = (k * jnp.exp(-b)).astype(BF16)
            k_s = (k * jnp.exp(b_last - b)).astype(BF16)
            decay = jnp.exp(b_last)
            attn = lax.dot_general(q_t, k_t, (((1,), (1,)), ((), ())), preferred_element_type=F32)
            attn = jnp.where(causal, attn, 0.0).astype(BF16)
            st = st_ref[hd]
            o = _dot(attn, v) + lax.dot_general(
                q_t, st.astype(BF16), (((1,), (1,)), ((), ())), preferred_element_type=F32)
            upd = lax.dot_general(v, k_s, (((0,), (0,)), ((), ())), preferred_element_type=F32)
            st_ref[hd] = st * decay + upd
            ms = jnp.mean(o * o, axis=-1, keepdims=True)
            on = o * lax.rsqrt(ms + NORM_EPS) * ng_ref[:, hd * dvh:(hd + 1) * dvh]
            o_ref[rows, hd * dvh:(hd + 1) * dvh] = (on * (r * jax.nn.sigmoid(r))).astype(BF16)


def _gla(za, zg, gw2, gb, ng, batch, seq, dk, dv, rank):
    N = za.shape[0]
    ts = GLA_TILE
    ns = seq // ts
    heads = GLA_HEADS
    kern = functools.partial(_gla_kernel, dk=dk, dv=dv, rank=rank)
    return pl.pallas_call(
        kern,
        grid=(batch, ns),
        in_specs=[
            pl.BlockSpec((ts, za.shape[1]), lambda b, s: (b * ns + s, 0)),
            pl.BlockSpec((ts, LANES), lambda b, s: (b * ns + s, 0)),
            _const_spec((LANES, dk)),
            _const_spec((1, dk)),
            _const_spec((1, dv)),
        ],
        out_specs=pl.BlockSpec((ts, dv), lambda b, s: (b * ns + s, 0)),
        out_shape=jax.ShapeDtypeStruct((N, dv), BF16),
        scratch_shapes=[
            pltpu.VMEM((heads, dv // heads, dk // heads), F32),
            pltpu.VMEM((ts, 2 * dk), BF16),
        ],
        compiler_params=_params(("arbitrary", "arbitrary")),
        name="gla",
    )(za, zg, gw2, gb, ng)


def _mixout_kernel(og_ref, zb_ref, x_ref, mod_ref, wga_ref, wco_ref, wo_ref, n2_ref,
                   wr_ref, br_ref, x1_ref, h2_ref, logit_ref):
    tm, d = x_ref.shape
    m = mod_ref[0]
    sub = MIX_SUB
    gain2 = n2_ref[...] * (1.0 + m[4:5])
    for r0 in range(0, tm, sub):
        rs = slice(r0, r0 + sub)
        ga = zb_ref[rs, d:2 * d].astype(F32)
        gc = zb_ref[rs, 2 * d:3 * d].astype(F32)
        y_conv = _dot(zb_ref[rs, 0:d], wco_ref[...])
        y_gla = _dot(og_ref[rs, :], wga_ref[...])
        y = jax.nn.sigmoid(ga) * y_gla + jax.nn.sigmoid(gc) * y_conv
        y = _dot(y.astype(BF16), wo_ref[...])
        x1 = x_ref[rs, :] + m[2:3] * y
        x1_ref[rs, :] = x1

        ms = jnp.mean(x1 * x1, axis=-1, keepdims=True)
        h2 = x1 * lax.rsqrt(ms + NORM_EPS) * gain2 + m[3:4]
        h_hi, h_r = _round_bf16(h2)
        for p, words in enumerate(_pack_rows(h_r)):
            h2_ref[p, rs, :] = words

        h_lo = (h2 - h_r).astype(BF16)
        two = _dot(h_hi, wr_ref[...])
        logits = two[:, 0:LANES] + two[:, LANES:2 * LANES] + _dot(h_lo, wr_ref[:, 0:LANES]) + br_ref[...]
        logit_ref[rs, :] = logits


def _mixer_out(og, zb, x, modl, wga, wco, wo, n2g, wr, br, batch, seq):
    N, D = x.shape
    tm = MIX_TILE
    ns = seq // tm
    tok = lambda b, s: (b * ns + s, 0)
    return pl.pallas_call(
        _mixout_kernel,
        grid=(batch, ns),
        in_specs=[
            pl.BlockSpec((tm, D), tok),
            pl.BlockSpec((tm, zb.shape[1]), tok),
            pl.BlockSpec((tm, D), tok),
            pl.BlockSpec((1, 6, D), lambda b, s: (b, 0, 0)),
            _const_spec((D, D)),
            _const_spec((D, D)),
            _const_spec((D, D)),
            _const_spec((1, D)),
            _const_spec((D, 2 * LANES)),
            _const_spec((1, LANES)),
        ],
        out_specs=[
            pl.BlockSpec((tm, D), tok),
            pl.BlockSpec((D // 2 // ROW_PIECE, tm, ROW_PIECE), lambda b, s: (0, b * ns + s, 0)),
            pl.BlockSpec((tm, LANES), tok),
        ],
        out_shape=[
            jax.ShapeDtypeStruct((N, D), F32),
            jax.ShapeDtypeStruct((D // 2 // ROW_PIECE, N, ROW_PIECE), U32),
            jax.ShapeDtypeStruct((N, LANES), F32),
        ],
        compiler_params=_params(("arbitrary", "arbitrary")),
        name="mixer_out",
    )(og, zb, x, modl, wga, wco, wo, n2g, wr, br)


def _route_kernel(logit_ref, routet_ref, route_ref, cnt_ref, run_ref):
    tm = logit_ref.shape[0]
    sub = MIX_SUB
    rows8 = EXPERTS_PER_GROUP

    @pl.when(pl.program_id(0) == 0)
    def _():
        run_ref[...] = jnp.zeros_like(run_ref)

    sub8 = lax.broadcasted_iota(jnp.int32, (rows8, sub), 0)
    erow = lax.broadcasted_iota(jnp.int32, (N_EXPERTS, sub), 0)
    tr = lax.broadcasted_iota(jnp.int32, (sub, sub), 0)
    tc = lax.broadcasted_iota(jnp.int32, (sub, sub), 1)
    earlier = (tr < tc).astype(BF16)
    ones = jnp.ones((sub, LANES), BF16)
    neg = -jnp.inf
    run = run_ref[...]
    for r0 in range(0, tm, sub):
        lt = logit_ref[r0:r0 + sub, :].T
        gl = lt[0:N_GROUPS, :]
        gmax = jnp.max(gl, axis=0, keepdims=True)
        gsum = jnp.sum(jnp.exp(gl - gmax), axis=0, keepdims=True)
        g_w = 1.0 / gsum
        g_idx = jnp.min(jnp.where(gl == gmax, sub8, N_GROUPS), axis=0, keepdims=True)
        el = lt[N_GROUPS:N_GROUPS + rows8, :]
        for g in range(1, N_GROUPS):
            el = jnp.where(g_idx == g, lt[N_GROUPS + g * rows8:N_GROUPS + (g + 1) * rows8, :], el)
        e1 = jnp.max(el, axis=0, keepdims=True)
        i1 = jnp.min(jnp.where(el == e1, sub8, rows8), axis=0, keepdims=True)
        el2 = jnp.where(sub8 == i1, neg, el)
        e2 = jnp.max(el2, axis=0, keepdims=True)
        i2 = jnp.min(jnp.where(el2 == e2, sub8, rows8), axis=0, keepdims=True)
        ratio = jnp.exp(e2 - e1)
        w1 = g_w / (1.0 + ratio)
        w2 = g_w * ratio / (1.0 + ratio)
        id1 = g_idx * rows8 + i1
        id2 = g_idx * rows8 + i2

        oh1 = erow == id1
        oh2 = erow == id2
        oh1b = jnp.where(oh1, 1.0, 0.0).astype(BF16)
        oh2b = jnp.where(oh2, 1.0, 0.0).astype(BF16)
        tot1 = _dot(oh1b, ones)
        tot2 = _dot(oh2b, ones)
        base1 = jnp.concatenate([run] * (sub // LANES), axis=1)
        base2 = jnp.concatenate([run + tot1] * (sub // LANES), axis=1)
        c1 = _dot(oh1b, earlier) + base1
        c2 = _dot(oh2b, earlier) + base2
        rank1 = jnp.sum(jnp.where(oh1, c1, 0.0), axis=0, keepdims=True)
        rank2 = jnp.sum(jnp.where(oh2, c2, 0.0), axis=0, keepdims=True)
        run = run + tot1 + tot2

        rec = jnp.where(sub8 == R_ID1, id1.astype(F32), 0.0)
        rec = jnp.where(sub8 == R_ID2, id2.astype(F32), rec)
        rec = jnp.where(sub8 == R_W1, w1, rec)
        rec = jnp.where(sub8 == R_W2, w2, rec)
        rec = jnp.where(sub8 == R_RANK1, rank1, rec)
        rec = jnp.where(sub8 == R_RANK2, rank2, rec)
        routet_ref[:, r0:r0 + sub] = rec
        rec_full = jnp.concatenate([rec, jnp.zeros((LANES - rows8, sub), F32)], axis=0)
        route_ref[r0:r0 + sub, :] = rec_full.T
    run_ref[...] = run
    cnt_ref[...] = run


def _routing(logits):
    N = logits.shape[0]
    tm = 2048 if N % 2048 == 0 else MIX_SUB
    return pl.pallas_call(
        _route_kernel,
        grid=(N // tm,),
        in_specs=[pl.BlockSpec((tm, LANES), lambda i: (i, 0))],
        out_specs=[
            pl.BlockSpec((8, tm), lambda i: (0, i)),
            pl.BlockSpec((tm, LANES), lambda i: (i, 0)),
            pl.BlockSpec((N_EXPERTS, LANES), lambda i: (0, 0)),
        ],
        out_shape=[
            jax.ShapeDtypeStruct((8, N), F32),
            jax.ShapeDtypeStruct((N, LANES), F32),
            jax.ShapeDtypeStruct((N_EXPERTS, LANES), F32),
        ],
        scratch_shapes=[pltpu.VMEM((N_EXPERTS, LANES), F32)],
        compiler_params=_params(("arbitrary",)),
        name="moe_route",
    )(logits)


SC_WINDOW = 128


def _sc_mesh():
    return plsc.VectorSubcoreMesh(core_axis_name="core", subcore_axis_name="subcore")


def _piece_index(rows, pieces, n_rows):
    return (jnp.arange(pieces, dtype=jnp.int32)[:, None] * n_rows + rows[None, :]).reshape(1, -1)


def _sc_dispatch(h2, dests, n_rows):
    pieces, N, pc = h2.shape
    w = SC_WINDOW

    @functools.partial(pl.kernel, out_type=jax.ShapeDtypeStruct((pieces * n_rows, pc), h2.dtype),
                       mesh=_sc_mesh(), scratch_types=[], name="moe_dispatch_sc")
    def run(x_hbm, *refs):
        o_hbm = refs[-1]

        def body(x_vmem, *idx_vmem):
            for i_vmem in idx_vmem:
                pltpu.sync_copy(x_vmem, o_hbm.at[i_vmem.at[0]])

        pltpu.emit_pipeline(
            body,
            grid=(pieces * N // w,),
            in_specs=[pl.BlockSpec((w, pc), lambda i: (i, 0))]
            + [pl.BlockSpec((1, w), lambda i: (0, i))] * len(dests),
            out_specs=[],
            core_axis_name=("core", "subcore"),
            dimension_semantics=(pltpu.PARALLEL,),
        )(x_hbm, *refs[:-1])

    xs = run(h2.reshape(pieces * N, pc), *[_piece_index(d, pieces, n_rows) for d in dests])
    return xs.reshape(pieces, n_rows, pc)


def _sc_return(yb, dest):
    M = dest.shape[0]
    pieces, P, pc = yb.shape
    w = SC_WINDOW

    @functools.partial(pl.kernel, out_type=jax.ShapeDtypeStruct((pieces * M, pc), yb.dtype),
                       mesh=_sc_mesh(), scratch_types=[], name="moe_return_sc")
    def run(y_hbm, i_hbm, o_hbm):
        def body(i_vmem, o_vmem):
            pltpu.sync_copy(y_hbm.at[i_vmem.at[0]], o_vmem)

        pltpu.emit_pipeline(
            body,
            grid=(pieces * M // w,),
            in_specs=[pl.BlockSpec((1, w), lambda i: (0, i))],
            out_specs=[pl.BlockSpec((w, pc), lambda i: (i, 0))],
            core_axis_name=("core", "subcore"),
            dimension_semantics=(pltpu.PARALLEL,),
        )(i_hbm, o_hbm)

    return run(yb.reshape(pieces * P, pc), _piece_index(dest, pieces, P)).reshape(pieces, M, pc)


def _expert_kernel(ib_ref, ie_ref, nxt_ref, slot_ref, nb_ref, xs_ref, w1_ref, w3_ref, w2_ref, y_ref,
                   wf1, wf3, wf2, w1b, w3b, w2b, sem, *, layer):
    j = pl.program_id(0)
    jp = jnp.maximum(j - 1, 0)
    live = j < nb_ref[0]
    new_expert = jnp.logical_or(j == 0, ie_ref[j] != ie_ref[jp])
    streams = ((w1_ref, wf1), (w3_ref, wf3), (w2_ref, wf2))

    def weight_copies(expert, s):
        return [pltpu.make_async_copy(w.at[layer, expert], buf.at[s], sem.at[s, k])
                for k, (w, buf) in enumerate(streams)]

    @pl.when(j == 0)
    def _():
        for cp in weight_copies(ie_ref[0], 0):
            cp.start()

    @pl.when(jnp.logical_and(new_expert, live))
    def _():
        s = slot_ref[j]
        for cp in weight_copies(ie_ref[j], s):
            cp.wait()
        w1b[...] = wf1[s].astype(BF16)
        w3b[...] = wf3[s].astype(BF16)
        w2b[...] = wf2[s].astype(BF16)

        @pl.when(nxt_ref[j] >= 0)
        def _():
            for cp in weight_copies(nxt_ref[j], 1 - s):
                cp.start()

    @pl.when(live)
    def _():
        xb = _unpack_rows(xs_ref).astype(BF16)
        h1 = _dot(xb, w1b[...])
        h3 = _dot(xb, w3b[...])
        a = (h1 * jax.nn.sigmoid(h1) * h3).astype(BF16)
        for p, words in enumerate(_pack_rows(_round_bf16(_dot(a, w2b[...]))[1])):
            y_ref[p] = words


def _experts(xs, items, w1, w3, w2, layer):
    pieces, P, pc = xs.shape
    _, E, D, DE = w1.shape
    bm = MOE_BLOCK
    n_items_max = P // bm
    rows = lambda j, *prefetch: (0, prefetch[0][j], 0)
    return pl.pallas_call(
        functools.partial(_expert_kernel, layer=layer),
        grid_spec=pltpu.PrefetchScalarGridSpec(
            num_scalar_prefetch=5,
            grid=(n_items_max,),
            in_specs=[
                pl.BlockSpec((pieces, bm, pc), rows),
                pl.BlockSpec(memory_space=pl.ANY),
                pl.BlockSpec(memory_space=pl.ANY),
                pl.BlockSpec(memory_space=pl.ANY),
            ],
            out_specs=pl.BlockSpec((pieces, bm, pc), rows),
            scratch_shapes=[
                pltpu.VMEM((2, D, DE), F32),
                pltpu.VMEM((2, D, DE), F32),
                pltpu.VMEM((2, DE, D), F32),
                pltpu.VMEM((D, DE), BF16),
                pltpu.VMEM((D, DE), BF16),
                pltpu.VMEM((DE, D), BF16),
                pltpu.SemaphoreType.DMA((2, 3)),
            ],
        ),
        out_shape=jax.ShapeDtypeStruct((pieces, P, pc), U32),
        compiler_params=_params(("arbitrary",)),
        name="moe_experts",
    )(*items, xs, w1, w3, w2)


def _combine_kernel(x1_ref, route_ref, mod_ref, fg_ref, y0_ref, y1_ref, *rest):
    o_ref = rest[-1]
    x2 = x1_ref[...] + mod_ref[0][5:6] * _moe_mix(route_ref, y0_ref, y1_ref)
    ms = jnp.mean(x2 * x2, axis=-1, keepdims=True)
    o_ref[...] = x2 * lax.rsqrt(ms + NORM_EPS) * fg_ref[...]


def _final_combine(x1, route, modl, final_g, yk, seq, first_tile, prev_out):
    N, D = x1.shape
    tm = ROW_TILE
    nt = yk.shape[1] // 2 // tm
    tok = lambda i: (first_tile + i, 0)
    in_specs = [
        pl.BlockSpec((tm, D), tok),
        pl.BlockSpec((tm, LANES), tok),
        pl.BlockSpec((1, 6, D), lambda i: (((first_tile + i) * tm) // seq, 0, 0)),
        pl.BlockSpec((1, D), lambda i: (0, 0)),
        pl.BlockSpec((yk.shape[0], tm, ROW_PIECE), lambda i: (0, i, 0)),
        pl.BlockSpec((yk.shape[0], tm, ROW_PIECE), lambda i: (0, nt + i, 0)),
    ]
    args = [x1, route, modl, final_g, yk, yk]
    aliases = {}
    if prev_out is not None:
        in_specs.append(pl.BlockSpec(memory_space=pl.ANY))
        args.append(prev_out)
        aliases = {len(args) - 1: 0}
    return pl.pallas_call(
        _combine_kernel,
        grid=(nt,),
        in_specs=in_specs,
        out_specs=pl.BlockSpec((tm, D), tok),
        out_shape=jax.ShapeDtypeStruct((N, D), F32),
        input_output_aliases=aliases,
        compiler_params=_params(("arbitrary",)),
        name="moe_combine",
    )(*args)


ITEM_LANES = 256
I_BLOCK, I_EXPERT, I_NEXT, I_SLOT, I_COUNT = 0, 1, 2, 3, 4


def _dest_kernel(routet_ref, cnt_ref, d1_ref, d2_ref, d3_ref, items_ref):
    tm = routet_ref.shape[1]
    ne = N_EXPERTS
    bm = float(MOE_BLOCK)
    cnt = cnt_ref[...]
    r = lax.broadcasted_iota(jnp.int32, (ne, ne), 0)
    c = lax.broadcasted_iota(jnp.int32, (ne, ne), 1)
    lower = (r >= c).astype(F32)
    cumsum = lambda a: jnp.dot(lower, a, precision=HIGHEST, preferred_element_type=F32)
    padded = jnp.floor((cnt + (bm - 1.0)) * (1.0 / bm)) * bm
    pend = cumsum(padded)
    pstart = pend - padded
    npad = padded - cnt
    pad_end = cumsum(npad)
    pad_start = pad_end - npad

    tile = lambda a, n: jnp.concatenate([a] * (n // LANES), axis=1)
    expert = lax.broadcasted_iota(jnp.int32, (ne, tm), 0).astype(F32)
    rec = routet_ref[...]
    pstart_t = tile(pstart, tm)

    def sorted_row(row_id, row_rank):
        sel = expert == rec[row_id:row_id + 1, :]
        return jnp.sum(jnp.where(sel, pstart_t, 0.0), axis=0, keepdims=True) + rec[row_rank:row_rank + 1, :]

    dest1 = sorted_row(R_ID1, R_RANK1)
    dest2 = sorted_row(R_ID2, R_RANK2)
    t = (lax.broadcasted_iota(jnp.int32, (1, tm), 1) + pl.program_id(0) * tm).astype(F32)
    ps_t, pe_t = tile(pad_start, tm), tile(pad_end, tm)
    in_e = jnp.logical_and(ps_t <= t, t < pe_t)
    pad_row = jnp.sum(jnp.where(in_e, tile(pstart + cnt, tm) + (t - ps_t), 0.0), axis=0, keepdims=True)
    dest3 = jnp.where(t < pe_t[ne - 1:ne, :], pad_row, dest1)
    d1_ref[0] = dest1.astype(jnp.int32)
    d2_ref[0] = dest2.astype(jnp.int32)
    d3_ref[0] = dest3.astype(jnp.int32)

    @pl.when(pl.program_id(0) == 0)
    def _():
        nl = ITEM_LANES
        pend_i = tile(pend, nl)
        n_items = pend_i[ne - 1:ne, :] * (1.0 / bm)
        blk = jnp.minimum(lax.broadcasted_iota(jnp.int32, (1, nl), 1).astype(F32), n_items - 1.0)
        erow = lax.broadcasted_iota(jnp.int32, (ne, nl), 0).astype(F32)
        ie = jnp.sum(jnp.where(pend_i <= blk * bm, 1.0, 0.0), axis=0, keepdims=True)
        nonempty = tile(cnt, nl) > 0.0
        order = jnp.sum(jnp.where(jnp.logical_and(nonempty, erow < ie), 1.0, 0.0), axis=0, keepdims=True)
        nxt = jnp.min(jnp.where(jnp.logical_and(nonempty, erow > ie), erow, float(ne)), axis=0, keepdims=True)
        nxt = jnp.where(nxt == float(ne), -1.0, nxt)
        slot = order - 2.0 * jnp.floor(order * 0.5)
        row = lax.broadcasted_iota(jnp.int32, (8, nl), 0)
        tab = jnp.where(row == I_BLOCK, blk, 0.0)
        tab = jnp.where(row == I_EXPERT, ie, tab)
        tab = jnp.where(row == I_NEXT, nxt, tab)
        tab = jnp.where(row == I_SLOT, slot, tab)
        tab = jnp.where(row == I_COUNT, n_items, tab)
        items_ref[...] = tab.astype(jnp.int32)


def _destinations(route_t, cnt, n_blocks):
    N = route_t.shape[1]
    tm = 2048 if N % 2048 == 0 else MIX_SUB
    out = jax.ShapeDtypeStruct((N // tm, 1, tm), jnp.int32)
    d1, d2, d3, items = pl.pallas_call(
        _dest_kernel,
        grid=(N // tm,),
        in_specs=[pl.BlockSpec((8, tm), lambda i: (0, i)),
                  pl.BlockSpec((N_EXPERTS, LANES), lambda i: (0, 0))],
        out_specs=[pl.BlockSpec((1, 1, tm), lambda i: (i, 0, 0))] * 3
        + [pl.BlockSpec((8, ITEM_LANES), lambda i: (0, 0))],
        out_shape=[out, out, out, jax.ShapeDtypeStruct((8, ITEM_LANES), jnp.int32)],
        compiler_params=_params(("arbitrary",)),
        name="moe_dest",
    )(route_t, cnt)
    table = tuple(items[k, :n_blocks] for k in (I_BLOCK, I_EXPERT, I_NEXT, I_SLOT)) + (items[I_COUNT, :1],)
    return d1.reshape(N), d2.reshape(N), d3.reshape(N), table


def kernel(x, c, mod_w, mod_b, norm1_g, w_in, gate_w2, gate_b, gla_norm_g, conv_w, w_gla_out,
           w_conv_out, w_out, norm2_g, router_group_w, router_group_b, router_expert_w,
           router_expert_b, expert_w1, expert_w3, expert_w2, final_norm_g):
    B, S, D = x.shape
    L = mod_w.shape[0]
    N = B * S
    dk = gate_w2.shape[2]
    rank = gate_w2.shape[1]
    dv = gla_norm_g.shape[1]
    n_slots = N * 2
    n_blocks = n_slots // MOE_BLOCK + N_EXPERTS
    assert S % GLA_TILE == 0 and S % MIX_TILE == 0 and S % IN_TILE == 0
    assert N % (FINAL_PARTS * ROW_TILE) == 0
    assert n_slots % MOE_BLOCK == 0 and MOE_BLOCK & (MOE_BLOCK - 1) == 0 and n_blocks <= ITEM_LANES
    assert N_EXPERTS * (MOE_BLOCK - 1) <= N
    assert N_GROUPS + N_EXPERTS <= LANES and EXPERTS_PER_GROUP == 8 and 3 * rank <= LANES and D % (2 * ROW_PIECE) == 0

    mod = [_modulation(c, mod_w, mod_b, l)[0] for l in range(L)]
    xf = x.reshape(N, D)
    o_gd = 2 * dk + dv
    o_r = o_gd + rank
    o_b = o_r + dv
    for l in range(L):
        modl = mod[l].reshape(B, 6, D)
        wl = w_in[l]
        wa = jnp.concatenate([wl[:, :o_gd], wl[:, o_r:o_b]], axis=1).astype(BF16)
        wb = wl[:, o_b:].astype(BF16)
        wgd = wl[:, o_gd:o_r]
        wg = jnp.pad(jnp.concatenate([wgd, wgd, wgd], axis=1), ((0, 0), (0, LANES - 3 * rank))).astype(BF16)
        g_hi = gate_w2[l].astype(BF16)
        g_lo = (gate_w2[l] - g_hi.astype(F32)).astype(BF16)
        gw2 = jnp.pad(jnp.concatenate([g_hi, g_hi, g_lo], axis=0), ((0, LANES - 3 * rank), (0, 0)))
        if l == 0:
            za, zb, zg = _in_projection(xf, modl, norm1_g[l].reshape(1, D), conv_w[l], wa, wb, wg, S)
        else:
            xf, za, zb, zg = _in_projection(x1, modl, norm1_g[l].reshape(1, D), conv_w[l], wa, wb, wg, S,
                                            combine=(route, yk, mod[l - 1].reshape(B, 6, D)))
        og = _gla(za, zg, gw2, gate_b[l].reshape(1, dk), gla_norm_g[l].reshape(1, dv), B, S, dk, dv, rank)
        wr = jnp.pad(jnp.concatenate([router_group_w[l], router_expert_w[l]], axis=1),
                     ((0, 0), (0, LANES - N_GROUPS - N_EXPERTS)))
        wr_hi = wr.astype(BF16)
        wr = jnp.concatenate([wr_hi, (wr - wr_hi.astype(F32)).astype(BF16)], axis=1)
        br = jnp.pad(jnp.concatenate([router_group_b[l], router_expert_b[l]]),
                     (0, LANES - N_GROUPS - N_EXPERTS)).reshape(1, LANES)
        x1, h2, logits = _mixer_out(
            og, zb, xf, modl, w_gla_out[l].astype(BF16), w_conv_out[l].astype(BF16),
            w_out[l].astype(BF16), norm2_g[l].reshape(1, D), wr, br, B, S)
        route_t, route, cnt = _routing(logits)
        dest1, dest2, dest3, items = _destinations(route_t, cnt, n_blocks)
        xs = _sc_dispatch(h2, (dest1, dest2, dest3), n_blocks * MOE_BLOCK)
        yb = _experts(xs, items, expert_w1, expert_w3, expert_w2, l)
        if l < L - 1:
            yk = _sc_return(yb, jnp.concatenate([dest1, dest2]))
    out = None
    part = N // FINAL_PARTS
    for p in range(FINAL_PARTS):
        tok = slice(p * part, (p + 1) * part)
        yk = _sc_return(yb, jnp.concatenate([dest1[tok], dest2[tok]]))
        out = _final_combine(x1, route, modl, final_norm_g.reshape(1, D), yk, S,
                             p * part // ROW_TILE, out)
    return out.reshape(B, S, D)
```

```python
import functools

import jax
import jax.numpy as jnp
from jax import lax
from jax.experimental import pallas as pl
from jax.experimental.pallas import tpu as pltpu
from jax.experimental.pallas import tpu_sc as plsc

F32 = jnp.float32
BF16 = jnp.bfloat16
HIGHEST = lax.Precision.HIGHEST

GLA_HEADS = 4
GATE_TAU = 16.0
GLA_CHUNK = 64
N_GROUPS = 8
EXPERTS_PER_GROUP = 8
N_EXPERTS = N_GROUPS * EXPERTS_PER_GROUP
NORM_EPS = 1e-6

LANES = 128
VMEM_LIMIT_BYTES = 56 * 1024 * 1024

IN_TILE = 512
GLA_TILE = 512
MIX_TILE = 1024
MIX_SUB = 1024
ROUTE_SUB = 256
MOE_BLOCK = 512
ROW_TILE = 1024
FINAL_PARTS = 4
ROW_PIECE = 256
U32 = jnp.uint32

R_ID1, R_ID2, R_W1, R_W2, R_RANK1, R_RANK2 = 0, 1, 2, 3, 4, 5


def _dot(a, b):
    return jnp.dot(a, b, preferred_element_type=F32)


def _round_bf16(x):
    xb = x.astype(BF16)
    return xb, xb.astype(F32)


def _pack_rows(xr):
    half = xr.shape[1] // 2
    out = []
    for p in range(half // ROW_PIECE):
        lo = xr[:, p * ROW_PIECE:(p + 1) * ROW_PIECE]
        hi = xr[:, half + p * ROW_PIECE:half + (p + 1) * ROW_PIECE]
        out.append((pltpu.bitcast(lo, U32) >> 16) | pltpu.bitcast(hi, U32))
    return out


def _unpack_rows(ref):
    words = [ref[p] for p in range(ref.shape[0])]
    lo = [pltpu.bitcast(w << 16, F32) for w in words]
    hi = [pltpu.bitcast(w & jnp.uint32(0xFFFF0000), F32) for w in words]
    return jnp.concatenate(lo + hi, axis=-1)


def _const_spec(shape):
    nd = len(shape)
    return pl.BlockSpec(shape, lambda *_: (0,) * nd, pipeline_mode=pl.Buffered(1))


def _params(sem):
    return pltpu.CompilerParams(dimension_semantics=sem, vmem_limit_bytes=VMEM_LIMIT_BYTES)


MOD_ROWS = 256


def _mod_kernel(c_ref, w_ref, b_ref, o_ref):
    c = c_ref[...]
    sc = c * jax.nn.sigmoid(c)
    part = jnp.dot(sc, w_ref[0], precision=HIGHEST, preferred_element_type=F32)

    @pl.when(pl.program_id(1) == 0)
    def _():
        o_ref[0] = part + b_ref[0]

    @pl.when(pl.program_id(1) > 0)
    def _():
        o_ref[0] += part


def _modulation(c, mod_w, mod_b, layer):
    L, D, D6 = mod_w.shape
    B = c.shape[0]
    kb = MOD_ROWS
    return pl.pallas_call(
        _mod_kernel,
        grid=(1, D // kb),
        in_specs=[
            pl.BlockSpec((B, kb), lambda l, k: (0, k)),
            pl.BlockSpec((1, kb, D6), lambda l, k: (layer, k, 0)),
            pl.BlockSpec((1, 1, D6), lambda l, k: (layer, 0, 0)),
        ],
        out_specs=pl.BlockSpec((1, B, D6), lambda l, k: (0, 0, 0)),
        out_shape=jax.ShapeDtypeStruct((1, B, D6), F32),
        compiler_params=_params(("arbitrary", "arbitrary")),
        name="adaln_mod",
    )(c, mod_w, mod_b.reshape(L, 1, D6))


CONV_COLS = 256


def _moe_mix(route_ref, y0_ref, y1_ref):
    rec = route_ref[...]
    w1 = rec[:, R_W1:R_W1 + 1]
    w2 = rec[:, R_W2:R_W2 + 1]
    return w1 * _unpack_rows(y0_ref) + w2 * _unpack_rows(y1_ref)


def _inproj_kernel(*refs, seq, fuse_combine):
    if fuse_combine:
        (x_ref, route_ref, y0_ref, y1_ref, modp_ref, mod_ref, g_ref, cw_ref, wa_ref, wb_ref, wg_ref,
         xo_ref, za_ref, zb_ref, zg_ref, carry_ref) = refs
        x = x_ref[...] + modp_ref[0][5:6] * _moe_mix(route_ref, y0_ref, y1_ref)
        xo_ref[...] = x
    else:
        (x_ref, mod_ref, g_ref, cw_ref, wa_ref, wb_ref, wg_ref, za_ref, zb_ref, zg_ref, carry_ref) = refs
        x = x_ref[...]
    tm, d = x.shape

    @pl.when((pl.program_id(0) * tm) % seq == 0)
    def _():
        carry_ref[...] = jnp.zeros_like(carry_ref)

    ms = jnp.mean(x * x, axis=-1, keepdims=True)
    m = mod_ref[0]
    h = (x * lax.rsqrt(ms + NORM_EPS) * (g_ref[...] * (1.0 + m[1:2])) + m[0:1]).astype(BF16)
    for j in range(za_ref.shape[1] // d):
        za_ref[:, j * d:(j + 1) * d] = _dot(h, wa_ref[:, j * d:(j + 1) * d]).astype(BF16)
    zg_ref[...] = _dot(h, wg_ref[...])
    for j in range(1, zb_ref.shape[1] // d):
        zb_ref[:, j * d:(j + 1) * d] = _dot(h, wb_ref[:, (j + 2) * d:(j + 3) * d]).astype(BF16)

    w = CONV_COLS
    rowi = lax.broadcasted_iota(jnp.int32, (tm, w), 0)
    for j in range(d // w):
        cols = slice(j * w, (j + 1) * w)
        cb = _dot(h, wb_ref[:, j * w:(j + 1) * w])
        cc = _dot(h, wb_ref[:, d + j * w:d + (j + 1) * w])
        ch = _dot(h, wb_ref[:, 2 * d + j * w:2 * d + (j + 1) * w])
        u = cc * ch
        prev = carry_ref[:, cols]
        u1 = jnp.where(rowi == 0, prev[7:8], pltpu.roll(u, 1, 0))
        u2 = jnp.where(rowi == 0, prev[6:7], jnp.where(rowi == 1, prev[7:8], pltpu.roll(u, 2, 0)))
        carry_ref[:, cols] = u[tm - 8:tm]
        conv = cw_ref[0:1, cols] * u2 + cw_ref[1:2, cols] * u1 + cw_ref[2:3, cols] * u
        zb_ref[:, cols] = (cb * conv).astype(BF16)


def _in_projection(x, modl, norm_g, conv_w, wa, wb, wg, seq, combine=None):
    N, D = x.shape
    tm = IN_TILE
    nt = N // tm
    ca, cb = wa.shape[1], wb.shape[1] - 2 * D
    tok = lambda i: (i, 0)
    per_batch = lambda i: ((i * tm) // seq, 0, 0)
    in_specs = [pl.BlockSpec((tm, D), tok)]
    args = [x]
    out_specs, out_shape = [], []
    if combine is not None:
        route, yk, mod_prev = combine
        piece_blk = (yk.shape[0], tm, ROW_PIECE)
        in_specs += [
            pl.BlockSpec((tm, LANES), tok),
            pl.BlockSpec(piece_blk, lambda i: (0, i, 0)),
            pl.BlockSpec(piece_blk, lambda i: (0, nt + i, 0)),
            pl.BlockSpec((1, 6, D), per_batch),
        ]
        args += [route, yk, yk, mod_prev]
        out_specs.append(pl.BlockSpec((tm, D), tok))
        out_shape.append(jax.ShapeDtypeStruct((N, D), F32))
    in_specs += [
        pl.BlockSpec((1, 6, D), per_batch),
        _const_spec((1, D)),
        _const_spec(conv_w.shape),
        _const_spec((D, ca)),
        _const_spec(wb.shape),
        _const_spec((D, LANES)),
    ]
    args += [modl, norm_g, conv_w, wa, wb, wg]
    out_specs += [
        pl.BlockSpec((tm, ca), tok),
        pl.BlockSpec((tm, cb), tok),
        pl.BlockSpec((tm, LANES), tok),
    ]
    out_shape += [
        jax.ShapeDtypeStruct((N, ca), BF16),
        jax.ShapeDtypeStruct((N, cb), BF16),
        jax.ShapeDtypeStruct((N, LANES), F32),
    ]
    kern = functools.partial(_inproj_kernel, seq=seq, fuse_combine=combine is not None)
    return pl.pallas_call(
        kern,
        grid=(nt,),
        in_specs=in_specs,
        out_specs=out_specs,
        out_shape=out_shape,
        scratch_shapes=[pltpu.VMEM((8, D), F32)],
        compiler_params=_params(("arbitrary",)),
        name="in_projection",
    )(*args)


def _log_sigmoid(x):
    return jnp.minimum(x, 0.0) - jnp.log(1.0 + jnp.exp(-jnp.abs(x)))


def _gla_kernel(za_ref, zg_ref, gw2_ref, gb_ref, ng_ref, o_ref, st_ref, lg_ref, *, dk, dv, rank):
    heads = GLA_HEADS
    dkh, dvh = dk // heads, dv // heads
    c = GLA_CHUNK
    ts = za_ref.shape[0]

    @pl.when(pl.program_id(1) == 0)
    def _():
        st_ref[...] = jnp.zeros_like(st_ref)

    zg = zg_ref[...]
    zg_hi = zg.astype(BF16)
    zg_lo = (zg - zg_hi.astype(F32)).astype(BF16)
    lane = lax.broadcasted_iota(jnp.int32, zg.shape, 1)
    lhs = jnp.where(jnp.logical_and(lane >= rank, lane < 2 * rank), zg_lo, zg_hi)
    pre = _dot(lhs, gw2_ref[...]) + gb_ref[...]
    lg = _log_sigmoid(pre) * (1.0 / GATE_TAU)
    lg_hi = lg.astype(BF16)
    lg_ref[:, 0:dk] = lg_hi
    lg_ref[:, dk:2 * dk] = (lg - lg_hi.astype(F32)).astype(BF16)

    row = lax.broadcasted_iota(jnp.int32, (c, c), 0)
    col = lax.broadcasted_iota(jnp.int32, (c, c), 1)
    causal = row >= col
    tril = causal.astype(BF16)
    qscale = dkh ** -0.5

    for ci in range(ts // c):
        rows = slice(ci * c, (ci + 1) * c)
        b_two = _dot(tril, lg_ref[rows, :])
        b_all = b_two[:, 0:dk] + b_two[:, dk:2 * dk]
        for hd in range(heads):
            ks = slice(hd * dkh, (hd + 1) * dkh)
            q = za_ref[rows, hd * dkh:(hd + 1) * dkh].astype(F32) * qscale
            k = za_ref[rows, dk + hd * dkh:dk + (hd + 1) * dkh].astype(F32)
            v = za_ref[rows, 2 * dk + hd * dvh:2 * dk + (hd + 1) * dvh]
            r = za_ref[rows, 2 * dk + dv + hd * dvh:2 * dk + dv + (hd + 1) * dvh].astype(F32)
            b = b_all[:, ks]
            b_last = b[c - 1:c, :]
            q_t = (q * jnp.exp(b)).astype(BF16)
            k_t = (k * jnp.exp(-b)).astype(BF16)
            k_s = (k * jnp.exp(b_last - b)).astype(BF16)
            decay = jnp.exp(b_last)
            attn = lax.dot_general(q_t, k_t, (((1,), (1,)), ((), ())), preferred_element_type=F32)
            attn = jnp.where(causal, attn, 0.0).astype(BF16)
            st = st_ref[hd]
            o = _dot(attn, v) + lax.dot_general(
                q_t, st.astype(BF16), (((1,), (1,)), ((), ())), preferred_element_type=F32)
            upd = lax.dot_general(v, k_s, (((0,), (0,)), ((), ())), preferred_element_type=F32)
            st_ref[hd] = st * decay + upd
            ms = jnp.mean(o * o, axis=-1, keepdims=True)
            on = o * lax.rsqrt(ms + NORM_EPS) * ng_ref[:, hd * dvh:(hd + 1) * dvh]
            o_ref[rows, hd * dvh:(hd + 1) * dvh] = (on * (r * jax.nn.sigmoid(r))).astype(BF16)


def _gla(za, zg, gw2, gb, ng, batch, seq, dk, dv, rank):
    N = za.shape[0]
    ts = GLA_TILE
    ns = seq // ts
    heads = GLA_HEADS
    kern = functools.partial(_gla_kernel, dk=dk, dv=dv, rank=rank)
    return pl.pallas_call(
        kern,
        grid=(batch, ns),
        in_specs=[
            pl.BlockSpec((ts, za.shape[1]), lambda b, s: (b * ns + s, 0)),
            pl.BlockSpec((ts, LANES), lambda b, s: (b * ns + s, 0)),
            _const_spec((LANES, dk)),
            _const_spec((1, dk)),
            _const_spec((1, dv)),
        ],
        out_specs=pl.BlockSpec((ts, dv), lambda b, s: (b * ns + s, 0)),
        out_shape=jax.ShapeDtypeStruct((N, dv), BF16),
        scratch_shapes=[
            pltpu.VMEM((heads, dv // heads, dk // heads), F32),
            pltpu.VMEM((ts, 2 * dk), BF16),
        ],
        compiler_params=_params(("arbitrary", "arbitrary")),
        name="gla",
    )(za, zg, gw2, gb, ng)


def _mixout_kernel(og_ref, zb_ref, x_ref, mod_ref, wga_ref, wco_ref, wo_ref, n2_ref,
                   wr_ref, br_ref, x1_ref, h2_ref, logit_ref):
    tm, d = x_ref.shape
    m = mod_ref[0]
    sub = MIX_SUB
    gain2 = n2_ref[...] * (1.0 + m[4:5])
    for r0 in range(0, tm, sub):
        rs = slice(r0, r0 + sub)
        ga = zb_ref[rs, d:2 * d].astype(F32)
        gc = zb_ref[rs, 2 * d:3 * d].astype(F32)
        y_conv = _dot(zb_ref[rs, 0:d], wco_ref[...])
        y_gla = _dot(og_ref[rs, :], wga_ref[...])
        y = jax.nn.sigmoid(ga) * y_gla + jax.nn.sigmoid(gc) * y_conv
        y = _dot(y.astype(BF16), wo_ref[...])
        x1 = x_ref[rs, :] + m[2:3] * y
        x1_ref[rs, :] = x1

        ms = jnp.mean(x1 * x1, axis=-1, keepdims=True)
        h2 = x1 * lax.rsqrt(ms + NORM_EPS) * gain2 + m[3:4]
        h_hi, h_r = _round_bf16(h2)
        for p, words in enumerate(_pack_rows(h_r)):
            h2_ref[p, rs, :] = words

        h_lo = (h2 - h_r).astype(BF16)
        two = _dot(h_hi, wr_ref[...])
        logits = two[:, 0:LANES] + two[:, LANES:2 * LANES] + _dot(h_lo, wr_ref[:, 0:LANES]) + br_ref[...]
        logit_ref[rs, :] = logits


def _mixer_out(og, zb, x, modl, wga, wco, wo, n2g, wr, br, batch, seq):
    N, D = x.shape
    tm = MIX_TILE
    ns = seq // tm
    tok = lambda b, s: (b * ns + s, 0)
    return pl.pallas_call(
        _mixout_kernel,
        grid=(batch, ns),
        in_specs=[
            pl.BlockSpec((tm, D), tok),
            pl.BlockSpec((tm, zb.shape[1]), tok),
            pl.BlockSpec((tm, D), tok),
            pl.BlockSpec((1, 6, D), lambda b, s: (b, 0, 0)),
            _const_spec((D, D)),
            _const_spec((D, D)),
            _const_spec((D, D)),
            _const_spec((1, D)),
            _const_spec((D, 2 * LANES)),
            _const_spec((1, LANES)),
        ],
        out_specs=[
            pl.BlockSpec((tm, D), tok),
            pl.BlockSpec((D // 2 // ROW_PIECE, tm, ROW_PIECE), lambda b, s: (0, b * ns + s, 0)),
            pl.BlockSpec((tm, LANES), tok),
        ],
        out_shape=[
            jax.ShapeDtypeStruct((N, D), F32),
            jax.ShapeDtypeStruct((D // 2 // ROW_PIECE, N, ROW_PIECE), U32),
            jax.ShapeDtypeStruct((N, LANES), F32),
        ],
        compiler_params=_params(("arbitrary", "arbitrary")),
        name="mixer_out",
    )(og, zb, x, modl, wga, wco, wo, n2g, wr, br)


def _route_kernel(logit_ref, routet_ref, route_ref, cnt_ref, run_ref):
    tm = logit_ref.shape[0]
    sub = ROUTE_SUB
    rows8 = EXPERTS_PER_GROUP

    @pl.when(pl.program_id(0) == 0)
    def _():
        run_ref[...] = jnp.zeros_like(run_ref)

    sub8 = lax.broadcasted_iota(jnp.int32, (rows8, sub), 0)
    erow = lax.broadcasted_iota(jnp.int32, (N_EXPERTS, sub), 0)
    tr = lax.broadcasted_iota(jnp.int32, (sub, sub), 0)
    tc = lax.broadcasted_iota(jnp.int32, (sub, sub), 1)
    earlier = (tr < tc).astype(BF16)
    ones = jnp.ones((sub, LANES), BF16)
    neg = -jnp.inf
    run = run_ref[...]
    for r0 in range(0, tm, sub):
        lt = logit_ref[r0:r0 + sub, :].T
        gl = lt[0:N_GROUPS, :]
        gmax = jnp.max(gl, axis=0, keepdims=True)
        gsum = jnp.sum(jnp.exp(gl - gmax), axis=0, keepdims=True)
        g_w = 1.0 / gsum
        g_idx = jnp.min(jnp.where(gl == gmax, sub8, N_GROUPS), axis=0, keepdims=True)
        el = lt[N_GROUPS:N_GROUPS + rows8, :]
        for g in range(1, N_GROUPS):
            el = jnp.where(g_idx == g, lt[N_GROUPS + g * rows8:N_GROUPS + (g + 1) * rows8, :], el)
        e1 = jnp.max(el, axis=0, keepdims=True)
        i1 = jnp.min(jnp.where(el == e1, sub8, rows8), axis=0, keepdims=True)
        el2 = jnp.where(sub8 == i1, neg, el)
        e2 = jnp.max(el2, axis=0, keepdims=True)
        i2 = jnp.min(jnp.where(el2 == e2, sub8, rows8), axis=0, keepdims=True)
        ratio = jnp.exp(e2 - e1)
        w1 = g_w / (1.0 + ratio)
        w2 = g_w * ratio / (1.0 + ratio)
        id1 = g_idx * rows8 + i1
        id2 = g_idx * rows8 + i2

        oh1 = erow == id1
        oh2 = erow == id2
        oh1b = jnp.where(oh1, 1.0, 0.0).astype(BF16)
        oh2b = jnp.where(oh2, 1.0, 0.0).astype(BF16)
        tot1 = _dot(oh1b, ones)
        tot2 = _dot(oh2b, ones)
        base1 = jnp.concatenate([run] * (sub // LANES), axis=1)
        base2 = jnp.concatenate([run + tot1] * (sub // LANES), axis=1)
        c1 = _dot(oh1b, earlier) + base1
        c2 = _dot(oh2b, earlier) + base2
        rank1 = jnp.sum(jnp.where(oh1, c1, 0.0), axis=0, keepdims=True)
        rank2 = jnp.sum(jnp.where(oh2, c2, 0.0), axis=0, keepdims=True)
        run = run + tot1 + tot2

        rec = jnp.where(sub8 == R_ID1, id1.astype(F32), 0.0)
        rec = jnp.where(sub8 == R_ID2, id2.astype(F32), rec)
        rec = jnp.where(sub8 == R_W1, w1, rec)
        rec = jnp.where(sub8 == R_W2, w2, rec)
        rec = jnp.where(sub8 == R_RANK1, rank1, rec)
        rec = jnp.where(sub8 == R_RANK2, rank2, rec)
        routet_ref[:, r0:r0 + sub] = rec
        rec_full = jnp.concatenate([rec, jnp.zeros((LANES - rows8, sub), F32)], axis=0)
        route_ref[r0:r0 + sub, :] = rec_full.T
    run_ref[...] = run
    cnt_ref[...] = run


def _routing(logits):
    N = logits.shape[0]
    tm = 2048 if N % 2048 == 0 else ROUTE_SUB
    return pl.pallas_call(
        _route_kernel,
        grid=(N // tm,),
        in_specs=[pl.BlockSpec((tm, LANES), lambda i: (i, 0))],
        out_specs=[
            pl.BlockSpec((8, tm), lambda i: (0, i)),
            pl.BlockSpec((tm, LANES), lambda i: (i, 0)),
            pl.BlockSpec((N_EXPERTS, LANES), lambda i: (0, 0)),
        ],
        out_shape=[
            jax.ShapeDtypeStruct((8, N), F32),
            jax.ShapeDtypeStruct((N, LANES), F32),
            jax.ShapeDtypeStruct((N_EXPERTS, LANES), F32),
        ],
        scratch_shapes=[pltpu.VMEM((N_EXPERTS, LANES), F32)],
        compiler_params=_params(("arbitrary",)),
        name="moe_route",
    )(logits)


SC_WINDOW = 128


def _sc_mesh():
    return plsc.VectorSubcoreMesh(core_axis_name="core", subcore_axis_name="subcore")


def _piece_index(rows, pieces, n_rows):
    return (jnp.arange(pieces, dtype=jnp.int32)[:, None] * n_rows + rows[None, :]).reshape(1, -1)


def _sc_dispatch(h2, dests, n_rows):
    pieces, N, pc = h2.shape
    w = SC_WINDOW

    @functools.partial(pl.kernel, out_type=jax.ShapeDtypeStruct((pieces * n_rows, pc), h2.dtype),
                       mesh=_sc_mesh(), scratch_types=[], name="moe_dispatch_sc")
    def run(x_hbm, *refs):
        o_hbm = refs[-1]

        def body(x_vmem, *idx_vmem):
            for i_vmem in idx_vmem:
                pltpu.sync_copy(x_vmem, o_hbm.at[i_vmem.at[0]])

        pltpu.emit_pipeline(
            body,
            grid=(pieces * N // w,),
            in_specs=[pl.BlockSpec((w, pc), lambda i: (i, 0))]
            + [pl.BlockSpec((1, w), lambda i: (0, i))] * len(dests),
            out_specs=[],
            core_axis_name=("core", "subcore"),
            dimension_semantics=(pltpu.PARALLEL,),
        )(x_hbm, *refs[:-1])

    xs = run(h2.reshape(pieces * N, pc), *[_piece_index(d, pieces, n_rows) for d in dests])
    return xs.reshape(pieces, n_rows, pc)


def _sc_return(yb, dest):
    M = dest.shape[0]
    pieces, P, pc = yb.shape
    w = SC_WINDOW

    @functools.partial(pl.kernel, out_type=jax.ShapeDtypeStruct((pieces * M, pc), yb.dtype),
                       mesh=_sc_mesh(), scratch_types=[], name="moe_return_sc")
    def run(y_hbm, i_hbm, o_hbm):
        def body(i_vmem, o_vmem):
            pltpu.sync_copy(y_hbm.at[i_vmem.at[0]], o_vmem)

        pltpu.emit_pipeline(
            body,
            grid=(pieces * M // w,),
            in_specs=[pl.BlockSpec((1, w), lambda i: (0, i))],
            out_specs=[pl.BlockSpec((w, pc), lambda i: (i, 0))],
            core_axis_name=("core", "subcore"),
            dimension_semantics=(pltpu.PARALLEL,),
        )(i_hbm, o_hbm)

    return run(yb.reshape(pieces * P, pc), _piece_index(dest, pieces, P)).reshape(pieces, M, pc)


def _expert_kernel(ib_ref, ie_ref, nxt_ref, slot_ref, nb_ref, xs_ref, w1_ref, w3_ref, w2_ref, y_ref,
                   wf1, wf3, wf2, w1b, w3b, w2b, sem, *, layer):
    j = pl.program_id(0)
    jp = jnp.maximum(j - 1, 0)
    live = j < nb_ref[0]
    new_expert = jnp.logical_or(j == 0, ie_ref[j] != ie_ref[jp])
    streams = ((w1_ref, wf1), (w3_ref, wf3), (w2_ref, wf2))

    def weight_copies(expert, s):
        return [pltpu.make_async_copy(w.at[layer, expert], buf.at[s], sem.at[s, k])
                for k, (w, buf) in enumerate(streams)]

    @pl.when(j == 0)
    def _():
        for cp in weight_copies(ie_ref[0], 0):
            cp.start()

    @pl.when(jnp.logical_and(new_expert, live))
    def _():
        s = slot_ref[j]
        for cp in weight_copies(ie_ref[j], s):
            cp.wait()
        w1b[...] = wf1[s].astype(BF16)
        w3b[...] = wf3[s].astype(BF16)
        w2b[...] = wf2[s].astype(BF16)

        @pl.when(nxt_ref[j] >= 0)
        def _():
            for cp in weight_copies(nxt_ref[j], 1 - s):
                cp.start()

    @pl.when(live)
    def _():
        xb = _unpack_rows(xs_ref).astype(BF16)
        h1 = _dot(xb, w1b[...])
        h3 = _dot(xb, w3b[...])
        a = (h1 * jax.nn.sigmoid(h1) * h3).astype(BF16)
        for p, words in enumerate(_pack_rows(_round_bf16(_dot(a, w2b[...]))[1])):
            y_ref[p] = words


def _experts(xs, items, w1, w3, w2, layer):
    pieces, P, pc = xs.shape
    _, E, D, DE = w1.shape
    bm = MOE_BLOCK
    n_items_max = P // bm
    rows = lambda j, *prefetch: (0, prefetch[0][j], 0)
    return pl.pallas_call(
        functools.partial(_expert_kernel, layer=layer),
        grid_spec=pltpu.PrefetchScalarGridSpec(
            num_scalar_prefetch=5,
            grid=(n_items_max,),
            in_specs=[
                pl.BlockSpec((pieces, bm, pc), rows),
                pl.BlockSpec(memory_space=pl.ANY),
                pl.BlockSpec(memory_space=pl.ANY),
                pl.BlockSpec(memory_space=pl.ANY),
            ],
            out_specs=pl.BlockSpec((pieces, bm, pc), rows),
            scratch_shapes=[
                pltpu.VMEM((2, D, DE), F32),
                pltpu.VMEM((2, D, DE), F32),
                pltpu.VMEM((2, DE, D), F32),
                pltpu.VMEM((D, DE), BF16),
                pltpu.VMEM((D, DE), BF16),
                pltpu.VMEM((DE, D), BF16),
                pltpu.SemaphoreType.DMA((2, 3)),
            ],
        ),
        out_shape=jax.ShapeDtypeStruct((pieces, P, pc), U32),
        compiler_params=_params(("arbitrary",)),
        name="moe_experts",
    )(*items, xs, w1, w3, w2)


def _combine_kernel(x1_ref, route_ref, mod_ref, fg_ref, y0_ref, y1_ref, *rest):
    o_ref = rest[-1]
    x2 = x1_ref[...] + mod_ref[0][5:6] * _moe_mix(route_ref, y0_ref, y1_ref)
    ms = jnp.mean(x2 * x2, axis=-1, keepdims=True)
    o_ref[...] = x2 * lax.rsqrt(ms + NORM_EPS) * fg_ref[...]


def _final_combine(x1, route, modl, final_g, yk, seq, first_tile, prev_out):
    N, D = x1.shape
    tm = ROW_TILE
    nt = yk.shape[1] // 2 // tm
    tok = lambda i: (first_tile + i, 0)
    in_specs = [
        pl.BlockSpec((tm, D), tok),
        pl.BlockSpec((tm, LANES), tok),
        pl.BlockSpec((1, 6, D), lambda i: (((first_tile + i) * tm) // seq, 0, 0)),
        pl.BlockSpec((1, D), lambda i: (0, 0)),
        pl.BlockSpec((yk.shape[0], tm, ROW_PIECE), lambda i: (0, i, 0)),
        pl.BlockSpec((yk.shape[0], tm, ROW_PIECE), lambda i: (0, nt + i, 0)),
    ]
    args = [x1, route, modl, final_g, yk, yk]
    aliases = {}
    if prev_out is not None:
        in_specs.append(pl.BlockSpec(memory_space=pl.ANY))
        args.append(prev_out)
        aliases = {len(args) - 1: 0}
    return pl.pallas_call(
        _combine_kernel,
        grid=(nt,),
        in_specs=in_specs,
        out_specs=pl.BlockSpec((tm, D), tok),
        out_shape=jax.ShapeDtypeStruct((N, D), F32),
        input_output_aliases=aliases,
        compiler_params=_params(("arbitrary",)),
        name="moe_combine",
    )(*args)


ITEM_LANES = 256
I_BLOCK, I_EXPERT, I_NEXT, I_SLOT, I_COUNT = 0, 1, 2, 3, 4


def _dest_kernel(routet_ref, cnt_ref, d1_ref, d2_ref, d3_ref, items_ref):
    tm = routet_ref.shape[1]
    ne = N_EXPERTS
    bm = float(MOE_BLOCK)
    cnt = cnt_ref[...]
    r = lax.broadcasted_iota(jnp.int32, (ne, ne), 0)
    c = lax.broadcasted_iota(jnp.int32, (ne, ne), 1)
    lower = (r >= c).astype(F32)
    cumsum = lambda a: jnp.dot(lower, a, precision=HIGHEST, preferred_element_type=F32)
    padded = jnp.floor((cnt + (bm - 1.0)) * (1.0 / bm)) * bm
    pend = cumsum(padded)
    pstart = pend - padded
    npad = padded - cnt
    pad_end = cumsum(npad)
    pad_start = pad_end - npad

    tile = lambda a, n: jnp.concatenate([a] * (n // LANES), axis=1)
    expert = lax.broadcasted_iota(jnp.int32, (ne, tm), 0).astype(F32)
    rec = routet_ref[...]
    pstart_t = tile(pstart, tm)

    def sorted_row(row_id, row_rank):
        sel = expert == rec[row_id:row_id + 1, :]
        return jnp.sum(jnp.where(sel, pstart_t, 0.0), axis=0, keepdims=True) + rec[row_rank:row_rank + 1, :]

    dest1 = sorted_row(R_ID1, R_RANK1)
    dest2 = sorted_row(R_ID2, R_RANK2)
    t = (lax.broadcasted_iota(jnp.int32, (1, tm), 1) + pl.program_id(0) * tm).astype(F32)
    ps_t, pe_t = tile(pad_start, tm), tile(pad_end, tm)
    in_e = jnp.logical_and(ps_t <= t, t < pe_t)
    pad_row = jnp.sum(jnp.where(in_e, tile(pstart + cnt, tm) + (t - ps_t), 0.0), axis=0, keepdims=True)
    dest3 = jnp.where(t < pe_t[ne - 1:ne, :], pad_row, dest1)
    d1_ref[0] = dest1.astype(jnp.int32)
    d2_ref[0] = dest2.astype(jnp.int32)
    d3_ref[0] = dest3.astype(jnp.int32)

    @pl.when(pl.program_id(0) == 0)
    def _():
        nl = ITEM_LANES
        pend_i = tile(pend, nl)
        n_items = pend_i[ne - 1:ne, :] * (1.0 / bm)
        blk = jnp.minimum(lax.broadcasted_iota(jnp.int32, (1, nl), 1).astype(F32), n_items - 1.0)
        erow = lax.broadcasted_iota(jnp.int32, (ne, nl), 0).astype(F32)
        ie = jnp.sum(jnp.where(pend_i <= blk * bm, 1.0, 0.0), axis=0, keepdims=True)
        nonempty = tile(cnt, nl) > 0.0
        order = jnp.sum(jnp.where(jnp.logical_and(nonempty, erow < ie), 1.0, 0.0), axis=0, keepdims=True)
        nxt = jnp.min(jnp.where(jnp.logical_and(nonempty, erow > ie), erow, float(ne)), axis=0, keepdims=True)
        nxt = jnp.where(nxt == float(ne), -1.0, nxt)
        slot = order - 2.0 * jnp.floor(order * 0.5)
        row = lax.broadcasted_iota(jnp.int32, (8, nl), 0)
        tab = jnp.where(row == I_BLOCK, blk, 0.0)
        tab = jnp.where(row == I_EXPERT, ie, tab)
        tab = jnp.where(row == I_NEXT, nxt, tab)
        tab = jnp.where(row == I_SLOT, slot, tab)
        tab = jnp.where(row == I_COUNT, n_items, tab)
        items_ref[...] = tab.astype(jnp.int32)


def _destinations(route_t, cnt, n_blocks):
    N = route_t.shape[1]
    tm = 2048 if N % 2048 == 0 else ROUTE_SUB
    out = jax.ShapeDtypeStruct((N // tm, 1, tm), jnp.int32)
    d1, d2, d3, items = pl.pallas_call(
        _dest_kernel,
        grid=(N // tm,),
        in_specs=[pl.BlockSpec((8, tm), lambda i: (0, i)),
                  pl.BlockSpec((N_EXPERTS, LANES), lambda i: (0, 0))],
        out_specs=[pl.BlockSpec((1, 1, tm), lambda i: (i, 0, 0))] * 3
        + [pl.BlockSpec((8, ITEM_LANES), lambda i: (0, 0))],
        out_shape=[out, out, out, jax.ShapeDtypeStruct((8, ITEM_LANES), jnp.int32)],
        compiler_params=_params(("arbitrary",)),
        name="moe_dest",
    )(route_t, cnt)
    table = tuple(items[k, :n_blocks] for k in (I_BLOCK, I_EXPERT, I_NEXT, I_SLOT)) + (items[I_COUNT, :1],)
    return d1.reshape(N), d2.reshape(N), d3.reshape(N), table


def kernel(x, c, mod_w, mod_b, norm1_g, w_in, gate_w2, gate_b, gla_norm_g, conv_w, w_gla_out,
           w_conv_out, w_out, norm2_g, router_group_w, router_group_b, router_expert_w,
           router_expert_b, expert_w1, expert_w3, expert_w2, final_norm_g):
    B, S, D = x.shape
    L = mod_w.shape[0]
    N = B * S
    dk = gate_w2.shape[2]
    rank = gate_w2.shape[1]
    dv = gla_norm_g.shape[1]
    n_slots = N * 2
    n_blocks = n_slots // MOE_BLOCK + N_EXPERTS
    assert S % GLA_TILE == 0 and S % MIX_TILE == 0 and S % IN_TILE == 0
    assert N % (FINAL_PARTS * ROW_TILE) == 0
    assert n_slots % MOE_BLOCK == 0 and MOE_BLOCK & (MOE_BLOCK - 1) == 0 and n_blocks <= ITEM_LANES
    assert N_EXPERTS * (MOE_BLOCK - 1) <= N
    assert N_GROUPS + N_EXPERTS <= LANES and EXPERTS_PER_GROUP == 8 and 3 * rank <= LANES and D % (2 * ROW_PIECE) == 0

    mod = [_modulation(c, mod_w, mod_b, l)[0] for l in range(L)]
    xf = x.reshape(N, D)
    o_gd = 2 * dk + dv
    o_r = o_gd + rank
    o_b = o_r + dv
    for l in range(L):
        modl = mod[l].reshape(B, 6, D)
        wl = w_in[l]
        wa = jnp.concatenate([wl[:, :o_gd], wl[:, o_r:o_b]], axis=1).astype(BF16)
        wb = wl[:, o_b:].astype(BF16)
        wgd = wl[:, o_gd:o_r]
        wg = jnp.pad(jnp.concatenate([wgd, wgd, wgd], axis=1), ((0, 0), (0, LANES - 3 * rank))).astype(BF16)
        g_hi = gate_w2[l].astype(BF16)
        g_lo = (gate_w2[l] - g_hi.astype(F32)).astype(BF16)
        gw2 = jnp.pad(jnp.concatenate([g_hi, g_hi, g_lo], axis=0), ((0, LANES - 3 * rank), (0, 0)))
        if l == 0:
            za, zb, zg = _in_projection(xf, modl, norm1_g[l].reshape(1, D), conv_w[l], wa, wb, wg, S)
        else:
            xf, za, zb, zg = _in_projection(x1, modl, norm1_g[l].reshape(1, D), conv_w[l], wa, wb, wg, S,
                                            combine=(route, yk, mod[l - 1].reshape(B, 6, D)))
        og = _gla(za, zg, gw2, gate_b[l].reshape(1, dk), gla_norm_g[l].reshape(1, dv), B, S, dk, dv, rank)
        wr = jnp.pad(jnp.concatenate([router_group_w[l], router_expert_w[l]], axis=1),
                     ((0, 0), (0, LANES - N_GROUPS - N_EXPERTS)))
        wr_hi = wr.astype(BF16)
        wr = jnp.concatenate([wr_hi, (wr - wr_hi.astype(F32)).astype(BF16)], axis=1)
        br = jnp.pad(jnp.concatenate([router_group_b[l], router_expert_b[l]]),
                     (0, LANES - N_GROUPS - N_EXPERTS)).reshape(1, LANES)
        x1, h2, logits = _mixer_out(
            og, zb, xf, modl, w_gla_out[l].astype(BF16), w_conv_out[l].astype(BF16),
            w_out[l].astype(BF16), norm2_g[l].reshape(1, D), wr, br, B, S)
        route_t, route, cnt = _routing(logits)
        dest1, dest2, dest3, items = _destinations(route_t, cnt, n_blocks)
        xs = _sc_dispatch(h2, (dest1, dest2, dest3), n_blocks * MOE_BLOCK)
        yb = _experts(xs, items, expert_w1, expert_w3, expert_w2, l)
        if l < L - 1:
            yk = _sc_return(yb, jnp.concatenate([dest1, dest2]))
    out = None
    part = N // FINAL_PARTS
    for p in range(FINAL_PARTS):
        tok = slice(p * part, (p + 1) * part)
        yk = _sc_return(yb, jnp.concatenate([dest1[tok], dest2[tok]]))
        out = _final_combine(x1, route, modl, final_norm_g.reshape(1, D), yk, S,
                             p * part // ROW_TILE, out)
    return out.reshape(B, S, D)
```

```python
import functools

import jax
import jax.numpy as jnp
from jax import lax
from jax.experimental import pallas as pl
from jax.experimental.pallas import tpu as pltpu
from jax.experimental.pallas import tpu_sc as plsc

F32 = jnp.float32
BF16 = jnp.bfloat16
HIGHEST = lax.Precision.HIGHEST

GLA_HEADS = 4
GATE_TAU = 16.0
GLA_CHUNK = 64
N_GROUPS = 8
EXPERTS_PER_GROUP = 8
N_EXPERTS = N_GROUPS * EXPERTS_PER_GROUP
NORM_EPS = 1e-6

LANES = 128
VMEM_LIMIT_BYTES = 56 * 1024 * 1024

IN_TILE = 512
IN_TILE_PLAIN = 1024
GLA_TILE = 512
MIX_TILE = 1024
MIX_SUB = 1024
ROUTE_SUB = 256
MOE_BLOCK = 512
ROW_TILE = 1024
FINAL_PARTS = 4
ROW_PIECE = 256
U32 = jnp.uint32

R_ID1, R_ID2, R_W1, R_W2, R_RANK1, R_RANK2 = 0, 1, 2, 3, 4, 5


def _dot(a, b):
    return jnp.dot(a, b, preferred_element_type=F32)


def _round_bf16(x):
    xb = x.astype(BF16)
    return xb, xb.astype(F32)


def _pack_rows(xr):
    half = xr.shape[1] // 2
    out = []
    for p in range(half // ROW_PIECE):
        lo = xr[:, p * ROW_PIECE:(p + 1) * ROW_PIECE]
        hi = xr[:, half + p * ROW_PIECE:half + (p + 1) * ROW_PIECE]
        out.append((pltpu.bitcast(lo, U32) >> 16) | pltpu.bitcast(hi, U32))
    return out


def _unpack_rows(ref):
    words = [ref[p] for p in range(ref.shape[0])]
    lo = [pltpu.bitcast(w << 16, F32) for w in words]
    hi = [pltpu.bitcast(w & jnp.uint32(0xFFFF0000), F32) for w in words]
    return jnp.concatenate(lo + hi, axis=-1)


def _const_spec(shape):
    nd = len(shape)
    return pl.BlockSpec(shape, lambda *_: (0,) * nd, pipeline_mode=pl.Buffered(1))


def _params(sem):
    return pltpu.CompilerParams(dimension_semantics=sem, vmem_limit_bytes=VMEM_LIMIT_BYTES)


MOD_ROWS = 256


def _mod_kernel(c_ref, w_ref, b_ref, o_ref):
    c = c_ref[...]
    sc = c * jax.nn.sigmoid(c)
    s_hi, s_r = _round_bf16(sc)
    w = w_ref[0]
    w_hi, w_r = _round_bf16(w)
    s_lo = (sc - s_r).astype(BF16)
    part = _dot(s_hi, w_hi) + _dot(s_lo, w_hi) + _dot(s_hi, (w - w_r).astype(BF16))

    @pl.when(pl.program_id(1) == 0)
    def _():
        o_ref[0] = part + b_ref[0]

    @pl.when(pl.program_id(1) > 0)
    def _():
        o_ref[0] += part


def _modulation(c, mod_w, mod_b, layer):
    L, D, D6 = mod_w.shape
    B = c.shape[0]
    kb = MOD_ROWS
    return pl.pallas_call(
        _mod_kernel,
        grid=(1, D // kb),
        in_specs=[
            pl.BlockSpec((B, kb), lambda l, k: (0, k)),
            pl.BlockSpec((1, kb, D6), lambda l, k: (layer, k, 0)),
            pl.BlockSpec((1, 1, D6), lambda l, k: (layer, 0, 0)),
        ],
        out_specs=pl.BlockSpec((1, B, D6), lambda l, k: (0, 0, 0)),
        out_shape=jax.ShapeDtypeStruct((1, B, D6), F32),
        compiler_params=_params(("arbitrary", "arbitrary")),
        name="adaln_mod",
    )(c, mod_w, mod_b.reshape(L, 1, D6))


CONV_COLS = 256


def _moe_mix(route_ref, y0_ref, y1_ref):
    rec = route_ref[...]
    w1 = rec[:, R_W1:R_W1 + 1]
    w2 = rec[:, R_W2:R_W2 + 1]
    return w1 * _unpack_rows(y0_ref) + w2 * _unpack_rows(y1_ref)


def _inproj_kernel(*refs, seq, fuse_combine):
    if fuse_combine:
        (x_ref, route_ref, y0_ref, y1_ref, modp_ref, mod_ref, g_ref, cw_ref, wa_ref, wb_ref, wg_ref,
         xo_ref, za_ref, zb_ref, zg_ref, carry_ref) = refs
        x = x_ref[...] + modp_ref[0][5:6] * _moe_mix(route_ref, y0_ref, y1_ref)
        xo_ref[...] = x
    else:
        (x_ref, mod_ref, g_ref, cw_ref, wa_ref, wb_ref, wg_ref, za_ref, zb_ref, zg_ref, carry_ref) = refs
        x = x_ref[...]
    tm, d = x.shape

    @pl.when((pl.program_id(0) * tm) % seq == 0)
    def _():
        carry_ref[...] = jnp.zeros_like(carry_ref)

    ms = jnp.mean(x * x, axis=-1, keepdims=True)
    m = mod_ref[0]
    h = (x * lax.rsqrt(ms + NORM_EPS) * (g_ref[...] * (1.0 + m[1:2])) + m[0:1]).astype(BF16)
    for j in range(za_ref.shape[1] // d):
        za_ref[:, j * d:(j + 1) * d] = _dot(h, wa_ref[:, j * d:(j + 1) * d]).astype(BF16)
    zg_ref[...] = _dot(h, wg_ref[...])
    for j in range(1, zb_ref.shape[1] // d):
        zb_ref[:, j * d:(j + 1) * d] = _dot(h, wb_ref[:, (j + 2) * d:(j + 3) * d]).astype(BF16)

    w = CONV_COLS
    rowi = lax.broadcasted_iota(jnp.int32, (tm, w), 0)
    for j in range(d // w):
        cols = slice(j * w, (j + 1) * w)
        cb = _dot(h, wb_ref[:, j * w:(j + 1) * w])
        cc = _dot(h, wb_ref[:, d + j * w:d + (j + 1) * w])
        ch = _dot(h, wb_ref[:, 2 * d + j * w:2 * d + (j + 1) * w])
        u = cc * ch
        prev = carry_ref[:, cols]
        u1 = jnp.where(rowi == 0, prev[7:8], pltpu.roll(u, 1, 0))
        u2 = jnp.where(rowi == 0, prev[6:7], jnp.where(rowi == 1, prev[7:8], pltpu.roll(u, 2, 0)))
        carry_ref[:, cols] = u[tm - 8:tm]
        conv = cw_ref[0:1, cols] * u2 + cw_ref[1:2, cols] * u1 + cw_ref[2:3, cols] * u
        zb_ref[:, cols] = (cb * conv).astype(BF16)


def _in_projection(x, modl, norm_g, conv_w, wa, wb, wg, seq, combine=None):
    N, D = x.shape
    tm = IN_TILE_PLAIN if combine is None else IN_TILE
    nt = N // tm
    ca, cb = wa.shape[1], wb.shape[1] - 2 * D
    tok = lambda i: (i, 0)
    per_batch = lambda i: ((i * tm) // seq, 0, 0)
    in_specs = [pl.BlockSpec((tm, D), tok)]
    args = [x]
    out_specs, out_shape = [], []
    if combine is not None:
        route, yk, mod_prev = combine
        piece_blk = (yk.shape[0], tm, ROW_PIECE)
        in_specs += [
            pl.BlockSpec((tm, LANES), tok),
            pl.BlockSpec(piece_blk, lambda i: (0, i, 0)),
            pl.BlockSpec(piece_blk, lambda i: (0, nt + i, 0)),
            pl.BlockSpec((1, 6, D), per_batch),
        ]
        args += [route, yk, yk, mod_prev]
        out_specs.append(pl.BlockSpec((tm, D), tok))
        out_shape.append(jax.ShapeDtypeStruct((N, D), F32))
    in_specs += [
        pl.BlockSpec((1, 6, D), per_batch),
        _const_spec((1, D)),
        _const_spec(conv_w.shape),
        _const_spec((D, ca)),
        _const_spec(wb.shape),
        _const_spec((D, LANES)),
    ]
    args += [modl, norm_g, conv_w, wa, wb, wg]
    out_specs += [
        pl.BlockSpec((tm, ca), tok),
        pl.BlockSpec((tm, cb), tok),
        pl.BlockSpec((tm, LANES), tok),
    ]
    out_shape += [
        jax.ShapeDtypeStruct((N, ca), BF16),
        jax.ShapeDtypeStruct((N, cb), BF16),
        jax.ShapeDtypeStruct((N, LANES), F32),
    ]
    kern = functools.partial(_inproj_kernel, seq=seq, fuse_combine=combine is not None)
    return pl.pallas_call(
        kern,
        grid=(nt,),
        in_specs=in_specs,
        out_specs=out_specs,
        out_shape=out_shape,
        scratch_shapes=[pltpu.VMEM((8, D), F32)],
        compiler_params=_params(("arbitrary",)),
        name="in_projection",
    )(*args)


def _log_sigmoid(x):
    return jnp.minimum(x, 0.0) - jnp.log(1.0 + jnp.exp(-jnp.abs(x)))


def _gla_kernel(za_ref, zg_ref, gw2_ref, gb_ref, ng_ref, o_ref, st_ref, lg_ref, *, dk, dv, rank):
    heads = GLA_HEADS
    dkh, dvh = dk // heads, dv // heads
    c = GLA_CHUNK
    ts = za_ref.shape[0]

    @pl.when(pl.program_id(1) == 0)
    def _():
        st_ref[...] = jnp.zeros_like(st_ref)

    zg = zg_ref[...]
    zg_hi = zg.astype(BF16)
    zg_lo = (zg - zg_hi.astype(F32)).astype(BF16)
    lane = lax.broadcasted_iota(jnp.int32, zg.shape, 1)
    lhs = jnp.where(jnp.logical_and(lane >= rank, lane < 2 * rank), zg_lo, zg_hi)
    pre = _dot(lhs, gw2_ref[...]) + gb_ref[...]
    lg = _log_sigmoid(pre) * (1.0 / GATE_TAU)
    lg_hi = lg.astype(BF16)
    lg_ref[:, 0:dk] = lg_hi
    lg_ref[:, dk:2 * dk] = (lg - lg_hi.astype(F32)).astype(BF16)

    row = lax.broadcasted_iota(jnp.int32, (c, c), 0)
    col = lax.broadcasted_iota(jnp.int32, (c, c), 1)
    causal = row >= col
    tril = causal.astype(BF16)
    qscale = dkh ** -0.5

    for ci in range(ts // c):
        rows = slice(ci * c, (ci + 1) * c)
        b_two = _dot(tril, lg_ref[rows, :])
        b_all = b_two[:, 0:dk] + b_two[:, dk:2 * dk]
        for hd in range(heads):
            ks = slice(hd * dkh, (hd + 1) * dkh)
            q = za_ref[rows, hd * dkh:(hd + 1) * dkh].astype(F32) * qscale
            k = za_ref[rows, dk + hd * dkh:dk + (hd + 1) * dkh].astype(F32)
            v = za_ref[rows, 2 * dk + hd * dvh:2 * dk + (hd + 1) * dvh]
            r = za_ref[rows, 2 * dk + dv + hd * dvh:2 * dk + dv + (hd + 1) * dvh].astype(F32)
            b = b_all[:, ks]
            b_last = b[c - 1:c, :]
            q_t = (q * jnp.exp(b)).astype(BF16)
            k_t = (k * jnp.exp(-b)).astype(BF16)
            k_s = (k * jnp.exp(b_last - b)).astype(BF16)
            decay = jnp.exp(b_last)
            attn = lax.dot_general(q_t, k_t, (((1,), (1,)), ((), ())), preferred_element_type=F32)
            attn = jnp.where(causal, attn, 0.0).astype(BF16)
            st = st_ref[hd]
            o = _dot(attn, v) + lax.dot_general(
                q_t, st.astype(BF16), (((1,), (1,)), ((), ())), preferred_element_type=F32)
            upd = lax.dot_general(v, k_s, (((0,), (0,)), ((), ())), preferred_element_type=F32)
            st_ref[hd] = st * decay + upd
            ms = jnp.mean(o * o, axis=-1, keepdims=True)
            on = o * lax.rsqrt(ms + NORM_EPS) * ng_ref[:, hd * dvh:(hd + 1) * dvh]
            o_ref[rows, hd * dvh:(hd + 1) * dvh] = (on * (r * jax.nn.sigmoid(r))).astype(BF16)


def _gla(za, zg, gw2, gb, ng, batch, seq, dk, dv, rank):
    N = za.shape[0]
    ts = GLA_TILE
    ns = seq // ts
    heads = GLA_HEADS
    kern = functools.partial(_gla_kernel, dk=dk, dv=dv, rank=rank)
    return pl.pallas_call(
        kern,
        grid=(batch, ns),
        in_specs=[
            pl.BlockSpec((ts, za.shape[1]), lambda b, s: (b * ns + s, 0)),
            pl.BlockSpec((ts, LANES), lambda b, s: (b * ns + s, 0)),
            _const_spec((LANES, dk)),
            _const_spec((1, dk)),
            _const_spec((1, dv)),
        ],
        out_specs=pl.BlockSpec((ts, dv), lambda b, s: (b * ns + s, 0)),
        out_shape=jax.ShapeDtypeStruct((N, dv), BF16),
        scratch_shapes=[
            pltpu.VMEM((heads, dv // heads, dk // heads), F32),
            pltpu.VMEM((ts, 2 * dk), BF16),
        ],
        compiler_params=_params(("arbitrary", "arbitrary")),
        name="gla",
    )(za, zg, gw2, gb, ng)


def _mixout_kernel(og_ref, zb_ref, x_ref, mod_ref, wga_ref, wco_ref, wo_ref, n2_ref,
                   wr_ref, br_ref, x1_ref, h2_ref, logit_ref):
    tm, d = x_ref.shape
    m = mod_ref[0]
    sub = MIX_SUB
    gain2 = n2_ref[...] * (1.0 + m[4:5])
    for r0 in range(0, tm, sub):
        rs = slice(r0, r0 + sub)
        ga = zb_ref[rs, d:2 * d].astype(F32)
        gc = zb_ref[rs, 2 * d:3 * d].astype(F32)
        y_conv = _dot(zb_ref[rs, 0:d], wco_ref[...])
        y_gla = _dot(og_ref[rs, :], wga_ref[...])
        y = jax.nn.sigmoid(ga) * y_gla + jax.nn.sigmoid(gc) * y_conv
        y = _dot(y.astype(BF16), wo_ref[...])
        x1 = x_ref[rs, :] + m[2:3] * y
        x1_ref[rs, :] = x1

        ms = jnp.mean(x1 * x1, axis=-1, keepdims=True)
        h2 = x1 * lax.rsqrt(ms + NORM_EPS) * gain2 + m[3:4]
        h_hi, h_r = _round_bf16(h2)
        for p, words in enumerate(_pack_rows(h_r)):
            h2_ref[p, rs, :] = words

        h_lo = (h2 - h_r).astype(BF16)
        two = _dot(h_hi, wr_ref[...])
        logits = two[:, 0:LANES] + two[:, LANES:2 * LANES] + _dot(h_lo, wr_ref[:, 0:LANES]) + br_ref[...]
        logit_ref[rs, :] = logits


def _mixer_out(og, zb, x, modl, wga, wco, wo, n2g, wr, br, batch, seq):
    N, D = x.shape
    tm = MIX_TILE
    ns = seq // tm
    tok = lambda b, s: (b * ns + s, 0)
    return pl.pallas_call(
        _mixout_kernel,
        grid=(batch, ns),
        in_specs=[
            pl.BlockSpec((tm, D), tok),
            pl.BlockSpec((tm, zb.shape[1]), tok),
            pl.BlockSpec((tm, D), tok),
            pl.BlockSpec((1, 6, D), lambda b, s: (b, 0, 0)),
            _const_spec((D, D)),
            _const_spec((D, D)),
            _const_spec((D, D)),
            _const_spec((1, D)),
            _const_spec((D, 2 * LANES)),
            _const_spec((1, LANES)),
        ],
        out_specs=[
            pl.BlockSpec((tm, D), tok),
            pl.BlockSpec((D // 2 // ROW_PIECE, tm, ROW_PIECE), lambda b, s: (0, b * ns + s, 0)),
            pl.BlockSpec((tm, LANES), tok),
        ],
        out_shape=[
            jax.ShapeDtypeStruct((N, D), F32),
            jax.ShapeDtypeStruct((D // 2 // ROW_PIECE, N, ROW_PIECE), U32),
            jax.ShapeDtypeStruct((N, LANES), F32),
        ],
        compiler_params=_params(("arbitrary", "arbitrary")),
        name="mixer_out",
    )(og, zb, x, modl, wga, wco, wo, n2g, wr, br)


def _route_kernel(logit_ref, routet_ref, route_ref, cnt_ref, run_ref):
    tm = logit_ref.shape[0]
    sub = ROUTE_SUB
    rows8 = EXPERTS_PER_GROUP

    @pl.when(pl.program_id(0) == 0)
    def _():
        run_ref[...] = jnp.zeros_like(run_ref)

    sub8 = lax.broadcasted_iota(jnp.int32, (rows8, sub), 0)
    erow = lax.broadcasted_iota(jnp.int32, (N_EXPERTS, sub), 0)
    tr = lax.broadcasted_iota(jnp.int32, (sub, sub), 0)
    tc = lax.broadcasted_iota(jnp.int32, (sub, sub), 1)
    earlier = (tr < tc).astype(BF16)
    ones = jnp.ones((sub, LANES), BF16)
    neg = -jnp.inf
    run = run_ref[...]
    for r0 in range(0, tm, sub):
        lt = logit_ref[r0:r0 + sub, :].T
        gl = lt[0:N_GROUPS, :]
        gmax = jnp.max(gl, axis=0, keepdims=True)
        gsum = jnp.sum(jnp.exp(gl - gmax), axis=0, keepdims=True)
        g_w = 1.0 / gsum
        g_idx = jnp.min(jnp.where(gl == gmax, sub8, N_GROUPS), axis=0, keepdims=True)
        el = lt[N_GROUPS:N_GROUPS + rows8, :]
        for g in range(1, N_GROUPS):
            el = jnp.where(g_idx == g, lt[N_GROUPS + g * rows8:N_GROUPS + (g + 1) * rows8, :], el)
        e1 = jnp.max(el, axis=0, keepdims=True)
        i1 = jnp.min(jnp.where(el == e1, sub8, rows8), axis=0, keepdims=True)
        el2 = jnp.where(sub8 == i1, neg, el)
        e2 = jnp.max(el2, axis=0, keepdims=True)
        i2 = jnp.min(jnp.where(el2 == e2, sub8, rows8), axis=0, keepdims=True)
        ratio = jnp.exp(e2 - e1)
        w1 = g_w / (1.0 + ratio)
        w2 = g_w * ratio / (1.0 + ratio)
        id1 = g_idx * rows8 + i1
        id2 = g_idx * rows8 + i2

        oh1 = erow == id1
        oh2 = erow == id2
        oh1b = jnp.where(oh1, 1.0, 0.0).astype(BF16)
        oh2b = jnp.where(oh2, 1.0, 0.0).astype(BF16)
        tot1 = _dot(oh1b, ones)
        tot2 = _dot(oh2b, ones)
        base1 = jnp.concatenate([run] * (sub // LANES), axis=1)
        base2 = jnp.concatenate([run + tot1] * (sub // LANES), axis=1)
        c1 = _dot(oh1b, earlier) + base1
        c2 = _dot(oh2b, earlier) + base2
        rank1 = jnp.sum(jnp.where(oh1, c1, 0.0), axis=0, keepdims=True)
        rank2 = jnp.sum(jnp.where(oh2, c2, 0.0), axis=0, keepdims=True)
        run = run + tot1 + tot2

        rec = jnp.where(sub8 == R_ID1, id1.astype(F32), 0.0)
        rec = jnp.where(sub8 == R_ID2, id2.astype(F32), rec)
        rec = jnp.where(sub8 == R_W1, w1, rec)
        rec = jnp.where(sub8 == R_W2, w2, rec)
        rec = jnp.where(sub8 == R_RANK1, rank1, rec)
        rec = jnp.where(sub8 == R_RANK2, rank2, rec)
        routet_ref[:, r0:r0 + sub] = rec
        rec_full = jnp.concatenate([rec, jnp.zeros((LANES - rows8, sub), F32)], axis=0)
        route_ref[r0:r0 + sub, :] = rec_full.T
    run_ref[...] = run
    cnt_ref[...] = run


def _routing(logits):
    N = logits.shape[0]
    tm = 2048 if N % 2048 == 0 else ROUTE_SUB
    return pl.pallas_call(
        _route_kernel,
        grid=(N // tm,),
        in_specs=[pl.BlockSpec((tm, LANES), lambda i: (i, 0))],
        out_specs=[
            pl.BlockSpec((8, tm), lambda i: (0, i)),
            pl.BlockSpec((tm, LANES), lambda i: (i, 0)),
            pl.BlockSpec((N_EXPERTS, LANES), lambda i: (0, 0)),
        ],
        out_shape=[
            jax.ShapeDtypeStruct((8, N), F32),
            jax.ShapeDtypeStruct((N, LANES), F32),
            jax.ShapeDtypeStruct((N_EXPERTS, LANES), F32),
        ],
        scratch_shapes=[pltpu.VMEM((N_EXPERTS, LANES), F32)],
        compiler_params=_params(("arbitrary",)),
        name="moe_route",
    )(logits)


SC_WINDOW = 128


def _sc_mesh():
    return plsc.VectorSubcoreMesh(core_axis_name="core", subcore_axis_name="subcore")


def _piece_index(rows, pieces, n_rows):
    return (jnp.arange(pieces, dtype=jnp.int32)[:, None] * n_rows + rows[None, :]).reshape(1, -1)


def _sc_dispatch(h2, dests, n_rows):
    pieces, N, pc = h2.shape
    w = SC_WINDOW

    @functools.partial(pl.kernel, out_type=jax.ShapeDtypeStruct((pieces * n_rows, pc), h2.dtype),
                       mesh=_sc_mesh(), scratch_types=[], name="moe_dispatch_sc")
    def run(x_hbm, *refs):
        o_hbm = refs[-1]

        def body(x_vmem, *idx_vmem):
            for i_vmem in idx_vmem:
                pltpu.sync_copy(x_vmem, o_hbm.at[i_vmem.at[0]])

        pltpu.emit_pipeline(
            body,
            grid=(pieces * N // w,),
            in_specs=[pl.BlockSpec((w, pc), lambda i: (i, 0))]
            + [pl.BlockSpec((1, w), lambda i: (0, i))] * len(dests),
            out_specs=[],
            core_axis_name=("core", "subcore"),
            dimension_semantics=(pltpu.PARALLEL,),
        )(x_hbm, *refs[:-1])

    xs = run(h2.reshape(pieces * N, pc), *[_piece_index(d, pieces, n_rows) for d in dests])
    return xs.reshape(pieces, n_rows, pc)


def _sc_return(yb, dest):
    M = dest.shape[0]
    pieces, P, pc = yb.shape
    w = SC_WINDOW

    @functools.partial(pl.kernel, out_type=jax.ShapeDtypeStruct((pieces * M, pc), yb.dtype),
                       mesh=_sc_mesh(), scratch_types=[], name="moe_return_sc")
    def run(y_hbm, i_hbm, o_hbm):
        def body(i_vmem, o_vmem):
            pltpu.sync_copy(y_hbm.at[i_vmem.at[0]], o_vmem)

        pltpu.emit_pipeline(
            body,
            grid=(pieces * M // w,),
            in_specs=[pl.BlockSpec((1, w), lambda i: (0, i))],
            out_specs=[pl.BlockSpec((w, pc), lambda i: (i, 0))],
            core_axis_name=("core", "subcore"),
            dimension_semantics=(pltpu.PARALLEL,),
        )(i_hbm, o_hbm)

    return run(yb.reshape(pieces * P, pc), _piece_index(dest, pieces, P)).reshape(pieces, M, pc)


def _expert_kernel(ib_ref, ie_ref, nxt_ref, slot_ref, nb_ref, xs_ref, w1_ref, w3_ref, w2_ref, y_ref,
                   wf1, wf3, wf2, w1b, w3b, w2b, sem, *, layer):
    j = pl.program_id(0)
    jp = jnp.maximum(j - 1, 0)
    live = j < nb_ref[0]
    new_expert = jnp.logical_or(j == 0, ie_ref[j] != ie_ref[jp])
    streams = ((w1_ref, wf1), (w3_ref, wf3), (w2_ref, wf2))

    def weight_copies(expert, s):
        return [pltpu.make_async_copy(w.at[layer, expert], buf.at[s], sem.at[s, k])
                for k, (w, buf) in enumerate(streams)]

    @pl.when(j == 0)
    def _():
        for cp in weight_copies(ie_ref[0], 0):
            cp.start()

    @pl.when(jnp.logical_and(new_expert, live))
    def _():
        s = slot_ref[j]
        for cp in weight_copies(ie_ref[j], s):
            cp.wait()
        w1b[...] = wf1[s].astype(BF16)
        w3b[...] = wf3[s].astype(BF16)
        w2b[...] = wf2[s].astype(BF16)

        @pl.when(nxt_ref[j] >= 0)
        def _():
            for cp in weight_copies(nxt_ref[j], 1 - s):
                cp.start()

    @pl.when(live)
    def _():
        xb = _unpack_rows(xs_ref).astype(BF16)
        h1 = _dot(xb, w1b[...])
        h3 = _dot(xb, w3b[...])
        a = (h1 * jax.nn.sigmoid(h1) * h3).astype(BF16)
        for p, words in enumerate(_pack_rows(_round_bf16(_dot(a, w2b[...]))[1])):
            y_ref[p] = words


def _experts(xs, items, w1, w3, w2, layer):
    pieces, P, pc = xs.shape
    _, E, D, DE = w1.shape
    bm = MOE_BLOCK
    n_items_max = P // bm
    rows = lambda j, *prefetch: (0, prefetch[0][j], 0)
    return pl.pallas_call(
        functools.partial(_expert_kernel, layer=layer),
        grid_spec=pltpu.PrefetchScalarGridSpec(
            num_scalar_prefetch=5,
            grid=(n_items_max,),
            in_specs=[
                pl.BlockSpec((pieces, bm, pc), rows),
                pl.BlockSpec(memory_space=pl.ANY),
                pl.BlockSpec(memory_space=pl.ANY),
                pl.BlockSpec(memory_space=pl.ANY),
            ],
            out_specs=pl.BlockSpec((pieces, bm, pc), rows),
            scratch_shapes=[
                pltpu.VMEM((2, D, DE), F32),
                pltpu.VMEM((2, D, DE), F32),
                pltpu.VMEM((2, DE, D), F32),
                pltpu.VMEM((D, DE), BF16),
                pltpu.VMEM((D, DE), BF16),
                pltpu.VMEM((DE, D), BF16),
                pltpu.SemaphoreType.DMA((2, 3)),
            ],
        ),
        out_shape=jax.ShapeDtypeStruct((pieces, P, pc), U32),
        compiler_params=_params(("arbitrary",)),
        name="moe_experts",
    )(*items, xs, w1, w3, w2)


def _combine_kernel(x1_ref, route_ref, mod_ref, fg_ref, y0_ref, y1_ref, *rest):
    o_ref = rest[-1]
    x2 = x1_ref[...] + mod_ref[0][5:6] * _moe_mix(route_ref, y0_ref, y1_ref)
    ms = jnp.mean(x2 * x2, axis=-1, keepdims=True)
    o_ref[...] = x2 * lax.rsqrt(ms + NORM_EPS) * fg_ref[...]


def _final_combine(x1, route, modl, final_g, yk, seq, first_tile, prev_out):
    N, D = x1.shape
    tm = ROW_TILE
    nt = yk.shape[1] // 2 // tm
    tok = lambda i: (first_tile + i, 0)
    in_specs = [
        pl.BlockSpec((tm, D), tok),
        pl.BlockSpec((tm, LANES), tok),
        pl.BlockSpec((1, 6, D), lambda i: (((first_tile + i) * tm) // seq, 0, 0)),
        pl.BlockSpec((1, D), lambda i: (0, 0)),
        pl.BlockSpec((yk.shape[0], tm, ROW_PIECE), lambda i: (0, i, 0)),
        pl.BlockSpec((yk.shape[0], tm, ROW_PIECE), lambda i: (0, nt + i, 0)),
    ]
    args = [x1, route, modl, final_g, yk, yk]
    aliases = {}
    if prev_out is not None:
        in_specs.append(pl.BlockSpec(memory_space=pl.ANY))
        args.append(prev_out)
        aliases = {len(args) - 1: 0}
    return pl.pallas_call(
        _combine_kernel,
        grid=(nt,),
        in_specs=in_specs,
        out_specs=pl.BlockSpec((tm, D), tok),
        out_shape=jax.ShapeDtypeStruct((N, D), F32),
        input_output_aliases=aliases,
        compiler_params=_params(("arbitrary",)),
        name="moe_combine",
    )(*args)


ITEM_LANES = 256
I_BLOCK, I_EXPERT, I_NEXT, I_SLOT, I_COUNT = 0, 1, 2, 3, 4


def _dest_kernel(routet_ref, cnt_ref, d1_ref, d2_ref, d3_ref, items_ref):
    tm = routet_ref.shape[1]
    ne = N_EXPERTS
    bm = float(MOE_BLOCK)
    cnt = cnt_ref[...]
    r = lax.broadcasted_iota(jnp.int32, (ne, ne), 0)
    c = lax.broadcasted_iota(jnp.int32, (ne, ne), 1)
    lower = (r >= c).astype(F32)
    cumsum = lambda a: jnp.dot(lower, a, precision=HIGHEST, preferred_element_type=F32)
    padded = jnp.floor((cnt + (bm - 1.0)) * (1.0 / bm)) * bm
    pend = cumsum(padded)
    pstart = pend - padded
    npad = padded - cnt
    pad_end = cumsum(npad)
    pad_start = pad_end - npad

    tile = lambda a, n: jnp.concatenate([a] * (n // LANES), axis=1)
    expert = lax.broadcasted_iota(jnp.int32, (ne, tm), 0).astype(F32)
    rec = routet_ref[...]
    pstart_t = tile(pstart, tm)

    def sorted_row(row_id, row_rank):
        sel = expert == rec[row_id:row_id + 1, :]
        return jnp.sum(jnp.where(sel, pstart_t, 0.0), axis=0, keepdims=True) + rec[row_rank:row_rank + 1, :]

    dest1 = sorted_row(R_ID1, R_RANK1)
    dest2 = sorted_row(R_ID2, R_RANK2)
    t = (lax.broadcasted_iota(jnp.int32, (1, tm), 1) + pl.program_id(0) * tm).astype(F32)
    ps_t, pe_t = tile(pad_start, tm), tile(pad_end, tm)
    in_e = jnp.logical_and(ps_t <= t, t < pe_t)
    pad_row = jnp.sum(jnp.where(in_e, tile(pstart + cnt, tm) + (t - ps_t), 0.0), axis=0, keepdims=True)
    dest3 = jnp.where(t < pe_t[ne - 1:ne, :], pad_row, dest1)
    d1_ref[0] = dest1.astype(jnp.int32)
    d2_ref[0] = dest2.astype(jnp.int32)
    d3_ref[0] = dest3.astype(jnp.int32)

    @pl.when(pl.program_id(0) == 0)
    def _():
        nl = ITEM_LANES
        pend_i = tile(pend, nl)
        n_items = pend_i[ne - 1:ne, :] * (1.0 / bm)
        blk = jnp.minimum(lax.broadcasted_iota(jnp.int32, (1, nl), 1).astype(F32), n_items - 1.0)
        erow = lax.broadcasted_iota(jnp.int32, (ne, nl), 0).astype(F32)
        ie = jnp.sum(jnp.where(pend_i <= blk * bm, 1.0, 0.0), axis=0, keepdims=True)
        nonempty = tile(cnt, nl) > 0.0
        order = jnp.sum(jnp.where(jnp.logical_and(nonempty, erow < ie), 1.0, 0.0), axis=0, keepdims=True)
        nxt = jnp.min(jnp.where(jnp.logical_and(nonempty, erow > ie), erow, float(ne)), axis=0, keepdims=True)
        nxt = jnp.where(nxt == float(ne), -1.0, nxt)
        slot = order - 2.0 * jnp.floor(order * 0.5)
        row = lax.broadcasted_iota(jnp.int32, (8, nl), 0)
        tab = jnp.where(row == I_BLOCK, blk, 0.0)
        tab = jnp.where(row == I_EXPERT, ie, tab)
        tab = jnp.where(row == I_NEXT, nxt, tab)
        tab = jnp.where(row == I_SLOT, slot, tab)
        tab = jnp.where(row == I_COUNT, n_items, tab)
        items_ref[...] = tab.astype(jnp.int32)


def _destinations(route_t, cnt, n_blocks):
    N = route_t.shape[1]
    tm = 2048 if N % 2048 == 0 else ROUTE_SUB
    out = jax.ShapeDtypeStruct((N // tm, 1, tm), jnp.int32)
    d1, d2, d3, items = pl.pallas_call(
        _dest_kernel,
        grid=(N // tm,),
        in_specs=[pl.BlockSpec((8, tm), lambda i: (0, i)),
                  pl.BlockSpec((N_EXPERTS, LANES), lambda i: (0, 0))],
        out_specs=[pl.BlockSpec((1, 1, tm), lambda i: (i, 0, 0))] * 3
        + [pl.BlockSpec((8, ITEM_LANES), lambda i: (0, 0))],
        out_shape=[out, out, out, jax.ShapeDtypeStruct((8, ITEM_LANES), jnp.int32)],
        compiler_params=_params(("arbitrary",)),
        name="moe_dest",
    )(route_t, cnt)
    table = tuple(items[k, :n_blocks] for k in (I_BLOCK, I_EXPERT, I_NEXT, I_SLOT)) + (items[I_COUNT, :1],)
    return d1.reshape(N), d2.reshape(N), d3.reshape(N), table


def kernel(x, c, mod_w, mod_b, norm1_g, w_in, gate_w2, gate_b, gla_norm_g, conv_w, w_gla_out,
           w_conv_out, w_out, norm2_g, router_group_w, router_group_b, router_expert_w,
           router_expert_b, expert_w1, expert_w3, expert_w2, final_norm_g):
    B, S, D = x.shape
    L = mod_w.shape[0]
    N = B * S
    dk = gate_w2.shape[2]
    rank = gate_w2.shape[1]
    dv = gla_norm_g.shape[1]
    n_slots = N * 2
    n_blocks = n_slots // MOE_BLOCK + N_EXPERTS
    assert S % GLA_TILE == 0 and S % MIX_TILE == 0 and S % IN_TILE == 0 and S % IN_TILE_PLAIN == 0
    assert N % (FINAL_PARTS * ROW_TILE) == 0
    assert n_slots % MOE_BLOCK == 0 and MOE_BLOCK & (MOE_BLOCK - 1) == 0 and n_blocks <= ITEM_LANES
    assert N_EXPERTS * (MOE_BLOCK - 1) <= N
    assert N_GROUPS + N_EXPERTS <= LANES and EXPERTS_PER_GROUP == 8 and 3 * rank <= LANES and D % (2 * ROW_PIECE) == 0

    mod = [_modulation(c, mod_w, mod_b, l)[0] for l in range(L)]
    xf = x.reshape(N, D)
    o_gd = 2 * dk + dv
    o_r = o_gd + rank
    o_b = o_r + dv
    for l in range(L):
        modl = mod[l].reshape(B, 6, D)
        wl = w_in[l]
        wa = jnp.concatenate([wl[:, :o_gd], wl[:, o_r:o_b]], axis=1).astype(BF16)
        wb = wl[:, o_b:].astype(BF16)
        wgd = wl[:, o_gd:o_r]
        wg = jnp.pad(jnp.concatenate([wgd, wgd, wgd], axis=1), ((0, 0), (0, LANES - 3 * rank))).astype(BF16)
        g_hi = gate_w2[l].astype(BF16)
        g_lo = (gate_w2[l] - g_hi.astype(F32)).astype(BF16)
        gw2 = jnp.pad(jnp.concatenate([g_hi, g_hi, g_lo], axis=0), ((0, LANES - 3 * rank), (0, 0)))
        if l == 0:
            za, zb, zg = _in_projection(xf, modl, norm1_g[l].reshape(1, D), conv_w[l], wa, wb, wg, S)
        else:
            xf, za, zb, zg = _in_projection(x1, modl, norm1_g[l].reshape(1, D), conv_w[l], wa, wb, wg, S,
                                            combine=(route, yk, mod[l - 1].reshape(B, 6, D)))
        og = _gla(za, zg, gw2, gate_b[l].reshape(1, dk), gla_norm_g[l].reshape(1, dv), B, S, dk, dv, rank)
        wr = jnp.pad(jnp.concatenate([router_group_w[l], router_expert_w[l]], axis=1),
                     ((0, 0), (0, LANES - N_GROUPS - N_EXPERTS)))
        wr_hi = wr.astype(BF16)
        wr = jnp.concatenate([wr_hi, (wr - wr_hi.astype(F32)).astype(BF16)], axis=1)
        br = jnp.pad(jnp.concatenate([router_group_b[l], router_expert_b[l]]),
                     (0, LANES - N_GROUPS - N_EXPERTS)).reshape(1, LANES)
        x1, h2, logits = _mixer_out(
            og, zb, xf, modl, w_gla_out[l].astype(BF16), w_conv_out[l].astype(BF16),
            w_out[l].astype(BF16), norm2_g[l].reshape(1, D), wr, br, B, S)
        route_t, route, cnt = _routing(logits)
        dest1, dest2, dest3, items = _destinations(route_t, cnt, n_blocks)
        xs = _sc_dispatch(h2, (dest1, dest2, dest3), n_blocks * MOE_BLOCK)
        yb = _experts(xs, items, expert_w1, expert_w3, expert_w2, l)
        if l < L - 1:
            yk = _sc_return(yb, jnp.concatenate([dest1, dest2]))
    out = None
    part = N // FINAL_PARTS
    for p in range(FINAL_PARTS):
        tok = slice(p * part, (p + 1) * part)
        yk = _sc_return(yb, jnp.concatenate([dest1[tok], dest2[tok]]))
        out = _final_combine(x1, route, modl, final_norm_g.reshape(1, D), yk, S,
                             p * part // ROW_TILE, out)
    return out.reshape(B, S, D)
```

```python
import functools

import jax
import jax.numpy as jnp
from jax import lax
from jax.experimental import pallas as pl
from jax.experimental.pallas import tpu as pltpu
from jax.experimental.pallas import tpu_sc as plsc

F32 = jnp.float32
BF16 = jnp.bfloat16
HIGHEST = lax.Precision.HIGHEST

GLA_HEADS = 4
GATE_TAU = 16.0
GLA_CHUNK = 64
N_GROUPS = 8
EXPERTS_PER_GROUP = 8
N_EXPERTS = N_GROUPS * EXPERTS_PER_GROUP
NORM_EPS = 1e-6

LANES = 128
VMEM_LIMIT_BYTES = 56 * 1024 * 1024

IN_TILE = 512
IN_TILE_PLAIN = 1024
GLA_TILE = 1024
MIX_TILE = 1024
MIX_SUB = 1024
ROUTE_SUB = 256
MOE_BLOCK = 512
ROW_TILE = 1024
FINAL_PARTS = 4
ROW_PIECE = 256
U32 = jnp.uint32

R_ID1, R_ID2, R_W1, R_W2, R_RANK1, R_RANK2 = 0, 1, 2, 3, 4, 5


def _dot(a, b):
    return jnp.dot(a, b, preferred_element_type=F32)


def _round_bf16(x):
    xb = x.astype(BF16)
    return xb, xb.astype(F32)


def _pack_rows(xr):
    half = xr.shape[1] // 2
    out = []
    for p in range(half // ROW_PIECE):
        lo = xr[:, p * ROW_PIECE:(p + 1) * ROW_PIECE]
        hi = xr[:, half + p * ROW_PIECE:half + (p + 1) * ROW_PIECE]
        out.append((pltpu.bitcast(lo, U32) >> 16) | pltpu.bitcast(hi, U32))
    return out


def _unpack_rows(ref):
    words = [ref[p] for p in range(ref.shape[0])]
    lo = [pltpu.bitcast(w << 16, F32) for w in words]
    hi = [pltpu.bitcast(w & jnp.uint32(0xFFFF0000), F32) for w in words]
    return jnp.concatenate(lo + hi, axis=-1)


def _const_spec(shape):
    nd = len(shape)
    return pl.BlockSpec(shape, lambda *_: (0,) * nd, pipeline_mode=pl.Buffered(1))


def _params(sem):
    return pltpu.CompilerParams(dimension_semantics=sem, vmem_limit_bytes=VMEM_LIMIT_BYTES)


MOD_ROWS = 256


def _mod_kernel(c_ref, w_ref, b_ref, o_ref):
    c = c_ref[...]
    sc = c * jax.nn.sigmoid(c)
    s_hi, s_r = _round_bf16(sc)
    w = w_ref[0]
    w_hi, w_r = _round_bf16(w)
    s_lo = (sc - s_r).astype(BF16)
    part = _dot(s_hi, w_hi) + _dot(s_lo, w_hi) + _dot(s_hi, (w - w_r).astype(BF16))

    @pl.when(pl.program_id(1) == 0)
    def _():
        o_ref[0] = part + b_ref[0]

    @pl.when(pl.program_id(1) > 0)
    def _():
        o_ref[0] += part


def _modulation(c, mod_w, mod_b, layer):
    L, D, D6 = mod_w.shape
    B = c.shape[0]
    kb = MOD_ROWS
    return pl.pallas_call(
        _mod_kernel,
        grid=(1, D // kb),
        in_specs=[
            pl.BlockSpec((B, kb), lambda l, k: (0, k)),
            pl.BlockSpec((1, kb, D6), lambda l, k: (layer, k, 0)),
            pl.BlockSpec((1, 1, D6), lambda l, k: (layer, 0, 0)),
        ],
        out_specs=pl.BlockSpec((1, B, D6), lambda l, k: (0, 0, 0)),
        out_shape=jax.ShapeDtypeStruct((1, B, D6), F32),
        compiler_params=_params(("arbitrary", "arbitrary")),
        name="adaln_mod",
    )(c, mod_w, mod_b.reshape(L, 1, D6))


CONV_COLS = 256


def _moe_mix(route_ref, y0_ref, y1_ref):
    rec = route_ref[...]
    w1 = rec[:, R_W1:R_W1 + 1]
    w2 = rec[:, R_W2:R_W2 + 1]
    return w1 * _unpack_rows(y0_ref) + w2 * _unpack_rows(y1_ref)


def _inproj_kernel(*refs, seq, fuse_combine):
    if fuse_combine:
        (x_ref, route_ref, y0_ref, y1_ref, modp_ref, mod_ref, g_ref, cw_ref, wa_ref, wb_ref, wg_ref,
         xo_ref, za_ref, zb_ref, zg_ref, carry_ref) = refs
        x = x_ref[...] + modp_ref[0][5:6] * _moe_mix(route_ref, y0_ref, y1_ref)
        xo_ref[...] = x
    else:
        (x_ref, mod_ref, g_ref, cw_ref, wa_ref, wb_ref, wg_ref, za_ref, zb_ref, zg_ref, carry_ref) = refs
        x = x_ref[...]
    tm, d = x.shape

    @pl.when((pl.program_id(0) * tm) % seq == 0)
    def _():
        carry_ref[...] = jnp.zeros_like(carry_ref)

    ms = jnp.mean(x * x, axis=-1, keepdims=True)
    m = mod_ref[0]
    h = (x * lax.rsqrt(ms + NORM_EPS) * (g_ref[...] * (1.0 + m[1:2])) + m[0:1]).astype(BF16)
    for j in range(za_ref.shape[1] // d):
        za_ref[:, j * d:(j + 1) * d] = _dot(h, wa_ref[:, j * d:(j + 1) * d]).astype(BF16)
    zg_ref[...] = _dot(h, wg_ref[...])
    for j in range(1, zb_ref.shape[1] // d):
        zb_ref[:, j * d:(j + 1) * d] = _dot(h, wb_ref[:, (j + 2) * d:(j + 3) * d]).astype(BF16)

    w = CONV_COLS
    rowi = lax.broadcasted_iota(jnp.int32, (tm, w), 0)
    for j in range(d // w):
        cols = slice(j * w, (j + 1) * w)
        cb = _dot(h, wb_ref[:, j * w:(j + 1) * w])
        cc = _dot(h, wb_ref[:, d + j * w:d + (j + 1) * w])
        ch = _dot(h, wb_ref[:, 2 * d + j * w:2 * d + (j + 1) * w])
        u = cc * ch
        prev = carry_ref[:, cols]
        u1 = jnp.where(rowi == 0, prev[7:8], pltpu.roll(u, 1, 0))
        u2 = jnp.where(rowi == 0, prev[6:7], jnp.where(rowi == 1, prev[7:8], pltpu.roll(u, 2, 0)))
        carry_ref[:, cols] = u[tm - 8:tm]
        conv = cw_ref[0:1, cols] * u2 + cw_ref[1:2, cols] * u1 + cw_ref[2:3, cols] * u
        zb_ref[:, cols] = (cb * conv).astype(BF16)


def _in_projection(x, modl, norm_g, conv_w, wa, wb, wg, seq, combine=None):
    N, D = x.shape
    tm = IN_TILE_PLAIN if combine is None else IN_TILE
    nt = N // tm
    ca, cb = wa.shape[1], wb.shape[1] - 2 * D
    tok = lambda i: (i, 0)
    per_batch = lambda i: ((i * tm) // seq, 0, 0)
    in_specs = [pl.BlockSpec((tm, D), tok)]
    args = [x]
    out_specs, out_shape = [], []
    if combine is not None:
        route, yk, mod_prev = combine
        piece_blk = (yk.shape[0], tm, ROW_PIECE)
        in_specs += [
            pl.BlockSpec((tm, LANES), tok),
            pl.BlockSpec(piece_blk, lambda i: (0, i, 0)),
            pl.BlockSpec(piece_blk, lambda i: (0, nt + i, 0)),
            pl.BlockSpec((1, 6, D), per_batch),
        ]
        args += [route, yk, yk, mod_prev]
        out_specs.append(pl.BlockSpec((tm, D), tok))
        out_shape.append(jax.ShapeDtypeStruct((N, D), F32))
    in_specs += [
        pl.BlockSpec((1, 6, D), per_batch),
        _const_spec((1, D)),
        _const_spec(conv_w.shape),
        _const_spec((D, ca)),
        _const_spec(wb.shape),
        _const_spec((D, LANES)),
    ]
    args += [modl, norm_g, conv_w, wa, wb, wg]
    out_specs += [
        pl.BlockSpec((tm, ca), tok),
        pl.BlockSpec((tm, cb), tok),
        pl.BlockSpec((tm, LANES), tok),
    ]
    out_shape += [
        jax.ShapeDtypeStruct((N, ca), BF16),
        jax.ShapeDtypeStruct((N, cb), BF16),
        jax.ShapeDtypeStruct((N, LANES), F32),
    ]
    kern = functools.partial(_inproj_kernel, seq=seq, fuse_combine=combine is not None)
    return pl.pallas_call(
        kern,
        grid=(nt,),
        in_specs=in_specs,
        out_specs=out_specs,
        out_shape=out_shape,
        scratch_shapes=[pltpu.VMEM((8, D), F32)],
        compiler_params=_params(("arbitrary",)),
        name="in_projection",
    )(*args)


def _log_sigmoid(x):
    return jnp.minimum(x, 0.0) - jnp.log(1.0 + jnp.exp(-jnp.abs(x)))


def _gla_kernel(za_ref, zg_ref, gw2_ref, gb_ref, ng_ref, o_ref, st_ref, lg_ref, *, dk, dv, rank):
    heads = GLA_HEADS
    dkh, dvh = dk // heads, dv // heads
    c = GLA_CHUNK
    ts = za_ref.shape[0]

    @pl.when(pl.program_id(1) == 0)
    def _():
        st_ref[...] = jnp.zeros_like(st_ref)

    zg = zg_ref[...]
    zg_hi = zg.astype(BF16)
    zg_lo = (zg - zg_hi.astype(F32)).astype(BF16)
    lane = lax.broadcasted_iota(jnp.int32, zg.shape, 1)
    lhs = jnp.where(jnp.logical_and(lane >= rank, lane < 2 * rank), zg_lo, zg_hi)
    pre = _dot(lhs, gw2_ref[...]) + gb_ref[...]
    lg = _log_sigmoid(pre) * (1.0 / GATE_TAU)
    lg_hi = lg.astype(BF16)
    lg_ref[:, 0:dk] = lg_hi
    lg_ref[:, dk:2 * dk] = (lg - lg_hi.astype(F32)).astype(BF16)

    row = lax.broadcasted_iota(jnp.int32, (c, c), 0)
    col = lax.broadcasted_iota(jnp.int32, (c, c), 1)
    causal = row >= col
    tril = causal.astype(BF16)
    qscale = dkh ** -0.5

    for ci in range(ts // c):
        rows = slice(ci * c, (ci + 1) * c)
        b_two = _dot(tril, lg_ref[rows, :])
        b_all = b_two[:, 0:dk] + b_two[:, dk:2 * dk]
        for hd in range(heads):
            ks = slice(hd * dkh, (hd + 1) * dkh)
            q = za_ref[rows, hd * dkh:(hd + 1) * dkh].astype(F32) * qscale
            k = za_ref[rows, dk + hd * dkh:dk + (hd + 1) * dkh].astype(F32)
            v = za_ref[rows, 2 * dk + hd * dvh:2 * dk + (hd + 1) * dvh]
            r = za_ref[rows, 2 * dk + dv + hd * dvh:2 * dk + dv + (hd + 1) * dvh].astype(F32)
            b = b_all[:, ks]
            b_last = b[c - 1:c, :]
            q_t = (q * jnp.exp(b)).astype(BF16)
            k_t = (k * jnp.exp(-b)).astype(BF16)
            k_s = (k * jnp.exp(b_last - b)).astype(BF16)
            decay = jnp.exp(b_last)
            attn = lax.dot_general(q_t, k_t, (((1,), (1,)), ((), ())), preferred_element_type=F32)
            attn = jnp.where(causal, attn, 0.0).astype(BF16)
            st = st_ref[hd]
            o = _dot(attn, v) + lax.dot_general(
                q_t, st.astype(BF16), (((1,), (1,)), ((), ())), preferred_element_type=F32)
            upd = lax.dot_general(v, k_s, (((0,), (0,)), ((), ())), preferred_element_type=F32)
            st_ref[hd] = st * decay + upd
            ms = jnp.mean(o * o, axis=-1, keepdims=True)
            on = o * lax.rsqrt(ms + NORM_EPS) * ng_ref[:, hd * dvh:(hd + 1) * dvh]
            o_ref[rows, hd * dvh:(hd + 1) * dvh] = (on * (r * jax.nn.sigmoid(r))).astype(BF16)


def _gla(za, zg, gw2, gb, ng, batch, seq, dk, dv, rank):
    N = za.shape[0]
    ts = GLA_TILE
    ns = seq // ts
    heads = GLA_HEADS
    kern = functools.partial(_gla_kernel, dk=dk, dv=dv, rank=rank)
    return pl.pallas_call(
        kern,
        grid=(batch, ns),
        in_specs=[
            pl.BlockSpec((ts, za.shape[1]), lambda b, s: (b * ns + s, 0)),
            pl.BlockSpec((ts, LANES), lambda b, s: (b * ns + s, 0)),
            _const_spec((LANES, dk)),
            _const_spec((1, dk)),
            _const_spec((1, dv)),
        ],
        out_specs=pl.BlockSpec((ts, dv), lambda b, s: (b * ns + s, 0)),
        out_shape=jax.ShapeDtypeStruct((N, dv), BF16),
        scratch_shapes=[
            pltpu.VMEM((heads, dv // heads, dk // heads), F32),
            pltpu.VMEM((ts, 2 * dk), BF16),
        ],
        compiler_params=_params(("arbitrary", "arbitrary")),
        name="gla",
    )(za, zg, gw2, gb, ng)


def _mixout_kernel(og_ref, zb_ref, x_ref, mod_ref, wga_ref, wco_ref, wo_ref, n2_ref,
                   wr_ref, br_ref, x1_ref, h2_ref, logit_ref):
    tm, d = x_ref.shape
    m = mod_ref[0]
    sub = MIX_SUB
    gain2 = n2_ref[...] * (1.0 + m[4:5])
    for r0 in range(0, tm, sub):
        rs = slice(r0, r0 + sub)
        ga = zb_ref[rs, d:2 * d].astype(F32)
        gc = zb_ref[rs, 2 * d:3 * d].astype(F32)
        y_conv = _dot(zb_ref[rs, 0:d], wco_ref[...])
        y_gla = _dot(og_ref[rs, :], wga_ref[...])
        y = jax.nn.sigmoid(ga) * y_gla + jax.nn.sigmoid(gc) * y_conv
        y = _dot(y.astype(BF16), wo_ref[...])
        x1 = x_ref[rs, :] + m[2:3] * y
        x1_ref[rs, :] = x1

        ms = jnp.mean(x1 * x1, axis=-1, keepdims=True)
        h2 = x1 * lax.rsqrt(ms + NORM_EPS) * gain2 + m[3:4]
        h_hi, h_r = _round_bf16(h2)
        for p, words in enumerate(_pack_rows(h_r)):
            h2_ref[p, rs, :] = words

        h_lo = (h2 - h_r).astype(BF16)
        two = _dot(h_hi, wr_ref[...])
        logits = two[:, 0:LANES] + two[:, LANES:2 * LANES] + _dot(h_lo, wr_ref[:, 0:LANES]) + br_ref[...]
        logit_ref[rs, :] = logits


def _mixer_out(og, zb, x, modl, wga, wco, wo, n2g, wr, br, batch, seq):
    N, D = x.shape
    tm = MIX_TILE
    ns = seq // tm
    tok = lambda b, s: (b * ns + s, 0)
    return pl.pallas_call(
        _mixout_kernel,
        grid=(batch, ns),
        in_specs=[
            pl.BlockSpec((tm, D), tok),
            pl.BlockSpec((tm, zb.shape[1]), tok),
            pl.BlockSpec((tm, D), tok),
            pl.BlockSpec((1, 6, D), lambda b, s: (b, 0, 0)),
            _const_spec((D, D)),
            _const_spec((D, D)),
            _const_spec((D, D)),
            _const_spec((1, D)),
            _const_spec((D, 2 * LANES)),
            _const_spec((1, LANES)),
        ],
        out_specs=[
            pl.BlockSpec((tm, D), tok),
            pl.BlockSpec((D // 2 // ROW_PIECE, tm, ROW_PIECE), lambda b, s: (0, b * ns + s, 0)),
            pl.BlockSpec((tm, LANES), tok),
        ],
        out_shape=[
            jax.ShapeDtypeStruct((N, D), F32),
            jax.ShapeDtypeStruct((D // 2 // ROW_PIECE, N, ROW_PIECE), U32),
            jax.ShapeDtypeStruct((N, LANES), F32),
        ],
        compiler_params=_params(("arbitrary", "arbitrary")),
        name="mixer_out",
    )(og, zb, x, modl, wga, wco, wo, n2g, wr, br)


def _route_kernel(logit_ref, routet_ref, route_ref, cnt_ref, run_ref):
    tm = logit_ref.shape[0]
    sub = ROUTE_SUB
    rows8 = EXPERTS_PER_GROUP

    @pl.when(pl.program_id(0) == 0)
    def _():
        run_ref[...] = jnp.zeros_like(run_ref)

    sub8 = lax.broadcasted_iota(jnp.int32, (rows8, sub), 0)
    erow = lax.broadcasted_iota(jnp.int32, (N_EXPERTS, sub), 0)
    tr = lax.broadcasted_iota(jnp.int32, (sub, sub), 0)
    tc = lax.broadcasted_iota(jnp.int32, (sub, sub), 1)
    earlier = (tr < tc).astype(BF16)
    ones = jnp.ones((sub, LANES), BF16)
    neg = -jnp.inf
    run = run_ref[...]
    for r0 in range(0, tm, sub):
        lt = logit_ref[r0:r0 + sub, :].T
        gl = lt[0:N_GROUPS, :]
        gmax = jnp.max(gl, axis=0, keepdims=True)
        gsum = jnp.sum(jnp.exp(gl - gmax), axis=0, keepdims=True)
        g_w = 1.0 / gsum
        g_idx = jnp.min(jnp.where(gl == gmax, sub8, N_GROUPS), axis=0, keepdims=True)
        el = lt[N_GROUPS:N_GROUPS + rows8, :]
        for g in range(1, N_GROUPS):
            el = jnp.where(g_idx == g, lt[N_GROUPS + g * rows8:N_GROUPS + (g + 1) * rows8, :], el)
        e1 = jnp.max(el, axis=0, keepdims=True)
        i1 = jnp.min(jnp.where(el == e1, sub8, rows8), axis=0, keepdims=True)
        el2 = jnp.where(sub8 == i1, neg, el)
        e2 = jnp.max(el2, axis=0, keepdims=True)
        i2 = jnp.min(jnp.where(el2 == e2, sub8, rows8), axis=0, keepdims=True)
        ratio = jnp.exp(e2 - e1)
        w1 = g_w / (1.0 + ratio)
        w2 = g_w * ratio / (1.0 + ratio)
        id1 = g_idx * rows8 + i1
        id2 = g_idx * rows8 + i2

        oh1 = erow == id1
        oh2 = erow == id2
        oh1b = jnp.where(oh1, 1.0, 0.0).astype(BF16)
        oh2b = jnp.where(oh2, 1.0, 0.0).astype(BF16)
        tot1 = _dot(oh1b, ones)
        tot2 = _dot(oh2b, ones)
        base1 = jnp.concatenate([run] * (sub // LANES), axis=1)
        base2 = jnp.concatenate([run + tot1] * (sub // LANES), axis=1)
        c1 = _dot(oh1b, earlier) + base1
        c2 = _dot(oh2b, earlier) + base2
        rank1 = jnp.sum(jnp.where(oh1, c1, 0.0), axis=0, keepdims=True)
        rank2 = jnp.sum(jnp.where(oh2, c2, 0.0), axis=0, keepdims=True)
        run = run + tot1 + tot2

        rec = jnp.where(sub8 == R_ID1, id1.astype(F32), 0.0)
        rec = jnp.where(sub8 == R_ID2, id2.astype(F32), rec)
        rec = jnp.where(sub8 == R_W1, w1, rec)
        rec = jnp.where(sub8 == R_W2, w2, rec)
        rec = jnp.where(sub8 == R_RANK1, rank1, rec)
        rec = jnp.where(sub8 == R_RANK2, rank2, rec)
        routet_ref[:, r0:r0 + sub] = rec
        rec_full = jnp.concatenate([rec, jnp.zeros((LANES - rows8, sub), F32)], axis=0)
        route_ref[r0:r0 + sub, :] = rec_full.T
    run_ref[...] = run
    cnt_ref[...] = run


def _routing(logits):
    N = logits.shape[0]
    tm = 2048 if N % 2048 == 0 else ROUTE_SUB
    return pl.pallas_call(
        _route_kernel,
        grid=(N // tm,),
        in_specs=[pl.BlockSpec((tm, LANES), lambda i: (i, 0))],
        out_specs=[
            pl.BlockSpec((8, tm), lambda i: (0, i)),
            pl.BlockSpec((tm, LANES), lambda i: (i, 0)),
            pl.BlockSpec((N_EXPERTS, LANES), lambda i: (0, 0)),
        ],
        out_shape=[
            jax.ShapeDtypeStruct((8, N), F32),
            jax.ShapeDtypeStruct((N, LANES), F32),
            jax.ShapeDtypeStruct((N_EXPERTS, LANES), F32),
        ],
        scratch_shapes=[pltpu.VMEM((N_EXPERTS, LANES), F32)],
        compiler_params=_params(("arbitrary",)),
        name="moe_route",
    )(logits)


SC_WINDOW = 128


def _sc_mesh():
    return plsc.VectorSubcoreMesh(core_axis_name="core", subcore_axis_name="subcore")


def _piece_index(rows, pieces, n_rows):
    return (jnp.arange(pieces, dtype=jnp.int32)[:, None] * n_rows + rows[None, :]).reshape(1, -1)


def _sc_dispatch(h2, dests, n_rows):
    pieces, N, pc = h2.shape
    w = SC_WINDOW

    @functools.partial(pl.kernel, out_type=jax.ShapeDtypeStruct((pieces * n_rows, pc), h2.dtype),
                       mesh=_sc_mesh(), scratch_types=[], name="moe_dispatch_sc")
    def run(x_hbm, *refs):
        o_hbm = refs[-1]

        def body(x_vmem, *idx_vmem):
            for i_vmem in idx_vmem:
                pltpu.sync_copy(x_vmem, o_hbm.at[i_vmem.at[0]])

        pltpu.emit_pipeline(
            body,
            grid=(pieces * N // w,),
            in_specs=[pl.BlockSpec((w, pc), lambda i: (i, 0))]
            + [pl.BlockSpec((1, w), lambda i: (0, i))] * len(dests),
            out_specs=[],
            core_axis_name=("core", "subcore"),
            dimension_semantics=(pltpu.PARALLEL,),
        )(x_hbm, *refs[:-1])

    xs = run(h2.reshape(pieces * N, pc), *[_piece_index(d, pieces, n_rows) for d in dests])
    return xs.reshape(pieces, n_rows, pc)


def _sc_return(yb, dest):
    M = dest.shape[0]
    pieces, P, pc = yb.shape
    w = SC_WINDOW

    @functools.partial(pl.kernel, out_type=jax.ShapeDtypeStruct((pieces * M, pc), yb.dtype),
                       mesh=_sc_mesh(), scratch_types=[], name="moe_return_sc")
    def run(y_hbm, i_hbm, o_hbm):
        def body(i_vmem, o_vmem):
            pltpu.sync_copy(y_hbm.at[i_vmem.at[0]], o_vmem)

        pltpu.emit_pipeline(
            body,
            grid=(pieces * M // w,),
            in_specs=[pl.BlockSpec((1, w), lambda i: (0, i))],
            out_specs=[pl.BlockSpec((w, pc), lambda i: (i, 0))],
            core_axis_name=("core", "subcore"),
            dimension_semantics=(pltpu.PARALLEL,),
        )(i_hbm, o_hbm)

    return run(yb.reshape(pieces * P, pc), _piece_index(dest, pieces, P)).reshape(pieces, M, pc)


def _expert_kernel(ib_ref, ie_ref, nxt_ref, slot_ref, nb_ref, xs_ref, w1_ref, w3_ref, w2_ref, y_ref,
                   wf1, wf3, wf2, w1b, w3b, w2b, sem, *, layer):
    j = pl.program_id(0)
    jp = jnp.maximum(j - 1, 0)
    live = j < nb_ref[0]
    new_expert = jnp.logical_or(j == 0, ie_ref[j] != ie_ref[jp])
    streams = ((w1_ref, wf1), (w3_ref, wf3), (w2_ref, wf2))

    def weight_copies(expert, s):
        return [pltpu.make_async_copy(w.at[layer, expert], buf.at[s], sem.at[s, k])
                for k, (w, buf) in enumerate(streams)]

    @pl.when(j == 0)
    def _():
        for cp in weight_copies(ie_ref[0], 0):
            cp.start()

    @pl.when(jnp.logical_and(new_expert, live))
    def _():
        s = slot_ref[j]
        for cp in weight_copies(ie_ref[j], s):
            cp.wait()
        w1b[...] = wf1[s].astype(BF16)
        w3b[...] = wf3[s].astype(BF16)
        w2b[...] = wf2[s].astype(BF16)

        @pl.when(nxt_ref[j] >= 0)
        def _():
            for cp in weight_copies(nxt_ref[j], 1 - s):
                cp.start()

    @pl.when(live)
    def _():
        xb = _unpack_rows(xs_ref).astype(BF16)
        h1 = _dot(xb, w1b[...])
        h3 = _dot(xb, w3b[...])
        a = (h1 * jax.nn.sigmoid(h1) * h3).astype(BF16)
        for p, words in enumerate(_pack_rows(_round_bf16(_dot(a, w2b[...]))[1])):
            y_ref[p] = words


def _experts(xs, items, w1, w3, w2, layer):
    pieces, P, pc = xs.shape
    _, E, D, DE = w1.shape
    bm = MOE_BLOCK
    n_items_max = P // bm
    rows = lambda j, *prefetch: (0, prefetch[0][j], 0)
    return pl.pallas_call(
        functools.partial(_expert_kernel, layer=layer),
        grid_spec=pltpu.PrefetchScalarGridSpec(
            num_scalar_prefetch=5,
            grid=(n_items_max,),
            in_specs=[
                pl.BlockSpec((pieces, bm, pc), rows),
                pl.BlockSpec(memory_space=pl.ANY),
                pl.BlockSpec(memory_space=pl.ANY),
                pl.BlockSpec(memory_space=pl.ANY),
            ],
            out_specs=pl.BlockSpec((pieces, bm, pc), rows),
            scratch_shapes=[
                pltpu.VMEM((2, D, DE), F32),
                pltpu.VMEM((2, D, DE), F32),
                pltpu.VMEM((2, DE, D), F32),
                pltpu.VMEM((D, DE), BF16),
                pltpu.VMEM((D, DE), BF16),
                pltpu.VMEM((DE, D), BF16),
                pltpu.SemaphoreType.DMA((2, 3)),
            ],
        ),
        out_shape=jax.ShapeDtypeStruct((pieces, P, pc), U32),
        compiler_params=_params(("arbitrary",)),
        name="moe_experts",
    )(*items, xs, w1, w3, w2)


def _combine_kernel(x1_ref, route_ref, mod_ref, fg_ref, y0_ref, y1_ref, *rest):
    o_ref = rest[-1]
    x2 = x1_ref[...] + mod_ref[0][5:6] * _moe_mix(route_ref, y0_ref, y1_ref)
    ms = jnp.mean(x2 * x2, axis=-1, keepdims=True)
    o_ref[...] = x2 * lax.rsqrt(ms + NORM_EPS) * fg_ref[...]


def _final_combine(x1, route, modl, final_g, yk, seq, first_tile, prev_out):
    N, D = x1.shape
    tm = ROW_TILE
    nt = yk.shape[1] // 2 // tm
    tok = lambda i: (first_tile + i, 0)
    in_specs = [
        pl.BlockSpec((tm, D), tok),
        pl.BlockSpec((tm, LANES), tok),
        pl.BlockSpec((1, 6, D), lambda i: (((first_tile + i) * tm) // seq, 0, 0)),
        pl.BlockSpec((1, D), lambda i: (0, 0)),
        pl.BlockSpec((yk.shape[0], tm, ROW_PIECE), lambda i: (0, i, 0)),
        pl.BlockSpec((yk.shape[0], tm, ROW_PIECE), lambda i: (0, nt + i, 0)),
    ]
    args = [x1, route, modl, final_g, yk, yk]
    aliases = {}
    if prev_out is not None:
        in_specs.append(pl.BlockSpec(memory_space=pl.ANY))
        args.append(prev_out)
        aliases = {len(args) - 1: 0}
    return pl.pallas_call(
        _combine_kernel,
        grid=(nt,),
        in_specs=in_specs,
        out_specs=pl.BlockSpec((tm, D), tok),
        out_shape=jax.ShapeDtypeStruct((N, D), F32),
        input_output_aliases=aliases,
        compiler_params=_params(("arbitrary",)),
        name="moe_combine",
    )(*args)


ITEM_LANES = 256
I_BLOCK, I_EXPERT, I_NEXT, I_SLOT, I_COUNT = 0, 1, 2, 3, 4


def _dest_kernel(routet_ref, cnt_ref, d1_ref, d2_ref, d3_ref, items_ref):
    tm = routet_ref.shape[1]
    ne = N_EXPERTS
    bm = float(MOE_BLOCK)
    cnt = cnt_ref[...]
    r = lax.broadcasted_iota(jnp.int32, (ne, ne), 0)
    c = lax.broadcasted_iota(jnp.int32, (ne, ne), 1)
    lower = (r >= c).astype(F32)
    cumsum = lambda a: jnp.dot(lower, a, precision=HIGHEST, preferred_element_type=F32)
    padded = jnp.floor((cnt + (bm - 1.0)) * (1.0 / bm)) * bm
    pend = cumsum(padded)
    pstart = pend - padded
    npad = padded - cnt
    pad_end = cumsum(npad)
    pad_start = pad_end - npad

    tile = lambda a, n: jnp.concatenate([a] * (n // LANES), axis=1)
    expert = lax.broadcasted_iota(jnp.int32, (ne, tm), 0).astype(F32)
    rec = routet_ref[...]
    pstart_t = tile(pstart, tm)

    def sorted_row(row_id, row_rank):
        sel = expert == rec[row_id:row_id + 1, :]
        return jnp.sum(jnp.where(sel, pstart_t, 0.0), axis=0, keepdims=True) + rec[row_rank:row_rank + 1, :]

    dest1 = sorted_row(R_ID1, R_RANK1)
    dest2 = sorted_row(R_ID2, R_RANK2)
    t = (lax.broadcasted_iota(jnp.int32, (1, tm), 1) + pl.program_id(0) * tm).astype(F32)
    ps_t, pe_t = tile(pad_start, tm), tile(pad_end, tm)
    in_e = jnp.logical_and(ps_t <= t, t < pe_t)
    pad_row = jnp.sum(jnp.where(in_e, tile(pstart + cnt, tm) + (t - ps_t), 0.0), axis=0, keepdims=True)
    dest3 = jnp.where(t < pe_t[ne - 1:ne, :], pad_row, dest1)
    d1_ref[0] = dest1.astype(jnp.int32)
    d2_ref[0] = dest2.astype(jnp.int32)
    d3_ref[0] = dest3.astype(jnp.int32)

    @pl.when(pl.program_id(0) == 0)
    def _():
        nl = ITEM_LANES
        pend_i = tile(pend, nl)
        n_items = pend_i[ne - 1:ne, :] * (1.0 / bm)
        blk = jnp.minimum(lax.broadcasted_iota(jnp.int32, (1, nl), 1).astype(F32), n_items - 1.0)
        erow = lax.broadcasted_iota(jnp.int32, (ne, nl), 0).astype(F32)
        ie = jnp.sum(jnp.where(pend_i <= blk * bm, 1.0, 0.0), axis=0, keepdims=True)
        nonempty = tile(cnt, nl) > 0.0
        order = jnp.sum(jnp.where(jnp.logical_and(nonempty, erow < ie), 1.0, 0.0), axis=0, keepdims=True)
        nxt = jnp.min(jnp.where(jnp.logical_and(nonempty, erow > ie), erow, float(ne)), axis=0, keepdims=True)
        nxt = jnp.where(nxt == float(ne), -1.0, nxt)
        slot = order - 2.0 * jnp.floor(order * 0.5)
        row = lax.broadcasted_iota(jnp.int32, (8, nl), 0)
        tab = jnp.where(row == I_BLOCK, blk, 0.0)
        tab = jnp.where(row == I_EXPERT, ie, tab)
        tab = jnp.where(row == I_NEXT, nxt, tab)
        tab = jnp.where(row == I_SLOT, slot, tab)
        tab = jnp.where(row == I_COUNT, n_items, tab)
        items_ref[...] = tab.astype(jnp.int32)


def _destinations(route_t, cnt, n_blocks):
    N = route_t.shape[1]
    tm = 2048 if N % 2048 == 0 else ROUTE_SUB
    out = jax.ShapeDtypeStruct((N // tm, 1, tm), jnp.int32)
    d1, d2, d3, items = pl.pallas_call(
        _dest_kernel,
        grid=(N // tm,),
        in_specs=[pl.BlockSpec((8, tm), lambda i: (0, i)),
                  pl.BlockSpec((N_EXPERTS, LANES), lambda i: (0, 0))],
        out_specs=[pl.BlockSpec((1, 1, tm), lambda i: (i, 0, 0))] * 3
        + [pl.BlockSpec((8, ITEM_LANES), lambda i: (0, 0))],
        out_shape=[out, out, out, jax.ShapeDtypeStruct((8, ITEM_LANES), jnp.int32)],
        compiler_params=_params(("arbitrary",)),
        name="moe_dest",
    )(route_t, cnt)
    table = tuple(items[k, :n_blocks] for k in (I_BLOCK, I_EXPERT, I_NEXT, I_SLOT)) + (items[I_COUNT, :1],)
    return d1.reshape(N), d2.reshape(N), d3.reshape(N), table


def kernel(x, c, mod_w, mod_b, norm1_g, w_in, gate_w2, gate_b, gla_norm_g, conv_w, w_gla_out,
           w_conv_out, w_out, norm2_g, router_group_w, router_group_b, router_expert_w,
           router_expert_b, expert_w1, expert_w3, expert_w2, final_norm_g):
    B, S, D = x.shape
    L = mod_w.shape[0]
    N = B * S
    dk = gate_w2.shape[2]
    rank = gate_w2.shape[1]
    dv = gla_norm_g.shape[1]
    n_slots = N * 2
    n_blocks = n_slots // MOE_BLOCK + N_EXPERTS
    assert S % GLA_TILE == 0 and S % MIX_TILE == 0 and S % IN_TILE == 0 and S % IN_TILE_PLAIN == 0
    assert N % (FINAL_PARTS * ROW_TILE) == 0
    assert n_slots % MOE_BLOCK == 0 and MOE_BLOCK & (MOE_BLOCK - 1) == 0 and n_blocks <= ITEM_LANES
    assert N_EXPERTS * (MOE_BLOCK - 1) <= N
    assert N_GROUPS + N_EXPERTS <= LANES and EXPERTS_PER_GROUP == 8 and 3 * rank <= LANES and D % (2 * ROW_PIECE) == 0

    mod = [_modulation(c, mod_w, mod_b, l)[0] for l in range(L)]
    xf = x.reshape(N, D)
    o_gd = 2 * dk + dv
    o_r = o_gd + rank
    o_b = o_r + dv
    for l in range(L):
        modl = mod[l].reshape(B, 6, D)
        wl = w_in[l]
        wa = jnp.concatenate([wl[:, :o_gd], wl[:, o_r:o_b]], axis=1).astype(BF16)
        wb = wl[:, o_b:].astype(BF16)
        wgd = wl[:, o_gd:o_r]
        wg = jnp.pad(jnp.concatenate([wgd, wgd, wgd], axis=1), ((0, 0), (0, LANES - 3 * rank))).astype(BF16)
        g_hi = gate_w2[l].astype(BF16)
        g_lo = (gate_w2[l] - g_hi.astype(F32)).astype(BF16)
        gw2 = jnp.pad(jnp.concatenate([g_hi, g_hi, g_lo], axis=0), ((0, LANES - 3 * rank), (0, 0)))
        if l == 0:
            za, zb, zg = _in_projection(xf, modl, norm1_g[l].reshape(1, D), conv_w[l], wa, wb, wg, S)
        else:
            xf, za, zb, zg = _in_projection(x1, modl, norm1_g[l].reshape(1, D), conv_w[l], wa, wb, wg, S,
                                            combine=(route, yk, mod[l - 1].reshape(B, 6, D)))
        og = _gla(za, zg, gw2, gate_b[l].reshape(1, dk), gla_norm_g[l].reshape(1, dv), B, S, dk, dv, rank)
        wr = jnp.pad(jnp.concatenate([router_group_w[l], router_expert_w[l]], axis=1),
                     ((0, 0), (0, LANES - N_GROUPS - N_EXPERTS)))
        wr_hi = wr.astype(BF16)
        wr = jnp.concatenate([wr_hi, (wr - wr_hi.astype(F32)).astype(BF16)], axis=1)
        br = jnp.pad(jnp.concatenate([router_group_b[l], router_expert_b[l]]),
                     (0, LANES - N_GROUPS - N_EXPERTS)).reshape(1, LANES)
        x1, h2, logits = _mixer_out(
            og, zb, xf, modl, w_gla_out[l].astype(BF16), w_conv_out[l].astype(BF16),
            w_out[l].astype(BF16), norm2_g[l].reshape(1, D), wr, br, B, S)
        route_t, route, cnt = _routing(logits)
        dest1, dest2, dest3, items = _destinations(route_t, cnt, n_blocks)
        xs = _sc_dispatch(h2, (dest1, dest2, dest3), n_blocks * MOE_BLOCK)
        yb = _experts(xs, items, expert_w1, expert_w3, expert_w2, l)
        if l < L - 1:
            yk = _sc_return(yb, jnp.concatenate([dest1, dest2]))
    out = None
    part = N // FINAL_PARTS
    for p in range(FINAL_PARTS):
        tok = slice(p * part, (p + 1) * part)
        yk = _sc_return(yb, jnp.concatenate([dest1[tok], dest2[tok]]))
        out = _final_combine(x1, route, modl, final_norm_g.reshape(1, D), yk, S,
                             p * part // ROW_TILE, out)
    return out.reshape(B, S, D)
```

```python
import functools

import jax
import jax.numpy as jnp
from jax import lax
from jax.experimental import pallas as pl
from jax.experimental.pallas import tpu as pltpu
from jax.experimental.pallas import tpu_sc as plsc

F32 = jnp.float32
BF16 = jnp.bfloat16
HIGHEST = lax.Precision.HIGHEST

GLA_HEADS = 4
GATE_TAU = 16.0
GLA_CHUNK = 64
N_GROUPS = 8
EXPERTS_PER_GROUP = 8
N_EXPERTS = N_GROUPS * EXPERTS_PER_GROUP
NORM_EPS = 1e-6

LANES = 128
VMEM_LIMIT_BYTES = 56 * 1024 * 1024

IN_TILE = 512
IN_TILE_PLAIN = 1024
GLA_TILE = 1024
MIX_TILE = 1024
MIX_SUB = 1024
ROUTE_SUB = 256
MOE_BLOCK = 512
EXPERT_ROWS_STEP = 128
ROW_TILE = 1024
FINAL_PARTS = 4
ROW_PIECE = 256
U32 = jnp.uint32

R_ID1, R_ID2, R_W1, R_W2, R_RANK1, R_RANK2 = 0, 1, 2, 3, 4, 5


def _dot(a, b):
    return jnp.dot(a, b, preferred_element_type=F32)


def _round_bf16(x):
    xb = x.astype(BF16)
    return xb, xb.astype(F32)


def _pack_rows(xr):
    half = xr.shape[1] // 2
    out = []
    for p in range(half // ROW_PIECE):
        lo = xr[:, p * ROW_PIECE:(p + 1) * ROW_PIECE]
        hi = xr[:, half + p * ROW_PIECE:half + (p + 1) * ROW_PIECE]
        out.append((pltpu.bitcast(lo, U32) >> 16) | pltpu.bitcast(hi, U32))
    return out


def _unpack_rows(ref, rows=None):
    rows = ref.shape[1] if rows is None else rows
    words = [ref[p, 0:rows, :] for p in range(ref.shape[0])]
    lo = [pltpu.bitcast(w << 16, F32) for w in words]
    hi = [pltpu.bitcast(w & jnp.uint32(0xFFFF0000), F32) for w in words]
    return jnp.concatenate(lo + hi, axis=-1)


def _const_spec(shape):
    nd = len(shape)
    return pl.BlockSpec(shape, lambda *_: (0,) * nd, pipeline_mode=pl.Buffered(1))


def _params(sem):
    return pltpu.CompilerParams(dimension_semantics=sem, vmem_limit_bytes=VMEM_LIMIT_BYTES)


MOD_ROWS = 256


def _mod_kernel(c_ref, w_ref, b_ref, o_ref):
    c = c_ref[...]
    sc = c * jax.nn.sigmoid(c)
    s_hi, s_r = _round_bf16(sc)
    w = w_ref[0]
    w_hi, w_r = _round_bf16(w)
    s_lo = (sc - s_r).astype(BF16)
    part = _dot(s_hi, w_hi) + _dot(s_lo, w_hi) + _dot(s_hi, (w - w_r).astype(BF16))

    @pl.when(pl.program_id(1) == 0)
    def _():
        o_ref[0] = part + b_ref[0]

    @pl.when(pl.program_id(1) > 0)
    def _():
        o_ref[0] += part


def _modulation(c, mod_w, mod_b, layer):
    L, D, D6 = mod_w.shape
    B = c.shape[0]
    kb = MOD_ROWS
    return pl.pallas_call(
        _mod_kernel,
        grid=(1, D // kb),
        in_specs=[
            pl.BlockSpec((B, kb), lambda l, k: (0, k)),
            pl.BlockSpec((1, kb, D6), lambda l, k: (layer, k, 0)),
            pl.BlockSpec((1, 1, D6), lambda l, k: (layer, 0, 0)),
        ],
        out_specs=pl.BlockSpec((1, B, D6), lambda l, k: (0, 0, 0)),
        out_shape=jax.ShapeDtypeStruct((1, B, D6), F32),
        compiler_params=_params(("arbitrary", "arbitrary")),
        name="adaln_mod",
    )(c, mod_w, mod_b.reshape(L, 1, D6))


CONV_COLS = 256


def _moe_mix(route_ref, y0_ref, y1_ref):
    rec = route_ref[...]
    w1 = rec[:, R_W1:R_W1 + 1]
    w2 = rec[:, R_W2:R_W2 + 1]
    return w1 * _unpack_rows(y0_ref) + w2 * _unpack_rows(y1_ref)


def _inproj_kernel(*refs, seq, fuse_combine):
    if fuse_combine:
        (x_ref, route_ref, y0_ref, y1_ref, modp_ref, mod_ref, g_ref, cw_ref, wa_ref, wb_ref, wg_ref,
         xo_ref, za_ref, zb_ref, zg_ref, carry_ref) = refs
        x = x_ref[...] + modp_ref[0][5:6] * _moe_mix(route_ref, y0_ref, y1_ref)
        xo_ref[...] = x
    else:
        (x_ref, mod_ref, g_ref, cw_ref, wa_ref, wb_ref, wg_ref, za_ref, zb_ref, zg_ref, carry_ref) = refs
        x = x_ref[...]
    tm, d = x.shape

    @pl.when((pl.program_id(0) * tm) % seq == 0)
    def _():
        carry_ref[...] = jnp.zeros_like(carry_ref)

    ms = jnp.mean(x * x, axis=-1, keepdims=True)
    m = mod_ref[0]
    h = (x * lax.rsqrt(ms + NORM_EPS) * (g_ref[...] * (1.0 + m[1:2])) + m[0:1]).astype(BF16)
    for j in range(za_ref.shape[1] // d):
        za_ref[:, j * d:(j + 1) * d] = _dot(h, wa_ref[:, j * d:(j + 1) * d]).astype(BF16)
    zg_ref[...] = _dot(h, wg_ref[...])
    for j in range(1, zb_ref.shape[1] // d):
        zb_ref[:, j * d:(j + 1) * d] = _dot(h, wb_ref[:, (j + 2) * d:(j + 3) * d]).astype(BF16)

    w = CONV_COLS
    rowi = lax.broadcasted_iota(jnp.int32, (tm, w), 0)
    for j in range(d // w):
        cols = slice(j * w, (j + 1) * w)
        cb = _dot(h, wb_ref[:, j * w:(j + 1) * w])
        cc = _dot(h, wb_ref[:, d + j * w:d + (j + 1) * w])
        ch = _dot(h, wb_ref[:, 2 * d + j * w:2 * d + (j + 1) * w])
        u = cc * ch
        prev = carry_ref[:, cols]
        u1 = jnp.where(rowi == 0, prev[7:8], pltpu.roll(u, 1, 0))
        u2 = jnp.where(rowi == 0, prev[6:7], jnp.where(rowi == 1, prev[7:8], pltpu.roll(u, 2, 0)))
        carry_ref[:, cols] = u[tm - 8:tm]
        conv = cw_ref[0:1, cols] * u2 + cw_ref[1:2, cols] * u1 + cw_ref[2:3, cols] * u
        zb_ref[:, cols] = (cb * conv).astype(BF16)


def _in_projection(x, modl, norm_g, conv_w, wa, wb, wg, seq, combine=None):
    N, D = x.shape
    tm = IN_TILE_PLAIN if combine is None else IN_TILE
    nt = N // tm
    ca, cb = wa.shape[1], wb.shape[1] - 2 * D
    tok = lambda i: (i, 0)
    per_batch = lambda i: ((i * tm) // seq, 0, 0)
    in_specs = [pl.BlockSpec((tm, D), tok)]
    args = [x]
    out_specs, out_shape = [], []
    if combine is not None:
        route, yk, mod_prev = combine
        piece_blk = (yk.shape[0], tm, ROW_PIECE)
        in_specs += [
            pl.BlockSpec((tm, LANES), tok),
            pl.BlockSpec(piece_blk, lambda i: (0, i, 0)),
            pl.BlockSpec(piece_blk, lambda i: (0, nt + i, 0)),
            pl.BlockSpec((1, 6, D), per_batch),
        ]
        args += [route, yk, yk, mod_prev]
        out_specs.append(pl.BlockSpec((tm, D), tok))
        out_shape.append(jax.ShapeDtypeStruct((N, D), F32))
    in_specs += [
        pl.BlockSpec((1, 6, D), per_batch),
        _const_spec((1, D)),
        _const_spec(conv_w.shape),
        _const_spec((D, ca)),
        _const_spec(wb.shape),
        _const_spec((D, LANES)),
    ]
    args += [modl, norm_g, conv_w, wa, wb, wg]
    out_specs += [
        pl.BlockSpec((tm, ca), tok),
        pl.BlockSpec((tm, cb), tok),
        pl.BlockSpec((tm, LANES), tok),
    ]
    out_shape += [
        jax.ShapeDtypeStruct((N, ca), BF16),
        jax.ShapeDtypeStruct((N, cb), BF16),
        jax.ShapeDtypeStruct((N, LANES), F32),
    ]
    kern = functools.partial(_inproj_kernel, seq=seq, fuse_combine=combine is not None)
    return pl.pallas_call(
        kern,
        grid=(nt,),
        in_specs=in_specs,
        out_specs=out_specs,
        out_shape=out_shape,
        scratch_shapes=[pltpu.VMEM((8, D), F32)],
        compiler_params=_params(("arbitrary",)),
        name="in_projection",
    )(*args)


def _log_sigmoid(x):
    return jnp.minimum(x, 0.0) - jnp.log(1.0 + jnp.exp(-jnp.abs(x)))


def _gla_kernel(za_ref, zg_ref, gw2_ref, gb_ref, ng_ref, o_ref, st_ref, lg_ref, *, dk, dv, rank):
    heads = GLA_HEADS
    dkh, dvh = dk // heads, dv // heads
    c = GLA_CHUNK
    ts = za_ref.shape[0]

    @pl.when(pl.program_id(1) == 0)
    def _():
        st_ref[...] = jnp.zeros_like(st_ref)

    zg = zg_ref[...]
    zg_hi = zg.astype(BF16)
    zg_lo = (zg - zg_hi.astype(F32)).astype(BF16)
    lane = lax.broadcasted_iota(jnp.int32, zg.shape, 1)
    lhs = jnp.where(jnp.logical_and(lane >= rank, lane < 2 * rank), zg_lo, zg_hi)
    pre = _dot(lhs, gw2_ref[...]) + gb_ref[...]
    lg = _log_sigmoid(pre) * (1.0 / GATE_TAU)
    lg_hi = lg.astype(BF16)
    lg_ref[:, 0:dk] = lg_hi
    lg_ref[:, dk:2 * dk] = (lg - lg_hi.astype(F32)).astype(BF16)

    row = lax.broadcasted_iota(jnp.int32, (c, c), 0)
    col = lax.broadcasted_iota(jnp.int32, (c, c), 1)
    causal = row >= col
    tril = causal.astype(BF16)
    qscale = dkh ** -0.5

    for ci in range(ts // c):
        rows = slice(ci * c, (ci + 1) * c)
        b_two = _dot(tril, lg_ref[rows, :])
        b_all = b_two[:, 0:dk] + b_two[:, dk:2 * dk]
        for hd in range(heads):
            ks = slice(hd * dkh, (hd + 1) * dkh)
            q = za_ref[rows, hd * dkh:(hd + 1) * dkh].astype(F32) * qscale
            k = za_ref[rows, dk + hd * dkh:dk + (hd + 1) * dkh].astype(F32)
            v = za_ref[rows, 2 * dk + hd * dvh:2 * dk + (hd + 1) * dvh]
            r = za_ref[rows, 2 * dk + dv + hd * dvh:2 * dk + dv + (hd + 1) * dvh].astype(F32)
            b = b_all[:, ks]
            b_last = b[c - 1:c, :]
            q_t = (q * jnp.exp(b)).astype(BF16)
            k_t = (k * jnp.exp(-b)).astype(BF16)
            k_s = (k * jnp.exp(b_last - b)).astype(BF16)
            decay = jnp.exp(b_last)
            attn = lax.dot_general(q_t, k_t, (((1,), (1,)), ((), ())), preferred_element_type=F32)
            attn = jnp.where(causal, attn, 0.0).astype(BF16)
            st = st_ref[hd]
            o = _dot(attn, v) + lax.dot_general(
                q_t, st.astype(BF16), (((1,), (1,)), ((), ())), preferred_element_type=F32)
            upd = lax.dot_general(v, k_s, (((0,), (0,)), ((), ())), preferred_element_type=F32)
            st_ref[hd] = st * decay + upd
            ms = jnp.mean(o * o, axis=-1, keepdims=True)
            on = o * lax.rsqrt(ms + NORM_EPS) * ng_ref[:, hd * dvh:(hd + 1) * dvh]
            o_ref[rows, hd * dvh:(hd + 1) * dvh] = (on * (r * jax.nn.sigmoid(r))).astype(BF16)


def _gla(za, zg, gw2, gb, ng, batch, seq, dk, dv, rank):
    N = za.shape[0]
    ts = GLA_TILE
    ns = seq // ts
    heads = GLA_HEADS
    kern = functools.partial(_gla_kernel, dk=dk, dv=dv, rank=rank)
    return pl.pallas_call(
        kern,
        grid=(batch, ns),
        in_specs=[
            pl.BlockSpec((ts, za.shape[1]), lambda b, s: (b * ns + s, 0)),
            pl.BlockSpec((ts, LANES), lambda b, s: (b * ns + s, 0)),
            _const_spec((LANES, dk)),
            _const_spec((1, dk)),
            _const_spec((1, dv)),
        ],
        out_specs=pl.BlockSpec((ts, dv), lambda b, s: (b * ns + s, 0)),
        out_shape=jax.ShapeDtypeStruct((N, dv), BF16),
        scratch_shapes=[
            pltpu.VMEM((heads, dv // heads, dk // heads), F32),
            pltpu.VMEM((ts, 2 * dk), BF16),
        ],
        compiler_params=_params(("arbitrary", "arbitrary")),
        name="gla",
    )(za, zg, gw2, gb, ng)


def _mixout_kernel(og_ref, zb_ref, x_ref, mod_ref, wga_ref, wco_ref, wo_ref, n2_ref,
                   wr_ref, br_ref, x1_ref, h2_ref, logit_ref):
    tm, d = x_ref.shape
    m = mod_ref[0]
    sub = MIX_SUB
    gain2 = n2_ref[...] * (1.0 + m[4:5])
    for r0 in range(0, tm, sub):
        rs = slice(r0, r0 + sub)
        ga = zb_ref[rs, d:2 * d].astype(F32)
        gc = zb_ref[rs, 2 * d:3 * d].astype(F32)
        y_conv = _dot(zb_ref[rs, 0:d], wco_ref[...])
        y_gla = _dot(og_ref[rs, :], wga_ref[...])
        y = jax.nn.sigmoid(ga) * y_gla + jax.nn.sigmoid(gc) * y_conv
        y = _dot(y.astype(BF16), wo_ref[...])
        x1 = x_ref[rs, :] + m[2:3] * y
        x1_ref[rs, :] = x1

        ms = jnp.mean(x1 * x1, axis=-1, keepdims=True)
        h2 = x1 * lax.rsqrt(ms + NORM_EPS) * gain2 + m[3:4]
        h_hi, h_r = _round_bf16(h2)
        for p, words in enumerate(_pack_rows(h_r)):
            h2_ref[p, rs, :] = words

        h_lo = (h2 - h_r).astype(BF16)
        two = _dot(h_hi, wr_ref[...])
        logits = two[:, 0:LANES] + two[:, LANES:2 * LANES] + _dot(h_lo, wr_ref[:, 0:LANES]) + br_ref[...]
        logit_ref[rs, :] = logits


def _mixer_out(og, zb, x, modl, wga, wco, wo, n2g, wr, br, batch, seq):
    N, D = x.shape
    tm = MIX_TILE
    ns = seq // tm
    tok = lambda b, s: (b * ns + s, 0)
    return pl.pallas_call(
        _mixout_kernel,
        grid=(batch, ns),
        in_specs=[
            pl.BlockSpec((tm, D), tok),
            pl.BlockSpec((tm, zb.shape[1]), tok),
            pl.BlockSpec((tm, D), tok),
            pl.BlockSpec((1, 6, D), lambda b, s: (b, 0, 0)),
            _const_spec((D, D)),
            _const_spec((D, D)),
            _const_spec((D, D)),
            _const_spec((1, D)),
            _const_spec((D, 2 * LANES)),
            _const_spec((1, LANES)),
        ],
        out_specs=[
            pl.BlockSpec((tm, D), tok),
            pl.BlockSpec((D // 2 // ROW_PIECE, tm, ROW_PIECE), lambda b, s: (0, b * ns + s, 0)),
            pl.BlockSpec((tm, LANES), tok),
        ],
        out_shape=[
            jax.ShapeDtypeStruct((N, D), F32),
            jax.ShapeDtypeStruct((D // 2 // ROW_PIECE, N, ROW_PIECE), U32),
            jax.ShapeDtypeStruct((N, LANES), F32),
        ],
        compiler_params=_params(("arbitrary", "arbitrary")),
        name="mixer_out",
    )(og, zb, x, modl, wga, wco, wo, n2g, wr, br)


def _route_kernel(logit_ref, routet_ref, route_ref, cnt_ref, run_ref):
    tm = logit_ref.shape[0]
    sub = ROUTE_SUB
    rows8 = EXPERTS_PER_GROUP

    @pl.when(pl.program_id(0) == 0)
    def _():
        run_ref[...] = jnp.zeros_like(run_ref)

    sub8 = lax.broadcasted_iota(jnp.int32, (rows8, sub), 0)
    erow = lax.broadcasted_iota(jnp.int32, (N_EXPERTS, sub), 0)
    tr = lax.broadcasted_iota(jnp.int32, (sub, sub), 0)
    tc = lax.broadcasted_iota(jnp.int32, (sub, sub), 1)
    earlier = (tr < tc).astype(BF16)
    ones = jnp.ones((sub, LANES), BF16)
    neg = -jnp.inf
    run = run_ref[...]
    for r0 in range(0, tm, sub):
        lt = logit_ref[r0:r0 + sub, :].T
        gl = lt[0:N_GROUPS, :]
        gmax = jnp.max(gl, axis=0, keepdims=True)
        gsum = jnp.sum(jnp.exp(gl - gmax), axis=0, keepdims=True)
        g_w = 1.0 / gsum
        g_idx = jnp.min(jnp.where(gl == gmax, sub8, N_GROUPS), axis=0, keepdims=True)
        el = lt[N_GROUPS:N_GROUPS + rows8, :]
        for g in range(1, N_GROUPS):
            el = jnp.where(g_idx == g, lt[N_GROUPS + g * rows8:N_GROUPS + (g + 1) * rows8, :], el)
        e1 = jnp.max(el, axis=0, keepdims=True)
        i1 = jnp.min(jnp.where(el == e1, sub8, rows8), axis=0, keepdims=True)
        el2 = jnp.where(sub8 == i1, neg, el)
        e2 = jnp.max(el2, axis=0, keepdims=True)
        i2 = jnp.min(jnp.where(el2 == e2, sub8, rows8), axis=0, keepdims=True)
        ratio = jnp.exp(e2 - e1)
        w1 = g_w / (1.0 + ratio)
        w2 = g_w * ratio / (1.0 + ratio)
        id1 = g_idx * rows8 + i1
        id2 = g_idx * rows8 + i2

        oh1 = erow == id1
        oh2 = erow == id2
        oh1b = jnp.where(oh1, 1.0, 0.0).astype(BF16)
        oh2b = jnp.where(oh2, 1.0, 0.0).astype(BF16)
        tot1 = _dot(oh1b, ones)
        tot2 = _dot(oh2b, ones)
        base1 = jnp.concatenate([run] * (sub // LANES), axis=1)
        base2 = jnp.concatenate([run + tot1] * (sub // LANES), axis=1)
        c1 = _dot(oh1b, earlier) + base1
        c2 = _dot(oh2b, earlier) + base2
        rank1 = jnp.sum(jnp.where(oh1, c1, 0.0), axis=0, keepdims=True)
        rank2 = jnp.sum(jnp.where(oh2, c2, 0.0), axis=0, keepdims=True)
        run = run + tot1 + tot2

        rec = jnp.where(sub8 == R_ID1, id1.astype(F32), 0.0)
        rec = jnp.where(sub8 == R_ID2, id2.astype(F32), rec)
        rec = jnp.where(sub8 == R_W1, w1, rec)
        rec = jnp.where(sub8 == R_W2, w2, rec)
        rec = jnp.where(sub8 == R_RANK1, rank1, rec)
        rec = jnp.where(sub8 == R_RANK2, rank2, rec)
        routet_ref[:, r0:r0 + sub] = rec
        rec_full = jnp.concatenate([rec, jnp.zeros((LANES - rows8, sub), F32)], axis=0)
        route_ref[r0:r0 + sub, :] = rec_full.T
    run_ref[...] = run
    cnt_ref[...] = run


def _routing(logits):
    N = logits.shape[0]
    tm = 2048 if N % 2048 == 0 else ROUTE_SUB
    return pl.pallas_call(
        _route_kernel,
        grid=(N // tm,),
        in_specs=[pl.BlockSpec((tm, LANES), lambda i: (i, 0))],
        out_specs=[
            pl.BlockSpec((8, tm), lambda i: (0, i)),
            pl.BlockSpec((tm, LANES), lambda i: (i, 0)),
            pl.BlockSpec((N_EXPERTS, LANES), lambda i: (0, 0)),
        ],
        out_shape=[
            jax.ShapeDtypeStruct((8, N), F32),
            jax.ShapeDtypeStruct((N, LANES), F32),
            jax.ShapeDtypeStruct((N_EXPERTS, LANES), F32),
        ],
        scratch_shapes=[pltpu.VMEM((N_EXPERTS, LANES), F32)],
        compiler_params=_params(("arbitrary",)),
        name="moe_route",
    )(logits)


SC_WINDOW = 128


def _sc_mesh():
    return plsc.VectorSubcoreMesh(core_axis_name="core", subcore_axis_name="subcore")


def _piece_index(rows, pieces, n_rows):
    return (jnp.arange(pieces, dtype=jnp.int32)[:, None] * n_rows + rows[None, :]).reshape(1, -1)


def _sc_dispatch(h2, dests, n_rows):
    pieces, N, pc = h2.shape
    w = SC_WINDOW

    @functools.partial(pl.kernel, out_type=jax.ShapeDtypeStruct((pieces * n_rows, pc), h2.dtype),
                       mesh=_sc_mesh(), scratch_types=[], name="moe_dispatch_sc")
    def run(x_hbm, *refs):
        o_hbm = refs[-1]

        def body(x_vmem, *idx_vmem):
            for i_vmem in idx_vmem:
                pltpu.sync_copy(x_vmem, o_hbm.at[i_vmem.at[0]])

        pltpu.emit_pipeline(
            body,
            grid=(pieces * N // w,),
            in_specs=[pl.BlockSpec((w, pc), lambda i: (i, 0))]
            + [pl.BlockSpec((1, w), lambda i: (0, i))] * len(dests),
            out_specs=[],
            core_axis_name=("core", "subcore"),
            dimension_semantics=(pltpu.PARALLEL,),
        )(x_hbm, *refs[:-1])

    xs = run(h2.reshape(pieces * N, pc), *[_piece_index(d, pieces, n_rows) for d in dests])
    return xs.reshape(pieces, n_rows, pc)


def _sc_return(yb, dest):
    M = dest.shape[0]
    pieces, P, pc = yb.shape
    w = SC_WINDOW

    @functools.partial(pl.kernel, out_type=jax.ShapeDtypeStruct((pieces * M, pc), yb.dtype),
                       mesh=_sc_mesh(), scratch_types=[], name="moe_return_sc")
    def run(y_hbm, i_hbm, o_hbm):
        def body(i_vmem, o_vmem):
            pltpu.sync_copy(y_hbm.at[i_vmem.at[0]], o_vmem)

        pltpu.emit_pipeline(
            body,
            grid=(pieces * M // w,),
            in_specs=[pl.BlockSpec((1, w), lambda i: (0, i))],
            out_specs=[pl.BlockSpec((w, pc), lambda i: (i, 0))],
            core_axis_name=("core", "subcore"),
            dimension_semantics=(pltpu.PARALLEL,),
        )(i_hbm, o_hbm)

    return run(yb.reshape(pieces * P, pc), _piece_index(dest, pieces, P)).reshape(pieces, M, pc)


def _expert_kernel(ib_ref, ie_ref, nxt_ref, slot_ref, rows_ref, nb_ref, xs_ref, w1_ref, w3_ref, w2_ref, y_ref,
                   wf1, wf3, wf2, w1b, w3b, w2b, sem, *, layer):
    j = pl.program_id(0)
    jp = jnp.maximum(j - 1, 0)
    live = j < nb_ref[0]
    new_expert = jnp.logical_or(j == 0, ie_ref[j] != ie_ref[jp])
    streams = ((w1_ref, wf1), (w3_ref, wf3), (w2_ref, wf2))

    def weight_copies(expert, s):
        return [pltpu.make_async_copy(w.at[layer, expert], buf.at[s], sem.at[s, k])
                for k, (w, buf) in enumerate(streams)]

    @pl.when(j == 0)
    def _():
        for cp in weight_copies(ie_ref[0], 0):
            cp.start()

    @pl.when(jnp.logical_and(new_expert, live))
    def _():
        s = slot_ref[j]
        for cp in weight_copies(ie_ref[j], s):
            cp.wait()
        w1b[...] = wf1[s].astype(BF16)
        w3b[...] = wf3[s].astype(BF16)
        w2b[...] = wf2[s].astype(BF16)

        @pl.when(nxt_ref[j] >= 0)
        def _():
            for cp in weight_copies(nxt_ref[j], 1 - s):
                cp.start()

    bm = xs_ref.shape[1]
    for m in range(EXPERT_ROWS_STEP, bm + 1, EXPERT_ROWS_STEP):
        @pl.when(jnp.logical_and(live, rows_ref[j] == m))
        def _(m=m):
            xb = _unpack_rows(xs_ref, m).astype(BF16)
            h1 = _dot(xb, w1b[...])
            h3 = _dot(xb, w3b[...])
            a = (h1 * jax.nn.sigmoid(h1) * h3).astype(BF16)
            for p, words in enumerate(_pack_rows(_round_bf16(_dot(a, w2b[...]))[1])):
                y_ref[p, 0:m, :] = words
                if m < bm:
                    y_ref[p, m:bm, :] = jnp.zeros((bm - m, words.shape[1]), U32)


def _experts(xs, items, w1, w3, w2, layer):
    pieces, P, pc = xs.shape
    _, E, D, DE = w1.shape
    bm = MOE_BLOCK
    n_items_max = P // bm
    rows = lambda j, *prefetch: (0, prefetch[0][j], 0)
    return pl.pallas_call(
        functools.partial(_expert_kernel, layer=layer),
        grid_spec=pltpu.PrefetchScalarGridSpec(
            num_scalar_prefetch=6,
            grid=(n_items_max,),
            in_specs=[
                pl.BlockSpec((pieces, bm, pc), rows),
                pl.BlockSpec(memory_space=pl.ANY),
                pl.BlockSpec(memory_space=pl.ANY),
                pl.BlockSpec(memory_space=pl.ANY),
            ],
            out_specs=pl.BlockSpec((pieces, bm, pc), rows),
            scratch_shapes=[
                pltpu.VMEM((2, D, DE), F32),
                pltpu.VMEM((2, D, DE), F32),
                pltpu.VMEM((2, DE, D), F32),
                pltpu.VMEM((D, DE), BF16),
                pltpu.VMEM((D, DE), BF16),
                pltpu.VMEM((DE, D), BF16),
                pltpu.SemaphoreType.DMA((2, 3)),
            ],
        ),
        out_shape=jax.ShapeDtypeStruct((pieces, P, pc), U32),
        compiler_params=_params(("arbitrary",)),
        name="moe_experts",
    )(*items, xs, w1, w3, w2)


def _combine_kernel(x1_ref, route_ref, mod_ref, fg_ref, y0_ref, y1_ref, *rest):
    o_ref = rest[-1]
    x2 = x1_ref[...] + mod_ref[0][5:6] * _moe_mix(route_ref, y0_ref, y1_ref)
    ms = jnp.mean(x2 * x2, axis=-1, keepdims=True)
    o_ref[...] = x2 * lax.rsqrt(ms + NORM_EPS) * fg_ref[...]


def _final_combine(x1, route, modl, final_g, yk, seq, first_tile, prev_out):
    N, D = x1.shape
    tm = ROW_TILE
    nt = yk.shape[1] // 2 // tm
    tok = lambda i: (first_tile + i, 0)
    in_specs = [
        pl.BlockSpec((tm, D), tok),
        pl.BlockSpec((tm, LANES), tok),
        pl.BlockSpec((1, 6, D), lambda i: (((first_tile + i) * tm) // seq, 0, 0)),
        pl.BlockSpec((1, D), lambda i: (0, 0)),
        pl.BlockSpec((yk.shape[0], tm, ROW_PIECE), lambda i: (0, i, 0)),
        pl.BlockSpec((yk.shape[0], tm, ROW_PIECE), lambda i: (0, nt + i, 0)),
    ]
    args = [x1, route, modl, final_g, yk, yk]
    aliases = {}
    if prev_out is not None:
        in_specs.append(pl.BlockSpec(memory_space=pl.ANY))
        args.append(prev_out)
        aliases = {len(args) - 1: 0}
    return pl.pallas_call(
        _combine_kernel,
        grid=(nt,),
        in_specs=in_specs,
        out_specs=pl.BlockSpec((tm, D), tok),
        out_shape=jax.ShapeDtypeStruct((N, D), F32),
        input_output_aliases=aliases,
        compiler_params=_params(("arbitrary",)),
        name="moe_combine",
    )(*args)


ITEM_LANES = 256
I_BLOCK, I_EXPERT, I_NEXT, I_SLOT, I_COUNT, I_ROWS = 0, 1, 2, 3, 4, 5


def _dest_kernel(routet_ref, cnt_ref, d1_ref, d2_ref, d3_ref, items_ref):
    tm = routet_ref.shape[1]
    ne = N_EXPERTS
    bm = float(MOE_BLOCK)
    cnt = cnt_ref[...]
    r = lax.broadcasted_iota(jnp.int32, (ne, ne), 0)
    c = lax.broadcasted_iota(jnp.int32, (ne, ne), 1)
    lower = (r >= c).astype(F32)
    cumsum = lambda a: jnp.dot(lower, a, precision=HIGHEST, preferred_element_type=F32)
    padded = jnp.floor((cnt + (bm - 1.0)) * (1.0 / bm)) * bm
    pend = cumsum(padded)
    pstart = pend - padded
    npad = padded - cnt
    pad_end = cumsum(npad)
    pad_start = pad_end - npad

    tile = lambda a, n: jnp.concatenate([a] * (n // LANES), axis=1)
    expert = lax.broadcasted_iota(jnp.int32, (ne, tm), 0).astype(F32)
    rec = routet_ref[...]
    pstart_t = tile(pstart, tm)

    def sorted_row(row_id, row_rank):
        sel = expert == rec[row_id:row_id + 1, :]
        return jnp.sum(jnp.where(sel, pstart_t, 0.0), axis=0, keepdims=True) + rec[row_rank:row_rank + 1, :]

    dest1 = sorted_row(R_ID1, R_RANK1)
    dest2 = sorted_row(R_ID2, R_RANK2)
    t = (lax.broadcasted_iota(jnp.int32, (1, tm), 1) + pl.program_id(0) * tm).astype(F32)
    ps_t, pe_t = tile(pad_start, tm), tile(pad_end, tm)
    in_e = jnp.logical_and(ps_t <= t, t < pe_t)
    pad_row = jnp.sum(jnp.where(in_e, tile(pstart + cnt, tm) + (t - ps_t), 0.0), axis=0, keepdims=True)
    dest3 = jnp.where(t < pe_t[ne - 1:ne, :], pad_row, dest1)
    d1_ref[0] = dest1.astype(jnp.int32)
    d2_ref[0] = dest2.astype(jnp.int32)
    d3_ref[0] = dest3.astype(jnp.int32)

    @pl.when(pl.program_id(0) == 0)
    def _():
        nl = ITEM_LANES
        pend_i = tile(pend, nl)
        n_items = pend_i[ne - 1:ne, :] * (1.0 / bm)
        blk = jnp.minimum(lax.broadcasted_iota(jnp.int32, (1, nl), 1).astype(F32), n_items - 1.0)
        erow = lax.broadcasted_iota(jnp.int32, (ne, nl), 0).astype(F32)
        ie = jnp.sum(jnp.where(pend_i <= blk * bm, 1.0, 0.0), axis=0, keepdims=True)
        nonempty = tile(cnt, nl) > 0.0
        order = jnp.sum(jnp.where(jnp.logical_and(nonempty, erow < ie), 1.0, 0.0), axis=0, keepdims=True)
        nxt = jnp.min(jnp.where(jnp.logical_and(nonempty, erow > ie), erow, float(ne)), axis=0, keepdims=True)
        nxt = jnp.where(nxt == float(ne), -1.0, nxt)
        slot = order - 2.0 * jnp.floor(order * 0.5)
        own_end = jnp.sum(jnp.where(erow == ie, tile(pstart + cnt, nl), 0.0), axis=0, keepdims=True)
        step = float(EXPERT_ROWS_STEP)
        rows = jnp.minimum(jnp.ceil((own_end - blk * bm) * (1.0 / step)) * step, bm)
        row = lax.broadcasted_iota(jnp.int32, (8, nl), 0)
        tab = jnp.where(row == I_BLOCK, blk, 0.0)
        tab = jnp.where(row == I_EXPERT, ie, tab)
        tab = jnp.where(row == I_NEXT, nxt, tab)
        tab = jnp.where(row == I_SLOT, slot, tab)
        tab = jnp.where(row == I_COUNT, n_items, tab)
        tab = jnp.where(row == I_ROWS, rows, tab)
        items_ref[...] = tab.astype(jnp.int32)


def _destinations(route_t, cnt, n_blocks):
    N = route_t.shape[1]
    tm = 2048 if N % 2048 == 0 else ROUTE_SUB
    out = jax.ShapeDtypeStruct((N // tm, 1, tm), jnp.int32)
    d1, d2, d3, items = pl.pallas_call(
        _dest_kernel,
        grid=(N // tm,),
        in_specs=[pl.BlockSpec((8, tm), lambda i: (0, i)),
                  pl.BlockSpec((N_EXPERTS, LANES), lambda i: (0, 0))],
        out_specs=[pl.BlockSpec((1, 1, tm), lambda i: (i, 0, 0))] * 3
        + [pl.BlockSpec((8, ITEM_LANES), lambda i: (0, 0))],
        out_shape=[out, out, out, jax.ShapeDtypeStruct((8, ITEM_LANES), jnp.int32)],
        compiler_params=_params(("arbitrary",)),
        name="moe_dest",
    )(route_t, cnt)
    table = tuple(items[k, :n_blocks] for k in (I_BLOCK, I_EXPERT, I_NEXT, I_SLOT, I_ROWS)) + (items[I_COUNT, :1],)
    return d1.reshape(N), d2.reshape(N), d3.reshape(N), table


def kernel(x, c, mod_w, mod_b, norm1_g, w_in, gate_w2, gate_b, gla_norm_g, conv_w, w_gla_out,
           w_conv_out, w_out, norm2_g, router_group_w, router_group_b, router_expert_w,
           router_expert_b, expert_w1, expert_w3, expert_w2, final_norm_g):
    B, S, D = x.shape
    L = mod_w.shape[0]
    N = B * S
    dk = gate_w2.shape[2]
    rank = gate_w2.shape[1]
    dv = gla_norm_g.shape[1]
    n_slots = N * 2
    n_blocks = n_slots // MOE_BLOCK + N_EXPERTS
    assert S % GLA_TILE == 0 and S % MIX_TILE == 0 and S % IN_TILE == 0 and S % IN_TILE_PLAIN == 0
    assert N % (FINAL_PARTS * ROW_TILE) == 0
    assert n_slots % MOE_BLOCK == 0 and MOE_BLOCK & (MOE_BLOCK - 1) == 0 and n_blocks <= ITEM_LANES
    assert N_EXPERTS * (MOE_BLOCK - 1) <= N
    assert N_GROUPS + N_EXPERTS <= LANES and EXPERTS_PER_GROUP == 8 and 3 * rank <= LANES and D % (2 * ROW_PIECE) == 0

    mod = [_modulation(c, mod_w, mod_b, l)[0] for l in range(L)]
    xf = x.reshape(N, D)
    o_gd = 2 * dk + dv
    o_r = o_gd + rank
    o_b = o_r + dv
    for l in range(L):
        modl = mod[l].reshape(B, 6, D)
        wl = w_in[l]
        wa = jnp.concatenate([wl[:, :o_gd], wl[:, o_r:o_b]], axis=1).astype(BF16)
        wb = wl[:, o_b:].astype(BF16)
        wgd = wl[:, o_gd:o_r]
        wg = jnp.pad(jnp.concatenate([wgd, wgd, wgd], axis=1), ((0, 0), (0, LANES - 3 * rank))).astype(BF16)
        g_hi = gate_w2[l].astype(BF16)
        g_lo = (gate_w2[l] - g_hi.astype(F32)).astype(BF16)
        gw2 = jnp.pad(jnp.concatenate([g_hi, g_hi, g_lo], axis=0), ((0, LANES - 3 * rank), (0, 0)))
        if l == 0:
            za, zb, zg = _in_projection(xf, modl, norm1_g[l].reshape(1, D), conv_w[l], wa, wb, wg, S)
        else:
            xf, za, zb, zg = _in_projection(x1, modl, norm1_g[l].reshape(1, D), conv_w[l], wa, wb, wg, S,
                                            combine=(route, yk, mod[l - 1].reshape(B, 6, D)))
        og = _gla(za, zg, gw2, gate_b[l].reshape(1, dk), gla_norm_g[l].reshape(1, dv), B, S, dk, dv, rank)
        wr = jnp.pad(jnp.concatenate([router_group_w[l], router_expert_w[l]], axis=1),
                     ((0, 0), (0, LANES - N_GROUPS - N_EXPERTS)))
        wr_hi = wr.astype(BF16)
        wr = jnp.concatenate([wr_hi, (wr - wr_hi.astype(F32)).astype(BF16)], axis=1)
        br = jnp.pad(jnp.concatenate([router_group_b[l], router_expert_b[l]]),
                     (0, LANES - N_GROUPS - N_EXPERTS)).reshape(1, LANES)
        x1, h2, logits = _mixer_out(
            og, zb, xf, modl, w_gla_out[l].astype(BF16), w_conv_out[l].astype(BF16),
            w_out[l].astype(BF16), norm2_g[l].reshape(1, D), wr, br, B, S)
        route_t, route, cnt = _routing(logits)
        dest1, dest2, dest3, items = _destinations(route_t, cnt, n_blocks)
        xs = _sc_dispatch(h2, (dest1, dest2, dest3), n_blocks * MOE_BLOCK)
        yb = _experts(xs, items, expert_w1, expert_w3, expert_w2, l)
        if l < L - 1:
            yk = _sc_return(yb, jnp.concatenate([dest1, dest2]))
    out = None
    part = N // FINAL_PARTS
    for p in range(FINAL_PARTS):
        tok = slice(p * part, (p + 1) * part)
        yk = _sc_return(yb, jnp.concatenate([dest1[tok], dest2[tok]]))
        out = _final_combine(x1, route, modl, final_norm_g.reshape(1, D), yk, S,
                             p * part // ROW_TILE, out)
    return out.reshape(B, S, D)
```

```python
import functools

import jax
import jax.numpy as jnp
from jax import lax
from jax.experimental import pallas as pl
from jax.experimental.pallas import tpu as pltpu
from jax.experimental.pallas import tpu_sc as plsc

F32 = jnp.float32
BF16 = jnp.bfloat16
HIGHEST = lax.Precision.HIGHEST

GLA_HEADS = 4
GATE_TAU = 16.0
GLA_CHUNK = 64
N_GROUPS = 8
EXPERTS_PER_GROUP = 8
N_EXPERTS = N_GROUPS * EXPERTS_PER_GROUP
NORM_EPS = 1e-6

LANES = 128
VMEM_LIMIT_BYTES = 56 * 1024 * 1024

IN_TILE = 512
IN_TILE_PLAIN = 1024
GLA_TILE = 1024
MIX_TILE = 1024
MIX_SUB = 1024
ROUTE_SUB = 256
MOE_BLOCK = 512
EXPERT_ROWS_STEP = 128
ROW_TILE = 1024
FINAL_PARTS = 4
ROW_PIECE = 256
U32 = jnp.uint32

R_ID1, R_ID2, R_W1, R_W2, R_RANK1, R_RANK2 = 0, 1, 2, 3, 4, 5


def _dot(a, b):
    return jnp.dot(a, b, preferred_element_type=F32)


def _round_bf16(x):
    xb = x.astype(BF16)
    return xb, xb.astype(F32)


def _pack_rows(xr):
    half = xr.shape[1] // 2
    out = []
    for p in range(half // ROW_PIECE):
        lo = xr[:, p * ROW_PIECE:(p + 1) * ROW_PIECE]
        hi = xr[:, half + p * ROW_PIECE:half + (p + 1) * ROW_PIECE]
        out.append((pltpu.bitcast(lo, U32) >> 16) | pltpu.bitcast(hi, U32))
    return out


def _unpack_rows(ref, rows=None):
    rows = ref.shape[1] if rows is None else rows
    words = [ref[p, 0:rows, :] for p in range(ref.shape[0])]
    lo = [pltpu.bitcast(w << 16, F32) for w in words]
    hi = [pltpu.bitcast(w & jnp.uint32(0xFFFF0000), F32) for w in words]
    return jnp.concatenate(lo + hi, axis=-1)


def _const_spec(shape):
    nd = len(shape)
    return pl.BlockSpec(shape, lambda *_: (0,) * nd, pipeline_mode=pl.Buffered(1))


def _params(sem):
    return pltpu.CompilerParams(dimension_semantics=sem, vmem_limit_bytes=VMEM_LIMIT_BYTES)


MOD_ROWS = 256


def _mod_kernel(c_ref, w_ref, b_ref, o_ref):
    c = c_ref[...]
    sc = c * jax.nn.sigmoid(c)
    s_hi, s_r = _round_bf16(sc)
    w = w_ref[0]
    w_hi, w_r = _round_bf16(w)
    s_lo = (sc - s_r).astype(BF16)
    part = _dot(s_hi, w_hi) + _dot(s_lo, w_hi) + _dot(s_hi, (w - w_r).astype(BF16))

    @pl.when(pl.program_id(1) == 0)
    def _():
        o_ref[0] = part + b_ref[0]

    @pl.when(pl.program_id(1) > 0)
    def _():
        o_ref[0] += part


def _modulation(c, mod_w, mod_b, layer):
    L, D, D6 = mod_w.shape
    B = c.shape[0]
    kb = MOD_ROWS
    return pl.pallas_call(
        _mod_kernel,
        grid=(1, D // kb),
        in_specs=[
            pl.BlockSpec((B, kb), lambda l, k: (0, k)),
            pl.BlockSpec((1, kb, D6), lambda l, k: (layer, k, 0)),
            pl.BlockSpec((1, 1, D6), lambda l, k: (layer, 0, 0)),
        ],
        out_specs=pl.BlockSpec((1, B, D6), lambda l, k: (0, 0, 0)),
        out_shape=jax.ShapeDtypeStruct((1, B, D6), F32),
        compiler_params=_params(("arbitrary", "arbitrary")),
        name="adaln_mod",
    )(c, mod_w, mod_b.reshape(L, 1, D6))


CONV_COLS = 256


def _moe_mix(route_ref, y0_ref, y1_ref):
    rec = route_ref[...]
    w1 = rec[:, R_W1:R_W1 + 1]
    w2 = rec[:, R_W2:R_W2 + 1]
    return w1 * _unpack_rows(y0_ref) + w2 * _unpack_rows(y1_ref)


def _inproj_kernel(*refs, seq, fuse_combine):
    if fuse_combine:
        (x_ref, route_ref, y0_ref, y1_ref, modp_ref, mod_ref, g_ref, cw_ref, wa_ref, wb_ref, wg_ref,
         xo_ref, za_ref, zb_ref, zg_ref, carry_ref) = refs
        x = x_ref[...] + modp_ref[0][5:6] * _moe_mix(route_ref, y0_ref, y1_ref)
        xo_ref[...] = x
    else:
        (x_ref, mod_ref, g_ref, cw_ref, wa_ref, wb_ref, wg_ref, za_ref, zb_ref, zg_ref, carry_ref) = refs
        x = x_ref[...]
    tm, d = x.shape

    @pl.when((pl.program_id(0) * tm) % seq == 0)
    def _():
        carry_ref[...] = jnp.zeros_like(carry_ref)

    ms = jnp.mean(x * x, axis=-1, keepdims=True)
    m = mod_ref[0]
    h = (x * lax.rsqrt(ms + NORM_EPS) * (g_ref[...] * (1.0 + m[1:2])) + m[0:1]).astype(BF16)
    for j in range(za_ref.shape[1] // d):
        za_ref[:, j * d:(j + 1) * d] = _dot(h, wa_ref[:, j * d:(j + 1) * d]).astype(BF16)
    zg_ref[...] = _dot(h, wg_ref[...])
    for j in range(1, zb_ref.shape[1] // d):
        zb_ref[:, j * d:(j + 1) * d] = _dot(h, wb_ref[:, (j + 2) * d:(j + 3) * d]).astype(BF16)

    w = CONV_COLS
    rowi = lax.broadcasted_iota(jnp.int32, (tm, w), 0)
    for j in range(d // w):
        cols = slice(j * w, (j + 1) * w)
        cb = _dot(h, wb_ref[:, j * w:(j + 1) * w])
        cc = _dot(h, wb_ref[:, d + j * w:d + (j + 1) * w])
        ch = _dot(h, wb_ref[:, 2 * d + j * w:2 * d + (j + 1) * w])
        u = cc * ch
        prev = carry_ref[:, cols]
        u1 = jnp.where(rowi == 0, prev[7:8], pltpu.roll(u, 1, 0))
        u2 = jnp.where(rowi == 0, prev[6:7], jnp.where(rowi == 1, prev[7:8], pltpu.roll(u, 2, 0)))
        carry_ref[:, cols] = u[tm - 8:tm]
        conv = cw_ref[0:1, cols] * u2 + cw_ref[1:2, cols] * u1 + cw_ref[2:3, cols] * u
        zb_ref[:, cols] = (cb * conv).astype(BF16)


def _in_projection(x, modl, norm_g, conv_w, wa, wb, wg, seq, combine=None):
    N, D = x.shape
    tm = IN_TILE_PLAIN if combine is None else IN_TILE
    nt = N // tm
    ca, cb = wa.shape[1], wb.shape[1] - 2 * D
    tok = lambda i: (i, 0)
    per_batch = lambda i: ((i * tm) // seq, 0, 0)
    in_specs = [pl.BlockSpec((tm, D), tok)]
    args = [x]
    out_specs, out_shape = [], []
    if combine is not None:
        route, yk, mod_prev = combine
        piece_blk = (yk.shape[0], tm, ROW_PIECE)
        in_specs += [
            pl.BlockSpec((tm, LANES), tok),
            pl.BlockSpec(piece_blk, lambda i: (0, i, 0)),
            pl.BlockSpec(piece_blk, lambda i: (0, nt + i, 0)),
            pl.BlockSpec((1, 6, D), per_batch),
        ]
        args += [route, yk, yk, mod_prev]
        out_specs.append(pl.BlockSpec((tm, D), tok))
        out_shape.append(jax.ShapeDtypeStruct((N, D), F32))
    in_specs += [
        pl.BlockSpec((1, 6, D), per_batch),
        _const_spec((1, D)),
        _const_spec(conv_w.shape),
        _const_spec((D, ca)),
        _const_spec(wb.shape),
        _const_spec((D, LANES)),
    ]
    args += [modl, norm_g, conv_w, wa, wb, wg]
    out_specs += [
        pl.BlockSpec((tm, ca), tok),
        pl.BlockSpec((tm, cb), tok),
        pl.BlockSpec((tm, LANES), tok),
    ]
    out_shape += [
        jax.ShapeDtypeStruct((N, ca), BF16),
        jax.ShapeDtypeStruct((N, cb), BF16),
        jax.ShapeDtypeStruct((N, LANES), F32),
    ]
    kern = functools.partial(_inproj_kernel, seq=seq, fuse_combine=combine is not None)
    return pl.pallas_call(
        kern,
        grid=(nt,),
        in_specs=in_specs,
        out_specs=out_specs,
        out_shape=out_shape,
        scratch_shapes=[pltpu.VMEM((8, D), F32)],
        compiler_params=_params(("arbitrary",)),
        name="in_projection",
    )(*args)


def _log_sigmoid(x):
    return jnp.minimum(x, 0.0) - jnp.log(1.0 + jnp.exp(-jnp.abs(x)))


def _gla_kernel(za_ref, zg_ref, gw2_ref, gb_ref, ng_ref, o_ref, st_ref, lg_ref, *, dk, dv, rank):
    heads = GLA_HEADS
    dkh, dvh = dk // heads, dv // heads
    c = GLA_CHUNK
    ts = za_ref.shape[0]

    @pl.when(pl.program_id(1) == 0)
    def _():
        st_ref[...] = jnp.zeros_like(st_ref)

    zg = zg_ref[...]
    zg_hi = zg.astype(BF16)
    zg_lo = (zg - zg_hi.astype(F32)).astype(BF16)
    lane = lax.broadcasted_iota(jnp.int32, zg.shape, 1)
    lhs = jnp.where(jnp.logical_and(lane >= rank, lane < 2 * rank), zg_lo, zg_hi)
    pre = _dot(lhs, gw2_ref[...]) + gb_ref[...]
    lg = _log_sigmoid(pre) * (1.0 / GATE_TAU)
    lg_hi = lg.astype(BF16)
    lg_ref[:, 0:dk] = lg_hi
    lg_ref[:, dk:2 * dk] = (lg - lg_hi.astype(F32)).astype(BF16)

    row = lax.broadcasted_iota(jnp.int32, (c, c), 0)
    col = lax.broadcasted_iota(jnp.int32, (c, c), 1)
    causal = row >= col
    tril = causal.astype(BF16)
    qscale = dkh ** -0.5

    for ci in range(ts // c):
        rows = slice(ci * c, (ci + 1) * c)
        b_two = _dot(tril, lg_ref[rows, :])
        b_all = b_two[:, 0:dk] + b_two[:, dk:2 * dk]
        for hd in range(heads):
            ks = slice(hd * dkh, (hd + 1) * dkh)
            q = za_ref[rows, hd * dkh:(hd + 1) * dkh].astype(F32) * qscale
            k = za_ref[rows, dk + hd * dkh:dk + (hd + 1) * dkh].astype(F32)
            v = za_ref[rows, 2 * dk + hd * dvh:2 * dk + (hd + 1) * dvh]
            r = za_ref[rows, 2 * dk + dv + hd * dvh:2 * dk + dv + (hd + 1) * dvh].astype(F32)
            b = b_all[:, ks]
            b_last = b[c - 1:c, :]
            q_t = (q * jnp.exp(b)).astype(BF16)
            k_t = (k * jnp.exp(-b)).astype(BF16)
            k_s = (k * jnp.exp(b_last - b)).astype(BF16)
            decay = jnp.exp(b_last)
            attn = lax.dot_general(q_t, k_t, (((1,), (1,)), ((), ())), preferred_element_type=F32)
            attn = jnp.where(causal, attn, 0.0).astype(BF16)
            st = st_ref[hd]
            o = _dot(attn, v) + lax.dot_general(
                q_t, st.astype(BF16), (((1,), (1,)), ((), ())), preferred_element_type=F32)
            upd = lax.dot_general(v, k_s, (((0,), (0,)), ((), ())), preferred_element_type=F32)
            st_ref[hd] = st * decay + upd
            ms = jnp.mean(o * o, axis=-1, keepdims=True)
            on = o * lax.rsqrt(ms + NORM_EPS) * ng_ref[:, hd * dvh:(hd + 1) * dvh]
            o_ref[rows, hd * dvh:(hd + 1) * dvh] = (on * (r * jax.nn.sigmoid(r))).astype(BF16)


def _gla(za, zg, gw2, gb, ng, batch, seq, dk, dv, rank):
    N = za.shape[0]
    ts = GLA_TILE
    ns = seq // ts
    heads = GLA_HEADS
    kern = functools.partial(_gla_kernel, dk=dk, dv=dv, rank=rank)
    return pl.pallas_call(
        kern,
        grid=(batch, ns),
        in_specs=[
            pl.BlockSpec((ts, za.shape[1]), lambda b, s: (b * ns + s, 0)),
            pl.BlockSpec((ts, LANES), lambda b, s: (b * ns + s, 0)),
            _const_spec((LANES, dk)),
            _const_spec((1, dk)),
            _const_spec((1, dv)),
        ],
        out_specs=pl.BlockSpec((ts, dv), lambda b, s: (b * ns + s, 0)),
        out_shape=jax.ShapeDtypeStruct((N, dv), BF16),
        scratch_shapes=[
            pltpu.VMEM((heads, dv // heads, dk // heads), F32),
            pltpu.VMEM((ts, 2 * dk), BF16),
        ],
        compiler_params=_params(("arbitrary", "arbitrary")),
        name="gla",
    )(za, zg, gw2, gb, ng)


def _mixout_kernel(og_ref, zb_ref, x_ref, mod_ref, wga_ref, wco_ref, wo_ref, n2_ref,
                   wr_ref, br_ref, x1_ref, h2_ref, logit_ref):
    tm, d = x_ref.shape
    m = mod_ref[0]
    sub = MIX_SUB
    gain2 = n2_ref[...] * (1.0 + m[4:5])
    for r0 in range(0, tm, sub):
        rs = slice(r0, r0 + sub)
        ga = zb_ref[rs, d:2 * d].astype(F32)
        gc = zb_ref[rs, 2 * d:3 * d].astype(F32)
        y_conv = _dot(zb_ref[rs, 0:d], wco_ref[...])
        y_gla = _dot(og_ref[rs, :], wga_ref[...])
        y = jax.nn.sigmoid(ga) * y_gla + jax.nn.sigmoid(gc) * y_conv
        y = _dot(y.astype(BF16), wo_ref[...])
        x1 = x_ref[rs, :] + m[2:3] * y
        x1_ref[rs, :] = x1

        ms = jnp.mean(x1 * x1, axis=-1, keepdims=True)
        h2 = x1 * lax.rsqrt(ms + NORM_EPS) * gain2 + m[3:4]
        h_hi, h_r = _round_bf16(h2)
        for p, words in enumerate(_pack_rows(h_r)):
            h2_ref[p, rs, :] = words

        h_lo = (h2 - h_r).astype(BF16)
        two = _dot(h_hi, wr_ref[...])
        logits = two[:, 0:LANES] + two[:, LANES:2 * LANES] + _dot(h_lo, wr_ref[:, 0:LANES]) + br_ref[...]
        logit_ref[rs, :] = logits


def _mixer_out(og, zb, x, modl, wga, wco, wo, n2g, wr, br, batch, seq):
    N, D = x.shape
    tm = MIX_TILE
    ns = seq // tm
    tok = lambda b, s: (b * ns + s, 0)
    return pl.pallas_call(
        _mixout_kernel,
        grid=(batch, ns),
        in_specs=[
            pl.BlockSpec((tm, D), tok),
            pl.BlockSpec((tm, zb.shape[1]), tok),
            pl.BlockSpec((tm, D), tok),
            pl.BlockSpec((1, 6, D), lambda b, s: (b, 0, 0)),
            _const_spec((D, D)),
            _const_spec((D, D)),
            _const_spec((D, D)),
            _const_spec((1, D)),
            _const_spec((D, 2 * LANES)),
            _const_spec((1, LANES)),
        ],
        out_specs=[
            pl.BlockSpec((tm, D), tok),
            pl.BlockSpec((D // 2 // ROW_PIECE, tm, ROW_PIECE), lambda b, s: (0, b * ns + s, 0)),
            pl.BlockSpec((tm, LANES), tok),
        ],
        out_shape=[
            jax.ShapeDtypeStruct((N, D), F32),
            jax.ShapeDtypeStruct((D // 2 // ROW_PIECE, N, ROW_PIECE), U32),
            jax.ShapeDtypeStruct((N, LANES), F32),
        ],
        compiler_params=_params(("arbitrary", "arbitrary")),
        name="mixer_out",
    )(og, zb, x, modl, wga, wco, wo, n2g, wr, br)


def _route_kernel(logit_ref, routet_ref, route_ref, cnt_ref, run_ref):
    tm = logit_ref.shape[0]
    sub = ROUTE_SUB
    rows8 = EXPERTS_PER_GROUP

    @pl.when(pl.program_id(0) == 0)
    def _():
        run_ref[...] = jnp.zeros_like(run_ref)

    sub8 = lax.broadcasted_iota(jnp.int32, (rows8, sub), 0)
    erow = lax.broadcasted_iota(jnp.int32, (N_EXPERTS, sub), 0)
    tr = lax.broadcasted_iota(jnp.int32, (sub, sub), 0)
    tc = lax.broadcasted_iota(jnp.int32, (sub, sub), 1)
    earlier = (tr < tc).astype(BF16)
    ones = jnp.ones((sub, LANES), BF16)
    neg = -jnp.inf
    run = run_ref[...]
    for r0 in range(0, tm, sub):
        lt = logit_ref[r0:r0 + sub, :].T
        gl = lt[0:N_GROUPS, :]
        gmax = jnp.max(gl, axis=0, keepdims=True)
        gsum = jnp.sum(jnp.exp(gl - gmax), axis=0, keepdims=True)
        g_w = 1.0 / gsum
        g_idx = jnp.min(jnp.where(gl == gmax, sub8, N_GROUPS), axis=0, keepdims=True)
        el = lt[N_GROUPS:N_GROUPS + rows8, :]
        for g in range(1, N_GROUPS):
            el = jnp.where(g_idx == g, lt[N_GROUPS + g * rows8:N_GROUPS + (g + 1) * rows8, :], el)
        e1 = jnp.max(el, axis=0, keepdims=True)
        i1 = jnp.min(jnp.where(el == e1, sub8, rows8), axis=0, keepdims=True)
        el2 = jnp.where(sub8 == i1, neg, el)
        e2 = jnp.max(el2, axis=0, keepdims=True)
        i2 = jnp.min(jnp.where(el2 == e2, sub8, rows8), axis=0, keepdims=True)
        ratio = jnp.exp(e2 - e1)
        w1 = g_w / (1.0 + ratio)
        w2 = g_w * ratio / (1.0 + ratio)
        id1 = g_idx * rows8 + i1
        id2 = g_idx * rows8 + i2

        oh1 = erow == id1
        oh2 = erow == id2
        oh1b = jnp.where(oh1, 1.0, 0.0).astype(BF16)
        oh2b = jnp.where(oh2, 1.0, 0.0).astype(BF16)
        tot1 = _dot(oh1b, ones)
        tot2 = _dot(oh2b, ones)
        base1 = jnp.concatenate([run] * (sub // LANES), axis=1)
        base2 = jnp.concatenate([run + tot1] * (sub // LANES), axis=1)
        c1 = _dot(oh1b, earlier) + base1
        c2 = _dot(oh2b, earlier) + base2
        rank1 = jnp.sum(jnp.where(oh1, c1, 0.0), axis=0, keepdims=True)
        rank2 = jnp.sum(jnp.where(oh2, c2, 0.0), axis=0, keepdims=True)
        run = run + tot1 + tot2

        rec = jnp.where(sub8 == R_ID1, id1.astype(F32), 0.0)
        rec = jnp.where(sub8 == R_ID2, id2.astype(F32), rec)
        rec = jnp.where(sub8 == R_W1, w1, rec)
        rec = jnp.where(sub8 == R_W2, w2, rec)
        rec = jnp.where(sub8 == R_RANK1, rank1, rec)
        rec = jnp.where(sub8 == R_RANK2, rank2, rec)
        routet_ref[:, r0:r0 + sub] = rec
        rec_full = jnp.concatenate([rec, jnp.zeros((LANES - rows8, sub), F32)], axis=0)
        route_ref[r0:r0 + sub, :] = rec_full.T
    run_ref[...] = run
    cnt_ref[...] = run


def _routing(logits):
    N = logits.shape[0]
    tm = 2048 if N % 2048 == 0 else ROUTE_SUB
    return pl.pallas_call(
        _route_kernel,
        grid=(N // tm,),
        in_specs=[pl.BlockSpec((tm, LANES), lambda i: (i, 0))],
        out_specs=[
            pl.BlockSpec((8, tm), lambda i: (0, i)),
            pl.BlockSpec((tm, LANES), lambda i: (i, 0)),
            pl.BlockSpec((N_EXPERTS, LANES), lambda i: (0, 0)),
        ],
        out_shape=[
            jax.ShapeDtypeStruct((8, N), F32),
            jax.ShapeDtypeStruct((N, LANES), F32),
            jax.ShapeDtypeStruct((N_EXPERTS, LANES), F32),
        ],
        scratch_shapes=[pltpu.VMEM((N_EXPERTS, LANES), F32)],
        compiler_params=_params(("arbitrary",)),
        name="moe_route",
    )(logits)


SC_WINDOW = 128


def _sc_mesh():
    return plsc.VectorSubcoreMesh(core_axis_name="core", subcore_axis_name="subcore")


def _piece_index(rows, pieces, n_rows):
    return (jnp.arange(pieces, dtype=jnp.int32)[:, None] * n_rows + rows[None, :]).reshape(1, -1)


def _sc_dispatch(h2, dests, n_rows):
    pieces, N, pc = h2.shape
    w = SC_WINDOW

    @functools.partial(pl.kernel, out_type=jax.ShapeDtypeStruct((pieces * n_rows, pc), h2.dtype),
                       mesh=_sc_mesh(), scratch_types=[], name="moe_dispatch_sc")
    def run(x_hbm, *refs):
        o_hbm = refs[-1]

        def body(x_vmem, *idx_vmem):
            for i_vmem in idx_vmem:
                pltpu.sync_copy(x_vmem, o_hbm.at[i_vmem.at[0]])

        pltpu.emit_pipeline(
            body,
            grid=(pieces * N // w,),
            in_specs=[pl.BlockSpec((w, pc), lambda i: (i, 0))]
            + [pl.BlockSpec((1, w), lambda i: (0, i))] * len(dests),
            out_specs=[],
            core_axis_name=("core", "subcore"),
            dimension_semantics=(pltpu.PARALLEL,),
        )(x_hbm, *refs[:-1])

    xs = run(h2.reshape(pieces * N, pc), *[_piece_index(d, pieces, n_rows) for d in dests])
    return xs.reshape(pieces, n_rows, pc)


def _sc_return(yb, dest):
    M = dest.shape[0]
    pieces, P, pc = yb.shape
    w = SC_WINDOW

    @functools.partial(pl.kernel, out_type=jax.ShapeDtypeStruct((pieces * M, pc), yb.dtype),
                       mesh=_sc_mesh(), scratch_types=[], name="moe_return_sc")
    def run(y_hbm, i_hbm, o_hbm):
        def body(i_vmem, o_vmem):
            pltpu.sync_copy(y_hbm.at[i_vmem.at[0]], o_vmem)

        pltpu.emit_pipeline(
            body,
            grid=(pieces * M // w,),
            in_specs=[pl.BlockSpec((1, w), lambda i: (0, i))],
            out_specs=[pl.BlockSpec((w, pc), lambda i: (i, 0))],
            core_axis_name=("core", "subcore"),
            dimension_semantics=(pltpu.PARALLEL,),
        )(i_hbm, o_hbm)

    return run(yb.reshape(pieces * P, pc), _piece_index(dest, pieces, P)).reshape(pieces, M, pc)


def _expert_kernel(ib_ref, ie_ref, nxt_ref, slot_ref, rows_ref, nb_ref, xs_ref, w1_ref, w3_ref, w2_ref, y_ref,
                   wf1, wf3, wf2, sem, *, layer):
    j = pl.program_id(0)
    jp = jnp.maximum(j - 1, 0)
    live = j < nb_ref[0]
    new_expert = jnp.logical_or(j == 0, ie_ref[j] != ie_ref[jp])
    streams = ((w1_ref, wf1), (w3_ref, wf3), (w2_ref, wf2))

    def weight_copies(expert, s):
        return [pltpu.make_async_copy(w.at[layer, expert], buf.at[s], sem.at[s, k])
                for k, (w, buf) in enumerate(streams)]

    @pl.when(j == 0)
    def _():
        for cp in weight_copies(ie_ref[0], 0):
            cp.start()

    @pl.when(jnp.logical_and(new_expert, live))
    def _():
        s = slot_ref[j]
        for cp in weight_copies(ie_ref[j], s):
            cp.wait()

        @pl.when(nxt_ref[j] >= 0)
        def _():
            for cp in weight_copies(nxt_ref[j], 1 - s):
                cp.start()

    bm = xs_ref.shape[1]
    for m in range(EXPERT_ROWS_STEP, bm + 1, EXPERT_ROWS_STEP):
        @pl.when(jnp.logical_and(live, rows_ref[j] == m))
        def _(m=m):
            xb = _unpack_rows(xs_ref, m).astype(BF16)
            s = slot_ref[j]
            h1 = _dot(xb, wf1[s].astype(BF16))
            h3 = _dot(xb, wf3[s].astype(BF16))
            a = (h1 * jax.nn.sigmoid(h1) * h3).astype(BF16)
            for p, words in enumerate(_pack_rows(_round_bf16(_dot(a, wf2[s].astype(BF16)))[1])):
                y_ref[p, 0:m, :] = words
                if m < bm:
                    y_ref[p, m:bm, :] = jnp.zeros((bm - m, words.shape[1]), U32)


def _experts(xs, items, w1, w3, w2, layer):
    pieces, P, pc = xs.shape
    _, E, D, DE = w1.shape
    bm = MOE_BLOCK
    n_items_max = P // bm
    rows = lambda j, *prefetch: (0, prefetch[0][j], 0)
    return pl.pallas_call(
        functools.partial(_expert_kernel, layer=layer),
        grid_spec=pltpu.PrefetchScalarGridSpec(
            num_scalar_prefetch=6,
            grid=(n_items_max,),
            in_specs=[
                pl.BlockSpec((pieces, bm, pc), rows),
                pl.BlockSpec(memory_space=pl.ANY),
                pl.BlockSpec(memory_space=pl.ANY),
                pl.BlockSpec(memory_space=pl.ANY),
            ],
            out_specs=pl.BlockSpec((pieces, bm, pc), rows),
            scratch_shapes=[
                pltpu.VMEM((2, D, DE), F32),
                pltpu.VMEM((2, D, DE), F32),
                pltpu.VMEM((2, DE, D), F32),
                pltpu.SemaphoreType.DMA((2, 3)),
            ],
        ),
        out_shape=jax.ShapeDtypeStruct((pieces, P, pc), U32),
        compiler_params=_params(("arbitrary",)),
        name="moe_experts",
    )(*items, xs, w1, w3, w2)


def _combine_kernel(x1_ref, route_ref, mod_ref, fg_ref, y0_ref, y1_ref, *rest):
    o_ref = rest[-1]
    x2 = x1_ref[...] + mod_ref[0][5:6] * _moe_mix(route_ref, y0_ref, y1_ref)
    ms = jnp.mean(x2 * x2, axis=-1, keepdims=True)
    o_ref[...] = x2 * lax.rsqrt(ms + NORM_EPS) * fg_ref[...]


def _final_combine(x1, route, modl, final_g, yk, seq, first_tile, prev_out):
    N, D = x1.shape
    tm = ROW_TILE
    nt = yk.shape[1] // 2 // tm
    tok = lambda i: (first_tile + i, 0)
    in_specs = [
        pl.BlockSpec((tm, D), tok),
        pl.BlockSpec((tm, LANES), tok),
        pl.BlockSpec((1, 6, D), lambda i: (((first_tile + i) * tm) // seq, 0, 0)),
        pl.BlockSpec((1, D), lambda i: (0, 0)),
        pl.BlockSpec((yk.shape[0], tm, ROW_PIECE), lambda i: (0, i, 0)),
        pl.BlockSpec((yk.shape[0], tm, ROW_PIECE), lambda i: (0, nt + i, 0)),
    ]
    args = [x1, route, modl, final_g, yk, yk]
    aliases = {}
    if prev_out is not None:
        in_specs.append(pl.BlockSpec(memory_space=pl.ANY))
        args.append(prev_out)
        aliases = {len(args) - 1: 0}
    return pl.pallas_call(
        _combine_kernel,
        grid=(nt,),
        in_specs=in_specs,
        out_specs=pl.BlockSpec((tm, D), tok),
        out_shape=jax.ShapeDtypeStruct((N, D), F32),
        input_output_aliases=aliases,
        compiler_params=_params(("arbitrary",)),
        name="moe_combine",
    )(*args)


ITEM_LANES = 256
I_BLOCK, I_EXPERT, I_NEXT, I_SLOT, I_COUNT, I_ROWS = 0, 1, 2, 3, 4, 5


def _dest_kernel(routet_ref, cnt_ref, d1_ref, d2_ref, d3_ref, items_ref):
    tm = routet_ref.shape[1]
    ne = N_EXPERTS
    bm = float(MOE_BLOCK)
    cnt = cnt_ref[...]
    r = lax.broadcasted_iota(jnp.int32, (ne, ne), 0)
    c = lax.broadcasted_iota(jnp.int32, (ne, ne), 1)
    lower = (r >= c).astype(F32)
    cumsum = lambda a: jnp.dot(lower, a, precision=HIGHEST, preferred_element_type=F32)
    padded = jnp.floor((cnt + (bm - 1.0)) * (1.0 / bm)) * bm
    pend = cumsum(padded)
    pstart = pend - padded
    npad = padded - cnt
    pad_end = cumsum(npad)
    pad_start = pad_end - npad

    tile = lambda a, n: jnp.concatenate([a] * (n // LANES), axis=1)
    expert = lax.broadcasted_iota(jnp.int32, (ne, tm), 0).astype(F32)
    rec = routet_ref[...]
    pstart_t = tile(pstart, tm)

    def sorted_row(row_id, row_rank):
        sel = expert == rec[row_id:row_id + 1, :]
        return jnp.sum(jnp.where(sel, pstart_t, 0.0), axis=0, keepdims=True) + rec[row_rank:row_rank + 1, :]

    dest1 = sorted_row(R_ID1, R_RANK1)
    dest2 = sorted_row(R_ID2, R_RANK2)
    t = (lax.broadcasted_iota(jnp.int32, (1, tm), 1) + pl.program_id(0) * tm).astype(F32)
    ps_t, pe_t = tile(pad_start, tm), tile(pad_end, tm)
    in_e = jnp.logical_and(ps_t <= t, t < pe_t)
    pad_row = jnp.sum(jnp.where(in_e, tile(pstart + cnt, tm) + (t - ps_t), 0.0), axis=0, keepdims=True)
    dest3 = jnp.where(t < pe_t[ne - 1:ne, :], pad_row, dest1)
    d1_ref[0] = dest1.astype(jnp.int32)
    d2_ref[0] = dest2.astype(jnp.int32)
    d3_ref[0] = dest3.astype(jnp.int32)

    @pl.when(pl.program_id(0) == 0)
    def _():
        nl = ITEM_LANES
        pend_i = tile(pend, nl)
        n_items = pend_i[ne - 1:ne, :] * (1.0 / bm)
        blk = jnp.minimum(lax.broadcasted_iota(jnp.int32, (1, nl), 1).astype(F32), n_items - 1.0)
        erow = lax.broadcasted_iota(jnp.int32, (ne, nl), 0).astype(F32)
        ie = jnp.sum(jnp.where(pend_i <= blk * bm, 1.0, 0.0), axis=0, keepdims=True)
        nonempty = tile(cnt, nl) > 0.0
        order = jnp.sum(jnp.where(jnp.logical_and(nonempty, erow < ie), 1.0, 0.0), axis=0, keepdims=True)
        nxt = jnp.min(jnp.where(jnp.logical_and(nonempty, erow > ie), erow, float(ne)), axis=0, keepdims=True)
        nxt = jnp.where(nxt == float(ne), -1.0, nxt)
        slot = order - 2.0 * jnp.floor(order * 0.5)
        own_end = jnp.sum(jnp.where(erow == ie, tile(pstart + cnt, nl), 0.0), axis=0, keepdims=True)
        step = float(EXPERT_ROWS_STEP)
        rows = jnp.minimum(jnp.ceil((own_end - blk * bm) * (1.0 / step)) * step, bm)
        row = lax.broadcasted_iota(jnp.int32, (8, nl), 0)
        tab = jnp.where(row == I_BLOCK, blk, 0.0)
        tab = jnp.where(row == I_EXPERT, ie, tab)
        tab = jnp.where(row == I_NEXT, nxt, tab)
        tab = jnp.where(row == I_SLOT, slot, tab)
        tab = jnp.where(row == I_COUNT, n_items, tab)
        tab = jnp.where(row == I_ROWS, rows, tab)
        items_ref[...] = tab.astype(jnp.int32)


def _destinations(route_t, cnt, n_blocks):
    N = route_t.shape[1]
    tm = 2048 if N % 2048 == 0 else ROUTE_SUB
    out = jax.ShapeDtypeStruct((N // tm, 1, tm), jnp.int32)
    d1, d2, d3, items = pl.pallas_call(
        _dest_kernel,
        grid=(N // tm,),
        in_specs=[pl.BlockSpec((8, tm), lambda i: (0, i)),
                  pl.BlockSpec((N_EXPERTS, LANES), lambda i: (0, 0))],
        out_specs=[pl.BlockSpec((1, 1, tm), lambda i: (i, 0, 0))] * 3
        + [pl.BlockSpec((8, ITEM_LANES), lambda i: (0, 0))],
        out_shape=[out, out, out, jax.ShapeDtypeStruct((8, ITEM_LANES), jnp.int32)],
        compiler_params=_params(("arbitrary",)),
        name="moe_dest",
    )(route_t, cnt)
    table = tuple(items[k, :n_blocks] for k in (I_BLOCK, I_EXPERT, I_NEXT, I_SLOT, I_ROWS)) + (items[I_COUNT, :1],)
    return d1.reshape(N), d2.reshape(N), d3.reshape(N), table


def kernel(x, c, mod_w, mod_b, norm1_g, w_in, gate_w2, gate_b, gla_norm_g, conv_w, w_gla_out,
           w_conv_out, w_out, norm2_g, router_group_w, router_group_b, router_expert_w,
           router_expert_b, expert_w1, expert_w3, expert_w2, final_norm_g):
    B, S, D = x.shape
    L = mod_w.shape[0]
    N = B * S
    dk = gate_w2.shape[2]
    rank = gate_w2.shape[1]
    dv = gla_norm_g.shape[1]
    n_slots = N * 2
    n_blocks = n_slots // MOE_BLOCK + N_EXPERTS
    assert S % GLA_TILE == 0 and S % MIX_TILE == 0 and S % IN_TILE == 0 and S % IN_TILE_PLAIN == 0
    assert N % (FINAL_PARTS * ROW_TILE) == 0
    assert n_slots % MOE_BLOCK == 0 and MOE_BLOCK & (MOE_BLOCK - 1) == 0 and n_blocks <= ITEM_LANES
    assert N_EXPERTS * (MOE_BLOCK - 1) <= N
    assert N_GROUPS + N_EXPERTS <= LANES and EXPERTS_PER_GROUP == 8 and 3 * rank <= LANES and D % (2 * ROW_PIECE) == 0

    mod = [_modulation(c, mod_w, mod_b, l)[0] for l in range(L)]
    xf = x.reshape(N, D)
    o_gd = 2 * dk + dv
    o_r = o_gd + rank
    o_b = o_r + dv
    for l in range(L):
        modl = mod[l].reshape(B, 6, D)
        wl = w_in[l]
        wa = jnp.concatenate([wl[:, :o_gd], wl[:, o_r:o_b]], axis=1).astype(BF16)
        wb = wl[:, o_b:].astype(BF16)
        wgd = wl[:, o_gd:o_r]
        wg = jnp.pad(jnp.concatenate([wgd, wgd, wgd], axis=1), ((0, 0), (0, LANES - 3 * rank))).astype(BF16)
        g_hi = gate_w2[l].astype(BF16)
        g_lo = (gate_w2[l] - g_hi.astype(F32)).astype(BF16)
        gw2 = jnp.pad(jnp.concatenate([g_hi, g_hi, g_lo], axis=0), ((0, LANES - 3 * rank), (0, 0)))
        if l == 0:
            za, zb, zg = _in_projection(xf, modl, norm1_g[l].reshape(1, D), conv_w[l], wa, wb, wg, S)
        else:
            xf, za, zb, zg = _in_projection(x1, modl, norm1_g[l].reshape(1, D), conv_w[l], wa, wb, wg, S,
                                            combine=(route, yk, mod[l - 1].reshape(B, 6, D)))
        og = _gla(za, zg, gw2, gate_b[l].reshape(1, dk), gla_norm_g[l].reshape(1, dv), B, S, dk, dv, rank)
        wr = jnp.pad(jnp.concatenate([router_group_w[l], router_expert_w[l]], axis=1),
                     ((0, 0), (0, LANES - N_GROUPS - N_EXPERTS)))
        wr_hi = wr.astype(BF16)
        wr = jnp.concatenate([wr_hi, (wr - wr_hi.astype(F32)).astype(BF16)], axis=1)
        br = jnp.pad(jnp.concatenate([router_group_b[l], router_expert_b[l]]),
                     (0, LANES - N_GROUPS - N_EXPERTS)).reshape(1, LANES)
        x1, h2, logits = _mixer_out(
            og, zb, xf, modl, w_gla_out[l].astype(BF16), w_conv_out[l].astype(BF16),
            w_out[l].astype(BF16), norm2_g[l].reshape(1, D), wr, br, B, S)
        route_t, route, cnt = _routing(logits)
        dest1, dest2, dest3, items = _destinations(route_t, cnt, n_blocks)
        xs = _sc_dispatch(h2, (dest1, dest2, dest3), n_blocks * MOE_BLOCK)
        yb = _experts(xs, items, expert_w1, expert_w3, expert_w2, l)
        if l < L - 1:
            yk = _sc_return(yb, jnp.concatenate([dest1, dest2]))
    out = None
    part = N // FINAL_PARTS
    for p in range(FINAL_PARTS):
        tok = slice(p * part, (p + 1) * part)
        yk = _sc_return(yb, jnp.concatenate([dest1[tok], dest2[tok]]))
        out = _final_combine(x1, route, modl, final_norm_g.reshape(1, D), yk, S,
                             p * part // ROW_TILE, out)
    return out.reshape(B, S, D)
```

```python
import functools

import jax
import jax.numpy as jnp
from jax import lax
from jax.experimental import pallas as pl
from jax.experimental.pallas import tpu as pltpu
from jax.experimental.pallas import tpu_sc as plsc

F32 = jnp.float32
BF16 = jnp.bfloat16
HIGHEST = lax.Precision.HIGHEST

GLA_HEADS = 4
GATE_TAU = 16.0
GLA_CHUNK = 64
N_GROUPS = 8
EXPERTS_PER_GROUP = 8
N_EXPERTS = N_GROUPS * EXPERTS_PER_GROUP
NORM_EPS = 1e-6

LANES = 128
VMEM_LIMIT_BYTES = 56 * 1024 * 1024

IN_TILE = 512
IN_TILE_PLAIN = 1024
GLA_TILE = 1024
MIX_TILE = 1024
MIX_SUB = 1024
ROUTE_SUB = 256
MOE_BLOCK = 512
EXPERT_ROWS_STEP = 128
ROW_TILE = 1024
FINAL_PARTS = 4
ROW_PIECE = 256
U32 = jnp.uint32

R_ID1, R_ID2, R_W1, R_W2, R_RANK1, R_RANK2 = 0, 1, 2, 3, 4, 5


def _dot(a, b):
    return jnp.dot(a, b, preferred_element_type=F32)


def _round_bf16(x):
    xb = x.astype(BF16)
    return xb, xb.astype(F32)


def _pack_rows(xr):
    half = xr.shape[1] // 2
    out = []
    for p in range(half // ROW_PIECE):
        lo = xr[:, p * ROW_PIECE:(p + 1) * ROW_PIECE]
        hi = xr[:, half + p * ROW_PIECE:half + (p + 1) * ROW_PIECE]
        out.append((pltpu.bitcast(lo, U32) >> 16) | pltpu.bitcast(hi, U32))
    return out


def _unpack_rows(ref, rows=None):
    rows = ref.shape[1] if rows is None else rows
    words = [ref[p, 0:rows, :] for p in range(ref.shape[0])]
    lo = [pltpu.bitcast(w << 16, F32) for w in words]
    hi = [pltpu.bitcast(w & jnp.uint32(0xFFFF0000), F32) for w in words]
    return jnp.concatenate(lo + hi, axis=-1)


def _const_spec(shape):
    nd = len(shape)
    return pl.BlockSpec(shape, lambda *_: (0,) * nd, pipeline_mode=pl.Buffered(1))


def _params(sem):
    return pltpu.CompilerParams(dimension_semantics=sem, vmem_limit_bytes=VMEM_LIMIT_BYTES)


MOD_ROWS = 256


def _mod_kernel(c_ref, w_ref, b_ref, o_ref):
    c = c_ref[...]
    sc = c * jax.nn.sigmoid(c)
    s_hi, s_r = _round_bf16(sc)
    w = w_ref[0]
    w_hi, w_r = _round_bf16(w)
    s_lo = (sc - s_r).astype(BF16)
    part = _dot(s_hi, w_hi) + _dot(s_lo, w_hi) + _dot(s_hi, (w - w_r).astype(BF16))

    @pl.when(pl.program_id(1) == 0)
    def _():
        o_ref[0] = part + b_ref[0]

    @pl.when(pl.program_id(1) > 0)
    def _():
        o_ref[0] += part


def _modulation(c, mod_w, mod_b, layer):
    L, D, D6 = mod_w.shape
    B = c.shape[0]
    kb = MOD_ROWS
    return pl.pallas_call(
        _mod_kernel,
        grid=(1, D // kb),
        in_specs=[
            pl.BlockSpec((B, kb), lambda l, k: (0, k)),
            pl.BlockSpec((1, kb, D6), lambda l, k: (layer, k, 0)),
            pl.BlockSpec((1, 1, D6), lambda l, k: (layer, 0, 0)),
        ],
        out_specs=pl.BlockSpec((1, B, D6), lambda l, k: (0, 0, 0)),
        out_shape=jax.ShapeDtypeStruct((1, B, D6), F32),
        compiler_params=_params(("arbitrary", "arbitrary")),
        name="adaln_mod",
    )(c, mod_w, mod_b.reshape(L, 1, D6))


CONV_COLS = 256


def _moe_mix(route_ref, y0_ref, y1_ref):
    rec = route_ref[...]
    w1 = rec[:, R_W1:R_W1 + 1]
    w2 = rec[:, R_W2:R_W2 + 1]
    return w1 * _unpack_rows(y0_ref) + w2 * _unpack_rows(y1_ref)


def _inproj_kernel(*refs, seq, fuse_combine):
    if fuse_combine:
        (x_ref, route_ref, y0_ref, y1_ref, modp_ref, mod_ref, g_ref, cw_ref, wa_ref, wb_ref, wg_ref,
         xo_ref, za_ref, zb_ref, zg_ref, carry_ref) = refs
        x = x_ref[...] + modp_ref[0][5:6] * _moe_mix(route_ref, y0_ref, y1_ref)
        xo_ref[...] = x
    else:
        (x_ref, mod_ref, g_ref, cw_ref, wa_ref, wb_ref, wg_ref, za_ref, zb_ref, zg_ref, carry_ref) = refs
        x = x_ref[...]
    tm, d = x.shape

    @pl.when((pl.program_id(0) * tm) % seq == 0)
    def _():
        carry_ref[...] = jnp.zeros_like(carry_ref)

    ms = jnp.mean(x * x, axis=-1, keepdims=True)
    m = mod_ref[0]
    h = (x * lax.rsqrt(ms + NORM_EPS) * (g_ref[...] * (1.0 + m[1:2])) + m[0:1]).astype(BF16)
    for j in range(za_ref.shape[1] // d):
        za_ref[:, j * d:(j + 1) * d] = _dot(h, wa_ref[:, j * d:(j + 1) * d]).astype(BF16)
    zg_ref[...] = _dot(h, wg_ref[...])
    for j in range(1, zb_ref.shape[1] // d):
        zb_ref[:, j * d:(j + 1) * d] = _dot(h, wb_ref[:, (j + 2) * d:(j + 3) * d]).astype(BF16)

    w = CONV_COLS
    rowi = lax.broadcasted_iota(jnp.int32, (tm, w), 0)
    for j in range(d // w):
        cols = slice(j * w, (j + 1) * w)
        cb = _dot(h, wb_ref[:, j * w:(j + 1) * w])
        cc = _dot(h, wb_ref[:, d + j * w:d + (j + 1) * w])
        ch = _dot(h, wb_ref[:, 2 * d + j * w:2 * d + (j + 1) * w])
        u = cc * ch
        prev = carry_ref[:, cols]
        u1 = jnp.where(rowi == 0, prev[7:8], pltpu.roll(u, 1, 0))
        u2 = jnp.where(rowi == 0, prev[6:7], jnp.where(rowi == 1, prev[7:8], pltpu.roll(u, 2, 0)))
        carry_ref[:, cols] = u[tm - 8:tm]
        conv = cw_ref[0:1, cols] * u2 + cw_ref[1:2, cols] * u1 + cw_ref[2:3, cols] * u
        zb_ref[:, cols] = (cb * conv).astype(BF16)


def _in_projection(x, modl, norm_g, conv_w, wa, wb, wg, seq, combine=None):
    N, D = x.shape
    tm = IN_TILE_PLAIN if combine is None else IN_TILE
    nt = N // tm
    ca, cb = wa.shape[1], wb.shape[1] - 2 * D
    tok = lambda i: (i, 0)
    per_batch = lambda i: ((i * tm) // seq, 0, 0)
    in_specs = [pl.BlockSpec((tm, D), tok)]
    args = [x]
    out_specs, out_shape = [], []
    if combine is not None:
        route, yk, mod_prev = combine
        piece_blk = (yk.shape[0], tm, ROW_PIECE)
        in_specs += [
            pl.BlockSpec((tm, LANES), tok),
            pl.BlockSpec(piece_blk, lambda i: (0, i, 0)),
            pl.BlockSpec(piece_blk, lambda i: (0, nt + i, 0)),
            pl.BlockSpec((1, 6, D), per_batch),
        ]
        args += [route, yk, yk, mod_prev]
        out_specs.append(pl.BlockSpec((tm, D), tok))
        out_shape.append(jax.ShapeDtypeStruct((N, D), F32))
    in_specs += [
        pl.BlockSpec((1, 6, D), per_batch),
        _const_spec((1, D)),
        _const_spec(conv_w.shape),
        _const_spec((D, ca)),
        _const_spec(wb.shape),
        _const_spec((D, LANES)),
    ]
    args += [modl, norm_g, conv_w, wa, wb, wg]
    out_specs += [
        pl.BlockSpec((tm, ca), tok),
        pl.BlockSpec((tm, cb), tok),
        pl.BlockSpec((tm, LANES), tok),
    ]
    out_shape += [
        jax.ShapeDtypeStruct((N, ca), BF16),
        jax.ShapeDtypeStruct((N, cb), BF16),
        jax.ShapeDtypeStruct((N, LANES), F32),
    ]
    kern = functools.partial(_inproj_kernel, seq=seq, fuse_combine=combine is not None)
    return pl.pallas_call(
        kern,
        grid=(nt,),
        in_specs=in_specs,
        out_specs=out_specs,
        out_shape=out_shape,
        scratch_shapes=[pltpu.VMEM((8, D), F32)],
        compiler_params=_params(("arbitrary",)),
        name="in_projection",
    )(*args)


def _log_sigmoid(x):
    return jnp.minimum(x, 0.0) - jnp.log(1.0 + jnp.exp(-jnp.abs(x)))


def _gla_kernel(za_ref, zg_ref, gw2_ref, gb_ref, ng_ref, o_ref, st_ref, lg_ref, *, dk, dv, rank):
    heads = GLA_HEADS
    dkh, dvh = dk // heads, dv // heads
    c = GLA_CHUNK
    ts = za_ref.shape[0]

    @pl.when(pl.program_id(1) == 0)
    def _():
        st_ref[...] = jnp.zeros_like(st_ref)

    zg = zg_ref[...]
    zg_hi = zg.astype(BF16)
    zg_lo = (zg - zg_hi.astype(F32)).astype(BF16)
    lane = lax.broadcasted_iota(jnp.int32, zg.shape, 1)
    lhs = jnp.where(jnp.logical_and(lane >= rank, lane < 2 * rank), zg_lo, zg_hi)
    pre = _dot(lhs, gw2_ref[...]) + gb_ref[...]
    lg = _log_sigmoid(pre) * (1.0 / GATE_TAU)
    lg_hi = lg.astype(BF16)
    lg_ref[:, 0:dk] = lg_hi
    lg_ref[:, dk:2 * dk] = (lg - lg_hi.astype(F32)).astype(BF16)

    row = lax.broadcasted_iota(jnp.int32, (c, c), 0)
    col = lax.broadcasted_iota(jnp.int32, (c, c), 1)
    causal = row >= col
    tril = causal.astype(BF16)
    qscale = dkh ** -0.5

    for ci in range(ts // c):
        rows = slice(ci * c, (ci + 1) * c)
        b_two = _dot(tril, lg_ref[rows, :])
        b_all = b_two[:, 0:dk] + b_two[:, dk:2 * dk]
        for hd in range(heads):
            ks = slice(hd * dkh, (hd + 1) * dkh)
            q = za_ref[rows, hd * dkh:(hd + 1) * dkh].astype(F32) * qscale
            k = za_ref[rows, dk + hd * dkh:dk + (hd + 1) * dkh].astype(F32)
            v = za_ref[rows, 2 * dk + hd * dvh:2 * dk + (hd + 1) * dvh]
            r = za_ref[rows, 2 * dk + dv + hd * dvh:2 * dk + dv + (hd + 1) * dvh].astype(F32)
            b = b_all[:, ks]
            b_last = b[c - 1:c, :]
            q_t = (q * jnp.exp(b)).astype(BF16)
            k_t = (k * jnp.exp(-b)).astype(BF16)
            k_s = (k * jnp.exp(b_last - b)).astype(BF16)
            decay = jnp.exp(b_last)
            attn = lax.dot_general(q_t, k_t, (((1,), (1,)), ((), ())), preferred_element_type=F32)
            attn = jnp.where(causal, attn, 0.0).astype(BF16)
            st = st_ref[hd]
            o = _dot(attn, v) + lax.dot_general(
                q_t, st.astype(BF16), (((1,), (1,)), ((), ())), preferred_element_type=F32)
            upd = lax.dot_general(v, k_s, (((0,), (0,)), ((), ())), preferred_element_type=F32)
            st_ref[hd] = st * decay + upd
            ms = jnp.mean(o * o, axis=-1, keepdims=True)
            on = o * lax.rsqrt(ms + NORM_EPS) * ng_ref[:, hd * dvh:(hd + 1) * dvh]
            o_ref[rows, hd * dvh:(hd + 1) * dvh] = (on * (r * jax.nn.sigmoid(r))).astype(BF16)


def _gla(za, zg, gw2, gb, ng, batch, seq, dk, dv, rank):
    N = za.shape[0]
    ts = GLA_TILE
    ns = seq // ts
    heads = GLA_HEADS
    kern = functools.partial(_gla_kernel, dk=dk, dv=dv, rank=rank)
    return pl.pallas_call(
        kern,
        grid=(batch, ns),
        in_specs=[
            pl.BlockSpec((ts, za.shape[1]), lambda b, s: (b * ns + s, 0)),
            pl.BlockSpec((ts, LANES), lambda b, s: (b * ns + s, 0)),
            _const_spec((LANES, dk)),
            _const_spec((1, dk)),
            _const_spec((1, dv)),
        ],
        out_specs=pl.BlockSpec((ts, dv), lambda b, s: (b * ns + s, 0)),
        out_shape=jax.ShapeDtypeStruct((N, dv), BF16),
        scratch_shapes=[
            pltpu.VMEM((heads, dv // heads, dk // heads), F32),
            pltpu.VMEM((ts, 2 * dk), BF16),
        ],
        compiler_params=_params(("arbitrary", "arbitrary")),
        name="gla",
    )(za, zg, gw2, gb, ng)


def _mixout_kernel(og_ref, zb_ref, x_ref, mod_ref, wga_ref, wco_ref, wo_ref, n2_ref,
                   wr_ref, br_ref, x1_ref, h2_ref, logit_ref):
    tm, d = x_ref.shape
    m = mod_ref[0]
    sub = MIX_SUB
    gain2 = n2_ref[...] * (1.0 + m[4:5])
    for r0 in range(0, tm, sub):
        rs = slice(r0, r0 + sub)
        ga = zb_ref[rs, d:2 * d].astype(F32)
        gc = zb_ref[rs, 2 * d:3 * d].astype(F32)
        y_conv = _dot(zb_ref[rs, 0:d], wco_ref[...])
        y_gla = _dot(og_ref[rs, :], wga_ref[...])
        y = jax.nn.sigmoid(ga) * y_gla + jax.nn.sigmoid(gc) * y_conv
        y = _dot(y.astype(BF16), wo_ref[...])
        x1 = x_ref[rs, :] + m[2:3] * y
        x1_ref[rs, :] = x1

        ms = jnp.mean(x1 * x1, axis=-1, keepdims=True)
        h2 = x1 * lax.rsqrt(ms + NORM_EPS) * gain2 + m[3:4]
        h_hi, h_r = _round_bf16(h2)
        for p, words in enumerate(_pack_rows(h_r)):
            h2_ref[p, rs, :] = words

        h_lo = (h2 - h_r).astype(BF16)
        two = _dot(h_hi, wr_ref[...])
        logits = two[:, 0:LANES] + two[:, LANES:2 * LANES] + _dot(h_lo, wr_ref[:, 0:LANES]) + br_ref[...]
        logit_ref[rs, :] = logits


def _mixer_out(og, zb, x, modl, wga, wco, wo, n2g, wr, br, batch, seq):
    N, D = x.shape
    tm = MIX_TILE
    ns = seq // tm
    tok = lambda b, s: (b * ns + s, 0)
    return pl.pallas_call(
        _mixout_kernel,
        grid=(batch, ns),
        in_specs=[
            pl.BlockSpec((tm, D), tok),
            pl.BlockSpec((tm, zb.shape[1]), tok),
            pl.BlockSpec((tm, D), tok),
            pl.BlockSpec((1, 6, D), lambda b, s: (b, 0, 0)),
            _const_spec((D, D)),
            _const_spec((D, D)),
            _const_spec((D, D)),
            _const_spec((1, D)),
            _const_spec((D, 2 * LANES)),
            _const_spec((1, LANES)),
        ],
        out_specs=[
            pl.BlockSpec((tm, D), tok),
            pl.BlockSpec((D // 2 // ROW_PIECE, tm, ROW_PIECE), lambda b, s: (0, b * ns + s, 0)),
            pl.BlockSpec((tm, LANES), tok),
        ],
        out_shape=[
            jax.ShapeDtypeStruct((N, D), F32),
            jax.ShapeDtypeStruct((D // 2 // ROW_PIECE, N, ROW_PIECE), U32),
            jax.ShapeDtypeStruct((N, LANES), F32),
        ],
        compiler_params=_params(("arbitrary", "arbitrary")),
        name="mixer_out",
    )(og, zb, x, modl, wga, wco, wo, n2g, wr, br)


def _route_kernel(logit_ref, routet_ref, route_ref, cnt_ref, run_ref):
    tm = logit_ref.shape[0]
    sub = ROUTE_SUB
    rows8 = EXPERTS_PER_GROUP

    @pl.when(pl.program_id(0) == 0)
    def _():
        run_ref[...] = jnp.zeros_like(run_ref)

    sub8 = lax.broadcasted_iota(jnp.int32, (rows8, sub), 0)
    erow = lax.broadcasted_iota(jnp.int32, (N_EXPERTS, sub), 0)
    tr = lax.broadcasted_iota(jnp.int32, (sub, sub), 0)
    tc = lax.broadcasted_iota(jnp.int32, (sub, sub), 1)
    earlier = (tr < tc).astype(BF16)
    ones = jnp.ones((sub, LANES), BF16)
    neg = -jnp.inf
    run = run_ref[...]
    for r0 in range(0, tm, sub):
        lt = logit_ref[r0:r0 + sub, :].T
        gl = lt[0:N_GROUPS, :]
        gmax = jnp.max(gl, axis=0, keepdims=True)
        gsum = jnp.sum(jnp.exp(gl - gmax), axis=0, keepdims=True)
        g_w = 1.0 / gsum
        g_idx = jnp.min(jnp.where(gl == gmax, sub8, N_GROUPS), axis=0, keepdims=True)
        el = lt[N_GROUPS:N_GROUPS + rows8, :]
        for g in range(1, N_GROUPS):
            el = jnp.where(g_idx == g, lt[N_GROUPS + g * rows8:N_GROUPS + (g + 1) * rows8, :], el)
        e1 = jnp.max(el, axis=0, keepdims=True)
        i1 = jnp.min(jnp.where(el == e1, sub8, rows8), axis=0, keepdims=True)
        el2 = jnp.where(sub8 == i1, neg, el)
        e2 = jnp.max(el2, axis=0, keepdims=True)
        i2 = jnp.min(jnp.where(el2 == e2, sub8, rows8), axis=0, keepdims=True)
        ratio = jnp.exp(e2 - e1)
        w1 = g_w / (1.0 + ratio)
        w2 = g_w * ratio / (1.0 + ratio)
        id1 = g_idx * rows8 + i1
        id2 = g_idx * rows8 + i2

        oh1 = erow == id1
        oh2 = erow == id2
        oh1b = jnp.where(oh1, 1.0, 0.0).astype(BF16)
        oh2b = jnp.where(oh2, 1.0, 0.0).astype(BF16)
        tot1 = _dot(oh1b, ones)
        tot2 = _dot(oh2b, ones)
        base1 = jnp.concatenate([run] * (sub // LANES), axis=1)
        base2 = jnp.concatenate([run + tot1] * (sub // LANES), axis=1)
        c1 = _dot(oh1b, earlier) + base1
        c2 = _dot(oh2b, earlier) + base2
        rank1 = jnp.sum(jnp.where(oh1, c1, 0.0), axis=0, keepdims=True)
        rank2 = jnp.sum(jnp.where(oh2, c2, 0.0), axis=0, keepdims=True)
        run = run + tot1 + tot2

        rec = jnp.where(sub8 == R_ID1, id1.astype(F32), 0.0)
        rec = jnp.where(sub8 == R_ID2, id2.astype(F32), rec)
        rec = jnp.where(sub8 == R_W1, w1, rec)
        rec = jnp.where(sub8 == R_W2, w2, rec)
        rec = jnp.where(sub8 == R_RANK1, rank1, rec)
        rec = jnp.where(sub8 == R_RANK2, rank2, rec)
        routet_ref[:, r0:r0 + sub] = rec
        rec_full = jnp.concatenate([rec, jnp.zeros((LANES - rows8, sub), F32)], axis=0)
        route_ref[r0:r0 + sub, :] = rec_full.T
    run_ref[...] = run
    cnt_ref[...] = run


def _routing(logits):
    N = logits.shape[0]
    tm = 2048 if N % 2048 == 0 else ROUTE_SUB
    return pl.pallas_call(
        _route_kernel,
        grid=(N // tm,),
        in_specs=[pl.BlockSpec((tm, LANES), lambda i: (i, 0))],
        out_specs=[
            pl.BlockSpec((8, tm), lambda i: (0, i)),
            pl.BlockSpec((tm, LANES), lambda i: (i, 0)),
            pl.BlockSpec((N_EXPERTS, LANES), lambda i: (0, 0)),
        ],
        out_shape=[
            jax.ShapeDtypeStruct((8, N), F32),
            jax.ShapeDtypeStruct((N, LANES), F32),
            jax.ShapeDtypeStruct((N_EXPERTS, LANES), F32),
        ],
        scratch_shapes=[pltpu.VMEM((N_EXPERTS, LANES), F32)],
        compiler_params=_params(("arbitrary",)),
        name="moe_route",
    )(logits)


SC_WINDOW = 128


def _sc_mesh():
    return plsc.VectorSubcoreMesh(core_axis_name="core", subcore_axis_name="subcore")


def _piece_index(rows, pieces, n_rows):
    return (jnp.arange(pieces, dtype=jnp.int32)[:, None] * n_rows + rows[None, :]).reshape(1, -1)


def _sc_dispatch(h2, dests, n_rows):
    pieces, N, pc = h2.shape
    w = SC_WINDOW

    @functools.partial(pl.kernel, out_type=jax.ShapeDtypeStruct((pieces * n_rows, pc), h2.dtype),
                       mesh=_sc_mesh(), scratch_types=[], name="moe_dispatch_sc")
    def run(x_hbm, *refs):
        o_hbm = refs[-1]

        def body(x_vmem, *idx_vmem):
            for i_vmem in idx_vmem:
                pltpu.sync_copy(x_vmem, o_hbm.at[i_vmem.at[0]])

        pltpu.emit_pipeline(
            body,
            grid=(pieces * N // w,),
            in_specs=[pl.BlockSpec((w, pc), lambda i: (i, 0))]
            + [pl.BlockSpec((1, w), lambda i: (0, i))] * len(dests),
            out_specs=[],
            core_axis_name=("core", "subcore"),
            dimension_semantics=(pltpu.PARALLEL,),
        )(x_hbm, *refs[:-1])

    xs = run(h2.reshape(pieces * N, pc), *[_piece_index(d, pieces, n_rows) for d in dests])
    return xs.reshape(pieces, n_rows, pc)


def _sc_return(yb, dest):
    M = dest.shape[0]
    pieces, P, pc = yb.shape
    w = SC_WINDOW

    @functools.partial(pl.kernel, out_type=jax.ShapeDtypeStruct((pieces * M, pc), yb.dtype),
                       mesh=_sc_mesh(), scratch_types=[], name="moe_return_sc")
    def run(y_hbm, i_hbm, o_hbm):
        def body(i_vmem, o_vmem):
            pltpu.sync_copy(y_hbm.at[i_vmem.at[0]], o_vmem)

        pltpu.emit_pipeline(
            body,
            grid=(pieces * M // w,),
            in_specs=[pl.BlockSpec((1, w), lambda i: (0, i))],
            out_specs=[pl.BlockSpec((w, pc), lambda i: (i, 0))],
            core_axis_name=("core", "subcore"),
            dimension_semantics=(pltpu.PARALLEL,),
        )(i_hbm, o_hbm)

    return run(yb.reshape(pieces * P, pc), _piece_index(dest, pieces, P)).reshape(pieces, M, pc)


def _expert_kernel(tab_ref, xs_ref, w1_ref, w3_ref, w2_ref, y_ref, wf1, wf3, wf2, sem, *, layer):
    j = pl.program_id(0)
    jp = jnp.maximum(j - 1, 0)
    live = j < tab_ref[I_COUNT, 0]
    expert = tab_ref[I_EXPERT, j]
    new_expert = jnp.logical_or(j == 0, expert != tab_ref[I_EXPERT, jp])
    streams = ((w1_ref, wf1), (w3_ref, wf3), (w2_ref, wf2))

    def weight_copies(expert, s):
        return [pltpu.make_async_copy(w.at[layer, expert], buf.at[s], sem.at[s, k])
                for k, (w, buf) in enumerate(streams)]

    @pl.when(j == 0)
    def _():
        for cp in weight_copies(expert, 0):
            cp.start()

    @pl.when(jnp.logical_and(new_expert, live))
    def _():
        s = tab_ref[I_SLOT, j]
        for cp in weight_copies(expert, s):
            cp.wait()

        @pl.when(tab_ref[I_NEXT, j] >= 0)
        def _():
            for cp in weight_copies(tab_ref[I_NEXT, j], 1 - s):
                cp.start()

    bm = xs_ref.shape[1]
    for m in range(EXPERT_ROWS_STEP, bm + 1, EXPERT_ROWS_STEP):
        @pl.when(jnp.logical_and(live, tab_ref[I_ROWS, j] == m))
        def _(m=m):
            xb = _unpack_rows(xs_ref, m).astype(BF16)
            s = tab_ref[I_SLOT, j]
            h1 = _dot(xb, wf1[s].astype(BF16))
            h3 = _dot(xb, wf3[s].astype(BF16))
            a = (h1 * jax.nn.sigmoid(h1) * h3).astype(BF16)
            for p, words in enumerate(_pack_rows(_round_bf16(_dot(a, wf2[s].astype(BF16)))[1])):
                y_ref[p, 0:m, :] = words
                if m < bm:
                    y_ref[p, m:bm, :] = jnp.zeros((bm - m, words.shape[1]), U32)


def _experts(xs, items, w1, w3, w2, layer):
    pieces, P, pc = xs.shape
    _, E, D, DE = w1.shape
    bm = MOE_BLOCK
    n_items_max = P // bm
    rows = lambda j, tab: (0, tab[I_BLOCK, j], 0)
    return pl.pallas_call(
        functools.partial(_expert_kernel, layer=layer),
        grid_spec=pltpu.PrefetchScalarGridSpec(
            num_scalar_prefetch=1,
            grid=(n_items_max,),
            in_specs=[
                pl.BlockSpec((pieces, bm, pc), rows),
                pl.BlockSpec(memory_space=pl.ANY),
                pl.BlockSpec(memory_space=pl.ANY),
                pl.BlockSpec(memory_space=pl.ANY),
            ],
            out_specs=pl.BlockSpec((pieces, bm, pc), rows),
            scratch_shapes=[
                pltpu.VMEM((2, D, DE), F32),
                pltpu.VMEM((2, D, DE), F32),
                pltpu.VMEM((2, DE, D), F32),
                pltpu.SemaphoreType.DMA((2, 3)),
            ],
        ),
        out_shape=jax.ShapeDtypeStruct((pieces, P, pc), U32),
        compiler_params=_params(("arbitrary",)),
        name="moe_experts",
    )(items, xs, w1, w3, w2)


def _combine_kernel(x1_ref, route_ref, mod_ref, fg_ref, y0_ref, y1_ref, *rest):
    o_ref = rest[-1]
    x2 = x1_ref[...] + mod_ref[0][5:6] * _moe_mix(route_ref, y0_ref, y1_ref)
    ms = jnp.mean(x2 * x2, axis=-1, keepdims=True)
    o_ref[...] = x2 * lax.rsqrt(ms + NORM_EPS) * fg_ref[...]


def _final_combine(x1, route, modl, final_g, yk, seq, first_tile, prev_out):
    N, D = x1.shape
    tm = ROW_TILE
    nt = yk.shape[1] // 2 // tm
    tok = lambda i: (first_tile + i, 0)
    in_specs = [
        pl.BlockSpec((tm, D), tok),
        pl.BlockSpec((tm, LANES), tok),
        pl.BlockSpec((1, 6, D), lambda i: (((first_tile + i) * tm) // seq, 0, 0)),
        pl.BlockSpec((1, D), lambda i: (0, 0)),
        pl.BlockSpec((yk.shape[0], tm, ROW_PIECE), lambda i: (0, i, 0)),
        pl.BlockSpec((yk.shape[0], tm, ROW_PIECE), lambda i: (0, nt + i, 0)),
    ]
    args = [x1, route, modl, final_g, yk, yk]
    aliases = {}
    if prev_out is not None:
        in_specs.append(pl.BlockSpec(memory_space=pl.ANY))
        args.append(prev_out)
        aliases = {len(args) - 1: 0}
    return pl.pallas_call(
        _combine_kernel,
        grid=(nt,),
        in_specs=in_specs,
        out_specs=pl.BlockSpec((tm, D), tok),
        out_shape=jax.ShapeDtypeStruct((N, D), F32),
        input_output_aliases=aliases,
        compiler_params=_params(("arbitrary",)),
        name="moe_combine",
    )(*args)


ITEM_LANES = 256
I_BLOCK, I_EXPERT, I_NEXT, I_SLOT, I_COUNT, I_ROWS = 0, 1, 2, 3, 4, 5


def _dest_kernel(routet_ref, cnt_ref, d1_ref, d2_ref, d3_ref, items_ref):
    tm = routet_ref.shape[1]
    ne = N_EXPERTS
    bm = float(MOE_BLOCK)
    cnt = cnt_ref[...]
    r = lax.broadcasted_iota(jnp.int32, (ne, ne), 0)
    c = lax.broadcasted_iota(jnp.int32, (ne, ne), 1)
    lower = (r >= c).astype(F32)
    cumsum = lambda a: jnp.dot(lower, a, precision=HIGHEST, preferred_element_type=F32)
    padded = jnp.floor((cnt + (bm - 1.0)) * (1.0 / bm)) * bm
    pend = cumsum(padded)
    pstart = pend - padded
    npad = padded - cnt
    pad_end = cumsum(npad)
    pad_start = pad_end - npad

    tile = lambda a, n: jnp.concatenate([a] * (n // LANES), axis=1)
    expert = lax.broadcasted_iota(jnp.int32, (ne, tm), 0).astype(F32)
    rec = routet_ref[...]
    pstart_t = tile(pstart, tm)

    def sorted_row(row_id, row_rank):
        sel = expert == rec[row_id:row_id + 1, :]
        return jnp.sum(jnp.where(sel, pstart_t, 0.0), axis=0, keepdims=True) + rec[row_rank:row_rank + 1, :]

    dest1 = sorted_row(R_ID1, R_RANK1)
    dest2 = sorted_row(R_ID2, R_RANK2)
    t = (lax.broadcasted_iota(jnp.int32, (1, tm), 1) + pl.program_id(0) * tm).astype(F32)
    ps_t, pe_t = tile(pad_start, tm), tile(pad_end, tm)
    in_e = jnp.logical_and(ps_t <= t, t < pe_t)
    pad_row = jnp.sum(jnp.where(in_e, tile(pstart + cnt, tm) + (t - ps_t), 0.0), axis=0, keepdims=True)
    dest3 = jnp.where(t < pe_t[ne - 1:ne, :], pad_row, dest1)
    d1_ref[0] = dest1.astype(jnp.int32)
    d2_ref[0] = dest2.astype(jnp.int32)
    d3_ref[0] = dest3.astype(jnp.int32)

    @pl.when(pl.program_id(0) == 0)
    def _():
        nl = ITEM_LANES
        pend_i = tile(pend, nl)
        n_items = pend_i[ne - 1:ne, :] * (1.0 / bm)
        blk = jnp.minimum(lax.broadcasted_iota(jnp.int32, (1, nl), 1).astype(F32), n_items - 1.0)
        erow = lax.broadcasted_iota(jnp.int32, (ne, nl), 0).astype(F32)
        ie = jnp.sum(jnp.where(pend_i <= blk * bm, 1.0, 0.0), axis=0, keepdims=True)
        nonempty = tile(cnt, nl) > 0.0
        order = jnp.sum(jnp.where(jnp.logical_and(nonempty, erow < ie), 1.0, 0.0), axis=0, keepdims=True)
        nxt = jnp.min(jnp.where(jnp.logical_and(nonempty, erow > ie), erow, float(ne)), axis=0, keepdims=True)
        nxt = jnp.where(nxt == float(ne), -1.0, nxt)
        slot = order - 2.0 * jnp.floor(order * 0.5)
        own_end = jnp.sum(jnp.where(erow == ie, tile(pstart + cnt, nl), 0.0), axis=0, keepdims=True)
        step = float(EXPERT_ROWS_STEP)
        rows = jnp.minimum(jnp.ceil((own_end - blk * bm) * (1.0 / step)) * step, bm)
        row = lax.broadcasted_iota(jnp.int32, (8, nl), 0)
        tab = jnp.where(row == I_BLOCK, blk, 0.0)
        tab = jnp.where(row == I_EXPERT, ie, tab)
        tab = jnp.where(row == I_NEXT, nxt, tab)
        tab = jnp.where(row == I_SLOT, slot, tab)
        tab = jnp.where(row == I_COUNT, n_items, tab)
        tab = jnp.where(row == I_ROWS, rows, tab)
        items_ref[...] = tab.astype(jnp.int32)


def _destinations(route_t, cnt, n_blocks):
    N = route_t.shape[1]
    tm = 2048 if N % 2048 == 0 else ROUTE_SUB
    out = jax.ShapeDtypeStruct((N // tm, 1, tm), jnp.int32)
    d1, d2, d3, items = pl.pallas_call(
        _dest_kernel,
        grid=(N // tm,),
        in_specs=[pl.BlockSpec((8, tm), lambda i: (0, i)),
                  pl.BlockSpec((N_EXPERTS, LANES), lambda i: (0, 0))],
        out_specs=[pl.BlockSpec((1, 1, tm), lambda i: (i, 0, 0))] * 3
        + [pl.BlockSpec((8, ITEM_LANES), lambda i: (0, 0))],
        out_shape=[out, out, out, jax.ShapeDtypeStruct((8, ITEM_LANES), jnp.int32)],
        compiler_params=_params(("arbitrary",)),
        name="moe_dest",
    )(route_t, cnt)
    return d1.reshape(N), d2.reshape(N), d3.reshape(N), items


def kernel(x, c, mod_w, mod_b, norm1_g, w_in, gate_w2, gate_b, gla_norm_g, conv_w, w_gla_out,
           w_conv_out, w_out, norm2_g, router_group_w, router_group_b, router_expert_w,
           router_expert_b, expert_w1, expert_w3, expert_w2, final_norm_g):
    B, S, D = x.shape
    L = mod_w.shape[0]
    N = B * S
    dk = gate_w2.shape[2]
    rank = gate_w2.shape[1]
    dv = gla_norm_g.shape[1]
    n_slots = N * 2
    n_blocks = n_slots // MOE_BLOCK + N_EXPERTS
    assert S % GLA_TILE == 0 and S % MIX_TILE == 0 and S % IN_TILE == 0 and S % IN_TILE_PLAIN == 0
    assert N % (FINAL_PARTS * ROW_TILE) == 0
    assert n_slots % MOE_BLOCK == 0 and MOE_BLOCK & (MOE_BLOCK - 1) == 0 and n_blocks <= ITEM_LANES
    assert N_EXPERTS * (MOE_BLOCK - 1) <= N
    assert N_GROUPS + N_EXPERTS <= LANES and EXPERTS_PER_GROUP == 8 and 3 * rank <= LANES and D % (2 * ROW_PIECE) == 0

    mod = [_modulation(c, mod_w, mod_b, l)[0] for l in range(L)]
    xf = x.reshape(N, D)
    o_gd = 2 * dk + dv
    o_r = o_gd + rank
    o_b = o_r + dv
    for l in range(L):
        modl = mod[l].reshape(B, 6, D)
        wl = w_in[l]
        wa = jnp.concatenate([wl[:, :o_gd], wl[:, o_r:o_b]], axis=1).astype(BF16)
        wb = wl[:, o_b:].astype(BF16)
        wgd = wl[:, o_gd:o_r]
        wg = jnp.pad(jnp.concatenate([wgd, wgd, wgd], axis=1), ((0, 0), (0, LANES - 3 * rank))).astype(BF16)
        g_hi = gate_w2[l].astype(BF16)
        g_lo = (gate_w2[l] - g_hi.astype(F32)).astype(BF16)
        gw2 = jnp.pad(jnp.concatenate([g_hi, g_hi, g_lo], axis=0), ((0, LANES - 3 * rank), (0, 0)))
        if l == 0:
            za, zb, zg = _in_projection(xf, modl, norm1_g[l].reshape(1, D), conv_w[l], wa, wb, wg, S)
        else:
            xf, za, zb, zg = _in_projection(x1, modl, norm1_g[l].reshape(1, D), conv_w[l], wa, wb, wg, S,
                                            combine=(route, yk, mod[l - 1].reshape(B, 6, D)))
        og = _gla(za, zg, gw2, gate_b[l].reshape(1, dk), gla_norm_g[l].reshape(1, dv), B, S, dk, dv, rank)
        wr = jnp.pad(jnp.concatenate([router_group_w[l], router_expert_w[l]], axis=1),
                     ((0, 0), (0, LANES - N_GROUPS - N_EXPERTS)))
        wr_hi = wr.astype(BF16)
        wr = jnp.concatenate([wr_hi, (wr - wr_hi.astype(F32)).astype(BF16)], axis=1)
        br = jnp.pad(jnp.concatenate([router_group_b[l], router_expert_b[l]]),
                     (0, LANES - N_GROUPS - N_EXPERTS)).reshape(1, LANES)
        x1, h2, logits = _mixer_out(
            og, zb, xf, modl, w_gla_out[l].astype(BF16), w_conv_out[l].astype(BF16),
            w_out[l].astype(BF16), norm2_g[l].reshape(1, D), wr, br, B, S)
        route_t, route, cnt = _routing(logits)
        dest1, dest2, dest3, items = _destinations(route_t, cnt, n_blocks)
        xs = _sc_dispatch(h2, (dest1, dest2, dest3), n_blocks * MOE_BLOCK)
        yb = _experts(xs, items, expert_w1, expert_w3, expert_w2, l)
        if l < L - 1:
            yk = _sc_return(yb, jnp.concatenate([dest1, dest2]))
    out = None
    part = N // FINAL_PARTS
    for p in range(FINAL_PARTS):
        tok = slice(p * part, (p + 1) * part)
        yk = _sc_return(yb, jnp.concatenate([dest1[tok], dest2[tok]]))
        out = _final_combine(x1, route, modl, final_norm_g.reshape(1, D), yk, S,
                             p * part // ROW_TILE, out)
    return out.reshape(B, S, D)
```

```python
import functools

import jax
import jax.numpy as jnp
from jax import lax
from jax.experimental import pallas as pl
from jax.experimental.pallas import tpu as pltpu
from jax.experimental.pallas import tpu_sc as plsc

F32 = jnp.float32
BF16 = jnp.bfloat16
HIGHEST = lax.Precision.HIGHEST

GLA_HEADS = 4
GATE_TAU = 16.0
GLA_CHUNK = 64
N_GROUPS = 8
EXPERTS_PER_GROUP = 8
N_EXPERTS = N_GROUPS * EXPERTS_PER_GROUP
NORM_EPS = 1e-6

LANES = 128
VMEM_LIMIT_BYTES = 56 * 1024 * 1024

IN_TILE = 512
IN_TILE_PLAIN = 1024
GLA_TILE = 1024
MIX_TILE = 1024
MIX_SUB = 1024
ROUTE_SUB = 256
MOE_BLOCK = 512
EXPERT_ROWS_STEP = 128
ROW_TILE = 1024
FINAL_PARTS = 4
ROW_PIECE = 256
U32 = jnp.uint32

R_ID1, R_ID2, R_W1, R_W2, R_RANK1, R_RANK2 = 0, 1, 2, 3, 4, 5


def _dot(a, b):
    return jnp.dot(a, b, preferred_element_type=F32)


def _round_bf16(x):
    xb = x.astype(BF16)
    return xb, xb.astype(F32)


def _pack_rows(xr):
    half = xr.shape[1] // 2
    out = []
    for p in range(half // ROW_PIECE):
        lo = xr[:, p * ROW_PIECE:(p + 1) * ROW_PIECE]
        hi = xr[:, half + p * ROW_PIECE:half + (p + 1) * ROW_PIECE]
        out.append((pltpu.bitcast(lo, U32) >> 16) | pltpu.bitcast(hi, U32))
    return out


def _unpack_rows(ref, rows=None):
    rows = ref.shape[1] if rows is None else rows
    words = [ref[p, 0:rows, :] for p in range(ref.shape[0])]
    lo = [pltpu.bitcast(w << 16, F32) for w in words]
    hi = [pltpu.bitcast(w & jnp.uint32(0xFFFF0000), F32) for w in words]
    return jnp.concatenate(lo + hi, axis=-1)


def _const_spec(shape):
    nd = len(shape)
    return pl.BlockSpec(shape, lambda *_: (0,) * nd, pipeline_mode=pl.Buffered(1))


def _params(sem):
    return pltpu.CompilerParams(dimension_semantics=sem, vmem_limit_bytes=VMEM_LIMIT_BYTES)


MOD_ROWS = 256


def _mod_kernel(c_ref, w_ref, b_ref, o_ref):
    c = c_ref[...]
    sc = c * jax.nn.sigmoid(c)
    s_hi, s_r = _round_bf16(sc)
    w = w_ref[0]
    w_hi, w_r = _round_bf16(w)
    s_lo = (sc - s_r).astype(BF16)
    part = _dot(s_hi, w_hi) + _dot(s_lo, w_hi) + _dot(s_hi, (w - w_r).astype(BF16))

    @pl.when(pl.program_id(1) == 0)
    def _():
        o_ref[0] = part + b_ref[0]

    @pl.when(pl.program_id(1) > 0)
    def _():
        o_ref[0] += part


def _modulation(c, mod_w, mod_b, layer):
    L, D, D6 = mod_w.shape
    B = c.shape[0]
    kb = MOD_ROWS
    return pl.pallas_call(
        _mod_kernel,
        grid=(1, D // kb),
        in_specs=[
            pl.BlockSpec((B, kb), lambda l, k: (0, k)),
            pl.BlockSpec((1, kb, D6), lambda l, k: (layer, k, 0)),
            pl.BlockSpec((1, 1, D6), lambda l, k: (layer, 0, 0)),
        ],
        out_specs=pl.BlockSpec((1, B, D6), lambda l, k: (0, 0, 0)),
        out_shape=jax.ShapeDtypeStruct((1, B, D6), F32),
        compiler_params=_params(("arbitrary", "arbitrary")),
        name="adaln_mod",
    )(c, mod_w, mod_b.reshape(L, 1, D6))


CONV_COLS = 256


def _moe_mix(route_ref, y0_ref, y1_ref):
    rec = route_ref[...]
    w1 = rec[:, R_W1:R_W1 + 1]
    w2 = rec[:, R_W2:R_W2 + 1]
    return w1 * _unpack_rows(y0_ref) + w2 * _unpack_rows(y1_ref)


def _inproj_kernel(*refs, seq, fuse_combine):
    if fuse_combine:
        (x_ref, route_ref, y0_ref, y1_ref, modp_ref, mod_ref, g_ref, cw_ref, wa_ref, wb_ref, wg_ref,
         xo_ref, za_ref, zb_ref, zg_ref, carry_ref) = refs
        x = x_ref[...] + modp_ref[0][5:6] * _moe_mix(route_ref, y0_ref, y1_ref)
        xo_ref[...] = x
    else:
        (x_ref, mod_ref, g_ref, cw_ref, wa_ref, wb_ref, wg_ref, za_ref, zb_ref, zg_ref, carry_ref) = refs
        x = x_ref[...]
    tm, d = x.shape

    @pl.when((pl.program_id(0) * tm) % seq == 0)
    def _():
        carry_ref[...] = jnp.zeros_like(carry_ref)

    ms = jnp.mean(x * x, axis=-1, keepdims=True)
    m = mod_ref[0]
    h = (x * lax.rsqrt(ms + NORM_EPS) * (g_ref[...] * (1.0 + m[1:2])) + m[0:1]).astype(BF16)
    for j in range(za_ref.shape[1] // d):
        za_ref[:, j * d:(j + 1) * d] = _dot(h, wa_ref[:, j * d:(j + 1) * d]).astype(BF16)
    zg_ref[...] = _dot(h, wg_ref[...])
    for j in range(1, zb_ref.shape[1] // d):
        zb_ref[:, j * d:(j + 1) * d] = _dot(h, wb_ref[:, (j + 2) * d:(j + 3) * d]).astype(BF16)

    w = CONV_COLS
    rowi = lax.broadcasted_iota(jnp.int32, (tm, w), 0)
    for j in range(d // w):
        cols = slice(j * w, (j + 1) * w)
        cb = _dot(h, wb_ref[:, j * w:(j + 1) * w])
        cc = _dot(h, wb_ref[:, d + j * w:d + (j + 1) * w])
        ch = _dot(h, wb_ref[:, 2 * d + j * w:2 * d + (j + 1) * w])
        u = cc * ch
        prev = carry_ref[:, cols]
        u1 = jnp.where(rowi == 0, prev[7:8], pltpu.roll(u, 1, 0))
        u2 = jnp.where(rowi == 0, prev[6:7], jnp.where(rowi == 1, prev[7:8], pltpu.roll(u, 2, 0)))
        carry_ref[:, cols] = u[tm - 8:tm]
        conv = cw_ref[0:1, cols] * u2 + cw_ref[1:2, cols] * u1 + cw_ref[2:3, cols] * u
        zb_ref[:, cols] = (cb * conv).astype(BF16)


def _in_projection(x, modl, norm_g, conv_w, wa, wb, wg, seq, combine=None):
    N, D = x.shape
    tm = IN_TILE_PLAIN if combine is None else IN_TILE
    nt = N // tm
    ca, cb = wa.shape[1], wb.shape[1] - 2 * D
    tok = lambda i: (i, 0)
    per_batch = lambda i: ((i * tm) // seq, 0, 0)
    in_specs = [pl.BlockSpec((tm, D), tok)]
    args = [x]
    out_specs, out_shape = [], []
    if combine is not None:
        route, yk, mod_prev = combine
        piece_blk = (yk.shape[0], tm, ROW_PIECE)
        in_specs += [
            pl.BlockSpec((tm, LANES), tok),
            pl.BlockSpec(piece_blk, lambda i: (0, i, 0)),
            pl.BlockSpec(piece_blk, lambda i: (0, nt + i, 0)),
            pl.BlockSpec((1, 6, D), per_batch),
        ]
        args += [route, yk, yk, mod_prev]
        out_specs.append(pl.BlockSpec((tm, D), tok))
        out_shape.append(jax.ShapeDtypeStruct((N, D), F32))
    in_specs += [
        pl.BlockSpec((1, 6, D), per_batch),
        _const_spec((1, D)),
        _const_spec(conv_w.shape),
        _const_spec((D, ca)),
        _const_spec(wb.shape),
        _const_spec((D, LANES)),
    ]
    args += [modl, norm_g, conv_w, wa, wb, wg]
    out_specs += [
        pl.BlockSpec((tm, ca), tok),
        pl.BlockSpec((tm, cb), tok),
        pl.BlockSpec((tm, LANES), tok),
    ]
    out_shape += [
        jax.ShapeDtypeStruct((N, ca), BF16),
        jax.ShapeDtypeStruct((N, cb), BF16),
        jax.ShapeDtypeStruct((N, LANES), F32),
    ]
    kern = functools.partial(_inproj_kernel, seq=seq, fuse_combine=combine is not None)
    return pl.pallas_call(
        kern,
        grid=(nt,),
        in_specs=in_specs,
        out_specs=out_specs,
        out_shape=out_shape,
        scratch_shapes=[pltpu.VMEM((8, D), F32)],
        compiler_params=_params(("arbitrary",)),
        name="in_projection",
    )(*args)


def _log_sigmoid(x):
    return jnp.minimum(x, 0.0) - jnp.log(1.0 + jnp.exp(-jnp.abs(x)))


def _gla_kernel(za_ref, zg_ref, gw2_ref, gb_ref, ng_ref, o_ref, st_ref, lg_ref, *, dk, dv, rank):
    heads = GLA_HEADS
    dkh, dvh = dk // heads, dv // heads
    c = GLA_CHUNK
    ts = za_ref.shape[0]

    @pl.when(pl.program_id(1) == 0)
    def _():
        st_ref[...] = jnp.zeros_like(st_ref)

    zg = zg_ref[...]
    zg_hi = zg.astype(BF16)
    zg_lo = (zg - zg_hi.astype(F32)).astype(BF16)
    lane = lax.broadcasted_iota(jnp.int32, zg.shape, 1)
    lhs = jnp.where(jnp.logical_and(lane >= rank, lane < 2 * rank), zg_lo, zg_hi)
    pre = _dot(lhs, gw2_ref[...]) + gb_ref[...]
    lg = _log_sigmoid(pre) * (1.0 / GATE_TAU)
    lg_hi = lg.astype(BF16)
    lg_ref[:, 0:dk] = lg_hi
    lg_ref[:, dk:2 * dk] = (lg - lg_hi.astype(F32)).astype(BF16)

    row = lax.broadcasted_iota(jnp.int32, (c, c), 0)
    col = lax.broadcasted_iota(jnp.int32, (c, c), 1)
    causal = row >= col
    tril = causal.astype(BF16)
    qscale = dkh ** -0.5

    for ci in range(ts // c):
        rows = slice(ci * c, (ci + 1) * c)
        b_two = _dot(tril, lg_ref[rows, :])
        b_all = b_two[:, 0:dk] + b_two[:, dk:2 * dk]
        for hd in range(heads):
            ks = slice(hd * dkh, (hd + 1) * dkh)
            q = za_ref[rows, hd * dkh:(hd + 1) * dkh].astype(F32) * qscale
            k = za_ref[rows, dk + hd * dkh:dk + (hd + 1) * dkh].astype(F32)
            v = za_ref[rows, 2 * dk + hd * dvh:2 * dk + (hd + 1) * dvh]
            r = za_ref[rows, 2 * dk + dv + hd * dvh:2 * dk + dv + (hd + 1) * dvh].astype(F32)
            b = b_all[:, ks]
            b_last = b[c - 1:c, :]
            q_t = (q * jnp.exp(b)).astype(BF16)
            k_t = (k * jnp.exp(-b)).astype(BF16)
            k_s = (k * jnp.exp(b_last - b)).astype(BF16)
            decay = jnp.exp(b_last)
            attn = lax.dot_general(q_t, k_t, (((1,), (1,)), ((), ())), preferred_element_type=F32)
            attn = jnp.where(causal, attn, 0.0).astype(BF16)
            st = st_ref[hd]
            o = _dot(attn, v) + lax.dot_general(
                q_t, st.astype(BF16), (((1,), (1,)), ((), ())), preferred_element_type=F32)
            upd = lax.dot_general(v, k_s, (((0,), (0,)), ((), ())), preferred_element_type=F32)
            st_ref[hd] = st * decay + upd
            ms = jnp.mean(o * o, axis=-1, keepdims=True)
            on = o * lax.rsqrt(ms + NORM_EPS) * ng_ref[:, hd * dvh:(hd + 1) * dvh]
            o_ref[rows, hd * dvh:(hd + 1) * dvh] = (on * (r * jax.nn.sigmoid(r))).astype(BF16)


def _gla(za, zg, gw2, gb, ng, batch, seq, dk, dv, rank):
    N = za.shape[0]
    ts = GLA_TILE
    ns = seq // ts
    heads = GLA_HEADS
    kern = functools.partial(_gla_kernel, dk=dk, dv=dv, rank=rank)
    return pl.pallas_call(
        kern,
        grid=(batch, ns),
        in_specs=[
            pl.BlockSpec((ts, za.shape[1]), lambda b, s: (b * ns + s, 0)),
            pl.BlockSpec((ts, LANES), lambda b, s: (b * ns + s, 0)),
            _const_spec((LANES, dk)),
            _const_spec((1, dk)),
            _const_spec((1, dv)),
        ],
        out_specs=pl.BlockSpec((ts, dv), lambda b, s: (b * ns + s, 0)),
        out_shape=jax.ShapeDtypeStruct((N, dv), BF16),
        scratch_shapes=[
            pltpu.VMEM((heads, dv // heads, dk // heads), F32),
            pltpu.VMEM((ts, 2 * dk), BF16),
        ],
        compiler_params=_params(("arbitrary", "arbitrary")),
        name="gla",
    )(za, zg, gw2, gb, ng)


def _mixout_kernel(og_ref, zb_ref, x_ref, mod_ref, wga_ref, wco_ref, wo_ref, n2_ref,
                   wr_ref, br_ref, x1_ref, h2_ref, logit_ref):
    tm, d = x_ref.shape
    m = mod_ref[0]
    sub = MIX_SUB
    gain2 = n2_ref[...] * (1.0 + m[4:5])
    for r0 in range(0, tm, sub):
        rs = slice(r0, r0 + sub)
        ga = zb_ref[rs, d:2 * d].astype(F32)
        gc = zb_ref[rs, 2 * d:3 * d].astype(F32)
        y_conv = _dot(zb_ref[rs, 0:d], wco_ref[...])
        y_gla = _dot(og_ref[rs, :], wga_ref[...])
        y = jax.nn.sigmoid(ga) * y_gla + jax.nn.sigmoid(gc) * y_conv
        y = _dot(y.astype(BF16), wo_ref[...])
        x1 = x_ref[rs, :] + m[2:3] * y
        x1_ref[rs, :] = x1

        ms = jnp.mean(x1 * x1, axis=-1, keepdims=True)
        h2 = x1 * lax.rsqrt(ms + NORM_EPS) * gain2 + m[3:4]
        h_hi, h_r = _round_bf16(h2)
        for p, words in enumerate(_pack_rows(h_r)):
            h2_ref[p, rs, :] = words

        h_lo = (h2 - h_r).astype(BF16)
        two = _dot(h_hi, wr_ref[...])
        logits = two[:, 0:LANES] + two[:, LANES:2 * LANES] + _dot(h_lo, wr_ref[:, 0:LANES]) + br_ref[...]
        logit_ref[rs, :] = logits


def _mixer_out(og, zb, x, modl, wga, wco, wo, n2g, wr, br, batch, seq):
    N, D = x.shape
    tm = MIX_TILE
    ns = seq // tm
    tok = lambda b, s: (b * ns + s, 0)
    return pl.pallas_call(
        _mixout_kernel,
        grid=(batch, ns),
        in_specs=[
            pl.BlockSpec((tm, D), tok),
            pl.BlockSpec((tm, zb.shape[1]), tok),
            pl.BlockSpec((tm, D), tok),
            pl.BlockSpec((1, 6, D), lambda b, s: (b, 0, 0)),
            _const_spec((D, D)),
            _const_spec((D, D)),
            _const_spec((D, D)),
            _const_spec((1, D)),
            _const_spec((D, 2 * LANES)),
            _const_spec((1, LANES)),
        ],
        out_specs=[
            pl.BlockSpec((tm, D), tok),
            pl.BlockSpec((D // 2 // ROW_PIECE, tm, ROW_PIECE), lambda b, s: (0, b * ns + s, 0)),
            pl.BlockSpec((tm, LANES), tok),
        ],
        out_shape=[
            jax.ShapeDtypeStruct((N, D), F32),
            jax.ShapeDtypeStruct((D // 2 // ROW_PIECE, N, ROW_PIECE), U32),
            jax.ShapeDtypeStruct((N, LANES), F32),
        ],
        compiler_params=_params(("arbitrary", "arbitrary")),
        name="mixer_out",
    )(og, zb, x, modl, wga, wco, wo, n2g, wr, br)


def _route_kernel(logit_ref, routet_ref, route_ref, cnt_ref, run_ref):
    tm = logit_ref.shape[0]
    sub = ROUTE_SUB
    rows8 = EXPERTS_PER_GROUP

    @pl.when(pl.program_id(0) == 0)
    def _():
        run_ref[...] = jnp.zeros_like(run_ref)

    sub8 = lax.broadcasted_iota(jnp.int32, (rows8, sub), 0)
    erow = lax.broadcasted_iota(jnp.int32, (N_EXPERTS, sub), 0)
    tr = lax.broadcasted_iota(jnp.int32, (sub, sub), 0)
    tc = lax.broadcasted_iota(jnp.int32, (sub, sub), 1)
    earlier = (tr < tc).astype(BF16)
    ones = jnp.ones((sub, LANES), BF16)
    neg = -jnp.inf
    run = run_ref[...]
    for r0 in range(0, tm, sub):
        lt = logit_ref[r0:r0 + sub, :].T
        gl = lt[0:N_GROUPS, :]
        gmax = jnp.max(gl, axis=0, keepdims=True)
        gsum = jnp.sum(jnp.exp(gl - gmax), axis=0, keepdims=True)
        g_w = 1.0 / gsum
        g_idx = jnp.min(jnp.where(gl == gmax, sub8, N_GROUPS), axis=0, keepdims=True)
        el = lt[N_GROUPS:N_GROUPS + rows8, :]
        for g in range(1, N_GROUPS):
            el = jnp.where(g_idx == g, lt[N_GROUPS + g * rows8:N_GROUPS + (g + 1) * rows8, :], el)
        e1 = jnp.max(el, axis=0, keepdims=True)
        i1 = jnp.min(jnp.where(el == e1, sub8, rows8), axis=0, keepdims=True)
        el2 = jnp.where(sub8 == i1, neg, el)
        e2 = jnp.max(el2, axis=0, keepdims=True)
        i2 = jnp.min(jnp.where(el2 == e2, sub8, rows8), axis=0, keepdims=True)
        ratio = jnp.exp(e2 - e1)
        w1 = g_w / (1.0 + ratio)
        w2 = g_w * ratio / (1.0 + ratio)
        id1 = g_idx * rows8 + i1
        id2 = g_idx * rows8 + i2

        oh1 = erow == id1
        oh2 = erow == id2
        oh1b = jnp.where(oh1, 1.0, 0.0).astype(BF16)
        oh2b = jnp.where(oh2, 1.0, 0.0).astype(BF16)
        tot1 = _dot(oh1b, ones)
        tot2 = _dot(oh2b, ones)
        base1 = jnp.concatenate([run] * (sub // LANES), axis=1)
        base2 = jnp.concatenate([run + tot1] * (sub // LANES), axis=1)
        c1 = _dot(oh1b, earlier) + base1
        c2 = _dot(oh2b, earlier) + base2
        rank1 = jnp.sum(jnp.where(oh1, c1, 0.0), axis=0, keepdims=True)
        rank2 = jnp.sum(jnp.where(oh2, c2, 0.0), axis=0, keepdims=True)
        run = run + tot1 + tot2

        rec = jnp.where(sub8 == R_ID1, id1.astype(F32), 0.0)
        rec = jnp.where(sub8 == R_ID2, id2.astype(F32), rec)
        rec = jnp.where(sub8 == R_W1, w1, rec)
        rec = jnp.where(sub8 == R_W2, w2, rec)
        rec = jnp.where(sub8 == R_RANK1, rank1, rec)
        rec = jnp.where(sub8 == R_RANK2, rank2, rec)
        routet_ref[:, r0:r0 + sub] = rec
        rec_full = jnp.concatenate([rec, jnp.zeros((LANES - rows8, sub), F32)], axis=0)
        route_ref[r0:r0 + sub, :] = rec_full.T
    run_ref[...] = run
    cnt_ref[...] = run


def _routing(logits):
    N = logits.shape[0]
    tm = 2048 if N % 2048 == 0 else ROUTE_SUB
    return pl.pallas_call(
        _route_kernel,
        grid=(N // tm,),
        in_specs=[pl.BlockSpec((tm, LANES), lambda i: (i, 0))],
        out_specs=[
            pl.BlockSpec((8, tm), lambda i: (0, i)),
            pl.BlockSpec((tm, LANES), lambda i: (i, 0)),
            pl.BlockSpec((N_EXPERTS, LANES), lambda i: (0, 0)),
        ],
        out_shape=[
            jax.ShapeDtypeStruct((8, N), F32),
            jax.ShapeDtypeStruct((N, LANES), F32),
            jax.ShapeDtypeStruct((N_EXPERTS, LANES), F32),
        ],
        scratch_shapes=[pltpu.VMEM((N_EXPERTS, LANES), F32)],
        compiler_params=_params(("arbitrary",)),
        name="moe_route",
    )(logits)


SC_WINDOW = 128


def _sc_mesh():
    return plsc.VectorSubcoreMesh(core_axis_name="core", subcore_axis_name="subcore")


def _piece_index(rows, pieces, n_rows):
    return (jnp.arange(pieces, dtype=jnp.int32)[:, None] * n_rows + rows[None, :]).reshape(1, -1)


def _sc_dispatch(h2, dests, n_rows):
    pieces, N, pc = h2.shape
    w = SC_WINDOW

    @functools.partial(pl.kernel, out_type=jax.ShapeDtypeStruct((pieces * n_rows, pc), h2.dtype),
                       mesh=_sc_mesh(), scratch_types=[], name="moe_dispatch_sc")
    def run(x_hbm, *refs):
        o_hbm = refs[-1]

        def body(x_vmem, *idx_vmem):
            for i_vmem in idx_vmem:
                pltpu.sync_copy(x_vmem, o_hbm.at[i_vmem.at[0]])

        pltpu.emit_pipeline(
            body,
            grid=(pieces * N // w,),
            in_specs=[pl.BlockSpec((w, pc), lambda i: (i, 0))]
            + [pl.BlockSpec((1, w), lambda i: (0, i))] * len(dests),
            out_specs=[],
            core_axis_name=("core", "subcore"),
            dimension_semantics=(pltpu.PARALLEL,),
        )(x_hbm, *refs[:-1])

    xs = run(h2.reshape(pieces * N, pc), *[_piece_index(d, pieces, n_rows) for d in dests])
    return xs.reshape(pieces, n_rows, pc)


def _sc_return(yb, dest):
    M = dest.shape[0]
    pieces, P, pc = yb.shape
    w = SC_WINDOW

    @functools.partial(pl.kernel, out_type=jax.ShapeDtypeStruct((pieces * M, pc), yb.dtype),
                       mesh=_sc_mesh(), scratch_types=[], name="moe_return_sc")
    def run(y_hbm, i_hbm, o_hbm):
        def body(i_vmem, o_vmem):
            pltpu.sync_copy(y_hbm.at[i_vmem.at[0]], o_vmem)

        pltpu.emit_pipeline(
            body,
            grid=(pieces * M // w,),
            in_specs=[pl.BlockSpec((1, w), lambda i: (0, i))],
            out_specs=[pl.BlockSpec((w, pc), lambda i: (i, 0))],
            core_axis_name=("core", "subcore"),
            dimension_semantics=(pltpu.PARALLEL,),
        )(i_hbm, o_hbm)

    return run(yb.reshape(pieces * P, pc), _piece_index(dest, pieces, P)).reshape(pieces, M, pc)


def _expert_kernel(tab_ref, xs_ref, w1_ref, w3_ref, w2_ref, y_ref, wf1, wf3, wf2, sem, *, layer):
    j = pl.program_id(0)
    jp = jnp.maximum(j - 1, 0)
    live = j < tab_ref[I_COUNT, 0]
    expert = tab_ref[I_EXPERT, j]
    new_expert = jnp.logical_or(j == 0, expert != tab_ref[I_EXPERT, jp])
    streams = ((w1_ref, wf1), (w3_ref, wf3), (w2_ref, wf2))

    def weight_copies(expert, s):
        return [pltpu.make_async_copy(w.at[layer, expert], buf.at[s], sem.at[s, k])
                for k, (w, buf) in enumerate(streams)]

    @pl.when(j == 0)
    def _():
        for cp in weight_copies(expert, 0):
            cp.start()

    @pl.when(jnp.logical_and(new_expert, live))
    def _():
        s = tab_ref[I_SLOT, j]
        for cp in weight_copies(expert, s):
            cp.wait()

        @pl.when(tab_ref[I_NEXT, j] >= 0)
        def _():
            for cp in weight_copies(tab_ref[I_NEXT, j], 1 - s):
                cp.start()

    bm = xs_ref.shape[1]
    for m in range(EXPERT_ROWS_STEP, bm + 1, EXPERT_ROWS_STEP):
        @pl.when(jnp.logical_and(live, tab_ref[I_ROWS, j] == m))
        def _(m=m):
            xb = _unpack_rows(xs_ref, m).astype(BF16)
            s = tab_ref[I_SLOT, j]
            h1 = _dot(xb, wf1[s].astype(BF16))
            h3 = _dot(xb, wf3[s].astype(BF16))
            a = (h1 * jax.nn.sigmoid(h1) * h3).astype(BF16)
            for p, words in enumerate(_pack_rows(_round_bf16(_dot(a, wf2[s].astype(BF16)))[1])):
                y_ref[p, 0:m, :] = words
                if m < bm:
                    y_ref[p, m:bm, :] = jnp.zeros((bm - m, words.shape[1]), U32)


def _experts(xs, items, w1, w3, w2, layer):
    pieces, P, pc = xs.shape
    _, E, D, DE = w1.shape
    bm = MOE_BLOCK
    n_items_max = P // bm
    rows = lambda j, tab: (0, tab[I_BLOCK, j], 0)
    return pl.pallas_call(
        functools.partial(_expert_kernel, layer=layer),
        grid_spec=pltpu.PrefetchScalarGridSpec(
            num_scalar_prefetch=1,
            grid=(n_items_max,),
            in_specs=[
                pl.BlockSpec((pieces, bm, pc), rows),
                pl.BlockSpec(memory_space=pl.ANY),
                pl.BlockSpec(memory_space=pl.ANY),
                pl.BlockSpec(memory_space=pl.ANY),
            ],
            out_specs=pl.BlockSpec((pieces, bm, pc), rows),
            scratch_shapes=[
                pltpu.VMEM((2, D, DE), F32),
                pltpu.VMEM((2, D, DE), F32),
                pltpu.VMEM((2, DE, D), F32),
                pltpu.SemaphoreType.DMA((2, 3)),
            ],
        ),
        out_shape=jax.ShapeDtypeStruct((pieces, P, pc), U32),
        compiler_params=_params(("arbitrary",)),
        name="moe_experts",
    )(items, xs, w1, w3, w2)


def _combine_kernel(x1_ref, route_ref, mod_ref, fg_ref, y0_ref, y1_ref, *rest):
    o_ref = rest[-1]
    x2 = x1_ref[...] + mod_ref[0][5:6] * _moe_mix(route_ref, y0_ref, y1_ref)
    ms = jnp.mean(x2 * x2, axis=-1, keepdims=True)
    o_ref[...] = x2 * lax.rsqrt(ms + NORM_EPS) * fg_ref[...]


def _final_combine(x1, route, modl, final_g, yk, seq, first_tile, prev_out):
    N, D = x1.shape
    tm = ROW_TILE
    nt = yk.shape[1] // 2 // tm
    tok = lambda i: (first_tile + i, 0)
    in_specs = [
        pl.BlockSpec((tm, D), tok),
        pl.BlockSpec((tm, LANES), tok),
        pl.BlockSpec((1, 6, D), lambda i: (((first_tile + i) * tm) // seq, 0, 0)),
        pl.BlockSpec((1, D), lambda i: (0, 0)),
        pl.BlockSpec((yk.shape[0], tm, ROW_PIECE), lambda i: (0, i, 0)),
        pl.BlockSpec((yk.shape[0], tm, ROW_PIECE), lambda i: (0, nt + i, 0)),
    ]
    args = [x1, route, modl, final_g, yk, yk]
    aliases = {}
    if prev_out is not None:
        in_specs.append(pl.BlockSpec(memory_space=pl.ANY))
        args.append(prev_out)
        aliases = {len(args) - 1: 0}
    return pl.pallas_call(
        _combine_kernel,
        grid=(nt,),
        in_specs=in_specs,
        out_specs=pl.BlockSpec((tm, D), tok),
        out_shape=jax.ShapeDtypeStruct((N, D), F32),
        input_output_aliases=aliases,
        compiler_params=_params(("arbitrary",)),
        name="moe_combine",
    )(*args)


ITEM_LANES = 256
I_BLOCK, I_EXPERT, I_NEXT, I_SLOT, I_COUNT, I_ROWS = 0, 1, 2, 3, 4, 5


def _dest_kernel(routet_ref, cnt_ref, d1_ref, d2_ref, d3_ref, items_ref):
    tm = routet_ref.shape[1]
    ne = N_EXPERTS
    bm = float(MOE_BLOCK)
    cnt = cnt_ref[...]
    r = lax.broadcasted_iota(jnp.int32, (ne, ne), 0)
    c = lax.broadcasted_iota(jnp.int32, (ne, ne), 1)
    lower = (r >= c).astype(F32)
    cumsum = lambda a: jnp.dot(lower, a, precision=HIGHEST, preferred_element_type=F32)
    padded = jnp.floor((cnt + (bm - 1.0)) * (1.0 / bm)) * bm
    pend = cumsum(padded)
    pstart = pend - padded
    npad = padded - cnt
    pad_end = cumsum(npad)
    pad_start = pad_end - npad

    tile = lambda a, n: jnp.concatenate([a] * (n // LANES), axis=1)
    expert = lax.broadcasted_iota(jnp.int32, (ne, tm), 0).astype(F32)
    rec = routet_ref[...]
    pstart_t = tile(pstart, tm)

    def sorted_row(row_id, row_rank):
        sel = expert == rec[row_id:row_id + 1, :]
        return jnp.sum(jnp.where(sel, pstart_t, 0.0), axis=0, keepdims=True) + rec[row_rank:row_rank + 1, :]

    dest1 = sorted_row(R_ID1, R_RANK1)
    dest2 = sorted_row(R_ID2, R_RANK2)
    t = (lax.broadcasted_iota(jnp.int32, (1, tm), 1) + pl.program_id(0) * tm).astype(F32)
    ps_t, pe_t = tile(pad_start, tm), tile(pad_end, tm)
    in_e = jnp.logical_and(ps_t <= t, t < pe_t)
    pad_row = jnp.sum(jnp.where(in_e, tile(pstart + cnt, tm) + (t - ps_t), 0.0), axis=0, keepdims=True)
    dest3 = jnp.where(t < pe_t[ne - 1:ne, :], pad_row, dest1)
    d1_ref[0] = dest1.astype(jnp.int32)
    d2_ref[0] = dest2.astype(jnp.int32)
    d3_ref[0] = dest3.astype(jnp.int32)

    @pl.when(pl.program_id(0) == 0)
    def _():
        nl = ITEM_LANES
        pend_i = tile(pend, nl)
        n_items = pend_i[ne - 1:ne, :] * (1.0 / bm)
        blk = jnp.minimum(lax.broadcasted_iota(jnp.int32, (1, nl), 1).astype(F32), n_items - 1.0)
        erow = lax.broadcasted_iota(jnp.int32, (ne, nl), 0).astype(F32)
        ie = jnp.sum(jnp.where(pend_i <= blk * bm, 1.0, 0.0), axis=0, keepdims=True)
        nonempty = tile(cnt, nl) > 0.0
        order = jnp.sum(jnp.where(jnp.logical_and(nonempty, erow < ie), 1.0, 0.0), axis=0, keepdims=True)
        nxt = jnp.min(jnp.where(jnp.logical_and(nonempty, erow > ie), erow, float(ne)), axis=0, keepdims=True)
        nxt = jnp.where(nxt == float(ne), -1.0, nxt)
        slot = order - 2.0 * jnp.floor(order * 0.5)
        own_end = jnp.sum(jnp.where(erow == ie, tile(pstart + cnt, nl), 0.0), axis=0, keepdims=True)
        step = float(EXPERT_ROWS_STEP)
        rows = jnp.minimum(jnp.ceil((own_end - blk * bm) * (1.0 / step)) * step, bm)
        row = lax.broadcasted_iota(jnp.int32, (8, nl), 0)
        tab = jnp.where(row == I_BLOCK, blk, 0.0)
        tab = jnp.where(row == I_EXPERT, ie, tab)
        tab = jnp.where(row == I_NEXT, nxt, tab)
        tab = jnp.where(row == I_SLOT, slot, tab)
        tab = jnp.where(row == I_COUNT, n_items, tab)
        tab = jnp.where(row == I_ROWS, rows, tab)
        items_ref[...] = tab.astype(jnp.int32)


def _destinations(route_t, cnt, n_blocks):
    N = route_t.shape[1]
    tm = 2048 if N % 2048 == 0 else ROUTE_SUB
    out = jax.ShapeDtypeStruct((N // tm, 1, tm), jnp.int32)
    d1, d2, d3, items = pl.pallas_call(
        _dest_kernel,
        grid=(N // tm,),
        in_specs=[pl.BlockSpec((8, tm), lambda i: (0, i)),
                  pl.BlockSpec((N_EXPERTS, LANES), lambda i: (0, 0))],
        out_specs=[pl.BlockSpec((1, 1, tm), lambda i: (i, 0, 0))] * 3
        + [pl.BlockSpec((8, ITEM_LANES), lambda i: (0, 0))],
        out_shape=[out, out, out, jax.ShapeDtypeStruct((8, ITEM_LANES), jnp.int32)],
        compiler_params=_params(("arbitrary",)),
        name="moe_dest",
    )(route_t, cnt)
    return d1.reshape(N), d2.reshape(N), d3.reshape(N), items


def kernel(x, c, mod_w, mod_b, norm1_g, w_in, gate_w2, gate_b, gla_norm_g, conv_w, w_gla_out,
           w_conv_out, w_out, norm2_g, router_group_w, router_group_b, router_expert_w,
           router_expert_b, expert_w1, expert_w3, expert_w2, final_norm_g):
    B, S, D = x.shape
    L = mod_w.shape[0]
    N = B * S
    dk = gate_w2.shape[2]
    rank = gate_w2.shape[1]
    dv = gla_norm_g.shape[1]
    n_slots = N * 2
    n_blocks = n_slots // MOE_BLOCK + N_EXPERTS
    assert S % GLA_TILE == 0 and S % MIX_TILE == 0 and S % IN_TILE == 0 and S % IN_TILE_PLAIN == 0
    assert N % (FINAL_PARTS * ROW_TILE) == 0
    assert n_slots % MOE_BLOCK == 0 and MOE_BLOCK & (MOE_BLOCK - 1) == 0 and n_blocks <= ITEM_LANES
    assert N_EXPERTS * (MOE_BLOCK - 1) <= N
    assert N_GROUPS + N_EXPERTS <= LANES and EXPERTS_PER_GROUP == 8 and 3 * rank <= LANES and D % (2 * ROW_PIECE) == 0

    mod = [_modulation(c, mod_w, mod_b, l)[0] for l in range(L)]
    xf = x.reshape(N, D)
    o_gd = 2 * dk + dv
    o_r = o_gd + rank
    o_b = o_r + dv
    for l in range(L):
        modl = mod[l].reshape(B, 6, D)
        wl = w_in[l]
        if l > 0:
            wl, _ = lax.optimization_barrier((wl, cnt))
        wa = jnp.concatenate([wl[:, :o_gd], wl[:, o_r:o_b]], axis=1).astype(BF16)
        wb = wl[:, o_b:].astype(BF16)
        wgd = wl[:, o_gd:o_r]
        wg = jnp.pad(jnp.concatenate([wgd, wgd, wgd], axis=1), ((0, 0), (0, LANES - 3 * rank))).astype(BF16)
        g_hi = gate_w2[l].astype(BF16)
        g_lo = (gate_w2[l] - g_hi.astype(F32)).astype(BF16)
        gw2 = jnp.pad(jnp.concatenate([g_hi, g_hi, g_lo], axis=0), ((0, LANES - 3 * rank), (0, 0)))
        if l == 0:
            za, zb, zg = _in_projection(xf, modl, norm1_g[l].reshape(1, D), conv_w[l], wa, wb, wg, S)
        else:
            xf, za, zb, zg = _in_projection(x1, modl, norm1_g[l].reshape(1, D), conv_w[l], wa, wb, wg, S,
                                            combine=(route, yk, mod[l - 1].reshape(B, 6, D)))
        og = _gla(za, zg, gw2, gate_b[l].reshape(1, dk), gla_norm_g[l].reshape(1, dv), B, S, dk, dv, rank)
        wr = jnp.pad(jnp.concatenate([router_group_w[l], router_expert_w[l]], axis=1),
                     ((0, 0), (0, LANES - N_GROUPS - N_EXPERTS)))
        wr_hi = wr.astype(BF16)
        wr = jnp.concatenate([wr_hi, (wr - wr_hi.astype(F32)).astype(BF16)], axis=1)
        br = jnp.pad(jnp.concatenate([router_group_b[l], router_expert_b[l]]),
                     (0, LANES - N_GROUPS - N_EXPERTS)).reshape(1, LANES)
        x1, h2, logits = _mixer_out(
            og, zb, xf, modl, w_gla_out[l].astype(BF16), w_conv_out[l].astype(BF16),
            w_out[l].astype(BF16), norm2_g[l].reshape(1, D), wr, br, B, S)
        route_t, route, cnt = _routing(logits)
        dest1, dest2, dest3, items = _destinations(route_t, cnt, n_blocks)
        xs = _sc_dispatch(h2, (dest1, dest2, dest3), n_blocks * MOE_BLOCK)
        yb = _experts(xs, items, expert_w1, expert_w3, expert_w2, l)
        if l < L - 1:
            yk = _sc_return(yb, jnp.concatenate([dest1, dest2]))
    out = None
    part = N // FINAL_PARTS
    for p in range(FINAL_PARTS):
        tok = slice(p * part, (p + 1) * part)
        yk = _sc_return(yb, jnp.concatenate([dest1[tok], dest2[tok]]))
        out = _final_combine(x1, route, modl, final_norm_g.reshape(1, D), yk, S,
                             p * part // ROW_TILE, out)
    return out.reshape(B, S, D)
```

```python
import functools

import jax
import jax.numpy as jnp
from jax import lax
from jax.experimental import pallas as pl
from jax.experimental.pallas import tpu as pltpu
from jax.experimental.pallas import tpu_sc as plsc

F32 = jnp.float32
BF16 = jnp.bfloat16
HIGHEST = lax.Precision.HIGHEST

GLA_HEADS = 4
GATE_TAU = 16.0
GLA_CHUNK = 64
N_GROUPS = 8
EXPERTS_PER_GROUP = 8
N_EXPERTS = N_GROUPS * EXPERTS_PER_GROUP
NORM_EPS = 1e-6

LANES = 128
VMEM_LIMIT_BYTES = 56 * 1024 * 1024

IN_TILE = 512
IN_TILE_PLAIN = 1024
GLA_TILE = 1024
MIX_TILE = 1024
MIX_SUB = 1024
MIX_COLS = 256
ROUTE_SUB = 256
MOE_BLOCK = 512
EXPERT_ROWS_STEP = 128
ROW_TILE = 1024
FINAL_PARTS = 4
ROW_PIECE = 256
U32 = jnp.uint32

R_ID1, R_ID2, R_W1, R_W2, R_RANK1, R_RANK2 = 0, 1, 2, 3, 4, 5


def _dot(a, b):
    return jnp.dot(a, b, preferred_element_type=F32)


def _round_bf16(x):
    xb = x.astype(BF16)
    return xb, xb.astype(F32)


def _pack_rows(xr):
    half = xr.shape[1] // 2
    out = []
    for p in range(half // ROW_PIECE):
        lo = xr[:, p * ROW_PIECE:(p + 1) * ROW_PIECE]
        hi = xr[:, half + p * ROW_PIECE:half + (p + 1) * ROW_PIECE]
        out.append((pltpu.bitcast(lo, U32) >> 16) | pltpu.bitcast(hi, U32))
    return out


def _unpack_rows(ref, rows=None):
    rows = ref.shape[1] if rows is None else rows
    words = [ref[p, 0:rows, :] for p in range(ref.shape[0])]
    lo = [pltpu.bitcast(w << 16, F32) for w in words]
    hi = [pltpu.bitcast(w & jnp.uint32(0xFFFF0000), F32) for w in words]
    return jnp.concatenate(lo + hi, axis=-1)


def _const_spec(shape):
    nd = len(shape)
    return pl.BlockSpec(shape, lambda *_: (0,) * nd, pipeline_mode=pl.Buffered(1))


def _params(sem):
    return pltpu.CompilerParams(dimension_semantics=sem, vmem_limit_bytes=VMEM_LIMIT_BYTES)


MOD_ROWS = 256


def _mod_kernel(c_ref, w_ref, b_ref, o_ref):
    c = c_ref[...]
    sc = c * jax.nn.sigmoid(c)
    s_hi, s_r = _round_bf16(sc)
    w = w_ref[0]
    w_hi, w_r = _round_bf16(w)
    s_lo = (sc - s_r).astype(BF16)
    part = _dot(s_hi, w_hi) + _dot(s_lo, w_hi) + _dot(s_hi, (w - w_r).astype(BF16))

    @pl.when(pl.program_id(1) == 0)
    def _():
        o_ref[0] = part + b_ref[0]

    @pl.when(pl.program_id(1) > 0)
    def _():
        o_ref[0] += part


def _modulation(c, mod_w, mod_b, layer):
    L, D, D6 = mod_w.shape
    B = c.shape[0]
    kb = MOD_ROWS
    return pl.pallas_call(
        _mod_kernel,
        grid=(1, D // kb),
        in_specs=[
            pl.BlockSpec((B, kb), lambda l, k: (0, k)),
            pl.BlockSpec((1, kb, D6), lambda l, k: (layer, k, 0)),
            pl.BlockSpec((1, 1, D6), lambda l, k: (layer, 0, 0)),
        ],
        out_specs=pl.BlockSpec((1, B, D6), lambda l, k: (0, 0, 0)),
        out_shape=jax.ShapeDtypeStruct((1, B, D6), F32),
        compiler_params=_params(("arbitrary", "arbitrary")),
        name="adaln_mod",
    )(c, mod_w, mod_b.reshape(L, 1, D6))


CONV_COLS = 256


def _moe_mix(route_ref, y0_ref, y1_ref):
    rec = route_ref[...]
    w1 = rec[:, R_W1:R_W1 + 1]
    w2 = rec[:, R_W2:R_W2 + 1]
    return w1 * _unpack_rows(y0_ref) + w2 * _unpack_rows(y1_ref)


def _inproj_kernel(*refs, seq, fuse_combine):
    if fuse_combine:
        (x_ref, route_ref, y0_ref, y1_ref, modp_ref, mod_ref, g_ref, cw_ref, wa_ref, wb_ref, wg_ref,
         xo_ref, za_ref, zb_ref, zg_ref, carry_ref) = refs
        x = x_ref[...] + modp_ref[0][5:6] * _moe_mix(route_ref, y0_ref, y1_ref)
        xo_ref[...] = x
    else:
        (x_ref, mod_ref, g_ref, cw_ref, wa_ref, wb_ref, wg_ref, za_ref, zb_ref, zg_ref, carry_ref) = refs
        x = x_ref[...]
    tm, d = x.shape

    @pl.when((pl.program_id(0) * tm) % seq == 0)
    def _():
        carry_ref[...] = jnp.zeros_like(carry_ref)

    ms = jnp.mean(x * x, axis=-1, keepdims=True)
    m = mod_ref[0]
    h = (x * lax.rsqrt(ms + NORM_EPS) * (g_ref[...] * (1.0 + m[1:2])) + m[0:1]).astype(BF16)
    for j in range(za_ref.shape[1] // d):
        za_ref[:, j * d:(j + 1) * d] = _dot(h, wa_ref[:, j * d:(j + 1) * d]).astype(BF16)
    zg_ref[...] = _dot(h, wg_ref[...])
    for j in range(1, zb_ref.shape[1] // d):
        zb_ref[:, j * d:(j + 1) * d] = _dot(h, wb_ref[:, (j + 2) * d:(j + 3) * d]).astype(BF16)

    w = CONV_COLS
    rowi = lax.broadcasted_iota(jnp.int32, (tm, w), 0)
    for j in range(d // w):
        cols = slice(j * w, (j + 1) * w)
        cb = _dot(h, wb_ref[:, j * w:(j + 1) * w])
        cc = _dot(h, wb_ref[:, d + j * w:d + (j + 1) * w])
        ch = _dot(h, wb_ref[:, 2 * d + j * w:2 * d + (j + 1) * w])
        u = cc * ch
        prev = carry_ref[:, cols]
        u1 = jnp.where(rowi == 0, prev[7:8], pltpu.roll(u, 1, 0))
        u2 = jnp.where(rowi == 0, prev[6:7], jnp.where(rowi == 1, prev[7:8], pltpu.roll(u, 2, 0)))
        carry_ref[:, cols] = u[tm - 8:tm]
        conv = cw_ref[0:1, cols] * u2 + cw_ref[1:2, cols] * u1 + cw_ref[2:3, cols] * u
        zb_ref[:, cols] = (cb * conv).astype(BF16)


def _in_projection(x, modl, norm_g, conv_w, wa, wb, wg, seq, combine=None):
    N, D = x.shape
    tm = IN_TILE_PLAIN if combine is None else IN_TILE
    nt = N // tm
    ca, cb = wa.shape[1], wb.shape[1] - 2 * D
    tok = lambda i: (i, 0)
    per_batch = lambda i: ((i * tm) // seq, 0, 0)
    in_specs = [pl.BlockSpec((tm, D), tok)]
    args = [x]
    out_specs, out_shape = [], []
    if combine is not None:
        route, yk, mod_prev = combine
        piece_blk = (yk.shape[0], tm, ROW_PIECE)
        in_specs += [
            pl.BlockSpec((tm, LANES), tok),
            pl.BlockSpec(piece_blk, lambda i: (0, i, 0)),
            pl.BlockSpec(piece_blk, lambda i: (0, nt + i, 0)),
            pl.BlockSpec((1, 6, D), per_batch),
        ]
        args += [route, yk, yk, mod_prev]
        out_specs.append(pl.BlockSpec((tm, D), tok))
        out_shape.append(jax.ShapeDtypeStruct((N, D), F32))
    in_specs += [
        pl.BlockSpec((1, 6, D), per_batch),
        _const_spec((1, D)),
        _const_spec(conv_w.shape),
        _const_spec((D, ca)),
        _const_spec(wb.shape),
        _const_spec((D, LANES)),
    ]
    args += [modl, norm_g, conv_w, wa, wb, wg]
    out_specs += [
        pl.BlockSpec((tm, ca), tok),
        pl.BlockSpec((tm, cb), tok),
        pl.BlockSpec((tm, LANES), tok),
    ]
    out_shape += [
        jax.ShapeDtypeStruct((N, ca), BF16),
        jax.ShapeDtypeStruct((N, cb), BF16),
        jax.ShapeDtypeStruct((N, LANES), F32),
    ]
    kern = functools.partial(_inproj_kernel, seq=seq, fuse_combine=combine is not None)
    return pl.pallas_call(
        kern,
        grid=(nt,),
        in_specs=in_specs,
        out_specs=out_specs,
        out_shape=out_shape,
        scratch_shapes=[pltpu.VMEM((8, D), F32)],
        compiler_params=_params(("arbitrary",)),
        name="in_projection",
    )(*args)


def _log_sigmoid(x):
    return jnp.minimum(x, 0.0) - jnp.log(1.0 + jnp.exp(-jnp.abs(x)))


def _gla_kernel(za_ref, zg_ref, gw2_ref, gb_ref, ng_ref, o_ref, st_ref, lg_ref, *, dk, dv, rank):
    heads = GLA_HEADS
    dkh, dvh = dk // heads, dv // heads
    c = GLA_CHUNK
    ts = za_ref.shape[0]

    @pl.when(pl.program_id(1) == 0)
    def _():
        st_ref[...] = jnp.zeros_like(st_ref)

    zg = zg_ref[...]
    zg_hi = zg.astype(BF16)
    zg_lo = (zg - zg_hi.astype(F32)).astype(BF16)
    lane = lax.broadcasted_iota(jnp.int32, zg.shape, 1)
    lhs = jnp.where(jnp.logical_and(lane >= rank, lane < 2 * rank), zg_lo, zg_hi)
    pre = _dot(lhs, gw2_ref[...]) + gb_ref[...]
    lg = _log_sigmoid(pre) * (1.0 / GATE_TAU)
    lg_hi = lg.astype(BF16)
    lg_ref[:, 0:dk] = lg_hi
    lg_ref[:, dk:2 * dk] = (lg - lg_hi.astype(F32)).astype(BF16)

    row = lax.broadcasted_iota(jnp.int32, (c, c), 0)
    col = lax.broadcasted_iota(jnp.int32, (c, c), 1)
    causal = row >= col
    tril = causal.astype(BF16)
    qscale = dkh ** -0.5

    for ci in range(ts // c):
        rows = slice(ci * c, (ci + 1) * c)
        b_two = _dot(tril, lg_ref[rows, :])
        b_all = b_two[:, 0:dk] + b_two[:, dk:2 * dk]
        for hd in range(heads):
            ks = slice(hd * dkh, (hd + 1) * dkh)
            q = za_ref[rows, hd * dkh:(hd + 1) * dkh].astype(F32) * qscale
            k = za_ref[rows, dk + hd * dkh:dk + (hd + 1) * dkh].astype(F32)
            v = za_ref[rows, 2 * dk + hd * dvh:2 * dk + (hd + 1) * dvh]
            r = za_ref[rows, 2 * dk + dv + hd * dvh:2 * dk + dv + (hd + 1) * dvh].astype(F32)
            b = b_all[:, ks]
            b_last = b[c - 1:c, :]
            q_t = (q * jnp.exp(b)).astype(BF16)
            k_t = (k * jnp.exp(-b)).astype(BF16)
            k_s = (k * jnp.exp(b_last - b)).astype(BF16)
            decay = jnp.exp(b_last)
            attn = lax.dot_general(q_t, k_t, (((1,), (1,)), ((), ())), preferred_element_type=F32)
            attn = jnp.where(causal, attn, 0.0).astype(BF16)
            st = st_ref[hd]
            o = _dot(attn, v) + lax.dot_general(
                q_t, st.astype(BF16), (((1,), (1,)), ((), ())), preferred_element_type=F32)
            upd = lax.dot_general(v, k_s, (((0,), (0,)), ((), ())), preferred_element_type=F32)
            st_ref[hd] = st * decay + upd
            ms = jnp.mean(o * o, axis=-1, keepdims=True)
            on = o * lax.rsqrt(ms + NORM_EPS) * ng_ref[:, hd * dvh:(hd + 1) * dvh]
            o_ref[rows, hd * dvh:(hd + 1) * dvh] = (on * (r * jax.nn.sigmoid(r))).astype(BF16)


def _gla(za, zg, gw2, gb, ng, batch, seq, dk, dv, rank):
    N = za.shape[0]
    ts = GLA_TILE
    ns = seq // ts
    heads = GLA_HEADS
    kern = functools.partial(_gla_kernel, dk=dk, dv=dv, rank=rank)
    return pl.pallas_call(
        kern,
        grid=(batch, ns),
        in_specs=[
            pl.BlockSpec((ts, za.shape[1]), lambda b, s: (b * ns + s, 0)),
            pl.BlockSpec((ts, LANES), lambda b, s: (b * ns + s, 0)),
            _const_spec((LANES, dk)),
            _const_spec((1, dk)),
            _const_spec((1, dv)),
        ],
        out_specs=pl.BlockSpec((ts, dv), lambda b, s: (b * ns + s, 0)),
        out_shape=jax.ShapeDtypeStruct((N, dv), BF16),
        scratch_shapes=[
            pltpu.VMEM((heads, dv // heads, dk // heads), F32),
            pltpu.VMEM((ts, 2 * dk), BF16),
        ],
        compiler_params=_params(("arbitrary", "arbitrary")),
        name="gla",
    )(za, zg, gw2, gb, ng)


def _mixout_kernel(og_ref, zb_ref, x_ref, mod_ref, wga_ref, wco_ref, wo_ref, n2_ref,
                   wr_ref, br_ref, x1_ref, h2_ref, logit_ref):
    tm, d = x_ref.shape
    m = mod_ref[0]
    sub = MIX_SUB
    gain2 = n2_ref[...] * (1.0 + m[4:5])
    for r0 in range(0, tm, sub):
        rs = slice(r0, r0 + sub)
        w = MIX_COLS
        ys = []
        for j in range(d // w):
            cols = slice(j * w, (j + 1) * w)
            ga = zb_ref[rs, d + j * w:d + (j + 1) * w].astype(F32)
            gc = zb_ref[rs, 2 * d + j * w:2 * d + (j + 1) * w].astype(F32)
            y_conv = _dot(zb_ref[rs, 0:d], wco_ref[:, cols])
            y_gla = _dot(og_ref[rs, :], wga_ref[:, cols])
            ys.append((jax.nn.sigmoid(ga) * y_gla + jax.nn.sigmoid(gc) * y_conv).astype(BF16))
        y = _dot(jnp.concatenate(ys, axis=1), wo_ref[...])
        x1 = x_ref[rs, :] + m[2:3] * y
        x1_ref[rs, :] = x1

        ms = jnp.mean(x1 * x1, axis=-1, keepdims=True)
        h2 = x1 * lax.rsqrt(ms + NORM_EPS) * gain2 + m[3:4]
        h_hi, h_r = _round_bf16(h2)
        for p, words in enumerate(_pack_rows(h_r)):
            h2_ref[p, rs, :] = words

        h_lo = (h2 - h_r).astype(BF16)
        two = _dot(h_hi, wr_ref[...])
        logits = two[:, 0:LANES] + two[:, LANES:2 * LANES] + _dot(h_lo, wr_ref[:, 0:LANES]) + br_ref[...]
        logit_ref[rs, :] = logits


def _mixer_out(og, zb, x, modl, wga, wco, wo, n2g, wr, br, batch, seq):
    N, D = x.shape
    tm = MIX_TILE
    ns = seq // tm
    tok = lambda b, s: (b * ns + s, 0)
    return pl.pallas_call(
        _mixout_kernel,
        grid=(batch, ns),
        in_specs=[
            pl.BlockSpec((tm, D), tok),
            pl.BlockSpec((tm, zb.shape[1]), tok),
            pl.BlockSpec((tm, D), tok),
            pl.BlockSpec((1, 6, D), lambda b, s: (b, 0, 0)),
            _const_spec((D, D)),
            _const_spec((D, D)),
            _const_spec((D, D)),
            _const_spec((1, D)),
            _const_spec((D, 2 * LANES)),
            _const_spec((1, LANES)),
        ],
        out_specs=[
            pl.BlockSpec((tm, D), tok),
            pl.BlockSpec((D // 2 // ROW_PIECE, tm, ROW_PIECE), lambda b, s: (0, b * ns + s, 0)),
            pl.BlockSpec((tm, LANES), tok),
        ],
        out_shape=[
            jax.ShapeDtypeStruct((N, D), F32),
            jax.ShapeDtypeStruct((D // 2 // ROW_PIECE, N, ROW_PIECE), U32),
            jax.ShapeDtypeStruct((N, LANES), F32),
        ],
        compiler_params=_params(("arbitrary", "arbitrary")),
        name="mixer_out",
    )(og, zb, x, modl, wga, wco, wo, n2g, wr, br)


def _route_kernel(logit_ref, routet_ref, route_ref, cnt_ref, run_ref):
    tm = logit_ref.shape[0]
    sub = ROUTE_SUB
    rows8 = EXPERTS_PER_GROUP

    @pl.when(pl.program_id(0) == 0)
    def _():
        run_ref[...] = jnp.zeros_like(run_ref)

    sub8 = lax.broadcasted_iota(jnp.int32, (rows8, sub), 0)
    erow = lax.broadcasted_iota(jnp.int32, (N_EXPERTS, sub), 0)
    tr = lax.broadcasted_iota(jnp.int32, (sub, sub), 0)
    tc = lax.broadcasted_iota(jnp.int32, (sub, sub), 1)
    earlier = (tr < tc).astype(BF16)
    ones = jnp.ones((sub, LANES), BF16)
    neg = -jnp.inf
    run = run_ref[...]
    for r0 in range(0, tm, sub):
        lt = logit_ref[r0:r0 + sub, :].T
        gl = lt[0:N_GROUPS, :]
        gmax = jnp.max(gl, axis=0, keepdims=True)
        gsum = jnp.sum(jnp.exp(gl - gmax), axis=0, keepdims=True)
        g_w = 1.0 / gsum
        g_idx = jnp.min(jnp.where(gl == gmax, sub8, N_GROUPS), axis=0, keepdims=True)
        el = lt[N_GROUPS:N_GROUPS + rows8, :]
        for g in range(1, N_GROUPS):
            el = jnp.where(g_idx == g, lt[N_GROUPS + g * rows8:N_GROUPS + (g + 1) * rows8, :], el)
        e1 = jnp.max(el, axis=0, keepdims=True)
        i1 = jnp.min(jnp.where(el == e1, sub8, rows8), axis=0, keepdims=True)
        el2 = jnp.where(sub8 == i1, neg, el)
        e2 = jnp.max(el2, axis=0, keepdims=True)
        i2 = jnp.min(jnp.where(el2 == e2, sub8, rows8), axis=0, keepdims=True)
        ratio = jnp.exp(e2 - e1)
        w1 = g_w / (1.0 + ratio)
        w2 = g_w * ratio / (1.0 + ratio)
        id1 = g_idx * rows8 + i1
        id2 = g_idx * rows8 + i2

        oh1 = erow == id1
        oh2 = erow == id2
        oh1b = jnp.where(oh1, 1.0, 0.0).astype(BF16)
        oh2b = jnp.where(oh2, 1.0, 0.0).astype(BF16)
        tot1 = _dot(oh1b, ones)
        tot2 = _dot(oh2b, ones)
        base1 = jnp.concatenate([run] * (sub // LANES), axis=1)
        base2 = jnp.concatenate([run + tot1] * (sub // LANES), axis=1)
        c1 = _dot(oh1b, earlier) + base1
        c2 = _dot(oh2b, earlier) + base2
        rank1 = jnp.sum(jnp.where(oh1, c1, 0.0), axis=0, keepdims=True)
        rank2 = jnp.sum(jnp.where(oh2, c2, 0.0), axis=0, keepdims=True)
        run = run + tot1 + tot2

        rec = jnp.where(sub8 == R_ID1, id1.astype(F32), 0.0)
        rec = jnp.where(sub8 == R_ID2, id2.astype(F32), rec)
        rec = jnp.where(sub8 == R_W1, w1, rec)
        rec = jnp.where(sub8 == R_W2, w2, rec)
        rec = jnp.where(sub8 == R_RANK1, rank1, rec)
        rec = jnp.where(sub8 == R_RANK2, rank2, rec)
        routet_ref[:, r0:r0 + sub] = rec
        rec_full = jnp.concatenate([rec, jnp.zeros((LANES - rows8, sub), F32)], axis=0)
        route_ref[r0:r0 + sub, :] = rec_full.T
    run_ref[...] = run
    cnt_ref[...] = run


def _routing(logits):
    N = logits.shape[0]
    tm = 2048 if N % 2048 == 0 else ROUTE_SUB
    return pl.pallas_call(
        _route_kernel,
        grid=(N // tm,),
        in_specs=[pl.BlockSpec((tm, LANES), lambda i: (i, 0))],
        out_specs=[
            pl.BlockSpec((8, tm), lambda i: (0, i)),
            pl.BlockSpec((tm, LANES), lambda i: (i, 0)),
            pl.BlockSpec((N_EXPERTS, LANES), lambda i: (0, 0)),
        ],
        out_shape=[
            jax.ShapeDtypeStruct((8, N), F32),
            jax.ShapeDtypeStruct((N, LANES), F32),
            jax.ShapeDtypeStruct((N_EXPERTS, LANES), F32),
        ],
        scratch_shapes=[pltpu.VMEM((N_EXPERTS, LANES), F32)],
        compiler_params=_params(("arbitrary",)),
        name="moe_route",
    )(logits)


SC_WINDOW = 128


def _sc_mesh():
    return plsc.VectorSubcoreMesh(core_axis_name="core", subcore_axis_name="subcore")


def _piece_index(rows, pieces, n_rows):
    return (jnp.arange(pieces, dtype=jnp.int32)[:, None] * n_rows + rows[None, :]).reshape(1, -1)


def _sc_dispatch(h2, dests, n_rows):
    pieces, N, pc = h2.shape
    w = SC_WINDOW

    @functools.partial(pl.kernel, out_type=jax.ShapeDtypeStruct((pieces * n_rows, pc), h2.dtype),
                       mesh=_sc_mesh(), scratch_types=[], name="moe_dispatch_sc")
    def run(x_hbm, *refs):
        o_hbm = refs[-1]

        def body(x_vmem, *idx_vmem):
            for i_vmem in idx_vmem:
                pltpu.sync_copy(x_vmem, o_hbm.at[i_vmem.at[0]])

        pltpu.emit_pipeline(
            body,
            grid=(pieces * N // w,),
            in_specs=[pl.BlockSpec((w, pc), lambda i: (i, 0))]
            + [pl.BlockSpec((1, w), lambda i: (0, i))] * len(dests),
            out_specs=[],
            core_axis_name=("core", "subcore"),
            dimension_semantics=(pltpu.PARALLEL,),
        )(x_hbm, *refs[:-1])

    xs = run(h2.reshape(pieces * N, pc), *[_piece_index(d, pieces, n_rows) for d in dests])
    return xs.reshape(pieces, n_rows, pc)


def _sc_return(yb, dest):
    M = dest.shape[0]
    pieces, P, pc = yb.shape
    w = SC_WINDOW

    @functools.partial(pl.kernel, out_type=jax.ShapeDtypeStruct((pieces * M, pc), yb.dtype),
                       mesh=_sc_mesh(), scratch_types=[], name="moe_return_sc")
    def run(y_hbm, i_hbm, o_hbm):
        def body(i_vmem, o_vmem):
            pltpu.sync_copy(y_hbm.at[i_vmem.at[0]], o_vmem)

        pltpu.emit_pipeline(
            body,
            grid=(pieces * M // w,),
            in_specs=[pl.BlockSpec((1, w), lambda i: (0, i))],
            out_specs=[pl.BlockSpec((w, pc), lambda i: (i, 0))],
            core_axis_name=("core", "subcore"),
            dimension_semantics=(pltpu.PARALLEL,),
        )(i_hbm, o_hbm)

    return run(yb.reshape(pieces * P, pc), _piece_index(dest, pieces, P)).reshape(pieces, M, pc)


def _expert_kernel(tab_ref, xs_ref, w1_ref, w3_ref, w2_ref, y_ref, wf1, wf3, wf2, sem, *, layer):
    j = pl.program_id(0)
    jp = jnp.maximum(j - 1, 0)
    live = j < tab_ref[I_COUNT, 0]
    expert = tab_ref[I_EXPERT, j]
    new_expert = jnp.logical_or(j == 0, expert != tab_ref[I_EXPERT, jp])
    streams = ((w1_ref, wf1), (w3_ref, wf3), (w2_ref, wf2))

    def weight_copies(expert, s):
        return [pltpu.make_async_copy(w.at[layer, expert], buf.at[s], sem.at[s, k])
                for k, (w, buf) in enumerate(streams)]

    @pl.when(j == 0)
    def _():
        for cp in weight_copies(expert, 0):
            cp.start()

    @pl.when(jnp.logical_and(new_expert, live))
    def _():
        s = tab_ref[I_SLOT, j]
        for cp in weight_copies(expert, s):
            cp.wait()

        @pl.when(tab_ref[I_NEXT, j] >= 0)
        def _():
            for cp in weight_copies(tab_ref[I_NEXT, j], 1 - s):
                cp.start()

    bm = xs_ref.shape[1]
    for m in range(EXPERT_ROWS_STEP, bm + 1, EXPERT_ROWS_STEP):
        @pl.when(jnp.logical_and(live, tab_ref[I_ROWS, j] == m))
        def _(m=m):
            xb = _unpack_rows(xs_ref, m).astype(BF16)
            s = tab_ref[I_SLOT, j]
            h1 = _dot(xb, wf1[s].astype(BF16))
            h3 = _dot(xb, wf3[s].astype(BF16))
            a = (h1 * jax.nn.sigmoid(h1) * h3).astype(BF16)
            for p, words in enumerate(_pack_rows(_round_bf16(_dot(a, wf2[s].astype(BF16)))[1])):
                y_ref[p, 0:m, :] = words
                if m < bm:
                    y_ref[p, m:bm, :] = jnp.zeros((bm - m, words.shape[1]), U32)


def _experts(xs, items, w1, w3, w2, layer):
    pieces, P, pc = xs.shape
    _, E, D, DE = w1.shape
    bm = MOE_BLOCK
    n_items_max = P // bm
    rows = lambda j, tab: (0, tab[I_BLOCK, j], 0)
    return pl.pallas_call(
        functools.partial(_expert_kernel, layer=layer),
        grid_spec=pltpu.PrefetchScalarGridSpec(
            num_scalar_prefetch=1,
            grid=(n_items_max,),
            in_specs=[
                pl.BlockSpec((pieces, bm, pc), rows),
                pl.BlockSpec(memory_space=pl.ANY),
                pl.BlockSpec(memory_space=pl.ANY),
                pl.BlockSpec(memory_space=pl.ANY),
            ],
            out_specs=pl.BlockSpec((pieces, bm, pc), rows),
            scratch_shapes=[
                pltpu.VMEM((2, D, DE), F32),
                pltpu.VMEM((2, D, DE), F32),
                pltpu.VMEM((2, DE, D), F32),
                pltpu.SemaphoreType.DMA((2, 3)),
            ],
        ),
        out_shape=jax.ShapeDtypeStruct((pieces, P, pc), U32),
        compiler_params=_params(("arbitrary",)),
        name="moe_experts",
    )(items, xs, w1, w3, w2)


def _combine_kernel(x1_ref, route_ref, mod_ref, fg_ref, y0_ref, y1_ref, *rest):
    o_ref = rest[-1]
    x2 = x1_ref[...] + mod_ref[0][5:6] * _moe_mix(route_ref, y0_ref, y1_ref)
    ms = jnp.mean(x2 * x2, axis=-1, keepdims=True)
    o_ref[...] = x2 * lax.rsqrt(ms + NORM_EPS) * fg_ref[...]


def _final_combine(x1, route, modl, final_g, yk, seq, first_tile, prev_out):
    N, D = x1.shape
    tm = ROW_TILE
    nt = yk.shape[1] // 2 // tm
    tok = lambda i: (first_tile + i, 0)
    in_specs = [
        pl.BlockSpec((tm, D), tok),
        pl.BlockSpec((tm, LANES), tok),
        pl.BlockSpec((1, 6, D), lambda i: (((first_tile + i) * tm) // seq, 0, 0)),
        pl.BlockSpec((1, D), lambda i: (0, 0)),
        pl.BlockSpec((yk.shape[0], tm, ROW_PIECE), lambda i: (0, i, 0)),
        pl.BlockSpec((yk.shape[0], tm, ROW_PIECE), lambda i: (0, nt + i, 0)),
    ]
    args = [x1, route, modl, final_g, yk, yk]
    aliases = {}
    if prev_out is not None:
        in_specs.append(pl.BlockSpec(memory_space=pl.ANY))
        args.append(prev_out)
        aliases = {len(args) - 1: 0}
    return pl.pallas_call(
        _combine_kernel,
        grid=(nt,),
        in_specs=in_specs,
        out_specs=pl.BlockSpec((tm, D), tok),
        out_shape=jax.ShapeDtypeStruct((N, D), F32),
        input_output_aliases=aliases,
        compiler_params=_params(("arbitrary",)),
        name="moe_combine",
    )(*args)


ITEM_LANES = 256
I_BLOCK, I_EXPERT, I_NEXT, I_SLOT, I_COUNT, I_ROWS = 0, 1, 2, 3, 4, 5


def _dest_kernel(routet_ref, cnt_ref, d1_ref, d2_ref, d3_ref, items_ref):
    tm = routet_ref.shape[1]
    ne = N_EXPERTS
    bm = float(MOE_BLOCK)
    cnt = cnt_ref[...]
    r = lax.broadcasted_iota(jnp.int32, (ne, ne), 0)
    c = lax.broadcasted_iota(jnp.int32, (ne, ne), 1)
    lower = (r >= c).astype(F32)
    cumsum = lambda a: jnp.dot(lower, a, precision=HIGHEST, preferred_element_type=F32)
    padded = jnp.floor((cnt + (bm - 1.0)) * (1.0 / bm)) * bm
    pend = cumsum(padded)
    pstart = pend - padded
    npad = padded - cnt
    pad_end = cumsum(npad)
    pad_start = pad_end - npad

    tile = lambda a, n: jnp.concatenate([a] * (n // LANES), axis=1)
    expert = lax.broadcasted_iota(jnp.int32, (ne, tm), 0).astype(F32)
    rec = routet_ref[...]
    pstart_t = tile(pstart, tm)

    def sorted_row(row_id, row_rank):
        sel = expert == rec[row_id:row_id + 1, :]
        return jnp.sum(jnp.where(sel, pstart_t, 0.0), axis=0, keepdims=True) + rec[row_rank:row_rank + 1, :]

    dest1 = sorted_row(R_ID1, R_RANK1)
    dest2 = sorted_row(R_ID2, R_RANK2)
    t = (lax.broadcasted_iota(jnp.int32, (1, tm), 1) + pl.program_id(0) * tm).astype(F32)
    ps_t, pe_t = tile(pad_start, tm), tile(pad_end, tm)
    in_e = jnp.logical_and(ps_t <= t, t < pe_t)
    pad_row = jnp.sum(jnp.where(in_e, tile(pstart + cnt, tm) + (t - ps_t), 0.0), axis=0, keepdims=True)
    dest3 = jnp.where(t < pe_t[ne - 1:ne, :], pad_row, dest1)
    d1_ref[0] = dest1.astype(jnp.int32)
    d2_ref[0] = dest2.astype(jnp.int32)
    d3_ref[0] = dest3.astype(jnp.int32)

    @pl.when(pl.program_id(0) == 0)
    def _():
        nl = ITEM_LANES
        pend_i = tile(pend, nl)
        n_items = pend_i[ne - 1:ne, :] * (1.0 / bm)
        blk = jnp.minimum(lax.broadcasted_iota(jnp.int32, (1, nl), 1).astype(F32), n_items - 1.0)
        erow = lax.broadcasted_iota(jnp.int32, (ne, nl), 0).astype(F32)
        ie = jnp.sum(jnp.where(pend_i <= blk * bm, 1.0, 0.0), axis=0, keepdims=True)
        nonempty = tile(cnt, nl) > 0.0
        order = jnp.sum(jnp.where(jnp.logical_and(nonempty, erow < ie), 1.0, 0.0), axis=0, keepdims=True)
        nxt = jnp.min(jnp.where(jnp.logical_and(nonempty, erow > ie), erow, float(ne)), axis=0, keepdims=True)
        nxt = jnp.where(nxt == float(ne), -1.0, nxt)
        slot = order - 2.0 * jnp.floor(order * 0.5)
        own_end = jnp.sum(jnp.where(erow == ie, tile(pstart + cnt, nl), 0.0), axis=0, keepdims=True)
        step = float(EXPERT_ROWS_STEP)
        rows = jnp.minimum(jnp.ceil((own_end - blk * bm) * (1.0 / step)) * step, bm)
        row = lax.broadcasted_iota(jnp.int32, (8, nl), 0)
        tab = jnp.where(row == I_BLOCK, blk, 0.0)
        tab = jnp.where(row == I_EXPERT, ie, tab)
        tab = jnp.where(row == I_NEXT, nxt, tab)
        tab = jnp.where(row == I_SLOT, slot, tab)
        tab = jnp.where(row == I_COUNT, n_items, tab)
        tab = jnp.where(row == I_ROWS, rows, tab)
        items_ref[...] = tab.astype(jnp.int32)


def _destinations(route_t, cnt, n_blocks):
    N = route_t.shape[1]
    tm = 2048 if N % 2048 == 0 else ROUTE_SUB
    out = jax.ShapeDtypeStruct((N // tm, 1, tm), jnp.int32)
    d1, d2, d3, items = pl.pallas_call(
        _dest_kernel,
        grid=(N // tm,),
        in_specs=[pl.BlockSpec((8, tm), lambda i: (0, i)),
                  pl.BlockSpec((N_EXPERTS, LANES), lambda i: (0, 0))],
        out_specs=[pl.BlockSpec((1, 1, tm), lambda i: (i, 0, 0))] * 3
        + [pl.BlockSpec((8, ITEM_LANES), lambda i: (0, 0))],
        out_shape=[out, out, out, jax.ShapeDtypeStruct((8, ITEM_LANES), jnp.int32)],
        compiler_params=_params(("arbitrary",)),
        name="moe_dest",
    )(route_t, cnt)
    return d1.reshape(N), d2.reshape(N), d3.reshape(N), items


def kernel(x, c, mod_w, mod_b, norm1_g, w_in, gate_w2, gate_b, gla_norm_g, conv_w, w_gla_out,
           w_conv_out, w_out, norm2_g, router_group_w, router_group_b, router_expert_w,
           router_expert_b, expert_w1, expert_w3, expert_w2, final_norm_g):
    B, S, D = x.shape
    L = mod_w.shape[0]
    N = B * S
    dk = gate_w2.shape[2]
    rank = gate_w2.shape[1]
    dv = gla_norm_g.shape[1]
    n_slots = N * 2
    n_blocks = n_slots // MOE_BLOCK + N_EXPERTS
    assert S % GLA_TILE == 0 and S % MIX_TILE == 0 and S % IN_TILE == 0 and S % IN_TILE_PLAIN == 0
    assert N % (FINAL_PARTS * ROW_TILE) == 0
    assert n_slots % MOE_BLOCK == 0 and MOE_BLOCK & (MOE_BLOCK - 1) == 0 and n_blocks <= ITEM_LANES
    assert N_EXPERTS * (MOE_BLOCK - 1) <= N
    assert N_GROUPS + N_EXPERTS <= LANES and EXPERTS_PER_GROUP == 8 and 3 * rank <= LANES and D % (2 * ROW_PIECE) == 0

    mod = [_modulation(c, mod_w, mod_b, l)[0] for l in range(L)]
    xf = x.reshape(N, D)
    o_gd = 2 * dk + dv
    o_r = o_gd + rank
    o_b = o_r + dv
    for l in range(L):
        modl = mod[l].reshape(B, 6, D)
        wl = w_in[l]
        wa = jnp.concatenate([wl[:, :o_gd], wl[:, o_r:o_b]], axis=1).astype(BF16)
        wb = wl[:, o_b:].astype(BF16)
        wgd = wl[:, o_gd:o_r]
        wg = jnp.pad(jnp.concatenate([wgd, wgd, wgd], axis=1), ((0, 0), (0, LANES - 3 * rank))).astype(BF16)
        g_hi = gate_w2[l].astype(BF16)
        g_lo = (gate_w2[l] - g_hi.astype(F32)).astype(BF16)
        gw2 = jnp.pad(jnp.concatenate([g_hi, g_hi, g_lo], axis=0), ((0, LANES - 3 * rank), (0, 0)))
        if l == 0:
            za, zb, zg = _in_projection(xf, modl, norm1_g[l].reshape(1, D), conv_w[l], wa, wb, wg, S)
        else:
            xf, za, zb, zg = _in_projection(x1, modl, norm1_g[l].reshape(1, D), conv_w[l], wa, wb, wg, S,
                                            combine=(route, yk, mod[l - 1].reshape(B, 6, D)))
        og = _gla(za, zg, gw2, gate_b[l].reshape(1, dk), gla_norm_g[l].reshape(1, dv), B, S, dk, dv, rank)
        wr = jnp.pad(jnp.concatenate([router_group_w[l], router_expert_w[l]], axis=1),
                     ((0, 0), (0, LANES - N_GROUPS - N_EXPERTS)))
        wr_hi = wr.astype(BF16)
        wr = jnp.concatenate([wr_hi, (wr - wr_hi.astype(F32)).astype(BF16)], axis=1)
        br = jnp.pad(jnp.concatenate([router_group_b[l], router_expert_b[l]]),
                     (0, LANES - N_GROUPS - N_EXPERTS)).reshape(1, LANES)
        x1, h2, logits = _mixer_out(
            og, zb, xf, modl, w_gla_out[l].astype(BF16), w_conv_out[l].astype(BF16),
            w_out[l].astype(BF16), norm2_g[l].reshape(1, D), wr, br, B, S)
        route_t, route, cnt = _routing(logits)
        dest1, dest2, dest3, items = _destinations(route_t, cnt, n_blocks)
        xs = _sc_dispatch(h2, (dest1, dest2, dest3), n_blocks * MOE_BLOCK)
        yb = _experts(xs, items, expert_w1, expert_w3, expert_w2, l)
        if l < L - 1:
            yk = _sc_return(yb, jnp.concatenate([dest1, dest2]))
    out = None
    part = N // FINAL_PARTS
    for p in range(FINAL_PARTS):
        tok = slice(p * part, (p + 1) * part)
        yk = _sc_return(yb, jnp.concatenate([dest1[tok], dest2[tok]]))
        out = _final_combine(x1, route, modl, final_norm_g.reshape(1, D), yk, S,
                             p * part // ROW_TILE, out)
    return out.reshape(B, S, D)
```

```python
import functools

import jax
import jax.numpy as jnp
from jax import lax
from jax.experimental import pallas as pl
from jax.experimental.pallas import tpu as pltpu
from jax.experimental.pallas import tpu_sc as plsc

F32 = jnp.float32
BF16 = jnp.bfloat16
HIGHEST = lax.Precision.HIGHEST

GLA_HEADS = 4
GATE_TAU = 16.0
GLA_CHUNK = 64
N_GROUPS = 8
EXPERTS_PER_GROUP = 8
N_EXPERTS = N_GROUPS * EXPERTS_PER_GROUP
NORM_EPS = 1e-6

LANES = 128
VMEM_LIMIT_BYTES = 56 * 1024 * 1024

IN_TILE = 512
IN_TILE_PLAIN = 1024
GLA_TILE = 1024
MIX_TILE = 1024
MIX_SUB = 1024
MIX_COLS = 256
ROUTE_SUB = 256
MOE_BLOCK = 512
EXPERT_ROWS_STEP = 128
ROW_TILE = 1024
FINAL_PARTS = 4
ROW_PIECE = 256
U32 = jnp.uint32

R_ID1, R_ID2, R_W1, R_W2, R_RANK1, R_RANK2 = 0, 1, 2, 3, 4, 5


def _dot(a, b):
    return jnp.dot(a, b, preferred_element_type=F32)


def _round_bf16(x):
    xb = x.astype(BF16)
    return xb, xb.astype(F32)


def _pack_rows(xr):
    half = xr.shape[1] // 2
    out = []
    for p in range(half // ROW_PIECE):
        lo = xr[:, p * ROW_PIECE:(p + 1) * ROW_PIECE]
        hi = xr[:, half + p * ROW_PIECE:half + (p + 1) * ROW_PIECE]
        out.append((pltpu.bitcast(lo, U32) >> 16) | pltpu.bitcast(hi, U32))
    return out


def _unpack_rows(ref, rows=None):
    rows = ref.shape[1] if rows is None else rows
    words = [ref[p, 0:rows, :] for p in range(ref.shape[0])]
    lo = [pltpu.bitcast(w << 16, F32) for w in words]
    hi = [pltpu.bitcast(w & jnp.uint32(0xFFFF0000), F32) for w in words]
    return jnp.concatenate(lo + hi, axis=-1)


def _const_spec(shape):
    nd = len(shape)
    return pl.BlockSpec(shape, lambda *_: (0,) * nd, pipeline_mode=pl.Buffered(1))


def _params(sem):
    return pltpu.CompilerParams(dimension_semantics=sem, vmem_limit_bytes=VMEM_LIMIT_BYTES)


MOD_ROWS = 256


def _mod_kernel(c_ref, w_ref, b_ref, o_ref):
    c = c_ref[...]
    sc = c * jax.nn.sigmoid(c)
    s_hi, s_r = _round_bf16(sc)
    w = w_ref[0]
    w_hi, w_r = _round_bf16(w)
    s_lo = (sc - s_r).astype(BF16)
    part = _dot(s_hi, w_hi) + _dot(s_lo, w_hi) + _dot(s_hi, (w - w_r).astype(BF16))

    @pl.when(pl.program_id(1) == 0)
    def _():
        o_ref[0] = part + b_ref[0]

    @pl.when(pl.program_id(1) > 0)
    def _():
        o_ref[0] += part


def _modulation(c, mod_w, mod_b, layer):
    L, D, D6 = mod_w.shape
    B = c.shape[0]
    kb = MOD_ROWS
    return pl.pallas_call(
        _mod_kernel,
        grid=(1, D // kb),
        in_specs=[
            pl.BlockSpec((B, kb), lambda l, k: (0, k)),
            pl.BlockSpec((1, kb, D6), lambda l, k: (layer, k, 0)),
            pl.BlockSpec((1, 1, D6), lambda l, k: (layer, 0, 0)),
        ],
        out_specs=pl.BlockSpec((1, B, D6), lambda l, k: (0, 0, 0)),
        out_shape=jax.ShapeDtypeStruct((1, B, D6), F32),
        compiler_params=_params(("arbitrary", "arbitrary")),
        name="adaln_mod",
    )(c, mod_w, mod_b.reshape(L, 1, D6))


CONV_COLS = 256


def _moe_mix(route_ref, y0_ref, y1_ref):
    rec = route_ref[...]
    w1 = rec[:, R_W1:R_W1 + 1]
    w2 = rec[:, R_W2:R_W2 + 1]
    return w1 * _unpack_rows(y0_ref) + w2 * _unpack_rows(y1_ref)


def _inproj_kernel(*refs, seq, fuse_combine):
    if fuse_combine:
        (x_ref, route_ref, y0_ref, y1_ref, modp_ref, mod_ref, g_ref, cw_ref, wa_ref, wb_ref, wg_ref,
         xo_ref, za_ref, zb_ref, zg_ref, carry_ref) = refs
        x = x_ref[...] + modp_ref[0][5:6] * _moe_mix(route_ref, y0_ref, y1_ref)
        xo_ref[...] = x
    else:
        (x_ref, mod_ref, g_ref, cw_ref, wa_ref, wb_ref, wg_ref, za_ref, zb_ref, zg_ref, carry_ref) = refs
        x = x_ref[...]
    tm, d = x.shape

    @pl.when((pl.program_id(0) * tm) % seq == 0)
    def _():
        carry_ref[...] = jnp.zeros_like(carry_ref)

    ms = jnp.mean(x * x, axis=-1, keepdims=True)
    m = mod_ref[0]
    h = (x * lax.rsqrt(ms + NORM_EPS) * (g_ref[...] * (1.0 + m[1:2])) + m[0:1]).astype(BF16)
    for j in range(za_ref.shape[1] // d):
        za_ref[:, j * d:(j + 1) * d] = _dot(h, wa_ref[:, j * d:(j + 1) * d]).astype(BF16)
    zg_ref[...] = _dot(h, wg_ref[...])
    for j in range(1, zb_ref.shape[1] // d):
        zb_ref[:, j * d:(j + 1) * d] = _dot(h, wb_ref[:, (j + 2) * d:(j + 3) * d]).astype(BF16)

    w = CONV_COLS
    rowi = lax.broadcasted_iota(jnp.int32, (tm, w), 0)
    for j in range(d // w):
        cols = slice(j * w, (j + 1) * w)
        cb = _dot(h, wb_ref[:, j * w:(j + 1) * w])
        cc = _dot(h, wb_ref[:, d + j * w:d + (j + 1) * w])
        ch = _dot(h, wb_ref[:, 2 * d + j * w:2 * d + (j + 1) * w])
        u = cc * ch
        prev = carry_ref[:, cols]
        u1 = jnp.where(rowi == 0, prev[7:8], pltpu.roll(u, 1, 0))
        u2 = jnp.where(rowi == 0, prev[6:7], jnp.where(rowi == 1, prev[7:8], pltpu.roll(u, 2, 0)))
        carry_ref[:, cols] = u[tm - 8:tm]
        conv = cw_ref[0:1, cols] * u2 + cw_ref[1:2, cols] * u1 + cw_ref[2:3, cols] * u
        zb_ref[:, cols] = (cb * conv).astype(BF16)


def _in_projection(x, modl, norm_g, conv_w, wa, wb, wg, seq, combine=None):
    N, D = x.shape
    tm = IN_TILE_PLAIN if combine is None else IN_TILE
    nt = N // tm
    ca, cb = wa.shape[1], wb.shape[1] - 2 * D
    tok = lambda i: (i, 0)
    per_batch = lambda i: ((i * tm) // seq, 0, 0)
    in_specs = [pl.BlockSpec((tm, D), tok)]
    args = [x]
    out_specs, out_shape = [], []
    if combine is not None:
        route, yk, mod_prev = combine
        piece_blk = (yk.shape[0], tm, ROW_PIECE)
        in_specs += [
            pl.BlockSpec((tm, LANES), tok),
            pl.BlockSpec(piece_blk, lambda i: (0, i, 0)),
            pl.BlockSpec(piece_blk, lambda i: (0, nt + i, 0)),
            pl.BlockSpec((1, 6, D), per_batch),
        ]
        args += [route, yk, yk, mod_prev]
        out_specs.append(pl.BlockSpec((tm, D), tok))
        out_shape.append(jax.ShapeDtypeStruct((N, D), F32))
    in_specs += [
        pl.BlockSpec((1, 6, D), per_batch),
        _const_spec((1, D)),
        _const_spec(conv_w.shape),
        _const_spec((D, ca)),
        _const_spec(wb.shape),
        _const_spec((D, LANES)),
    ]
    args += [modl, norm_g, conv_w, wa, wb, wg]
    out_specs += [
        pl.BlockSpec((tm, ca), tok),
        pl.BlockSpec((tm, cb), tok),
        pl.BlockSpec((tm, LANES), tok),
    ]
    out_shape += [
        jax.ShapeDtypeStruct((N, ca), BF16),
        jax.ShapeDtypeStruct((N, cb), BF16),
        jax.ShapeDtypeStruct((N, LANES), F32),
    ]
    kern = functools.partial(_inproj_kernel, seq=seq, fuse_combine=combine is not None)
    return pl.pallas_call(
        kern,
        grid=(nt,),
        in_specs=in_specs,
        out_specs=out_specs,
        out_shape=out_shape,
        scratch_shapes=[pltpu.VMEM((8, D), F32)],
        compiler_params=_params(("arbitrary",)),
        name="in_projection",
    )(*args)


def _log_sigmoid(x):
    return jnp.minimum(x, 0.0) - jnp.log(1.0 + jnp.exp(-jnp.abs(x)))


def _gla_kernel(za_ref, zg_ref, gw2_ref, gb_ref, ng_ref, o_ref, st_ref, lg_ref, *, dk, dv, rank):
    heads = GLA_HEADS
    dkh, dvh = dk // heads, dv // heads
    c = GLA_CHUNK
    ts = za_ref.shape[0]

    @pl.when(pl.program_id(1) == 0)
    def _():
        st_ref[...] = jnp.zeros_like(st_ref)

    zg = zg_ref[...]
    zg_hi = zg.astype(BF16)
    zg_lo = (zg - zg_hi.astype(F32)).astype(BF16)
    lane = lax.broadcasted_iota(jnp.int32, zg.shape, 1)
    lhs = jnp.where(jnp.logical_and(lane >= rank, lane < 2 * rank), zg_lo, zg_hi)
    pre = _dot(lhs, gw2_ref[...]) + gb_ref[...]
    lg = _log_sigmoid(pre) * (1.0 / GATE_TAU)
    lg_hi = lg.astype(BF16)
    lg_ref[:, 0:dk] = lg_hi
    lg_ref[:, dk:2 * dk] = (lg - lg_hi.astype(F32)).astype(BF16)

    row = lax.broadcasted_iota(jnp.int32, (c, c), 0)
    col = lax.broadcasted_iota(jnp.int32, (c, c), 1)
    causal = row >= col
    tril = causal.astype(BF16)
    qscale = dkh ** -0.5

    for ci in range(ts // c):
        rows = slice(ci * c, (ci + 1) * c)
        b_two = _dot(tril, lg_ref[rows, :])
        b_all = b_two[:, 0:dk] + b_two[:, dk:2 * dk]
        for hd in range(heads):
            ks = slice(hd * dkh, (hd + 1) * dkh)
            q = za_ref[rows, hd * dkh:(hd + 1) * dkh].astype(F32) * qscale
            k = za_ref[rows, dk + hd * dkh:dk + (hd + 1) * dkh].astype(F32)
            v = za_ref[rows, 2 * dk + hd * dvh:2 * dk + (hd + 1) * dvh]
            r = za_ref[rows, 2 * dk + dv + hd * dvh:2 * dk + dv + (hd + 1) * dvh].astype(F32)
            b = b_all[:, ks]
            b_last = b[c - 1:c, :]
            q_t = (q * jnp.exp(b)).astype(BF16)
            k_t = (k * jnp.exp(-b)).astype(BF16)
            k_s = (k * jnp.exp(b_last - b)).astype(BF16)
            decay = jnp.exp(b_last)
            attn = lax.dot_general(q_t, k_t, (((1,), (1,)), ((), ())), preferred_element_type=F32)
            attn = jnp.where(causal, attn, 0.0).astype(BF16)
            st = st_ref[hd]
            o = _dot(attn, v) + lax.dot_general(
                q_t, st.astype(BF16), (((1,), (1,)), ((), ())), preferred_element_type=F32)
            upd = lax.dot_general(v, k_s, (((0,), (0,)), ((), ())), preferred_element_type=F32)
            st_ref[hd] = st * decay + upd
            ms = jnp.mean(o * o, axis=-1, keepdims=True)
            on = o * lax.rsqrt(ms + NORM_EPS) * ng_ref[:, hd * dvh:(hd + 1) * dvh]
            o_ref[rows, hd * dvh:(hd + 1) * dvh] = (on * (r * jax.nn.sigmoid(r))).astype(BF16)


def _gla(za, zg, gw2, gb, ng, batch, seq, dk, dv, rank):
    N = za.shape[0]
    ts = GLA_TILE
    ns = seq // ts
    heads = GLA_HEADS
    kern = functools.partial(_gla_kernel, dk=dk, dv=dv, rank=rank)
    return pl.pallas_call(
        kern,
        grid=(batch, ns),
        in_specs=[
            pl.BlockSpec((ts, za.shape[1]), lambda b, s: (b * ns + s, 0)),
            pl.BlockSpec((ts, LANES), lambda b, s: (b * ns + s, 0)),
            _const_spec((LANES, dk)),
            _const_spec((1, dk)),
            _const_spec((1, dv)),
        ],
        out_specs=pl.BlockSpec((ts, dv), lambda b, s: (b * ns + s, 0)),
        out_shape=jax.ShapeDtypeStruct((N, dv), BF16),
        scratch_shapes=[
            pltpu.VMEM((heads, dv // heads, dk // heads), F32),
            pltpu.VMEM((ts, 2 * dk), BF16),
        ],
        compiler_params=_params(("arbitrary", "arbitrary")),
        name="gla",
    )(za, zg, gw2, gb, ng)


def _mixout_kernel(og_ref, zb_ref, x_ref, mod_ref, wga_ref, wco_ref, wo_ref, n2_ref,
                   wr_ref, br_ref, x1_ref, h2_ref, logit_ref):
    tm, d = x_ref.shape
    m = mod_ref[0]
    sub = MIX_SUB
    gain2 = n2_ref[...] * (1.0 + m[4:5])
    for r0 in range(0, tm, sub):
        rs = slice(r0, r0 + sub)
        w = MIX_COLS
        ys = []
        for j in range(d // w):
            cols = slice(j * w, (j + 1) * w)
            ga = zb_ref[rs, d + j * w:d + (j + 1) * w].astype(F32)
            gc = zb_ref[rs, 2 * d + j * w:2 * d + (j + 1) * w].astype(F32)
            y_conv = _dot(zb_ref[rs, 0:d], wco_ref[:, cols])
            y_gla = _dot(og_ref[rs, :], wga_ref[:, cols])
            ys.append((jax.nn.sigmoid(ga) * y_gla + jax.nn.sigmoid(gc) * y_conv).astype(BF16))
        y = _dot(jnp.concatenate(ys, axis=1), wo_ref[...])
        x1 = x_ref[rs, :] + m[2:3] * y
        x1_ref[rs, :] = x1

        ms = jnp.mean(x1 * x1, axis=-1, keepdims=True)
        h2 = x1 * lax.rsqrt(ms + NORM_EPS) * gain2 + m[3:4]
        h_hi, h_r = _round_bf16(h2)
        for p, words in enumerate(_pack_rows(h_r)):
            h2_ref[p, rs, :] = words

        h_lo = (h2 - h_r).astype(BF16)
        two = _dot(h_hi, wr_ref[...])
        logits = two[:, 0:LANES] + two[:, LANES:2 * LANES] + _dot(h_lo, wr_ref[:, 0:LANES]) + br_ref[...]
        logit_ref[rs, :] = logits


def _mixer_out(og, zb, x, modl, wga, wco, wo, n2g, wr, br, batch, seq):
    N, D = x.shape
    tm = MIX_TILE
    ns = seq // tm
    tok = lambda b, s: (b * ns + s, 0)
    return pl.pallas_call(
        _mixout_kernel,
        grid=(batch, ns),
        in_specs=[
            pl.BlockSpec((tm, D), tok),
            pl.BlockSpec((tm, zb.shape[1]), tok),
            pl.BlockSpec((tm, D), tok),
            pl.BlockSpec((1, 6, D), lambda b, s: (b, 0, 0)),
            _const_spec((D, D)),
            _const_spec((D, D)),
            _const_spec((D, D)),
            _const_spec((1, D)),
            _const_spec((D, 2 * LANES)),
            _const_spec((1, LANES)),
        ],
        out_specs=[
            pl.BlockSpec((tm, D), tok),
            pl.BlockSpec((D // 2 // ROW_PIECE, tm, ROW_PIECE), lambda b, s: (0, b * ns + s, 0)),
            pl.BlockSpec((tm, LANES), tok),
        ],
        out_shape=[
            jax.ShapeDtypeStruct((N, D), F32),
            jax.ShapeDtypeStruct((D // 2 // ROW_PIECE, N, ROW_PIECE), U32),
            jax.ShapeDtypeStruct((N, LANES), F32),
        ],
        compiler_params=_params(("arbitrary", "arbitrary")),
        name="mixer_out",
    )(og, zb, x, modl, wga, wco, wo, n2g, wr, br)


def _route_kernel(logit_ref, routet_ref, route_ref, cnt_ref, run_ref):
    tm = logit_ref.shape[0]
    sub = ROUTE_SUB
    rows8 = EXPERTS_PER_GROUP

    @pl.when(pl.program_id(0) == 0)
    def _():
        run_ref[...] = jnp.zeros_like(run_ref)

    sub8 = lax.broadcasted_iota(jnp.int32, (rows8, sub), 0)
    erow = lax.broadcasted_iota(jnp.int32, (N_EXPERTS, sub), 0)
    tr = lax.broadcasted_iota(jnp.int32, (sub, sub), 0)
    tc = lax.broadcasted_iota(jnp.int32, (sub, sub), 1)
    earlier = (tr < tc).astype(BF16)
    ones = jnp.ones((sub, LANES), BF16)
    neg = -jnp.inf
    run = run_ref[...]
    for r0 in range(0, tm, sub):
        lt = logit_ref[r0:r0 + sub, :].T
        gl = lt[0:N_GROUPS, :]
        gmax = jnp.max(gl, axis=0, keepdims=True)
        gsum = jnp.sum(jnp.exp(gl - gmax), axis=0, keepdims=True)
        g_w = 1.0 / gsum
        g_idx = jnp.min(jnp.where(gl == gmax, sub8, N_GROUPS), axis=0, keepdims=True)
        el = lt[N_GROUPS:N_GROUPS + rows8, :]
        for g in range(1, N_GROUPS):
            el = jnp.where(g_idx == g, lt[N_GROUPS + g * rows8:N_GROUPS + (g + 1) * rows8, :], el)
        e1 = jnp.max(el, axis=0, keepdims=True)
        i1 = jnp.min(jnp.where(el == e1, sub8, rows8), axis=0, keepdims=True)
        el2 = jnp.where(sub8 == i1, neg, el)
        e2 = jnp.max(el2, axis=0, keepdims=True)
        i2 = jnp.min(jnp.where(el2 == e2, sub8, rows8), axis=0, keepdims=True)
        ratio = jnp.exp(e2 - e1)
        w1 = g_w / (1.0 + ratio)
        w2 = g_w * ratio / (1.0 + ratio)
        id1 = g_idx * rows8 + i1
        id2 = g_idx * rows8 + i2

        oh1 = erow == id1
        oh2 = erow == id2
        oh1b = jnp.where(oh1, 1.0, 0.0).astype(BF16)
        oh2b = jnp.where(oh2, 1.0, 0.0).astype(BF16)
        tot1 = _dot(oh1b, ones)
        tot2 = _dot(oh2b, ones)
        base1 = jnp.concatenate([run] * (sub // LANES), axis=1)
        base2 = jnp.concatenate([run + tot1] * (sub // LANES), axis=1)
        c1 = _dot(oh1b, earlier) + base1
        c2 = _dot(oh2b, earlier) + base2
        rank1 = jnp.sum(jnp.where(oh1, c1, 0.0), axis=0, keepdims=True)
        rank2 = jnp.sum(jnp.where(oh2, c2, 0.0), axis=0, keepdims=True)
        run = run + tot1 + tot2

        rec = jnp.where(sub8 == R_ID1, id1.astype(F32), 0.0)
        rec = jnp.where(sub8 == R_ID2, id2.astype(F32), rec)
        rec = jnp.where(sub8 == R_W1, w1, rec)
        rec = jnp.where(sub8 == R_W2, w2, rec)
        rec = jnp.where(sub8 == R_RANK1, rank1, rec)
        rec = jnp.where(sub8 == R_RANK2, rank2, rec)
        routet_ref[:, r0:r0 + sub] = rec
        rec_full = jnp.concatenate([rec, jnp.zeros((LANES - rows8, sub), F32)], axis=0)
        route_ref[r0:r0 + sub, :] = rec_full.T
    run_ref[...] = run
    cnt_ref[...] = run


def _routing(logits):
    N = logits.shape[0]
    tm = 2048 if N % 2048 == 0 else ROUTE_SUB
    return pl.pallas_call(
        _route_kernel,
        grid=(N // tm,),
        in_specs=[pl.BlockSpec((tm, LANES), lambda i: (i, 0))],
        out_specs=[
            pl.BlockSpec((8, tm), lambda i: (0, i)),
            pl.BlockSpec((tm, LANES), lambda i: (i, 0)),
            pl.BlockSpec((N_EXPERTS, LANES), lambda i: (0, 0)),
        ],
        out_shape=[
            jax.ShapeDtypeStruct((8, N), F32),
            jax.ShapeDtypeStruct((N, LANES), F32),
            jax.ShapeDtypeStruct((N_EXPERTS, LANES), F32),
        ],
        scratch_shapes=[pltpu.VMEM((N_EXPERTS, LANES), F32)],
        compiler_params=_params(("arbitrary",)),
        name="moe_route",
    )(logits)


SC_WINDOW = 128


def _sc_mesh():
    return plsc.VectorSubcoreMesh(core_axis_name="core", subcore_axis_name="subcore")


def _piece_index(rows, pieces, n_rows):
    return (jnp.arange(pieces, dtype=jnp.int32)[:, None] * n_rows + rows[None, :]).reshape(1, -1)


def _sc_dispatch(h2, dests, n_rows):
    pieces, N, pc = h2.shape
    w = SC_WINDOW

    @functools.partial(pl.kernel, out_type=jax.ShapeDtypeStruct((pieces * n_rows, pc), h2.dtype),
                       mesh=_sc_mesh(), scratch_types=[], name="moe_dispatch_sc")
    def run(x_hbm, *refs):
        o_hbm = refs[-1]

        def body(x_vmem, *idx_vmem):
            for i_vmem in idx_vmem:
                pltpu.sync_copy(x_vmem, o_hbm.at[i_vmem.at[0]])

        pltpu.emit_pipeline(
            body,
            grid=(pieces * N // w,),
            in_specs=[pl.BlockSpec((w, pc), lambda i: (i, 0))]
            + [pl.BlockSpec((1, w), lambda i: (0, i))] * len(dests),
            out_specs=[],
            core_axis_name=("core", "subcore"),
            dimension_semantics=(pltpu.PARALLEL,),
        )(x_hbm, *refs[:-1])

    xs = run(h2.reshape(pieces * N, pc), *[_piece_index(d, pieces, n_rows) for d in dests])
    return xs.reshape(pieces, n_rows, pc)


def _sc_return(yb, dest):
    M = dest.shape[0]
    pieces, P, pc = yb.shape
    w = SC_WINDOW

    @functools.partial(pl.kernel, out_type=jax.ShapeDtypeStruct((pieces * M, pc), yb.dtype),
                       mesh=_sc_mesh(), scratch_types=[], name="moe_return_sc")
    def run(y_hbm, i_hbm, o_hbm):
        def body(i_vmem, o_vmem):
            pltpu.sync_copy(y_hbm.at[i_vmem.at[0]], o_vmem)

        pltpu.emit_pipeline(
            body,
            grid=(pieces * M // w,),
            in_specs=[pl.BlockSpec((1, w), lambda i: (0, i))],
            out_specs=[pl.BlockSpec((w, pc), lambda i: (i, 0))],
            core_axis_name=("core", "subcore"),
            dimension_semantics=(pltpu.PARALLEL,),
        )(i_hbm, o_hbm)

    return run(yb.reshape(pieces * P, pc), _piece_index(dest, pieces, P)).reshape(pieces, M, pc)


def _expert_kernel(tab_ref, xs_ref, w1_ref, w3_ref, w2_ref, y_ref, wf1, wf3, wf2, sem, *, layer):
    j = pl.program_id(0)
    jp = jnp.maximum(j - 1, 0)
    live = j < tab_ref[I_COUNT, 0]
    expert = tab_ref[I_EXPERT, j]
    new_expert = jnp.logical_or(j == 0, expert != tab_ref[I_EXPERT, jp])
    streams = ((w1_ref, wf1), (w3_ref, wf3), (w2_ref, wf2))

    def weight_copies(expert, s):
        return [pltpu.make_async_copy(w.at[layer, expert], buf.at[s], sem.at[s, k])
                for k, (w, buf) in enumerate(streams)]

    @pl.when(j == 0)
    def _():
        for cp in weight_copies(expert, 0):
            cp.start()

    @pl.when(jnp.logical_and(new_expert, live))
    def _():
        s = tab_ref[I_SLOT, j]
        for cp in weight_copies(expert, s):
            cp.wait()

        @pl.when(tab_ref[I_NEXT, j] >= 0)
        def _():
            for cp in weight_copies(tab_ref[I_NEXT, j], 1 - s):
                cp.start()

    bm = xs_ref.shape[1]
    for m in range(EXPERT_ROWS_STEP, bm + 1, EXPERT_ROWS_STEP):
        @pl.when(jnp.logical_and(live, tab_ref[I_ROWS, j] == m))
        def _(m=m):
            xb = _unpack_rows(xs_ref, m).astype(BF16)
            s = tab_ref[I_SLOT, j]
            w = MIX_COLS
            acts = []
            for c in range(wf1.shape[2] // w):
                h1 = _dot(xb, wf1[s, :, c * w:(c + 1) * w].astype(BF16))
                h3 = _dot(xb, wf3[s, :, c * w:(c + 1) * w].astype(BF16))
                acts.append((h1 * jax.nn.sigmoid(h1) * h3).astype(BF16))
            a = jnp.concatenate(acts, axis=1)
            for p, words in enumerate(_pack_rows(_round_bf16(_dot(a, wf2[s].astype(BF16)))[1])):
                y_ref[p, 0:m, :] = words
                if m < bm:
                    y_ref[p, m:bm, :] = jnp.zeros((bm - m, words.shape[1]), U32)


def _experts(xs, items, w1, w3, w2, layer):
    pieces, P, pc = xs.shape
    _, E, D, DE = w1.shape
    bm = MOE_BLOCK
    n_items_max = P // bm
    rows = lambda j, tab: (0, tab[I_BLOCK, j], 0)
    return pl.pallas_call(
        functools.partial(_expert_kernel, layer=layer),
        grid_spec=pltpu.PrefetchScalarGridSpec(
            num_scalar_prefetch=1,
            grid=(n_items_max,),
            in_specs=[
                pl.BlockSpec((pieces, bm, pc), rows),
                pl.BlockSpec(memory_space=pl.ANY),
                pl.BlockSpec(memory_space=pl.ANY),
                pl.BlockSpec(memory_space=pl.ANY),
            ],
            out_specs=pl.BlockSpec((pieces, bm, pc), rows),
            scratch_shapes=[
                pltpu.VMEM((2, D, DE), F32),
                pltpu.VMEM((2, D, DE), F32),
                pltpu.VMEM((2, DE, D), F32),
                pltpu.SemaphoreType.DMA((2, 3)),
            ],
        ),
        out_shape=jax.ShapeDtypeStruct((pieces, P, pc), U32),
        compiler_params=_params(("arbitrary",)),
        name="moe_experts",
    )(items, xs, w1, w3, w2)


def _combine_kernel(x1_ref, route_ref, mod_ref, fg_ref, y0_ref, y1_ref, *rest):
    o_ref = rest[-1]
    x2 = x1_ref[...] + mod_ref[0][5:6] * _moe_mix(route_ref, y0_ref, y1_ref)
    ms = jnp.mean(x2 * x2, axis=-1, keepdims=True)
    o_ref[...] = x2 * lax.rsqrt(ms + NORM_EPS) * fg_ref[...]


def _final_combine(x1, route, modl, final_g, yk, seq, first_tile, prev_out):
    N, D = x1.shape
    tm = ROW_TILE
    nt = yk.shape[1] // 2 // tm
    tok = lambda i: (first_tile + i, 0)
    in_specs = [
        pl.BlockSpec((tm, D), tok),
        pl.BlockSpec((tm, LANES), tok),
        pl.BlockSpec((1, 6, D), lambda i: (((first_tile + i) * tm) // seq, 0, 0)),
        pl.BlockSpec((1, D), lambda i: (0, 0)),
        pl.BlockSpec((yk.shape[0], tm, ROW_PIECE), lambda i: (0, i, 0)),
        pl.BlockSpec((yk.shape[0], tm, ROW_PIECE), lambda i: (0, nt + i, 0)),
    ]
    args = [x1, route, modl, final_g, yk, yk]
    aliases = {}
    if prev_out is not None:
        in_specs.append(pl.BlockSpec(memory_space=pl.ANY))
        args.append(prev_out)
        aliases = {len(args) - 1: 0}
    return pl.pallas_call(
        _combine_kernel,
        grid=(nt,),
        in_specs=in_specs,
        out_specs=pl.BlockSpec((tm, D), tok),
        out_shape=jax.ShapeDtypeStruct((N, D), F32),
        input_output_aliases=aliases,
        compiler_params=_params(("arbitrary",)),
        name="moe_combine",
    )(*args)


ITEM_LANES = 256
I_BLOCK, I_EXPERT, I_NEXT, I_SLOT, I_COUNT, I_ROWS = 0, 1, 2, 3, 4, 5


def _dest_kernel(routet_ref, cnt_ref, d1_ref, d2_ref, d3_ref, items_ref):
    tm = routet_ref.shape[1]
    ne = N_EXPERTS
    bm = float(MOE_BLOCK)
    cnt = cnt_ref[...]
    r = lax.broadcasted_iota(jnp.int32, (ne, ne), 0)
    c = lax.broadcasted_iota(jnp.int32, (ne, ne), 1)
    lower = (r >= c).astype(F32)
    cumsum = lambda a: jnp.dot(lower, a, precision=HIGHEST, preferred_element_type=F32)
    padded = jnp.floor((cnt + (bm - 1.0)) * (1.0 / bm)) * bm
    pend = cumsum(padded)
    pstart = pend - padded
    npad = padded - cnt
    pad_end = cumsum(npad)
    pad_start = pad_end - npad

    tile = lambda a, n: jnp.concatenate([a] * (n // LANES), axis=1)
    expert = lax.broadcasted_iota(jnp.int32, (ne, tm), 0).astype(F32)
    rec = routet_ref[...]
    pstart_t = tile(pstart, tm)

    def sorted_row(row_id, row_rank):
        sel = expert == rec[row_id:row_id + 1, :]
        return jnp.sum(jnp.where(sel, pstart_t, 0.0), axis=0, keepdims=True) + rec[row_rank:row_rank + 1, :]

    dest1 = sorted_row(R_ID1, R_RANK1)
    dest2 = sorted_row(R_ID2, R_RANK2)
    t = (lax.broadcasted_iota(jnp.int32, (1, tm), 1) + pl.program_id(0) * tm).astype(F32)
    ps_t, pe_t = tile(pad_start, tm), tile(pad_end, tm)
    in_e = jnp.logical_and(ps_t <= t, t < pe_t)
    pad_row = jnp.sum(jnp.where(in_e, tile(pstart + cnt, tm) + (t - ps_t), 0.0), axis=0, keepdims=True)
    dest3 = jnp.where(t < pe_t[ne - 1:ne, :], pad_row, dest1)
    d1_ref[0] = dest1.astype(jnp.int32)
    d2_ref[0] = dest2.astype(jnp.int32)
    d3_ref[0] = dest3.astype(jnp.int32)

    @pl.when(pl.program_id(0) == 0)
    def _():
        nl = ITEM_LANES
        pend_i = tile(pend, nl)
        n_items = pend_i[ne - 1:ne, :] * (1.0 / bm)
        blk = jnp.minimum(lax.broadcasted_iota(jnp.int32, (1, nl), 1).astype(F32), n_items - 1.0)
        erow = lax.broadcasted_iota(jnp.int32, (ne, nl), 0).astype(F32)
        ie = jnp.sum(jnp.where(pend_i <= blk * bm, 1.0, 0.0), axis=0, keepdims=True)
        nonempty = tile(cnt, nl) > 0.0
        order = jnp.sum(jnp.where(jnp.logical_and(nonempty, erow < ie), 1.0, 0.0), axis=0, keepdims=True)
        nxt = jnp.min(jnp.where(jnp.logical_and(nonempty, erow > ie), erow, float(ne)), axis=0, keepdims=True)
        nxt = jnp.where(nxt == float(ne), -1.0, nxt)
        slot = order - 2.0 * jnp.floor(order * 0.5)
        own_end = jnp.sum(jnp.where(erow == ie, tile(pstart + cnt, nl), 0.0), axis=0, keepdims=True)
        step = float(EXPERT_ROWS_STEP)
        rows = jnp.minimum(jnp.ceil((own_end - blk * bm) * (1.0 / step)) * step, bm)
        row = lax.broadcasted_iota(jnp.int32, (8, nl), 0)
        tab = jnp.where(row == I_BLOCK, blk, 0.0)
        tab = jnp.where(row == I_EXPERT, ie, tab)
        tab = jnp.where(row == I_NEXT, nxt, tab)
        tab = jnp.where(row == I_SLOT, slot, tab)
        tab = jnp.where(row == I_COUNT, n_items, tab)
        tab = jnp.where(row == I_ROWS, rows, tab)
        items_ref[...] = tab.astype(jnp.int32)


def _destinations(route_t, cnt, n_blocks):
    N = route_t.shape[1]
    tm = 2048 if N % 2048 == 0 else ROUTE_SUB
    out = jax.ShapeDtypeStruct((N // tm, 1, tm), jnp.int32)
    d1, d2, d3, items = pl.pallas_call(
        _dest_kernel,
        grid=(N // tm,),
        in_specs=[pl.BlockSpec((8, tm), lambda i: (0, i)),
                  pl.BlockSpec((N_EXPERTS, LANES), lambda i: (0, 0))],
        out_specs=[pl.BlockSpec((1, 1, tm), lambda i: (i, 0, 0))] * 3
        + [pl.BlockSpec((8, ITEM_LANES), lambda i: (0, 0))],
        out_shape=[out, out, out, jax.ShapeDtypeStruct((8, ITEM_LANES), jnp.int32)],
        compiler_params=_params(("arbitrary",)),
        name="moe_dest",
    )(route_t, cnt)
    return d1.reshape(N), d2.reshape(N), d3.reshape(N), items


def kernel(x, c, mod_w, mod_b, norm1_g, w_in, gate_w2, gate_b, gla_norm_g, conv_w, w_gla_out,
           w_conv_out, w_out, norm2_g, router_group_w, router_group_b, router_expert_w,
           router_expert_b, expert_w1, expert_w3, expert_w2, final_norm_g):
    B, S, D = x.shape
    L = mod_w.shape[0]
    N = B * S
    dk = gate_w2.shape[2]
    rank = gate_w2.shape[1]
    dv = gla_norm_g.shape[1]
    n_slots = N * 2
    n_blocks = n_slots // MOE_BLOCK + N_EXPERTS
    assert S % GLA_TILE == 0 and S % MIX_TILE == 0 and S % IN_TILE == 0 and S % IN_TILE_PLAIN == 0
    assert N % (FINAL_PARTS * ROW_TILE) == 0
    assert n_slots % MOE_BLOCK == 0 and MOE_BLOCK & (MOE_BLOCK - 1) == 0 and n_blocks <= ITEM_LANES
    assert N_EXPERTS * (MOE_BLOCK - 1) <= N
    assert N_GROUPS + N_EXPERTS <= LANES and EXPERTS_PER_GROUP == 8 and 3 * rank <= LANES and D % (2 * ROW_PIECE) == 0

    mod = [_modulation(c, mod_w, mod_b, l)[0] for l in range(L)]
    xf = x.reshape(N, D)
    o_gd = 2 * dk + dv
    o_r = o_gd + rank
    o_b = o_r + dv
    for l in range(L):
        modl = mod[l].reshape(B, 6, D)
        wl = w_in[l]
        wa = jnp.concatenate([wl[:, :o_gd], wl[:, o_r:o_b]], axis=1).astype(BF16)
        wb = wl[:, o_b:].astype(BF16)
        wgd = wl[:, o_gd:o_r]
        wg = jnp.pad(jnp.concatenate([wgd, wgd, wgd], axis=1), ((0, 0), (0, LANES - 3 * rank))).astype(BF16)
        g_hi = gate_w2[l].astype(BF16)
        g_lo = (gate_w2[l] - g_hi.astype(F32)).astype(BF16)
        gw2 = jnp.pad(jnp.concatenate([g_hi, g_hi, g_lo], axis=0), ((0, LANES - 3 * rank), (0, 0)))
        if l == 0:
            za, zb, zg = _in_projection(xf, modl, norm1_g[l].reshape(1, D), conv_w[l], wa, wb, wg, S)
        else:
            xf, za, zb, zg = _in_projection(x1, modl, norm1_g[l].reshape(1, D), conv_w[l], wa, wb, wg, S,
                                            combine=(route, yk, mod[l - 1].reshape(B, 6, D)))
        og = _gla(za, zg, gw2, gate_b[l].reshape(1, dk), gla_norm_g[l].reshape(1, dv), B, S, dk, dv, rank)
        wr = jnp.pad(jnp.concatenate([router_group_w[l], router_expert_w[l]], axis=1),
                     ((0, 0), (0, LANES - N_GROUPS - N_EXPERTS)))
        wr_hi = wr.astype(BF16)
        wr = jnp.concatenate([wr_hi, (wr - wr_hi.astype(F32)).astype(BF16)], axis=1)
        br = jnp.pad(jnp.concatenate([router_group_b[l], router_expert_b[l]]),
                     (0, LANES - N_GROUPS - N_EXPERTS)).reshape(1, LANES)
        x1, h2, logits = _mixer_out(
            og, zb, xf, modl, w_gla_out[l].astype(BF16), w_conv_out[l].astype(BF16),
            w_out[l].astype(BF16), norm2_g[l].reshape(1, D), wr, br, B, S)
        route_t, route, cnt = _routing(logits)
        dest1, dest2, dest3, items = _destinations(route_t, cnt, n_blocks)
        xs = _sc_dispatch(h2, (dest1, dest2, dest3), n_blocks * MOE_BLOCK)
        yb = _experts(xs, items, expert_w1, expert_w3, expert_w2, l)
        if l < L - 1:
            yk = _sc_return(yb, jnp.concatenate([dest1, dest2]))
    out = None
    part = N // FINAL_PARTS
    for p in range(FINAL_PARTS):
        tok = slice(p * part, (p + 1) * part)
        yk = _sc_return(yb, jnp.concatenate([dest1[tok], dest2[tok]]))
        out = _final_combine(x1, route, modl, final_norm_g.reshape(1, D), yk, S,
                             p * part // ROW_TILE, out)
    return out.reshape(B, S, D)
```
